```python
import math
import jax, jax.numpy as jnp
from jax import lax
import numpy as np

D_MODEL = 1024
BATCH = 8
SEQ = 2048
DEPTH = 1

SSD_HEADS = 8
SSD_HEAD_DIM = 64
SSD_INNER = SSD_HEADS * SSD_HEAD_DIM
SSD_GROUPS = 2
SSD_STATE = 128
SSD_CONV = 4
SSD_CHUNK = 128
SSD_XBC = SSD_INNER + 2 * SSD_GROUPS * SSD_STATE
DT_MIN = 0.001
DT_MAX = 0.1
MLA_HEADS = 8
MLA_NOPE = 64
MLA_ROPE = 32
MLA_QK = MLA_NOPE + MLA_ROPE
MLA_V = 64
MLA_Q_RANK = 256
MLA_KV_RANK = 128
MLA_INNER = MLA_HEADS * MLA_V
ROPE_THETA = 10000.0
ATTN_BLOCK = 128
MAX_POS_OFFSET = 4096
D_MIX = SSD_INNER + MLA_INNER
IN_SPLITS = (SSD_INNER, SSD_XBC, SSD_HEADS, MLA_Q_RANK, MLA_KV_RANK, MLA_ROPE)
D_IN_PROJ = sum(IN_SPLITS)
MOE_GROUPS = 4
MOE_EXPERTS_PER_GROUP = 8
MOE_EXPERTS = MOE_GROUPS * MOE_EXPERTS_PER_GROUP
MOE_TOP_K = 2
MOE_FF = 256
N_MOD = 6
EPS = 1e-6

kernel_name = "hymba_ssd_mla_hmoe_block"


def rms_norm(x, g):
    xf = x.astype(jnp.float32)
    y = xf * lax.rsqrt(jnp.mean(xf * xf, axis=-1, keepdims=True) + EPS)
    return (y * g.astype(jnp.float32)).astype(x.dtype)


def causal_depthwise_conv(u, w, b):
    k_w, ch = w.shape
    y = lax.conv_general_dilated(u, w[:, None, :].astype(u.dtype), window_strides=(1,),
                                 padding=[(k_w - 1, 0)],
                                 dimension_numbers=('NWC', 'WIO', 'NWC'),
                                 feature_group_count=ch)
    return y + b


def segsum(a):
    n = a.shape[-1]
    cs = jnp.cumsum(a, axis=-1)
    diff = cs[..., :, None] - cs[..., None, :]
    mask = jnp.tril(jnp.ones((n, n), dtype=bool))
    return jnp.where(mask, diff, -jnp.inf)


def ssd_chunked(x, dt, a, bm, cm):
    b, s, h, p = x.shape
    g, n = bm.shape[-2:]
    r = h // g
    nc = s // SSD_CHUNK
    f32 = jnp.float32
    xd = (x.astype(f32) * dt[..., None]).reshape(b, nc, SSD_CHUNK, g, r, p)
    adt = jnp.moveaxis((dt * a).reshape(b, nc, SSD_CHUNK, g, r), 2, -1)
    bc = bm.astype(f32).reshape(b, nc, SSD_CHUNK, g, n)
    cc = cm.astype(f32).reshape(b, nc, SSD_CHUNK, g, n)
    a_cs = jnp.cumsum(adt, axis=-1)
    decay = jnp.exp(segsum(adt))
    cb = jnp.einsum('bclgn,bcsgn->bcgls', cc, bc)
    y_diag = jnp.einsum('bcgrls,bcsgrp->bclgrp', cb[:, :, :, None] * decay, xd)
    decay_states = jnp.exp(a_cs[..., -1:] - a_cs)
    states = jnp.einsum('bclgn,bcgrl,bclgrp->bcgrpn', bc, decay_states, xd)
    chunk_decay = jnp.exp(a_cs[..., -1])

    def step(carry, inp):
        st, dec = inp
        return carry * dec[..., None, None] + st, carry

    init = jnp.zeros((b, g, r, p, n), f32)
    _, prev = lax.scan(step, init, (jnp.moveaxis(states, 1, 0), jnp.moveaxis(chunk_decay, 1, 0)))
    prev = jnp.moveaxis(prev, 0, 1)
    y_off = jnp.einsum('bclgn,bcgrpn,bcgrl->bclgrp', cc, prev, jnp.exp(a_cs))
    return (y_diag + y_off).reshape(b, s, h, p)


def rope_tables(positions):
    inv = 1.0 / (ROPE_THETA ** (jnp.arange(0, MLA_ROPE, 2, dtype=jnp.float32) / MLA_ROPE))
    ang = positions.astype(jnp.float32)[..., None] * inv
    return jnp.cos(ang), jnp.sin(ang)


def apply_rope(x, cos, sin):
    xf = x.astype(jnp.float32)
    x1, x2 = jnp.split(xf, 2, axis=-1)
    return jnp.concatenate([x1 * cos - x2 * sin, x2 * cos + x1 * sin], axis=-1).astype(x.dtype)


def causal_attention_blocked(q, k, v):
    b, s, h, d = q.shape
    nb = s // ATTN_BLOCK
    scale = d ** -0.5
    qb = jnp.moveaxis(q.reshape(b, nb, ATTN_BLOCK, h, d), 1, 0)
    k_idx = jnp.arange(s)

    def one_block(args):
        q_blk, i = args
        sc = jnp.einsum('bqhd,bkhd->bhqk', q_blk, k).astype(jnp.float32) * scale
        q_idx = i * ATTN_BLOCK + jnp.arange(ATTN_BLOCK)
        sc = jnp.where(k_idx[None, :] <= q_idx[:, None], sc, -jnp.inf)
        pr = jax.nn.softmax(sc, axis=-1).astype(v.dtype)
        return jnp.einsum('bhqk,bkhd->bqhd', pr, v)

    out = lax.map(one_block, (qb, jnp.arange(nb)))
    return jnp.moveaxis(out, 0, 1).reshape(b, s, h, v.shape[-1])


def hybrid_mixer(h, cos, sin, w_in, conv_w, conv_b, dt_bias, a_log, d_skip, ssd_norm,
                 q_norm, w_uq, kv_norm, w_ukv, mla_out_norm, w_out):
    b, s, _ = h.shape
    u = h @ w_in
    cuts = list(np.cumsum(IN_SPLITS)[:-1])
    z, xbc, dt_raw, cq, ckv, k_rope = jnp.split(u, cuts, axis=-1)
    xbc = jax.nn.silu(causal_depthwise_conv(xbc, conv_w, conv_b))
    xs, bs, cs = jnp.split(xbc, [SSD_INNER, SSD_INNER + SSD_GROUPS * SSD_STATE], axis=-1)
    xs = xs.reshape(b, s, SSD_HEADS, SSD_HEAD_DIM)
    bs = bs.reshape(b, s, SSD_GROUPS, SSD_STATE)
    cs = cs.reshape(b, s, SSD_GROUPS, SSD_STATE)
    dt = jax.nn.softplus(dt_raw.astype(jnp.float32) + dt_bias.astype(jnp.float32))
    a = -jnp.exp(a_log.astype(jnp.float32))
    y_ssd = ssd_chunked(xs, dt, a, bs, cs) + d_skip.astype(jnp.float32)[:, None] * xs.astype(jnp.float32)
    y_ssd = y_ssd.reshape(b, s, SSD_INNER).astype(h.dtype)
    y_ssd = rms_norm(y_ssd * jax.nn.silu(z), ssd_norm)
    q = jnp.einsum('bsr,rhd->bshd', rms_norm(cq, q_norm), w_uq)
    q_nope, q_pe = jnp.split(q, [MLA_NOPE], axis=-1)
    kv = jnp.einsum('bsr,rhd->bshd', rms_norm(ckv, kv_norm), w_ukv)
    k_nope, v = jnp.split(kv, [MLA_NOPE], axis=-1)
    q_pe = apply_rope(q_pe, cos[:, :, None, :], sin[:, :, None, :])
    k_pe = apply_rope(k_rope, cos, sin)
    q_full = jnp.concatenate([q_nope, q_pe], axis=-1)
    k_full = jnp.concatenate(
        [k_nope, jnp.broadcast_to(k_pe[:, :, None, :], (b, s, MLA_HEADS, MLA_ROPE))], axis=-1)
    y_mla = causal_attention_blocked(q_full, k_full, v).reshape(b, s, MLA_INNER)
    y_mla = rms_norm(y_mla, mla_out_norm)
    return jnp.concatenate([y_ssd, y_mla], axis=-1) @ w_out


def hierarchical_moe(h, w_group_router, b_group_router, w_expert_router, b_expert_router,
                     w_gate, w_up, w_down):
    b, s, d = h.shape
    t = h.reshape(b * s, d)
    n_tok = t.shape[0]
    g_prob = jax.nn.softmax((t @ w_group_router).astype(jnp.float32)
                            + b_group_router.astype(jnp.float32), axis=-1)
    g_w, g_idx = lax.top_k(g_prob, 1)
    e_logits = (t @ w_expert_router).astype(jnp.float32).reshape(
        n_tok, MOE_GROUPS, MOE_EXPERTS_PER_GROUP) + b_expert_router.astype(jnp.float32)
    e_sel = jnp.take_along_axis(e_logits, g_idx[:, :, None], axis=1)[:, 0]
    e_prob = jax.nn.softmax(e_sel, axis=-1)
    e_w, e_idx = lax.top_k(e_prob, MOE_TOP_K)
    e_w = e_w / jnp.sum(e_w, axis=-1, keepdims=True)
    w = g_w * e_w
    flat_idx = g_idx * MOE_EXPERTS_PER_GROUP + e_idx
    combine = jnp.einsum('tk,tke->te', w, jax.nn.one_hot(flat_idx, MOE_EXPERTS, dtype=jnp.float32))
    gate = jnp.einsum('td,edf->tef', t, w_gate)
    up = jnp.einsum('td,edf->tef', t, w_up)
    hid = (jax.nn.silu(gate) * up * combine[:, :, None].astype(t.dtype))
    out = jnp.einsum('tef,efd->td', hid, w_down)
    return out.reshape(b, s, d).astype(h.dtype)


def setup_inputs(seed: int = 0) -> dict:
    key = jax.random.key(seed)
    ks = jax.random.split(key, 32)
    f32 = jnp.float32
    L = DEPTH

    def nrm(k, shape, scale):
        return jax.random.normal(k, shape, f32) * scale

    def gain(k, n):
        return 1.0 + 0.02 * jax.random.normal(k, (L, n), f32)

    x = nrm(ks[0], (BATCH, SEQ, D_MODEL), 1.0)
    c = nrm(ks[1], (BATCH, D_MODEL), 1.0)
    offset = jax.random.randint(ks[2], (BATCH, 1), 0, MAX_POS_OFFSET, dtype=jnp.int32)
    positions = offset + jnp.arange(SEQ, dtype=jnp.int32)[None, :]
    dt_init = jnp.exp(jax.random.uniform(ks[11], (L, SSD_HEADS), f32,
                                         math.log(DT_MIN), math.log(DT_MAX)))
    return {
        "x": x,
        "c": c,
        "positions": positions,
        "ada_w": nrm(ks[3], (L, D_MODEL, N_MOD * D_MODEL), 0.5 * D_MODEL ** -0.5),
        "ada_b": nrm(ks[4], (L, N_MOD * D_MODEL), 0.02),
        "pre_norm_mix": gain(ks[5], D_MODEL),
        "post_norm_mix": gain(ks[6], D_MODEL),
        "pre_norm_ffn": gain(ks[7], D_MODEL),
        "post_norm_ffn": gain(ks[8], D_MODEL),
        "w_in": nrm(ks[9], (L, D_MODEL, D_IN_PROJ), D_MODEL ** -0.5),
        "conv_w": nrm(ks[10], (L, SSD_CONV, SSD_XBC), SSD_CONV ** -0.5),
        "conv_b": nrm(ks[12], (L, SSD_XBC), 0.01),
        "dt_bias": dt_init + jnp.log(-jnp.expm1(-dt_init)),
        "a_log": jnp.log(jax.random.uniform(ks[13], (L, SSD_HEADS), f32, 1.0, 16.0)),
        "d_skip": 1.0 + 0.1 * jax.random.normal(ks[14], (L, SSD_HEADS), f32),
        "ssd_norm": gain(ks[15], SSD_INNER),
        "q_norm": gain(ks[16], MLA_Q_RANK),
        "w_uq": nrm(ks[17], (L, MLA_Q_RANK, MLA_HEADS, MLA_QK), MLA_Q_RANK ** -0.5),
        "kv_norm": gain(ks[18], MLA_KV_RANK),
        "w_ukv": nrm(ks[19], (L, MLA_KV_RANK, MLA_HEADS, MLA_NOPE + MLA_V), MLA_KV_RANK ** -0.5),
        "mla_out_norm": gain(ks[20], MLA_INNER),
        "w_out": nrm(ks[21], (L, D_MIX, D_MODEL), D_MIX ** -0.5),
        "w_group_router": nrm(ks[22], (L, D_MODEL, MOE_GROUPS), D_MODEL ** -0.5),
        "b_group_router": nrm(ks[23], (L, MOE_GROUPS), 0.01),
        "w_expert_router": nrm(ks[24], (L, D_MODEL, MOE_EXPERTS), D_MODEL ** -0.5),
        "b_expert_router": nrm(ks[25], (L, MOE_GROUPS, MOE_EXPERTS_PER_GROUP), 0.01),
        "w_gate": nrm(ks[26], (L, MOE_EXPERTS, D_MODEL, MOE_FF), D_MODEL ** -0.5),
        "w_up": nrm(ks[27], (L, MOE_EXPERTS, D_MODEL, MOE_FF), D_MODEL ** -0.5),
        "w_down": nrm(ks[28], (L, MOE_EXPERTS, MOE_FF, D_MODEL), MOE_FF ** -0.5),
    }


def reference(x, c, positions, ada_w, ada_b, pre_norm_mix, post_norm_mix, pre_norm_ffn,
              post_norm_ffn, w_in, conv_w, conv_b, dt_bias, a_log, d_skip, ssd_norm,
              q_norm, w_uq, kv_norm, w_ukv, mla_out_norm, w_out, w_group_router,
              b_group_router, w_expert_router, b_expert_router, w_gate, w_up, w_down):
    cos, sin = rope_tables(positions)
    c_act = jax.nn.silu(c)
    for l in range(DEPTH):
        mod = (c_act @ ada_w[l] + ada_b[l])[:, None, :]
        sh1, sc1, g1, sh2, sc2, g2 = jnp.split(mod, N_MOD, axis=-1)
        h = rms_norm(x, pre_norm_mix[l]) * (1.0 + sc1) + sh1
        y = hybrid_mixer(h, cos, sin, w_in[l], conv_w[l], conv_b[l], dt_bias[l], a_log[l],
                         d_skip[l], ssd_norm[l], q_norm[l], w_uq[l], kv_norm[l], w_ukv[l],
                         mla_out_norm[l], w_out[l])
        x = x + g1 * rms_norm(y, post_norm_mix[l])
        h = rms_norm(x, pre_norm_ffn[l]) * (1.0 + sc2) + sh2
        y = hierarchical_moe(h, w_group_router[l], b_group_router[l], w_expert_router[l],
                             b_expert_router[l], w_gate[l], w_up[l], w_down[l])
        x = x + g2 * rms_norm(y, post_norm_ffn[l])
    return x
```

```python
import functools
import math

import jax
import jax.numpy as jnp
from jax import lax
from jax.experimental import pallas as pl
from jax.experimental.pallas import tpu as pltpu

F32 = jnp.float32
BF16 = jnp.bfloat16
HIGHEST = lax.Precision.HIGHEST

D_MODEL = 1024
SSD_HEADS = 8
SSD_HEAD_DIM = 64
SSD_INNER = SSD_HEADS * SSD_HEAD_DIM
SSD_GROUPS = 2
SSD_STATE = 128
SSD_CONV = 4
SSD_CHUNK = 128
SSD_XBC = SSD_INNER + 2 * SSD_GROUPS * SSD_STATE
MLA_HEADS = 8
MLA_NOPE = 64
MLA_ROPE = 32
MLA_QK = MLA_NOPE + MLA_ROPE
MLA_V = 64
MLA_Q_RANK = 256
MLA_KV_RANK = 128
MLA_INNER = MLA_HEADS * MLA_V
ROPE_THETA = 10000.0
MOE_GROUPS = 4
MOE_EXPERTS_PER_GROUP = 8
MOE_EXPERTS = MOE_GROUPS * MOE_EXPERTS_PER_GROUP
MOE_FF = 256
N_MOD = 6
EPS = 1e-6

LANES = 128
HEAD_PAD = 128
VMEM_LIMIT = 48 * 1024 * 1024

_C_Z = 0
_C_XBC = _C_Z + SSD_INNER
_C_CQ = _C_XBC + SSD_XBC
_C_CKV = _C_CQ + MLA_Q_RANK
_C_KR = _C_CKV + MLA_KV_RANK
_C_KRS = _C_KR + LANES
_C_DT = _C_KRS + LANES
_C_END = _C_DT + LANES


def _silu(v):
    return v * (1.0 / (1.0 + jnp.exp(-v)))


def _rms(v, gain):
    return v * lax.rsqrt(jnp.mean(v * v, axis=-1, keepdims=True) + EPS) * gain


def _params(*sem):
    return pltpu.CompilerParams(dimension_semantics=sem, vmem_limit_bytes=VMEM_LIMIT)


def _mod_kernel(c_ref, w_ref, b_ref, o_ref):
    o_ref[...] = jnp.dot(_silu(c_ref[...]), w_ref[...], precision=HIGHEST,
                         preferred_element_type=F32) + b_ref[...]


def _modulation(c, ada_w, ada_b):
    b, d = c.shape
    n = ada_w.shape[1]
    return pl.pallas_call(
        _mod_kernel,
        grid=(n // d,),
        in_specs=[pl.BlockSpec((b, d), lambda j: (0, 0)),
                  pl.BlockSpec((d, d), lambda j: (0, j)),
                  pl.BlockSpec((1, d), lambda j: (0, j))],
        out_specs=pl.BlockSpec((b, d), lambda j: (0, j)),
        out_shape=jax.ShapeDtypeStruct((b, n), F32),
        compiler_params=_params("arbitrary"),
        name="modulation",
    )(c, ada_w, ada_b.reshape(1, n))


def _rope_kernel(pos_ref, inv_ref, cos_ref, sin_ref):
    ang = pos_ref[...].astype(F32) * inv_ref[...]
    cos_ref[...] = jnp.cos(ang)
    sin_ref[...] = jnp.sin(ang)


def _rope_tables(positions):
    t = positions.size
    half = MLA_ROPE // 2
    per_row = LANES // half
    inv = 1.0 / (ROPE_THETA ** (jnp.arange(0, MLA_ROPE, 2, dtype=F32) / MLA_ROPE))
    pos_rep = jnp.repeat(positions.reshape(t), half).reshape(t // per_row, LANES)
    inv_t = jnp.tile(inv, per_row).reshape(1, LANES)
    rows = t // per_row
    blk = min(rows, 512)
    cos, sin = pl.pallas_call(
        _rope_kernel,
        grid=(rows // blk,),
        in_specs=[pl.BlockSpec((blk, LANES), lambda i: (i, 0)),
                  pl.BlockSpec((1, LANES), lambda i: (0, 0))],
        out_specs=[pl.BlockSpec((blk, LANES), lambda i: (i, 0))] * 2,
        out_shape=[jax.ShapeDtypeStruct((rows, LANES), F32)] * 2,
        compiler_params=_params("arbitrary"),
        name="rope_tables",
    )(pos_rep, inv_t)
    return cos.reshape(t, half), sin.reshape(t, half)


def _inproj_kernel(x_ref, mod_ref, gpre_ref, win_ref, qn_ref, wq_ref, kvn_ref, wkv_ref,
                   cos_ref, sin_ref, z_ref, xbc_ref, dt_ref, q_ref, k_ref, v_ref):
    x = x_ref[...]
    sh = mod_ref[0, 0:1, :]
    sc = mod_ref[0, 1:2, :]
    h = (_rms(x, gpre_ref[...]) * (1.0 + sc) + sh).astype(BF16)
    u = jnp.dot(h, win_ref[...], preferred_element_type=F32)
    z_ref[...] = u[:, _C_Z:_C_XBC].astype(BF16)
    xbc_ref[...] = u[:, _C_XBC:_C_CQ].astype(BF16)
    dt_ref[...] = u[:, _C_DT:_C_END]
    cos_t = cos_ref[...]
    sin_t = sin_ref[...]
    cq = _rms(u[:, _C_CQ:_C_CKV], qn_ref[...]).astype(BF16)
    q2 = jnp.dot(cq, wq_ref[...], preferred_element_type=F32)
    scale = MLA_QK ** -0.5
    nq = MLA_HEADS * HEAD_PAD
    for hd in range(MLA_HEADS):
        a = q2[:, hd * HEAD_PAD:(hd + 1) * HEAD_PAD]
        b = q2[:, nq + hd * HEAD_PAD:nq + (hd + 1) * HEAD_PAD]
        q_ref[:, hd * HEAD_PAD:(hd + 1) * HEAD_PAD] = ((a * cos_t + b * sin_t) * scale).astype(BF16)
    ckv = _rms(u[:, _C_CKV:_C_KR], kvn_ref[...]).astype(BF16)
    kv = jnp.dot(ckv, wkv_ref[...], preferred_element_type=F32)
    k_pe = u[:, _C_KR:_C_KRS] * cos_t + u[:, _C_KRS:_C_DT] * sin_t
    for hd in range(MLA_HEADS):
        k_ref[:, hd * HEAD_PAD:(hd + 1) * HEAD_PAD] = (
            kv[:, hd * HEAD_PAD:(hd + 1) * HEAD_PAD] + k_pe).astype(BF16)
    v_ref[...] = kv[:, nq:].astype(BF16)


def _inproj_weights(w_in, w_uq, w_ukv):
    d = w_in.shape[0]
    half = MLA_ROPE // 2
    o_z, o_xbc, o_dt = 0, SSD_INNER, SSD_INNER + SSD_XBC
    o_cq = o_dt + SSD_HEADS
    o_ckv = o_cq + MLA_Q_RANK
    o_kr = o_ckv + MLA_KV_RANK
    zeros = lambda n: jnp.zeros((d, n), F32)
    kr = w_in[:, o_kr:o_kr + MLA_ROPE]
    kr_blk = jnp.concatenate([zeros(MLA_NOPE), kr, zeros(HEAD_PAD - MLA_QK)], axis=1)
    krs_blk = jnp.concatenate([zeros(MLA_NOPE), -kr[:, half:], kr[:, :half], zeros(HEAD_PAD - MLA_QK)], axis=1)
    dt_blk = jnp.concatenate([w_in[:, o_dt:o_dt + SSD_HEADS], zeros(LANES - SSD_HEADS)], axis=1)
    w_in_r = jnp.concatenate([w_in[:, o_z:o_dt], w_in[:, o_cq:o_kr], kr_blk, krs_blk, dt_blk], axis=1)
    r = w_uq.shape[0]
    zq = jnp.zeros((r, MLA_HEADS, HEAD_PAD - MLA_QK), F32)
    zn = jnp.zeros((r, MLA_HEADS, MLA_NOPE), F32)
    wq_plain = jnp.concatenate([w_uq, zq], axis=2).reshape(r, MLA_HEADS * HEAD_PAD)
    wq_rot = jnp.concatenate([zn, -w_uq[:, :, MLA_NOPE + half:], w_uq[:, :, MLA_NOPE:MLA_NOPE + half], zq],
                             axis=2).reshape(r, MLA_HEADS * HEAD_PAD)
    wq2 = jnp.concatenate([wq_plain, wq_rot], axis=1)
    rk = w_ukv.shape[0]
    zk = jnp.zeros((rk, MLA_HEADS, HEAD_PAD - MLA_NOPE), F32)
    wk = jnp.concatenate([w_ukv[:, :, :MLA_NOPE], zk], axis=2).reshape(rk, MLA_HEADS * HEAD_PAD)
    wv = w_ukv[:, :, MLA_NOPE:].reshape(rk, MLA_INNER)
    wkv = jnp.concatenate([wk, wv], axis=1)
    return w_in_r.astype(BF16), wq2.astype(BF16), wkv.astype(BF16)


def _inproj(x2, mod3, gpre, w_in_r, q_norm, wq2, kv_norm, wkv, cos_t, sin_t, seq, tm):
    t, d = x2.shape
    per_b = seq // tm
    row = lambda i: (i, 0)
    const = lambda i: (0, 0)
    nqk = MLA_HEADS * HEAD_PAD
    return pl.pallas_call(
        _inproj_kernel,
        grid=(t // tm,),
        in_specs=[pl.BlockSpec((tm, d), row),
                  pl.BlockSpec((1, N_MOD, d), lambda i: (i // per_b, 0, 0)),
                  pl.BlockSpec((1, d), const),
                  pl.BlockSpec(w_in_r.shape, const),
                  pl.BlockSpec((1, MLA_Q_RANK), const),
                  pl.BlockSpec(wq2.shape, const),
                  pl.BlockSpec((1, MLA_KV_RANK), const),
                  pl.BlockSpec(wkv.shape, const),
                  pl.BlockSpec((tm, LANES), row),
                  pl.BlockSpec((tm, LANES), row)],
        out_specs=[pl.BlockSpec((tm, SSD_INNER), row),
                   pl.BlockSpec((tm, SSD_XBC), row),
                   pl.BlockSpec((tm, LANES), row),
                   pl.BlockSpec((tm, nqk), row),
                   pl.BlockSpec((tm, nqk), row),
                   pl.BlockSpec((tm, MLA_INNER), row)],
        out_shape=[jax.ShapeDtypeStruct((t, SSD_INNER), BF16),
                   jax.ShapeDtypeStruct((t, SSD_XBC), BF16),
                   jax.ShapeDtypeStruct((t, LANES), F32),
                   jax.ShapeDtypeStruct((t, nqk), BF16),
                   jax.ShapeDtypeStruct((t, nqk), BF16),
                   jax.ShapeDtypeStruct((t, MLA_INNER), BF16)],
        compiler_params=_params("arbitrary"),
        name="inproj",
    )(x2, mod3, gpre, w_in_r, q_norm, wq2, kv_norm, wkv, cos_t, sin_t)


def _ssd_kernel(xbc_ref, z_ref, dt_ref, cw_ref, cb_ref, dtb_ref, alog_ref, dsk_ref, gn_ref,
                e64_ref, e128_ref, y_ref, tail_ref, state_ref):
    L = SSD_CHUNK
    gw = SSD_INNER // SSD_GROUPS

    @pl.when(pl.program_id(1) == 0)
    def _():
        tail_ref[...] = jnp.zeros_like(tail_ref)
        state_ref[...] = jnp.zeros_like(state_ref)

    cur = xbc_ref[...].astype(F32)
    prev8 = tail_ref[...]
    row8 = lax.broadcasted_iota(jnp.int32, prev8.shape, 0)
    acc = cur * cw_ref[SSD_CONV - 1:SSD_CONV, :] + cb_ref[...]
    for j in range(1, SSD_CONV):
        r = pltpu.roll(cur, j, 0)
        head = jnp.where(row8 < j, pltpu.roll(prev8, j, 0), r[0:8])
        r = jnp.concatenate([head, r[8:]], axis=0)
        acc = acc + r * cw_ref[SSD_CONV - 1 - j:SSD_CONV - j, :]
    tail_ref[...] = cur[L - 8:L]
    act = _silu(acc)
    xs = act[:, :SSD_INNER]
    bm = act[:, SSD_INNER:SSD_INNER + SSD_GROUPS * SSD_STATE].astype(BF16)
    cm = act[:, SSD_INNER + SSD_GROUPS * SSD_STATE:].astype(BF16)

    dt_in = dt_ref[...] + dtb_ref[...]
    dt = jnp.maximum(dt_in, 0.0) + jnp.log(1.0 + jnp.exp(-jnp.abs(dt_in)))
    adt = dt * (-jnp.exp(alog_ref[...]))
    ri = lax.broadcasted_iota(jnp.int32, (L, L), 0)
    ci = lax.broadcasted_iota(jnp.int32, (L, L), 1)
    causal = ci <= ri
    tril = causal.astype(F32)
    a_cs = jnp.dot(tril, adt, precision=HIGHEST, preferred_element_type=F32)
    a_cs_t = a_cs.T
    cs64 = jnp.dot(a_cs, e64_ref[...], precision=HIGHEST, preferred_element_type=F32)
    cs128 = jnp.dot(a_cs, e128_ref[...], precision=HIGHEST, preferred_element_type=F32)
    dt64 = jnp.dot(dt, e64_ref[...], precision=HIGHEST, preferred_element_type=F32)

    xd = xs * dt64
    xd_b = xd.astype(BF16)
    last = cs64[L - 1:L, :]
    xdw = (xd * jnp.exp(last - cs64)).astype(BF16)
    chunk_decay = jnp.exp(last)
    in_decay = jnp.exp(cs64)

    lane = lax.broadcasted_iota(jnp.int32, (L, LANES), 1)
    lo = lane < SSD_HEAD_DIM
    zero_b = jnp.zeros((L, LANES), BF16)
    y_parts = []
    new_states = []
    for g in range(SSD_GROUPS):
        bg = bm[:, g * SSD_STATE:(g + 1) * SSD_STATE]
        cg = cm[:, g * SSD_STATE:(g + 1) * SSD_STATE]
        cb = lax.dot_general(cg, bg, (((1,), (1,)), ((), ())), preferred_element_type=F32)
        prev = state_ref[:, g * gw:(g + 1) * gw]
        y_off = jnp.dot(cg, prev.astype(BF16), preferred_element_type=F32) * in_decay[:, g * gw:(g + 1) * gw]
        st = lax.dot_general(bg, xdw[:, g * gw:(g + 1) * gw], (((0,), (0,)), ((), ())),
                             preferred_element_type=F32)
        new_states.append(prev * chunk_decay[:, g * gw:(g + 1) * gw] + st)
        heads_per_group = SSD_HEADS // SSD_GROUPS
        for pair in range(heads_per_group // 2):
            h0 = g * heads_per_group + 2 * pair
            blk = xd_b[:, h0 * SSD_HEAD_DIM:(h0 + 2) * SSD_HEAD_DIM]
            y_pair = None
            for k in range(2):
                hh = h0 + k
                diff = cs128[:, hh * LANES:(hh + 1) * LANES] - a_cs_t[hh:hh + 1, :]
                decay = jnp.where(causal, jnp.exp(diff), 0.0)
                m = (cb * decay).astype(BF16)
                rhs = jnp.where(lo, blk, zero_b) if k == 0 else jnp.where(lo, zero_b, blk)
                part = jnp.dot(m, rhs, preferred_element_type=F32)
                y_pair = part if y_pair is None else y_pair + part
            c0 = (2 * pair) * SSD_HEAD_DIM
            y_parts.append(y_pair + y_off[:, c0:c0 + LANES])
    state_ref[...] = jnp.concatenate(new_states, axis=1)
    y = jnp.concatenate(y_parts, axis=1) + dsk_ref[...] * xs
    gated = y * _silu(z_ref[...].astype(F32))
    y_ref[...] = _rms(gated, gn_ref[...]).astype(BF16)


def _ssd(xbc, z, dt_raw, conv_w, conv_b, dt_bias, a_log, d_skip, ssd_norm, batch, seq):
    t = xbc.shape[0]
    nc = seq // SSD_CHUNK
    L = SSD_CHUNK
    row = lambda b, c: (b * nc + c, 0)
    const = lambda b, c: (0, 0)
    pad = lambda v: jnp.concatenate([v.reshape(1, -1), jnp.zeros((1, LANES - v.size), F32)], axis=1)
    head_of_lane64 = jnp.arange(SSD_INNER) // SSD_HEAD_DIM
    head_of_lane128 = jnp.arange(SSD_HEADS * LANES) // LANES
    rows = jnp.arange(LANES)[:, None]
    e64 = (rows == head_of_lane64[None, :]).astype(F32)
    e128 = (rows == head_of_lane128[None, :]).astype(F32)
    dsk = jnp.repeat(d_skip.astype(F32), SSD_HEAD_DIM).reshape(1, SSD_INNER)
    return pl.pallas_call(
        _ssd_kernel,
        grid=(batch, nc),
        in_specs=[pl.BlockSpec((L, SSD_XBC), row),
                  pl.BlockSpec((L, SSD_INNER), row),
                  pl.BlockSpec((L, LANES), row),
                  pl.BlockSpec((SSD_CONV, SSD_XBC), const),
                  pl.BlockSpec((1, SSD_XBC), const),
                  pl.BlockSpec((1, LANES), const),
                  pl.BlockSpec((1, LANES), const),
                  pl.BlockSpec((1, SSD_INNER), const),
                  pl.BlockSpec((1, SSD_INNER), const),
                  pl.BlockSpec(e64.shape, const),
                  pl.BlockSpec(e128.shape, const)],
        out_specs=pl.BlockSpec((L, SSD_INNER), row),
        out_shape=jax.ShapeDtypeStruct((t, SSD_INNER), BF16),
        scratch_shapes=[pltpu.VMEM((8, SSD_XBC), F32),
                        pltpu.VMEM((SSD_STATE, SSD_INNER), F32)],
        compiler_params=_params("arbitrary", "arbitrary"),
        name="ssd",
    )(xbc, z, dt_raw, conv_w, conv_b.reshape(1, -1), pad(dt_bias), pad(a_log), dsk,
      ssd_norm.reshape(1, -1), e64, e128)


def _attn_kernel(q_ref, k_ref, v_ref, gn_ref, o_ref, *, blk):
    qi = pl.program_id(1)
    ri = lax.broadcasted_iota(jnp.int32, (blk, blk), 0)
    ci = lax.broadcasted_iota(jnp.int32, (blk, blk), 1)
    causal = ci <= ri
    lo = lax.broadcasted_iota(jnp.int32, (blk, LANES), 1) < MLA_V
    zero_b = jnp.zeros((blk, LANES), BF16)
    dims = (((1,), (1,)), ((), ()))
    outs = []
    for pair in range(MLA_HEADS // 2):
        ha, hb = 2 * pair, 2 * pair + 1
        qa = q_ref[:, ha * HEAD_PAD:(ha + 1) * HEAD_PAD]
        qb = q_ref[:, hb * HEAD_PAD:(hb + 1) * HEAD_PAD]

        def step(j, carry, masked, qa=qa, qb=qb, ha=ha, hb=hb, pair=pair):
            ma, la, mb, lb, acc = carry
            rows = pl.ds(pl.multiple_of(j * blk, blk), blk)
            vv = v_ref[rows, pair * LANES:(pair + 1) * LANES]
            new = []
            for q, hd, m_old, l_old in ((qa, ha, ma, la), (qb, hb, mb, lb)):
                kk = k_ref[rows, hd * HEAD_PAD:(hd + 1) * HEAD_PAD]
                s = lax.dot_general(q, kk, dims, preferred_element_type=F32)
                if masked:
                    s = jnp.where(causal, s, -jnp.inf)
                m_new = jnp.maximum(m_old, jnp.max(s, axis=-1, keepdims=True))
                p = jnp.exp(s - m_new)
                alpha = jnp.exp(m_old - m_new)
                l_new = alpha * l_old + jnp.sum(p, axis=-1, keepdims=True)
                new.append((m_new, l_new, alpha, p.astype(BF16)))
            (ma, la, aa, pa), (mb, lb, ab, pb) = new
            acc = (acc * jnp.where(lo, aa, ab)
                   + jnp.dot(pa, jnp.where(lo, vv, zero_b), preferred_element_type=F32)
                   + jnp.dot(pb, jnp.where(lo, zero_b, vv), preferred_element_type=F32))
            return ma, la, mb, lb, acc

        neg = jnp.full((blk, 1), -jnp.inf, F32)
        zero = jnp.zeros((blk, 1), F32)
        init = (neg, zero, neg, zero, jnp.zeros((blk, LANES), F32))
        carry = lax.fori_loop(0, qi, functools.partial(step, masked=False), init)
        ma, la, mb, lb, acc = step(qi, carry, True)
        outs.append(acc / jnp.where(lo, la, lb))
    y = jnp.concatenate(outs, axis=1)
    o_ref[...] = _rms(y, gn_ref[...]).astype(BF16)


def _attention(q, k, v, gn, batch, seq, blk):
    t = q.shape[0]
    nq = seq // blk
    nqk = MLA_HEADS * HEAD_PAD
    return pl.pallas_call(
        functools.partial(_attn_kernel, blk=blk),
        grid=(batch, nq),
        in_specs=[pl.BlockSpec((blk, nqk), lambda b, i: (b * nq + i, 0)),
                  pl.BlockSpec((seq, nqk), lambda b, i: (b, 0)),
                  pl.BlockSpec((seq, MLA_INNER), lambda b, i: (b, 0)),
                  pl.BlockSpec((1, MLA_INNER), lambda b, i: (0, 0))],
        out_specs=pl.BlockSpec((blk, MLA_INNER), lambda b, i: (b * nq + i, 0)),
        out_shape=jax.ShapeDtypeStruct((t, MLA_INNER), BF16),
        compiler_params=_params("arbitrary", "arbitrary"),
        name="attention",
    )(q, k, v, gn)


def _outproj_kernel(ys_ref, ym_ref, x_ref, mod_ref, wtop_ref, wbot_ref, gpm_ref, gpf_ref,
                    wr_ref, br_ref, x1_ref, h2_ref, comb_ref):
    y = (jnp.dot(ys_ref[...], wtop_ref[...], preferred_element_type=F32)
         + jnp.dot(ym_ref[...], wbot_ref[...], preferred_element_type=F32))
    g1 = mod_ref[0, 2:3, :]
    sh2 = mod_ref[0, 3:4, :]
    sc2 = mod_ref[0, 4:5, :]
    x1 = x_ref[...] + g1 * _rms(y, gpm_ref[...])
    x1_ref[...] = x1
    h2 = _rms(x1, gpf_ref[...]) * (1.0 + sc2) + sh2
    h2_ref[...] = h2.astype(BF16)
    logits = jnp.dot(h2, wr_ref[...], precision=HIGHEST, preferred_element_type=F32) + br_ref[...]
    lane = lax.broadcasted_iota(jnp.int32, logits.shape, 1).astype(F32)
    ninf = -jnp.inf
    big = 1e9
    is_g = lane < MOE_GROUPS
    gl = jnp.where(is_g, logits, ninf)
    gmax = jnp.max(gl, axis=-1, keepdims=True)
    gsum = jnp.sum(jnp.where(is_g, jnp.exp(logits - gmax), 0.0), axis=-1, keepdims=True)
    g_w = 1.0 / gsum
    g_idx = jnp.min(jnp.where(gl == gmax, lane, big), axis=-1, keepdims=True)
    first = MOE_GROUPS + MOE_EXPERTS_PER_GROUP * g_idx
    el = jnp.where(lane >= first, jnp.where(lane < first + MOE_EXPERTS_PER_GROUP, logits, ninf), ninf)
    m1 = jnp.max(el, axis=-1, keepdims=True)
    i1 = jnp.min(jnp.where(el == m1, lane, big), axis=-1, keepdims=True)
    el2 = jnp.where(lane == i1, ninf, el)
    m2 = jnp.max(el2, axis=-1, keepdims=True)
    i2 = jnp.min(jnp.where(el2 == m2, lane, big), axis=-1, keepdims=True)
    r = jnp.exp(m2 - m1)
    w1 = g_w / (1.0 + r)
    w2 = g_w * r / (1.0 + r)
    comb_ref[...] = jnp.where(lane == i1, w1, 0.0) + jnp.where(lane == i2, w2, 0.0)


def _outproj(y_ssd, y_mla, x2, mod3, w_top, w_bot, gpm, gpf, w_router, b_router, seq, tm):
    t, d = x2.shape
    per_b = seq // tm
    row = lambda i: (i, 0)
    const = lambda i: (0, 0)
    return pl.pallas_call(
        _outproj_kernel,
        grid=(t // tm,),
        in_specs=[pl.BlockSpec((tm, SSD_INNER), row),
                  pl.BlockSpec((tm, MLA_INNER), row),
                  pl.BlockSpec((tm, d), row),
                  pl.BlockSpec((1, N_MOD, d), lambda i: (i // per_b, 0, 0)),
                  pl.BlockSpec(w_top.shape, const),
                  pl.BlockSpec(w_bot.shape, const),
                  pl.BlockSpec((1, d), const),
                  pl.BlockSpec((1, d), const),
                  pl.BlockSpec(w_router.shape, const),
                  pl.BlockSpec((1, LANES), const)],
        out_specs=[pl.BlockSpec((tm, d), row),
                   pl.BlockSpec((tm, d), row),
                   pl.BlockSpec((tm, LANES), row)],
        out_shape=[jax.ShapeDtypeStruct((t, d), F32),
                   jax.ShapeDtypeStruct((t, d), BF16),
                   jax.ShapeDtypeStruct((t, LANES), F32)],
        compiler_params=_params("arbitrary"),
        name="outproj_router",
    )(y_ssd, y_mla, x2, mod3, w_top, w_bot, gpm, gpf, w_router, b_router)


def _moe_kernel(h_ref, comb_ref, wgu_ref, wd_ref, x1_ref, mod_ref, gpost_ref, o_ref, acc_ref):
    e = pl.program_id(1)

    @pl.when(e == 0)
    def _():
        acc_ref[...] = jnp.zeros_like(acc_ref)

    gu = jnp.dot(h_ref[...], wgu_ref[0], preferred_element_type=F32)
    comb = comb_ref[...]
    lane = lax.broadcasted_iota(jnp.int32, comb.shape, 1)
    cw = jnp.sum(jnp.where(lane == e + MOE_GROUPS, comb, 0.0), axis=-1, keepdims=True)
    hid = (_silu(gu[:, :MOE_FF]) * gu[:, MOE_FF:] * cw).astype(BF16)
    acc_ref[...] += jnp.dot(hid, wd_ref[0], preferred_element_type=F32)

    @pl.when(e == pl.num_programs(1) - 1)
    def _():
        g2 = mod_ref[0, 5:6, :]
        o_ref[...] = x1_ref[...] + g2 * _rms(acc_ref[...], gpost_ref[...])


def _moe(h2, comb, wgu, wd, x1, mod3, gpost, seq, tm):
    t, d = x1.shape
    per_b = seq // tm
    n_e = wgu.shape[0]
    return pl.pallas_call(
        _moe_kernel,
        grid=(t // tm, n_e),
        in_specs=[pl.BlockSpec((tm, d), lambda i, e: (i, 0)),
                  pl.BlockSpec((tm, LANES), lambda i, e: (i, 0)),
                  pl.BlockSpec((1, d, 2 * MOE_FF), lambda i, e: (e, 0, 0)),
                  pl.BlockSpec((1, MOE_FF, d), lambda i, e: (e, 0, 0)),
                  pl.BlockSpec((tm, d), lambda i, e: (i, 0)),
                  pl.BlockSpec((1, N_MOD, d), lambda i, e: (i // per_b, 0, 0)),
                  pl.BlockSpec((1, d), lambda i, e: (0, 0))],
        out_specs=pl.BlockSpec((tm, d), lambda i, e: (i, 0)),
        out_shape=jax.ShapeDtypeStruct((t, d), F32),
        scratch_shapes=[pltpu.VMEM((tm, d), F32)],
        compiler_params=_params("arbitrary", "arbitrary"),
        name="moe",
    )(h2, comb, wgu, wd, x1, mod3, gpost)


def kernel(x, c, positions, ada_w, ada_b, pre_norm_mix, post_norm_mix, pre_norm_ffn, post_norm_ffn, w_in, conv_w, conv_b, dt_bias, a_log, d_skip, ssd_norm, q_norm, w_uq, kv_norm, w_ukv, mla_out_norm, w_out, w_group_router, b_group_router, w_expert_router, b_expert_router, w_gate, w_up, w_down):
    batch, seq, d = x.shape
    t = batch * seq
    depth = ada_w.shape[0]
    tm = min(512, seq)
    half = MLA_ROPE // 2

    cos16, sin16 = _rope_tables(positions)
    ones = lambda n: jnp.ones((t, n), F32)
    zeros = lambda n: jnp.zeros((t, n), F32)
    cos_t = jnp.concatenate([ones(MLA_NOPE), cos16, cos16, ones(HEAD_PAD - MLA_QK)], axis=1)
    sin_t = jnp.concatenate([zeros(MLA_NOPE), sin16, sin16, zeros(HEAD_PAD - MLA_QK)], axis=1)

    x2 = x.reshape(t, d)
    for l in range(depth):
        mod3 = _modulation(c, ada_w[l], ada_b[l]).reshape(batch, N_MOD, d)
        w_in_r, wq2, wkv = _inproj_weights(w_in[l], w_uq[l], w_ukv[l])
        z, xbc, dt_raw, q, k, v = _inproj(
            x2, mod3, pre_norm_mix[l].reshape(1, d), w_in_r, q_norm[l].reshape(1, -1), wq2,
            kv_norm[l].reshape(1, -1), wkv, cos_t, sin_t, seq, tm)
        y_ssd = _ssd(xbc, z, dt_raw, conv_w[l], conv_b[l], dt_bias[l], a_log[l], d_skip[l],
                     ssd_norm[l], batch, seq)
        y_mla = _attention(q, k, v, mla_out_norm[l].reshape(1, -1), batch, seq, min(256, seq))
        w_o = w_out[l].astype(BF16)
        pad_r = LANES - MOE_GROUPS - MOE_EXPERTS
        w_router = jnp.concatenate([w_group_router[l], w_expert_router[l], jnp.zeros((d, pad_r), F32)], axis=1)
        b_router = jnp.concatenate([b_group_router[l].reshape(-1), b_expert_router[l].reshape(-1),
                                    jnp.zeros((pad_r,), F32)]).reshape(1, LANES)
        x1, h2, comb = _outproj(y_ssd, y_mla, x2, mod3, w_o[:SSD_INNER], w_o[SSD_INNER:],
                                post_norm_mix[l].reshape(1, d), pre_norm_ffn[l].reshape(1, d),
                                w_router, b_router, seq, tm)
        wgu = jnp.concatenate([w_gate[l], w_up[l]], axis=2).astype(BF16)
        x2 = _moe(h2, comb, wgu, w_down[l].astype(BF16), x1, mod3, post_norm_ffn[l].reshape(1, d),
                  seq, min(1024, seq))
    return x2.reshape(batch, seq, d)
```

```python
import functools
import math

import jax
import jax.numpy as jnp
from jax import lax
from jax.experimental import pallas as pl
from jax.experimental.pallas import tpu as pltpu

F32 = jnp.float32
BF16 = jnp.bfloat16
HIGHEST = lax.Precision.HIGHEST

D_MODEL = 1024
SSD_HEADS = 8
SSD_HEAD_DIM = 64
SSD_INNER = SSD_HEADS * SSD_HEAD_DIM
SSD_GROUPS = 2
SSD_STATE = 128
SSD_CONV = 4
SSD_CHUNK = 128
SSD_XBC = SSD_INNER + 2 * SSD_GROUPS * SSD_STATE
MLA_HEADS = 8
MLA_NOPE = 64
MLA_ROPE = 32
MLA_QK = MLA_NOPE + MLA_ROPE
MLA_V = 64
MLA_Q_RANK = 256
MLA_KV_RANK = 128
MLA_INNER = MLA_HEADS * MLA_V
ROPE_THETA = 10000.0
MOE_GROUPS = 4
MOE_EXPERTS_PER_GROUP = 8
MOE_EXPERTS = MOE_GROUPS * MOE_EXPERTS_PER_GROUP
MOE_FF = 256
N_MOD = 6
EPS = 1e-6

LANES = 128
HEAD_PAD = 128
VMEM_LIMIT = 48 * 1024 * 1024

_C_Z = 0
_C_XBC = _C_Z + SSD_INNER
_C_CQ = _C_XBC + SSD_XBC
_C_CKV = _C_CQ + MLA_Q_RANK
_C_KR = _C_CKV + MLA_KV_RANK
_C_KRS = _C_KR + LANES
_C_DT = _C_KRS + LANES
_C_END = _C_DT + LANES


def _silu(v):
    return v * (1.0 / (1.0 + jnp.exp(-v)))


def _rms(v, gain):
    return v * lax.rsqrt(jnp.mean(v * v, axis=-1, keepdims=True) + EPS) * gain


def _params(*sem):
    return pltpu.CompilerParams(dimension_semantics=sem, vmem_limit_bytes=VMEM_LIMIT)


def _mod_kernel(c_ref, w_ref, b_ref, o_ref):
    o_ref[...] = jnp.dot(_silu(c_ref[...]), w_ref[...], precision=HIGHEST,
                         preferred_element_type=F32) + b_ref[...]


def _modulation(c, ada_w, ada_b):
    b, d = c.shape
    n = ada_w.shape[1]
    return pl.pallas_call(
        _mod_kernel,
        grid=(n // d,),
        in_specs=[pl.BlockSpec((b, d), lambda j: (0, 0)),
                  pl.BlockSpec((d, d), lambda j: (0, j)),
                  pl.BlockSpec((1, d), lambda j: (0, j))],
        out_specs=pl.BlockSpec((b, d), lambda j: (0, j)),
        out_shape=jax.ShapeDtypeStruct((b, n), F32),
        compiler_params=_params("arbitrary"),
        name="modulation",
    )(c, ada_w, ada_b.reshape(1, n))


def _rope_kernel(pos_ref, inv_ref, cos_ref, sin_ref):
    ang = pos_ref[...].astype(F32) * inv_ref[...]
    cos_ref[...] = jnp.cos(ang)
    sin_ref[...] = jnp.sin(ang)


def _rope_tables(positions):
    t = positions.size
    half = MLA_ROPE // 2
    per_row = LANES // half
    inv = 1.0 / (ROPE_THETA ** (jnp.arange(0, MLA_ROPE, 2, dtype=F32) / MLA_ROPE))
    pos_rep = jnp.repeat(positions.reshape(t), half).reshape(t // per_row, LANES)
    inv_t = jnp.tile(inv, per_row).reshape(1, LANES)
    rows = t // per_row
    blk = min(rows, 512)
    cos, sin = pl.pallas_call(
        _rope_kernel,
        grid=(rows // blk,),
        in_specs=[pl.BlockSpec((blk, LANES), lambda i: (i, 0)),
                  pl.BlockSpec((1, LANES), lambda i: (0, 0))],
        out_specs=[pl.BlockSpec((blk, LANES), lambda i: (i, 0))] * 2,
        out_shape=[jax.ShapeDtypeStruct((rows, LANES), F32)] * 2,
        compiler_params=_params("arbitrary"),
        name="rope_tables",
    )(pos_rep, inv_t)
    return cos.reshape(t, half), sin.reshape(t, half)


def _inproj_kernel(x_ref, mod_ref, gpre_ref, win_ref, qn_ref, wq_ref, kvn_ref, wkv_ref,
                   cos_ref, sin_ref, z_ref, xbc_ref, dt_ref, q_ref, k_ref, v_ref):
    x = x_ref[...]
    sh = mod_ref[0, 0:1, :]
    sc = mod_ref[0, 1:2, :]
    h = (_rms(x, gpre_ref[...]) * (1.0 + sc) + sh).astype(BF16)
    u = jnp.dot(h, win_ref[...], preferred_element_type=F32)
    z_ref[...] = u[:, _C_Z:_C_XBC].astype(BF16)
    xbc_ref[...] = u[:, _C_XBC:_C_CQ].astype(BF16)
    dt_ref[...] = u[:, _C_DT:_C_END]
    cos_t = cos_ref[...]
    sin_t = sin_ref[...]
    cq = _rms(u[:, _C_CQ:_C_CKV], qn_ref[...]).astype(BF16)
    q2 = jnp.dot(cq, wq_ref[...], preferred_element_type=F32)
    scale = MLA_QK ** -0.5
    nq = MLA_HEADS * HEAD_PAD
    for hd in range(MLA_HEADS):
        a = q2[:, hd * HEAD_PAD:(hd + 1) * HEAD_PAD]
        b = q2[:, nq + hd * HEAD_PAD:nq + (hd + 1) * HEAD_PAD]
        q_ref[:, hd * HEAD_PAD:(hd + 1) * HEAD_PAD] = ((a * cos_t + b * sin_t) * scale).astype(BF16)
    ckv = _rms(u[:, _C_CKV:_C_KR], kvn_ref[...]).astype(BF16)
    kv = jnp.dot(ckv, wkv_ref[...], preferred_element_type=F32)
    k_pe = u[:, _C_KR:_C_KRS] * cos_t + u[:, _C_KRS:_C_DT] * sin_t
    for hd in range(MLA_HEADS):
        k_ref[:, hd * HEAD_PAD:(hd + 1) * HEAD_PAD] = (
            kv[:, hd * HEAD_PAD:(hd + 1) * HEAD_PAD] + k_pe).astype(BF16)
    v_ref[...] = kv[:, nq:].astype(BF16)


def _inproj_weights(w_in, w_uq, w_ukv):
    d = w_in.shape[0]
    half = MLA_ROPE // 2
    o_z, o_xbc, o_dt = 0, SSD_INNER, SSD_INNER + SSD_XBC
    o_cq = o_dt + SSD_HEADS
    o_ckv = o_cq + MLA_Q_RANK
    o_kr = o_ckv + MLA_KV_RANK
    zeros = lambda n: jnp.zeros((d, n), F32)
    kr = w_in[:, o_kr:o_kr + MLA_ROPE]
    kr_blk = jnp.concatenate([zeros(MLA_NOPE), kr, zeros(HEAD_PAD - MLA_QK)], axis=1)
    krs_blk = jnp.concatenate([zeros(MLA_NOPE), -kr[:, half:], kr[:, :half], zeros(HEAD_PAD - MLA_QK)], axis=1)
    dt_blk = jnp.concatenate([w_in[:, o_dt:o_dt + SSD_HEADS], zeros(LANES - SSD_HEADS)], axis=1)
    w_in_r = jnp.concatenate([w_in[:, o_z:o_dt], w_in[:, o_cq:o_kr], kr_blk, krs_blk, dt_blk], axis=1)
    r = w_uq.shape[0]
    zq = jnp.zeros((r, MLA_HEADS, HEAD_PAD - MLA_QK), F32)
    zn = jnp.zeros((r, MLA_HEADS, MLA_NOPE), F32)
    wq_plain = jnp.concatenate([w_uq, zq], axis=2).reshape(r, MLA_HEADS * HEAD_PAD)
    wq_rot = jnp.concatenate([zn, -w_uq[:, :, MLA_NOPE + half:], w_uq[:, :, MLA_NOPE:MLA_NOPE + half], zq],
                             axis=2).reshape(r, MLA_HEADS * HEAD_PAD)
    wq2 = jnp.concatenate([wq_plain, wq_rot], axis=1)
    rk = w_ukv.shape[0]
    zk = jnp.zeros((rk, MLA_HEADS, HEAD_PAD - MLA_NOPE), F32)
    wk = jnp.concatenate([w_ukv[:, :, :MLA_NOPE], zk], axis=2).reshape(rk, MLA_HEADS * HEAD_PAD)
    wv = w_ukv[:, :, MLA_NOPE:].reshape(rk, MLA_INNER)
    wkv = jnp.concatenate([wk, wv], axis=1)
    return w_in_r.astype(BF16), wq2.astype(BF16), wkv.astype(BF16)


def _inproj(x2, mod3, gpre, w_in_r, q_norm, wq2, kv_norm, wkv, cos_t, sin_t, seq, tm):
    t, d = x2.shape
    per_b = seq // tm
    row = lambda i: (i, 0)
    const = lambda i: (0, 0)
    nqk = MLA_HEADS * HEAD_PAD
    return pl.pallas_call(
        _inproj_kernel,
        grid=(t // tm,),
        in_specs=[pl.BlockSpec((tm, d), row),
                  pl.BlockSpec((1, N_MOD, d), lambda i: (i // per_b, 0, 0)),
                  pl.BlockSpec((1, d), const),
                  pl.BlockSpec(w_in_r.shape, const),
                  pl.BlockSpec((1, MLA_Q_RANK), const),
                  pl.BlockSpec(wq2.shape, const),
                  pl.BlockSpec((1, MLA_KV_RANK), const),
                  pl.BlockSpec(wkv.shape, const),
                  pl.BlockSpec((tm, LANES), row),
                  pl.BlockSpec((tm, LANES), row)],
        out_specs=[pl.BlockSpec((tm, SSD_INNER), row),
                   pl.BlockSpec((tm, SSD_XBC), row),
                   pl.BlockSpec((tm, LANES), row),
                   pl.BlockSpec((tm, nqk), row),
                   pl.BlockSpec((tm, nqk), row),
                   pl.BlockSpec((tm, MLA_INNER), row)],
        out_shape=[jax.ShapeDtypeStruct((t, SSD_INNER), BF16),
                   jax.ShapeDtypeStruct((t, SSD_XBC), BF16),
                   jax.ShapeDtypeStruct((t, LANES), F32),
                   jax.ShapeDtypeStruct((t, nqk), BF16),
                   jax.ShapeDtypeStruct((t, nqk), BF16),
                   jax.ShapeDtypeStruct((t, MLA_INNER), BF16)],
        compiler_params=_params("arbitrary"),
        name="inproj",
    )(x2, mod3, gpre, w_in_r, q_norm, wq2, kv_norm, wkv, cos_t, sin_t)


def _ssd_kernel(xbc_ref, z_ref, dt_ref, cw_ref, cb_ref, dtb_ref, alog_ref, dsk_ref, gn_ref,
                e64_ref, e128_ref, y_ref, tail_ref, state_ref):
    L = SSD_CHUNK
    gw = SSD_INNER // SSD_GROUPS

    @pl.when(pl.program_id(1) == 0)
    def _():
        tail_ref[...] = jnp.zeros_like(tail_ref)
        state_ref[...] = jnp.zeros_like(state_ref)

    cur = xbc_ref[...].astype(F32)
    prev8 = tail_ref[...]
    row8 = lax.broadcasted_iota(jnp.int32, prev8.shape, 0)
    acc = cur * cw_ref[SSD_CONV - 1:SSD_CONV, :] + cb_ref[...]
    for j in range(1, SSD_CONV):
        r = pltpu.roll(cur, j, 0)
        head = jnp.where(row8 < j, pltpu.roll(prev8, j, 0), r[0:8])
        r = jnp.concatenate([head, r[8:]], axis=0)
        acc = acc + r * cw_ref[SSD_CONV - 1 - j:SSD_CONV - j, :]
    tail_ref[...] = cur[L - 8:L]
    act = _silu(acc)
    xs = act[:, :SSD_INNER]
    bm = act[:, SSD_INNER:SSD_INNER + SSD_GROUPS * SSD_STATE].astype(BF16)
    cm = act[:, SSD_INNER + SSD_GROUPS * SSD_STATE:].astype(BF16)

    dt_in = dt_ref[...] + dtb_ref[...]
    dt = jnp.maximum(dt_in, 0.0) + jnp.log(1.0 + jnp.exp(-jnp.abs(dt_in)))
    adt = dt * (-jnp.exp(alog_ref[...]))
    ri = lax.broadcasted_iota(jnp.int32, (L, L), 0)
    ci = lax.broadcasted_iota(jnp.int32, (L, L), 1)
    causal = ci <= ri
    tril = causal.astype(F32)
    a_cs = jnp.dot(tril, adt, precision=HIGHEST, preferred_element_type=F32)
    a_cs_t = a_cs.T
    cs64 = jnp.dot(a_cs, e64_ref[...], precision=HIGHEST, preferred_element_type=F32)
    cs128 = jnp.dot(a_cs, e128_ref[...], precision=HIGHEST, preferred_element_type=F32)
    dt64 = jnp.dot(dt, e64_ref[...], precision=HIGHEST, preferred_element_type=F32)

    xd = xs * dt64
    xd_b = xd.astype(BF16)
    last = cs64[L - 1:L, :]
    xdw = (xd * jnp.exp(last - cs64)).astype(BF16)
    chunk_decay = jnp.exp(last)
    in_decay = jnp.exp(cs64)

    lane = lax.broadcasted_iota(jnp.int32, (L, LANES), 1)
    lo = lane < SSD_HEAD_DIM
    zero_b = jnp.zeros((L, LANES), BF16)
    y_parts = []
    new_states = []
    for g in range(SSD_GROUPS):
        bg = bm[:, g * SSD_STATE:(g + 1) * SSD_STATE]
        cg = cm[:, g * SSD_STATE:(g + 1) * SSD_STATE]
        cb = lax.dot_general(cg, bg, (((1,), (1,)), ((), ())), preferred_element_type=F32)
        prev = state_ref[:, g * gw:(g + 1) * gw]
        y_off = jnp.dot(cg, prev.astype(BF16), preferred_element_type=F32) * in_decay[:, g * gw:(g + 1) * gw]
        st = lax.dot_general(bg, xdw[:, g * gw:(g + 1) * gw], (((0,), (0,)), ((), ())),
                             preferred_element_type=F32)
        new_states.append(prev * chunk_decay[:, g * gw:(g + 1) * gw] + st)
        heads_per_group = SSD_HEADS // SSD_GROUPS
        for pair in range(heads_per_group // 2):
            h0 = g * heads_per_group + 2 * pair
            blk = xd_b[:, h0 * SSD_HEAD_DIM:(h0 + 2) * SSD_HEAD_DIM]
            y_pair = None
            for k in range(2):
                hh = h0 + k
                diff = cs128[:, hh * LANES:(hh + 1) * LANES] - a_cs_t[hh:hh + 1, :]
                decay = jnp.where(causal, jnp.exp(diff), 0.0)
                m = (cb * decay).astype(BF16)
                rhs = jnp.where(lo, blk, zero_b) if k == 0 else jnp.where(lo, zero_b, blk)
                part = jnp.dot(m, rhs, preferred_element_type=F32)
                y_pair = part if y_pair is None else y_pair + part
            c0 = (2 * pair) * SSD_HEAD_DIM
            y_parts.append(y_pair + y_off[:, c0:c0 + LANES])
    state_ref[...] = jnp.concatenate(new_states, axis=1)
    y = jnp.concatenate(y_parts, axis=1) + dsk_ref[...] * xs
    gated = y * _silu(z_ref[...].astype(F32))
    y_ref[...] = _rms(gated, gn_ref[...]).astype(BF16)


def _ssd(xbc, z, dt_raw, conv_w, conv_b, dt_bias, a_log, d_skip, ssd_norm, batch, seq):
    t = xbc.shape[0]
    nc = seq // SSD_CHUNK
    L = SSD_CHUNK
    row = lambda b, c: (b * nc + c, 0)
    const = lambda b, c: (0, 0)
    pad = lambda v: jnp.concatenate([v.reshape(1, -1), jnp.zeros((1, LANES - v.size), F32)], axis=1)
    head_of_lane64 = jnp.arange(SSD_INNER) // SSD_HEAD_DIM
    head_of_lane128 = jnp.arange(SSD_HEADS * LANES) // LANES
    rows = jnp.arange(LANES)[:, None]
    e64 = (rows == head_of_lane64[None, :]).astype(F32)
    e128 = (rows == head_of_lane128[None, :]).astype(F32)
    dsk = jnp.repeat(d_skip.astype(F32), SSD_HEAD_DIM).reshape(1, SSD_INNER)
    return pl.pallas_call(
        _ssd_kernel,
        grid=(batch, nc),
        in_specs=[pl.BlockSpec((L, SSD_XBC), row),
                  pl.BlockSpec((L, SSD_INNER), row),
                  pl.BlockSpec((L, LANES), row),
                  pl.BlockSpec((SSD_CONV, SSD_XBC), const),
                  pl.BlockSpec((1, SSD_XBC), const),
                  pl.BlockSpec((1, LANES), const),
                  pl.BlockSpec((1, LANES), const),
                  pl.BlockSpec((1, SSD_INNER), const),
                  pl.BlockSpec((1, SSD_INNER), const),
                  pl.BlockSpec(e64.shape, const),
                  pl.BlockSpec(e128.shape, const)],
        out_specs=pl.BlockSpec((L, SSD_INNER), row),
        out_shape=jax.ShapeDtypeStruct((t, SSD_INNER), BF16),
        scratch_shapes=[pltpu.VMEM((8, SSD_XBC), F32),
                        pltpu.VMEM((SSD_STATE, SSD_INNER), F32)],
        compiler_params=_params("arbitrary", "arbitrary"),
        name="ssd",
    )(xbc, z, dt_raw, conv_w, conv_b.reshape(1, -1), pad(dt_bias), pad(a_log), dsk,
      ssd_norm.reshape(1, -1), e64, e128)


def _attn_kernel(q_ref, k_ref, v_ref, gn_ref, o_ref, *, blk):
    qi = pl.program_id(1)
    ri = lax.broadcasted_iota(jnp.int32, (blk, blk), 0)
    ci = lax.broadcasted_iota(jnp.int32, (blk, blk), 1)
    causal = ci <= ri
    lo = lax.broadcasted_iota(jnp.int32, (blk, LANES), 1) < MLA_V
    zero_b = jnp.zeros((blk, LANES), BF16)
    dims = (((1,), (1,)), ((), ()))
    outs = []
    for pair in range(MLA_HEADS // 2):
        ha, hb = 2 * pair, 2 * pair + 1
        qa = q_ref[:, ha * HEAD_PAD:(ha + 1) * HEAD_PAD]
        qb = q_ref[:, hb * HEAD_PAD:(hb + 1) * HEAD_PAD]

        def step(j, carry, masked, qa=qa, qb=qb, ha=ha, hb=hb, pair=pair):
            ma, la, mb, lb, acc = carry
            rows = pl.ds(pl.multiple_of(j * blk, blk), blk)
            vv = v_ref[rows, pair * LANES:(pair + 1) * LANES]
            new = []
            for q, hd, m_old, l_old in ((qa, ha, ma, la), (qb, hb, mb, lb)):
                kk = k_ref[rows, hd * HEAD_PAD:(hd + 1) * HEAD_PAD]
                s = lax.dot_general(q, kk, dims, preferred_element_type=F32)
                if masked:
                    s = jnp.where(causal, s, -jnp.inf)
                m_new = jnp.maximum(m_old, jnp.max(s, axis=-1, keepdims=True))
                p = jnp.exp(s - m_new)
                alpha = jnp.exp(m_old - m_new)
                l_new = alpha * l_old + jnp.sum(p, axis=-1, keepdims=True)
                new.append((m_new, l_new, alpha, p.astype(BF16)))
            (ma, la, aa, pa), (mb, lb, ab, pb) = new
            acc = (acc * jnp.where(lo, aa, ab)
                   + jnp.dot(pa, jnp.where(lo, vv, zero_b), preferred_element_type=F32)
                   + jnp.dot(pb, jnp.where(lo, zero_b, vv), preferred_element_type=F32))
            return ma, la, mb, lb, acc

        neg = jnp.full((blk, 1), -jnp.inf, F32)
        zero = jnp.zeros((blk, 1), F32)
        init = (neg, zero, neg, zero, jnp.zeros((blk, LANES), F32))
        carry = lax.fori_loop(0, qi, functools.partial(step, masked=False), init)
        ma, la, mb, lb, acc = step(qi, carry, True)
        outs.append(acc / jnp.where(lo, la, lb))
    y = jnp.concatenate(outs, axis=1)
    o_ref[...] = _rms(y, gn_ref[...]).astype(BF16)


def _attention(q, k, v, gn, batch, seq, blk):
    t = q.shape[0]
    nq = seq // blk
    nqk = MLA_HEADS * HEAD_PAD
    return pl.pallas_call(
        functools.partial(_attn_kernel, blk=blk),
        grid=(batch, nq),
        in_specs=[pl.BlockSpec((blk, nqk), lambda b, i: (b * nq + i, 0)),
                  pl.BlockSpec((seq, nqk), lambda b, i: (b, 0)),
                  pl.BlockSpec((seq, MLA_INNER), lambda b, i: (b, 0)),
                  pl.BlockSpec((1, MLA_INNER), lambda b, i: (0, 0))],
        out_specs=pl.BlockSpec((blk, MLA_INNER), lambda b, i: (b * nq + i, 0)),
        out_shape=jax.ShapeDtypeStruct((t, MLA_INNER), BF16),
        compiler_params=_params("arbitrary", "arbitrary"),
        name="attention",
    )(q, k, v, gn)


def _outproj_kernel(ys_ref, ym_ref, x_ref, mod_ref, wtop_ref, wbot_ref, gpm_ref, gpf_ref,
                    wr_ref, br_ref, x1_ref, h2_ref, route_ref, counts_ref, run_ref):
    @pl.when(pl.program_id(0) == 0)
    def _():
        run_ref[...] = jnp.zeros_like(run_ref)

    y = (jnp.dot(ys_ref[...], wtop_ref[...], preferred_element_type=F32)
         + jnp.dot(ym_ref[...], wbot_ref[...], preferred_element_type=F32))
    g1 = mod_ref[0, 2:3, :]
    sh2 = mod_ref[0, 3:4, :]
    sc2 = mod_ref[0, 4:5, :]
    x1 = x_ref[...] + g1 * _rms(y, gpm_ref[...])
    x1_ref[...] = x1
    h2 = _rms(x1, gpf_ref[...]) * (1.0 + sc2) + sh2
    h2_ref[...] = h2
    logits = jnp.dot(h2, wr_ref[...], precision=HIGHEST, preferred_element_type=F32) + br_ref[...]
    lane = lax.broadcasted_iota(jnp.int32, logits.shape, 1).astype(F32)
    ninf = -jnp.inf
    big = 1e9
    is_g = lane < MOE_GROUPS
    gl = jnp.where(is_g, logits, ninf)
    gmax = jnp.max(gl, axis=-1, keepdims=True)
    gsum = jnp.sum(jnp.where(is_g, jnp.exp(logits - gmax), 0.0), axis=-1, keepdims=True)
    g_w = 1.0 / gsum
    g_idx = jnp.min(jnp.where(gl == gmax, lane, big), axis=-1, keepdims=True)
    first = MOE_GROUPS + MOE_EXPERTS_PER_GROUP * g_idx
    el = jnp.where(lane >= first, jnp.where(lane < first + MOE_EXPERTS_PER_GROUP, logits, ninf), ninf)
    m1 = jnp.max(el, axis=-1, keepdims=True)
    i1 = jnp.min(jnp.where(el == m1, lane, big), axis=-1, keepdims=True)
    el2 = jnp.where(lane == i1, ninf, el)
    m2 = jnp.max(el2, axis=-1, keepdims=True)
    i2 = jnp.min(jnp.where(el2 == m2, lane, big), axis=-1, keepdims=True)
    r = jnp.exp(m2 - m1)
    w1 = g_w / (1.0 + r)
    w2 = g_w * r / (1.0 + r)
    sel1 = lane == i1
    sel2 = lane == i2
    both = jnp.where(sel1, 1.0, jnp.where(sel2, 1.0, 0.0))
    tm = logits.shape[0]
    before = (lax.broadcasted_iota(jnp.int32, (tm, tm), 1)
              < lax.broadcasted_iota(jnp.int32, (tm, tm), 0)).astype(BF16)
    prefix = jnp.dot(before, both.astype(BF16), preferred_element_type=F32) + run_ref[...]
    rank1 = jnp.sum(jnp.where(sel1, prefix, 0.0), axis=-1, keepdims=True)
    rank2 = jnp.sum(jnp.where(sel2, prefix, 0.0), axis=-1, keepdims=True)
    run_ref[...] += jnp.sum(both, axis=0, keepdims=True)
    counts_ref[...] = run_ref[...]
    fields = (i1 - MOE_GROUPS, i2 - MOE_GROUPS, rank1, rank2, w1, w2)
    route = jnp.zeros_like(logits)
    for k, val in enumerate(fields):
        route = jnp.where(lane == k, val, route)
    route_ref[...] = route


def _outproj(y_ssd, y_mla, x2, mod3, w_top, w_bot, gpm, gpf, w_router, b_router, seq, tm):
    t, d = x2.shape
    per_b = seq // tm
    row = lambda i: (i, 0)
    const = lambda i: (0, 0)
    return pl.pallas_call(
        _outproj_kernel,
        grid=(t // tm,),
        in_specs=[pl.BlockSpec((tm, SSD_INNER), row),
                  pl.BlockSpec((tm, MLA_INNER), row),
                  pl.BlockSpec((tm, d), row),
                  pl.BlockSpec((1, N_MOD, d), lambda i: (i // per_b, 0, 0)),
                  pl.BlockSpec(w_top.shape, const),
                  pl.BlockSpec(w_bot.shape, const),
                  pl.BlockSpec((1, d), const),
                  pl.BlockSpec((1, d), const),
                  pl.BlockSpec(w_router.shape, const),
                  pl.BlockSpec((1, LANES), const)],
        out_specs=[pl.BlockSpec((tm, d), row),
                   pl.BlockSpec((tm, d), row),
                   pl.BlockSpec((tm, LANES), row),
                   pl.BlockSpec((1, LANES), const)],
        out_shape=[jax.ShapeDtypeStruct((t, d), F32),
                   jax.ShapeDtypeStruct((t, d), F32),
                   jax.ShapeDtypeStruct((t, LANES), F32),
                   jax.ShapeDtypeStruct((1, LANES), F32)],
        scratch_shapes=[pltpu.VMEM((1, LANES), F32)],
        compiler_params=_params("arbitrary"),
        name="outproj_router",
    )(y_ssd, y_mla, x2, mod3, w_top, w_bot, gpm, gpf, w_router, b_router)


def _route_plan(route, counts, tile):
    t = route.shape[0]
    n_tiles = 2 * t // tile + MOE_EXPERTS
    cnt = counts[0, MOE_GROUPS:MOE_GROUPS + MOE_EXPERTS].astype(jnp.int32)
    tiles_e = (cnt + tile - 1) // tile
    ends = jnp.cumsum(tiles_e)
    offs = (ends - tiles_e) * tile
    n_used = ends[-1]
    pos = offs[route[:, 0:2].astype(jnp.int32)] + route[:, 2:4].astype(jnp.int32)
    j = jnp.minimum(jnp.arange(n_tiles, dtype=jnp.int32), n_used - 1)
    tile_expert = jnp.sum((ends[None, :] <= j[:, None]).astype(jnp.int32), axis=1)
    tile_expert = jnp.minimum(tile_expert, MOE_EXPERTS - 1)
    last_tiles = jnp.where(tiles_e > 0, ends - 1, -1)
    tail = n_used + jnp.arange(MOE_EXPERTS, dtype=jnp.int32)
    tail = jnp.where(tail < n_tiles, tail, -1)
    fill = jnp.concatenate([last_tiles, tail]).astype(jnp.int32)
    return pos.astype(jnp.int32), tile_expert, n_used.reshape(1).astype(jnp.int32), fill


def _dispatch_kernel(fill_ref, pos_ref, h_ref, xs_ref, zero_ref, sem, *, tile):
    rows = h_ref.shape[0]

    @pl.when(pl.program_id(0) == 0)
    def _():
        zero_ref[...] = jnp.zeros_like(zero_ref)
        for wait in (False, True):
            def body(k, carry, wait=wait):
                @pl.when(fill_ref[k] >= 0)
                def _():
                    start = pl.multiple_of(fill_ref[k] * tile, tile)
                    cp = pltpu.make_async_copy(zero_ref, xs_ref.at[pl.ds(start, tile), :], sem.at[1])
                    cp.wait() if wait else cp.start()
                return carry
            lax.fori_loop(0, fill_ref.shape[0], body, 0)

    def row_copy(r, k):
        dst = xs_ref.at[pl.ds(pos_ref[0, 0, 2 * r + k], 1), :]
        return pltpu.make_async_copy(h_ref.at[pl.ds(r, 1), :], dst, sem.at[0])

    for wait in (False, True):
        def body(r, carry, wait=wait):
            for k in range(2):
                cp = row_copy(r, k)
                cp.wait() if wait else cp.start()
            return carry
        lax.fori_loop(0, rows, body, 0)


def _dispatch(h2, pos, fill, tile, tm):
    t, d = h2.shape
    n_rows = (2 * t // tile + MOE_EXPERTS) * tile
    pos3 = pos.reshape(t // tm, 1, 2 * tm)
    return pl.pallas_call(
        functools.partial(_dispatch_kernel, tile=tile),
        grid_spec=pltpu.PrefetchScalarGridSpec(
            num_scalar_prefetch=1,
            grid=(t // tm,),
            in_specs=[pl.BlockSpec((1, 1, 2 * tm), lambda i, f: (i, 0, 0), memory_space=pltpu.SMEM),
                      pl.BlockSpec((tm, d), lambda i, f: (i, 0))],
            out_specs=pl.BlockSpec(memory_space=pl.ANY),
            scratch_shapes=[pltpu.VMEM((tile, d), F32), pltpu.SemaphoreType.DMA((2,))]),
        out_shape=jax.ShapeDtypeStruct((n_rows, d), F32),
        compiler_params=_params("arbitrary"),
        name="moe_dispatch",
    )(fill, pos3, h2)


def _experts_kernel(te_ref, nt_ref, xs_ref, wgu_ref, wd_ref, o_ref):
    used = pl.program_id(0) < nt_ref[0]

    @pl.when(used)
    def _():
        gu = jnp.dot(xs_ref[...].astype(BF16), wgu_ref[0], preferred_element_type=F32)
        hid = (_silu(gu[:, :MOE_FF]) * gu[:, MOE_FF:]).astype(BF16)
        o_ref[...] = jnp.dot(hid, wd_ref[0], preferred_element_type=F32)

    @pl.when(jnp.logical_not(used))
    def _():
        o_ref[...] = jnp.zeros_like(o_ref)


def _experts(xs, tile_expert, n_used, wgu, wd, tile):
    n_rows, d = xs.shape
    return pl.pallas_call(
        _experts_kernel,
        grid_spec=pltpu.PrefetchScalarGridSpec(
            num_scalar_prefetch=2,
            grid=(n_rows // tile,),
            in_specs=[pl.BlockSpec((tile, d), lambda j, te, nt: (jnp.minimum(j, nt[0] - 1), 0)),
                      pl.BlockSpec((1, d, 2 * MOE_FF), lambda j, te, nt: (te[j], 0, 0)),
                      pl.BlockSpec((1, MOE_FF, d), lambda j, te, nt: (te[j], 0, 0))],
            out_specs=pl.BlockSpec((tile, d), lambda j, te, nt: (j, 0))),
        out_shape=jax.ShapeDtypeStruct((n_rows, d), F32),
        compiler_params=_params("arbitrary"),
        name="moe_experts",
    )(tile_expert, n_used, xs, wgu, wd)


def _combine_kernel(pos_ref, route_ref, x1_ref, mod_ref, gpost_ref, ys_ref, o_ref, buf_ref, sem):
    rows = x1_ref.shape[0]

    def row_copy(r, k):
        src = ys_ref.at[pl.ds(pos_ref[0, 0, 2 * r + k], 1), :]
        return pltpu.make_async_copy(src, buf_ref.at[k, pl.ds(r, 1), :], sem.at[0])

    for wait in (False, True):
        def body(r, carry, wait=wait):
            for k in range(2):
                cp = row_copy(r, k)
                cp.wait() if wait else cp.start()
            return carry
        lax.fori_loop(0, rows, body, 0)

    route = route_ref[...]
    y = route[:, 4:5] * buf_ref[0] + route[:, 5:6] * buf_ref[1]
    g2 = mod_ref[0, 5:6, :]
    o_ref[...] = x1_ref[...] + g2 * _rms(y, gpost_ref[...])


def _combine(ys, pos, route, x1, mod3, gpost, seq, tm):
    t, d = x1.shape
    per_b = seq // tm
    pos3 = pos.reshape(t // tm, 1, 2 * tm)
    return pl.pallas_call(
        _combine_kernel,
        grid=(t // tm,),
        in_specs=[pl.BlockSpec((1, 1, 2 * tm), lambda i: (i, 0, 0), memory_space=pltpu.SMEM),
                  pl.BlockSpec((tm, LANES), lambda i: (i, 0)),
                  pl.BlockSpec((tm, d), lambda i: (i, 0)),
                  pl.BlockSpec((1, N_MOD, d), lambda i: (i // per_b, 0, 0)),
                  pl.BlockSpec((1, d), lambda i: (0, 0)),
                  pl.BlockSpec(memory_space=pl.ANY)],
        out_specs=pl.BlockSpec((tm, d), lambda i: (i, 0)),
        out_shape=jax.ShapeDtypeStruct((t, d), F32),
        scratch_shapes=[pltpu.VMEM((2, tm, d), F32), pltpu.SemaphoreType.DMA((1,))],
        compiler_params=_params("arbitrary"),
        name="moe_combine",
    )(pos3, route, x1, mod3, gpost, ys)


def kernel(x, c, positions, ada_w, ada_b, pre_norm_mix, post_norm_mix, pre_norm_ffn, post_norm_ffn, w_in, conv_w, conv_b, dt_bias, a_log, d_skip, ssd_norm, q_norm, w_uq, kv_norm, w_ukv, mla_out_norm, w_out, w_group_router, b_group_router, w_expert_router, b_expert_router, w_gate, w_up, w_down):
    batch, seq, d = x.shape
    t = batch * seq
    depth = ada_w.shape[0]
    tm = min(512, seq)
    half = MLA_ROPE // 2

    cos16, sin16 = _rope_tables(positions)
    ones = lambda n: jnp.ones((t, n), F32)
    zeros = lambda n: jnp.zeros((t, n), F32)
    cos_t = jnp.concatenate([ones(MLA_NOPE), cos16, cos16, ones(HEAD_PAD - MLA_QK)], axis=1)
    sin_t = jnp.concatenate([zeros(MLA_NOPE), sin16, sin16, zeros(HEAD_PAD - MLA_QK)], axis=1)

    x2 = x.reshape(t, d)
    for l in range(depth):
        mod3 = _modulation(c, ada_w[l], ada_b[l]).reshape(batch, N_MOD, d)
        w_in_r, wq2, wkv = _inproj_weights(w_in[l], w_uq[l], w_ukv[l])
        z, xbc, dt_raw, q, k, v = _inproj(
            x2, mod3, pre_norm_mix[l].reshape(1, d), w_in_r, q_norm[l].reshape(1, -1), wq2,
            kv_norm[l].reshape(1, -1), wkv, cos_t, sin_t, seq, tm)
        y_ssd = _ssd(xbc, z, dt_raw, conv_w[l], conv_b[l], dt_bias[l], a_log[l], d_skip[l],
                     ssd_norm[l], batch, seq)
        y_mla = _attention(q, k, v, mla_out_norm[l].reshape(1, -1), batch, seq, min(256, seq))
        w_o = w_out[l].astype(BF16)
        pad_r = LANES - MOE_GROUPS - MOE_EXPERTS
        w_router = jnp.concatenate([w_group_router[l], w_expert_router[l], jnp.zeros((d, pad_r), F32)], axis=1)
        b_router = jnp.concatenate([b_group_router[l].reshape(-1), b_expert_router[l].reshape(-1),
                                    jnp.zeros((pad_r,), F32)]).reshape(1, LANES)
        x1, h2, route, counts = _outproj(y_ssd, y_mla, x2, mod3, w_o[:SSD_INNER], w_o[SSD_INNER:],
                                         post_norm_mix[l].reshape(1, d), pre_norm_ffn[l].reshape(1, d),
                                         w_router, b_router, seq, tm)
        tile = 256
        tm_rows = min(256, seq)
        pos, tile_expert, n_used, fill = _route_plan(route, counts, tile)
        xs = _dispatch(h2, pos, fill, tile, tm_rows)
        wgu = jnp.concatenate([w_gate[l], w_up[l]], axis=2).astype(BF16)
        ys = _experts(xs, tile_expert, n_used, wgu, w_down[l].astype(BF16), tile)
        x2 = _combine(ys, pos, route, x1, mod3, post_norm_ffn[l].reshape(1, d), seq, tm_rows)
    return x2.reshape(batch, seq, d)
```

```python
import functools
import math

import jax
import jax.numpy as jnp
from jax import lax
from jax.experimental import pallas as pl
from jax.experimental.pallas import tpu as pltpu

F32 = jnp.float32
BF16 = jnp.bfloat16
HIGHEST = lax.Precision.HIGHEST

D_MODEL = 1024
SSD_HEADS = 8
SSD_HEAD_DIM = 64
SSD_INNER = SSD_HEADS * SSD_HEAD_DIM
SSD_GROUPS = 2
SSD_STATE = 128
SSD_CONV = 4
SSD_CHUNK = 128
SSD_XBC = SSD_INNER + 2 * SSD_GROUPS * SSD_STATE
MLA_HEADS = 8
MLA_NOPE = 64
MLA_ROPE = 32
MLA_QK = MLA_NOPE + MLA_ROPE
MLA_V = 64
MLA_Q_RANK = 256
MLA_KV_RANK = 128
MLA_INNER = MLA_HEADS * MLA_V
ROPE_THETA = 10000.0
MOE_GROUPS = 4
MOE_EXPERTS_PER_GROUP = 8
MOE_EXPERTS = MOE_GROUPS * MOE_EXPERTS_PER_GROUP
MOE_FF = 256
N_MOD = 6
EPS = 1e-6

LANES = 128
HEAD_PAD = 128
ATT_BLOCK = 256
VMEM_LIMIT = 48 * 1024 * 1024

_C_Z = 0
_C_XBC = _C_Z + SSD_INNER
_C_CQ = _C_XBC + SSD_XBC
_C_CKV = _C_CQ + MLA_Q_RANK
_C_KR = _C_CKV + MLA_KV_RANK
_C_KRS = _C_KR + LANES
_C_DT = _C_KRS + LANES
_C_END = _C_DT + LANES


def _silu(v):
    return v * (1.0 / (1.0 + jnp.exp(-v)))


def _rms(v, gain):
    return v * lax.rsqrt(jnp.mean(v * v, axis=-1, keepdims=True) + EPS) * gain


def _params(*sem, flags=None):
    return pltpu.CompilerParams(dimension_semantics=sem, vmem_limit_bytes=VMEM_LIMIT, flags=flags)


def _mod_kernel(c_ref, w_ref, b_ref, o_ref):
    o_ref[...] = jnp.dot(_silu(c_ref[...]), w_ref[...], precision=HIGHEST,
                         preferred_element_type=F32) + b_ref[...]


def _modulation(c, ada_w, ada_b):
    b, d = c.shape
    n = ada_w.shape[1]
    return pl.pallas_call(
        _mod_kernel,
        grid=(n // d,),
        in_specs=[pl.BlockSpec((b, d), lambda j: (0, 0)),
                  pl.BlockSpec((d, d), lambda j: (0, j)),
                  pl.BlockSpec((1, d), lambda j: (0, j))],
        out_specs=pl.BlockSpec((b, d), lambda j: (0, j)),
        out_shape=jax.ShapeDtypeStruct((b, n), F32),
        compiler_params=_params("arbitrary"),
        name="modulation",
    )(c, ada_w, ada_b.reshape(1, n))


def _rope_kernel(pos_ref, inv_ref, cos_ref, sin_ref):
    ang = pos_ref[...].astype(F32) * inv_ref[...]
    cos_ref[...] = jnp.cos(ang)
    sin_ref[...] = jnp.sin(ang)


def _rope_tables(positions):
    t = positions.size
    half = MLA_ROPE // 2
    per_row = LANES // half
    inv = 1.0 / (ROPE_THETA ** (jnp.arange(0, MLA_ROPE, 2, dtype=F32) / MLA_ROPE))
    pos_rep = jnp.repeat(positions.reshape(t), half).reshape(t // per_row, LANES)
    inv_t = jnp.tile(inv, per_row).reshape(1, LANES)
    rows = t // per_row
    blk = min(rows, 512)
    cos, sin = pl.pallas_call(
        _rope_kernel,
        grid=(rows // blk,),
        in_specs=[pl.BlockSpec((blk, LANES), lambda i: (i, 0)),
                  pl.BlockSpec((1, LANES), lambda i: (0, 0))],
        out_specs=[pl.BlockSpec((blk, LANES), lambda i: (i, 0))] * 2,
        out_shape=[jax.ShapeDtypeStruct((rows, LANES), F32)] * 2,
        compiler_params=_params("arbitrary"),
        name="rope_tables",
    )(pos_rep, inv_t)
    return cos.reshape(t, half), sin.reshape(t, half)


def _inproj_kernel(x_ref, mod_ref, gpre_ref, win_ref, qn_ref, wq_ref, kvn_ref, wk_ref, wvt_ref,
                   cos_ref, sin_ref, z_ref, xbc_ref, dt_ref, q_ref, k_ref, vt_ref):
    x = x_ref[...]
    sh = mod_ref[0, 0:1, :]
    sc = mod_ref[0, 1:2, :]
    h = (_rms(x, gpre_ref[...]) * (1.0 + sc) + sh).astype(BF16)
    u = jnp.dot(h, win_ref[...], preferred_element_type=F32)
    z_ref[...] = u[:, _C_Z:_C_XBC].astype(BF16)
    xbc_ref[...] = u[:, _C_XBC:_C_CQ].astype(BF16)
    dt_ref[...] = u[:, _C_DT:_C_END]
    cos_t = cos_ref[...]
    sin_t = sin_ref[...]
    cq = _rms(u[:, _C_CQ:_C_CKV], qn_ref[...]).astype(BF16)
    q2 = jnp.dot(cq, wq_ref[...], preferred_element_type=F32)
    scale = MLA_QK ** -0.5
    nq = MLA_HEADS * HEAD_PAD
    for hd in range(MLA_HEADS):
        a = q2[:, hd * HEAD_PAD:(hd + 1) * HEAD_PAD]
        b = q2[:, nq + hd * HEAD_PAD:nq + (hd + 1) * HEAD_PAD]
        q_ref[:, hd * HEAD_PAD:(hd + 1) * HEAD_PAD] = ((a * cos_t + b * sin_t) * scale).astype(BF16)
    ckv = _rms(u[:, _C_CKV:_C_KR], kvn_ref[...]).astype(BF16)
    kn = jnp.dot(ckv, wk_ref[...], preferred_element_type=F32)
    k_pe = u[:, _C_KR:_C_KRS] * cos_t + u[:, _C_KRS:_C_DT] * sin_t
    for hd in range(MLA_HEADS):
        k_ref[:, hd * HEAD_PAD:(hd + 1) * HEAD_PAD] = (
            kn[:, hd * HEAD_PAD:(hd + 1) * HEAD_PAD] + k_pe).astype(BF16)
    v_t = lax.dot_general(wvt_ref[...], ckv, (((1,), (1,)), ((), ())), preferred_element_type=F32)
    for s in range(vt_ref.shape[0]):
        vt_ref[s] = v_t[:, s * ATT_BLOCK:(s + 1) * ATT_BLOCK].astype(BF16)


def _inproj_weights(w_in, w_uq, w_ukv):
    d = w_in.shape[0]
    half = MLA_ROPE // 2
    o_z, o_xbc, o_dt = 0, SSD_INNER, SSD_INNER + SSD_XBC
    o_cq = o_dt + SSD_HEADS
    o_ckv = o_cq + MLA_Q_RANK
    o_kr = o_ckv + MLA_KV_RANK
    zeros = lambda n: jnp.zeros((d, n), F32)
    kr = w_in[:, o_kr:o_kr + MLA_ROPE]
    kr_blk = jnp.concatenate([zeros(MLA_NOPE), kr, zeros(HEAD_PAD - MLA_QK)], axis=1)
    krs_blk = jnp.concatenate([zeros(MLA_NOPE), -kr[:, half:], kr[:, :half], zeros(HEAD_PAD - MLA_QK)], axis=1)
    dt_blk = jnp.concatenate([w_in[:, o_dt:o_dt + SSD_HEADS], zeros(LANES - SSD_HEADS)], axis=1)
    w_in_r = jnp.concatenate([w_in[:, o_z:o_dt], w_in[:, o_cq:o_kr], kr_blk, krs_blk, dt_blk], axis=1)
    r = w_uq.shape[0]
    zq = jnp.zeros((r, MLA_HEADS, HEAD_PAD - MLA_QK), F32)
    zn = jnp.zeros((r, MLA_HEADS, MLA_NOPE), F32)
    wq_plain = jnp.concatenate([w_uq, zq], axis=2).reshape(r, MLA_HEADS * HEAD_PAD)
    wq_rot = jnp.concatenate([zn, -w_uq[:, :, MLA_NOPE + half:], w_uq[:, :, MLA_NOPE:MLA_NOPE + half], zq],
                             axis=2).reshape(r, MLA_HEADS * HEAD_PAD)
    wq2 = jnp.concatenate([wq_plain, wq_rot], axis=1)
    rk = w_ukv.shape[0]
    zk = jnp.zeros((rk, MLA_HEADS, HEAD_PAD - MLA_NOPE), F32)
    wk = jnp.concatenate([w_ukv[:, :, :MLA_NOPE], zk], axis=2).reshape(rk, MLA_HEADS * HEAD_PAD)
    wv_t = w_ukv[:, :, MLA_NOPE:].reshape(rk, MLA_INNER).T
    return w_in_r.astype(BF16), wq2.astype(BF16), wk.astype(BF16), wv_t.astype(BF16)


def _inproj(x2, mod3, gpre, w_in_r, q_norm, wq2, kv_norm, wk, wv_t, cos_t, sin_t, seq, tm):
    t, d = x2.shape
    per_b = seq // tm
    row = lambda i: (i, 0)
    const = lambda i: (0, 0)
    nqk = MLA_HEADS * HEAD_PAD
    slabs = tm // ATT_BLOCK
    return pl.pallas_call(
        _inproj_kernel,
        grid=(t // tm,),
        in_specs=[pl.BlockSpec((tm, d), row),
                  pl.BlockSpec((1, N_MOD, d), lambda i: (i // per_b, 0, 0)),
                  pl.BlockSpec((1, d), const),
                  pl.BlockSpec(w_in_r.shape, const),
                  pl.BlockSpec((1, MLA_Q_RANK), const),
                  pl.BlockSpec(wq2.shape, const),
                  pl.BlockSpec((1, MLA_KV_RANK), const),
                  pl.BlockSpec(wk.shape, const),
                  pl.BlockSpec(wv_t.shape, const),
                  pl.BlockSpec((tm, LANES), row),
                  pl.BlockSpec((tm, LANES), row)],
        out_specs=[pl.BlockSpec((tm, SSD_INNER), row),
                   pl.BlockSpec((tm, SSD_XBC), row),
                   pl.BlockSpec((tm, LANES), row),
                   pl.BlockSpec((tm, nqk), row),
                   pl.BlockSpec((tm, nqk), row),
                   pl.BlockSpec((slabs, MLA_INNER, ATT_BLOCK), lambda i: (i, 0, 0))],
        out_shape=[jax.ShapeDtypeStruct((t, SSD_INNER), BF16),
                   jax.ShapeDtypeStruct((t, SSD_XBC), BF16),
                   jax.ShapeDtypeStruct((t, LANES), F32),
                   jax.ShapeDtypeStruct((t, nqk), BF16),
                   jax.ShapeDtypeStruct((t, nqk), BF16),
                   jax.ShapeDtypeStruct((t // ATT_BLOCK, MLA_INNER, ATT_BLOCK), BF16)],
        compiler_params=_params("arbitrary"),
        name="inproj",
    )(x2, mod3, gpre, w_in_r, q_norm, wq2, kv_norm, wk, wv_t, cos_t, sin_t)


def _ssd_kernel(xbc_ref, z_ref, dt_ref, cw_ref, cb_ref, dtb_ref, alog_ref, dsk_ref, gn_ref,
                e64_ref, e128_ref, y_ref, tail_ref, state_ref):
    L = SSD_CHUNK
    gw = SSD_INNER // SSD_GROUPS

    @pl.when(pl.program_id(1) == 0)
    def _():
        tail_ref[...] = jnp.zeros_like(tail_ref)
        state_ref[...] = jnp.zeros_like(state_ref)

    cur = xbc_ref[...].astype(F32)
    prev8 = tail_ref[...]
    row8 = lax.broadcasted_iota(jnp.int32, prev8.shape, 0)
    acc = cur * cw_ref[SSD_CONV - 1:SSD_CONV, :] + cb_ref[...]
    for j in range(1, SSD_CONV):
        r = pltpu.roll(cur, j, 0)
        head = jnp.where(row8 < j, pltpu.roll(prev8, j, 0), r[0:8])
        r = jnp.concatenate([head, r[8:]], axis=0)
        acc = acc + r * cw_ref[SSD_CONV - 1 - j:SSD_CONV - j, :]
    tail_ref[...] = cur[L - 8:L]
    act = _silu(acc)
    xs = act[:, :SSD_INNER]
    bm = act[:, SSD_INNER:SSD_INNER + SSD_GROUPS * SSD_STATE].astype(BF16)
    cm = act[:, SSD_INNER + SSD_GROUPS * SSD_STATE:].astype(BF16)

    dt_in = dt_ref[...] + dtb_ref[...]
    dt = jnp.maximum(dt_in, 0.0) + jnp.log(1.0 + jnp.exp(-jnp.abs(dt_in)))
    adt = dt * (-jnp.exp(alog_ref[...]))
    ri = lax.broadcasted_iota(jnp.int32, (L, L), 0)
    ci = lax.broadcasted_iota(jnp.int32, (L, L), 1)
    causal = ci <= ri
    tril = causal.astype(F32)
    a_cs = jnp.dot(tril, adt, precision=HIGHEST, preferred_element_type=F32)
    a_cs_t = a_cs.T
    cs64 = jnp.dot(a_cs, e64_ref[...], precision=HIGHEST, preferred_element_type=F32)
    cs128 = jnp.dot(a_cs, e128_ref[...], precision=HIGHEST, preferred_element_type=F32)
    dt64 = jnp.dot(dt, e64_ref[...], precision=HIGHEST, preferred_element_type=F32)

    xd = xs * dt64
    xd_b = xd.astype(BF16)
    last = cs64[L - 1:L, :]
    xdw = (xd * jnp.exp(last - cs64)).astype(BF16)
    chunk_decay = jnp.exp(last)
    in_decay = jnp.exp(cs64)

    lane = lax.broadcasted_iota(jnp.int32, (L, LANES), 1)
    lo = lane < SSD_HEAD_DIM
    zero_b = jnp.zeros((L, LANES), BF16)
    y_parts = []
    new_states = []
    for g in range(SSD_GROUPS):
        bg = bm[:, g * SSD_STATE:(g + 1) * SSD_STATE]
        cg = cm[:, g * SSD_STATE:(g + 1) * SSD_STATE]
        cb = lax.dot_general(cg, bg, (((1,), (1,)), ((), ())), preferred_element_type=F32)
        prev = state_ref[:, g * gw:(g + 1) * gw]
        y_off = jnp.dot(cg, prev.astype(BF16), preferred_element_type=F32) * in_decay[:, g * gw:(g + 1) * gw]
        st = lax.dot_general(bg, xdw[:, g * gw:(g + 1) * gw], (((0,), (0,)), ((), ())),
                             preferred_element_type=F32)
        new_states.append(prev * chunk_decay[:, g * gw:(g + 1) * gw] + st)
        heads_per_group = SSD_HEADS // SSD_GROUPS
        for pair in range(heads_per_group // 2):
            h0 = g * heads_per_group + 2 * pair
            blk = xd_b[:, h0 * SSD_HEAD_DIM:(h0 + 2) * SSD_HEAD_DIM]
            y_pair = None
            for k in range(2):
                hh = h0 + k
                diff = cs128[:, hh * LANES:(hh + 1) * LANES] - a_cs_t[hh:hh + 1, :]
                decay = jnp.where(causal, jnp.exp(diff), 0.0)
                m = (cb * decay).astype(BF16)
                rhs = jnp.where(lo, blk, zero_b) if k == 0 else jnp.where(lo, zero_b, blk)
                part = jnp.dot(m, rhs, preferred_element_type=F32)
                y_pair = part if y_pair is None else y_pair + part
            c0 = (2 * pair) * SSD_HEAD_DIM
            y_parts.append(y_pair + y_off[:, c0:c0 + LANES])
    state_ref[...] = jnp.concatenate(new_states, axis=1)
    y = jnp.concatenate(y_parts, axis=1) + dsk_ref[...] * xs
    gated = y * _silu(z_ref[...].astype(F32))
    y_ref[...] = _rms(gated, gn_ref[...]).astype(BF16)


def _ssd(xbc, z, dt_raw, conv_w, conv_b, dt_bias, a_log, d_skip, ssd_norm, batch, seq):
    t = xbc.shape[0]
    nc = seq // SSD_CHUNK
    L = SSD_CHUNK
    row = lambda b, c: (b * nc + c, 0)
    const = lambda b, c: (0, 0)
    pad = lambda v: jnp.concatenate([v.reshape(1, -1), jnp.zeros((1, LANES - v.size), F32)], axis=1)
    head_of_lane64 = jnp.arange(SSD_INNER) // SSD_HEAD_DIM
    head_of_lane128 = jnp.arange(SSD_HEADS * LANES) // LANES
    rows = jnp.arange(LANES)[:, None]
    e64 = (rows == head_of_lane64[None, :]).astype(F32)
    e128 = (rows == head_of_lane128[None, :]).astype(F32)
    dsk = jnp.repeat(d_skip.astype(F32), SSD_HEAD_DIM).reshape(1, SSD_INNER)
    return pl.pallas_call(
        _ssd_kernel,
        grid=(batch, nc),
        in_specs=[pl.BlockSpec((L, SSD_XBC), row),
                  pl.BlockSpec((L, SSD_INNER), row),
                  pl.BlockSpec((L, LANES), row),
                  pl.BlockSpec((SSD_CONV, SSD_XBC), const),
                  pl.BlockSpec((1, SSD_XBC), const),
                  pl.BlockSpec((1, LANES), const),
                  pl.BlockSpec((1, LANES), const),
                  pl.BlockSpec((1, SSD_INNER), const),
                  pl.BlockSpec((1, SSD_INNER), const),
                  pl.BlockSpec(e64.shape, const),
                  pl.BlockSpec(e128.shape, const)],
        out_specs=pl.BlockSpec((L, SSD_INNER), row),
        out_shape=jax.ShapeDtypeStruct((t, SSD_INNER), BF16),
        scratch_shapes=[pltpu.VMEM((8, SSD_XBC), F32),
                        pltpu.VMEM((SSD_STATE, SSD_INNER), F32)],
        compiler_params=_params("arbitrary", "arbitrary"),
        name="ssd",
    )(xbc, z, dt_raw, conv_w, conv_b.reshape(1, -1), pad(dt_bias), pad(a_log), dsk,
      ssd_norm.reshape(1, -1), e64, e128)


def _attn_kernel(q_ref, k_ref, vt_ref, gn_ref, o_ref, s_ref):
    blk = ATT_BLOCK
    qi = pl.program_id(1)
    causal = (lax.broadcasted_iota(jnp.int32, (blk, blk), 0)
              <= lax.broadcasted_iota(jnp.int32, (blk, blk), 1))
    dims = (((1,), (1,)), ((), ()))
    qs = [q_ref[:, hd * HEAD_PAD:(hd + 1) * HEAD_PAD] for hd in range(MLA_HEADS)]

    def step(j, carry, masked):
        rows = pl.ds(pl.multiple_of(j * blk, blk), blk)
        m_new = []
        for hd in range(MLA_HEADS):
            kk = k_ref[rows, hd * HEAD_PAD:(hd + 1) * HEAD_PAD]
            s = lax.dot_general(kk, qs[hd], dims, preferred_element_type=F32)
            if masked:
                s = jnp.where(causal, s, -jnp.inf)
            s_ref[hd] = s
            m_new.append(jnp.maximum(carry[hd][0], jnp.max(s, axis=0, keepdims=True)))
        new = []
        for hd in range(MLA_HEADS):
            m_old, l_old, acc = carry[hd]
            p = jnp.exp(s_ref[hd] - m_new[hd])
            alpha = jnp.exp(m_old - m_new[hd])
            l_new = alpha * l_old + jnp.sum(p, axis=0, keepdims=True)
            vt = vt_ref[j, hd * MLA_V:(hd + 1) * MLA_V, :]
            acc = acc * alpha + jnp.dot(vt, p.astype(BF16), preferred_element_type=F32)
            new.append((m_new[hd], l_new, acc))
        return tuple(new)

    neg = jnp.full((1, blk), -jnp.inf, F32)
    init = tuple((neg, jnp.zeros((1, blk), F32), jnp.zeros((MLA_V, blk), F32))
                 for _ in range(MLA_HEADS))
    carry = lax.fori_loop(0, qi, functools.partial(step, masked=False), init)
    final = step(qi, carry, True)
    y_t = jnp.concatenate([acc / l for _, l, acc in final], axis=0)
    o_ref[...] = _rms(y_t.T, gn_ref[...]).astype(BF16)


def _attention(q, k, v_t, gn, batch, seq):
    t = q.shape[0]
    blk = ATT_BLOCK
    nq = seq // blk
    nqk = MLA_HEADS * HEAD_PAD
    return pl.pallas_call(
        _attn_kernel,
        grid=(batch, nq),
        in_specs=[pl.BlockSpec((blk, nqk), lambda b, i: (b * nq + i, 0)),
                  pl.BlockSpec((seq, nqk), lambda b, i: (b, 0)),
                  pl.BlockSpec((nq, MLA_INNER, blk), lambda b, i: (b, 0, 0)),
                  pl.BlockSpec((1, MLA_INNER), lambda b, i: (0, 0))],
        out_specs=pl.BlockSpec((blk, MLA_INNER), lambda b, i: (b * nq + i, 0)),
        out_shape=jax.ShapeDtypeStruct((t, MLA_INNER), BF16),
        scratch_shapes=[pltpu.VMEM((MLA_HEADS, blk, blk), F32)],
        compiler_params=_params("arbitrary", "arbitrary"),
        name="attention",
    )(q, k, v_t, gn)


def _outproj_kernel(ys_ref, ym_ref, x_ref, mod_ref, wtop_ref, wbot_ref, gpm_ref, gpf_ref,
                    wr_ref, br_ref, x1_ref, h2_ref, route_ref, counts_ref, run_ref):
    @pl.when(pl.program_id(0) == 0)
    def _():
        run_ref[...] = jnp.zeros_like(run_ref)

    y = (jnp.dot(ys_ref[...], wtop_ref[...], preferred_element_type=F32)
         + jnp.dot(ym_ref[...], wbot_ref[...], preferred_element_type=F32))
    g1 = mod_ref[0, 2:3, :]
    sh2 = mod_ref[0, 3:4, :]
    sc2 = mod_ref[0, 4:5, :]
    x1 = x_ref[...] + g1 * _rms(y, gpm_ref[...])
    x1_ref[...] = x1
    h2 = _rms(x1, gpf_ref[...]) * (1.0 + sc2) + sh2
    h2_ref[...] = h2
    logits = jnp.dot(h2, wr_ref[...], precision=HIGHEST, preferred_element_type=F32) + br_ref[...]
    lane = lax.broadcasted_iota(jnp.int32, logits.shape, 1).astype(F32)
    ninf = -jnp.inf
    big = 1e9
    is_g = lane < MOE_GROUPS
    gl = jnp.where(is_g, logits, ninf)
    gmax = jnp.max(gl, axis=-1, keepdims=True)
    gsum = jnp.sum(jnp.where(is_g, jnp.exp(logits - gmax), 0.0), axis=-1, keepdims=True)
    g_w = 1.0 / gsum
    g_idx = jnp.min(jnp.where(gl == gmax, lane, big), axis=-1, keepdims=True)
    first = MOE_GROUPS + MOE_EXPERTS_PER_GROUP * g_idx
    el = jnp.where(lane >= first, jnp.where(lane < first + MOE_EXPERTS_PER_GROUP, logits, ninf), ninf)
    m1 = jnp.max(el, axis=-1, keepdims=True)
    i1 = jnp.min(jnp.where(el == m1, lane, big), axis=-1, keepdims=True)
    el2 = jnp.where(lane == i1, ninf, el)
    m2 = jnp.max(el2, axis=-1, keepdims=True)
    i2 = jnp.min(jnp.where(el2 == m2, lane, big), axis=-1, keepdims=True)
    r = jnp.exp(m2 - m1)
    w1 = g_w / (1.0 + r)
    w2 = g_w * r / (1.0 + r)
    sel1 = lane == i1
    sel2 = lane == i2
    both = jnp.where(sel1, 1.0, jnp.where(sel2, 1.0, 0.0))
    tm = logits.shape[0]
    before = (lax.broadcasted_iota(jnp.int32, (tm, tm), 1)
              < lax.broadcasted_iota(jnp.int32, (tm, tm), 0)).astype(BF16)
    prefix = jnp.dot(before, both.astype(BF16), preferred_element_type=F32) + run_ref[...]
    rank1 = jnp.sum(jnp.where(sel1, prefix, 0.0), axis=-1, keepdims=True)
    rank2 = jnp.sum(jnp.where(sel2, prefix, 0.0), axis=-1, keepdims=True)
    run_ref[...] += jnp.sum(both, axis=0, keepdims=True)
    counts_ref[...] = run_ref[...]
    fields = (i1 - MOE_GROUPS, i2 - MOE_GROUPS, rank1, rank2, w1, w2)
    route = jnp.zeros_like(logits)
    for k, val in enumerate(fields):
        route = jnp.where(lane == k, val, route)
    route_ref[...] = route


def _outproj(y_ssd, y_mla, x2, mod3, w_top, w_bot, gpm, gpf, w_router, b_router, seq, tm):
    t, d = x2.shape
    per_b = seq // tm
    row = lambda i: (i, 0)
    const = lambda i: (0, 0)
    return pl.pallas_call(
        _outproj_kernel,
        grid=(t // tm,),
        in_specs=[pl.BlockSpec((tm, SSD_INNER), row),
                  pl.BlockSpec((tm, MLA_INNER), row),
                  pl.BlockSpec((tm, d), row),
                  pl.BlockSpec((1, N_MOD, d), lambda i: (i // per_b, 0, 0)),
                  pl.BlockSpec(w_top.shape, const),
                  pl.BlockSpec(w_bot.shape, const),
                  pl.BlockSpec((1, d), const),
                  pl.BlockSpec((1, d), const),
                  pl.BlockSpec(w_router.shape, const),
                  pl.BlockSpec((1, LANES), const)],
        out_specs=[pl.BlockSpec((tm, d), row),
                   pl.BlockSpec((tm, d), row),
                   pl.BlockSpec((tm, LANES), row),
                   pl.BlockSpec((1, LANES), const)],
        out_shape=[jax.ShapeDtypeStruct((t, d), F32),
                   jax.ShapeDtypeStruct((t, d), F32),
                   jax.ShapeDtypeStruct((t, LANES), F32),
                   jax.ShapeDtypeStruct((1, LANES), F32)],
        scratch_shapes=[pltpu.VMEM((1, LANES), F32)],
        compiler_params=_params("arbitrary"),
        name="outproj_router",
    )(y_ssd, y_mla, x2, mod3, w_top, w_bot, gpm, gpf, w_router, b_router)


def _route_plan(route, counts, tile):
    t = route.shape[0]
    n_tiles = 2 * t // tile + MOE_EXPERTS
    cnt = counts[0, MOE_GROUPS:MOE_GROUPS + MOE_EXPERTS].astype(jnp.int32)
    tiles_e = (cnt + tile - 1) // tile
    ends = jnp.cumsum(tiles_e)
    offs = (ends - tiles_e) * tile
    n_used = ends[-1]
    experts = jnp.arange(MOE_EXPERTS, dtype=jnp.int32)
    e12 = route[:, 0:2].astype(jnp.int32)
    first_row = jnp.sum(jnp.where(e12[:, :, None] == experts, offs, 0), axis=-1)
    pos = first_row + route[:, 2:4].astype(jnp.int32)
    j = jnp.minimum(jnp.arange(n_tiles, dtype=jnp.int32), n_used - 1)
    tile_expert = jnp.sum((ends[None, :] <= j[:, None]).astype(jnp.int32), axis=1)
    tile_expert = jnp.minimum(tile_expert, MOE_EXPERTS - 1)
    last_tiles = jnp.where(tiles_e > 0, ends - 1, -1)
    tail = n_used + jnp.arange(MOE_EXPERTS, dtype=jnp.int32)
    tail = jnp.where(tail < n_tiles, tail, -1)
    fill = jnp.concatenate([last_tiles, tail]).astype(jnp.int32)
    return pos.astype(jnp.int32), tile_expert, n_used.reshape(1).astype(jnp.int32), fill


def _dispatch_kernel(fill_ref, pos_ref, h_ref, xs_ref, zero_ref, sem, *, tile):
    rows = h_ref.shape[0]

    @pl.when(pl.program_id(0) == 0)
    def _():
        zero_ref[...] = jnp.zeros_like(zero_ref)
        for wait in (False, True):
            def body(k, carry, wait=wait):
                @pl.when(fill_ref[k] >= 0)
                def _():
                    start = pl.multiple_of(fill_ref[k] * tile, tile)
                    cp = pltpu.make_async_copy(zero_ref, xs_ref.at[pl.ds(start, tile), :], sem.at[1])
                    cp.wait() if wait else cp.start()
                return carry
            lax.fori_loop(0, fill_ref.shape[0], body, 0)

    def row_copy(r, k):
        dst = xs_ref.at[pl.ds(pos_ref[0, 0, 2 * r + k], 1), :]
        return pltpu.make_async_copy(h_ref.at[pl.ds(r, 1), :], dst, sem.at[0])

    for wait in (False, True):
        def body(r, carry, wait=wait):
            for k in range(2):
                cp = row_copy(r, k)
                cp.wait() if wait else cp.start()
            return carry
        lax.fori_loop(0, rows, body, 0)


def _dispatch(h2, pos, fill, tile, tm):
    t, d = h2.shape
    n_rows = (2 * t // tile + MOE_EXPERTS) * tile
    pos3 = pos.reshape(t // tm, 1, 2 * tm)
    return pl.pallas_call(
        functools.partial(_dispatch_kernel, tile=tile),
        grid_spec=pltpu.PrefetchScalarGridSpec(
            num_scalar_prefetch=1,
            grid=(t // tm,),
            in_specs=[pl.BlockSpec((1, 1, 2 * tm), lambda i, f: (i, 0, 0), memory_space=pltpu.SMEM),
                      pl.BlockSpec((tm, d), lambda i, f: (i, 0))],
            out_specs=pl.BlockSpec(memory_space=pl.ANY),
            scratch_shapes=[pltpu.VMEM((tile, d), F32), pltpu.SemaphoreType.DMA((2,))]),
        out_shape=jax.ShapeDtypeStruct((n_rows, d), F32),
        compiler_params=_params("arbitrary"),
        name="moe_dispatch",
    )(fill, pos3, h2)


def _experts_kernel(te_ref, nt_ref, xs_ref, wg_ref, wu_ref, wd_ref, o_ref):
    used = pl.program_id(0) < nt_ref[0]

    @pl.when(used)
    def _():
        x = xs_ref[...].astype(BF16)
        gate = jnp.dot(x, wg_ref[0].astype(BF16), preferred_element_type=F32)
        up = jnp.dot(x, wu_ref[0].astype(BF16), preferred_element_type=F32)
        hid = (_silu(gate) * up).astype(BF16)
        o_ref[...] = jnp.dot(hid, wd_ref[0].astype(BF16), preferred_element_type=F32)

    @pl.when(jnp.logical_not(used))
    def _():
        o_ref[...] = jnp.zeros_like(o_ref)


def _experts(xs, tile_expert, n_used, w_gate, w_up, w_down, tile):
    n_rows, d = xs.shape
    by_expert = lambda j, te, nt: (te[j], 0, 0)
    return pl.pallas_call(
        _experts_kernel,
        grid_spec=pltpu.PrefetchScalarGridSpec(
            num_scalar_prefetch=2,
            grid=(n_rows // tile,),
            in_specs=[pl.BlockSpec((tile, d), lambda j, te, nt: (jnp.minimum(j, nt[0] - 1), 0)),
                      pl.BlockSpec((1, d, MOE_FF), by_expert),
                      pl.BlockSpec((1, d, MOE_FF), by_expert),
                      pl.BlockSpec((1, MOE_FF, d), by_expert)],
            out_specs=pl.BlockSpec((tile, d), lambda j, te, nt: (j, 0))),
        out_shape=jax.ShapeDtypeStruct((n_rows, d), F32),
        compiler_params=_params("arbitrary"),
        name="moe_experts",
    )(tile_expert, n_used, xs, w_gate, w_up, w_down)


def _combine_kernel(pos_ref, route_ref, x1_ref, mod_ref, gpost_ref, ys_ref, o_ref, buf_ref, sem):
    rows = x1_ref.shape[0]

    def row_copy(r, k):
        src = ys_ref.at[pl.ds(pos_ref[0, 0, 2 * r + k], 1), :]
        return pltpu.make_async_copy(src, buf_ref.at[k, pl.ds(r, 1), :], sem.at[0])

    for wait in (False, True):
        def body(r, carry, wait=wait):
            for k in range(2):
                cp = row_copy(r, k)
                cp.wait() if wait else cp.start()
            return carry
        lax.fori_loop(0, rows, body, 0)

    route = route_ref[...]
    y = route[:, 4:5] * buf_ref[0] + route[:, 5:6] * buf_ref[1]
    g2 = mod_ref[0, 5:6, :]
    o_ref[...] = x1_ref[...] + g2 * _rms(y, gpost_ref[...])


def _combine(ys, pos, route, x1, mod3, gpost, seq, tm):
    t, d = x1.shape
    per_b = seq // tm
    pos3 = pos.reshape(t // tm, 1, 2 * tm)
    return pl.pallas_call(
        _combine_kernel,
        grid=(t // tm,),
        in_specs=[pl.BlockSpec((1, 1, 2 * tm), lambda i: (i, 0, 0), memory_space=pltpu.SMEM),
                  pl.BlockSpec((tm, LANES), lambda i: (i, 0)),
                  pl.BlockSpec((tm, d), lambda i: (i, 0)),
                  pl.BlockSpec((1, N_MOD, d), lambda i: (i // per_b, 0, 0)),
                  pl.BlockSpec((1, d), lambda i: (0, 0)),
                  pl.BlockSpec(memory_space=pl.ANY)],
        out_specs=pl.BlockSpec((tm, d), lambda i: (i, 0)),
        out_shape=jax.ShapeDtypeStruct((t, d), F32),
        scratch_shapes=[pltpu.VMEM((2, tm, d), F32), pltpu.SemaphoreType.DMA((1,))],
        compiler_params=_params("arbitrary"),
        name="moe_combine",
    )(pos3, route, x1, mod3, gpost, ys)


def kernel(x, c, positions, ada_w, ada_b, pre_norm_mix, post_norm_mix, pre_norm_ffn, post_norm_ffn, w_in, conv_w, conv_b, dt_bias, a_log, d_skip, ssd_norm, q_norm, w_uq, kv_norm, w_ukv, mla_out_norm, w_out, w_group_router, b_group_router, w_expert_router, b_expert_router, w_gate, w_up, w_down):
    batch, seq, d = x.shape
    t = batch * seq
    depth = ada_w.shape[0]
    tm = min(512, seq)
    half = MLA_ROPE // 2

    cos16, sin16 = _rope_tables(positions)
    ones = lambda n: jnp.ones((t, n), F32)
    zeros = lambda n: jnp.zeros((t, n), F32)
    cos_t = jnp.concatenate([ones(MLA_NOPE), cos16, cos16, ones(HEAD_PAD - MLA_QK)], axis=1)
    sin_t = jnp.concatenate([zeros(MLA_NOPE), sin16, sin16, zeros(HEAD_PAD - MLA_QK)], axis=1)

    x2 = x.reshape(t, d)
    for l in range(depth):
        mod3 = _modulation(c, ada_w[l], ada_b[l]).reshape(batch, N_MOD, d)
        w_in_r, wq2, wk, wv_t = _inproj_weights(w_in[l], w_uq[l], w_ukv[l])
        z, xbc, dt_raw, q, k, v_t = _inproj(
            x2, mod3, pre_norm_mix[l].reshape(1, d), w_in_r, q_norm[l].reshape(1, -1), wq2,
            kv_norm[l].reshape(1, -1), wk, wv_t, cos_t, sin_t, seq, tm)
        y_ssd = _ssd(xbc, z, dt_raw, conv_w[l], conv_b[l], dt_bias[l], a_log[l], d_skip[l],
                     ssd_norm[l], batch, seq)
        y_mla = _attention(q, k, v_t, mla_out_norm[l].reshape(1, -1), batch, seq)
        w_o = w_out[l].astype(BF16)
        pad_r = LANES - MOE_GROUPS - MOE_EXPERTS
        w_router = jnp.concatenate([w_group_router[l], w_expert_router[l], jnp.zeros((d, pad_r), F32)], axis=1)
        b_router = jnp.concatenate([b_group_router[l].reshape(-1), b_expert_router[l].reshape(-1),
                                    jnp.zeros((pad_r,), F32)]).reshape(1, LANES)
        x1, h2, route, counts = _outproj(y_ssd, y_mla, x2, mod3, w_o[:SSD_INNER], w_o[SSD_INNER:],
                                         post_norm_mix[l].reshape(1, d), pre_norm_ffn[l].reshape(1, d),
                                         w_router, b_router, seq, tm)
        tile = 256
        tm_rows = min(256, seq)
        pos, tile_expert, n_used, fill = _route_plan(route, counts, tile)
        xs = _dispatch(h2, pos, fill, tile, tm_rows)
        ys = _experts(xs, tile_expert, n_used, w_gate[l], w_up[l], w_down[l], tile)
        x2 = _combine(ys, pos, route, x1, mod3, post_norm_ffn[l].reshape(1, d), seq, tm_rows)
    return x2.reshape(batch, seq, d)
```

```python
import functools
import math

import jax
import jax.numpy as jnp
from jax import lax
from jax.experimental import pallas as pl
from jax.experimental.pallas import tpu as pltpu

F32 = jnp.float32
BF16 = jnp.bfloat16
HIGHEST = lax.Precision.HIGHEST

D_MODEL = 1024
SSD_HEADS = 8
SSD_HEAD_DIM = 64
SSD_INNER = SSD_HEADS * SSD_HEAD_DIM
SSD_GROUPS = 2
SSD_STATE = 128
SSD_CONV = 4
SSD_CHUNK = 128
SSD_XBC = SSD_INNER + 2 * SSD_GROUPS * SSD_STATE
MLA_HEADS = 8
MLA_NOPE = 64
MLA_ROPE = 32
MLA_QK = MLA_NOPE + MLA_ROPE
MLA_V = 64
MLA_Q_RANK = 256
MLA_KV_RANK = 128
MLA_INNER = MLA_HEADS * MLA_V
ROPE_THETA = 10000.0
MOE_GROUPS = 4
MOE_EXPERTS_PER_GROUP = 8
MOE_EXPERTS = MOE_GROUPS * MOE_EXPERTS_PER_GROUP
MOE_FF = 256
N_MOD = 6
EPS = 1e-6

LANES = 128
HEAD_PAD = 128
ATT_BLOCK = 256
MOE_TOKENS = 512
MOE_CHUNK = 16
MOE_TILE = 256
MOE_SLOTS = 2 * MOE_TOKENS + MOE_EXPERTS * MOE_CHUNK
VMEM_LIMIT = 48 * 1024 * 1024

_C_Z = 0
_C_XBC = _C_Z + SSD_INNER
_C_CQ = _C_XBC + SSD_XBC
_C_CKV = _C_CQ + MLA_Q_RANK
_C_KR = _C_CKV + MLA_KV_RANK
_C_KRS = _C_KR + LANES
_C_DT = _C_KRS + LANES
_C_END = _C_DT + LANES


def _silu(v):
    return v * (1.0 / (1.0 + jnp.exp(-v)))


def _rms(v, gain):
    return v * lax.rsqrt(jnp.mean(v * v, axis=-1, keepdims=True) + EPS) * gain


def _params(*sem, flags=None):
    return pltpu.CompilerParams(dimension_semantics=sem, vmem_limit_bytes=VMEM_LIMIT, flags=flags)


def _mod_kernel(c_ref, w_ref, b_ref, o_ref):
    o_ref[...] = jnp.dot(_silu(c_ref[...]), w_ref[...], precision=HIGHEST,
                         preferred_element_type=F32) + b_ref[...]


def _modulation(c, ada_w, ada_b):
    b, d = c.shape
    n = ada_w.shape[1]
    return pl.pallas_call(
        _mod_kernel,
        grid=(n // d,),
        in_specs=[pl.BlockSpec((b, d), lambda j: (0, 0)),
                  pl.BlockSpec((d, d), lambda j: (0, j)),
                  pl.BlockSpec((1, d), lambda j: (0, j))],
        out_specs=pl.BlockSpec((b, d), lambda j: (0, j)),
        out_shape=jax.ShapeDtypeStruct((b, n), F32),
        compiler_params=_params("arbitrary"),
        name="modulation",
    )(c, ada_w, ada_b.reshape(1, n))


def _rope_kernel(pos_ref, inv_ref, cos_ref, sin_ref):
    ang = pos_ref[...].astype(F32) * inv_ref[...]
    cos_ref[...] = jnp.cos(ang)
    sin_ref[...] = jnp.sin(ang)


def _rope_tables(positions):
    t = positions.size
    half = MLA_ROPE // 2
    per_row = LANES // half
    inv = 1.0 / (ROPE_THETA ** (jnp.arange(0, MLA_ROPE, 2, dtype=F32) / MLA_ROPE))
    pos_rep = jnp.repeat(positions.reshape(t), half).reshape(t // per_row, LANES)
    inv_t = jnp.tile(inv, per_row).reshape(1, LANES)
    rows = t // per_row
    blk = min(rows, 512)
    cos, sin = pl.pallas_call(
        _rope_kernel,
        grid=(rows // blk,),
        in_specs=[pl.BlockSpec((blk, LANES), lambda i: (i, 0)),
                  pl.BlockSpec((1, LANES), lambda i: (0, 0))],
        out_specs=[pl.BlockSpec((blk, LANES), lambda i: (i, 0))] * 2,
        out_shape=[jax.ShapeDtypeStruct((rows, LANES), F32)] * 2,
        compiler_params=_params("arbitrary"),
        name="rope_tables",
    )(pos_rep, inv_t)
    return cos.reshape(t, half), sin.reshape(t, half)


def _inproj_kernel(x_ref, mod_ref, gpre_ref, win_ref, qn_ref, wq_ref, kvn_ref, wk_ref, wvt_ref,
                   cos_ref, sin_ref, z_ref, xbc_ref, dt_ref, q_ref, k_ref, vt_ref):
    x = x_ref[...]
    sh = mod_ref[0, 0:1, :]
    sc = mod_ref[0, 1:2, :]
    h = (_rms(x, gpre_ref[...]) * (1.0 + sc) + sh).astype(BF16)
    u = jnp.dot(h, win_ref[...], preferred_element_type=F32)
    z_ref[...] = u[:, _C_Z:_C_XBC].astype(BF16)
    xbc_ref[...] = u[:, _C_XBC:_C_CQ].astype(BF16)
    dt_ref[...] = u[:, _C_DT:_C_END]
    cos_t = cos_ref[...]
    sin_t = sin_ref[...]
    cq = _rms(u[:, _C_CQ:_C_CKV], qn_ref[...]).astype(BF16)
    q2 = jnp.dot(cq, wq_ref[...], preferred_element_type=F32)
    scale = MLA_QK ** -0.5
    nq = MLA_HEADS * HEAD_PAD
    for hd in range(MLA_HEADS):
        a = q2[:, hd * HEAD_PAD:(hd + 1) * HEAD_PAD]
        b = q2[:, nq + hd * HEAD_PAD:nq + (hd + 1) * HEAD_PAD]
        q_ref[:, hd * HEAD_PAD:(hd + 1) * HEAD_PAD] = ((a * cos_t + b * sin_t) * scale).astype(BF16)
    ckv = _rms(u[:, _C_CKV:_C_KR], kvn_ref[...]).astype(BF16)
    kn = jnp.dot(ckv, wk_ref[...], preferred_element_type=F32)
    k_pe = u[:, _C_KR:_C_KRS] * cos_t + u[:, _C_KRS:_C_DT] * sin_t
    for hd in range(MLA_HEADS):
        k_ref[:, hd * HEAD_PAD:(hd + 1) * HEAD_PAD] = (
            kn[:, hd * HEAD_PAD:(hd + 1) * HEAD_PAD] + k_pe).astype(BF16)
    v_t = lax.dot_general(wvt_ref[...], ckv, (((1,), (1,)), ((), ())), preferred_element_type=F32)
    for s in range(vt_ref.shape[0]):
        vt_ref[s] = v_t[:, s * ATT_BLOCK:(s + 1) * ATT_BLOCK].astype(BF16)


def _inproj_weights(w_in, w_uq, w_ukv):
    d = w_in.shape[0]
    half = MLA_ROPE // 2
    o_z, o_xbc, o_dt = 0, SSD_INNER, SSD_INNER + SSD_XBC
    o_cq = o_dt + SSD_HEADS
    o_ckv = o_cq + MLA_Q_RANK
    o_kr = o_ckv + MLA_KV_RANK
    zeros = lambda n: jnp.zeros((d, n), F32)
    kr = w_in[:, o_kr:o_kr + MLA_ROPE]
    kr_blk = jnp.concatenate([zeros(MLA_NOPE), kr, zeros(HEAD_PAD - MLA_QK)], axis=1)
    krs_blk = jnp.concatenate([zeros(MLA_NOPE), -kr[:, half:], kr[:, :half], zeros(HEAD_PAD - MLA_QK)], axis=1)
    dt_blk = jnp.concatenate([w_in[:, o_dt:o_dt + SSD_HEADS], zeros(LANES - SSD_HEADS)], axis=1)
    w_in_r = jnp.concatenate([w_in[:, o_z:o_dt], w_in[:, o_cq:o_kr], kr_blk, krs_blk, dt_blk], axis=1)
    r = w_uq.shape[0]
    zq = jnp.zeros((r, MLA_HEADS, HEAD_PAD - MLA_QK), F32)
    zn = jnp.zeros((r, MLA_HEADS, MLA_NOPE), F32)
    wq_plain = jnp.concatenate([w_uq, zq], axis=2).reshape(r, MLA_HEADS * HEAD_PAD)
    wq_rot = jnp.concatenate([zn, -w_uq[:, :, MLA_NOPE + half:], w_uq[:, :, MLA_NOPE:MLA_NOPE + half], zq],
                             axis=2).reshape(r, MLA_HEADS * HEAD_PAD)
    wq2 = jnp.concatenate([wq_plain, wq_rot], axis=1)
    rk = w_ukv.shape[0]
    zk = jnp.zeros((rk, MLA_HEADS, HEAD_PAD - MLA_NOPE), F32)
    wk = jnp.concatenate([w_ukv[:, :, :MLA_NOPE], zk], axis=2).reshape(rk, MLA_HEADS * HEAD_PAD)
    wv_t = w_ukv[:, :, MLA_NOPE:].reshape(rk, MLA_INNER).T
    return w_in_r.astype(BF16), wq2.astype(BF16), wk.astype(BF16), wv_t.astype(BF16)


def _inproj(x2, mod3, gpre, w_in_r, q_norm, wq2, kv_norm, wk, wv_t, cos_t, sin_t, seq, tm):
    t, d = x2.shape
    per_b = seq // tm
    row = lambda i: (i, 0)
    const = lambda i: (0, 0)
    nqk = MLA_HEADS * HEAD_PAD
    slabs = tm // ATT_BLOCK
    return pl.pallas_call(
        _inproj_kernel,
        grid=(t // tm,),
        in_specs=[pl.BlockSpec((tm, d), row),
                  pl.BlockSpec((1, N_MOD, d), lambda i: (i // per_b, 0, 0)),
                  pl.BlockSpec((1, d), const),
                  pl.BlockSpec(w_in_r.shape, const),
                  pl.BlockSpec((1, MLA_Q_RANK), const),
                  pl.BlockSpec(wq2.shape, const),
                  pl.BlockSpec((1, MLA_KV_RANK), const),
                  pl.BlockSpec(wk.shape, const),
                  pl.BlockSpec(wv_t.shape, const),
                  pl.BlockSpec((tm, LANES), row),
                  pl.BlockSpec((tm, LANES), row)],
        out_specs=[pl.BlockSpec((tm, SSD_INNER), row),
                   pl.BlockSpec((tm, SSD_XBC), row),
                   pl.BlockSpec((tm, LANES), row),
                   pl.BlockSpec((tm, nqk), row),
                   pl.BlockSpec((tm, nqk), row),
                   pl.BlockSpec((slabs, MLA_INNER, ATT_BLOCK), lambda i: (i, 0, 0))],
        out_shape=[jax.ShapeDtypeStruct((t, SSD_INNER), BF16),
                   jax.ShapeDtypeStruct((t, SSD_XBC), BF16),
                   jax.ShapeDtypeStruct((t, LANES), F32),
                   jax.ShapeDtypeStruct((t, nqk), BF16),
                   jax.ShapeDtypeStruct((t, nqk), BF16),
                   jax.ShapeDtypeStruct((t // ATT_BLOCK, MLA_INNER, ATT_BLOCK), BF16)],
        compiler_params=_params("arbitrary"),
        name="inproj",
    )(x2, mod3, gpre, w_in_r, q_norm, wq2, kv_norm, wk, wv_t, cos_t, sin_t)


def _ssd_kernel(xbc_ref, z_ref, dt_ref, cw_ref, cb_ref, dtb_ref, alog_ref, dsk_ref, gn_ref,
                e64_ref, e128_ref, y_ref, tail_ref, state_ref):
    L = SSD_CHUNK
    gw = SSD_INNER // SSD_GROUPS

    @pl.when(pl.program_id(1) == 0)
    def _():
        tail_ref[...] = jnp.zeros_like(tail_ref)
        state_ref[...] = jnp.zeros_like(state_ref)

    cur = xbc_ref[...].astype(F32)
    prev8 = tail_ref[...]
    row8 = lax.broadcasted_iota(jnp.int32, prev8.shape, 0)
    acc = cur * cw_ref[SSD_CONV - 1:SSD_CONV, :] + cb_ref[...]
    for j in range(1, SSD_CONV):
        r = pltpu.roll(cur, j, 0)
        head = jnp.where(row8 < j, pltpu.roll(prev8, j, 0), r[0:8])
        r = jnp.concatenate([head, r[8:]], axis=0)
        acc = acc + r * cw_ref[SSD_CONV - 1 - j:SSD_CONV - j, :]
    tail_ref[...] = cur[L - 8:L]
    act = _silu(acc)
    xs = act[:, :SSD_INNER]
    bm = act[:, SSD_INNER:SSD_INNER + SSD_GROUPS * SSD_STATE].astype(BF16)
    cm = act[:, SSD_INNER + SSD_GROUPS * SSD_STATE:].astype(BF16)

    dt_in = dt_ref[...] + dtb_ref[...]
    dt = jnp.maximum(dt_in, 0.0) + jnp.log(1.0 + jnp.exp(-jnp.abs(dt_in)))
    adt = dt * (-jnp.exp(alog_ref[...]))
    ri = lax.broadcasted_iota(jnp.int32, (L, L), 0)
    ci = lax.broadcasted_iota(jnp.int32, (L, L), 1)
    causal = ci <= ri
    tril = causal.astype(F32)
    a_cs = jnp.dot(tril, adt, precision=HIGHEST, preferred_element_type=F32)
    a_cs_t = a_cs.T
    cs64 = jnp.dot(a_cs, e64_ref[...], precision=HIGHEST, preferred_element_type=F32)
    cs128 = jnp.dot(a_cs, e128_ref[...], precision=HIGHEST, preferred_element_type=F32)
    dt64 = jnp.dot(dt, e64_ref[...], precision=HIGHEST, preferred_element_type=F32)

    xd = xs * dt64
    xd_b = xd.astype(BF16)
    last = cs64[L - 1:L, :]
    xdw = (xd * jnp.exp(last - cs64)).astype(BF16)
    chunk_decay = jnp.exp(last)
    in_decay = jnp.exp(cs64)

    lane = lax.broadcasted_iota(jnp.int32, (L, LANES), 1)
    lo = lane < SSD_HEAD_DIM
    zero_b = jnp.zeros((L, LANES), BF16)
    y_parts = []
    new_states = []
    for g in range(SSD_GROUPS):
        bg = bm[:, g * SSD_STATE:(g + 1) * SSD_STATE]
        cg = cm[:, g * SSD_STATE:(g + 1) * SSD_STATE]
        cb = lax.dot_general(cg, bg, (((1,), (1,)), ((), ())), preferred_element_type=F32)
        prev = state_ref[:, g * gw:(g + 1) * gw]
        y_off = jnp.dot(cg, prev.astype(BF16), preferred_element_type=F32) * in_decay[:, g * gw:(g + 1) * gw]
        st = lax.dot_general(bg, xdw[:, g * gw:(g + 1) * gw], (((0,), (0,)), ((), ())),
                             preferred_element_type=F32)
        new_states.append(prev * chunk_decay[:, g * gw:(g + 1) * gw] + st)
        heads_per_group = SSD_HEADS // SSD_GROUPS
        for pair in range(heads_per_group // 2):
            h0 = g * heads_per_group + 2 * pair
            blk = xd_b[:, h0 * SSD_HEAD_DIM:(h0 + 2) * SSD_HEAD_DIM]
            y_pair = None
            for k in range(2):
                hh = h0 + k
                diff = cs128[:, hh * LANES:(hh + 1) * LANES] - a_cs_t[hh:hh + 1, :]
                decay = jnp.where(causal, jnp.exp(diff), 0.0)
                m = (cb * decay).astype(BF16)
                rhs = jnp.where(lo, blk, zero_b) if k == 0 else jnp.where(lo, zero_b, blk)
                part = jnp.dot(m, rhs, preferred_element_type=F32)
                y_pair = part if y_pair is None else y_pair + part
            c0 = (2 * pair) * SSD_HEAD_DIM
            y_parts.append(y_pair + y_off[:, c0:c0 + LANES])
    state_ref[...] = jnp.concatenate(new_states, axis=1)
    y = jnp.concatenate(y_parts, axis=1) + dsk_ref[...] * xs
    gated = y * _silu(z_ref[...].astype(F32))
    y_ref[...] = _rms(gated, gn_ref[...]).astype(BF16)


def _ssd(xbc, z, dt_raw, conv_w, conv_b, dt_bias, a_log, d_skip, ssd_norm, batch, seq):
    t = xbc.shape[0]
    nc = seq // SSD_CHUNK
    L = SSD_CHUNK
    row = lambda b, c: (b * nc + c, 0)
    const = lambda b, c: (0, 0)
    pad = lambda v: jnp.concatenate([v.reshape(1, -1), jnp.zeros((1, LANES - v.size), F32)], axis=1)
    head_of_lane64 = jnp.arange(SSD_INNER) // SSD_HEAD_DIM
    head_of_lane128 = jnp.arange(SSD_HEADS * LANES) // LANES
    rows = jnp.arange(LANES)[:, None]
    e64 = (rows == head_of_lane64[None, :]).astype(F32)
    e128 = (rows == head_of_lane128[None, :]).astype(F32)
    dsk = jnp.repeat(d_skip.astype(F32), SSD_HEAD_DIM).reshape(1, SSD_INNER)
    return pl.pallas_call(
        _ssd_kernel,
        grid=(batch, nc),
        in_specs=[pl.BlockSpec((L, SSD_XBC), row),
                  pl.BlockSpec((L, SSD_INNER), row),
                  pl.BlockSpec((L, LANES), row),
                  pl.BlockSpec((SSD_CONV, SSD_XBC), const),
                  pl.BlockSpec((1, SSD_XBC), const),
                  pl.BlockSpec((1, LANES), const),
                  pl.BlockSpec((1, LANES), const),
                  pl.BlockSpec((1, SSD_INNER), const),
                  pl.BlockSpec((1, SSD_INNER), const),
                  pl.BlockSpec(e64.shape, const),
                  pl.BlockSpec(e128.shape, const)],
        out_specs=pl.BlockSpec((L, SSD_INNER), row),
        out_shape=jax.ShapeDtypeStruct((t, SSD_INNER), BF16),
        scratch_shapes=[pltpu.VMEM((8, SSD_XBC), F32),
                        pltpu.VMEM((SSD_STATE, SSD_INNER), F32)],
        compiler_params=_params("arbitrary", "arbitrary"),
        name="ssd",
    )(xbc, z, dt_raw, conv_w, conv_b.reshape(1, -1), pad(dt_bias), pad(a_log), dsk,
      ssd_norm.reshape(1, -1), e64, e128)


def _attn_kernel(q_ref, k_ref, vt_ref, gn_ref, o_ref, s_ref):
    blk = ATT_BLOCK
    qi = pl.program_id(1)
    causal = (lax.broadcasted_iota(jnp.int32, (blk, blk), 0)
              <= lax.broadcasted_iota(jnp.int32, (blk, blk), 1))
    dims = (((1,), (1,)), ((), ()))
    qs = [q_ref[:, hd * HEAD_PAD:(hd + 1) * HEAD_PAD] for hd in range(MLA_HEADS)]

    def step(j, carry, masked):
        rows = pl.ds(pl.multiple_of(j * blk, blk), blk)
        m_new = []
        for hd in range(MLA_HEADS):
            kk = k_ref[rows, hd * HEAD_PAD:(hd + 1) * HEAD_PAD]
            s = lax.dot_general(kk, qs[hd], dims, preferred_element_type=F32)
            if masked:
                s = jnp.where(causal, s, -jnp.inf)
            s_ref[hd] = s
            m_new.append(jnp.maximum(carry[hd][0], jnp.max(s, axis=0, keepdims=True)))
        new = []
        for hd in range(MLA_HEADS):
            m_old, l_old, acc = carry[hd]
            p = jnp.exp(s_ref[hd] - m_new[hd])
            alpha = jnp.exp(m_old - m_new[hd])
            l_new = alpha * l_old + jnp.sum(p, axis=0, keepdims=True)
            vt = vt_ref[j, hd * MLA_V:(hd + 1) * MLA_V, :]
            acc = acc * alpha + jnp.dot(vt, p.astype(BF16), preferred_element_type=F32)
            new.append((m_new[hd], l_new, acc))
        return tuple(new)

    neg = jnp.full((1, blk), -jnp.inf, F32)
    init = tuple((neg, jnp.zeros((1, blk), F32), jnp.zeros((MLA_V, blk), F32))
                 for _ in range(MLA_HEADS))
    carry = lax.fori_loop(0, qi, functools.partial(step, masked=False), init)
    final = step(qi, carry, True)
    y_t = jnp.concatenate([acc / l for _, l, acc in final], axis=0)
    o_ref[...] = _rms(y_t.T, gn_ref[...]).astype(BF16)


def _attention(q, k, v_t, gn, batch, seq):
    t = q.shape[0]
    blk = ATT_BLOCK
    nq = seq // blk
    nqk = MLA_HEADS * HEAD_PAD
    return pl.pallas_call(
        _attn_kernel,
        grid=(batch, nq),
        in_specs=[pl.BlockSpec((blk, nqk), lambda b, i: (b * nq + i, 0)),
                  pl.BlockSpec((seq, nqk), lambda b, i: (b, 0)),
                  pl.BlockSpec((nq, MLA_INNER, blk), lambda b, i: (b, 0, 0)),
                  pl.BlockSpec((1, MLA_INNER), lambda b, i: (0, 0))],
        out_specs=pl.BlockSpec((blk, MLA_INNER), lambda b, i: (b * nq + i, 0)),
        out_shape=jax.ShapeDtypeStruct((t, MLA_INNER), BF16),
        scratch_shapes=[pltpu.VMEM((MLA_HEADS, blk, blk), F32)],
        compiler_params=_params("arbitrary", "arbitrary"),
        name="attention",
    )(q, k, v_t, gn)


def _outproj_kernel(ys_ref, ym_ref, x_ref, mod_ref, wtop_ref, wbot_ref, gpm_ref, gpf_ref,
                    wrh_ref, wrl_ref, br_ref, x1_ref, xs_ref, col_ref, wslot_ref, cnt_ref, ce_ref):
    y = (jnp.dot(ys_ref[...], wtop_ref[...], preferred_element_type=F32)
         + jnp.dot(ym_ref[...], wbot_ref[...], preferred_element_type=F32))
    g1 = mod_ref[0, 2:3, :]
    sh2 = mod_ref[0, 3:4, :]
    sc2 = mod_ref[0, 4:5, :]
    x1 = x_ref[...] + g1 * _rms(y, gpm_ref[...])
    x1_ref[...] = x1
    h2 = _rms(x1, gpf_ref[...]) * (1.0 + sc2) + sh2
    tm = h2.shape[0]
    h_hi = h2.astype(BF16)
    h_lo = (h2 - h_hi.astype(F32)).astype(BF16)
    logits = (jnp.dot(h_hi, wrh_ref[...], preferred_element_type=F32)
              + jnp.dot(h_lo, wrh_ref[...], preferred_element_type=F32)
              + jnp.dot(h_hi, wrl_ref[...], preferred_element_type=F32)) + br_ref[...]
    lt = logits.T
    row = lax.broadcasted_iota(jnp.int32, lt.shape, 0).astype(F32)
    ninf = -jnp.inf
    big = 1e9
    is_g = row < MOE_GROUPS
    gl = jnp.where(is_g, lt, ninf)
    gmax = jnp.max(gl, axis=0, keepdims=True)
    gsum = jnp.sum(jnp.where(is_g, jnp.exp(lt - gmax), 0.0), axis=0, keepdims=True)
    g_w = 1.0 / gsum
    g_idx = jnp.min(jnp.where(gl == gmax, row, big), axis=0, keepdims=True)
    first = MOE_GROUPS + MOE_EXPERTS_PER_GROUP * g_idx
    el = jnp.where(row >= first, jnp.where(row < first + MOE_EXPERTS_PER_GROUP, lt, ninf), ninf)
    m1 = jnp.max(el, axis=0, keepdims=True)
    i1 = jnp.min(jnp.where(el == m1, row, big), axis=0, keepdims=True)
    el2 = jnp.where(row == i1, ninf, el)
    m2 = jnp.max(el2, axis=0, keepdims=True)
    i2 = jnp.min(jnp.where(el2 == m2, row, big), axis=0, keepdims=True)
    r = jnp.exp(m2 - m1)
    w1 = g_w / (1.0 + r)
    w2 = g_w * r / (1.0 + r)
    sel1 = row == i1
    sel2 = row == i2
    both = jnp.where(sel1, 1.0, jnp.where(sel2, 1.0, 0.0))
    cnt16 = jnp.floor((jnp.sum(both, axis=1, keepdims=True) + (MOE_CHUNK - 1)) * (1.0 / MOE_CHUNK))
    er = lax.broadcasted_iota(jnp.int32, (LANES, LANES), 0)
    ec = lax.broadcasted_iota(jnp.int32, (LANES, LANES), 1)
    cnt16_b = jnp.broadcast_to(cnt16, (LANES, LANES))
    off16 = jnp.dot((ec < er).astype(BF16), cnt16_b.astype(BF16), preferred_element_type=F32)[:, 0:1]
    earlier = (lax.broadcasted_iota(jnp.int32, (tm, tm), 0)
               < lax.broadcasted_iota(jnp.int32, (tm, tm), 1)).astype(BF16)
    base = off16 * MOE_CHUNK + jnp.dot(both.astype(BF16), earlier, preferred_element_type=F32)
    lpos1 = jnp.sum(jnp.where(sel1, base, 0.0), axis=0, keepdims=True)
    lpos2 = jnp.sum(jnp.where(sel2, base, 0.0), axis=0, keepdims=True)
    srow = lax.broadcasted_iota(jnp.int32, (MOE_SLOTS, tm), 0)
    hit1 = srow == lpos1.astype(jnp.int32)
    hit2 = srow == lpos2.astype(jnp.int32)
    perm = jnp.where(hit1, 1.0, jnp.where(hit2, 1.0, 0.0)).astype(BF16)
    xs_ref[...] = jnp.dot(perm, h_hi, preferred_element_type=F32).astype(BF16)
    wslot = jnp.sum(jnp.where(hit1, w1, jnp.where(hit2, w2, 0.0)), axis=1, keepdims=True)
    wslot_ref[...] = jnp.broadcast_to(wslot, wslot_ref.shape)
    col_ref[...] = jnp.where(row == 0, lpos1, jnp.where(row == 1, lpos2, 0.0)).T
    cnt_ref[...] = cnt16_b
    chunk = lax.broadcasted_iota(jnp.int32, (LANES, LANES), 1).astype(F32)
    ce = jnp.sum(jnp.where(off16 + cnt16 <= chunk, 1.0, 0.0), axis=0, keepdims=True) - MOE_GROUPS
    ce_ref[...] = jnp.broadcast_to(ce, ce_ref.shape)


def _outproj(y_ssd, y_mla, x2, mod3, w_top, w_bot, gpm, gpf, wr_hi, wr_lo, b_router, seq, tm):
    t, d = x2.shape
    per_b = seq // tm
    n_tok_tiles = t // tm
    row = lambda i: (i, 0)
    const = lambda i: (0, 0)
    return pl.pallas_call(
        _outproj_kernel,
        grid=(n_tok_tiles,),
        in_specs=[pl.BlockSpec((tm, SSD_INNER), row),
                  pl.BlockSpec((tm, MLA_INNER), row),
                  pl.BlockSpec((tm, d), row),
                  pl.BlockSpec((1, N_MOD, d), lambda i: (i // per_b, 0, 0)),
                  pl.BlockSpec(w_top.shape, const),
                  pl.BlockSpec(w_bot.shape, const),
                  pl.BlockSpec((1, d), const),
                  pl.BlockSpec((1, d), const),
                  pl.BlockSpec(wr_hi.shape, const),
                  pl.BlockSpec(wr_lo.shape, const),
                  pl.BlockSpec((1, LANES), const)],
        out_specs=[pl.BlockSpec((tm, d), row),
                   pl.BlockSpec((MOE_SLOTS, d), row),
                   pl.BlockSpec((tm, LANES), row),
                   pl.BlockSpec((MOE_SLOTS, LANES), row),
                   pl.BlockSpec((LANES, LANES), row),
                   pl.BlockSpec((8, LANES), row)],
        out_shape=[jax.ShapeDtypeStruct((t, d), F32),
                   jax.ShapeDtypeStruct((n_tok_tiles * MOE_SLOTS, d), BF16),
                   jax.ShapeDtypeStruct((t, LANES), F32),
                   jax.ShapeDtypeStruct((n_tok_tiles * MOE_SLOTS, LANES), F32),
                   jax.ShapeDtypeStruct((n_tok_tiles * LANES, LANES), F32),
                   jax.ShapeDtypeStruct((n_tok_tiles * 8, LANES), F32)],
        compiler_params=_params("arbitrary"),
        name="outproj_router",
    )(y_ssd, y_mla, x2, mod3, w_top, w_bot, gpm, gpf, wr_hi, wr_lo, b_router)


def _moe_rows(n_tok_tiles):
    rows = n_tok_tiles * MOE_SLOTS + MOE_EXPERTS * (MOE_TILE - MOE_CHUNK)
    return (rows + MOE_TILE - 1) // MOE_TILE * MOE_TILE


def _moe_plan(cnt_tiles, chunk_expert, n_tok_tiles, n_rows):
    chunks_per_tile = MOE_SLOTS // MOE_CHUNK
    chunks_per_rows = MOE_TILE // MOE_CHUNK
    cnt = cnt_tiles.reshape(n_tok_tiles, LANES, LANES)[:, MOE_GROUPS:MOE_GROUPS + MOE_EXPERTS, 0]
    cnt = cnt.astype(jnp.int32)
    ce = chunk_expert.reshape(n_tok_tiles, 8, LANES)[:, 0, :chunks_per_tile].astype(jnp.int32)
    used = jnp.sum(cnt, axis=1)
    total = jnp.sum(cnt, axis=0)
    padded = (total + chunks_per_rows - 1) // chunks_per_rows * chunks_per_rows
    e_end = jnp.cumsum(padded)
    e_start = e_end - padded
    run_global = e_start[None, :] + jnp.cumsum(cnt, axis=0) - cnt
    run_local = jnp.cumsum(cnt, axis=1) - cnt
    experts = jnp.arange(MOE_EXPERTS, dtype=jnp.int32)
    shift = jnp.sum(jnp.where(ce[:, :, None] == experts, (run_global - run_local)[:, None, :], 0), axis=-1)
    dst = shift + jnp.arange(chunks_per_tile, dtype=jnp.int32)[None, :]
    n_row_tiles = n_rows // MOE_TILE
    n_used = e_end[-1] // chunks_per_rows
    j = jnp.minimum(jnp.arange(n_row_tiles, dtype=jnp.int32), jnp.maximum(n_used - 1, 0))
    tile_expert = jnp.sum((e_end[None, :] // chunks_per_rows <= j[:, None]).astype(jnp.int32), axis=1)
    tile_expert = jnp.minimum(tile_expert, MOE_EXPERTS - 1)
    i32 = lambda v: v.astype(jnp.int32)
    return (i32(dst.reshape(-1)), i32(used), i32(e_start + total), i32(padded - total),
            i32(n_used.reshape(1)), i32(tile_expert))


def _chunk_rows(chunk):
    return pl.ds(pl.multiple_of(chunk * MOE_CHUNK, MOE_CHUNK), MOE_CHUNK)


def _run_copies(*groups):
    for wait in (False, True):
        for lo, hi, make_copy in groups:
            def body(k, carry, wait=wait, make_copy=make_copy):
                cp = make_copy(k)
                cp.wait() if wait else cp.start()
                return carry
            lax.fori_loop(lo, hi, body, 0)


def _scatter_chunks_kernel(dst_ref, used_ref, pad_start_ref, pad_n_ref, nused_ref, src_ref, o_ref,
                           zero_ref, sem):
    i = pl.program_id(0)
    base = i * (MOE_SLOTS // MOE_CHUNK)

    @pl.when(i == 0)
    def _():
        zero_ref[...] = jnp.zeros_like(zero_ref)
        pads = [(0, pad_n_ref[e], lambda c, e=e: pltpu.make_async_copy(
            zero_ref.at[pl.ds(0, MOE_CHUNK), :], o_ref.at[_chunk_rows(pad_start_ref[e] + c), :], sem.at[1]))
            for e in range(MOE_EXPERTS)]
        tail = (nused_ref[0], o_ref.shape[0] // MOE_TILE, lambda j: pltpu.make_async_copy(
            zero_ref, o_ref.at[pl.ds(pl.multiple_of(j * MOE_TILE, MOE_TILE), MOE_TILE), :], sem.at[1]))
        _run_copies(tail, *pads)

    _run_copies((0, used_ref[i], lambda c: pltpu.make_async_copy(
        src_ref.at[_chunk_rows(base + c), :], o_ref.at[_chunk_rows(dst_ref[base + c]), :], sem.at[0])))


def _scatter_chunks(xs_local, dst, used, pad_start, pad_n, n_used, n_rows):
    d = xs_local.shape[1]
    return pl.pallas_call(
        _scatter_chunks_kernel,
        grid_spec=pltpu.PrefetchScalarGridSpec(
            num_scalar_prefetch=5,
            grid=(used.shape[0],),
            in_specs=[pl.BlockSpec(memory_space=pl.ANY)],
            out_specs=pl.BlockSpec(memory_space=pl.ANY),
            scratch_shapes=[pltpu.VMEM((MOE_TILE, d), BF16), pltpu.SemaphoreType.DMA((2,))]),
        out_shape=jax.ShapeDtypeStruct((n_rows, d), BF16),
        compiler_params=_params("arbitrary"),
        name="moe_scatter_chunks",
    )(dst, used, pad_start, pad_n, n_used, xs_local)


def _gather_chunks_kernel(dst_ref, used_ref, ys_ref, o_ref, zero_ref, sem):
    i = pl.program_id(0)
    chunks = MOE_SLOTS // MOE_CHUNK
    base = i * chunks

    @pl.when(i == 0)
    def _():
        zero_ref[...] = jnp.zeros_like(zero_ref)

    _run_copies(
        (0, used_ref[i], lambda c: pltpu.make_async_copy(
            ys_ref.at[_chunk_rows(dst_ref[base + c]), :], o_ref.at[_chunk_rows(base + c), :], sem.at[0])),
        (used_ref[i], chunks, lambda c: pltpu.make_async_copy(
            zero_ref, o_ref.at[_chunk_rows(base + c), :], sem.at[1])))


def _gather_chunks(ys, dst, used):
    d = ys.shape[1]
    return pl.pallas_call(
        _gather_chunks_kernel,
        grid_spec=pltpu.PrefetchScalarGridSpec(
            num_scalar_prefetch=2,
            grid=(used.shape[0],),
            in_specs=[pl.BlockSpec(memory_space=pl.ANY)],
            out_specs=pl.BlockSpec(memory_space=pl.ANY),
            scratch_shapes=[pltpu.VMEM((MOE_CHUNK, d), BF16), pltpu.SemaphoreType.DMA((2,))]),
        out_shape=jax.ShapeDtypeStruct((used.shape[0] * MOE_SLOTS, d), BF16),
        compiler_params=_params("arbitrary"),
        name="moe_gather_chunks",
    )(dst, used, ys)


def _experts_kernel(te_ref, nt_ref, xs_ref, wg_ref, wu_ref, wd_ref, o_ref):
    used = pl.program_id(0) < nt_ref[0]

    @pl.when(used)
    def _():
        x = xs_ref[...]
        gate = jnp.dot(x, wg_ref[0].astype(BF16), preferred_element_type=F32)
        up = jnp.dot(x, wu_ref[0].astype(BF16), preferred_element_type=F32)
        hid = (_silu(gate) * up).astype(BF16)
        o_ref[...] = jnp.dot(hid, wd_ref[0].astype(BF16), preferred_element_type=F32).astype(BF16)

    @pl.when(jnp.logical_not(used))
    def _():
        o_ref[...] = jnp.zeros_like(o_ref)


def _experts(xs, tile_expert, n_used, w_gate, w_up, w_down):
    n_rows, d = xs.shape
    tile = MOE_TILE
    by_expert = lambda j, te, nt: (te[j], 0, 0)
    return pl.pallas_call(
        _experts_kernel,
        grid_spec=pltpu.PrefetchScalarGridSpec(
            num_scalar_prefetch=2,
            grid=(n_rows // tile,),
            in_specs=[pl.BlockSpec((tile, d), lambda j, te, nt: (jnp.maximum(jnp.minimum(j, nt[0] - 1), 0), 0)),
                      pl.BlockSpec((1, d, MOE_FF), by_expert),
                      pl.BlockSpec((1, d, MOE_FF), by_expert),
                      pl.BlockSpec((1, MOE_FF, d), by_expert)],
            out_specs=pl.BlockSpec((tile, d), lambda j, te, nt: (j, 0))),
        out_shape=jax.ShapeDtypeStruct((n_rows, d), BF16),
        compiler_params=_params("arbitrary"),
        name="moe_experts",
    )(tile_expert, n_used, xs, w_gate, w_up, w_down)


def _combine_kernel(ysl_ref, wslot_ref, col_ref, x1_ref, mod_ref, gpost_ref, o_ref):
    col = col_ref[...]
    tm = col.shape[0]
    lane = lax.broadcasted_iota(jnp.int32, (tm, MOE_SLOTS), 1)
    hit = jnp.where(lane == col[:, 0:1].astype(jnp.int32), 1.0,
                    jnp.where(lane == col[:, 1:2].astype(jnp.int32), 1.0, 0.0)).astype(BF16)
    ysw = (ysl_ref[...].astype(F32) * wslot_ref[:, 0:1]).astype(BF16)
    y = jnp.dot(hit, ysw, preferred_element_type=F32)
    g2 = mod_ref[0, 5:6, :]
    o_ref[...] = x1_ref[...] + g2 * _rms(y, gpost_ref[...])


def _combine(ys_local, wslot, col, x1, mod3, gpost, seq, tm):
    t, d = x1.shape
    per_b = seq // tm
    row = lambda i: (i, 0)
    return pl.pallas_call(
        _combine_kernel,
        grid=(t // tm,),
        in_specs=[pl.BlockSpec((MOE_SLOTS, d), row),
                  pl.BlockSpec((MOE_SLOTS, LANES), row),
                  pl.BlockSpec((tm, LANES), row),
                  pl.BlockSpec((tm, d), row),
                  pl.BlockSpec((1, N_MOD, d), lambda i: (i // per_b, 0, 0)),
                  pl.BlockSpec((1, d), lambda i: (0, 0))],
        out_specs=pl.BlockSpec((tm, d), row),
        out_shape=jax.ShapeDtypeStruct((t, d), F32),
        compiler_params=_params("arbitrary"),
        name="moe_combine",
    )(ys_local, wslot, col, x1, mod3, gpost)


def kernel(x, c, positions, ada_w, ada_b, pre_norm_mix, post_norm_mix, pre_norm_ffn, post_norm_ffn, w_in, conv_w, conv_b, dt_bias, a_log, d_skip, ssd_norm, q_norm, w_uq, kv_norm, w_ukv, mla_out_norm, w_out, w_group_router, b_group_router, w_expert_router, b_expert_router, w_gate, w_up, w_down):
    batch, seq, d = x.shape
    t = batch * seq
    depth = ada_w.shape[0]
    tm = min(512, seq)
    half = MLA_ROPE // 2

    cos16, sin16 = _rope_tables(positions)
    ones = lambda n: jnp.ones((t, n), F32)
    zeros = lambda n: jnp.zeros((t, n), F32)
    cos_t = jnp.concatenate([ones(MLA_NOPE), cos16, cos16, ones(HEAD_PAD - MLA_QK)], axis=1)
    sin_t = jnp.concatenate([zeros(MLA_NOPE), sin16, sin16, zeros(HEAD_PAD - MLA_QK)], axis=1)

    x2 = x.reshape(t, d)
    for l in range(depth):
        mod3 = _modulation(c, ada_w[l], ada_b[l]).reshape(batch, N_MOD, d)
        w_in_r, wq2, wk, wv_t = _inproj_weights(w_in[l], w_uq[l], w_ukv[l])
        z, xbc, dt_raw, q, k, v_t = _inproj(
            x2, mod3, pre_norm_mix[l].reshape(1, d), w_in_r, q_norm[l].reshape(1, -1), wq2,
            kv_norm[l].reshape(1, -1), wk, wv_t, cos_t, sin_t, seq, tm)
        y_ssd = _ssd(xbc, z, dt_raw, conv_w[l], conv_b[l], dt_bias[l], a_log[l], d_skip[l],
                     ssd_norm[l], batch, seq)
        y_mla = _attention(q, k, v_t, mla_out_norm[l].reshape(1, -1), batch, seq)
        w_o = w_out[l].astype(BF16)
        pad_r = LANES - MOE_GROUPS - MOE_EXPERTS
        w_router = jnp.concatenate([w_group_router[l], w_expert_router[l], jnp.zeros((d, pad_r), F32)], axis=1)
        b_router = jnp.concatenate([b_group_router[l].reshape(-1), b_expert_router[l].reshape(-1),
                                    jnp.zeros((pad_r,), F32)]).reshape(1, LANES)
        wr_hi = w_router.astype(BF16)
        wr_lo = (w_router - wr_hi.astype(F32)).astype(BF16)
        x1, xs_local, col, wslot, cnt_tiles, chunk_expert = _outproj(
            y_ssd, y_mla, x2, mod3, w_o[:SSD_INNER], w_o[SSD_INNER:], post_norm_mix[l].reshape(1, d),
            pre_norm_ffn[l].reshape(1, d), wr_hi, wr_lo, b_router, seq, MOE_TOKENS)
        n_tok_tiles = t // MOE_TOKENS
        n_rows = _moe_rows(n_tok_tiles)
        dst, used, pad_start, pad_n, n_used, tile_expert = _moe_plan(cnt_tiles, chunk_expert, n_tok_tiles, n_rows)
        xs = _scatter_chunks(xs_local, dst, used, pad_start, pad_n, n_used, n_rows)
        ys = _experts(xs, tile_expert, n_used, w_gate[l], w_up[l], w_down[l])
        ys_local = _gather_chunks(ys, dst, used)
        x2 = _combine(ys_local, wslot, col, x1, mod3, post_norm_ffn[l].reshape(1, d), seq, MOE_TOKENS)
    return x2.reshape(batch, seq, d)
```

```python
import functools
import math

import jax
import jax.numpy as jnp
from jax import lax
from jax.experimental import pallas as pl
from jax.experimental.pallas import tpu as pltpu

F32 = jnp.float32
BF16 = jnp.bfloat16
HIGHEST = lax.Precision.HIGHEST

D_MODEL = 1024
SSD_HEADS = 8
SSD_HEAD_DIM = 64
SSD_INNER = SSD_HEADS * SSD_HEAD_DIM
SSD_GROUPS = 2
SSD_STATE = 128
SSD_CONV = 4
SSD_CHUNK = 128
SSD_XBC = SSD_INNER + 2 * SSD_GROUPS * SSD_STATE
MLA_HEADS = 8
MLA_NOPE = 64
MLA_ROPE = 32
MLA_QK = MLA_NOPE + MLA_ROPE
MLA_V = 64
MLA_Q_RANK = 256
MLA_KV_RANK = 128
MLA_INNER = MLA_HEADS * MLA_V
ROPE_THETA = 10000.0
MOE_GROUPS = 4
MOE_EXPERTS_PER_GROUP = 8
MOE_EXPERTS = MOE_GROUPS * MOE_EXPERTS_PER_GROUP
MOE_FF = 256
N_MOD = 6
EPS = 1e-6

LANES = 128
HEAD_PAD = 128
ATT_BLOCK = 256
MOE_TOKENS = 512
MOE_CHUNK = 16
MOE_TILE = 256
MOE_SLOTS = 2 * MOE_TOKENS + MOE_EXPERTS * MOE_CHUNK
VMEM_LIMIT = 48 * 1024 * 1024

_C_Z = 0
_C_XBC = _C_Z + SSD_INNER
_C_CQ = _C_XBC + SSD_XBC
_C_CKV = _C_CQ + MLA_Q_RANK
_C_KR = _C_CKV + MLA_KV_RANK
_C_KRS = _C_KR + LANES
_C_DT = _C_KRS + LANES
_C_END = _C_DT + LANES


def _silu(v):
    return v * (1.0 / (1.0 + jnp.exp(-v)))


def _rms(v, gain):
    return v * lax.rsqrt(jnp.mean(v * v, axis=-1, keepdims=True) + EPS) * gain


def _params(*sem, flags=None):
    return pltpu.CompilerParams(dimension_semantics=sem, vmem_limit_bytes=VMEM_LIMIT, flags=flags)


def _mod_kernel(c_ref, w_ref, b_ref, o_ref):
    o_ref[...] = jnp.dot(_silu(c_ref[...]), w_ref[...], precision=HIGHEST,
                         preferred_element_type=F32) + b_ref[...]


def _modulation(c, ada_w, ada_b):
    b, d = c.shape
    n = ada_w.shape[1]
    return pl.pallas_call(
        _mod_kernel,
        grid=(n // d,),
        in_specs=[pl.BlockSpec((b, d), lambda j: (0, 0)),
                  pl.BlockSpec((d, d), lambda j: (0, j)),
                  pl.BlockSpec((1, d), lambda j: (0, j))],
        out_specs=pl.BlockSpec((b, d), lambda j: (0, j)),
        out_shape=jax.ShapeDtypeStruct((b, n), F32),
        compiler_params=_params("arbitrary"),
        name="modulation",
    )(c, ada_w, ada_b.reshape(1, n))


def _rope_kernel(pos_ref, inv_ref, cos_ref, sin_ref):
    ang = pos_ref[...].astype(F32) * inv_ref[...]
    cos_ref[...] = jnp.cos(ang)
    sin_ref[...] = jnp.sin(ang)


def _rope_tables(positions):
    t = positions.size
    half = MLA_ROPE // 2
    per_row = LANES // half
    inv = 1.0 / (ROPE_THETA ** (jnp.arange(0, MLA_ROPE, 2, dtype=F32) / MLA_ROPE))
    pos_rep = jnp.repeat(positions.reshape(t), half).reshape(t // per_row, LANES)
    inv_t = jnp.tile(inv, per_row).reshape(1, LANES)
    rows = t // per_row
    blk = min(rows, 512)
    cos, sin = pl.pallas_call(
        _rope_kernel,
        grid=(rows // blk,),
        in_specs=[pl.BlockSpec((blk, LANES), lambda i: (i, 0)),
                  pl.BlockSpec((1, LANES), lambda i: (0, 0))],
        out_specs=[pl.BlockSpec((blk, LANES), lambda i: (i, 0))] * 2,
        out_shape=[jax.ShapeDtypeStruct((rows, LANES), F32)] * 2,
        compiler_params=_params("arbitrary"),
        name="rope_tables",
    )(pos_rep, inv_t)
    return cos.reshape(t, half), sin.reshape(t, half)


def _inproj_kernel(x_ref, mod_ref, gpre_ref, win_ref, qn_ref, wq_ref, kvn_ref, wk_ref, wvt_ref,
                   cos_ref, sin_ref, z_ref, xbc_ref, dt_ref, q_ref, k_ref, vt_ref):
    x = x_ref[...]
    sh = mod_ref[0, 0:1, :]
    sc = mod_ref[0, 1:2, :]
    h = (_rms(x, gpre_ref[...]) * (1.0 + sc) + sh).astype(BF16)
    u = jnp.dot(h, win_ref[...], preferred_element_type=F32)
    z_ref[...] = u[:, _C_Z:_C_XBC].astype(BF16)
    xbc_ref[...] = u[:, _C_XBC:_C_CQ].astype(BF16)
    dt_ref[...] = u[:, _C_DT:_C_END]
    cos_t = cos_ref[...]
    sin_t = sin_ref[...]
    cq = _rms(u[:, _C_CQ:_C_CKV], qn_ref[...]).astype(BF16)
    q2 = jnp.dot(cq, wq_ref[...], preferred_element_type=F32)
    scale = MLA_QK ** -0.5
    nq = MLA_HEADS * HEAD_PAD
    for hd in range(MLA_HEADS):
        a = q2[:, hd * HEAD_PAD:(hd + 1) * HEAD_PAD]
        b = q2[:, nq + hd * HEAD_PAD:nq + (hd + 1) * HEAD_PAD]
        q_ref[:, hd * HEAD_PAD:(hd + 1) * HEAD_PAD] = ((a * cos_t + b * sin_t) * scale).astype(BF16)
    ckv = _rms(u[:, _C_CKV:_C_KR], kvn_ref[...]).astype(BF16)
    kn = jnp.dot(ckv, wk_ref[...], preferred_element_type=F32)
    k_pe = u[:, _C_KR:_C_KRS] * cos_t + u[:, _C_KRS:_C_DT] * sin_t
    for hd in range(MLA_HEADS):
        k_ref[:, hd * HEAD_PAD:(hd + 1) * HEAD_PAD] = (
            kn[:, hd * HEAD_PAD:(hd + 1) * HEAD_PAD] + k_pe).astype(BF16)
    v_t = lax.dot_general(wvt_ref[...], ckv, (((1,), (1,)), ((), ())), preferred_element_type=F32)
    for s in range(vt_ref.shape[0]):
        vt_ref[s] = v_t[:, s * ATT_BLOCK:(s + 1) * ATT_BLOCK].astype(BF16)


def _inproj_weights(w_in, w_uq, w_ukv):
    d = w_in.shape[0]
    half = MLA_ROPE // 2
    o_z, o_xbc, o_dt = 0, SSD_INNER, SSD_INNER + SSD_XBC
    o_cq = o_dt + SSD_HEADS
    o_ckv = o_cq + MLA_Q_RANK
    o_kr = o_ckv + MLA_KV_RANK
    zeros = lambda n: jnp.zeros((d, n), F32)
    kr = w_in[:, o_kr:o_kr + MLA_ROPE]
    kr_blk = jnp.concatenate([zeros(MLA_NOPE), kr, zeros(HEAD_PAD - MLA_QK)], axis=1)
    krs_blk = jnp.concatenate([zeros(MLA_NOPE), -kr[:, half:], kr[:, :half], zeros(HEAD_PAD - MLA_QK)], axis=1)
    dt_blk = jnp.concatenate([w_in[:, o_dt:o_dt + SSD_HEADS], zeros(LANES - SSD_HEADS)], axis=1)
    w_in_r = jnp.concatenate([w_in[:, o_z:o_dt], w_in[:, o_cq:o_kr], kr_blk, krs_blk, dt_blk], axis=1)
    r = w_uq.shape[0]
    zq = jnp.zeros((r, MLA_HEADS, HEAD_PAD - MLA_QK), F32)
    zn = jnp.zeros((r, MLA_HEADS, MLA_NOPE), F32)
    wq_plain = jnp.concatenate([w_uq, zq], axis=2).reshape(r, MLA_HEADS * HEAD_PAD)
    wq_rot = jnp.concatenate([zn, -w_uq[:, :, MLA_NOPE + half:], w_uq[:, :, MLA_NOPE:MLA_NOPE + half], zq],
                             axis=2).reshape(r, MLA_HEADS * HEAD_PAD)
    wq2 = jnp.concatenate([wq_plain, wq_rot], axis=1)
    rk = w_ukv.shape[0]
    zk = jnp.zeros((rk, MLA_HEADS, HEAD_PAD - MLA_NOPE), F32)
    wk = jnp.concatenate([w_ukv[:, :, :MLA_NOPE], zk], axis=2).reshape(rk, MLA_HEADS * HEAD_PAD)
    wv_t = w_ukv[:, :, MLA_NOPE:].reshape(rk, MLA_INNER).T
    return w_in_r.astype(BF16), wq2.astype(BF16), wk.astype(BF16), wv_t.astype(BF16)


def _inproj(x2, mod3, gpre, w_in_r, q_norm, wq2, kv_norm, wk, wv_t, cos_t, sin_t, seq, tm):
    t, d = x2.shape
    per_b = seq // tm
    row = lambda i: (i, 0)
    const = lambda i: (0, 0)
    nqk = MLA_HEADS * HEAD_PAD
    slabs = tm // ATT_BLOCK
    return pl.pallas_call(
        _inproj_kernel,
        grid=(t // tm,),
        in_specs=[pl.BlockSpec((tm, d), row),
                  pl.BlockSpec((1, N_MOD, d), lambda i: (i // per_b, 0, 0)),
                  pl.BlockSpec((1, d), const),
                  pl.BlockSpec(w_in_r.shape, const),
                  pl.BlockSpec((1, MLA_Q_RANK), const),
                  pl.BlockSpec(wq2.shape, const),
                  pl.BlockSpec((1, MLA_KV_RANK), const),
                  pl.BlockSpec(wk.shape, const),
                  pl.BlockSpec(wv_t.shape, const),
                  pl.BlockSpec((tm, LANES), row),
                  pl.BlockSpec((tm, LANES), row)],
        out_specs=[pl.BlockSpec((tm, SSD_INNER), row),
                   pl.BlockSpec((tm, SSD_XBC), row),
                   pl.BlockSpec((tm, LANES), row),
                   pl.BlockSpec((tm, nqk), row),
                   pl.BlockSpec((tm, nqk), row),
                   pl.BlockSpec((slabs, MLA_INNER, ATT_BLOCK), lambda i: (i, 0, 0))],
        out_shape=[jax.ShapeDtypeStruct((t, SSD_INNER), BF16),
                   jax.ShapeDtypeStruct((t, SSD_XBC), BF16),
                   jax.ShapeDtypeStruct((t, LANES), F32),
                   jax.ShapeDtypeStruct((t, nqk), BF16),
                   jax.ShapeDtypeStruct((t, nqk), BF16),
                   jax.ShapeDtypeStruct((t // ATT_BLOCK, MLA_INNER, ATT_BLOCK), BF16)],
        compiler_params=_params("arbitrary"),
        name="inproj",
    )(x2, mod3, gpre, w_in_r, q_norm, wq2, kv_norm, wk, wv_t, cos_t, sin_t)


def _ssd_kernel(xbc_ref, z_ref, dt_ref, cw_ref, cb_ref, dtb_ref, alog_ref, dsk_ref, gn_ref,
                e64_ref, e128_ref, y_ref, tail_ref, state_ref):
    L = SSD_CHUNK
    gw = SSD_INNER // SSD_GROUPS

    @pl.when(pl.program_id(1) == 0)
    def _():
        tail_ref[...] = jnp.zeros_like(tail_ref)
        state_ref[...] = jnp.zeros_like(state_ref)

    cur = xbc_ref[...].astype(F32)
    prev8 = tail_ref[...]
    row8 = lax.broadcasted_iota(jnp.int32, prev8.shape, 0)
    acc = cur * cw_ref[SSD_CONV - 1:SSD_CONV, :] + cb_ref[...]
    for j in range(1, SSD_CONV):
        r = pltpu.roll(cur, j, 0)
        head = jnp.where(row8 < j, pltpu.roll(prev8, j, 0), r[0:8])
        r = jnp.concatenate([head, r[8:]], axis=0)
        acc = acc + r * cw_ref[SSD_CONV - 1 - j:SSD_CONV - j, :]
    tail_ref[...] = cur[L - 8:L]
    act = _silu(acc)
    xs = act[:, :SSD_INNER]
    bm = act[:, SSD_INNER:SSD_INNER + SSD_GROUPS * SSD_STATE].astype(BF16)
    cm = act[:, SSD_INNER + SSD_GROUPS * SSD_STATE:].astype(BF16)

    dt_in = dt_ref[...] + dtb_ref[...]
    dt = jnp.maximum(dt_in, 0.0) + jnp.log(1.0 + jnp.exp(-jnp.abs(dt_in)))
    adt = dt * (-jnp.exp(alog_ref[...]))
    ri = lax.broadcasted_iota(jnp.int32, (L, L), 0)
    ci = lax.broadcasted_iota(jnp.int32, (L, L), 1)
    causal = ci <= ri
    tril = causal.astype(F32)
    a_cs = jnp.dot(tril, adt, precision=HIGHEST, preferred_element_type=F32)
    a_cs_t = a_cs.T
    cs64 = jnp.dot(a_cs, e64_ref[...], precision=HIGHEST, preferred_element_type=F32)
    cs128 = jnp.dot(a_cs, e128_ref[...], precision=HIGHEST, preferred_element_type=F32)
    dt64 = jnp.dot(dt, e64_ref[...], precision=HIGHEST, preferred_element_type=F32)

    xd = xs * dt64
    xd_b = xd.astype(BF16)
    last = cs64[L - 1:L, :]
    xdw = (xd * jnp.exp(last - cs64)).astype(BF16)
    chunk_decay = jnp.exp(last)
    in_decay = jnp.exp(cs64)

    lane = lax.broadcasted_iota(jnp.int32, (L, LANES), 1)
    lo = lane < SSD_HEAD_DIM
    zero_b = jnp.zeros((L, LANES), BF16)
    y_parts = []
    new_states = []
    for g in range(SSD_GROUPS):
        bg = bm[:, g * SSD_STATE:(g + 1) * SSD_STATE]
        cg = cm[:, g * SSD_STATE:(g + 1) * SSD_STATE]
        cb = lax.dot_general(cg, bg, (((1,), (1,)), ((), ())), preferred_element_type=F32)
        prev = state_ref[:, g * gw:(g + 1) * gw]
        y_off = jnp.dot(cg, prev.astype(BF16), preferred_element_type=F32) * in_decay[:, g * gw:(g + 1) * gw]
        st = lax.dot_general(bg, xdw[:, g * gw:(g + 1) * gw], (((0,), (0,)), ((), ())),
                             preferred_element_type=F32)
        new_states.append(prev * chunk_decay[:, g * gw:(g + 1) * gw] + st)
        heads_per_group = SSD_HEADS // SSD_GROUPS
        for pair in range(heads_per_group // 2):
            h0 = g * heads_per_group + 2 * pair
            blk = xd_b[:, h0 * SSD_HEAD_DIM:(h0 + 2) * SSD_HEAD_DIM]
            y_pair = None
            for k in range(2):
                hh = h0 + k
                diff = cs128[:, hh * LANES:(hh + 1) * LANES] - a_cs_t[hh:hh + 1, :]
                decay = jnp.where(causal, jnp.exp(diff), 0.0)
                m = (cb * decay).astype(BF16)
                rhs = jnp.where(lo, blk, zero_b) if k == 0 else jnp.where(lo, zero_b, blk)
                part = jnp.dot(m, rhs, preferred_element_type=F32)
                y_pair = part if y_pair is None else y_pair + part
            c0 = (2 * pair) * SSD_HEAD_DIM
            y_parts.append(y_pair + y_off[:, c0:c0 + LANES])
    state_ref[...] = jnp.concatenate(new_states, axis=1)
    y = jnp.concatenate(y_parts, axis=1) + dsk_ref[...] * xs
    gated = y * _silu(z_ref[...].astype(F32))
    y_ref[...] = _rms(gated, gn_ref[...]).astype(BF16)


def _ssd(xbc, z, dt_raw, conv_w, conv_b, dt_bias, a_log, d_skip, ssd_norm, batch, seq):
    t = xbc.shape[0]
    nc = seq // SSD_CHUNK
    L = SSD_CHUNK
    row = lambda b, c: (b * nc + c, 0)
    const = lambda b, c: (0, 0)
    pad = lambda v: jnp.concatenate([v.reshape(1, -1), jnp.zeros((1, LANES - v.size), F32)], axis=1)
    head_of_lane64 = jnp.arange(SSD_INNER) // SSD_HEAD_DIM
    head_of_lane128 = jnp.arange(SSD_HEADS * LANES) // LANES
    rows = jnp.arange(LANES)[:, None]
    e64 = (rows == head_of_lane64[None, :]).astype(F32)
    e128 = (rows == head_of_lane128[None, :]).astype(F32)
    dsk = jnp.repeat(d_skip.astype(F32), SSD_HEAD_DIM).reshape(1, SSD_INNER)
    return pl.pallas_call(
        _ssd_kernel,
        grid=(batch, nc),
        in_specs=[pl.BlockSpec((L, SSD_XBC), row),
                  pl.BlockSpec((L, SSD_INNER), row),
                  pl.BlockSpec((L, LANES), row),
                  pl.BlockSpec((SSD_CONV, SSD_XBC), const),
                  pl.BlockSpec((1, SSD_XBC), const),
                  pl.BlockSpec((1, LANES), const),
                  pl.BlockSpec((1, LANES), const),
                  pl.BlockSpec((1, SSD_INNER), const),
                  pl.BlockSpec((1, SSD_INNER), const),
                  pl.BlockSpec(e64.shape, const),
                  pl.BlockSpec(e128.shape, const)],
        out_specs=pl.BlockSpec((L, SSD_INNER), row),
        out_shape=jax.ShapeDtypeStruct((t, SSD_INNER), BF16),
        scratch_shapes=[pltpu.VMEM((8, SSD_XBC), F32),
                        pltpu.VMEM((SSD_STATE, SSD_INNER), F32)],
        compiler_params=_params("arbitrary", "arbitrary"),
        name="ssd",
    )(xbc, z, dt_raw, conv_w, conv_b.reshape(1, -1), pad(dt_bias), pad(a_log), dsk,
      ssd_norm.reshape(1, -1), e64, e128)


def _attn_kernel(q_ref, k_ref, vt_ref, gn_ref, o_ref, s_ref):
    blk = ATT_BLOCK
    qi = pl.program_id(1)
    causal = (lax.broadcasted_iota(jnp.int32, (blk, blk), 0)
              <= lax.broadcasted_iota(jnp.int32, (blk, blk), 1))
    dims = (((1,), (1,)), ((), ()))
    qs = [q_ref[:, hd * HEAD_PAD:(hd + 1) * HEAD_PAD] for hd in range(MLA_HEADS)]

    def step(j, carry, masked):
        rows = pl.ds(pl.multiple_of(j * blk, blk), blk)
        m_new = []
        for hd in range(MLA_HEADS):
            kk = k_ref[rows, hd * HEAD_PAD:(hd + 1) * HEAD_PAD]
            s = lax.dot_general(kk, qs[hd], dims, preferred_element_type=F32)
            if masked:
                s = jnp.where(causal, s, -jnp.inf)
            s_ref[hd] = s
            m_new.append(jnp.maximum(carry[hd][0], jnp.max(s, axis=0, keepdims=True)))
        new = []
        for hd in range(MLA_HEADS):
            m_old, l_old, acc = carry[hd]
            p = jnp.exp(s_ref[hd] - m_new[hd])
            alpha = jnp.exp(m_old - m_new[hd])
            l_new = alpha * l_old + jnp.sum(p, axis=0, keepdims=True)
            vt = vt_ref[j, hd * MLA_V:(hd + 1) * MLA_V, :]
            acc = acc * alpha + jnp.dot(vt, p.astype(BF16), preferred_element_type=F32)
            new.append((m_new[hd], l_new, acc))
        return tuple(new)

    neg = jnp.full((1, blk), -jnp.inf, F32)
    init = tuple((neg, jnp.zeros((1, blk), F32), jnp.zeros((MLA_V, blk), F32))
                 for _ in range(MLA_HEADS))
    carry = lax.fori_loop(0, qi, functools.partial(step, masked=False), init)
    final = step(qi, carry, True)
    y_t = jnp.concatenate([acc / l for _, l, acc in final], axis=0)
    o_ref[...] = _rms(y_t.T, gn_ref[...]).astype(BF16)


def _attention(q, k, v_t, gn, batch, seq):
    t = q.shape[0]
    blk = ATT_BLOCK
    nq = seq // blk
    nqk = MLA_HEADS * HEAD_PAD
    return pl.pallas_call(
        _attn_kernel,
        grid=(batch, nq),
        in_specs=[pl.BlockSpec((blk, nqk), lambda b, i: (b * nq + i, 0)),
                  pl.BlockSpec((seq, nqk), lambda b, i: (b, 0)),
                  pl.BlockSpec((nq, MLA_INNER, blk), lambda b, i: (b, 0, 0)),
                  pl.BlockSpec((1, MLA_INNER), lambda b, i: (0, 0))],
        out_specs=pl.BlockSpec((blk, MLA_INNER), lambda b, i: (b * nq + i, 0)),
        out_shape=jax.ShapeDtypeStruct((t, MLA_INNER), BF16),
        scratch_shapes=[pltpu.VMEM((MLA_HEADS, blk, blk), F32)],
        compiler_params=_params("arbitrary", "arbitrary"),
        name="attention",
    )(q, k, v_t, gn)


def _outproj_kernel(ys_ref, ym_ref, x_ref, mod_ref, wtop_ref, wbot_ref, gpm_ref, gpf_ref,
                    wrh_ref, wrl_ref, br_ref, x1_ref, xs_ref, col_ref, wslot_ref, cnt_ref, ce_ref):
    y = (jnp.dot(ys_ref[...], wtop_ref[...], preferred_element_type=F32)
         + jnp.dot(ym_ref[...], wbot_ref[...], preferred_element_type=F32))
    g1 = mod_ref[0, 2:3, :]
    sh2 = mod_ref[0, 3:4, :]
    sc2 = mod_ref[0, 4:5, :]
    x1 = x_ref[...] + g1 * _rms(y, gpm_ref[...])
    x1_ref[...] = x1
    h2 = _rms(x1, gpf_ref[...]) * (1.0 + sc2) + sh2
    tm = h2.shape[0]
    h_hi = h2.astype(BF16)
    h_lo = (h2 - h_hi.astype(F32)).astype(BF16)
    logits = (jnp.dot(h_hi, wrh_ref[...], preferred_element_type=F32)
              + jnp.dot(h_lo, wrh_ref[...], preferred_element_type=F32)
              + jnp.dot(h_hi, wrl_ref[...], preferred_element_type=F32)) + br_ref[...]
    lt = logits.T
    row = lax.broadcasted_iota(jnp.int32, lt.shape, 0).astype(F32)
    ninf = -jnp.inf
    big = 1e9
    is_g = row < MOE_GROUPS
    gl = jnp.where(is_g, lt, ninf)
    gmax = jnp.max(gl, axis=0, keepdims=True)
    gsum = jnp.sum(jnp.where(is_g, jnp.exp(lt - gmax), 0.0), axis=0, keepdims=True)
    g_w = 1.0 / gsum
    g_idx = jnp.min(jnp.where(gl == gmax, row, big), axis=0, keepdims=True)
    first = MOE_GROUPS + MOE_EXPERTS_PER_GROUP * g_idx
    el = jnp.where(row >= first, jnp.where(row < first + MOE_EXPERTS_PER_GROUP, lt, ninf), ninf)
    m1 = jnp.max(el, axis=0, keepdims=True)
    i1 = jnp.min(jnp.where(el == m1, row, big), axis=0, keepdims=True)
    el2 = jnp.where(row == i1, ninf, el)
    m2 = jnp.max(el2, axis=0, keepdims=True)
    i2 = jnp.min(jnp.where(el2 == m2, row, big), axis=0, keepdims=True)
    r = jnp.exp(m2 - m1)
    w1 = g_w / (1.0 + r)
    w2 = g_w * r / (1.0 + r)
    sel1 = row == i1
    sel2 = row == i2
    both = jnp.where(sel1, 1.0, jnp.where(sel2, 1.0, 0.0))
    cnt16 = jnp.floor((jnp.sum(both, axis=1, keepdims=True) + (MOE_CHUNK - 1)) * (1.0 / MOE_CHUNK))
    er = lax.broadcasted_iota(jnp.int32, (LANES, LANES), 0)
    ec = lax.broadcasted_iota(jnp.int32, (LANES, LANES), 1)
    cnt16_b = jnp.broadcast_to(cnt16, (LANES, LANES))
    off16 = jnp.dot((ec < er).astype(BF16), cnt16_b.astype(BF16), preferred_element_type=F32)[:, 0:1]
    earlier = (lax.broadcasted_iota(jnp.int32, (tm, tm), 0)
               < lax.broadcasted_iota(jnp.int32, (tm, tm), 1)).astype(BF16)
    base = off16 * MOE_CHUNK + jnp.dot(both.astype(BF16), earlier, preferred_element_type=F32)
    lpos1 = jnp.sum(jnp.where(sel1, base, 0.0), axis=0, keepdims=True)
    lpos2 = jnp.sum(jnp.where(sel2, base, 0.0), axis=0, keepdims=True)
    srow = lax.broadcasted_iota(jnp.int32, (MOE_SLOTS, tm), 0)
    hit1 = srow == lpos1.astype(jnp.int32)
    hit2 = srow == lpos2.astype(jnp.int32)
    perm = jnp.where(hit1, 1.0, jnp.where(hit2, 1.0, 0.0)).astype(BF16)
    xs_ref[...] = jnp.dot(perm, h_hi, preferred_element_type=F32).astype(BF16)
    wslot = jnp.sum(jnp.where(hit1, w1, jnp.where(hit2, w2, 0.0)), axis=1, keepdims=True)
    wslot_ref[...] = jnp.broadcast_to(wslot, wslot_ref.shape)
    col_ref[...] = jnp.where(row == 0, lpos1, jnp.where(row == 1, lpos2, 0.0)).T
    cnt_ref[...] = cnt16_b
    chunk = lax.broadcasted_iota(jnp.int32, (LANES, LANES), 1).astype(F32)
    ce = jnp.sum(jnp.where(off16 + cnt16 <= chunk, 1.0, 0.0), axis=0, keepdims=True) - MOE_GROUPS
    ce_ref[...] = jnp.broadcast_to(ce, ce_ref.shape)


def _outproj(y_ssd, y_mla, x2, mod3, w_top, w_bot, gpm, gpf, wr_hi, wr_lo, b_router, seq, tm):
    t, d = x2.shape
    per_b = seq // tm
    n_tok_tiles = t // tm
    row = lambda i: (i, 0)
    const = lambda i: (0, 0)
    return pl.pallas_call(
        _outproj_kernel,
        grid=(n_tok_tiles,),
        in_specs=[pl.BlockSpec((tm, SSD_INNER), row),
                  pl.BlockSpec((tm, MLA_INNER), row),
                  pl.BlockSpec((tm, d), row),
                  pl.BlockSpec((1, N_MOD, d), lambda i: (i // per_b, 0, 0)),
                  pl.BlockSpec(w_top.shape, const),
                  pl.BlockSpec(w_bot.shape, const),
                  pl.BlockSpec((1, d), const),
                  pl.BlockSpec((1, d), const),
                  pl.BlockSpec(wr_hi.shape, const),
                  pl.BlockSpec(wr_lo.shape, const),
                  pl.BlockSpec((1, LANES), const)],
        out_specs=[pl.BlockSpec((tm, d), row),
                   pl.BlockSpec((MOE_SLOTS, d), row),
                   pl.BlockSpec((tm, LANES), row),
                   pl.BlockSpec((MOE_SLOTS, LANES), row),
                   pl.BlockSpec((LANES, LANES), row),
                   pl.BlockSpec((8, LANES), row)],
        out_shape=[jax.ShapeDtypeStruct((t, d), F32),
                   jax.ShapeDtypeStruct((n_tok_tiles * MOE_SLOTS, d), BF16),
                   jax.ShapeDtypeStruct((t, LANES), F32),
                   jax.ShapeDtypeStruct((n_tok_tiles * MOE_SLOTS, LANES), F32),
                   jax.ShapeDtypeStruct((n_tok_tiles * LANES, LANES), F32),
                   jax.ShapeDtypeStruct((n_tok_tiles * 8, LANES), F32)],
        compiler_params=_params("arbitrary"),
        name="outproj_router",
    )(y_ssd, y_mla, x2, mod3, w_top, w_bot, gpm, gpf, wr_hi, wr_lo, b_router)


def _moe_rows(n_tok_tiles):
    rows = n_tok_tiles * MOE_SLOTS + MOE_EXPERTS * (MOE_TILE - MOE_CHUNK)
    return (rows + MOE_TILE - 1) // MOE_TILE * MOE_TILE


def _moe_plan(cnt_tiles, chunk_expert, n_tok_tiles, n_rows):
    chunks_per_tile = MOE_SLOTS // MOE_CHUNK
    chunks_per_rows = MOE_TILE // MOE_CHUNK
    cnt = cnt_tiles.reshape(n_tok_tiles, LANES, LANES)[:, MOE_GROUPS:MOE_GROUPS + MOE_EXPERTS, 0]
    cnt = cnt.astype(jnp.int32)
    ce = chunk_expert.reshape(n_tok_tiles, 8, LANES)[:, 0, :chunks_per_tile].astype(jnp.int32)
    used = jnp.sum(cnt, axis=1)
    total = jnp.sum(cnt, axis=0)
    padded = (total + chunks_per_rows - 1) // chunks_per_rows * chunks_per_rows
    e_end = jnp.cumsum(padded)
    e_start = e_end - padded
    run_global = e_start[None, :] + jnp.cumsum(cnt, axis=0) - cnt
    run_local = jnp.cumsum(cnt, axis=1) - cnt
    experts = jnp.arange(MOE_EXPERTS, dtype=jnp.int32)
    shift = jnp.sum(jnp.where(ce[:, :, None] == experts, (run_global - run_local)[:, None, :], 0), axis=-1)
    dst = shift + jnp.arange(chunks_per_tile, dtype=jnp.int32)[None, :]
    n_row_tiles = n_rows // MOE_TILE
    n_used = e_end[-1] // chunks_per_rows
    j = jnp.minimum(jnp.arange(n_row_tiles, dtype=jnp.int32), jnp.maximum(n_used - 1, 0))
    tile_expert = jnp.sum((e_end[None, :] // chunks_per_rows <= j[:, None]).astype(jnp.int32), axis=1)
    tile_expert = jnp.minimum(tile_expert, MOE_EXPERTS - 1)
    i32 = lambda v: v.astype(jnp.int32)
    return (i32(dst.reshape(-1)), i32(used), i32(e_start + total), i32(padded - total),
            i32(n_used.reshape(1)), i32(tile_expert))


def _chunk_rows(chunk):
    return pl.ds(pl.multiple_of(chunk * MOE_CHUNK, MOE_CHUNK), MOE_CHUNK)


def _run_copies(*groups):
    for wait in (False, True):
        for lo, hi, make_copy in groups:
            def body(k, carry, wait=wait, make_copy=make_copy):
                cp = make_copy(k)
                cp.wait() if wait else cp.start()
                return carry
            lax.fori_loop(lo, hi, body, 0)


def _scatter_chunks_kernel(dst_ref, used_ref, pad_start_ref, pad_n_ref, nused_ref, src_ref, o_ref,
                           zero_ref, sem):
    i = pl.program_id(0)
    base = i * (MOE_SLOTS // MOE_CHUNK)

    @pl.when(i == 0)
    def _():
        zero_ref[...] = jnp.zeros_like(zero_ref)
        pads = [(0, pad_n_ref[e], lambda c, e=e: pltpu.make_async_copy(
            zero_ref.at[pl.ds(0, MOE_CHUNK), :], o_ref.at[_chunk_rows(pad_start_ref[e] + c), :], sem.at[1]))
            for e in range(MOE_EXPERTS)]
        tail = (nused_ref[0], o_ref.shape[0] // MOE_TILE, lambda j: pltpu.make_async_copy(
            zero_ref, o_ref.at[pl.ds(pl.multiple_of(j * MOE_TILE, MOE_TILE), MOE_TILE), :], sem.at[1]))
        _run_copies(tail, *pads)

    _run_copies((0, used_ref[i], lambda c: pltpu.make_async_copy(
        src_ref.at[_chunk_rows(c), :], o_ref.at[_chunk_rows(dst_ref[base + c]), :], sem.at[0])))


def _scatter_chunks(xs_local, dst, used, pad_start, pad_n, n_used, n_rows):
    d = xs_local.shape[1]
    return pl.pallas_call(
        _scatter_chunks_kernel,
        grid_spec=pltpu.PrefetchScalarGridSpec(
            num_scalar_prefetch=5,
            grid=(used.shape[0],),
            in_specs=[pl.BlockSpec((MOE_SLOTS, d), lambda i, *_: (i, 0))],
            out_specs=pl.BlockSpec(memory_space=pl.ANY),
            scratch_shapes=[pltpu.VMEM((MOE_TILE, d), BF16), pltpu.SemaphoreType.DMA((2,))]),
        out_shape=jax.ShapeDtypeStruct((n_rows, d), BF16),
        compiler_params=_params("arbitrary"),
        name="moe_scatter_chunks",
    )(dst, used, pad_start, pad_n, n_used, xs_local)


def _experts_kernel(te_ref, nt_ref, xs_ref, wg_ref, wu_ref, wd_ref, o_ref):
    used = pl.program_id(0) < nt_ref[0]

    @pl.when(used)
    def _():
        x = xs_ref[...]
        gate = jnp.dot(x, wg_ref[0].astype(BF16), preferred_element_type=F32)
        up = jnp.dot(x, wu_ref[0].astype(BF16), preferred_element_type=F32)
        hid = (_silu(gate) * up).astype(BF16)
        o_ref[...] = jnp.dot(hid, wd_ref[0].astype(BF16), preferred_element_type=F32).astype(BF16)

    @pl.when(jnp.logical_not(used))
    def _():
        o_ref[...] = jnp.zeros_like(o_ref)


def _experts(xs, tile_expert, n_used, w_gate, w_up, w_down):
    n_rows, d = xs.shape
    tile = MOE_TILE
    by_expert = lambda j, te, nt: (te[j], 0, 0)
    return pl.pallas_call(
        _experts_kernel,
        grid_spec=pltpu.PrefetchScalarGridSpec(
            num_scalar_prefetch=2,
            grid=(n_rows // tile,),
            in_specs=[pl.BlockSpec((tile, d), lambda j, te, nt: (jnp.maximum(jnp.minimum(j, nt[0] - 1), 0), 0)),
                      pl.BlockSpec((1, d, MOE_FF), by_expert),
                      pl.BlockSpec((1, d, MOE_FF), by_expert),
                      pl.BlockSpec((1, MOE_FF, d), by_expert)],
            out_specs=pl.BlockSpec((tile, d), lambda j, te, nt: (j, 0))),
        out_shape=jax.ShapeDtypeStruct((n_rows, d), BF16),
        compiler_params=_params("arbitrary"),
        name="moe_experts",
    )(tile_expert, n_used, xs, w_gate, w_up, w_down)


def _combine_kernel(dst_ref, used_ref, wslot_ref, col_ref, x1_ref, mod_ref, gpost_ref, ys_ref, o_ref,
                    ysl_ref, sem):
    i = pl.program_id(0)
    chunks = MOE_SLOTS // MOE_CHUNK
    base = i * chunks
    used = used_ref[i]

    def chunk_copy(c):
        return pltpu.make_async_copy(ys_ref.at[_chunk_rows(dst_ref[base + c]), :],
                                     ysl_ref.at[_chunk_rows(c), :], sem.at[0])

    def start(c, carry):
        chunk_copy(c).start()
        return carry

    def clear(c, carry):
        ysl_ref[_chunk_rows(c), :] = jnp.zeros((MOE_CHUNK, ysl_ref.shape[1]), BF16)
        return carry

    def wait(c, carry):
        chunk_copy(c).wait()
        return carry

    lax.fori_loop(0, used, start, 0)
    lax.fori_loop(used, chunks, clear, 0)
    col = col_ref[...]
    tm = col.shape[0]
    lane = lax.broadcasted_iota(jnp.int32, (tm, MOE_SLOTS), 1)
    hit = jnp.where(lane == col[:, 0:1].astype(jnp.int32), 1.0,
                    jnp.where(lane == col[:, 1:2].astype(jnp.int32), 1.0, 0.0)).astype(BF16)
    lax.fori_loop(0, used, wait, 0)
    ysw = (ysl_ref[...].astype(F32) * wslot_ref[:, 0:1]).astype(BF16)
    y = jnp.dot(hit, ysw, preferred_element_type=F32)
    g2 = mod_ref[0, 5:6, :]
    o_ref[...] = x1_ref[...] + g2 * _rms(y, gpost_ref[...])


def _combine(ys, dst, used, wslot, col, x1, mod3, gpost, seq, tm):
    t, d = x1.shape
    per_b = seq // tm
    row = lambda i, *_: (i, 0)
    return pl.pallas_call(
        _combine_kernel,
        grid_spec=pltpu.PrefetchScalarGridSpec(
            num_scalar_prefetch=2,
            grid=(t // tm,),
            in_specs=[pl.BlockSpec((MOE_SLOTS, LANES), row),
                      pl.BlockSpec((tm, LANES), row),
                      pl.BlockSpec((tm, d), row),
                      pl.BlockSpec((1, N_MOD, d), lambda i, *_: (i // per_b, 0, 0)),
                      pl.BlockSpec((1, d), lambda i, *_: (0, 0)),
                      pl.BlockSpec(memory_space=pl.ANY)],
            out_specs=pl.BlockSpec((tm, d), row),
            scratch_shapes=[pltpu.VMEM((MOE_SLOTS, d), BF16), pltpu.SemaphoreType.DMA((1,))]),
        out_shape=jax.ShapeDtypeStruct((t, d), F32),
        compiler_params=_params("arbitrary"),
        name="moe_combine",
    )(dst, used, wslot, col, x1, mod3, gpost, ys)


def kernel(x, c, positions, ada_w, ada_b, pre_norm_mix, post_norm_mix, pre_norm_ffn, post_norm_ffn, w_in, conv_w, conv_b, dt_bias, a_log, d_skip, ssd_norm, q_norm, w_uq, kv_norm, w_ukv, mla_out_norm, w_out, w_group_router, b_group_router, w_expert_router, b_expert_router, w_gate, w_up, w_down):
    batch, seq, d = x.shape
    t = batch * seq
    depth = ada_w.shape[0]
    tm = min(512, seq)
    half = MLA_ROPE // 2

    cos16, sin16 = _rope_tables(positions)
    ones = lambda n: jnp.ones((t, n), F32)
    zeros = lambda n: jnp.zeros((t, n), F32)
    cos_t = jnp.concatenate([ones(MLA_NOPE), cos16, cos16, ones(HEAD_PAD - MLA_QK)], axis=1)
    sin_t = jnp.concatenate([zeros(MLA_NOPE), sin16, sin16, zeros(HEAD_PAD - MLA_QK)], axis=1)

    x2 = x.reshape(t, d)
    for l in range(depth):
        mod3 = _modulation(c, ada_w[l], ada_b[l]).reshape(batch, N_MOD, d)
        w_in_r, wq2, wk, wv_t = _inproj_weights(w_in[l], w_uq[l], w_ukv[l])
        z, xbc, dt_raw, q, k, v_t = _inproj(
            x2, mod3, pre_norm_mix[l].reshape(1, d), w_in_r, q_norm[l].reshape(1, -1), wq2,
            kv_norm[l].reshape(1, -1), wk, wv_t, cos_t, sin_t, seq, tm)
        y_ssd = _ssd(xbc, z, dt_raw, conv_w[l], conv_b[l], dt_bias[l], a_log[l], d_skip[l],
                     ssd_norm[l], batch, seq)
        y_mla = _attention(q, k, v_t, mla_out_norm[l].reshape(1, -1), batch, seq)
        w_o = w_out[l].astype(BF16)
        pad_r = LANES - MOE_GROUPS - MOE_EXPERTS
        w_router = jnp.concatenate([w_group_router[l], w_expert_router[l], jnp.zeros((d, pad_r), F32)], axis=1)
        b_router = jnp.concatenate([b_group_router[l].reshape(-1), b_expert_router[l].reshape(-1),
                                    jnp.zeros((pad_r,), F32)]).reshape(1, LANES)
        wr_hi = w_router.astype(BF16)
        wr_lo = (w_router - wr_hi.astype(F32)).astype(BF16)
        x1, xs_local, col, wslot, cnt_tiles, chunk_expert = _outproj(
            y_ssd, y_mla, x2, mod3, w_o[:SSD_INNER], w_o[SSD_INNER:], post_norm_mix[l].reshape(1, d),
            pre_norm_ffn[l].reshape(1, d), wr_hi, wr_lo, b_router, seq, MOE_TOKENS)
        n_tok_tiles = t // MOE_TOKENS
        n_rows = _moe_rows(n_tok_tiles)
        dst, used, pad_start, pad_n, n_used, tile_expert = _moe_plan(cnt_tiles, chunk_expert, n_tok_tiles, n_rows)
        xs = _scatter_chunks(xs_local, dst, used, pad_start, pad_n, n_used, n_rows)
        ys = _experts(xs, tile_expert, n_used, w_gate[l], w_up[l], w_down[l])
        x2 = _combine(ys, dst, used, wslot, col, x1, mod3, post_norm_ffn[l].reshape(1, d), seq, MOE_TOKENS)
    return x2.reshape(batch, seq, d)
```

```python
import functools
import math

import jax
import jax.numpy as jnp
from jax import lax
from jax.experimental import pallas as pl
from jax.experimental.pallas import tpu as pltpu

F32 = jnp.float32
BF16 = jnp.bfloat16

D_MODEL = 1024
SSD_HEADS = 8
SSD_HEAD_DIM = 64
SSD_INNER = SSD_HEADS * SSD_HEAD_DIM
SSD_GROUPS = 2
SSD_STATE = 128
SSD_CONV = 4
SSD_CHUNK = 128
SSD_XBC = SSD_INNER + 2 * SSD_GROUPS * SSD_STATE
MLA_HEADS = 8
MLA_NOPE = 64
MLA_ROPE = 32
MLA_QK = MLA_NOPE + MLA_ROPE
MLA_V = 64
MLA_Q_RANK = 256
MLA_KV_RANK = 128
MLA_INNER = MLA_HEADS * MLA_V
ROPE_THETA = 10000.0
MOE_GROUPS = 4
MOE_EXPERTS_PER_GROUP = 8
MOE_EXPERTS = MOE_GROUPS * MOE_EXPERTS_PER_GROUP
MOE_FF = 256
N_MOD = 6
EPS = 1e-6
LOG2_E = math.log2(math.e)

LANES = 128
HEAD_PAD = 128
ATT_BLOCK = 256
MOE_TOKENS = 512
MOE_CHUNK = 16
MOE_TILE = 256
MOE_SLOTS = 2 * MOE_TOKENS + MOE_EXPERTS * MOE_CHUNK
VMEM_LIMIT = 48 * 1024 * 1024

_C_Z = 0
_C_XBC = _C_Z + SSD_INNER
_C_CQ = _C_XBC + SSD_XBC
_C_CKV = _C_CQ + MLA_Q_RANK
_C_KR = _C_CKV + MLA_KV_RANK
_C_KRS = _C_KR + LANES
_C_DT = _C_KRS + LANES
_C_END = _C_DT + LANES


def _silu(v):
    return v * (1.0 / (1.0 + jnp.exp(-v)))


def _rms(v, gain):
    return v * lax.rsqrt(jnp.mean(v * v, axis=-1, keepdims=True) + EPS) * gain


def _params(*sem, flags=None):
    return pltpu.CompilerParams(dimension_semantics=sem, vmem_limit_bytes=VMEM_LIMIT, flags=flags)


def _mod_kernel(c_ref, w_ref, b_ref, o_ref):
    act = _silu(c_ref[...])
    a_hi = act.astype(BF16)
    a_lo = (act - a_hi.astype(F32)).astype(BF16)
    w = w_ref[...]
    w_hi = w.astype(BF16)
    w_lo = (w - w_hi.astype(F32)).astype(BF16)
    o_ref[...] = (jnp.dot(a_hi, w_hi, preferred_element_type=F32)
                  + jnp.dot(a_lo, w_hi, preferred_element_type=F32)
                  + jnp.dot(a_hi, w_lo, preferred_element_type=F32)) + b_ref[...]


def _modulation(c, ada_w, ada_b):
    b, d = c.shape
    n = ada_w.shape[1]
    return pl.pallas_call(
        _mod_kernel,
        grid=(n // d,),
        in_specs=[pl.BlockSpec((b, d), lambda j: (0, 0)),
                  pl.BlockSpec((d, d), lambda j: (0, j)),
                  pl.BlockSpec((1, d), lambda j: (0, j))],
        out_specs=pl.BlockSpec((b, d), lambda j: (0, j)),
        out_shape=jax.ShapeDtypeStruct((b, n), F32),
        compiler_params=_params("arbitrary"),
        name="modulation",
    )(c, ada_w, ada_b.reshape(1, n))


def _rope_kernel(pos_ref, inv_ref, cos_ref, sin_ref):
    ang = pos_ref[...].astype(F32) * inv_ref[...]
    cos_ref[...] = jnp.cos(ang)
    sin_ref[...] = jnp.sin(ang)


def _rope_tables(positions):
    t = positions.size
    half = MLA_ROPE // 2
    per_row = LANES // half
    inv = 1.0 / (ROPE_THETA ** (jnp.arange(0, MLA_ROPE, 2, dtype=F32) / MLA_ROPE))
    pos_rep = jnp.repeat(positions.reshape(t), half).reshape(t // per_row, LANES)
    inv_t = jnp.tile(inv, per_row).reshape(1, LANES)
    rows = t // per_row
    blk = min(rows, 512)
    cos, sin = pl.pallas_call(
        _rope_kernel,
        grid=(rows // blk,),
        in_specs=[pl.BlockSpec((blk, LANES), lambda i: (i, 0)),
                  pl.BlockSpec((1, LANES), lambda i: (0, 0))],
        out_specs=[pl.BlockSpec((blk, LANES), lambda i: (i, 0))] * 2,
        out_shape=[jax.ShapeDtypeStruct((rows, LANES), F32)] * 2,
        compiler_params=_params("arbitrary"),
        name="rope_tables",
    )(pos_rep, inv_t)
    return cos.reshape(t, half), sin.reshape(t, half)


def _inproj_kernel(x_ref, mod_ref, gpre_ref, win_ref, qn_ref, wq_ref, kvn_ref, wk_ref, wvt_ref,
                   cos_ref, sin_ref, z_ref, xbc_ref, dt_ref, q_ref, k_ref, vt_ref):
    x = x_ref[...]
    sh = mod_ref[0, 0:1, :]
    sc = mod_ref[0, 1:2, :]
    h = (_rms(x, gpre_ref[...]) * (1.0 + sc) + sh).astype(BF16)
    u = jnp.dot(h, win_ref[...], preferred_element_type=F32)
    z_ref[...] = u[:, _C_Z:_C_XBC].astype(BF16)
    xbc_ref[...] = u[:, _C_XBC:_C_CQ].astype(BF16)
    dt_ref[...] = u[:, _C_DT:_C_END]
    cos_t = cos_ref[...]
    sin_t = sin_ref[...]
    cq = _rms(u[:, _C_CQ:_C_CKV], qn_ref[...]).astype(BF16)
    q2 = jnp.dot(cq, wq_ref[...], preferred_element_type=F32)
    scale = MLA_QK ** -0.5 * LOG2_E
    nq = MLA_HEADS * HEAD_PAD
    for hd in range(MLA_HEADS):
        a = q2[:, hd * HEAD_PAD:(hd + 1) * HEAD_PAD]
        b = q2[:, nq + hd * HEAD_PAD:nq + (hd + 1) * HEAD_PAD]
        q_ref[:, hd * HEAD_PAD:(hd + 1) * HEAD_PAD] = ((a * cos_t + b * sin_t) * scale).astype(BF16)
    ckv = _rms(u[:, _C_CKV:_C_KR], kvn_ref[...]).astype(BF16)
    kn = jnp.dot(ckv, wk_ref[...], preferred_element_type=F32)
    k_pe = u[:, _C_KR:_C_KRS] * cos_t + u[:, _C_KRS:_C_DT] * sin_t
    for hd in range(MLA_HEADS):
        k_ref[:, hd * HEAD_PAD:(hd + 1) * HEAD_PAD] = (
            kn[:, hd * HEAD_PAD:(hd + 1) * HEAD_PAD] + k_pe).astype(BF16)
    v_t = lax.dot_general(wvt_ref[...], ckv, (((1,), (1,)), ((), ())), preferred_element_type=F32)
    for s in range(vt_ref.shape[0]):
        vt_ref[s] = v_t[:, s * ATT_BLOCK:(s + 1) * ATT_BLOCK].astype(BF16)


def _inproj_weights(w_in, w_uq, w_ukv):
    d = w_in.shape[0]
    half = MLA_ROPE // 2
    o_z, o_xbc, o_dt = 0, SSD_INNER, SSD_INNER + SSD_XBC
    o_cq = o_dt + SSD_HEADS
    o_ckv = o_cq + MLA_Q_RANK
    o_kr = o_ckv + MLA_KV_RANK
    zeros = lambda n: jnp.zeros((d, n), F32)
    kr = w_in[:, o_kr:o_kr + MLA_ROPE]
    kr_blk = jnp.concatenate([zeros(MLA_NOPE), kr, zeros(HEAD_PAD - MLA_QK)], axis=1)
    krs_blk = jnp.concatenate([zeros(MLA_NOPE), -kr[:, half:], kr[:, :half], zeros(HEAD_PAD - MLA_QK)], axis=1)
    dt_blk = jnp.concatenate([w_in[:, o_dt:o_dt + SSD_HEADS], zeros(LANES - SSD_HEADS)], axis=1)
    w_in_r = jnp.concatenate([w_in[:, o_z:o_dt], w_in[:, o_cq:o_kr], kr_blk, krs_blk, dt_blk], axis=1)
    r = w_uq.shape[0]
    zq = jnp.zeros((r, MLA_HEADS, HEAD_PAD - MLA_QK), F32)
    zn = jnp.zeros((r, MLA_HEADS, MLA_NOPE), F32)
    wq_plain = jnp.concatenate([w_uq, zq], axis=2).reshape(r, MLA_HEADS * HEAD_PAD)
    wq_rot = jnp.concatenate([zn, -w_uq[:, :, MLA_NOPE + half:], w_uq[:, :, MLA_NOPE:MLA_NOPE + half], zq],
                             axis=2).reshape(r, MLA_HEADS * HEAD_PAD)
    wq2 = jnp.concatenate([wq_plain, wq_rot], axis=1)
    rk = w_ukv.shape[0]
    zk = jnp.zeros((rk, MLA_HEADS, HEAD_PAD - MLA_NOPE), F32)
    wk = jnp.concatenate([w_ukv[:, :, :MLA_NOPE], zk], axis=2).reshape(rk, MLA_HEADS * HEAD_PAD)
    wv_t = w_ukv[:, :, MLA_NOPE:].reshape(rk, MLA_INNER).T
    return w_in_r.astype(BF16), wq2.astype(BF16), wk.astype(BF16), wv_t.astype(BF16)


def _inproj(x2, mod3, gpre, w_in_r, q_norm, wq2, kv_norm, wk, wv_t, cos_t, sin_t, seq, tm):
    t, d = x2.shape
    per_b = seq // tm
    row = lambda i: (i, 0)
    const = lambda i: (0, 0)
    nqk = MLA_HEADS * HEAD_PAD
    slabs = tm // ATT_BLOCK
    return pl.pallas_call(
        _inproj_kernel,
        grid=(t // tm,),
        in_specs=[pl.BlockSpec((tm, d), row),
                  pl.BlockSpec((1, N_MOD, d), lambda i: (i // per_b, 0, 0)),
                  pl.BlockSpec((1, d), const),
                  pl.BlockSpec(w_in_r.shape, const),
                  pl.BlockSpec((1, MLA_Q_RANK), const),
                  pl.BlockSpec(wq2.shape, const),
                  pl.BlockSpec((1, MLA_KV_RANK), const),
                  pl.BlockSpec(wk.shape, const),
                  pl.BlockSpec(wv_t.shape, const),
                  pl.BlockSpec((tm, LANES), row),
                  pl.BlockSpec((tm, LANES), row)],
        out_specs=[pl.BlockSpec((tm, SSD_INNER), row),
                   pl.BlockSpec((tm, SSD_XBC), row),
                   pl.BlockSpec((tm, LANES), row),
                   pl.BlockSpec((tm, nqk), row),
                   pl.BlockSpec((tm, nqk), row),
                   pl.BlockSpec((slabs, MLA_INNER, ATT_BLOCK), lambda i: (i, 0, 0))],
        out_shape=[jax.ShapeDtypeStruct((t, SSD_INNER), BF16),
                   jax.ShapeDtypeStruct((t, SSD_XBC), BF16),
                   jax.ShapeDtypeStruct((t, LANES), F32),
                   jax.ShapeDtypeStruct((t, nqk), BF16),
                   jax.ShapeDtypeStruct((t, nqk), BF16),
                   jax.ShapeDtypeStruct((t // ATT_BLOCK, MLA_INNER, ATT_BLOCK), BF16)],
        compiler_params=_params("arbitrary"),
        name="inproj",
    )(x2, mod3, gpre, w_in_r, q_norm, wq2, kv_norm, wk, wv_t, cos_t, sin_t)


def _split3_packed(v):
    lane = lax.broadcasted_iota(jnp.int32, v.shape, 1)
    v = jnp.where(lane < SSD_HEADS, v, 0.0)
    hi = v.astype(BF16).astype(F32)
    rest = v - hi
    mid = rest.astype(BF16).astype(F32)
    lo = rest - mid
    return (hi + pltpu.roll(mid, SSD_HEADS, 1) + pltpu.roll(lo, 2 * SSD_HEADS, 1)).astype(BF16)


def _ssd_kernel(xbc_ref, z_ref, dt_ref, cw_ref, cb_ref, dtb_ref, alog_ref, dsk_ref, gn_ref,
                e_ref, y_ref, tail_ref, state_ref):
    L = SSD_CHUNK
    gw = SSD_INNER // SSD_GROUPS

    @pl.when(pl.program_id(1) == 0)
    def _():
        tail_ref[...] = jnp.zeros_like(tail_ref)
        state_ref[...] = jnp.zeros_like(state_ref)

    cur = xbc_ref[...].astype(F32)
    prev8 = tail_ref[...]
    row8 = lax.broadcasted_iota(jnp.int32, prev8.shape, 0)
    acc = cur * cw_ref[SSD_CONV - 1:SSD_CONV, :] + cb_ref[...]
    for j in range(1, SSD_CONV):
        r = pltpu.roll(cur, j, 0)
        head = jnp.where(row8 < j, pltpu.roll(prev8, j, 0), r[0:8])
        r = jnp.concatenate([head, r[8:]], axis=0)
        acc = acc + r * cw_ref[SSD_CONV - 1 - j:SSD_CONV - j, :]
    tail_ref[...] = cur[L - 8:L]
    act = _silu(acc)
    xs = act[:, :SSD_INNER]
    bm = act[:, SSD_INNER:SSD_INNER + SSD_GROUPS * SSD_STATE].astype(BF16)
    cm = act[:, SSD_INNER + SSD_GROUPS * SSD_STATE:].astype(BF16)

    dt_in = dt_ref[...] + dtb_ref[...]
    dt = jnp.maximum(dt_in, 0.0) + jnp.log(1.0 + jnp.exp(-jnp.abs(dt_in)))
    adt = dt * (-jnp.exp(alog_ref[...]))
    ri = lax.broadcasted_iota(jnp.int32, (L, L), 0)
    ci = lax.broadcasted_iota(jnp.int32, (L, L), 1)
    causal = ci <= ri
    acs_p = jnp.dot(causal.astype(BF16), _split3_packed(adt), preferred_element_type=F32)
    a_cs = (acs_p + pltpu.roll(acs_p, LANES - SSD_HEADS, 1)
            + pltpu.roll(acs_p, LANES - 2 * SSD_HEADS, 1))
    a_cs_t = a_cs.T
    cs_all = jnp.dot(_split3_packed(a_cs), e_ref[...], preferred_element_type=F32)
    cs64 = cs_all[:, :SSD_INNER]
    cs128 = cs_all[:, SSD_INNER:]
    dt64 = jnp.dot(_split3_packed(dt), e_ref[:, :SSD_INNER], preferred_element_type=F32)

    xd = xs * dt64
    xd_b = xd.astype(BF16)
    last = cs64[L - 1:L, :]
    xdw = (xd * jnp.exp(last - cs64)).astype(BF16)
    chunk_decay = jnp.exp(last)
    in_decay = jnp.exp(cs64)

    lane = lax.broadcasted_iota(jnp.int32, (L, LANES), 1)
    lo = lane < SSD_HEAD_DIM
    zero_b = jnp.zeros((L, LANES), BF16)
    y_parts = []
    new_states = []
    for g in range(SSD_GROUPS):
        bg = bm[:, g * SSD_STATE:(g + 1) * SSD_STATE]
        cg = cm[:, g * SSD_STATE:(g + 1) * SSD_STATE]
        cb = lax.dot_general(cg, bg, (((1,), (1,)), ((), ())), preferred_element_type=F32)
        prev = state_ref[:, g * gw:(g + 1) * gw]
        y_off = jnp.dot(cg, prev.astype(BF16), preferred_element_type=F32) * in_decay[:, g * gw:(g + 1) * gw]
        st = lax.dot_general(bg, xdw[:, g * gw:(g + 1) * gw], (((0,), (0,)), ((), ())),
                             preferred_element_type=F32)
        new_states.append(prev * chunk_decay[:, g * gw:(g + 1) * gw] + st)
        heads_per_group = SSD_HEADS // SSD_GROUPS
        for pair in range(heads_per_group // 2):
            h0 = g * heads_per_group + 2 * pair
            blk = xd_b[:, h0 * SSD_HEAD_DIM:(h0 + 2) * SSD_HEAD_DIM]
            y_pair = None
            for k in range(2):
                hh = h0 + k
                diff = cs128[:, hh * LANES:(hh + 1) * LANES] - a_cs_t[hh:hh + 1, :]
                decay = jnp.where(causal, jnp.exp(diff), 0.0)
                m = (cb * decay).astype(BF16)
                rhs = jnp.where(lo, blk, zero_b) if k == 0 else jnp.where(lo, zero_b, blk)
                part = jnp.dot(m, rhs, preferred_element_type=F32)
                y_pair = part if y_pair is None else y_pair + part
            c0 = (2 * pair) * SSD_HEAD_DIM
            y_parts.append(y_pair + y_off[:, c0:c0 + LANES])
    state_ref[...] = jnp.concatenate(new_states, axis=1)
    y = jnp.concatenate(y_parts, axis=1) + dsk_ref[...] * xs
    gated = y * _silu(z_ref[...].astype(F32))
    y_ref[...] = _rms(gated, gn_ref[...]).astype(BF16)


def _ssd(xbc, z, dt_raw, conv_w, conv_b, dt_bias, a_log, d_skip, ssd_norm, batch, seq):
    t = xbc.shape[0]
    nc = seq // SSD_CHUNK
    L = SSD_CHUNK
    row = lambda b, c: (b * nc + c, 0)
    const = lambda b, c: (0, 0)
    pad = lambda v: jnp.concatenate([v.reshape(1, -1), jnp.zeros((1, LANES - v.size), F32)], axis=1)
    head_of_lane = jnp.concatenate([jnp.arange(SSD_INNER) // SSD_HEAD_DIM,
                                    jnp.arange(SSD_HEADS * LANES) // LANES])
    rows = jnp.arange(LANES)[:, None]
    expand = ((rows < 3 * SSD_HEADS) & (rows % SSD_HEADS == head_of_lane[None, :])).astype(BF16)
    dsk = jnp.repeat(d_skip.astype(F32), SSD_HEAD_DIM).reshape(1, SSD_INNER)
    return pl.pallas_call(
        _ssd_kernel,
        grid=(batch, nc),
        in_specs=[pl.BlockSpec((L, SSD_XBC), row),
                  pl.BlockSpec((L, SSD_INNER), row),
                  pl.BlockSpec((L, LANES), row),
                  pl.BlockSpec((SSD_CONV, SSD_XBC), const),
                  pl.BlockSpec((1, SSD_XBC), const),
                  pl.BlockSpec((1, LANES), const),
                  pl.BlockSpec((1, LANES), const),
                  pl.BlockSpec((1, SSD_INNER), const),
                  pl.BlockSpec((1, SSD_INNER), const),
                  pl.BlockSpec(expand.shape, const)],
        out_specs=pl.BlockSpec((L, SSD_INNER), row),
        out_shape=jax.ShapeDtypeStruct((t, SSD_INNER), BF16),
        scratch_shapes=[pltpu.VMEM((8, SSD_XBC), F32),
                        pltpu.VMEM((SSD_STATE, SSD_INNER), F32)],
        compiler_params=_params("arbitrary", "arbitrary"),
        name="ssd",
    )(xbc, z, dt_raw, conv_w, conv_b.reshape(1, -1), pad(dt_bias), pad(a_log), dsk,
      ssd_norm.reshape(1, -1), expand)


def _attn_kernel(q_ref, k_ref, vt_ref, gn_ref, o_ref, s_ref):
    blk = ATT_BLOCK
    qi = pl.program_id(1)
    causal = (lax.broadcasted_iota(jnp.int32, (blk, blk), 0)
              <= lax.broadcasted_iota(jnp.int32, (blk, blk), 1))
    dims = (((1,), (1,)), ((), ()))
    qs = [q_ref[:, hd * HEAD_PAD:(hd + 1) * HEAD_PAD] for hd in range(MLA_HEADS)]

    def step(j, carry, masked):
        rows = pl.ds(pl.multiple_of(j * blk, blk), blk)
        m_new = []
        for hd in range(MLA_HEADS):
            kk = k_ref[rows, hd * HEAD_PAD:(hd + 1) * HEAD_PAD]
            s = lax.dot_general(kk, qs[hd], dims, preferred_element_type=F32)
            if masked:
                s = jnp.where(causal, s, -jnp.inf)
            s_ref[hd] = s
            m_new.append(jnp.maximum(carry[hd][0], jnp.max(s, axis=0, keepdims=True)))
        new = []
        for hd in range(MLA_HEADS):
            m_old, l_old, acc = carry[hd]
            p = jnp.exp2(s_ref[hd] - m_new[hd])
            alpha = jnp.exp2(m_old - m_new[hd])
            l_new = alpha * l_old + jnp.sum(p, axis=0, keepdims=True)
            vt = vt_ref[j, hd * MLA_V:(hd + 1) * MLA_V, :]
            acc = acc * alpha + jnp.dot(vt, p.astype(BF16), preferred_element_type=F32)
            new.append((m_new[hd], l_new, acc))
        return tuple(new)

    neg = jnp.full((1, blk), -jnp.inf, F32)
    init = tuple((neg, jnp.zeros((1, blk), F32), jnp.zeros((MLA_V, blk), F32))
                 for _ in range(MLA_HEADS))
    carry = lax.fori_loop(0, qi, functools.partial(step, masked=False), init)
    final = step(qi, carry, True)
    y_t = jnp.concatenate([acc / l for _, l, acc in final], axis=0)
    o_ref[...] = _rms(y_t.T, gn_ref[...]).astype(BF16)


def _attention(q, k, v_t, gn, batch, seq):
    t = q.shape[0]
    blk = ATT_BLOCK
    nq = seq // blk
    nqk = MLA_HEADS * HEAD_PAD
    return pl.pallas_call(
        _attn_kernel,
        grid=(batch, nq),
        in_specs=[pl.BlockSpec((blk, nqk), lambda b, i: (b * nq + i, 0)),
                  pl.BlockSpec((seq, nqk), lambda b, i: (b, 0)),
                  pl.BlockSpec((nq, MLA_INNER, blk), lambda b, i: (b, 0, 0)),
                  pl.BlockSpec((1, MLA_INNER), lambda b, i: (0, 0))],
        out_specs=pl.BlockSpec((blk, MLA_INNER), lambda b, i: (b * nq + i, 0)),
        out_shape=jax.ShapeDtypeStruct((t, MLA_INNER), BF16),
        scratch_shapes=[pltpu.VMEM((MLA_HEADS, blk, blk), F32)],
        compiler_params=_params("arbitrary", "arbitrary"),
        name="attention",
    )(q, k, v_t, gn)


def _outproj_kernel(ys_ref, ym_ref, x_ref, mod_ref, wtop_ref, wbot_ref, gpm_ref, gpf_ref,
                    wrh_ref, wrl_ref, br_ref, x1_ref, xs_ref, col_ref, wslot_ref, cnt_ref, ce_ref):
    y = (jnp.dot(ys_ref[...], wtop_ref[...], preferred_element_type=F32)
         + jnp.dot(ym_ref[...], wbot_ref[...], preferred_element_type=F32))
    g1 = mod_ref[0, 2:3, :]
    sh2 = mod_ref[0, 3:4, :]
    sc2 = mod_ref[0, 4:5, :]
    x1 = x_ref[...] + g1 * _rms(y, gpm_ref[...])
    x1_ref[...] = x1
    h2 = _rms(x1, gpf_ref[...]) * (1.0 + sc2) + sh2
    tm = h2.shape[0]
    h_hi = h2.astype(BF16)
    h_lo = (h2 - h_hi.astype(F32)).astype(BF16)
    logits = (jnp.dot(h_hi, wrh_ref[...], preferred_element_type=F32)
              + jnp.dot(h_lo, wrh_ref[...], preferred_element_type=F32)
              + jnp.dot(h_hi, wrl_ref[...], preferred_element_type=F32)) + br_ref[...]
    lt = logits.T
    row = lax.broadcasted_iota(jnp.int32, lt.shape, 0).astype(F32)
    ninf = -jnp.inf
    big = 1e9
    is_g = row < MOE_GROUPS
    gl = jnp.where(is_g, lt, ninf)
    gmax = jnp.max(gl, axis=0, keepdims=True)
    gsum = jnp.sum(jnp.where(is_g, jnp.exp(lt - gmax), 0.0), axis=0, keepdims=True)
    g_w = 1.0 / gsum
    g_idx = jnp.min(jnp.where(gl == gmax, row, big), axis=0, keepdims=True)
    first = MOE_GROUPS + MOE_EXPERTS_PER_GROUP * g_idx
    el = jnp.where(row >= first, jnp.where(row < first + MOE_EXPERTS_PER_GROUP, lt, ninf), ninf)
    m1 = jnp.max(el, axis=0, keepdims=True)
    i1 = jnp.min(jnp.where(el == m1, row, big), axis=0, keepdims=True)
    el2 = jnp.where(row == i1, ninf, el)
    m2 = jnp.max(el2, axis=0, keepdims=True)
    i2 = jnp.min(jnp.where(el2 == m2, row, big), axis=0, keepdims=True)
    r = jnp.exp(m2 - m1)
    w1 = g_w / (1.0 + r)
    w2 = g_w * r / (1.0 + r)
    sel1 = row == i1
    sel2 = row == i2
    both = jnp.where(sel1, 1.0, jnp.where(sel2, 1.0, 0.0))
    cnt16 = jnp.floor((jnp.sum(both, axis=1, keepdims=True) + (MOE_CHUNK - 1)) * (1.0 / MOE_CHUNK))
    er = lax.broadcasted_iota(jnp.int32, (LANES, LANES), 0)
    ec = lax.broadcasted_iota(jnp.int32, (LANES, LANES), 1)
    cnt16_b = jnp.broadcast_to(cnt16, (LANES, LANES))
    off16 = jnp.dot((ec < er).astype(BF16), cnt16_b.astype(BF16), preferred_element_type=F32)[:, 0:1]
    earlier = (lax.broadcasted_iota(jnp.int32, (tm, tm), 0)
               < lax.broadcasted_iota(jnp.int32, (tm, tm), 1)).astype(BF16)
    base = off16 * MOE_CHUNK + jnp.dot(both.astype(BF16), earlier, preferred_element_type=F32)
    lpos1 = jnp.sum(jnp.where(sel1, base, 0.0), axis=0, keepdims=True)
    lpos2 = jnp.sum(jnp.where(sel2, base, 0.0), axis=0, keepdims=True)
    srow = lax.broadcasted_iota(jnp.int32, (MOE_SLOTS, tm), 0)
    hit1 = srow == lpos1.astype(jnp.int32)
    hit2 = srow == lpos2.astype(jnp.int32)
    perm = jnp.where(hit1, 1.0, jnp.where(hit2, 1.0, 0.0)).astype(BF16)
    xs_ref[...] = jnp.dot(perm, h_hi, preferred_element_type=F32).astype(BF16)
    wslot = jnp.sum(jnp.where(hit1, w1, jnp.where(hit2, w2, 0.0)), axis=1, keepdims=True)
    wslot_ref[...] = jnp.broadcast_to(wslot, wslot_ref.shape)
    col_ref[...] = jnp.where(row == 0, lpos1, jnp.where(row == 1, lpos2, 0.0)).T
    cnt_ref[...] = cnt16_b
    chunk = lax.broadcasted_iota(jnp.int32, (LANES, LANES), 1).astype(F32)
    ce = jnp.sum(jnp.where(off16 + cnt16 <= chunk, 1.0, 0.0), axis=0, keepdims=True) - MOE_GROUPS
    ce_ref[...] = jnp.broadcast_to(ce, ce_ref.shape)


def _outproj(y_ssd, y_mla, x2, mod3, w_top, w_bot, gpm, gpf, wr_hi, wr_lo, b_router, seq, tm):
    t, d = x2.shape
    per_b = seq // tm
    n_tok_tiles = t // tm
    row = lambda i: (i, 0)
    const = lambda i: (0, 0)
    return pl.pallas_call(
        _outproj_kernel,
        grid=(n_tok_tiles,),
        in_specs=[pl.BlockSpec((tm, SSD_INNER), row),
                  pl.BlockSpec((tm, MLA_INNER), row),
                  pl.BlockSpec((tm, d), row),
                  pl.BlockSpec((1, N_MOD, d), lambda i: (i // per_b, 0, 0)),
                  pl.BlockSpec(w_top.shape, const),
                  pl.BlockSpec(w_bot.shape, const),
                  pl.BlockSpec((1, d), const),
                  pl.BlockSpec((1, d), const),
                  pl.BlockSpec(wr_hi.shape, const),
                  pl.BlockSpec(wr_lo.shape, const),
                  pl.BlockSpec((1, LANES), const)],
        out_specs=[pl.BlockSpec((tm, d), row),
                   pl.BlockSpec((MOE_SLOTS, d), row),
                   pl.BlockSpec((tm, LANES), row),
                   pl.BlockSpec((MOE_SLOTS, LANES), row),
                   pl.BlockSpec((LANES, LANES), row),
                   pl.BlockSpec((8, LANES), row)],
        out_shape=[jax.ShapeDtypeStruct((t, d), F32),
                   jax.ShapeDtypeStruct((n_tok_tiles * MOE_SLOTS, d), BF16),
                   jax.ShapeDtypeStruct((t, LANES), F32),
                   jax.ShapeDtypeStruct((n_tok_tiles * MOE_SLOTS, LANES), F32),
                   jax.ShapeDtypeStruct((n_tok_tiles * LANES, LANES), F32),
                   jax.ShapeDtypeStruct((n_tok_tiles * 8, LANES), F32)],
        compiler_params=_params("arbitrary"),
        name="outproj_router",
    )(y_ssd, y_mla, x2, mod3, w_top, w_bot, gpm, gpf, wr_hi, wr_lo, b_router)


def _moe_rows(n_tok_tiles):
    rows = n_tok_tiles * MOE_SLOTS + MOE_EXPERTS * (MOE_TILE - MOE_CHUNK)
    return (rows + MOE_TILE - 1) // MOE_TILE * MOE_TILE


def _moe_plan(cnt_tiles, chunk_expert, n_tok_tiles, n_rows):
    chunks_per_tile = MOE_SLOTS // MOE_CHUNK
    chunks_per_rows = MOE_TILE // MOE_CHUNK
    cnt = cnt_tiles.reshape(n_tok_tiles, LANES, LANES)[:, MOE_GROUPS:MOE_GROUPS + MOE_EXPERTS, 0]
    cnt = cnt.astype(jnp.int32)
    ce = chunk_expert.reshape(n_tok_tiles, 8, LANES)[:, 0, :chunks_per_tile].astype(jnp.int32)
    used = jnp.sum(cnt, axis=1)
    total = jnp.sum(cnt, axis=0)
    padded = (total + chunks_per_rows - 1) // chunks_per_rows * chunks_per_rows
    e_end = jnp.cumsum(padded)
    e_start = e_end - padded
    run_global = e_start[None, :] + jnp.cumsum(cnt, axis=0) - cnt
    run_local = jnp.cumsum(cnt, axis=1) - cnt
    experts = jnp.arange(MOE_EXPERTS, dtype=jnp.int32)
    shift = jnp.sum(jnp.where(ce[:, :, None] == experts, (run_global - run_local)[:, None, :], 0), axis=-1)
    dst = shift + jnp.arange(chunks_per_tile, dtype=jnp.int32)[None, :]
    n_row_tiles = n_rows // MOE_TILE
    n_used = e_end[-1] // chunks_per_rows
    j = jnp.minimum(jnp.arange(n_row_tiles, dtype=jnp.int32), jnp.maximum(n_used - 1, 0))
    tile_expert = jnp.sum((e_end[None, :] // chunks_per_rows <= j[:, None]).astype(jnp.int32), axis=1)
    tile_expert = jnp.minimum(tile_expert, MOE_EXPERTS - 1)
    i32 = lambda v: v.astype(jnp.int32)
    return (i32(dst.reshape(-1)), i32(used), i32(e_start + total), i32(padded - total),
            i32(n_used.reshape(1)), i32(tile_expert))


def _chunk_rows(chunk):
    return pl.ds(pl.multiple_of(chunk * MOE_CHUNK, MOE_CHUNK), MOE_CHUNK)


def _run_copies(*groups):
    for wait in (False, True):
        for lo, hi, make_copy in groups:
            def body(k, carry, wait=wait, make_copy=make_copy):
                cp = make_copy(k)
                cp.wait() if wait else cp.start()
                return carry
            lax.fori_loop(lo, hi, body, 0)


def _scatter_chunks_kernel(dst_ref, used_ref, pad_start_ref, pad_n_ref, nused_ref, src_ref, o_ref,
                           zero_ref, sem):
    i = pl.program_id(0)
    base = i * (MOE_SLOTS // MOE_CHUNK)

    @pl.when(i == 0)
    def _():
        zero_ref[...] = jnp.zeros_like(zero_ref)
        pads = [(0, pad_n_ref[e], lambda c, e=e: pltpu.make_async_copy(
            zero_ref.at[pl.ds(0, MOE_CHUNK), :], o_ref.at[_chunk_rows(pad_start_ref[e] + c), :], sem.at[1]))
            for e in range(MOE_EXPERTS)]
        tail = (nused_ref[0], o_ref.shape[0] // MOE_TILE, lambda j: pltpu.make_async_copy(
            zero_ref, o_ref.at[pl.ds(pl.multiple_of(j * MOE_TILE, MOE_TILE), MOE_TILE), :], sem.at[1]))
        _run_copies(tail, *pads)

    _run_copies((0, used_ref[i], lambda c: pltpu.make_async_copy(
        src_ref.at[_chunk_rows(c), :], o_ref.at[_chunk_rows(dst_ref[base + c]), :], sem.at[0])))


def _scatter_chunks(xs_local, dst, used, pad_start, pad_n, n_used, n_rows):
    d = xs_local.shape[1]
    return pl.pallas_call(
        _scatter_chunks_kernel,
        grid_spec=pltpu.PrefetchScalarGridSpec(
            num_scalar_prefetch=5,
            grid=(used.shape[0],),
            in_specs=[pl.BlockSpec((MOE_SLOTS, d), lambda i, *_: (i, 0))],
            out_specs=pl.BlockSpec(memory_space=pl.ANY),
            scratch_shapes=[pltpu.VMEM((MOE_TILE, d), BF16), pltpu.SemaphoreType.DMA((2,))]),
        out_shape=jax.ShapeDtypeStruct((n_rows, d), BF16),
        compiler_params=_params("arbitrary"),
        name="moe_scatter_chunks",
    )(dst, used, pad_start, pad_n, n_used, xs_local)


def _experts_kernel(te_ref, nt_ref, xs_ref, wg_ref, wu_ref, wd_ref, o_ref):
    used = pl.program_id(0) < nt_ref[0]

    @pl.when(used)
    def _():
        x = xs_ref[...]
        gate = jnp.dot(x, wg_ref[0].astype(BF16), preferred_element_type=F32)
        up = jnp.dot(x, wu_ref[0].astype(BF16), preferred_element_type=F32)
        hid = (_silu(gate) * up).astype(BF16)
        o_ref[...] = jnp.dot(hid, wd_ref[0].astype(BF16), preferred_element_type=F32).astype(BF16)

    @pl.when(jnp.logical_not(used))
    def _():
        o_ref[...] = jnp.zeros_like(o_ref)


def _experts(xs, tile_expert, n_used, w_gate, w_up, w_down):
    n_rows, d = xs.shape
    tile = MOE_TILE
    by_expert = lambda j, te, nt: (te[j], 0, 0)
    return pl.pallas_call(
        _experts_kernel,
        grid_spec=pltpu.PrefetchScalarGridSpec(
            num_scalar_prefetch=2,
            grid=(n_rows // tile,),
            in_specs=[pl.BlockSpec((tile, d), lambda j, te, nt: (jnp.maximum(jnp.minimum(j, nt[0] - 1), 0), 0)),
                      pl.BlockSpec((1, d, MOE_FF), by_expert),
                      pl.BlockSpec((1, d, MOE_FF), by_expert),
                      pl.BlockSpec((1, MOE_FF, d), by_expert)],
            out_specs=pl.BlockSpec((tile, d), lambda j, te, nt: (j, 0))),
        out_shape=jax.ShapeDtypeStruct((n_rows, d), BF16),
        compiler_params=_params("arbitrary"),
        name="moe_experts",
    )(tile_expert, n_used, xs, w_gate, w_up, w_down)


def _combine_kernel(dst_ref, used_ref, wslot_ref, col_ref, x1_ref, mod_ref, gpost_ref, ys_ref, o_ref,
                    ysl_ref, sem):
    i = pl.program_id(0)
    chunks = MOE_SLOTS // MOE_CHUNK
    slot = lax.rem(i, 2)

    def fetch(tile, buf, wait):
        base = tile * chunks
        used = used_ref[tile]

        def copy(c, carry):
            cp = pltpu.make_async_copy(ys_ref.at[_chunk_rows(dst_ref[base + c]), :],
                                       ysl_ref.at[buf, _chunk_rows(c), :], sem.at[buf])
            cp.wait() if wait else cp.start()
            return carry

        def clear(c, carry):
            ysl_ref[buf, _chunk_rows(c), :] = jnp.zeros((MOE_CHUNK, ysl_ref.shape[2]), BF16)
            return carry

        lax.fori_loop(0, used, copy, 0)
        if not wait:
            lax.fori_loop(used, chunks, clear, 0)

    @pl.when(i == 0)
    def _():
        fetch(0, 0, False)

    @pl.when(i + 1 < pl.num_programs(0))
    def _():
        fetch(i + 1, 1 - slot, False)

    col = col_ref[...]
    tm = col.shape[0]
    lane = lax.broadcasted_iota(jnp.int32, (tm, MOE_SLOTS), 1)
    hit = jnp.where(lane == col[:, 0:1].astype(jnp.int32), 1.0,
                    jnp.where(lane == col[:, 1:2].astype(jnp.int32), 1.0, 0.0)).astype(BF16)
    fetch(i, slot, True)
    ysw = (ysl_ref[slot].astype(F32) * wslot_ref[:, 0:1]).astype(BF16)
    y = jnp.dot(hit, ysw, preferred_element_type=F32)
    g2 = mod_ref[0, 5:6, :]
    o_ref[...] = x1_ref[...] + g2 * _rms(y, gpost_ref[...])


def _combine(ys, dst, used, wslot, col, x1, mod3, gpost, seq, tm):
    t, d = x1.shape
    per_b = seq // tm
    row = lambda i, *_: (i, 0)
    return pl.pallas_call(
        _combine_kernel,
        grid_spec=pltpu.PrefetchScalarGridSpec(
            num_scalar_prefetch=2,
            grid=(t // tm,),
            in_specs=[pl.BlockSpec((MOE_SLOTS, LANES), row),
                      pl.BlockSpec((tm, LANES), row),
                      pl.BlockSpec((tm, d), row),
                      pl.BlockSpec((1, N_MOD, d), lambda i, *_: (i // per_b, 0, 0)),
                      pl.BlockSpec((1, d), lambda i, *_: (0, 0)),
                      pl.BlockSpec(memory_space=pl.ANY)],
            out_specs=pl.BlockSpec((tm, d), row),
            scratch_shapes=[pltpu.VMEM((2, MOE_SLOTS, d), BF16), pltpu.SemaphoreType.DMA((2,))]),
        out_shape=jax.ShapeDtypeStruct((t, d), F32),
        compiler_params=_params("arbitrary"),
        name="moe_combine",
    )(dst, used, wslot, col, x1, mod3, gpost, ys)


def kernel(x, c, positions, ada_w, ada_b, pre_norm_mix, post_norm_mix, pre_norm_ffn, post_norm_ffn, w_in, conv_w, conv_b, dt_bias, a_log, d_skip, ssd_norm, q_norm, w_uq, kv_norm, w_ukv, mla_out_norm, w_out, w_group_router, b_group_router, w_expert_router, b_expert_router, w_gate, w_up, w_down):
    batch, seq, d = x.shape
    t = batch * seq
    depth = ada_w.shape[0]
    tm = min(512, seq)
    half = MLA_ROPE // 2

    cos16, sin16 = _rope_tables(positions)
    ones = lambda n: jnp.ones((t, n), F32)
    zeros = lambda n: jnp.zeros((t, n), F32)
    cos_t = jnp.concatenate([ones(MLA_NOPE), cos16, cos16, ones(HEAD_PAD - MLA_QK)], axis=1)
    sin_t = jnp.concatenate([zeros(MLA_NOPE), sin16, sin16, zeros(HEAD_PAD - MLA_QK)], axis=1)

    x2 = x.reshape(t, d)
    for l in range(depth):
        mod3 = _modulation(c, ada_w[l], ada_b[l]).reshape(batch, N_MOD, d)
        w_in_r, wq2, wk, wv_t = _inproj_weights(w_in[l], w_uq[l], w_ukv[l])
        z, xbc, dt_raw, q, k, v_t = _inproj(
            x2, mod3, pre_norm_mix[l].reshape(1, d), w_in_r, q_norm[l].reshape(1, -1), wq2,
            kv_norm[l].reshape(1, -1), wk, wv_t, cos_t, sin_t, seq, tm)
        y_ssd = _ssd(xbc, z, dt_raw, conv_w[l], conv_b[l], dt_bias[l], a_log[l], d_skip[l],
                     ssd_norm[l], batch, seq)
        y_mla = _attention(q, k, v_t, mla_out_norm[l].reshape(1, -1), batch, seq)
        w_o = w_out[l].astype(BF16)
        pad_r = LANES - MOE_GROUPS - MOE_EXPERTS
        w_router = jnp.concatenate([w_group_router[l], w_expert_router[l], jnp.zeros((d, pad_r), F32)], axis=1)
        b_router = jnp.concatenate([b_group_router[l].reshape(-1), b_expert_router[l].reshape(-1),
                                    jnp.zeros((pad_r,), F32)]).reshape(1, LANES)
        wr_hi = w_router.astype(BF16)
        wr_lo = (w_router - wr_hi.astype(F32)).astype(BF16)
        x1, xs_local, col, wslot, cnt_tiles, chunk_expert = _outproj(
            y_ssd, y_mla, x2, mod3, w_o[:SSD_INNER], w_o[SSD_INNER:], post_norm_mix[l].reshape(1, d),
            pre_norm_ffn[l].reshape(1, d), wr_hi, wr_lo, b_router, seq, MOE_TOKENS)
        n_tok_tiles = t // MOE_TOKENS
        n_rows = _moe_rows(n_tok_tiles)
        dst, used, pad_start, pad_n, n_used, tile_expert = _moe_plan(cnt_tiles, chunk_expert, n_tok_tiles, n_rows)
        xs = _scatter_chunks(xs_local, dst, used, pad_start, pad_n, n_used, n_rows)
        ys = _experts(xs, tile_expert, n_used, w_gate[l], w_up[l], w_down[l])
        x2 = _combine(ys, dst, used, wslot, col, x1, mod3, post_norm_ffn[l].reshape(1, d), seq, MOE_TOKENS)
    return x2.reshape(batch, seq, d)
```

```python
import functools
import math

import jax
import jax.numpy as jnp
from jax import lax
from jax.experimental import pallas as pl
from jax.experimental.pallas import tpu as pltpu

F32 = jnp.float32
BF16 = jnp.bfloat16

D_MODEL = 1024
SSD_HEADS = 8
SSD_HEAD_DIM = 64
SSD_INNER = SSD_HEADS * SSD_HEAD_DIM
SSD_GROUPS = 2
SSD_STATE = 128
SSD_CONV = 4
SSD_CHUNK = 128
SSD_XBC = SSD_INNER + 2 * SSD_GROUPS * SSD_STATE
MLA_HEADS = 8
MLA_NOPE = 64
MLA_ROPE = 32
MLA_QK = MLA_NOPE + MLA_ROPE
MLA_V = 64
MLA_Q_RANK = 256
MLA_KV_RANK = 128
MLA_INNER = MLA_HEADS * MLA_V
ROPE_THETA = 10000.0
MOE_GROUPS = 4
MOE_EXPERTS_PER_GROUP = 8
MOE_EXPERTS = MOE_GROUPS * MOE_EXPERTS_PER_GROUP
MOE_FF = 256
N_MOD = 6
EPS = 1e-6
LOG2_E = math.log2(math.e)

LANES = 128
HEAD_PAD = 128
ATT_BLOCK = 256
MLA_V_ROWS = MLA_V + 16
MOE_TOKENS = 512
MOE_CHUNK = 16
MOE_TILE = 512
MOE_SLOTS = 2 * MOE_TOKENS + MOE_EXPERTS * MOE_CHUNK
VMEM_LIMIT = 48 * 1024 * 1024

_C_Z = 0
_C_XBC = _C_Z + SSD_INNER
_C_CQ = _C_XBC + SSD_XBC
_C_CKV = _C_CQ + MLA_Q_RANK
_C_KR = _C_CKV + MLA_KV_RANK
_C_KRS = _C_KR + LANES
_C_DT = _C_KRS + LANES
_C_END = _C_DT + LANES


def _silu(v):
    return v * (1.0 / (1.0 + jnp.exp(-v)))


def _rms(v, gain):
    return v * lax.rsqrt(jnp.mean(v * v, axis=-1, keepdims=True) + EPS) * gain


def _params(*sem, flags=None):
    return pltpu.CompilerParams(dimension_semantics=sem, vmem_limit_bytes=VMEM_LIMIT, flags=flags)


def _mod_kernel(c_ref, w_ref, b_ref, o_ref):
    act = _silu(c_ref[...])
    a_hi = act.astype(BF16)
    a_lo = (act - a_hi.astype(F32)).astype(BF16)
    w = w_ref[...]
    w_hi = w.astype(BF16)
    w_lo = (w - w_hi.astype(F32)).astype(BF16)
    o_ref[...] = (jnp.dot(a_hi, w_hi, preferred_element_type=F32)
                  + jnp.dot(a_lo, w_hi, preferred_element_type=F32)
                  + jnp.dot(a_hi, w_lo, preferred_element_type=F32)) + b_ref[...]


def _modulation(c, ada_w, ada_b):
    b, d = c.shape
    n = ada_w.shape[1]
    return pl.pallas_call(
        _mod_kernel,
        grid=(n // d,),
        in_specs=[pl.BlockSpec((b, d), lambda j: (0, 0)),
                  pl.BlockSpec((d, d), lambda j: (0, j)),
                  pl.BlockSpec((1, d), lambda j: (0, j))],
        out_specs=pl.BlockSpec((b, d), lambda j: (0, j)),
        out_shape=jax.ShapeDtypeStruct((b, n), F32),
        compiler_params=_params("arbitrary"),
        name="modulation",
    )(c, ada_w, ada_b.reshape(1, n))


def _rope_kernel(pos_ref, inv_ref, cos_ref, sin_ref):
    ang = pos_ref[...].astype(F32) * inv_ref[...]
    cos_ref[...] = jnp.cos(ang)
    sin_ref[...] = jnp.sin(ang)


def _rope_tables(positions):
    t = positions.size
    half = MLA_ROPE // 2
    per_row = LANES // half
    inv = 1.0 / (ROPE_THETA ** (jnp.arange(0, MLA_ROPE, 2, dtype=F32) / MLA_ROPE))
    pos_rep = jnp.repeat(positions.reshape(t), half).reshape(t // per_row, LANES)
    inv_t = jnp.tile(inv, per_row).reshape(1, LANES)
    rows = t // per_row
    blk = min(rows, 512)
    cos, sin = pl.pallas_call(
        _rope_kernel,
        grid=(rows // blk,),
        in_specs=[pl.BlockSpec((blk, LANES), lambda i: (i, 0)),
                  pl.BlockSpec((1, LANES), lambda i: (0, 0))],
        out_specs=[pl.BlockSpec((blk, LANES), lambda i: (i, 0))] * 2,
        out_shape=[jax.ShapeDtypeStruct((rows, LANES), F32)] * 2,
        compiler_params=_params("arbitrary"),
        name="rope_tables",
    )(pos_rep, inv_t)
    return cos.reshape(t, half), sin.reshape(t, half)


def _inproj_kernel(x_ref, mod_ref, gpre_ref, win_ref, qn_ref, wq_ref, kvn_ref, wk_ref, wvt_ref,
                   cos_ref, sin_ref, z_ref, xbc_ref, dt_ref, q_ref, k_ref, vt_ref):
    x = x_ref[...]
    sh = mod_ref[0, 0:1, :]
    sc = mod_ref[0, 1:2, :]
    h = (_rms(x, gpre_ref[...]) * (1.0 + sc) + sh).astype(BF16)
    u = jnp.dot(h, win_ref[...], preferred_element_type=F32)
    z_ref[...] = u[:, _C_Z:_C_XBC].astype(BF16)
    xbc_ref[...] = u[:, _C_XBC:_C_CQ].astype(BF16)
    dt_ref[...] = u[:, _C_DT:_C_END]
    cos_t = cos_ref[...]
    sin_t = sin_ref[...]
    cq = _rms(u[:, _C_CQ:_C_CKV], qn_ref[...]).astype(BF16)
    q2 = jnp.dot(cq, wq_ref[...], preferred_element_type=F32)
    scale = MLA_QK ** -0.5 * LOG2_E
    nq = MLA_HEADS * HEAD_PAD
    for hd in range(MLA_HEADS):
        a = q2[:, hd * HEAD_PAD:(hd + 1) * HEAD_PAD]
        b = q2[:, nq + hd * HEAD_PAD:nq + (hd + 1) * HEAD_PAD]
        q_ref[:, hd * HEAD_PAD:(hd + 1) * HEAD_PAD] = ((a * cos_t + b * sin_t) * scale).astype(BF16)
    ckv = _rms(u[:, _C_CKV:_C_KR], kvn_ref[...]).astype(BF16)
    kn = jnp.dot(ckv, wk_ref[...], preferred_element_type=F32)
    k_pe = u[:, _C_KR:_C_KRS] * cos_t + u[:, _C_KRS:_C_DT] * sin_t
    for hd in range(MLA_HEADS):
        k_ref[:, hd * HEAD_PAD:(hd + 1) * HEAD_PAD] = (
            kn[:, hd * HEAD_PAD:(hd + 1) * HEAD_PAD] + k_pe).astype(BF16)
    v_t = lax.dot_general(wvt_ref[...], ckv, (((1,), (1,)), ((), ())), preferred_element_type=F32)
    head_row = lax.rem(lax.broadcasted_iota(jnp.int32, (v_t.shape[0], 1), 0), MLA_V_ROWS)
    v_t = v_t + jnp.where(head_row == MLA_V, 1.0, 0.0)
    for s in range(vt_ref.shape[0]):
        vt_ref[s] = v_t[:, s * ATT_BLOCK:(s + 1) * ATT_BLOCK].astype(BF16)


def _inproj_weights(w_in, w_uq, w_ukv):
    d = w_in.shape[0]
    half = MLA_ROPE // 2
    o_z, o_xbc, o_dt = 0, SSD_INNER, SSD_INNER + SSD_XBC
    o_cq = o_dt + SSD_HEADS
    o_ckv = o_cq + MLA_Q_RANK
    o_kr = o_ckv + MLA_KV_RANK
    zeros = lambda n: jnp.zeros((d, n), F32)
    kr = w_in[:, o_kr:o_kr + MLA_ROPE]
    kr_blk = jnp.concatenate([zeros(MLA_NOPE), kr, zeros(HEAD_PAD - MLA_QK)], axis=1)
    krs_blk = jnp.concatenate([zeros(MLA_NOPE), -kr[:, half:], kr[:, :half], zeros(HEAD_PAD - MLA_QK)], axis=1)
    dt_blk = jnp.concatenate([w_in[:, o_dt:o_dt + SSD_HEADS], zeros(LANES - SSD_HEADS)], axis=1)
    w_in_r = jnp.concatenate([w_in[:, o_z:o_dt], w_in[:, o_cq:o_kr], kr_blk, krs_blk, dt_blk], axis=1)
    r = w_uq.shape[0]
    zq = jnp.zeros((r, MLA_HEADS, HEAD_PAD - MLA_QK), F32)
    zn = jnp.zeros((r, MLA_HEADS, MLA_NOPE), F32)
    wq_plain = jnp.concatenate([w_uq, zq], axis=2).reshape(r, MLA_HEADS * HEAD_PAD)
    wq_rot = jnp.concatenate([zn, -w_uq[:, :, MLA_NOPE + half:], w_uq[:, :, MLA_NOPE:MLA_NOPE + half], zq],
                             axis=2).reshape(r, MLA_HEADS * HEAD_PAD)
    wq2 = jnp.concatenate([wq_plain, wq_rot], axis=1)
    rk = w_ukv.shape[0]
    zk = jnp.zeros((rk, MLA_HEADS, HEAD_PAD - MLA_NOPE), F32)
    wk = jnp.concatenate([w_ukv[:, :, :MLA_NOPE], zk], axis=2).reshape(rk, MLA_HEADS * HEAD_PAD)
    zv = jnp.zeros((rk, MLA_HEADS, MLA_V_ROWS - MLA_V), F32)
    wv_t = jnp.concatenate([w_ukv[:, :, MLA_NOPE:], zv], axis=2).reshape(rk, MLA_HEADS * MLA_V_ROWS).T
    return w_in_r.astype(BF16), wq2.astype(BF16), wk.astype(BF16), wv_t.astype(BF16)


def _inproj(x2, mod3, gpre, w_in_r, q_norm, wq2, kv_norm, wk, wv_t, cos_t, sin_t, seq, tm):
    t, d = x2.shape
    per_b = seq // tm
    row = lambda i: (i, 0)
    const = lambda i: (0, 0)
    nqk = MLA_HEADS * HEAD_PAD
    slabs = tm // ATT_BLOCK
    return pl.pallas_call(
        _inproj_kernel,
        grid=(t // tm,),
        in_specs=[pl.BlockSpec((tm, d), row),
                  pl.BlockSpec((1, N_MOD, d), lambda i: (i // per_b, 0, 0)),
                  pl.BlockSpec((1, d), const),
                  pl.BlockSpec(w_in_r.shape, const),
                  pl.BlockSpec((1, MLA_Q_RANK), const),
                  pl.BlockSpec(wq2.shape, const),
                  pl.BlockSpec((1, MLA_KV_RANK), const),
                  pl.BlockSpec(wk.shape, const),
                  pl.BlockSpec(wv_t.shape, const),
                  pl.BlockSpec((tm, LANES), row),
                  pl.BlockSpec((tm, LANES), row)],
        out_specs=[pl.BlockSpec((tm, SSD_INNER), row),
                   pl.BlockSpec((tm, SSD_XBC), row),
                   pl.BlockSpec((tm, LANES), row),
                   pl.BlockSpec((tm, nqk), row),
                   pl.BlockSpec((tm, nqk), row),
                   pl.BlockSpec((slabs, MLA_HEADS * MLA_V_ROWS, ATT_BLOCK), lambda i: (i, 0, 0))],
        out_shape=[jax.ShapeDtypeStruct((t, SSD_INNER), BF16),
                   jax.ShapeDtypeStruct((t, SSD_XBC), BF16),
                   jax.ShapeDtypeStruct((t, LANES), F32),
                   jax.ShapeDtypeStruct((t, nqk), BF16),
                   jax.ShapeDtypeStruct((t, nqk), BF16),
                   jax.ShapeDtypeStruct((t // ATT_BLOCK, MLA_HEADS * MLA_V_ROWS, ATT_BLOCK), BF16)],
        compiler_params=_params("arbitrary"),
        name="inproj",
    )(x2, mod3, gpre, w_in_r, q_norm, wq2, kv_norm, wk, wv_t, cos_t, sin_t)


def _split3_packed(v):
    lane = lax.broadcasted_iota(jnp.int32, v.shape, 1)
    v = jnp.where(lane < SSD_HEADS, v, 0.0)
    hi = v.astype(BF16).astype(F32)
    rest = v - hi
    mid = rest.astype(BF16).astype(F32)
    lo = rest - mid
    return (hi + pltpu.roll(mid, SSD_HEADS, 1) + pltpu.roll(lo, 2 * SSD_HEADS, 1)).astype(BF16)


def _ssd_kernel(xbc_ref, z_ref, dt_ref, cw_ref, cb_ref, dtb_ref, alog_ref, dsk_ref, gn_ref,
                e_ref, y_ref, tail_ref, state_ref):
    L = SSD_CHUNK
    gw = SSD_INNER // SSD_GROUPS

    @pl.when(pl.program_id(1) == 0)
    def _():
        tail_ref[...] = jnp.zeros_like(tail_ref)
        state_ref[...] = jnp.zeros_like(state_ref)

    cur = xbc_ref[...].astype(F32)
    prev8 = tail_ref[...]
    row8 = lax.broadcasted_iota(jnp.int32, prev8.shape, 0)
    acc = cur * cw_ref[SSD_CONV - 1:SSD_CONV, :] + cb_ref[...]
    for j in range(1, SSD_CONV):
        r = pltpu.roll(cur, j, 0)
        head = jnp.where(row8 < j, pltpu.roll(prev8, j, 0), r[0:8])
        r = jnp.concatenate([head, r[8:]], axis=0)
        acc = acc + r * cw_ref[SSD_CONV - 1 - j:SSD_CONV - j, :]
    tail_ref[...] = cur[L - 8:L]
    act = _silu(acc)
    xs = act[:, :SSD_INNER]
    bm = act[:, SSD_INNER:SSD_INNER + SSD_GROUPS * SSD_STATE].astype(BF16)
    cm = act[:, SSD_INNER + SSD_GROUPS * SSD_STATE:].astype(BF16)

    dt_in = dt_ref[...] + dtb_ref[...]
    dt = jnp.maximum(dt_in, 0.0) + jnp.log(1.0 + jnp.exp(-jnp.abs(dt_in)))
    adt = dt * (-jnp.exp(alog_ref[...]))
    ri = lax.broadcasted_iota(jnp.int32, (L, L), 0)
    ci = lax.broadcasted_iota(jnp.int32, (L, L), 1)
    causal = ci <= ri
    acs_p = jnp.dot(causal.astype(BF16), _split3_packed(adt), preferred_element_type=F32)
    a_cs = (acs_p + pltpu.roll(acs_p, LANES - SSD_HEADS, 1)
            + pltpu.roll(acs_p, LANES - 2 * SSD_HEADS, 1))
    a_cs_t = a_cs.T
    cs_all = jnp.dot(_split3_packed(a_cs), e_ref[...], preferred_element_type=F32)
    cs64 = cs_all[:, :SSD_INNER]
    cs128 = cs_all[:, SSD_INNER:]
    dt64 = jnp.dot(_split3_packed(dt), e_ref[:, :SSD_INNER], preferred_element_type=F32)

    xd = xs * dt64
    xd_b = xd.astype(BF16)
    last = cs64[L - 1:L, :]
    xdw = (xd * jnp.exp(last - cs64)).astype(BF16)
    chunk_decay = jnp.exp(last)
    in_decay = jnp.exp(cs64)

    lane = lax.broadcasted_iota(jnp.int32, (L, LANES), 1)
    lo = lane < SSD_HEAD_DIM
    zero_b = jnp.zeros((L, LANES), BF16)
    y_parts = []
    new_states = []
    for g in range(SSD_GROUPS):
        bg = bm[:, g * SSD_STATE:(g + 1) * SSD_STATE]
        cg = cm[:, g * SSD_STATE:(g + 1) * SSD_STATE]
        cb = lax.dot_general(cg, bg, (((1,), (1,)), ((), ())), preferred_element_type=F32)
        prev = state_ref[:, g * gw:(g + 1) * gw]
        y_off = jnp.dot(cg, prev.astype(BF16), preferred_element_type=F32) * in_decay[:, g * gw:(g + 1) * gw]
        st = lax.dot_general(bg, xdw[:, g * gw:(g + 1) * gw], (((0,), (0,)), ((), ())),
                             preferred_element_type=F32)
        new_states.append(prev * chunk_decay[:, g * gw:(g + 1) * gw] + st)
        heads_per_group = SSD_HEADS // SSD_GROUPS
        for pair in range(heads_per_group // 2):
            h0 = g * heads_per_group + 2 * pair
            blk = xd_b[:, h0 * SSD_HEAD_DIM:(h0 + 2) * SSD_HEAD_DIM]
            y_pair = None
            for k in range(2):
                hh = h0 + k
                diff = cs128[:, hh * LANES:(hh + 1) * LANES] - a_cs_t[hh:hh + 1, :]
                decay = jnp.where(causal, jnp.exp(diff), 0.0)
                m = (cb * decay).astype(BF16)
                rhs = jnp.where(lo, blk, zero_b) if k == 0 else jnp.where(lo, zero_b, blk)
                part = jnp.dot(m, rhs, preferred_element_type=F32)
                y_pair = part if y_pair is None else y_pair + part
            c0 = (2 * pair) * SSD_HEAD_DIM
            y_parts.append(y_pair + y_off[:, c0:c0 + LANES])
    state_ref[...] = jnp.concatenate(new_states, axis=1)
    y = jnp.concatenate(y_parts, axis=1) + dsk_ref[...] * xs
    gated = y * _silu(z_ref[...].astype(F32))
    y_ref[...] = _rms(gated, gn_ref[...]).astype(BF16)


def _ssd(xbc, z, dt_raw, conv_w, conv_b, dt_bias, a_log, d_skip, ssd_norm, batch, seq):
    t = xbc.shape[0]
    nc = seq // SSD_CHUNK
    L = SSD_CHUNK
    row = lambda b, c: (b * nc + c, 0)
    const = lambda b, c: (0, 0)
    pad = lambda v: jnp.concatenate([v.reshape(1, -1), jnp.zeros((1, LANES - v.size), F32)], axis=1)
    head_of_lane = jnp.concatenate([jnp.arange(SSD_INNER) // SSD_HEAD_DIM,
                                    jnp.arange(SSD_HEADS * LANES) // LANES])
    rows = jnp.arange(LANES)[:, None]
    expand = ((rows < 3 * SSD_HEADS) & (rows % SSD_HEADS == head_of_lane[None, :])).astype(BF16)
    dsk = jnp.repeat(d_skip.astype(F32), SSD_HEAD_DIM).reshape(1, SSD_INNER)
    return pl.pallas_call(
        _ssd_kernel,
        grid=(batch, nc),
        in_specs=[pl.BlockSpec((L, SSD_XBC), row),
                  pl.BlockSpec((L, SSD_INNER), row),
                  pl.BlockSpec((L, LANES), row),
                  pl.BlockSpec((SSD_CONV, SSD_XBC), const),
                  pl.BlockSpec((1, SSD_XBC), const),
                  pl.BlockSpec((1, LANES), const),
                  pl.BlockSpec((1, LANES), const),
                  pl.BlockSpec((1, SSD_INNER), const),
                  pl.BlockSpec((1, SSD_INNER), const),
                  pl.BlockSpec(expand.shape, const)],
        out_specs=pl.BlockSpec((L, SSD_INNER), row),
        out_shape=jax.ShapeDtypeStruct((t, SSD_INNER), BF16),
        scratch_shapes=[pltpu.VMEM((8, SSD_XBC), F32),
                        pltpu.VMEM((SSD_STATE, SSD_INNER), F32)],
        compiler_params=_params("arbitrary", "arbitrary"),
        name="ssd",
    )(xbc, z, dt_raw, conv_w, conv_b.reshape(1, -1), pad(dt_bias), pad(a_log), dsk,
      ssd_norm.reshape(1, -1), expand)


def _attn_kernel(q_ref, k_ref, vt_ref, gn_ref, o_ref, s_ref):
    blk = ATT_BLOCK
    qi = pl.program_id(1)
    causal = (lax.broadcasted_iota(jnp.int32, (blk, blk), 0)
              <= lax.broadcasted_iota(jnp.int32, (blk, blk), 1))
    dims = (((1,), (1,)), ((), ()))
    qs = [q_ref[:, hd * HEAD_PAD:(hd + 1) * HEAD_PAD] for hd in range(MLA_HEADS)]

    def step(j, carry, masked):
        rows = pl.ds(pl.multiple_of(j * blk, blk), blk)
        m_new = []
        for hd in range(MLA_HEADS):
            kk = k_ref[rows, hd * HEAD_PAD:(hd + 1) * HEAD_PAD]
            s = lax.dot_general(kk, qs[hd], dims, preferred_element_type=F32)
            if masked:
                s = jnp.where(causal, s, -jnp.inf)
            s_ref[hd] = s
            m_new.append(jnp.maximum(carry[hd][0], jnp.max(s, axis=0, keepdims=True)))
        new = []
        for hd in range(MLA_HEADS):
            m_old, acc = carry[hd]
            p = jnp.exp2(s_ref[hd] - m_new[hd])
            alpha = jnp.exp2(m_old - m_new[hd])
            vt = vt_ref[j, hd * MLA_V_ROWS:(hd + 1) * MLA_V_ROWS, :]
            acc = acc * alpha + jnp.dot(vt, p.astype(BF16), preferred_element_type=F32)
            new.append((m_new[hd], acc))
        return tuple(new)

    neg = jnp.full((1, blk), -jnp.inf, F32)
    init = tuple((neg, jnp.zeros((MLA_V_ROWS, blk), F32)) for _ in range(MLA_HEADS))
    carry = lax.fori_loop(0, qi, functools.partial(step, masked=False), init)
    final = step(qi, carry, True)
    y_t = jnp.concatenate([acc[:MLA_V] / acc[MLA_V:MLA_V + 1] for _, acc in final], axis=0)
    o_ref[...] = _rms(y_t.T, gn_ref[...]).astype(BF16)


def _attention(q, k, v_t, gn, batch, seq):
    t = q.shape[0]
    blk = ATT_BLOCK
    nq = seq // blk
    nqk = MLA_HEADS * HEAD_PAD
    return pl.pallas_call(
        _attn_kernel,
        grid=(batch, nq),
        in_specs=[pl.BlockSpec((blk, nqk), lambda b, i: (b * nq + i, 0)),
                  pl.BlockSpec((seq, nqk), lambda b, i: (b, 0)),
                  pl.BlockSpec((nq, MLA_HEADS * MLA_V_ROWS, blk), lambda b, i: (b, 0, 0)),
                  pl.BlockSpec((1, MLA_INNER), lambda b, i: (0, 0))],
        out_specs=pl.BlockSpec((blk, MLA_INNER), lambda b, i: (b * nq + i, 0)),
        out_shape=jax.ShapeDtypeStruct((t, MLA_INNER), BF16),
        scratch_shapes=[pltpu.VMEM((MLA_HEADS, blk, blk), F32)],
        compiler_params=_params("arbitrary", "arbitrary"),
        name="attention",
    )(q, k, v_t, gn)


def _outproj_kernel(ys_ref, ym_ref, x_ref, mod_ref, wtop_ref, wbot_ref, gpm_ref, gpf_ref,
                    wrh_ref, wrl_ref, br_ref, x1_ref, xs_ref, col_ref, wslot_ref, cnt_ref, ce_ref):
    y = (jnp.dot(ys_ref[...], wtop_ref[...], preferred_element_type=F32)
         + jnp.dot(ym_ref[...], wbot_ref[...], preferred_element_type=F32))
    g1 = mod_ref[0, 2:3, :]
    sh2 = mod_ref[0, 3:4, :]
    sc2 = mod_ref[0, 4:5, :]
    x1 = x_ref[...] + g1 * _rms(y, gpm_ref[...])
    x1_ref[...] = x1
    h2 = _rms(x1, gpf_ref[...]) * (1.0 + sc2) + sh2
    tm = h2.shape[0]
    h_hi = h2.astype(BF16)
    h_lo = (h2 - h_hi.astype(F32)).astype(BF16)
    logits = (jnp.dot(h_hi, wrh_ref[...], preferred_element_type=F32)
              + jnp.dot(h_lo, wrh_ref[...], preferred_element_type=F32)
              + jnp.dot(h_hi, wrl_ref[...], preferred_element_type=F32)) + br_ref[...]
    lt = logits.T
    row = lax.broadcasted_iota(jnp.int32, lt.shape, 0).astype(F32)
    ninf = -jnp.inf
    big = 1e9
    is_g = row < MOE_GROUPS
    gl = jnp.where(is_g, lt, ninf)
    gmax = jnp.max(gl, axis=0, keepdims=True)
    gsum = jnp.sum(jnp.where(is_g, jnp.exp(lt - gmax), 0.0), axis=0, keepdims=True)
    g_w = 1.0 / gsum
    g_idx = jnp.min(jnp.where(gl == gmax, row, big), axis=0, keepdims=True)
    first = MOE_GROUPS + MOE_EXPERTS_PER_GROUP * g_idx
    el = jnp.where(row >= first, jnp.where(row < first + MOE_EXPERTS_PER_GROUP, lt, ninf), ninf)
    m1 = jnp.max(el, axis=0, keepdims=True)
    i1 = jnp.min(jnp.where(el == m1, row, big), axis=0, keepdims=True)
    el2 = jnp.where(row == i1, ninf, el)
    m2 = jnp.max(el2, axis=0, keepdims=True)
    i2 = jnp.min(jnp.where(el2 == m2, row, big), axis=0, keepdims=True)
    r = jnp.exp(m2 - m1)
    w1 = g_w / (1.0 + r)
    w2 = g_w * r / (1.0 + r)
    sel1 = row == i1
    sel2 = row == i2
    both = jnp.where(sel1, 1.0, jnp.where(sel2, 1.0, 0.0))
    cnt16 = jnp.floor((jnp.sum(both, axis=1, keepdims=True) + (MOE_CHUNK - 1)) * (1.0 / MOE_CHUNK))
    er = lax.broadcasted_iota(jnp.int32, (LANES, LANES), 0)
    ec = lax.broadcasted_iota(jnp.int32, (LANES, LANES), 1)
    cnt16_b = jnp.broadcast_to(cnt16, (LANES, LANES))
    off16 = jnp.dot((ec < er).astype(BF16), cnt16_b.astype(BF16), preferred_element_type=F32)[:, 0:1]
    earlier = (lax.broadcasted_iota(jnp.int32, (tm, tm), 0)
               < lax.broadcasted_iota(jnp.int32, (tm, tm), 1)).astype(BF16)
    base = off16 * MOE_CHUNK + jnp.dot(both.astype(BF16), earlier, preferred_element_type=F32)
    lpos1 = jnp.sum(jnp.where(sel1, base, 0.0), axis=0, keepdims=True)
    lpos2 = jnp.sum(jnp.where(sel2, base, 0.0), axis=0, keepdims=True)
    srow = lax.broadcasted_iota(jnp.int32, (MOE_SLOTS, tm), 0)
    hit1 = srow == lpos1.astype(jnp.int32)
    hit2 = srow == lpos2.astype(jnp.int32)
    perm = jnp.where(hit1, 1.0, jnp.where(hit2, 1.0, 0.0)).astype(BF16)
    xs_ref[...] = jnp.dot(perm, h_hi, preferred_element_type=F32).astype(BF16)
    wslot = jnp.sum(jnp.where(hit1, w1, jnp.where(hit2, w2, 0.0)), axis=1, keepdims=True)
    wslot_ref[...] = jnp.broadcast_to(wslot, wslot_ref.shape)
    col_ref[...] = jnp.where(row == 0, lpos1, jnp.where(row == 1, lpos2, 0.0)).T
    cnt_ref[...] = cnt16_b
    chunk = lax.broadcasted_iota(jnp.int32, (LANES, LANES), 1).astype(F32)
    ce = jnp.sum(jnp.where(off16 + cnt16 <= chunk, 1.0, 0.0), axis=0, keepdims=True) - MOE_GROUPS
    ce_ref[...] = jnp.broadcast_to(ce, ce_ref.shape)


def _outproj(y_ssd, y_mla, x2, mod3, w_top, w_bot, gpm, gpf, wr_hi, wr_lo, b_router, seq, tm):
    t, d = x2.shape
    per_b = seq // tm
    n_tok_tiles = t // tm
    row = lambda i: (i, 0)
    const = lambda i: (0, 0)
    return pl.pallas_call(
        _outproj_kernel,
        grid=(n_tok_tiles,),
        in_specs=[pl.BlockSpec((tm, SSD_INNER), row),
                  pl.BlockSpec((tm, MLA_INNER), row),
                  pl.BlockSpec((tm, d), row),
                  pl.BlockSpec((1, N_MOD, d), lambda i: (i // per_b, 0, 0)),
                  pl.BlockSpec(w_top.shape, const),
                  pl.BlockSpec(w_bot.shape, const),
                  pl.BlockSpec((1, d), const),
                  pl.BlockSpec((1, d), const),
                  pl.BlockSpec(wr_hi.shape, const),
                  pl.BlockSpec(wr_lo.shape, const),
                  pl.BlockSpec((1, LANES), const)],
        out_specs=[pl.BlockSpec((tm, d), row),
                   pl.BlockSpec((MOE_SLOTS, d), row),
                   pl.BlockSpec((tm, LANES), row),
                   pl.BlockSpec((MOE_SLOTS, LANES), row),
                   pl.BlockSpec((LANES, LANES), row),
                   pl.BlockSpec((8, LANES), row)],
        out_shape=[jax.ShapeDtypeStruct((t, d), F32),
                   jax.ShapeDtypeStruct((n_tok_tiles * MOE_SLOTS, d), BF16),
                   jax.ShapeDtypeStruct((t, LANES), F32),
                   jax.ShapeDtypeStruct((n_tok_tiles * MOE_SLOTS, LANES), F32),
                   jax.ShapeDtypeStruct((n_tok_tiles * LANES, LANES), F32),
                   jax.ShapeDtypeStruct((n_tok_tiles * 8, LANES), F32)],
        compiler_params=_params("arbitrary"),
        name="outproj_router",
    )(y_ssd, y_mla, x2, mod3, w_top, w_bot, gpm, gpf, wr_hi, wr_lo, b_router)


def _moe_rows(n_tok_tiles):
    rows = n_tok_tiles * MOE_SLOTS + MOE_EXPERTS * (MOE_TILE - MOE_CHUNK)
    return (rows + MOE_TILE - 1) // MOE_TILE * MOE_TILE


def _moe_plan(cnt_tiles, chunk_expert, n_tok_tiles, n_rows):
    chunks_per_tile = MOE_SLOTS // MOE_CHUNK
    chunks_per_rows = MOE_TILE // MOE_CHUNK
    cnt = cnt_tiles.reshape(n_tok_tiles, LANES, LANES)[:, MOE_GROUPS:MOE_GROUPS + MOE_EXPERTS, 0]
    cnt = cnt.astype(jnp.int32)
    ce = chunk_expert.reshape(n_tok_tiles, 8, LANES)[:, 0, :chunks_per_tile].astype(jnp.int32)
    used = jnp.sum(cnt, axis=1)
    total = jnp.sum(cnt, axis=0)
    padded = (total + chunks_per_rows - 1) // chunks_per_rows * chunks_per_rows
    e_end = jnp.cumsum(padded)
    e_start = e_end - padded
    run_global = e_start[None, :] + jnp.cumsum(cnt, axis=0) - cnt
    run_local = jnp.cumsum(cnt, axis=1) - cnt
    experts = jnp.arange(MOE_EXPERTS, dtype=jnp.int32)
    shift = jnp.sum(jnp.where(ce[:, :, None] == experts, (run_global - run_local)[:, None, :], 0), axis=-1)
    dst = shift + jnp.arange(chunks_per_tile, dtype=jnp.int32)[None, :]
    n_row_tiles = n_rows // MOE_TILE
    n_used = e_end[-1] // chunks_per_rows
    j = jnp.minimum(jnp.arange(n_row_tiles, dtype=jnp.int32), jnp.maximum(n_used - 1, 0))
    tile_expert = jnp.sum((e_end[None, :] // chunks_per_rows <= j[:, None]).astype(jnp.int32), axis=1)
    tile_expert = jnp.minimum(tile_expert, MOE_EXPERTS - 1)
    i32 = lambda v: v.astype(jnp.int32)
    return (i32(dst.reshape(-1)), i32(used), i32(e_start + total), i32(padded - total),
            i32(n_used.reshape(1)), i32(tile_expert))


def _chunk_rows(chunk):
    return pl.ds(pl.multiple_of(chunk * MOE_CHUNK, MOE_CHUNK), MOE_CHUNK)


def _run_copies(*groups):
    for wait in (False, True):
        for lo, hi, make_copy in groups:
            def body(k, carry, wait=wait, make_copy=make_copy):
                cp = make_copy(k)
                cp.wait() if wait else cp.start()
                return carry
            lax.fori_loop(lo, hi, body, 0)


def _scatter_chunks_kernel(dst_ref, used_ref, pad_start_ref, pad_n_ref, nused_ref, src_ref, o_ref,
                           zero_ref, sem):
    i = pl.program_id(0)
    base = i * (MOE_SLOTS // MOE_CHUNK)

    @pl.when(i == 0)
    def _():
        zero_ref[...] = jnp.zeros_like(zero_ref)
        pads = [(0, pad_n_ref[e], lambda c, e=e: pltpu.make_async_copy(
            zero_ref.at[pl.ds(0, MOE_CHUNK), :], o_ref.at[_chunk_rows(pad_start_ref[e] + c), :], sem.at[1]))
            for e in range(MOE_EXPERTS)]
        tail = (nused_ref[0], o_ref.shape[0] // MOE_TILE, lambda j: pltpu.make_async_copy(
            zero_ref, o_ref.at[pl.ds(pl.multiple_of(j * MOE_TILE, MOE_TILE), MOE_TILE), :], sem.at[1]))
        _run_copies(tail, *pads)

    _run_copies((0, used_ref[i], lambda c: pltpu.make_async_copy(
        src_ref.at[_chunk_rows(c), :], o_ref.at[_chunk_rows(dst_ref[base + c]), :], sem.at[0])))


def _scatter_chunks(xs_local, dst, used, pad_start, pad_n, n_used, n_rows):
    d = xs_local.shape[1]
    return pl.pallas_call(
        _scatter_chunks_kernel,
        grid_spec=pltpu.PrefetchScalarGridSpec(
            num_scalar_prefetch=5,
            grid=(used.shape[0],),
            in_specs=[pl.BlockSpec((MOE_SLOTS, d), lambda i, *_: (i, 0))],
            out_specs=pl.BlockSpec(memory_space=pl.ANY),
            scratch_shapes=[pltpu.VMEM((MOE_TILE, d), BF16), pltpu.SemaphoreType.DMA((2,))]),
        out_shape=jax.ShapeDtypeStruct((n_rows, d), BF16),
        compiler_params=_params("arbitrary"),
        name="moe_scatter_chunks",
    )(dst, used, pad_start, pad_n, n_used, xs_local)


def _experts_kernel(te_ref, nt_ref, xs_ref, wg_ref, wu_ref, wd_ref, o_ref):
    used = pl.program_id(0) < nt_ref[0]

    @pl.when(used)
    def _():
        x = xs_ref[...]
        gate = jnp.dot(x, wg_ref[0].astype(BF16), preferred_element_type=F32)
        up = jnp.dot(x, wu_ref[0].astype(BF16), preferred_element_type=F32)
        hid = (_silu(gate) * up).astype(BF16)
        o_ref[...] = jnp.dot(hid, wd_ref[0].astype(BF16), preferred_element_type=F32).astype(BF16)

    @pl.when(jnp.logical_not(used))
    def _():
        o_ref[...] = jnp.zeros_like(o_ref)


def _experts(xs, tile_expert, n_used, w_gate, w_up, w_down):
    n_rows, d = xs.shape
    tile = MOE_TILE
    by_expert = lambda j, te, nt: (te[j], 0, 0)
    return pl.pallas_call(
        _experts_kernel,
        grid_spec=pltpu.PrefetchScalarGridSpec(
            num_scalar_prefetch=2,
            grid=(n_rows // tile,),
            in_specs=[pl.BlockSpec((tile, d), lambda j, te, nt: (jnp.maximum(jnp.minimum(j, nt[0] - 1), 0), 0)),
                      pl.BlockSpec((1, d, MOE_FF), by_expert),
                      pl.BlockSpec((1, d, MOE_FF), by_expert),
                      pl.BlockSpec((1, MOE_FF, d), by_expert)],
            out_specs=pl.BlockSpec((tile, d), lambda j, te, nt: (j, 0))),
        out_shape=jax.ShapeDtypeStruct((n_rows, d), BF16),
        compiler_params=_params("arbitrary"),
        name="moe_experts",
    )(tile_expert, n_used, xs, w_gate, w_up, w_down)


def _combine_kernel(dst_ref, used_ref, wslot_ref, col_ref, x1_ref, mod_ref, gpost_ref, ys_ref, o_ref,
                    ysl_ref, sem):
    i = pl.program_id(0)
    chunks = MOE_SLOTS // MOE_CHUNK
    slot = lax.rem(i, 2)

    def fetch(tile, buf, wait):
        base = tile * chunks
        used = used_ref[tile]

        def copy(c, carry):
            cp = pltpu.make_async_copy(ys_ref.at[_chunk_rows(dst_ref[base + c]), :],
                                       ysl_ref.at[buf, _chunk_rows(c), :], sem.at[buf])
            cp.wait() if wait else cp.start()
            return carry

        def clear(c, carry):
            ysl_ref[buf, _chunk_rows(c), :] = jnp.zeros((MOE_CHUNK, ysl_ref.shape[2]), BF16)
            return carry

        lax.fori_loop(0, used, copy, 0)
        if not wait:
            lax.fori_loop(used, chunks, clear, 0)

    @pl.when(i == 0)
    def _():
        fetch(0, 0, False)

    @pl.when(i + 1 < pl.num_programs(0))
    def _():
        fetch(i + 1, 1 - slot, False)

    col = col_ref[...]
    tm = col.shape[0]
    lane = lax.broadcasted_iota(jnp.int32, (tm, MOE_SLOTS), 1)
    hit = jnp.where(lane == col[:, 0:1].astype(jnp.int32), 1.0,
                    jnp.where(lane == col[:, 1:2].astype(jnp.int32), 1.0, 0.0)).astype(BF16)
    fetch(i, slot, True)
    ysw = (ysl_ref[slot].astype(F32) * wslot_ref[:, 0:1]).astype(BF16)
    y = jnp.dot(hit, ysw, preferred_element_type=F32)
    g2 = mod_ref[0, 5:6, :]
    o_ref[...] = x1_ref[...] + g2 * _rms(y, gpost_ref[...])


def _combine(ys, dst, used, wslot, col, x1, mod3, gpost, seq, tm):
    t, d = x1.shape
    per_b = seq // tm
    row = lambda i, *_: (i, 0)
    return pl.pallas_call(
        _combine_kernel,
        grid_spec=pltpu.PrefetchScalarGridSpec(
            num_scalar_prefetch=2,
            grid=(t // tm,),
            in_specs=[pl.BlockSpec((MOE_SLOTS, LANES), row),
                      pl.BlockSpec((tm, LANES), row),
                      pl.BlockSpec((tm, d), row),
                      pl.BlockSpec((1, N_MOD, d), lambda i, *_: (i // per_b, 0, 0)),
                      pl.BlockSpec((1, d), lambda i, *_: (0, 0)),
                      pl.BlockSpec(memory_space=pl.ANY)],
            out_specs=pl.BlockSpec((tm, d), row),
            scratch_shapes=[pltpu.VMEM((2, MOE_SLOTS, d), BF16), pltpu.SemaphoreType.DMA((2,))]),
        out_shape=jax.ShapeDtypeStruct((t, d), F32),
        compiler_params=_params("arbitrary"),
        name="moe_combine",
    )(dst, used, wslot, col, x1, mod3, gpost, ys)


def kernel(x, c, positions, ada_w, ada_b, pre_norm_mix, post_norm_mix, pre_norm_ffn, post_norm_ffn, w_in, conv_w, conv_b, dt_bias, a_log, d_skip, ssd_norm, q_norm, w_uq, kv_norm, w_ukv, mla_out_norm, w_out, w_group_router, b_group_router, w_expert_router, b_expert_router, w_gate, w_up, w_down):
    batch, seq, d = x.shape
    t = batch * seq
    depth = ada_w.shape[0]
    tm = min(512, seq)
    half = MLA_ROPE // 2

    cos16, sin16 = _rope_tables(positions)
    ones = lambda n: jnp.ones((t, n), F32)
    zeros = lambda n: jnp.zeros((t, n), F32)
    cos_t = jnp.concatenate([ones(MLA_NOPE), cos16, cos16, ones(HEAD_PAD - MLA_QK)], axis=1)
    sin_t = jnp.concatenate([zeros(MLA_NOPE), sin16, sin16, zeros(HEAD_PAD - MLA_QK)], axis=1)

    x2 = x.reshape(t, d)
    for l in range(depth):
        mod3 = _modulation(c, ada_w[l], ada_b[l]).reshape(batch, N_MOD, d)
        w_in_r, wq2, wk, wv_t = _inproj_weights(w_in[l], w_uq[l], w_ukv[l])
        z, xbc, dt_raw, q, k, v_t = _inproj(
            x2, mod3, pre_norm_mix[l].reshape(1, d), w_in_r, q_norm[l].reshape(1, -1), wq2,
            kv_norm[l].reshape(1, -1), wk, wv_t, cos_t, sin_t, seq, tm)
        y_ssd = _ssd(xbc, z, dt_raw, conv_w[l], conv_b[l], dt_bias[l], a_log[l], d_skip[l],
                     ssd_norm[l], batch, seq)
        y_mla = _attention(q, k, v_t, mla_out_norm[l].reshape(1, -1), batch, seq)
        w_o = w_out[l].astype(BF16)
        pad_r = LANES - MOE_GROUPS - MOE_EXPERTS
        w_router = jnp.concatenate([w_group_router[l], w_expert_router[l], jnp.zeros((d, pad_r), F32)], axis=1)
        b_router = jnp.concatenate([b_group_router[l].reshape(-1), b_expert_router[l].reshape(-1),
                                    jnp.zeros((pad_r,), F32)]).reshape(1, LANES)
        wr_hi = w_router.astype(BF16)
        wr_lo = (w_router - wr_hi.astype(F32)).astype(BF16)
        x1, xs_local, col, wslot, cnt_tiles, chunk_expert = _outproj(
            y_ssd, y_mla, x2, mod3, w_o[:SSD_INNER], w_o[SSD_INNER:], post_norm_mix[l].reshape(1, d),
            pre_norm_ffn[l].reshape(1, d), wr_hi, wr_lo, b_router, seq, MOE_TOKENS)
        n_tok_tiles = t // MOE_TOKENS
        n_rows = _moe_rows(n_tok_tiles)
        dst, used, pad_start, pad_n, n_used, tile_expert = _moe_plan(cnt_tiles, chunk_expert, n_tok_tiles, n_rows)
        xs = _scatter_chunks(xs_local, dst, used, pad_start, pad_n, n_used, n_rows)
        ys = _experts(xs, tile_expert, n_used, w_gate[l], w_up[l], w_down[l])
        x2 = _combine(ys, dst, used, wslot, col, x1, mod3, post_norm_ffn[l].reshape(1, d), seq, MOE_TOKENS)
    return x2.reshape(batch, seq, d)
```

```python
import functools
import math

import jax
import jax.numpy as jnp
from jax import lax
from jax.experimental import pallas as pl
from jax.experimental.pallas import tpu as pltpu

F32 = jnp.float32
BF16 = jnp.bfloat16

D_MODEL = 1024
SSD_HEADS = 8
SSD_HEAD_DIM = 64
SSD_INNER = SSD_HEADS * SSD_HEAD_DIM
SSD_GROUPS = 2
SSD_STATE = 128
SSD_CONV = 4
SSD_CHUNK = 128
SSD_XBC = SSD_INNER + 2 * SSD_GROUPS * SSD_STATE
MLA_HEADS = 8
MLA_NOPE = 64
MLA_ROPE = 32
MLA_QK = MLA_NOPE + MLA_ROPE
MLA_V = 64
MLA_Q_RANK = 256
MLA_KV_RANK = 128
MLA_INNER = MLA_HEADS * MLA_V
ROPE_THETA = 10000.0
MOE_GROUPS = 4
MOE_EXPERTS_PER_GROUP = 8
MOE_EXPERTS = MOE_GROUPS * MOE_EXPERTS_PER_GROUP
MOE_FF = 256
N_MOD = 6
EPS = 1e-6
LOG2_E = math.log2(math.e)

LANES = 128
HEAD_PAD = 128
ATT_BLOCK = 256
MLA_V_ROWS = MLA_V + 16
MOE_TOKENS = 512
MOE_CHUNK = 16
MOE_TILE = 512
MOE_SLOTS = 2 * MOE_TOKENS + MOE_EXPERTS * MOE_CHUNK
VMEM_LIMIT = 48 * 1024 * 1024

_C_Z = 0
_C_XBC = _C_Z + SSD_INNER
_C_CQ = _C_XBC + SSD_XBC
_C_CKV = _C_CQ + MLA_Q_RANK
_C_KR = _C_CKV + MLA_KV_RANK
_C_KRS = _C_KR + LANES
_C_DT = _C_KRS + LANES
_C_END = _C_DT + LANES


def _silu(v):
    return v * (1.0 / (1.0 + jnp.exp(-v)))


def _rms(v, gain):
    return v * lax.rsqrt(jnp.mean(v * v, axis=-1, keepdims=True) + EPS) * gain


def _params(*sem, flags=None):
    return pltpu.CompilerParams(dimension_semantics=sem, vmem_limit_bytes=VMEM_LIMIT, flags=flags)


def _mod_kernel(c_ref, w_ref, b_ref, o_ref):
    act = _silu(c_ref[...])
    a_hi = act.astype(BF16)
    a_lo = (act - a_hi.astype(F32)).astype(BF16)
    w = w_ref[...]
    w_hi = w.astype(BF16)
    w_lo = (w - w_hi.astype(F32)).astype(BF16)
    o_ref[...] = (jnp.dot(a_hi, w_hi, preferred_element_type=F32)
                  + jnp.dot(a_lo, w_hi, preferred_element_type=F32)
                  + jnp.dot(a_hi, w_lo, preferred_element_type=F32)) + b_ref[...]


def _modulation(c, ada_w, ada_b):
    b, d = c.shape
    n = ada_w.shape[1]
    return pl.pallas_call(
        _mod_kernel,
        grid=(n // d,),
        in_specs=[pl.BlockSpec((b, d), lambda j: (0, 0)),
                  pl.BlockSpec((d, d), lambda j: (0, j)),
                  pl.BlockSpec((1, d), lambda j: (0, j))],
        out_specs=pl.BlockSpec((b, d), lambda j: (0, j)),
        out_shape=jax.ShapeDtypeStruct((b, n), F32),
        compiler_params=_params("arbitrary"),
        name="modulation",
    )(c, ada_w, ada_b.reshape(1, n))


def _rope_kernel(pos_ref, inv_ref, cos_ref, sin_ref):
    ang = pos_ref[...].astype(F32) * inv_ref[...]
    blk = ang.shape[0]
    half = MLA_ROPE // 2
    per_row = LANES // half
    lane = lax.broadcasted_iota(jnp.int32, (blk * per_row, LANES), 1)
    first = (lane >= MLA_NOPE) & (lane < MLA_NOPE + half)
    second = (lane >= MLA_NOPE + half) & (lane < MLA_QK)
    for fn, out_ref, fill in ((jnp.cos, cos_ref, 1.0), (jnp.sin, sin_ref, 0.0)):
        compact = fn(ang)
        rep = jnp.broadcast_to(compact[:, None, :], (blk, per_row, LANES)).reshape(blk * per_row, LANES)
        a = pltpu.roll(rep, 0, 1, stride=half, stride_axis=0)
        b = pltpu.roll(a, half, 1)
        out_ref[...] = jnp.where(first, a, jnp.where(second, b, fill))


def _rope_tables(positions):
    t = positions.size
    half = MLA_ROPE // 2
    per_row = LANES // half
    inv = 1.0 / (ROPE_THETA ** (jnp.arange(0, MLA_ROPE, 2, dtype=F32) / MLA_ROPE))
    group_token = (MLA_NOPE // half - jnp.arange(per_row)) % per_row
    pos_rep = jnp.repeat(positions.reshape(t // per_row, per_row)[:, group_token], half, axis=1)
    inv_t = jnp.tile(inv, per_row).reshape(1, LANES)
    rows = t // per_row
    blk = min(rows, 512)
    return pl.pallas_call(
        _rope_kernel,
        grid=(rows // blk,),
        in_specs=[pl.BlockSpec((blk, LANES), lambda i: (i, 0)),
                  pl.BlockSpec((1, LANES), lambda i: (0, 0))],
        out_specs=[pl.BlockSpec((blk * per_row, LANES), lambda i: (i, 0))] * 2,
        out_shape=[jax.ShapeDtypeStruct((t, LANES), F32)] * 2,
        compiler_params=_params("arbitrary"),
        name="rope_tables",
    )(pos_rep, inv_t)


def _inproj_kernel(x_ref, mod_ref, gpre_ref, win_ref, qn_ref, wq_ref, kvn_ref, wk_ref, wvt_ref,
                   cos_ref, sin_ref, z_ref, xbc_ref, dt_ref, q_ref, k_ref, vt_ref):
    x = x_ref[...]
    sh = mod_ref[0, 0:1, :]
    sc = mod_ref[0, 1:2, :]
    h = (_rms(x, gpre_ref[...]) * (1.0 + sc) + sh).astype(BF16)
    u = jnp.dot(h, win_ref[...], preferred_element_type=F32)
    z_ref[...] = u[:, _C_Z:_C_XBC].astype(BF16)
    xbc_ref[...] = u[:, _C_XBC:_C_CQ].astype(BF16)
    dt_ref[...] = u[:, _C_DT:_C_END]
    cos_t = cos_ref[...]
    sin_t = sin_ref[...]
    cq = _rms(u[:, _C_CQ:_C_CKV], qn_ref[...]).astype(BF16)
    q2 = jnp.dot(cq, wq_ref[...], preferred_element_type=F32)
    scale = MLA_QK ** -0.5 * LOG2_E
    nq = MLA_HEADS * HEAD_PAD
    for hd in range(MLA_HEADS):
        a = q2[:, hd * HEAD_PAD:(hd + 1) * HEAD_PAD]
        b = q2[:, nq + hd * HEAD_PAD:nq + (hd + 1) * HEAD_PAD]
        q_ref[:, hd * HEAD_PAD:(hd + 1) * HEAD_PAD] = ((a * cos_t + b * sin_t) * scale).astype(BF16)
    ckv = _rms(u[:, _C_CKV:_C_KR], kvn_ref[...]).astype(BF16)
    kn = jnp.dot(ckv, wk_ref[...], preferred_element_type=F32)
    k_pe = u[:, _C_KR:_C_KRS] * cos_t + u[:, _C_KRS:_C_DT] * sin_t
    for hd in range(MLA_HEADS):
        k_ref[:, hd * HEAD_PAD:(hd + 1) * HEAD_PAD] = (
            kn[:, hd * HEAD_PAD:(hd + 1) * HEAD_PAD] + k_pe).astype(BF16)
    v_t = lax.dot_general(wvt_ref[...], ckv, (((1,), (1,)), ((), ())), preferred_element_type=F32)
    head_row = lax.rem(lax.broadcasted_iota(jnp.int32, (v_t.shape[0], 1), 0), MLA_V_ROWS)
    v_t = v_t + jnp.where(head_row == MLA_V, 1.0, 0.0)
    for s in range(vt_ref.shape[0]):
        vt_ref[s] = v_t[:, s * ATT_BLOCK:(s + 1) * ATT_BLOCK].astype(BF16)


def _inproj_weights(w_in, w_uq, w_ukv):
    d = w_in.shape[0]
    half = MLA_ROPE // 2
    o_z, o_xbc, o_dt = 0, SSD_INNER, SSD_INNER + SSD_XBC
    o_cq = o_dt + SSD_HEADS
    o_ckv = o_cq + MLA_Q_RANK
    o_kr = o_ckv + MLA_KV_RANK
    zeros = lambda n: jnp.zeros((d, n), F32)
    kr = w_in[:, o_kr:o_kr + MLA_ROPE]
    kr_blk = jnp.concatenate([zeros(MLA_NOPE), kr, zeros(HEAD_PAD - MLA_QK)], axis=1)
    krs_blk = jnp.concatenate([zeros(MLA_NOPE), -kr[:, half:], kr[:, :half], zeros(HEAD_PAD - MLA_QK)], axis=1)
    dt_blk = jnp.concatenate([w_in[:, o_dt:o_dt + SSD_HEADS], zeros(LANES - SSD_HEADS)], axis=1)
    w_in_r = jnp.concatenate([w_in[:, o_z:o_dt], w_in[:, o_cq:o_kr], kr_blk, krs_blk, dt_blk], axis=1)
    r = w_uq.shape[0]
    zq = jnp.zeros((r, MLA_HEADS, HEAD_PAD - MLA_QK), F32)
    zn = jnp.zeros((r, MLA_HEADS, MLA_NOPE), F32)
    wq_plain = jnp.concatenate([w_uq, zq], axis=2).reshape(r, MLA_HEADS * HEAD_PAD)
    wq_rot = jnp.concatenate([zn, -w_uq[:, :, MLA_NOPE + half:], w_uq[:, :, MLA_NOPE:MLA_NOPE + half], zq],
                             axis=2).reshape(r, MLA_HEADS * HEAD_PAD)
    wq2 = jnp.concatenate([wq_plain, wq_rot], axis=1)
    rk = w_ukv.shape[0]
    zk = jnp.zeros((rk, MLA_HEADS, HEAD_PAD - MLA_NOPE), F32)
    wk = jnp.concatenate([w_ukv[:, :, :MLA_NOPE], zk], axis=2).reshape(rk, MLA_HEADS * HEAD_PAD)
    zv = jnp.zeros((rk, MLA_HEADS, MLA_V_ROWS - MLA_V), F32)
    wv_t = jnp.concatenate([w_ukv[:, :, MLA_NOPE:], zv], axis=2).reshape(rk, MLA_HEADS * MLA_V_ROWS).T
    return w_in_r.astype(BF16), wq2.astype(BF16), wk.astype(BF16), wv_t.astype(BF16)


def _inproj(x2, mod3, gpre, w_in_r, q_norm, wq2, kv_norm, wk, wv_t, cos_t, sin_t, seq, tm):
    t, d = x2.shape
    per_b = seq // tm
    row = lambda i: (i, 0)
    const = lambda i: (0, 0)
    nqk = MLA_HEADS * HEAD_PAD
    slabs = tm // ATT_BLOCK
    return pl.pallas_call(
        _inproj_kernel,
        grid=(t // tm,),
        in_specs=[pl.BlockSpec((tm, d), row),
                  pl.BlockSpec((1, N_MOD, d), lambda i: (i // per_b, 0, 0)),
                  pl.BlockSpec((1, d), const),
                  pl.BlockSpec(w_in_r.shape, const),
                  pl.BlockSpec((1, MLA_Q_RANK), const),
                  pl.BlockSpec(wq2.shape, const),
                  pl.BlockSpec((1, MLA_KV_RANK), const),
                  pl.BlockSpec(wk.shape, const),
                  pl.BlockSpec(wv_t.shape, const),
                  pl.BlockSpec((tm, LANES), row),
                  pl.BlockSpec((tm, LANES), row)],
        out_specs=[pl.BlockSpec((tm, SSD_INNER), row),
                   pl.BlockSpec((tm, SSD_XBC), row),
                   pl.BlockSpec((tm, LANES), row),
                   pl.BlockSpec((tm, nqk), row),
                   pl.BlockSpec((tm, nqk), row),
                   pl.BlockSpec((slabs, MLA_HEADS * MLA_V_ROWS, ATT_BLOCK), lambda i: (i, 0, 0))],
        out_shape=[jax.ShapeDtypeStruct((t, SSD_INNER), BF16),
                   jax.ShapeDtypeStruct((t, SSD_XBC), BF16),
                   jax.ShapeDtypeStruct((t, LANES), F32),
                   jax.ShapeDtypeStruct((t, nqk), BF16),
                   jax.ShapeDtypeStruct((t, nqk), BF16),
                   jax.ShapeDtypeStruct((t // ATT_BLOCK, MLA_HEADS * MLA_V_ROWS, ATT_BLOCK), BF16)],
        compiler_params=_params("arbitrary"),
        name="inproj",
    )(x2, mod3, gpre, w_in_r, q_norm, wq2, kv_norm, wk, wv_t, cos_t, sin_t)


def _split3_packed(v):
    lane = lax.broadcasted_iota(jnp.int32, v.shape, 1)
    v = jnp.where(lane < SSD_HEADS, v, 0.0)
    hi = v.astype(BF16).astype(F32)
    rest = v - hi
    mid = rest.astype(BF16).astype(F32)
    lo = rest - mid
    return (hi + pltpu.roll(mid, SSD_HEADS, 1) + pltpu.roll(lo, 2 * SSD_HEADS, 1)).astype(BF16)


def _ssd_kernel(xbc_ref, z_ref, dt_ref, cw_ref, cb_ref, dtb_ref, alog_ref, dsk_ref, gn_ref,
                e_ref, y_ref, tail_ref, state_ref):
    L = SSD_CHUNK
    gw = SSD_INNER // SSD_GROUPS

    @pl.when(pl.program_id(1) == 0)
    def _():
        tail_ref[...] = jnp.zeros_like(tail_ref)
        state_ref[...] = jnp.zeros_like(state_ref)

    cur = xbc_ref[...].astype(F32)
    prev8 = tail_ref[...]
    row8 = lax.broadcasted_iota(jnp.int32, prev8.shape, 0)
    acc = cur * cw_ref[SSD_CONV - 1:SSD_CONV, :] + cb_ref[...]
    for j in range(1, SSD_CONV):
        r = pltpu.roll(cur, j, 0)
        head = jnp.where(row8 < j, pltpu.roll(prev8, j, 0), r[0:8])
        r = jnp.concatenate([head, r[8:]], axis=0)
        acc = acc + r * cw_ref[SSD_CONV - 1 - j:SSD_CONV - j, :]
    tail_ref[...] = cur[L - 8:L]
    act = _silu(acc)
    xs = act[:, :SSD_INNER]
    bm = act[:, SSD_INNER:SSD_INNER + SSD_GROUPS * SSD_STATE].astype(BF16)
    cm = act[:, SSD_INNER + SSD_GROUPS * SSD_STATE:].astype(BF16)

    dt_in = dt_ref[...] + dtb_ref[...]
    dt = jnp.maximum(dt_in, 0.0) + jnp.log(1.0 + jnp.exp(-jnp.abs(dt_in)))
    adt = dt * (-jnp.exp(alog_ref[...]))
    ri = lax.broadcasted_iota(jnp.int32, (L, L), 0)
    ci = lax.broadcasted_iota(jnp.int32, (L, L), 1)
    causal = ci <= ri
    acs_p = jnp.dot(causal.astype(BF16), _split3_packed(adt), preferred_element_type=F32)
    a_cs = (acs_p + pltpu.roll(acs_p, LANES - SSD_HEADS, 1)
            + pltpu.roll(acs_p, LANES - 2 * SSD_HEADS, 1))
    a_cs_t = a_cs.T
    cs_all = jnp.dot(_split3_packed(a_cs), e_ref[...], preferred_element_type=F32)
    cs64 = cs_all[:, :SSD_INNER]
    cs128 = cs_all[:, SSD_INNER:]
    dt64 = jnp.dot(_split3_packed(dt), e_ref[:, :SSD_INNER], preferred_element_type=F32)

    xd = xs * dt64
    xd_b = xd.astype(BF16)
    last = cs64[L - 1:L, :]
    xdw = (xd * jnp.exp(last - cs64)).astype(BF16)
    chunk_decay = jnp.exp(last)
    in_decay = jnp.exp(cs64)

    lane = lax.broadcasted_iota(jnp.int32, (L, LANES), 1)
    lo = lane < SSD_HEAD_DIM
    zero_b = jnp.zeros((L, LANES), BF16)
    y_parts = []
    new_states = []
    for g in range(SSD_GROUPS):
        bg = bm[:, g * SSD_STATE:(g + 1) * SSD_STATE]
        cg = cm[:, g * SSD_STATE:(g + 1) * SSD_STATE]
        cb = lax.dot_general(cg, bg, (((1,), (1,)), ((), ())), preferred_element_type=F32)
        prev = state_ref[:, g * gw:(g + 1) * gw]
        y_off = jnp.dot(cg, prev.astype(BF16), preferred_element_type=F32) * in_decay[:, g * gw:(g + 1) * gw]
        st = lax.dot_general(bg, xdw[:, g * gw:(g + 1) * gw], (((0,), (0,)), ((), ())),
                             preferred_element_type=F32)
        new_states.append(prev * chunk_decay[:, g * gw:(g + 1) * gw] + st)
        heads_per_group = SSD_HEADS // SSD_GROUPS
        for pair in range(heads_per_group // 2):
            h0 = g * heads_per_group + 2 * pair
            blk = xd_b[:, h0 * SSD_HEAD_DIM:(h0 + 2) * SSD_HEAD_DIM]
            y_pair = None
            for k in range(2):
                hh = h0 + k
                diff = cs128[:, hh * LANES:(hh + 1) * LANES] - a_cs_t[hh:hh + 1, :]
                decay = jnp.where(causal, jnp.exp(diff), 0.0)
                m = (cb * decay).astype(BF16)
                rhs = jnp.where(lo, blk, zero_b) if k == 0 else jnp.where(lo, zero_b, blk)
                part = jnp.dot(m, rhs, preferred_element_type=F32)
                y_pair = part if y_pair is None else y_pair + part
            c0 = (2 * pair) * SSD_HEAD_DIM
            y_parts.append(y_pair + y_off[:, c0:c0 + LANES])
    state_ref[...] = jnp.concatenate(new_states, axis=1)
    y = jnp.concatenate(y_parts, axis=1) + dsk_ref[...] * xs
    gated = y * _silu(z_ref[...].astype(F32))
    y_ref[...] = _rms(gated, gn_ref[...]).astype(BF16)


def _ssd(xbc, z, dt_raw, conv_w, conv_b, dt_bias, a_log, d_skip, ssd_norm, batch, seq):
    t = xbc.shape[0]
    nc = seq // SSD_CHUNK
    L = SSD_CHUNK
    row = lambda b, c: (b * nc + c, 0)
    const = lambda b, c: (0, 0)
    pad = lambda v: jnp.concatenate([v.reshape(1, -1), jnp.zeros((1, LANES - v.size), F32)], axis=1)
    head_of_lane = jnp.concatenate([jnp.arange(SSD_INNER) // SSD_HEAD_DIM,
                                    jnp.arange(SSD_HEADS * LANES) // LANES])
    rows = jnp.arange(LANES)[:, None]
    expand = ((rows < 3 * SSD_HEADS) & (rows % SSD_HEADS == head_of_lane[None, :])).astype(BF16)
    dsk = jnp.repeat(d_skip.astype(F32), SSD_HEAD_DIM).reshape(1, SSD_INNER)
    return pl.pallas_call(
        _ssd_kernel,
        grid=(batch, nc),
        in_specs=[pl.BlockSpec((L, SSD_XBC), row),
                  pl.BlockSpec((L, SSD_INNER), row),
                  pl.BlockSpec((L, LANES), row),
                  pl.BlockSpec((SSD_CONV, SSD_XBC), const),
                  pl.BlockSpec((1, SSD_XBC), const),
                  pl.BlockSpec((1, LANES), const),
                  pl.BlockSpec((1, LANES), const),
                  pl.BlockSpec((1, SSD_INNER), const),
                  pl.BlockSpec((1, SSD_INNER), const),
                  pl.BlockSpec(expand.shape, const)],
        out_specs=pl.BlockSpec((L, SSD_INNER), row),
        out_shape=jax.ShapeDtypeStruct((t, SSD_INNER), BF16),
        scratch_shapes=[pltpu.VMEM((8, SSD_XBC), F32),
                        pltpu.VMEM((SSD_STATE, SSD_INNER), F32)],
        compiler_params=_params("arbitrary", "arbitrary"),
        name="ssd",
    )(xbc, z, dt_raw, conv_w, conv_b.reshape(1, -1), pad(dt_bias), pad(a_log), dsk,
      ssd_norm.reshape(1, -1), expand)


def _attn_kernel(q_ref, k_ref, vt_ref, gn_ref, o_ref, s_ref):
    blk = ATT_BLOCK
    qi = pl.program_id(1)
    causal = (lax.broadcasted_iota(jnp.int32, (blk, blk), 0)
              <= lax.broadcasted_iota(jnp.int32, (blk, blk), 1))
    dims = (((1,), (1,)), ((), ()))
    qs = [q_ref[:, hd * HEAD_PAD:(hd + 1) * HEAD_PAD] for hd in range(MLA_HEADS)]

    def scores(j, buf, m_prev, masked):
        rows = pl.ds(pl.multiple_of(j * blk, blk), blk)
        m_new = []
        for hd in range(MLA_HEADS):
            kk = k_ref[rows, hd * HEAD_PAD:(hd + 1) * HEAD_PAD]
            s = lax.dot_general(kk, qs[hd], dims, preferred_element_type=F32)
            if masked:
                s = jnp.where(causal, s, -jnp.inf)
            s_ref[buf, hd] = s
            m_new.append(jnp.maximum(m_prev[hd], jnp.max(s, axis=0, keepdims=True)))
        return tuple(m_new)

    def accumulate(j, buf, m_old, m_new, acc):
        out = []
        for hd in range(MLA_HEADS):
            p = jnp.exp2(s_ref[buf, hd] - m_new[hd])
            alpha = jnp.exp2(m_old[hd] - m_new[hd])
            vt = vt_ref[j, hd * MLA_V_ROWS:(hd + 1) * MLA_V_ROWS, :]
            out.append(acc[hd] * alpha + jnp.dot(vt, p.astype(BF16), preferred_element_type=F32))
        return tuple(out)

    neg = tuple(jnp.full((1, blk), -jnp.inf, F32) for _ in range(MLA_HEADS))
    acc0 = tuple(jnp.zeros((MLA_V_ROWS, blk), F32) for _ in range(MLA_HEADS))
    m_first = scores(qi, 0, neg, True)

    def trip(t, carry, parity):
        m_old, m_new, acc = carry
        m_next = scores(t, 1 - parity, m_new, False)
        pending = jnp.where(t == 0, qi, t - 1)
        acc = accumulate(pending, parity, m_old, m_new, acc)
        return m_new, m_next, acc

    def two_trips(u, carry):
        return trip(2 * u + 1, trip(2 * u, carry, 0), 1)

    def finish(_, carry, parity):
        m_old, m_new, acc = carry
        last = jnp.where(qi == 0, qi, qi - 1)
        return m_old, m_new, accumulate(last, parity, m_old, m_new, acc)

    odd = lax.rem(qi, 2)
    carry = lax.fori_loop(0, qi // 2, two_trips, (neg, m_first, acc0))
    carry = lax.fori_loop(0, odd, lambda _, c: trip(qi - 1, c, 0), carry)
    carry = lax.fori_loop(0, 1 - odd, functools.partial(finish, parity=0), carry)
    carry = lax.fori_loop(0, odd, functools.partial(finish, parity=1), carry)
    y_t = jnp.concatenate([a[:MLA_V] / a[MLA_V:MLA_V + 1] for a in carry[2]], axis=0)
    o_ref[...] = _rms(y_t.T, gn_ref[...]).astype(BF16)


def _attention(q, k, v_t, gn, batch, seq):
    t = q.shape[0]
    blk = ATT_BLOCK
    nq = seq // blk
    nqk = MLA_HEADS * HEAD_PAD
    return pl.pallas_call(
        _attn_kernel,
        grid=(batch, nq),
        in_specs=[pl.BlockSpec((blk, nqk), lambda b, i: (b * nq + i, 0)),
                  pl.BlockSpec((seq, nqk), lambda b, i: (b, 0)),
                  pl.BlockSpec((nq, MLA_HEADS * MLA_V_ROWS, blk), lambda b, i: (b, 0, 0)),
                  pl.BlockSpec((1, MLA_INNER), lambda b, i: (0, 0))],
        out_specs=pl.BlockSpec((blk, MLA_INNER), lambda b, i: (b * nq + i, 0)),
        out_shape=jax.ShapeDtypeStruct((t, MLA_INNER), BF16),
        scratch_shapes=[pltpu.VMEM((2, MLA_HEADS, blk, blk), F32)],
        compiler_params=_params("arbitrary", "arbitrary"),
        name="attention",
    )(q, k, v_t, gn)


def _outproj_kernel(ys_ref, ym_ref, x_ref, mod_ref, wtop_ref, wbot_ref, gpm_ref, gpf_ref,
                    wrh_ref, wrl_ref, br_ref, x1_ref, xs_ref, col_ref, wslot_ref, cnt_ref, ce_ref):
    y = (jnp.dot(ys_ref[...], wtop_ref[...], preferred_element_type=F32)
         + jnp.dot(ym_ref[...], wbot_ref[...], preferred_element_type=F32))
    g1 = mod_ref[0, 2:3, :]
    sh2 = mod_ref[0, 3:4, :]
    sc2 = mod_ref[0, 4:5, :]
    x1 = x_ref[...] + g1 * _rms(y, gpm_ref[...])
    x1_ref[...] = x1
    h2 = _rms(x1, gpf_ref[...]) * (1.0 + sc2) + sh2
    tm = h2.shape[0]
    h_hi = h2.astype(BF16)
    h_lo = (h2 - h_hi.astype(F32)).astype(BF16)
    logits = (jnp.dot(h_hi, wrh_ref[...], preferred_element_type=F32)
              + jnp.dot(h_lo, wrh_ref[...], preferred_element_type=F32)
              + jnp.dot(h_hi, wrl_ref[...], preferred_element_type=F32)) + br_ref[...]
    lt = logits.T
    row = lax.broadcasted_iota(jnp.int32, lt.shape, 0).astype(F32)
    ninf = -jnp.inf
    big = 1e9
    is_g = row < MOE_GROUPS
    gl = jnp.where(is_g, lt, ninf)
    gmax = jnp.max(gl, axis=0, keepdims=True)
    gsum = jnp.sum(jnp.where(is_g, jnp.exp(lt - gmax), 0.0), axis=0, keepdims=True)
    g_w = 1.0 / gsum
    g_idx = jnp.min(jnp.where(gl == gmax, row, big), axis=0, keepdims=True)
    first = MOE_GROUPS + MOE_EXPERTS_PER_GROUP * g_idx
    el = jnp.where(row >= first, jnp.where(row < first + MOE_EXPERTS_PER_GROUP, lt, ninf), ninf)
    m1 = jnp.max(el, axis=0, keepdims=True)
    i1 = jnp.min(jnp.where(el == m1, row, big), axis=0, keepdims=True)
    el2 = jnp.where(row == i1, ninf, el)
    m2 = jnp.max(el2, axis=0, keepdims=True)
    i2 = jnp.min(jnp.where(el2 == m2, row, big), axis=0, keepdims=True)
    r = jnp.exp(m2 - m1)
    w1 = g_w / (1.0 + r)
    w2 = g_w * r / (1.0 + r)
    sel1 = row == i1
    sel2 = row == i2
    both = jnp.where(sel1, 1.0, jnp.where(sel2, 1.0, 0.0))
    cnt16 = jnp.floor((jnp.sum(both, axis=1, keepdims=True) + (MOE_CHUNK - 1)) * (1.0 / MOE_CHUNK))
    er = lax.broadcasted_iota(jnp.int32, (LANES, LANES), 0)
    ec = lax.broadcasted_iota(jnp.int32, (LANES, LANES), 1)
    cnt16_b = jnp.broadcast_to(cnt16, (LANES, LANES))
    off16 = jnp.dot((ec < er).astype(BF16), cnt16_b.astype(BF16), preferred_element_type=F32)[:, 0:1]
    earlier = (lax.broadcasted_iota(jnp.int32, (tm, tm), 0)
               < lax.broadcasted_iota(jnp.int32, (tm, tm), 1)).astype(BF16)
    base = off16 * MOE_CHUNK + jnp.dot(both.astype(BF16), earlier, preferred_element_type=F32)
    lpos1 = jnp.sum(jnp.where(sel1, base, 0.0), axis=0, keepdims=True)
    lpos2 = jnp.sum(jnp.where(sel2, base, 0.0), axis=0, keepdims=True)
    srow = lax.broadcasted_iota(jnp.int32, (MOE_SLOTS, tm), 0)
    hit1 = srow == lpos1.astype(jnp.int32)
    hit2 = srow == lpos2.astype(jnp.int32)
    perm = jnp.where(hit1, 1.0, jnp.where(hit2, 1.0, 0.0)).astype(BF16)
    xs_ref[...] = jnp.dot(perm, h_hi, preferred_element_type=F32).astype(BF16)
    wslot = jnp.sum(jnp.where(hit1, w1, jnp.where(hit2, w2, 0.0)), axis=1, keepdims=True)
    wslot_ref[...] = jnp.broadcast_to(wslot, (MOE_SLOTS, LANES)).T[0:8, :]
    col_ref[...] = jnp.where(row == 0, lpos1, jnp.where(row == 1, lpos2, 0.0)).T
    cnt_ref[...] = cnt16_b
    chunk = lax.broadcasted_iota(jnp.int32, (LANES, LANES), 1).astype(F32)
    ce = jnp.sum(jnp.where(off16 + cnt16 <= chunk, 1.0, 0.0), axis=0, keepdims=True) - MOE_GROUPS
    ce_ref[...] = jnp.broadcast_to(ce, ce_ref.shape)


def _outproj(y_ssd, y_mla, x2, mod3, w_top, w_bot, gpm, gpf, wr_hi, wr_lo, b_router, seq, tm):
    t, d = x2.shape
    per_b = seq // tm
    n_tok_tiles = t // tm
    row = lambda i: (i, 0)
    const = lambda i: (0, 0)
    return pl.pallas_call(
        _outproj_kernel,
        grid=(n_tok_tiles,),
        in_specs=[pl.BlockSpec((tm, SSD_INNER), row),
                  pl.BlockSpec((tm, MLA_INNER), row),
                  pl.BlockSpec((tm, d), row),
                  pl.BlockSpec((1, N_MOD, d), lambda i: (i // per_b, 0, 0)),
                  pl.BlockSpec(w_top.shape, const),
                  pl.BlockSpec(w_bot.shape, const),
                  pl.BlockSpec((1, d), const),
                  pl.BlockSpec((1, d), const),
                  pl.BlockSpec(wr_hi.shape, const),
                  pl.BlockSpec(wr_lo.shape, const),
                  pl.BlockSpec((1, LANES), const)],
        out_specs=[pl.BlockSpec((tm, d), row),
                   pl.BlockSpec((MOE_SLOTS, d), row),
                   pl.BlockSpec((tm, LANES), row),
                   pl.BlockSpec((8, MOE_SLOTS), row),
                   pl.BlockSpec((LANES, LANES), row),
                   pl.BlockSpec((8, LANES), row)],
        out_shape=[jax.ShapeDtypeStruct((t, d), F32),
                   jax.ShapeDtypeStruct((n_tok_tiles * MOE_SLOTS, d), BF16),
                   jax.ShapeDtypeStruct((t, LANES), F32),
                   jax.ShapeDtypeStruct((n_tok_tiles * 8, MOE_SLOTS), F32),
                   jax.ShapeDtypeStruct((n_tok_tiles * LANES, LANES), F32),
                   jax.ShapeDtypeStruct((n_tok_tiles * 8, LANES), F32)],
        compiler_params=_params("arbitrary"),
        name="outproj_router",
    )(y_ssd, y_mla, x2, mod3, w_top, w_bot, gpm, gpf, wr_hi, wr_lo, b_router)


def _moe_rows(n_tok_tiles):
    rows = n_tok_tiles * MOE_SLOTS + MOE_EXPERTS * (MOE_TILE - MOE_CHUNK)
    return (rows + MOE_TILE - 1) // MOE_TILE * MOE_TILE


def _moe_plan(cnt_tiles, chunk_expert, n_tok_tiles, n_rows):
    chunks_per_tile = MOE_SLOTS // MOE_CHUNK
    chunks_per_rows = MOE_TILE // MOE_CHUNK
    cnt = cnt_tiles.reshape(n_tok_tiles, LANES, LANES)[:, MOE_GROUPS:MOE_GROUPS + MOE_EXPERTS, 0]
    cnt = cnt.astype(jnp.int32)
    ce = chunk_expert.reshape(n_tok_tiles, 8, LANES)[:, 0, :chunks_per_tile].astype(jnp.int32)
    used = jnp.sum(cnt, axis=1)
    total = jnp.sum(cnt, axis=0)
    padded = (total + chunks_per_rows - 1) // chunks_per_rows * chunks_per_rows
    e_end = jnp.cumsum(padded)
    e_start = e_end - padded
    run_global = e_start[None, :] + jnp.cumsum(cnt, axis=0) - cnt
    run_local = jnp.cumsum(cnt, axis=1) - cnt
    experts = jnp.arange(MOE_EXPERTS, dtype=jnp.int32)
    shift = jnp.sum(jnp.where(ce[:, :, None] == experts, (run_global - run_local)[:, None, :], 0), axis=-1)
    dst = shift + jnp.arange(chunks_per_tile, dtype=jnp.int32)[None, :]
    n_row_tiles = n_rows // MOE_TILE
    n_used = e_end[-1] // chunks_per_rows
    j = jnp.minimum(jnp.arange(n_row_tiles, dtype=jnp.int32), jnp.maximum(n_used - 1, 0))
    tile_expert = jnp.sum((e_end[None, :] // chunks_per_rows <= j[:, None]).astype(jnp.int32), axis=1)
    tile_expert = jnp.minimum(tile_expert, MOE_EXPERTS - 1)
    i32 = lambda v: v.astype(jnp.int32)
    return (i32(dst.reshape(-1)), i32(used), i32(e_start + total), i32(padded - total),
            i32(n_used.reshape(1)), i32(tile_expert))


def _chunk_rows(chunk):
    return pl.ds(pl.multiple_of(chunk * MOE_CHUNK, MOE_CHUNK), MOE_CHUNK)


def _run_copies(*groups):
    for wait in (False, True):
        for lo, hi, make_copy in groups:
            def body(k, carry, wait=wait, make_copy=make_copy):
                cp = make_copy(k)
                cp.wait() if wait else cp.start()
                return carry
            lax.fori_loop(lo, hi, body, 0)


def _scatter_chunks_kernel(dst_ref, used_ref, pad_start_ref, pad_n_ref, nused_ref, src_ref, o_ref,
                           zero_ref, sem):
    i = pl.program_id(0)
    base = i * (MOE_SLOTS // MOE_CHUNK)

    @pl.when(i == 0)
    def _():
        zero_ref[...] = jnp.zeros_like(zero_ref)
        pads = [(0, pad_n_ref[e], lambda c, e=e: pltpu.make_async_copy(
            zero_ref.at[pl.ds(0, MOE_CHUNK), :], o_ref.at[_chunk_rows(pad_start_ref[e] + c), :], sem.at[1]))
            for e in range(MOE_EXPERTS)]
        tail = (nused_ref[0], o_ref.shape[0] // MOE_TILE, lambda j: pltpu.make_async_copy(
            zero_ref, o_ref.at[pl.ds(pl.multiple_of(j * MOE_TILE, MOE_TILE), MOE_TILE), :], sem.at[1]))
        _run_copies(tail, *pads)

    _run_copies((0, used_ref[i], lambda c: pltpu.make_async_copy(
        src_ref.at[_chunk_rows(c), :], o_ref.at[_chunk_rows(dst_ref[base + c]), :], sem.at[0])))


def _scatter_chunks(xs_local, dst, used, pad_start, pad_n, n_used, n_rows):
    d = xs_local.shape[1]
    return pl.pallas_call(
        _scatter_chunks_kernel,
        grid_spec=pltpu.PrefetchScalarGridSpec(
            num_scalar_prefetch=5,
            grid=(used.shape[0],),
            in_specs=[pl.BlockSpec((MOE_SLOTS, d), lambda i, *_: (i, 0))],
            out_specs=pl.BlockSpec(memory_space=pl.ANY),
            scratch_shapes=[pltpu.VMEM((MOE_TILE, d), BF16), pltpu.SemaphoreType.DMA((2,))]),
        out_shape=jax.ShapeDtypeStruct((n_rows, d), BF16),
        compiler_params=_params("arbitrary"),
        name="moe_scatter_chunks",
    )(dst, used, pad_start, pad_n, n_used, xs_local)


def _experts_kernel(te_ref, nt_ref, xs_ref, wg_ref, wu_ref, wd_ref, o_ref):
    used = pl.program_id(0) < nt_ref[0]

    @pl.when(used)
    def _():
        x = xs_ref[...]
        gate = jnp.dot(x, wg_ref[0].astype(BF16), preferred_element_type=F32)
        up = jnp.dot(x, wu_ref[0].astype(BF16), preferred_element_type=F32)
        hid = (_silu(gate) * up).astype(BF16)
        o_ref[...] = jnp.dot(hid, wd_ref[0].astype(BF16), preferred_element_type=F32).astype(BF16)

    @pl.when(jnp.logical_not(used))
    def _():
        o_ref[...] = jnp.zeros_like(o_ref)


def _experts(xs, tile_expert, n_used, w_gate, w_up, w_down):
    n_rows, d = xs.shape
    tile = MOE_TILE
    by_expert = lambda j, te, nt: (te[j], 0, 0)
    return pl.pallas_call(
        _experts_kernel,
        grid_spec=pltpu.PrefetchScalarGridSpec(
            num_scalar_prefetch=2,
            grid=(n_rows // tile,),
            in_specs=[pl.BlockSpec((tile, d), lambda j, te, nt: (jnp.maximum(jnp.minimum(j, nt[0] - 1), 0), 0)),
                      pl.BlockSpec((1, d, MOE_FF), by_expert),
                      pl.BlockSpec((1, d, MOE_FF), by_expert),
                      pl.BlockSpec((1, MOE_FF, d), by_expert)],
            out_specs=pl.BlockSpec((tile, d), lambda j, te, nt: (j, 0))),
        out_shape=jax.ShapeDtypeStruct((n_rows, d), BF16),
        compiler_params=_params("arbitrary"),
        name="moe_experts",
    )(tile_expert, n_used, xs, w_gate, w_up, w_down)


def _combine_kernel(dst_ref, used_ref, wslot_ref, col_ref, x1_ref, mod_ref, gpost_ref, ys_ref, o_ref,
                    ysl_ref, sem):
    i = pl.program_id(0)
    chunks = MOE_SLOTS // MOE_CHUNK
    slot = lax.rem(i, 2)

    def fetch(tile, buf, wait):
        base = tile * chunks
        used = used_ref[tile]

        def copy(c, carry):
            cp = pltpu.make_async_copy(ys_ref.at[_chunk_rows(dst_ref[base + c]), :],
                                       ysl_ref.at[buf, _chunk_rows(c), :], sem.at[buf])
            cp.wait() if wait else cp.start()
            return carry

        def clear(c, carry):
            ysl_ref[buf, _chunk_rows(c), :] = jnp.zeros((MOE_CHUNK, ysl_ref.shape[2]), BF16)
            return carry

        lax.fori_loop(0, used, copy, 0)
        if not wait:
            lax.fori_loop(used, chunks, clear, 0)

    @pl.when(i == 0)
    def _():
        fetch(0, 0, False)

    @pl.when(i + 1 < pl.num_programs(0))
    def _():
        fetch(i + 1, 1 - slot, False)

    col = col_ref[...]
    tm = col.shape[0]
    lane = lax.broadcasted_iota(jnp.int32, (tm, MOE_SLOTS), 1)
    hit = jnp.where(lane == col[:, 0:1].astype(jnp.int32), 1.0,
                    jnp.where(lane == col[:, 1:2].astype(jnp.int32), 1.0, 0.0)).astype(BF16)
    fetch(i, slot, True)
    w_col = jnp.broadcast_to(wslot_ref[0:1, :], (LANES, MOE_SLOTS)).T[:, 0:1]
    ysw = (ysl_ref[slot].astype(F32) * w_col).astype(BF16)
    y = jnp.dot(hit, ysw, preferred_element_type=F32)
    g2 = mod_ref[0, 5:6, :]
    o_ref[...] = x1_ref[...] + g2 * _rms(y, gpost_ref[...])


def _combine(ys, dst, used, wslot, col, x1, mod3, gpost, seq, tm):
    t, d = x1.shape
    per_b = seq // tm
    row = lambda i, *_: (i, 0)
    return pl.pallas_call(
        _combine_kernel,
        grid_spec=pltpu.PrefetchScalarGridSpec(
            num_scalar_prefetch=2,
            grid=(t // tm,),
            in_specs=[pl.BlockSpec((8, MOE_SLOTS), row),
                      pl.BlockSpec((tm, LANES), row),
                      pl.BlockSpec((tm, d), row),
                      pl.BlockSpec((1, N_MOD, d), lambda i, *_: (i // per_b, 0, 0)),
                      pl.BlockSpec((1, d), lambda i, *_: (0, 0)),
                      pl.BlockSpec(memory_space=pl.ANY)],
            out_specs=pl.BlockSpec((tm, d), row),
            scratch_shapes=[pltpu.VMEM((2, MOE_SLOTS, d), BF16), pltpu.SemaphoreType.DMA((2,))]),
        out_shape=jax.ShapeDtypeStruct((t, d), F32),
        compiler_params=_params("arbitrary"),
        name="moe_combine",
    )(dst, used, wslot, col, x1, mod3, gpost, ys)


def kernel(x, c, positions, ada_w, ada_b, pre_norm_mix, post_norm_mix, pre_norm_ffn, post_norm_ffn, w_in, conv_w, conv_b, dt_bias, a_log, d_skip, ssd_norm, q_norm, w_uq, kv_norm, w_ukv, mla_out_norm, w_out, w_group_router, b_group_router, w_expert_router, b_expert_router, w_gate, w_up, w_down):
    batch, seq, d = x.shape
    t = batch * seq
    depth = ada_w.shape[0]
    tm = min(512, seq)
    half = MLA_ROPE // 2

    cos_t, sin_t = _rope_tables(positions)

    x2 = x.reshape(t, d)
    for l in range(depth):
        mod3 = _modulation(c, ada_w[l], ada_b[l]).reshape(batch, N_MOD, d)
        w_in_r, wq2, wk, wv_t = _inproj_weights(w_in[l], w_uq[l], w_ukv[l])
        z, xbc, dt_raw, q, k, v_t = _inproj(
            x2, mod3, pre_norm_mix[l].reshape(1, d), w_in_r, q_norm[l].reshape(1, -1), wq2,
            kv_norm[l].reshape(1, -1), wk, wv_t, cos_t, sin_t, seq, tm)
        y_ssd = _ssd(xbc, z, dt_raw, conv_w[l], conv_b[l], dt_bias[l], a_log[l], d_skip[l],
                     ssd_norm[l], batch, seq)
        y_mla = _attention(q, k, v_t, mla_out_norm[l].reshape(1, -1), batch, seq)
        w_o = w_out[l].astype(BF16)
        pad_r = LANES - MOE_GROUPS - MOE_EXPERTS
        w_router = jnp.concatenate([w_group_router[l], w_expert_router[l], jnp.zeros((d, pad_r), F32)], axis=1)
        b_router = jnp.concatenate([b_group_router[l].reshape(-1), b_expert_router[l].reshape(-1),
                                    jnp.zeros((pad_r,), F32)]).reshape(1, LANES)
        wr_hi = w_router.astype(BF16)
        wr_lo = (w_router - wr_hi.astype(F32)).astype(BF16)
        x1, xs_local, col, wslot, cnt_tiles, chunk_expert = _outproj(
            y_ssd, y_mla, x2, mod3, w_o[:SSD_INNER], w_o[SSD_INNER:], post_norm_mix[l].reshape(1, d),
            pre_norm_ffn[l].reshape(1, d), wr_hi, wr_lo, b_router, seq, MOE_TOKENS)
        n_tok_tiles = t // MOE_TOKENS
        n_rows = _moe_rows(n_tok_tiles)
        dst, used, pad_start, pad_n, n_used, tile_expert = _moe_plan(cnt_tiles, chunk_expert, n_tok_tiles, n_rows)
        xs = _scatter_chunks(xs_local, dst, used, pad_start, pad_n, n_used, n_rows)
        ys = _experts(xs, tile_expert, n_used, w_gate[l], w_up[l], w_down[l])
        x2 = _combine(ys, dst, used, wslot, col, x1, mod3, post_norm_ffn[l].reshape(1, d), seq, MOE_TOKENS)
    return x2.reshape(batch, seq, d)
```

```python
import functools
import math

import jax
import jax.numpy as jnp
from jax import lax
from jax.experimental import pallas as pl
from jax.experimental.pallas import tpu as pltpu

F32 = jnp.float32
BF16 = jnp.bfloat16

D_MODEL = 1024
SSD_HEADS = 8
SSD_HEAD_DIM = 64
SSD_INNER = SSD_HEADS * SSD_HEAD_DIM
SSD_GROUPS = 2
SSD_STATE = 128
SSD_CONV = 4
SSD_CHUNK = 128
SSD_XBC = SSD_INNER + 2 * SSD_GROUPS * SSD_STATE
MLA_HEADS = 8
MLA_NOPE = 64
MLA_ROPE = 32
MLA_QK = MLA_NOPE + MLA_ROPE
MLA_V = 64
MLA_Q_RANK = 256
MLA_KV_RANK = 128
MLA_INNER = MLA_HEADS * MLA_V
ROPE_THETA = 10000.0
MOE_GROUPS = 4
MOE_EXPERTS_PER_GROUP = 8
MOE_EXPERTS = MOE_GROUPS * MOE_EXPERTS_PER_GROUP
MOE_FF = 256
N_MOD = 6
EPS = 1e-6
LOG2_E = math.log2(math.e)

LANES = 128
HEAD_PAD = 128
ATT_BLOCK = 256
MLA_V_ROWS = MLA_V + 16
SSD_STEP_ROWS = 4 * SSD_CHUNK
MOE_TOKENS = 512
MOE_CHUNK = 16
MOE_TILE = 512
MOE_SLOTS = 2 * MOE_TOKENS + MOE_EXPERTS * MOE_CHUNK
VMEM_LIMIT = 48 * 1024 * 1024

_C_Z = 0
_C_XBC = _C_Z + SSD_INNER
_C_CQ = _C_XBC + SSD_XBC
_C_CKV = _C_CQ + MLA_Q_RANK
_C_KR = _C_CKV + MLA_KV_RANK
_C_KRS = _C_KR + LANES
_C_DT = _C_KRS + LANES
_C_END = _C_DT + LANES


def _silu(v):
    return v * (1.0 / (1.0 + jnp.exp(-v)))


def _rms(v, gain):
    return v * lax.rsqrt(jnp.mean(v * v, axis=-1, keepdims=True) + EPS) * gain


def _params(*sem, flags=None):
    return pltpu.CompilerParams(dimension_semantics=sem, vmem_limit_bytes=VMEM_LIMIT, flags=flags)


def _mod_kernel(c_ref, w_ref, b_ref, o_ref):
    act = _silu(c_ref[...])
    a_hi = act.astype(BF16)
    a_lo = (act - a_hi.astype(F32)).astype(BF16)
    w = w_ref[...]
    w_hi = w.astype(BF16)
    w_lo = (w - w_hi.astype(F32)).astype(BF16)
    o_ref[...] = (jnp.dot(a_hi, w_hi, preferred_element_type=F32)
                  + jnp.dot(a_lo, w_hi, preferred_element_type=F32)
                  + jnp.dot(a_hi, w_lo, preferred_element_type=F32)) + b_ref[...]


def _modulation(c, ada_w, ada_b):
    b, d = c.shape
    n = ada_w.shape[1]
    return pl.pallas_call(
        _mod_kernel,
        grid=(n // d,),
        in_specs=[pl.BlockSpec((b, d), lambda j: (0, 0)),
                  pl.BlockSpec((d, d), lambda j: (0, j)),
                  pl.BlockSpec((1, d), lambda j: (0, j))],
        out_specs=pl.BlockSpec((b, d), lambda j: (0, j)),
        out_shape=jax.ShapeDtypeStruct((b, n), F32),
        compiler_params=_params("arbitrary"),
        name="modulation",
    )(c, ada_w, ada_b.reshape(1, n))


def _rope_kernel(pos_ref, inv_ref, cos_ref, sin_ref):
    ang = pos_ref[...].astype(F32) * inv_ref[...]
    blk = ang.shape[0]
    half = MLA_ROPE // 2
    per_row = LANES // half
    lane = lax.broadcasted_iota(jnp.int32, (blk * per_row, LANES), 1)
    first = (lane >= MLA_NOPE) & (lane < MLA_NOPE + half)
    second = (lane >= MLA_NOPE + half) & (lane < MLA_QK)
    for fn, out_ref, fill in ((jnp.cos, cos_ref, 1.0), (jnp.sin, sin_ref, 0.0)):
        compact = fn(ang)
        rep = jnp.broadcast_to(compact[:, None, :], (blk, per_row, LANES)).reshape(blk * per_row, LANES)
        a = pltpu.roll(rep, 0, 1, stride=half, stride_axis=0)
        b = pltpu.roll(a, half, 1)
        out_ref[...] = jnp.where(first, a, jnp.where(second, b, fill))


def _rope_tables(positions):
    t = positions.size
    half = MLA_ROPE // 2
    per_row = LANES // half
    inv = 1.0 / (ROPE_THETA ** (jnp.arange(0, MLA_ROPE, 2, dtype=F32) / MLA_ROPE))
    group_token = (MLA_NOPE // half - jnp.arange(per_row)) % per_row
    pos_rep = jnp.repeat(positions.reshape(t // per_row, per_row)[:, group_token], half, axis=1)
    inv_t = jnp.tile(inv, per_row).reshape(1, LANES)
    rows = t // per_row
    blk = min(rows, 512)
    return pl.pallas_call(
        _rope_kernel,
        grid=(rows // blk,),
        in_specs=[pl.BlockSpec((blk, LANES), lambda i: (i, 0)),
                  pl.BlockSpec((1, LANES), lambda i: (0, 0))],
        out_specs=[pl.BlockSpec((blk * per_row, LANES), lambda i: (i, 0))] * 2,
        out_shape=[jax.ShapeDtypeStruct((t, LANES), F32)] * 2,
        compiler_params=_params("arbitrary"),
        name="rope_tables",
    )(pos_rep, inv_t)


def _inproj_kernel(x_ref, mod_ref, gpre_ref, win_ref, qn_ref, wq_ref, kvn_ref, wk_ref, wvt_ref,
                   cos_ref, sin_ref, z_ref, xbc_ref, dt_ref, q_ref, k_ref, vt_ref):
    x = x_ref[...]
    sh = mod_ref[0, 0:1, :]
    sc = mod_ref[0, 1:2, :]
    h = (_rms(x, gpre_ref[...]) * (1.0 + sc) + sh).astype(BF16)
    u = jnp.dot(h, win_ref[...], preferred_element_type=F32)
    z_ref[...] = u[:, _C_Z:_C_XBC].astype(BF16)
    xbc_ref[...] = u[:, _C_XBC:_C_CQ].astype(BF16)
    dt_ref[...] = u[:, _C_DT:_C_END]
    cos_t = cos_ref[...]
    sin_t = sin_ref[...]
    cq = _rms(u[:, _C_CQ:_C_CKV], qn_ref[...]).astype(BF16)
    q2 = jnp.dot(cq, wq_ref[...], preferred_element_type=F32)
    scale = MLA_QK ** -0.5 * LOG2_E
    nq = MLA_HEADS * HEAD_PAD
    for hd in range(MLA_HEADS):
        a = q2[:, hd * HEAD_PAD:(hd + 1) * HEAD_PAD]
        b = q2[:, nq + hd * HEAD_PAD:nq + (hd + 1) * HEAD_PAD]
        q_ref[:, hd * HEAD_PAD:(hd + 1) * HEAD_PAD] = ((a * cos_t + b * sin_t) * scale).astype(BF16)
    ckv = _rms(u[:, _C_CKV:_C_KR], kvn_ref[...]).astype(BF16)
    kn = jnp.dot(ckv, wk_ref[...], preferred_element_type=F32)
    k_pe = u[:, _C_KR:_C_KRS] * cos_t + u[:, _C_KRS:_C_DT] * sin_t
    for hd in range(MLA_HEADS):
        k_ref[:, hd * HEAD_PAD:(hd + 1) * HEAD_PAD] = (
            kn[:, hd * HEAD_PAD:(hd + 1) * HEAD_PAD] + k_pe).astype(BF16)
    v_t = lax.dot_general(wvt_ref[...], ckv, (((1,), (1,)), ((), ())), preferred_element_type=F32)
    head_row = lax.rem(lax.broadcasted_iota(jnp.int32, (v_t.shape[0], 1), 0), MLA_V_ROWS)
    v_t = v_t + jnp.where(head_row == MLA_V, 1.0, 0.0)
    for s in range(vt_ref.shape[0]):
        vt_ref[s] = v_t[:, s * ATT_BLOCK:(s + 1) * ATT_BLOCK].astype(BF16)


def _inproj_weights(w_in, w_uq, w_ukv):
    d = w_in.shape[0]
    half = MLA_ROPE // 2
    o_z, o_xbc, o_dt = 0, SSD_INNER, SSD_INNER + SSD_XBC
    o_cq = o_dt + SSD_HEADS
    o_ckv = o_cq + MLA_Q_RANK
    o_kr = o_ckv + MLA_KV_RANK
    zeros = lambda n: jnp.zeros((d, n), F32)
    kr = w_in[:, o_kr:o_kr + MLA_ROPE]
    kr_blk = jnp.concatenate([zeros(MLA_NOPE), kr, zeros(HEAD_PAD - MLA_QK)], axis=1)
    krs_blk = jnp.concatenate([zeros(MLA_NOPE), -kr[:, half:], kr[:, :half], zeros(HEAD_PAD - MLA_QK)], axis=1)
    dt_blk = jnp.concatenate([w_in[:, o_dt:o_dt + SSD_HEADS], zeros(LANES - SSD_HEADS)], axis=1)
    w_in_r = jnp.concatenate([w_in[:, o_z:o_dt], w_in[:, o_cq:o_kr], kr_blk, krs_blk, dt_blk], axis=1)
    r = w_uq.shape[0]
    zq = jnp.zeros((r, MLA_HEADS, HEAD_PAD - MLA_QK), F32)
    zn = jnp.zeros((r, MLA_HEADS, MLA_NOPE), F32)
    wq_plain = jnp.concatenate([w_uq, zq], axis=2).reshape(r, MLA_HEADS * HEAD_PAD)
    wq_rot = jnp.concatenate([zn, -w_uq[:, :, MLA_NOPE + half:], w_uq[:, :, MLA_NOPE:MLA_NOPE + half], zq],
                             axis=2).reshape(r, MLA_HEADS * HEAD_PAD)
    wq2 = jnp.concatenate([wq_plain, wq_rot], axis=1)
    rk = w_ukv.shape[0]
    zk = jnp.zeros((rk, MLA_HEADS, HEAD_PAD - MLA_NOPE), F32)
    wk = jnp.concatenate([w_ukv[:, :, :MLA_NOPE], zk], axis=2).reshape(rk, MLA_HEADS * HEAD_PAD)
    zv = jnp.zeros((rk, MLA_HEADS, MLA_V_ROWS - MLA_V), F32)
    wv_t = jnp.concatenate([w_ukv[:, :, MLA_NOPE:], zv], axis=2).reshape(rk, MLA_HEADS * MLA_V_ROWS).T
    return w_in_r.astype(BF16), wq2.astype(BF16), wk.astype(BF16), wv_t.astype(BF16)


def _inproj(x2, mod3, gpre, w_in_r, q_norm, wq2, kv_norm, wk, wv_t, cos_t, sin_t, seq, tm):
    t, d = x2.shape
    per_b = seq // tm
    row = lambda i: (i, 0)
    const = lambda i: (0, 0)
    nqk = MLA_HEADS * HEAD_PAD
    slabs = tm // ATT_BLOCK
    return pl.pallas_call(
        _inproj_kernel,
        grid=(t // tm,),
        in_specs=[pl.BlockSpec((tm, d), row),
                  pl.BlockSpec((1, N_MOD, d), lambda i: (i // per_b, 0, 0)),
                  pl.BlockSpec((1, d), const),
                  pl.BlockSpec(w_in_r.shape, const),
                  pl.BlockSpec((1, MLA_Q_RANK), const),
                  pl.BlockSpec(wq2.shape, const),
                  pl.BlockSpec((1, MLA_KV_RANK), const),
                  pl.BlockSpec(wk.shape, const),
                  pl.BlockSpec(wv_t.shape, const),
                  pl.BlockSpec((tm, LANES), row),
                  pl.BlockSpec((tm, LANES), row)],
        out_specs=[pl.BlockSpec((tm, SSD_INNER), row),
                   pl.BlockSpec((tm, SSD_XBC), row),
                   pl.BlockSpec((tm, LANES), row),
                   pl.BlockSpec((tm, nqk), row),
                   pl.BlockSpec((tm, nqk), row),
                   pl.BlockSpec((slabs, MLA_HEADS * MLA_V_ROWS, ATT_BLOCK), lambda i: (i, 0, 0))],
        out_shape=[jax.ShapeDtypeStruct((t, SSD_INNER), BF16),
                   jax.ShapeDtypeStruct((t, SSD_XBC), BF16),
                   jax.ShapeDtypeStruct((t, LANES), F32),
                   jax.ShapeDtypeStruct((t, nqk), BF16),
                   jax.ShapeDtypeStruct((t, nqk), BF16),
                   jax.ShapeDtypeStruct((t // ATT_BLOCK, MLA_HEADS * MLA_V_ROWS, ATT_BLOCK), BF16)],
        compiler_params=_params("arbitrary"),
        name="inproj",
    )(x2, mod3, gpre, w_in_r, q_norm, wq2, kv_norm, wk, wv_t, cos_t, sin_t)


def _split3_packed(v):
    lane = lax.broadcasted_iota(jnp.int32, v.shape, 1)
    v = jnp.where(lane < SSD_HEADS, v, 0.0)
    hi = v.astype(BF16).astype(F32)
    rest = v - hi
    mid = rest.astype(BF16).astype(F32)
    lo = rest - mid
    return (hi + pltpu.roll(mid, SSD_HEADS, 1) + pltpu.roll(lo, 2 * SSD_HEADS, 1)).astype(BF16)


def _ssd_kernel(xbc_ref, z_ref, dt_ref, cw_ref, cb_ref, dtb_ref, alog_ref, dsk_ref, gn_ref,
                e_ref, y_ref, tail_ref, state_ref):
    @pl.when(pl.program_id(1) == 0)
    def _():
        tail_ref[...] = jnp.zeros_like(tail_ref)
        state_ref[...] = jnp.zeros_like(state_ref)

    for c in range(xbc_ref.shape[0] // SSD_CHUNK):
        _ssd_chunk(pl.ds(c * SSD_CHUNK, SSD_CHUNK), xbc_ref, z_ref, dt_ref, cw_ref, cb_ref, dtb_ref,
                   alog_ref, dsk_ref, gn_ref, e_ref, y_ref, tail_ref, state_ref)


def _ssd_chunk(rows, xbc_ref, z_ref, dt_ref, cw_ref, cb_ref, dtb_ref, alog_ref, dsk_ref, gn_ref,
               e_ref, y_ref, tail_ref, state_ref):
    L = SSD_CHUNK
    gw = SSD_INNER // SSD_GROUPS

    cur = xbc_ref[rows, :].astype(F32)
    prev8 = tail_ref[...]
    row8 = lax.broadcasted_iota(jnp.int32, prev8.shape, 0)
    acc = cur * cw_ref[SSD_CONV - 1:SSD_CONV, :] + cb_ref[...]
    for j in range(1, SSD_CONV):
        r = pltpu.roll(cur, j, 0)
        head = jnp.where(row8 < j, pltpu.roll(prev8, j, 0), r[0:8])
        r = jnp.concatenate([head, r[8:]], axis=0)
        acc = acc + r * cw_ref[SSD_CONV - 1 - j:SSD_CONV - j, :]
    tail_ref[...] = cur[L - 8:L]
    act = _silu(acc)
    xs = act[:, :SSD_INNER]
    bm = act[:, SSD_INNER:SSD_INNER + SSD_GROUPS * SSD_STATE].astype(BF16)
    cm = act[:, SSD_INNER + SSD_GROUPS * SSD_STATE:].astype(BF16)

    dt_in = dt_ref[rows, :] + dtb_ref[...]
    dt = jnp.maximum(dt_in, 0.0) + jnp.log(1.0 + jnp.exp(-jnp.abs(dt_in)))
    adt = dt * (-jnp.exp(alog_ref[...]))
    ri = lax.broadcasted_iota(jnp.int32, (L, L), 0)
    ci = lax.broadcasted_iota(jnp.int32, (L, L), 1)
    causal = ci <= ri
    acs_p = jnp.dot(causal.astype(BF16), _split3_packed(adt), preferred_element_type=F32)
    a_cs = (acs_p + pltpu.roll(acs_p, LANES - SSD_HEADS, 1)
            + pltpu.roll(acs_p, LANES - 2 * SSD_HEADS, 1))
    a_cs_t = a_cs.T
    cs_all = jnp.dot(_split3_packed(a_cs), e_ref[...], preferred_element_type=F32)
    cs64 = cs_all[:, :SSD_INNER]
    cs128 = cs_all[:, SSD_INNER:]
    dt64 = jnp.dot(_split3_packed(dt), e_ref[:, :SSD_INNER], preferred_element_type=F32)

    xd = xs * dt64
    xd_b = xd.astype(BF16)
    last = cs64[L - 1:L, :]
    xdw = (xd * jnp.exp(last - cs64)).astype(BF16)
    chunk_decay = jnp.exp(last)
    in_decay = jnp.exp(cs64)

    lane = lax.broadcasted_iota(jnp.int32, (L, LANES), 1)
    lo = lane < SSD_HEAD_DIM
    zero_b = jnp.zeros((L, LANES), BF16)
    y_parts = []
    new_states = []
    for g in range(SSD_GROUPS):
        bg = bm[:, g * SSD_STATE:(g + 1) * SSD_STATE]
        cg = cm[:, g * SSD_STATE:(g + 1) * SSD_STATE]
        cb = lax.dot_general(cg, bg, (((1,), (1,)), ((), ())), preferred_element_type=F32)
        prev = state_ref[:, g * gw:(g + 1) * gw]
        y_off = jnp.dot(cg, prev.astype(BF16), preferred_element_type=F32) * in_decay[:, g * gw:(g + 1) * gw]
        st = lax.dot_general(bg, xdw[:, g * gw:(g + 1) * gw], (((0,), (0,)), ((), ())),
                             preferred_element_type=F32)
        new_states.append(prev * chunk_decay[:, g * gw:(g + 1) * gw] + st)
        heads_per_group = SSD_HEADS // SSD_GROUPS
        for pair in range(heads_per_group // 2):
            h0 = g * heads_per_group + 2 * pair
            blk = xd_b[:, h0 * SSD_HEAD_DIM:(h0 + 2) * SSD_HEAD_DIM]
            y_pair = None
            for k in range(2):
                hh = h0 + k
                diff = cs128[:, hh * LANES:(hh + 1) * LANES] - a_cs_t[hh:hh + 1, :]
                decay = jnp.where(causal, jnp.exp(diff), 0.0)
                m = (cb * decay).astype(BF16)
                rhs = jnp.where(lo, blk, zero_b) if k == 0 else jnp.where(lo, zero_b, blk)
                part = jnp.dot(m, rhs, preferred_element_type=F32)
                y_pair = part if y_pair is None else y_pair + part
            c0 = (2 * pair) * SSD_HEAD_DIM
            y_parts.append(y_pair + y_off[:, c0:c0 + LANES])
    state_ref[...] = jnp.concatenate(new_states, axis=1)
    y = jnp.concatenate(y_parts, axis=1) + dsk_ref[...] * xs
    gated = y * _silu(z_ref[rows, :].astype(F32))
    y_ref[rows, :] = _rms(gated, gn_ref[...]).astype(BF16)


def _ssd(xbc, z, dt_raw, conv_w, conv_b, dt_bias, a_log, d_skip, ssd_norm, batch, seq):
    t = xbc.shape[0]
    L = SSD_STEP_ROWS
    nc = seq // L
    row = lambda b, c: (b * nc + c, 0)
    const = lambda b, c: (0, 0)
    pad = lambda v: jnp.concatenate([v.reshape(1, -1), jnp.zeros((1, LANES - v.size), F32)], axis=1)
    head_of_lane = jnp.concatenate([jnp.arange(SSD_INNER) // SSD_HEAD_DIM,
                                    jnp.arange(SSD_HEADS * LANES) // LANES])
    rows = jnp.arange(LANES)[:, None]
    expand = ((rows < 3 * SSD_HEADS) & (rows % SSD_HEADS == head_of_lane[None, :])).astype(BF16)
    dsk = jnp.repeat(d_skip.astype(F32), SSD_HEAD_DIM).reshape(1, SSD_INNER)
    return pl.pallas_call(
        _ssd_kernel,
        grid=(batch, nc),
        in_specs=[pl.BlockSpec((L, SSD_XBC), row),
                  pl.BlockSpec((L, SSD_INNER), row),
                  pl.BlockSpec((L, LANES), row),
                  pl.BlockSpec((SSD_CONV, SSD_XBC), const),
                  pl.BlockSpec((1, SSD_XBC), const),
                  pl.BlockSpec((1, LANES), const),
                  pl.BlockSpec((1, LANES), const),
                  pl.BlockSpec((1, SSD_INNER), const),
                  pl.BlockSpec((1, SSD_INNER), const),
                  pl.BlockSpec(expand.shape, const)],
        out_specs=pl.BlockSpec((L, SSD_INNER), row),
        out_shape=jax.ShapeDtypeStruct((t, SSD_INNER), BF16),
        scratch_shapes=[pltpu.VMEM((8, SSD_XBC), F32),
                        pltpu.VMEM((SSD_STATE, SSD_INNER), F32)],
        compiler_params=_params("arbitrary", "arbitrary"),
        name="ssd",
    )(xbc, z, dt_raw, conv_w, conv_b.reshape(1, -1), pad(dt_bias), pad(a_log), dsk,
      ssd_norm.reshape(1, -1), expand)


def _attn_kernel(q_ref, k_ref, vt_ref, gn_ref, o_ref, s_ref):
    blk = ATT_BLOCK
    qi = pl.program_id(1)
    causal = (lax.broadcasted_iota(jnp.int32, (blk, blk), 0)
              <= lax.broadcasted_iota(jnp.int32, (blk, blk), 1))
    dims = (((1,), (1,)), ((), ()))
    qs = [q_ref[:, hd * HEAD_PAD:(hd + 1) * HEAD_PAD] for hd in range(MLA_HEADS)]

    def step(j, carry, masked):
        rows = pl.ds(pl.multiple_of(j * blk, blk), blk)
        m_new = []
        for hd in range(MLA_HEADS):
            kk = k_ref[rows, hd * HEAD_PAD:(hd + 1) * HEAD_PAD]
            s = lax.dot_general(kk, qs[hd], dims, preferred_element_type=F32)
            if masked:
                s = jnp.where(causal, s, -jnp.inf)
            s_ref[hd] = s
            m_new.append(jnp.maximum(carry[hd][0], jnp.max(s, axis=0, keepdims=True)))
        new = []
        for hd in range(MLA_HEADS):
            m_old, acc = carry[hd]
            p = jnp.exp2(s_ref[hd] - m_new[hd])
            alpha = jnp.exp2(m_old - m_new[hd])
            vt = vt_ref[j, hd * MLA_V_ROWS:(hd + 1) * MLA_V_ROWS, :]
            acc = acc * alpha + jnp.dot(vt, p.astype(BF16), preferred_element_type=F32)
            new.append((m_new[hd], acc))
        return tuple(new)

    neg = jnp.full((1, blk), -jnp.inf, F32)
    init = tuple((neg, jnp.zeros((MLA_V_ROWS, blk), F32)) for _ in range(MLA_HEADS))
    carry = lax.fori_loop(0, qi, functools.partial(step, masked=False), init)
    final = step(qi, carry, True)
    y_t = jnp.concatenate([acc[:MLA_V] / acc[MLA_V:MLA_V + 1] for _, acc in final], axis=0)
    o_ref[...] = _rms(y_t.T, gn_ref[...]).astype(BF16)


def _attention(q, k, v_t, gn, batch, seq):
    t = q.shape[0]
    blk = ATT_BLOCK
    nq = seq // blk
    nqk = MLA_HEADS * HEAD_PAD
    return pl.pallas_call(
        _attn_kernel,
        grid=(batch, nq),
        in_specs=[pl.BlockSpec((blk, nqk), lambda b, i: (b * nq + i, 0)),
                  pl.BlockSpec((seq, nqk), lambda b, i: (b, 0)),
                  pl.BlockSpec((nq, MLA_HEADS * MLA_V_ROWS, blk), lambda b, i: (b, 0, 0)),
                  pl.BlockSpec((1, MLA_INNER), lambda b, i: (0, 0))],
        out_specs=pl.BlockSpec((blk, MLA_INNER), lambda b, i: (b * nq + i, 0)),
        out_shape=jax.ShapeDtypeStruct((t, MLA_INNER), BF16),
        scratch_shapes=[pltpu.VMEM((MLA_HEADS, blk, blk), F32)],
        compiler_params=_params("arbitrary", "arbitrary"),
        name="attention",
    )(q, k, v_t, gn)


def _outproj_kernel(ys_ref, ym_ref, x_ref, mod_ref, wtop_ref, wbot_ref, gpm_ref, gpf_ref,
                    wrh_ref, wrl_ref, br_ref, x1_ref, xs_ref, col_ref, wslot_ref, cnt_ref, ce_ref):
    y = (jnp.dot(ys_ref[...], wtop_ref[...], preferred_element_type=F32)
         + jnp.dot(ym_ref[...], wbot_ref[...], preferred_element_type=F32))
    g1 = mod_ref[0, 2:3, :]
    sh2 = mod_ref[0, 3:4, :]
    sc2 = mod_ref[0, 4:5, :]
    x1 = x_ref[...] + g1 * _rms(y, gpm_ref[...])
    x1_ref[...] = x1
    h2 = _rms(x1, gpf_ref[...]) * (1.0 + sc2) + sh2
    tm = h2.shape[0]
    h_hi = h2.astype(BF16)
    h_lo = (h2 - h_hi.astype(F32)).astype(BF16)
    logits = (jnp.dot(h_hi, wrh_ref[...], preferred_element_type=F32)
              + jnp.dot(h_lo, wrh_ref[...], preferred_element_type=F32)
              + jnp.dot(h_hi, wrl_ref[...], preferred_element_type=F32)) + br_ref[...]
    lt = logits.T
    row = lax.broadcasted_iota(jnp.int32, lt.shape, 0).astype(F32)
    ninf = -jnp.inf
    big = 1e9
    is_g = row < MOE_GROUPS
    gl = jnp.where(is_g, lt, ninf)
    gmax = jnp.max(gl, axis=0, keepdims=True)
    gsum = jnp.sum(jnp.where(is_g, jnp.exp(lt - gmax), 0.0), axis=0, keepdims=True)
    g_w = 1.0 / gsum
    g_idx = jnp.min(jnp.where(gl == gmax, row, big), axis=0, keepdims=True)
    first = MOE_GROUPS + MOE_EXPERTS_PER_GROUP * g_idx
    el = jnp.where(row >= first, jnp.where(row < first + MOE_EXPERTS_PER_GROUP, lt, ninf), ninf)
    m1 = jnp.max(el, axis=0, keepdims=True)
    i1 = jnp.min(jnp.where(el == m1, row, big), axis=0, keepdims=True)
    el2 = jnp.where(row == i1, ninf, el)
    m2 = jnp.max(el2, axis=0, keepdims=True)
    i2 = jnp.min(jnp.where(el2 == m2, row, big), axis=0, keepdims=True)
    r = jnp.exp(m2 - m1)
    w1 = g_w / (1.0 + r)
    w2 = g_w * r / (1.0 + r)
    sel1 = row == i1
    sel2 = row == i2
    both = jnp.where(sel1, 1.0, jnp.where(sel2, 1.0, 0.0))
    cnt16 = jnp.floor((jnp.sum(both, axis=1, keepdims=True) + (MOE_CHUNK - 1)) * (1.0 / MOE_CHUNK))
    er = lax.broadcasted_iota(jnp.int32, (LANES, LANES), 0)
    ec = lax.broadcasted_iota(jnp.int32, (LANES, LANES), 1)
    cnt16_b = jnp.broadcast_to(cnt16, (LANES, LANES))
    off16 = jnp.dot((ec < er).astype(BF16), cnt16_b.astype(BF16), preferred_element_type=F32)[:, 0:1]
    earlier = (lax.broadcasted_iota(jnp.int32, (tm, tm), 0)
               < lax.broadcasted_iota(jnp.int32, (tm, tm), 1)).astype(BF16)
    base = off16 * MOE_CHUNK + jnp.dot(both.astype(BF16), earlier, preferred_element_type=F32)
    lpos1 = jnp.sum(jnp.where(sel1, base, 0.0), axis=0, keepdims=True)
    lpos2 = jnp.sum(jnp.where(sel2, base, 0.0), axis=0, keepdims=True)
    srow = lax.broadcasted_iota(jnp.int32, (MOE_SLOTS, tm), 0)
    hit1 = srow == lpos1.astype(jnp.int32)
    hit2 = srow == lpos2.astype(jnp.int32)
    perm = jnp.where(hit1, 1.0, jnp.where(hit2, 1.0, 0.0)).astype(BF16)
    xs_ref[...] = jnp.dot(perm, h_hi, preferred_element_type=F32).astype(BF16)
    wslot = jnp.sum(jnp.where(hit1, w1, jnp.where(hit2, w2, 0.0)), axis=1, keepdims=True)
    wslot_ref[...] = jnp.broadcast_to(wslot, wslot_ref.shape)
    col_ref[...] = jnp.where(row == 0, lpos1, jnp.where(row == 1, lpos2, 0.0)).T
    cnt_ref[...] = cnt16_b
    chunk = lax.broadcasted_iota(jnp.int32, (LANES, LANES), 1).astype(F32)
    ce = jnp.sum(jnp.where(off16 + cnt16 <= chunk, 1.0, 0.0), axis=0, keepdims=True) - MOE_GROUPS
    ce_ref[...] = jnp.broadcast_to(ce, ce_ref.shape)


def _outproj(y_ssd, y_mla, x2, mod3, w_top, w_bot, gpm, gpf, wr_hi, wr_lo, b_router, seq, tm):
    t, d = x2.shape
    per_b = seq // tm
    n_tok_tiles = t // tm
    row = lambda i: (i, 0)
    const = lambda i: (0, 0)
    return pl.pallas_call(
        _outproj_kernel,
        grid=(n_tok_tiles,),
        in_specs=[pl.BlockSpec((tm, SSD_INNER), row),
                  pl.BlockSpec((tm, MLA_INNER), row),
                  pl.BlockSpec((tm, d), row),
                  pl.BlockSpec((1, N_MOD, d), lambda i: (i // per_b, 0, 0)),
                  pl.BlockSpec(w_top.shape, const),
                  pl.BlockSpec(w_bot.shape, const),
                  pl.BlockSpec((1, d), const),
                  pl.BlockSpec((1, d), const),
                  pl.BlockSpec(wr_hi.shape, const),
                  pl.BlockSpec(wr_lo.shape, const),
                  pl.BlockSpec((1, LANES), const)],
        out_specs=[pl.BlockSpec((tm, d), row),
                   pl.BlockSpec((MOE_SLOTS, d), row),
                   pl.BlockSpec((tm, LANES), row),
                   pl.BlockSpec((MOE_SLOTS, LANES), row),
                   pl.BlockSpec((LANES, LANES), row),
                   pl.BlockSpec((8, LANES), row)],
        out_shape=[jax.ShapeDtypeStruct((t, d), F32),
                   jax.ShapeDtypeStruct((n_tok_tiles * MOE_SLOTS, d), BF16),
                   jax.ShapeDtypeStruct((t, LANES), F32),
                   jax.ShapeDtypeStruct((n_tok_tiles * MOE_SLOTS, LANES), F32),
                   jax.ShapeDtypeStruct((n_tok_tiles * LANES, LANES), F32),
                   jax.ShapeDtypeStruct((n_tok_tiles * 8, LANES), F32)],
        compiler_params=_params("arbitrary"),
        name="outproj_router",
    )(y_ssd, y_mla, x2, mod3, w_top, w_bot, gpm, gpf, wr_hi, wr_lo, b_router)


def _moe_rows(n_tok_tiles):
    rows = n_tok_tiles * MOE_SLOTS + MOE_EXPERTS * (MOE_TILE - MOE_CHUNK)
    return (rows + MOE_TILE - 1) // MOE_TILE * MOE_TILE


def _moe_plan(cnt_tiles, chunk_expert, n_tok_tiles, n_rows):
    chunks_per_tile = MOE_SLOTS // MOE_CHUNK
    chunks_per_rows = MOE_TILE // MOE_CHUNK
    cnt = cnt_tiles.reshape(n_tok_tiles, LANES, LANES)[:, MOE_GROUPS:MOE_GROUPS + MOE_EXPERTS, 0]
    cnt = cnt.astype(jnp.int32)
    ce = chunk_expert.reshape(n_tok_tiles, 8, LANES)[:, 0, :chunks_per_tile].astype(jnp.int32)
    used = jnp.sum(cnt, axis=1)
    total = jnp.sum(cnt, axis=0)
    padded = (total + chunks_per_rows - 1) // chunks_per_rows * chunks_per_rows
    e_end = jnp.cumsum(padded)
    e_start = e_end - padded
    run_global = e_start[None, :] + jnp.cumsum(cnt, axis=0) - cnt
    run_local = jnp.cumsum(cnt, axis=1) - cnt
    experts = jnp.arange(MOE_EXPERTS, dtype=jnp.int32)
    shift = jnp.sum(jnp.where(ce[:, :, None] == experts, (run_global - run_local)[:, None, :], 0), axis=-1)
    dst = shift + jnp.arange(chunks_per_tile, dtype=jnp.int32)[None, :]
    n_row_tiles = n_rows // MOE_TILE
    n_used = e_end[-1] // chunks_per_rows
    j = jnp.minimum(jnp.arange(n_row_tiles, dtype=jnp.int32), jnp.maximum(n_used - 1, 0))
    tile_expert = jnp.sum((e_end[None, :] // chunks_per_rows <= j[:, None]).astype(jnp.int32), axis=1)
    tile_expert = jnp.minimum(tile_expert, MOE_EXPERTS - 1)
    i32 = lambda v: v.astype(jnp.int32)
    return (i32(dst.reshape(-1)), i32(used), i32(e_start + total), i32(padded - total),
            i32(n_used.reshape(1)), i32(tile_expert))


def _chunk_rows(chunk):
    return pl.ds(pl.multiple_of(chunk * MOE_CHUNK, MOE_CHUNK), MOE_CHUNK)


def _run_copies(*groups):
    for wait in (False, True):
        for lo, hi, make_copy in groups:
            def body(k, carry, wait=wait, make_copy=make_copy):
                cp = make_copy(k)
                cp.wait() if wait else cp.start()
                return carry
            lax.fori_loop(lo, hi, body, 0)


def _scatter_chunks_kernel(dst_ref, used_ref, pad_start_ref, pad_n_ref, nused_ref, src_ref, o_ref,
                           zero_ref, sem):
    i = pl.program_id(0)
    base = i * (MOE_SLOTS // MOE_CHUNK)

    @pl.when(i == 0)
    def _():
        zero_ref[...] = jnp.zeros_like(zero_ref)
        pads = [(0, pad_n_ref[e], lambda c, e=e: pltpu.make_async_copy(
            zero_ref.at[pl.ds(0, MOE_CHUNK), :], o_ref.at[_chunk_rows(pad_start_ref[e] + c), :], sem.at[1]))
            for e in range(MOE_EXPERTS)]
        tail = (nused_ref[0], o_ref.shape[0] // MOE_TILE, lambda j: pltpu.make_async_copy(
            zero_ref, o_ref.at[pl.ds(pl.multiple_of(j * MOE_TILE, MOE_TILE), MOE_TILE), :], sem.at[1]))
        _run_copies(tail, *pads)

    _run_copies((0, used_ref[i], lambda c: pltpu.make_async_copy(
        src_ref.at[_chunk_rows(c), :], o_ref.at[_chunk_rows(dst_ref[base + c]), :], sem.at[0])))


def _scatter_chunks(xs_local, dst, used, pad_start, pad_n, n_used, n_rows):
    d = xs_local.shape[1]
    return pl.pallas_call(
        _scatter_chunks_kernel,
        grid_spec=pltpu.PrefetchScalarGridSpec(
            num_scalar_prefetch=5,
            grid=(used.shape[0],),
            in_specs=[pl.BlockSpec((MOE_SLOTS, d), lambda i, *_: (i, 0))],
            out_specs=pl.BlockSpec(memory_space=pl.ANY),
            scratch_shapes=[pltpu.VMEM((MOE_TILE, d), BF16), pltpu.SemaphoreType.DMA((2,))]),
        out_shape=jax.ShapeDtypeStruct((n_rows, d), BF16),
        compiler_params=_params("arbitrary"),
        name="moe_scatter_chunks",
    )(dst, used, pad_start, pad_n, n_used, xs_local)


def _experts_kernel(te_ref, nt_ref, xs_ref, wg_ref, wu_ref, wd_ref, o_ref):
    used = pl.program_id(0) < nt_ref[0]

    @pl.when(used)
    def _():
        x = xs_ref[...]
        gate = jnp.dot(x, wg_ref[0].astype(BF16), preferred_element_type=F32)
        up = jnp.dot(x, wu_ref[0].astype(BF16), preferred_element_type=F32)
        hid = (_silu(gate) * up).astype(BF16)
        o_ref[...] = jnp.dot(hid, wd_ref[0].astype(BF16), preferred_element_type=F32).astype(BF16)

    @pl.when(jnp.logical_not(used))
    def _():
        o_ref[...] = jnp.zeros_like(o_ref)


def _experts(xs, tile_expert, n_used, w_gate, w_up, w_down):
    n_rows, d = xs.shape
    tile = MOE_TILE
    by_expert = lambda j, te, nt: (te[j], 0, 0)
    return pl.pallas_call(
        _experts_kernel,
        grid_spec=pltpu.PrefetchScalarGridSpec(
            num_scalar_prefetch=2,
            grid=(n_rows // tile,),
            in_specs=[pl.BlockSpec((tile, d), lambda j, te, nt: (jnp.maximum(jnp.minimum(j, nt[0] - 1), 0), 0)),
                      pl.BlockSpec((1, d, MOE_FF), by_expert),
                      pl.BlockSpec((1, d, MOE_FF), by_expert),
                      pl.BlockSpec((1, MOE_FF, d), by_expert)],
            out_specs=pl.BlockSpec((tile, d), lambda j, te, nt: (j, 0))),
        out_shape=jax.ShapeDtypeStruct((n_rows, d), BF16),
        compiler_params=_params("arbitrary"),
        name="moe_experts",
    )(tile_expert, n_used, xs, w_gate, w_up, w_down)


def _combine_kernel(dst_ref, used_ref, wslot_ref, col_ref, x1_ref, mod_ref, gpost_ref, ys_ref, o_ref,
                    ysl_ref, sem):
    i = pl.program_id(0)
    chunks = MOE_SLOTS // MOE_CHUNK
    slot = lax.rem(i, 2)

    def fetch(tile, buf, wait):
        base = tile * chunks
        used = used_ref[tile]

        def copy(c, carry):
            cp = pltpu.make_async_copy(ys_ref.at[_chunk_rows(dst_ref[base + c]), :],
                                       ysl_ref.at[buf, _chunk_rows(c), :], sem.at[buf])
            cp.wait() if wait else cp.start()
            return carry

        def clear(c, carry):
            ysl_ref[buf, _chunk_rows(c), :] = jnp.zeros((MOE_CHUNK, ysl_ref.shape[2]), BF16)
            return carry

        lax.fori_loop(0, used, copy, 0)
        if not wait:
            lax.fori_loop(used, chunks, clear, 0)

    @pl.when(i == 0)
    def _():
        fetch(0, 0, False)

    @pl.when(i + 1 < pl.num_programs(0))
    def _():
        fetch(i + 1, 1 - slot, False)

    col = col_ref[...]
    tm = col.shape[0]
    lane = lax.broadcasted_iota(jnp.int32, (tm, MOE_SLOTS), 1)
    hit = jnp.where(lane == col[:, 0:1].astype(jnp.int32), 1.0,
                    jnp.where(lane == col[:, 1:2].astype(jnp.int32), 1.0, 0.0)).astype(BF16)
    fetch(i, slot, True)
    ysw = (ysl_ref[slot].astype(F32) * wslot_ref[:, 0:1]).astype(BF16)
    y = jnp.dot(hit, ysw, preferred_element_type=F32)
    g2 = mod_ref[0, 5:6, :]
    o_ref[...] = x1_ref[...] + g2 * _rms(y, gpost_ref[...])


def _combine(ys, dst, used, wslot, col, x1, mod3, gpost, seq, tm):
    t, d = x1.shape
    per_b = seq // tm
    row = lambda i, *_: (i, 0)
    return pl.pallas_call(
        _combine_kernel,
        grid_spec=pltpu.PrefetchScalarGridSpec(
            num_scalar_prefetch=2,
            grid=(t // tm,),
            in_specs=[pl.BlockSpec((MOE_SLOTS, LANES), row),
                      pl.BlockSpec((tm, LANES), row),
                      pl.BlockSpec((tm, d), row),
                      pl.BlockSpec((1, N_MOD, d), lambda i, *_: (i // per_b, 0, 0)),
                      pl.BlockSpec((1, d), lambda i, *_: (0, 0)),
                      pl.BlockSpec(memory_space=pl.ANY)],
            out_specs=pl.BlockSpec((tm, d), row),
            scratch_shapes=[pltpu.VMEM((2, MOE_SLOTS, d), BF16), pltpu.SemaphoreType.DMA((2,))]),
        out_shape=jax.ShapeDtypeStruct((t, d), F32),
        compiler_params=_params("arbitrary"),
        name="moe_combine",
    )(dst, used, wslot, col, x1, mod3, gpost, ys)


def kernel(x, c, positions, ada_w, ada_b, pre_norm_mix, post_norm_mix, pre_norm_ffn, post_norm_ffn, w_in, conv_w, conv_b, dt_bias, a_log, d_skip, ssd_norm, q_norm, w_uq, kv_norm, w_ukv, mla_out_norm, w_out, w_group_router, b_group_router, w_expert_router, b_expert_router, w_gate, w_up, w_down):
    batch, seq, d = x.shape
    t = batch * seq
    depth = ada_w.shape[0]
    tm = min(512, seq)
    half = MLA_ROPE // 2

    cos_t, sin_t = _rope_tables(positions)

    x2 = x.reshape(t, d)
    for l in range(depth):
        mod3 = _modulation(c, ada_w[l], ada_b[l]).reshape(batch, N_MOD, d)
        w_in_r, wq2, wk, wv_t = _inproj_weights(w_in[l], w_uq[l], w_ukv[l])
        z, xbc, dt_raw, q, k, v_t = _inproj(
            x2, mod3, pre_norm_mix[l].reshape(1, d), w_in_r, q_norm[l].reshape(1, -1), wq2,
            kv_norm[l].reshape(1, -1), wk, wv_t, cos_t, sin_t, seq, tm)
        y_ssd = _ssd(xbc, z, dt_raw, conv_w[l], conv_b[l], dt_bias[l], a_log[l], d_skip[l],
                     ssd_norm[l], batch, seq)
        y_mla = _attention(q, k, v_t, mla_out_norm[l].reshape(1, -1), batch, seq)
        w_o = w_out[l].astype(BF16)
        pad_r = LANES - MOE_GROUPS - MOE_EXPERTS
        w_router = jnp.concatenate([w_group_router[l], w_expert_router[l], jnp.zeros((d, pad_r), F32)], axis=1)
        b_router = jnp.concatenate([b_group_router[l].reshape(-1), b_expert_router[l].reshape(-1),
                                    jnp.zeros((pad_r,), F32)]).reshape(1, LANES)
        wr_hi = w_router.astype(BF16)
        wr_lo = (w_router - wr_hi.astype(F32)).astype(BF16)
        x1, xs_local, col, wslot, cnt_tiles, chunk_expert = _outproj(
            y_ssd, y_mla, x2, mod3, w_o[:SSD_INNER], w_o[SSD_INNER:], post_norm_mix[l].reshape(1, d),
            pre_norm_ffn[l].reshape(1, d), wr_hi, wr_lo, b_router, seq, MOE_TOKENS)
        n_tok_tiles = t // MOE_TOKENS
        n_rows = _moe_rows(n_tok_tiles)
        dst, used, pad_start, pad_n, n_used, tile_expert = _moe_plan(cnt_tiles, chunk_expert, n_tok_tiles, n_rows)
        xs = _scatter_chunks(xs_local, dst, used, pad_start, pad_n, n_used, n_rows)
        ys = _experts(xs, tile_expert, n_used, w_gate[l], w_up[l], w_down[l])
        x2 = _combine(ys, dst, used, wslot, col, x1, mod3, post_norm_ffn[l].reshape(1, d), seq, MOE_TOKENS)
    return x2.reshape(batch, seq, d)
```

```python
import functools
import math

import jax
import jax.numpy as jnp
from jax import lax
from jax.experimental import pallas as pl
from jax.experimental.pallas import tpu as pltpu

F32 = jnp.float32
BF16 = jnp.bfloat16

D_MODEL = 1024
SSD_HEADS = 8
SSD_HEAD_DIM = 64
SSD_INNER = SSD_HEADS * SSD_HEAD_DIM
SSD_GROUPS = 2
SSD_STATE = 128
SSD_CONV = 4
SSD_CHUNK = 128
SSD_XBC = SSD_INNER + 2 * SSD_GROUPS * SSD_STATE
MLA_HEADS = 8
MLA_NOPE = 64
MLA_ROPE = 32
MLA_QK = MLA_NOPE + MLA_ROPE
MLA_V = 64
MLA_Q_RANK = 256
MLA_KV_RANK = 128
MLA_INNER = MLA_HEADS * MLA_V
ROPE_THETA = 10000.0
MOE_GROUPS = 4
MOE_EXPERTS_PER_GROUP = 8
MOE_EXPERTS = MOE_GROUPS * MOE_EXPERTS_PER_GROUP
MOE_FF = 256
N_MOD = 6
EPS = 1e-6
LOG2_E = math.log2(math.e)

LANES = 128
HEAD_PAD = 128
ATT_BLOCK = 256
MLA_V_ROWS = MLA_V + 16
SSD_STEP_ROWS = 4 * SSD_CHUNK
MOE_TOKENS = 512
MOE_CHUNK = 16
MOE_TILE = 512
MOE_SLOTS = 2 * MOE_TOKENS + MOE_EXPERTS * MOE_CHUNK
VMEM_LIMIT = 48 * 1024 * 1024

_C_Z = 0
_C_XBC = _C_Z + SSD_INNER
_C_CQ = _C_XBC + SSD_XBC
_C_CKV = _C_CQ + MLA_Q_RANK
_C_KR = _C_CKV + MLA_KV_RANK
_C_KRS = _C_KR + LANES
_C_DT = _C_KRS + LANES
_C_END = _C_DT + LANES


def _silu(v):
    return v * (1.0 / (1.0 + jnp.exp(-v)))


def _rms(v, gain):
    return v * lax.rsqrt(jnp.mean(v * v, axis=-1, keepdims=True) + EPS) * gain


def _params(*sem, flags=None):
    return pltpu.CompilerParams(dimension_semantics=sem, vmem_limit_bytes=VMEM_LIMIT, flags=flags)


def _mod_kernel(c_ref, w_ref, b_ref, o_ref):
    act = _silu(c_ref[...])
    a_hi = act.astype(BF16)
    a_lo = (act - a_hi.astype(F32)).astype(BF16)
    w = w_ref[...]
    w_hi = w.astype(BF16)
    w_lo = (w - w_hi.astype(F32)).astype(BF16)
    o_ref[...] = (jnp.dot(a_hi, w_hi, preferred_element_type=F32)
                  + jnp.dot(a_lo, w_hi, preferred_element_type=F32)
                  + jnp.dot(a_hi, w_lo, preferred_element_type=F32)) + b_ref[...]


def _modulation(c, ada_w, ada_b):
    b, d = c.shape
    n = ada_w.shape[1]
    return pl.pallas_call(
        _mod_kernel,
        grid=(n // d,),
        in_specs=[pl.BlockSpec((b, d), lambda j: (0, 0)),
                  pl.BlockSpec((d, d), lambda j: (0, j)),
                  pl.BlockSpec((1, d), lambda j: (0, j))],
        out_specs=pl.BlockSpec((b, d), lambda j: (0, j)),
        out_shape=jax.ShapeDtypeStruct((b, n), F32),
        compiler_params=_params("arbitrary"),
        name="modulation",
    )(c, ada_w, ada_b.reshape(1, n))


def _rope_kernel(pos_ref, inv_ref, cos_ref, sin_ref):
    ang = pos_ref[...].astype(F32) * inv_ref[...]
    blk = ang.shape[0]
    half = MLA_ROPE // 2
    per_row = LANES // half
    lane = lax.broadcasted_iota(jnp.int32, (blk * per_row, LANES), 1)
    first = (lane >= MLA_NOPE) & (lane < MLA_NOPE + half)
    second = (lane >= MLA_NOPE + half) & (lane < MLA_QK)
    for fn, out_ref, fill in ((jnp.cos, cos_ref, 1.0), (jnp.sin, sin_ref, 0.0)):
        compact = fn(ang)
        rep = jnp.broadcast_to(compact[:, None, :], (blk, per_row, LANES)).reshape(blk * per_row, LANES)
        a = pltpu.roll(rep, 0, 1, stride=half, stride_axis=0)
        b = pltpu.roll(a, half, 1)
        out_ref[...] = jnp.where(first, a, jnp.where(second, b, fill))


def _rope_tables(positions):
    t = positions.size
    half = MLA_ROPE // 2
    per_row = LANES // half
    inv = 1.0 / (ROPE_THETA ** (jnp.arange(0, MLA_ROPE, 2, dtype=F32) / MLA_ROPE))
    group_token = (MLA_NOPE // half - jnp.arange(per_row)) % per_row
    pos_rep = jnp.repeat(positions.reshape(t // per_row, per_row)[:, group_token], half, axis=1)
    inv_t = jnp.tile(inv, per_row).reshape(1, LANES)
    rows = t // per_row
    blk = min(rows, 512)
    return pl.pallas_call(
        _rope_kernel,
        grid=(rows // blk,),
        in_specs=[pl.BlockSpec((blk, LANES), lambda i: (i, 0)),
                  pl.BlockSpec((1, LANES), lambda i: (0, 0))],
        out_specs=[pl.BlockSpec((blk * per_row, LANES), lambda i: (i, 0))] * 2,
        out_shape=[jax.ShapeDtypeStruct((t, LANES), F32)] * 2,
        compiler_params=_params("arbitrary"),
        name="rope_tables",
    )(pos_rep, inv_t)


def _inproj_kernel(x_ref, mod_ref, gpre_ref, win_ref, qn_ref, wq_ref, kvn_ref, wk_ref, wvt_ref,
                   cos_ref, sin_ref, z_ref, xbc_ref, dt_ref, q_ref, k_ref, vt_ref):
    x = x_ref[...]
    sh = mod_ref[0, 0:1, :]
    sc = mod_ref[0, 1:2, :]
    h = (_rms(x, gpre_ref[...]) * (1.0 + sc) + sh).astype(BF16)
    u = jnp.dot(h, win_ref[...], preferred_element_type=F32)
    z_ref[...] = u[:, _C_Z:_C_XBC].astype(BF16)
    xbc_ref[...] = u[:, _C_XBC:_C_CQ].astype(BF16)
    dt_ref[...] = u[:, _C_DT:_C_END]
    cos_t = cos_ref[...]
    sin_t = sin_ref[...]
    cq = _rms(u[:, _C_CQ:_C_CKV], qn_ref[...]).astype(BF16)
    q2 = jnp.dot(cq, wq_ref[...], preferred_element_type=F32)
    scale = MLA_QK ** -0.5 * LOG2_E
    nq = MLA_HEADS * HEAD_PAD
    for hd in range(MLA_HEADS):
        a = q2[:, hd * HEAD_PAD:(hd + 1) * HEAD_PAD]
        b = q2[:, nq + hd * HEAD_PAD:nq + (hd + 1) * HEAD_PAD]
        q_ref[:, hd * HEAD_PAD:(hd + 1) * HEAD_PAD] = ((a * cos_t + b * sin_t) * scale).astype(BF16)
    ckv = _rms(u[:, _C_CKV:_C_KR], kvn_ref[...]).astype(BF16)
    kn = jnp.dot(ckv, wk_ref[...], preferred_element_type=F32)
    k_pe = u[:, _C_KR:_C_KRS] * cos_t + u[:, _C_KRS:_C_DT] * sin_t
    for hd in range(MLA_HEADS):
        k_ref[:, hd * HEAD_PAD:(hd + 1) * HEAD_PAD] = (
            kn[:, hd * HEAD_PAD:(hd + 1) * HEAD_PAD] + k_pe).astype(BF16)
    v_t = lax.dot_general(wvt_ref[...], ckv, (((1,), (1,)), ((), ())), preferred_element_type=F32)
    head_row = lax.rem(lax.broadcasted_iota(jnp.int32, (v_t.shape[0], 1), 0), MLA_V_ROWS)
    v_t = v_t + jnp.where(head_row == MLA_V, 1.0, 0.0)
    for s in range(vt_ref.shape[0]):
        vt_ref[s] = v_t[:, s * ATT_BLOCK:(s + 1) * ATT_BLOCK].astype(BF16)


def _inproj_weights(w_in, w_uq, w_ukv):
    d = w_in.shape[0]
    half = MLA_ROPE // 2
    o_z, o_xbc, o_dt = 0, SSD_INNER, SSD_INNER + SSD_XBC
    o_cq = o_dt + SSD_HEADS
    o_ckv = o_cq + MLA_Q_RANK
    o_kr = o_ckv + MLA_KV_RANK
    zeros = lambda n: jnp.zeros((d, n), F32)
    kr = w_in[:, o_kr:o_kr + MLA_ROPE]
    kr_blk = jnp.concatenate([zeros(MLA_NOPE), kr, zeros(HEAD_PAD - MLA_QK)], axis=1)
    krs_blk = jnp.concatenate([zeros(MLA_NOPE), -kr[:, half:], kr[:, :half], zeros(HEAD_PAD - MLA_QK)], axis=1)
    dt_blk = jnp.concatenate([w_in[:, o_dt:o_dt + SSD_HEADS], zeros(LANES - SSD_HEADS)], axis=1)
    w_in_r = jnp.concatenate([w_in[:, o_z:o_dt], w_in[:, o_cq:o_kr], kr_blk, krs_blk, dt_blk], axis=1)
    r = w_uq.shape[0]
    zq = jnp.zeros((r, MLA_HEADS, HEAD_PAD - MLA_QK), F32)
    zn = jnp.zeros((r, MLA_HEADS, MLA_NOPE), F32)
    wq_plain = jnp.concatenate([w_uq, zq], axis=2).reshape(r, MLA_HEADS * HEAD_PAD)
    wq_rot = jnp.concatenate([zn, -w_uq[:, :, MLA_NOPE + half:], w_uq[:, :, MLA_NOPE:MLA_NOPE + half], zq],
                             axis=2).reshape(r, MLA_HEADS * HEAD_PAD)
    wq2 = jnp.concatenate([wq_plain, wq_rot], axis=1)
    rk = w_ukv.shape[0]
    zk = jnp.zeros((rk, MLA_HEADS, HEAD_PAD - MLA_NOPE), F32)
    wk = jnp.concatenate([w_ukv[:, :, :MLA_NOPE], zk], axis=2).reshape(rk, MLA_HEADS * HEAD_PAD)
    zv = jnp.zeros((rk, MLA_HEADS, MLA_V_ROWS - MLA_V), F32)
    wv_t = jnp.concatenate([w_ukv[:, :, MLA_NOPE:], zv], axis=2).reshape(rk, MLA_HEADS * MLA_V_ROWS).T
    return w_in_r.astype(BF16), wq2.astype(BF16), wk.astype(BF16), wv_t.astype(BF16)


def _inproj(x2, mod3, gpre, w_in_r, q_norm, wq2, kv_norm, wk, wv_t, cos_t, sin_t, seq, tm):
    t, d = x2.shape
    per_b = seq // tm
    row = lambda i: (i, 0)
    const = lambda i: (0, 0)
    nqk = MLA_HEADS * HEAD_PAD
    slabs = tm // ATT_BLOCK
    return pl.pallas_call(
        _inproj_kernel,
        grid=(t // tm,),
        in_specs=[pl.BlockSpec((tm, d), row),
                  pl.BlockSpec((1, N_MOD, d), lambda i: (i // per_b, 0, 0)),
                  pl.BlockSpec((1, d), const),
                  pl.BlockSpec(w_in_r.shape, const),
                  pl.BlockSpec((1, MLA_Q_RANK), const),
                  pl.BlockSpec(wq2.shape, const),
                  pl.BlockSpec((1, MLA_KV_RANK), const),
                  pl.BlockSpec(wk.shape, const),
                  pl.BlockSpec(wv_t.shape, const),
                  pl.BlockSpec((tm, LANES), row),
                  pl.BlockSpec((tm, LANES), row)],
        out_specs=[pl.BlockSpec((tm, SSD_INNER), row),
                   pl.BlockSpec((tm, SSD_XBC), row),
                   pl.BlockSpec((tm, LANES), row),
                   pl.BlockSpec((tm, nqk), row),
                   pl.BlockSpec((tm, nqk), row),
                   pl.BlockSpec((slabs, MLA_HEADS * MLA_V_ROWS, ATT_BLOCK), lambda i: (i, 0, 0))],
        out_shape=[jax.ShapeDtypeStruct((t, SSD_INNER), BF16),
                   jax.ShapeDtypeStruct((t, SSD_XBC), BF16),
                   jax.ShapeDtypeStruct((t, LANES), F32),
                   jax.ShapeDtypeStruct((t, nqk), BF16),
                   jax.ShapeDtypeStruct((t, nqk), BF16),
                   jax.ShapeDtypeStruct((t // ATT_BLOCK, MLA_HEADS * MLA_V_ROWS, ATT_BLOCK), BF16)],
        compiler_params=_params("arbitrary"),
        name="inproj",
    )(x2, mod3, gpre, w_in_r, q_norm, wq2, kv_norm, wk, wv_t, cos_t, sin_t)


def _split3_packed(v):
    lane = lax.broadcasted_iota(jnp.int32, v.shape, 1)
    v = jnp.where(lane < SSD_HEADS, v, 0.0)
    hi = v.astype(BF16).astype(F32)
    rest = v - hi
    mid = rest.astype(BF16).astype(F32)
    lo = rest - mid
    return (hi + pltpu.roll(mid, SSD_HEADS, 1) + pltpu.roll(lo, 2 * SSD_HEADS, 1)).astype(BF16)


def _ssd_kernel(xbc_ref, z_ref, dt_ref, cw_ref, cb_ref, dtb_ref, alog_ref, dsk_ref, gn_ref,
                e_ref, y_ref, tail_ref, state_ref):
    @pl.when(pl.program_id(1) == 0)
    def _():
        tail_ref[...] = jnp.zeros_like(tail_ref)
        state_ref[...] = jnp.zeros_like(state_ref)

    for c in range(xbc_ref.shape[0] // SSD_CHUNK):
        _ssd_chunk(pl.ds(c * SSD_CHUNK, SSD_CHUNK), xbc_ref, z_ref, dt_ref, cw_ref, cb_ref, dtb_ref,
                   alog_ref, dsk_ref, gn_ref, e_ref, y_ref, tail_ref, state_ref)


def _ssd_chunk(rows, xbc_ref, z_ref, dt_ref, cw_ref, cb_ref, dtb_ref, alog_ref, dsk_ref, gn_ref,
               e_ref, y_ref, tail_ref, state_ref):
    L = SSD_CHUNK
    gw = SSD_INNER // SSD_GROUPS

    cur = xbc_ref[rows, :].astype(F32)
    prev8 = tail_ref[...]
    row8 = lax.broadcasted_iota(jnp.int32, prev8.shape, 0)
    acc = cur * cw_ref[SSD_CONV - 1:SSD_CONV, :] + cb_ref[...]
    for j in range(1, SSD_CONV):
        r = pltpu.roll(cur, j, 0)
        head = jnp.where(row8 < j, pltpu.roll(prev8, j, 0), r[0:8])
        r = jnp.concatenate([head, r[8:]], axis=0)
        acc = acc + r * cw_ref[SSD_CONV - 1 - j:SSD_CONV - j, :]
    tail_ref[...] = cur[L - 8:L]
    act = _silu(acc)
    xs = act[:, :SSD_INNER]
    bm = act[:, SSD_INNER:SSD_INNER + SSD_GROUPS * SSD_STATE].astype(BF16)
    cm = act[:, SSD_INNER + SSD_GROUPS * SSD_STATE:].astype(BF16)

    dt_in = dt_ref[rows, :] + dtb_ref[...]
    dt = jnp.maximum(dt_in, 0.0) + jnp.log(1.0 + jnp.exp(-jnp.abs(dt_in)))
    adt = dt * (-jnp.exp(alog_ref[...]))
    ri = lax.broadcasted_iota(jnp.int32, (L, L), 0)
    ci = lax.broadcasted_iota(jnp.int32, (L, L), 1)
    causal = ci <= ri
    acs_p = jnp.dot(causal.astype(BF16), _split3_packed(adt), preferred_element_type=F32)
    a_cs = (acs_p + pltpu.roll(acs_p, LANES - SSD_HEADS, 1)
            + pltpu.roll(acs_p, LANES - 2 * SSD_HEADS, 1))
    a_cs_t = a_cs.T
    cs_all = jnp.dot(_split3_packed(a_cs), e_ref[...], preferred_element_type=F32)
    cs64 = cs_all[:, :SSD_INNER]
    cs128 = cs_all[:, SSD_INNER:]
    dt64 = jnp.dot(_split3_packed(dt), e_ref[:, :SSD_INNER], preferred_element_type=F32)

    xd = xs * dt64
    xd_b = xd.astype(BF16)
    last = cs64[L - 1:L, :]
    xdw = (xd * jnp.exp(last - cs64)).astype(BF16)
    chunk_decay = jnp.exp(last)
    in_decay = jnp.exp(cs64)

    lane = lax.broadcasted_iota(jnp.int32, (L, LANES), 1)
    lo = lane < SSD_HEAD_DIM
    zero_b = jnp.zeros((L, LANES), BF16)
    y_parts = []
    new_states = []
    for g in range(SSD_GROUPS):
        bg = bm[:, g * SSD_STATE:(g + 1) * SSD_STATE]
        cg = cm[:, g * SSD_STATE:(g + 1) * SSD_STATE]
        cb = lax.dot_general(cg, bg, (((1,), (1,)), ((), ())), preferred_element_type=F32)
        prev = state_ref[:, g * gw:(g + 1) * gw]
        y_off = jnp.dot(cg, prev.astype(BF16), preferred_element_type=F32) * in_decay[:, g * gw:(g + 1) * gw]
        st = lax.dot_general(bg, xdw[:, g * gw:(g + 1) * gw], (((0,), (0,)), ((), ())),
                             preferred_element_type=F32)
        new_states.append(prev * chunk_decay[:, g * gw:(g + 1) * gw] + st)
        heads_per_group = SSD_HEADS // SSD_GROUPS
        for pair in range(heads_per_group // 2):
            h0 = g * heads_per_group + 2 * pair
            blk = xd_b[:, h0 * SSD_HEAD_DIM:(h0 + 2) * SSD_HEAD_DIM]
            y_pair = None
            for k in range(2):
                hh = h0 + k
                diff = cs128[:, hh * LANES:(hh + 1) * LANES] - a_cs_t[hh:hh + 1, :]
                decay = jnp.where(causal, jnp.exp(diff), 0.0)
                m = (cb * decay).astype(BF16)
                rhs = jnp.where(lo, blk, zero_b) if k == 0 else jnp.where(lo, zero_b, blk)
                part = jnp.dot(m, rhs, preferred_element_type=F32)
                y_pair = part if y_pair is None else y_pair + part
            c0 = (2 * pair) * SSD_HEAD_DIM
            y_parts.append(y_pair + y_off[:, c0:c0 + LANES])
    state_ref[...] = jnp.concatenate(new_states, axis=1)
    y = jnp.concatenate(y_parts, axis=1) + dsk_ref[...] * xs
    gated = y * _silu(z_ref[rows, :].astype(F32))
    y_ref[rows, :] = _rms(gated, gn_ref[...]).astype(BF16)


def _ssd(xbc, z, dt_raw, conv_w, conv_b, dt_bias, a_log, d_skip, ssd_norm, batch, seq):
    t = xbc.shape[0]
    L = SSD_STEP_ROWS
    nc = seq // L
    row = lambda b, c: (b * nc + c, 0)
    const = lambda b, c: (0, 0)
    pad = lambda v: jnp.concatenate([v.reshape(1, -1), jnp.zeros((1, LANES - v.size), F32)], axis=1)
    head_of_lane = jnp.concatenate([jnp.arange(SSD_INNER) // SSD_HEAD_DIM,
                                    jnp.arange(SSD_HEADS * LANES) // LANES])
    rows = jnp.arange(LANES)[:, None]
    expand = ((rows < 3 * SSD_HEADS) & (rows % SSD_HEADS == head_of_lane[None, :])).astype(BF16)
    dsk = jnp.repeat(d_skip.astype(F32), SSD_HEAD_DIM).reshape(1, SSD_INNER)
    return pl.pallas_call(
        _ssd_kernel,
        grid=(batch, nc),
        in_specs=[pl.BlockSpec((L, SSD_XBC), row),
                  pl.BlockSpec((L, SSD_INNER), row),
                  pl.BlockSpec((L, LANES), row),
                  pl.BlockSpec((SSD_CONV, SSD_XBC), const),
                  pl.BlockSpec((1, SSD_XBC), const),
                  pl.BlockSpec((1, LANES), const),
                  pl.BlockSpec((1, LANES), const),
                  pl.BlockSpec((1, SSD_INNER), const),
                  pl.BlockSpec((1, SSD_INNER), const),
                  pl.BlockSpec(expand.shape, const)],
        out_specs=pl.BlockSpec((L, SSD_INNER), row),
        out_shape=jax.ShapeDtypeStruct((t, SSD_INNER), BF16),
        scratch_shapes=[pltpu.VMEM((8, SSD_XBC), F32),
                        pltpu.VMEM((SSD_STATE, SSD_INNER), F32)],
        compiler_params=_params("arbitrary", "arbitrary"),
        name="ssd",
    )(xbc, z, dt_raw, conv_w, conv_b.reshape(1, -1), pad(dt_bias), pad(a_log), dsk,
      ssd_norm.reshape(1, -1), expand)


def _attn_kernel(q_ref, k_ref, vt_ref, gn_ref, o_ref, s_ref):
    blk = ATT_BLOCK
    qi = pl.program_id(1)
    causal = (lax.broadcasted_iota(jnp.int32, (blk, blk), 0)
              <= lax.broadcasted_iota(jnp.int32, (blk, blk), 1))
    dims = (((1,), (1,)), ((), ()))
    qs = [q_ref[:, hd * HEAD_PAD:(hd + 1) * HEAD_PAD] for hd in range(MLA_HEADS)]

    def step(j, carry, masked):
        rows = pl.ds(pl.multiple_of(j * blk, blk), blk)
        m_new = []
        for hd in range(MLA_HEADS):
            kk = k_ref[rows, hd * HEAD_PAD:(hd + 1) * HEAD_PAD]
            s = lax.dot_general(kk, qs[hd], dims, preferred_element_type=F32)
            if masked:
                s = jnp.where(causal, s, -jnp.inf)
            s_ref[hd] = s
            m_new.append(jnp.maximum(carry[hd][0], jnp.max(s, axis=0, keepdims=True)))
        new = []
        for hd in range(MLA_HEADS):
            m_old, acc = carry[hd]
            p = jnp.exp2(s_ref[hd] - m_new[hd])
            alpha = jnp.exp2(m_old - m_new[hd])
            vt = vt_ref[j, hd * MLA_V_ROWS:(hd + 1) * MLA_V_ROWS, :]
            acc = acc * alpha + jnp.dot(vt, p.astype(BF16), preferred_element_type=F32)
            new.append((m_new[hd], acc))
        return tuple(new)

    neg = jnp.full((1, blk), -jnp.inf, F32)
    init = tuple((neg, jnp.zeros((MLA_V_ROWS, blk), F32)) for _ in range(MLA_HEADS))
    carry = lax.fori_loop(0, qi, functools.partial(step, masked=False), init)
    final = step(qi, carry, True)
    y_t = jnp.concatenate([acc[:MLA_V] / acc[MLA_V:MLA_V + 1] for _, acc in final], axis=0)
    o_ref[...] = _rms(y_t.T, gn_ref[...]).astype(BF16)


def _attention(q, k, v_t, gn, batch, seq):
    t = q.shape[0]
    blk = ATT_BLOCK
    nq = seq // blk
    nqk = MLA_HEADS * HEAD_PAD
    return pl.pallas_call(
        _attn_kernel,
        grid=(batch, nq),
        in_specs=[pl.BlockSpec((blk, nqk), lambda b, i: (b * nq + i, 0)),
                  pl.BlockSpec((seq, nqk), lambda b, i: (b, 0)),
                  pl.BlockSpec((nq, MLA_HEADS * MLA_V_ROWS, blk), lambda b, i: (b, 0, 0)),
                  pl.BlockSpec((1, MLA_INNER), lambda b, i: (0, 0))],
        out_specs=pl.BlockSpec((blk, MLA_INNER), lambda b, i: (b * nq + i, 0)),
        out_shape=jax.ShapeDtypeStruct((t, MLA_INNER), BF16),
        scratch_shapes=[pltpu.VMEM((MLA_HEADS, blk, blk), F32)],
        compiler_params=_params("arbitrary", "arbitrary"),
        name="attention",
    )(q, k, v_t, gn)


def _outproj_kernel(ys_ref, ym_ref, x_ref, mod_ref, wtop_ref, wbot_ref, gpm_ref, gpf_ref,
                    wrh_ref, wrl_ref, br_ref, x1_ref, h2_ref, col_ref, info_ref, cnt_ref, ce_ref):
    y = (jnp.dot(ys_ref[...], wtop_ref[...], preferred_element_type=F32)
         + jnp.dot(ym_ref[...], wbot_ref[...], preferred_element_type=F32))
    g1 = mod_ref[0, 2:3, :]
    sh2 = mod_ref[0, 3:4, :]
    sc2 = mod_ref[0, 4:5, :]
    x1 = x_ref[...] + g1 * _rms(y, gpm_ref[...])
    x1_ref[...] = x1
    h2 = _rms(x1, gpf_ref[...]) * (1.0 + sc2) + sh2
    tm = h2.shape[0]
    h_hi = h2.astype(BF16)
    h_lo = (h2 - h_hi.astype(F32)).astype(BF16)
    logits = (jnp.dot(h_hi, wrh_ref[...], preferred_element_type=F32)
              + jnp.dot(h_lo, wrh_ref[...], preferred_element_type=F32)
              + jnp.dot(h_hi, wrl_ref[...], preferred_element_type=F32)) + br_ref[...]
    lt = logits.T
    row = lax.broadcasted_iota(jnp.int32, lt.shape, 0).astype(F32)
    ninf = -jnp.inf
    big = 1e9
    is_g = row < MOE_GROUPS
    gl = jnp.where(is_g, lt, ninf)
    gmax = jnp.max(gl, axis=0, keepdims=True)
    gsum = jnp.sum(jnp.where(is_g, jnp.exp(lt - gmax), 0.0), axis=0, keepdims=True)
    g_w = 1.0 / gsum
    g_idx = jnp.min(jnp.where(gl == gmax, row, big), axis=0, keepdims=True)
    first = MOE_GROUPS + MOE_EXPERTS_PER_GROUP * g_idx
    el = jnp.where(row >= first, jnp.where(row < first + MOE_EXPERTS_PER_GROUP, lt, ninf), ninf)
    m1 = jnp.max(el, axis=0, keepdims=True)
    i1 = jnp.min(jnp.where(el == m1, row, big), axis=0, keepdims=True)
    el2 = jnp.where(row == i1, ninf, el)
    m2 = jnp.max(el2, axis=0, keepdims=True)
    i2 = jnp.min(jnp.where(el2 == m2, row, big), axis=0, keepdims=True)
    r = jnp.exp(m2 - m1)
    w1 = g_w / (1.0 + r)
    w2 = g_w * r / (1.0 + r)
    sel1 = row == i1
    sel2 = row == i2
    both = jnp.where(sel1, 1.0, jnp.where(sel2, 1.0, 0.0))
    cnt16 = jnp.floor((jnp.sum(both, axis=1, keepdims=True) + (MOE_CHUNK - 1)) * (1.0 / MOE_CHUNK))
    er = lax.broadcasted_iota(jnp.int32, (LANES, LANES), 0)
    ec = lax.broadcasted_iota(jnp.int32, (LANES, LANES), 1)
    cnt16_b = jnp.broadcast_to(cnt16, (LANES, LANES))
    off16 = jnp.dot((ec < er).astype(BF16), cnt16_b.astype(BF16), preferred_element_type=F32)[:, 0:1]
    earlier = (lax.broadcasted_iota(jnp.int32, (tm, tm), 0)
               < lax.broadcasted_iota(jnp.int32, (tm, tm), 1)).astype(BF16)
    base = off16 * MOE_CHUNK + jnp.dot(both.astype(BF16), earlier, preferred_element_type=F32)
    lpos1 = jnp.sum(jnp.where(sel1, base, 0.0), axis=0, keepdims=True)
    lpos2 = jnp.sum(jnp.where(sel2, base, 0.0), axis=0, keepdims=True)
    h2_ref[...] = h_hi
    col_ref[...] = jnp.where(row == 0, lpos1, jnp.where(row == 1, lpos2, 0.0)).T
    row8 = lax.broadcasted_iota(jnp.int32, info_ref.shape, 0)
    info_ref[...] = jnp.where(row8 == 0, lpos1, jnp.where(row8 == 1, lpos2,
                              jnp.where(row8 == 2, w1, jnp.where(row8 == 3, w2, 0.0))))
    cnt_ref[...] = cnt16_b
    chunk = lax.broadcasted_iota(jnp.int32, (LANES, LANES), 1).astype(F32)
    ce = jnp.sum(jnp.where(off16 + cnt16 <= chunk, 1.0, 0.0), axis=0, keepdims=True) - MOE_GROUPS
    ce_ref[...] = jnp.broadcast_to(ce, ce_ref.shape)


def _outproj(y_ssd, y_mla, x2, mod3, w_top, w_bot, gpm, gpf, wr_hi, wr_lo, b_router, seq, tm):
    t, d = x2.shape
    per_b = seq // tm
    n_tok_tiles = t // tm
    row = lambda i: (i, 0)
    const = lambda i: (0, 0)
    return pl.pallas_call(
        _outproj_kernel,
        grid=(n_tok_tiles,),
        in_specs=[pl.BlockSpec((tm, SSD_INNER), row),
                  pl.BlockSpec((tm, MLA_INNER), row),
                  pl.BlockSpec((tm, d), row),
                  pl.BlockSpec((1, N_MOD, d), lambda i: (i // per_b, 0, 0)),
                  pl.BlockSpec(w_top.shape, const),
                  pl.BlockSpec(w_bot.shape, const),
                  pl.BlockSpec((1, d), const),
                  pl.BlockSpec((1, d), const),
                  pl.BlockSpec(wr_hi.shape, const),
                  pl.BlockSpec(wr_lo.shape, const),
                  pl.BlockSpec((1, LANES), const)],
        out_specs=[pl.BlockSpec((tm, d), row),
                   pl.BlockSpec((tm, d), row),
                   pl.BlockSpec((tm, LANES), row),
                   pl.BlockSpec((8, tm), row),
                   pl.BlockSpec((LANES, LANES), row),
                   pl.BlockSpec((8, LANES), row)],
        out_shape=[jax.ShapeDtypeStruct((t, d), F32),
                   jax.ShapeDtypeStruct((t, d), BF16),
                   jax.ShapeDtypeStruct((t, LANES), F32),
                   jax.ShapeDtypeStruct((n_tok_tiles * 8, tm), F32),
                   jax.ShapeDtypeStruct((n_tok_tiles * LANES, LANES), F32),
                   jax.ShapeDtypeStruct((n_tok_tiles * 8, LANES), F32)],
        compiler_params=_params("arbitrary"),
        name="outproj_router",
    )(y_ssd, y_mla, x2, mod3, w_top, w_bot, gpm, gpf, wr_hi, wr_lo, b_router)


def _moe_rows(n_tok_tiles):
    rows = n_tok_tiles * MOE_SLOTS + MOE_EXPERTS * (MOE_TILE - MOE_CHUNK)
    return (rows + MOE_TILE - 1) // MOE_TILE * MOE_TILE


def _moe_plan(cnt_tiles, chunk_expert, n_tok_tiles, n_rows):
    chunks_per_tile = MOE_SLOTS // MOE_CHUNK
    chunks_per_rows = MOE_TILE // MOE_CHUNK
    cnt = cnt_tiles.reshape(n_tok_tiles, LANES, LANES)[:, MOE_GROUPS:MOE_GROUPS + MOE_EXPERTS, 0]
    cnt = cnt.astype(jnp.int32)
    ce = chunk_expert.reshape(n_tok_tiles, 8, LANES)[:, 0, :chunks_per_tile].astype(jnp.int32)
    used = jnp.sum(cnt, axis=1)
    total = jnp.sum(cnt, axis=0)
    padded = (total + chunks_per_rows - 1) // chunks_per_rows * chunks_per_rows
    e_end = jnp.cumsum(padded)
    e_start = e_end - padded
    run_global = e_start[None, :] + jnp.cumsum(cnt, axis=0) - cnt
    run_local = jnp.cumsum(cnt, axis=1) - cnt
    experts = jnp.arange(MOE_EXPERTS, dtype=jnp.int32)
    shift = jnp.sum(jnp.where(ce[:, :, None] == experts, (run_global - run_local)[:, None, :], 0), axis=-1)
    dst = shift + jnp.arange(chunks_per_tile, dtype=jnp.int32)[None, :]
    n_row_tiles = n_rows // MOE_TILE
    n_used = e_end[-1] // chunks_per_rows
    j = jnp.minimum(jnp.arange(n_row_tiles, dtype=jnp.int32), jnp.maximum(n_used - 1, 0))
    tile_expert = jnp.sum((e_end[None, :] // chunks_per_rows <= j[:, None]).astype(jnp.int32), axis=1)
    tile_expert = jnp.minimum(tile_expert, MOE_EXPERTS - 1)
    i32 = lambda v: v.astype(jnp.int32)
    return (i32(dst.reshape(-1)), i32(used), i32(e_start + total), i32(padded - total),
            i32(n_used.reshape(1)), i32(tile_expert))


def _chunk_rows(chunk):
    return pl.ds(pl.multiple_of(chunk * MOE_CHUNK, MOE_CHUNK), MOE_CHUNK)


def _run_copies(*groups):
    for wait in (False, True):
        for lo, hi, make_copy in groups:
            def body(k, carry, wait=wait, make_copy=make_copy):
                cp = make_copy(k)
                cp.wait() if wait else cp.start()
                return carry
            lax.fori_loop(lo, hi, body, 0)


def _scatter_chunks_kernel(dst_ref, used_ref, pad_start_ref, pad_n_ref, nused_ref, h2_ref, info_ref,
                           o_ref, wslot_ref, src_ref, zero_ref, sem):
    i = pl.program_id(0)
    base = i * (MOE_SLOTS // MOE_CHUNK)

    info = info_ref[...]
    srow = lax.broadcasted_iota(jnp.int32, (MOE_SLOTS, info.shape[1]), 0)
    hit1 = srow == info[0:1, :].astype(jnp.int32)
    hit2 = srow == info[1:2, :].astype(jnp.int32)
    perm = jnp.where(hit1, 1.0, jnp.where(hit2, 1.0, 0.0)).astype(BF16)
    src_ref[...] = jnp.dot(perm, h2_ref[...], preferred_element_type=F32).astype(BF16)
    wslot = jnp.sum(jnp.where(hit1, info[2:3, :], jnp.where(hit2, info[3:4, :], 0.0)), axis=1, keepdims=True)
    wslot_ref[...] = jnp.broadcast_to(wslot, wslot_ref.shape)

    @pl.when(i == 0)
    def _():
        zero_ref[...] = jnp.zeros_like(zero_ref)
        pads = [(0, pad_n_ref[e], lambda c, e=e: pltpu.make_async_copy(
            zero_ref.at[pl.ds(0, MOE_CHUNK), :], o_ref.at[_chunk_rows(pad_start_ref[e] + c), :], sem.at[1]))
            for e in range(MOE_EXPERTS)]
        tail = (nused_ref[0], o_ref.shape[0] // MOE_TILE, lambda j: pltpu.make_async_copy(
            zero_ref, o_ref.at[pl.ds(pl.multiple_of(j * MOE_TILE, MOE_TILE), MOE_TILE), :], sem.at[1]))
        _run_copies(tail, *pads)

    _run_copies((0, used_ref[i], lambda c: pltpu.make_async_copy(
        src_ref.at[_chunk_rows(c), :], o_ref.at[_chunk_rows(dst_ref[base + c]), :], sem.at[0])))


def _scatter_chunks(h2, info, dst, used, pad_start, pad_n, n_used, n_rows):
    t, d = h2.shape
    tm = info.shape[1]
    row = lambda i, *_: (i, 0)
    return pl.pallas_call(
        _scatter_chunks_kernel,
        grid_spec=pltpu.PrefetchScalarGridSpec(
            num_scalar_prefetch=5,
            grid=(used.shape[0],),
            in_specs=[pl.BlockSpec((tm, d), row), pl.BlockSpec((8, tm), row)],
            out_specs=[pl.BlockSpec(memory_space=pl.ANY), pl.BlockSpec((MOE_SLOTS, LANES), row)],
            scratch_shapes=[pltpu.VMEM((MOE_SLOTS, d), BF16), pltpu.VMEM((MOE_TILE, d), BF16),
                            pltpu.SemaphoreType.DMA((2,))]),
        out_shape=[jax.ShapeDtypeStruct((n_rows, d), BF16),
                   jax.ShapeDtypeStruct((used.shape[0] * MOE_SLOTS, LANES), F32)],
        compiler_params=_params("arbitrary"),
        name="moe_scatter_chunks",
    )(dst, used, pad_start, pad_n, n_used, h2, info)


def _experts_kernel(te_ref, nt_ref, xs_ref, wg_ref, wu_ref, wd_ref, o_ref):
    used = pl.program_id(0) < nt_ref[0]

    @pl.when(used)
    def _():
        x = xs_ref[...]
        gate = jnp.dot(x, wg_ref[0].astype(BF16), preferred_element_type=F32)
        up = jnp.dot(x, wu_ref[0].astype(BF16), preferred_element_type=F32)
        hid = (_silu(gate) * up).astype(BF16)
        o_ref[...] = jnp.dot(hid, wd_ref[0].astype(BF16), preferred_element_type=F32).astype(BF16)

    @pl.when(jnp.logical_not(used))
    def _():
        o_ref[...] = jnp.zeros_like(o_ref)


def _experts(xs, tile_expert, n_used, w_gate, w_up, w_down):
    n_rows, d = xs.shape
    tile = MOE_TILE
    by_expert = lambda j, te, nt: (te[j], 0, 0)
    return pl.pallas_call(
        _experts_kernel,
        grid_spec=pltpu.PrefetchScalarGridSpec(
            num_scalar_prefetch=2,
            grid=(n_rows // tile,),
            in_specs=[pl.BlockSpec((tile, d), lambda j, te, nt: (jnp.maximum(jnp.minimum(j, nt[0] - 1), 0), 0)),
                      pl.BlockSpec((1, d, MOE_FF), by_expert),
                      pl.BlockSpec((1, d, MOE_FF), by_expert),
                      pl.BlockSpec((1, MOE_FF, d), by_expert)],
            out_specs=pl.BlockSpec((tile, d), lambda j, te, nt: (j, 0))),
        out_shape=jax.ShapeDtypeStruct((n_rows, d), BF16),
        compiler_params=_params("arbitrary"),
        name="moe_experts",
    )(tile_expert, n_used, xs, w_gate, w_up, w_down)


def _combine_kernel(dst_ref, used_ref, wslot_ref, col_ref, x1_ref, mod_ref, gpost_ref, ys_ref, o_ref,
                    ysl_ref, sem):
    i = pl.program_id(0)
    chunks = MOE_SLOTS // MOE_CHUNK
    slot = lax.rem(i, 2)

    def fetch(tile, buf, wait):
        base = tile * chunks
        used = used_ref[tile]

        def copy(c, carry):
            cp = pltpu.make_async_copy(ys_ref.at[_chunk_rows(dst_ref[base + c]), :],
                                       ysl_ref.at[buf, _chunk_rows(c), :], sem.at[buf])
            cp.wait() if wait else cp.start()
            return carry

        def clear(c, carry):
            ysl_ref[buf, _chunk_rows(c), :] = jnp.zeros((MOE_CHUNK, ysl_ref.shape[2]), BF16)
            return carry

        lax.fori_loop(0, used, copy, 0)
        if not wait:
            lax.fori_loop(used, chunks, clear, 0)

    @pl.when(i == 0)
    def _():
        fetch(0, 0, False)

    @pl.when(i + 1 < pl.num_programs(0))
    def _():
        fetch(i + 1, 1 - slot, False)

    col = col_ref[...]
    tm = col.shape[0]
    lane = lax.broadcasted_iota(jnp.int32, (tm, MOE_SLOTS), 1)
    hit = jnp.where(lane == col[:, 0:1].astype(jnp.int32), 1.0,
                    jnp.where(lane == col[:, 1:2].astype(jnp.int32), 1.0, 0.0)).astype(BF16)
    fetch(i, slot, True)
    ysw = (ysl_ref[slot].astype(F32) * wslot_ref[:, 0:1]).astype(BF16)
    y = jnp.dot(hit, ysw, preferred_element_type=F32)
    g2 = mod_ref[0, 5:6, :]
    o_ref[...] = x1_ref[...] + g2 * _rms(y, gpost_ref[...])


def _combine(ys, dst, used, wslot, col, x1, mod3, gpost, seq, tm):
    t, d = x1.shape
    per_b = seq // tm
    row = lambda i, *_: (i, 0)
    return pl.pallas_call(
        _combine_kernel,
        grid_spec=pltpu.PrefetchScalarGridSpec(
            num_scalar_prefetch=2,
            grid=(t // tm,),
            in_specs=[pl.BlockSpec((MOE_SLOTS, LANES), row),
                      pl.BlockSpec((tm, LANES), row),
                      pl.BlockSpec((tm, d), row),
                      pl.BlockSpec((1, N_MOD, d), lambda i, *_: (i // per_b, 0, 0)),
                      pl.BlockSpec((1, d), lambda i, *_: (0, 0)),
                      pl.BlockSpec(memory_space=pl.ANY)],
            out_specs=pl.BlockSpec((tm, d), row),
            scratch_shapes=[pltpu.VMEM((2, MOE_SLOTS, d), BF16), pltpu.SemaphoreType.DMA((2,))]),
        out_shape=jax.ShapeDtypeStruct((t, d), F32),
        compiler_params=_params("arbitrary"),
        name="moe_combine",
    )(dst, used, wslot, col, x1, mod3, gpost, ys)


def kernel(x, c, positions, ada_w, ada_b, pre_norm_mix, post_norm_mix, pre_norm_ffn, post_norm_ffn, w_in, conv_w, conv_b, dt_bias, a_log, d_skip, ssd_norm, q_norm, w_uq, kv_norm, w_ukv, mla_out_norm, w_out, w_group_router, b_group_router, w_expert_router, b_expert_router, w_gate, w_up, w_down):
    batch, seq, d = x.shape
    t = batch * seq
    depth = ada_w.shape[0]
    tm = min(512, seq)
    half = MLA_ROPE // 2

    cos_t, sin_t = _rope_tables(positions)

    x2 = x.reshape(t, d)
    for l in range(depth):
        mod3 = _modulation(c, ada_w[l], ada_b[l]).reshape(batch, N_MOD, d)
        w_in_r, wq2, wk, wv_t = _inproj_weights(w_in[l], w_uq[l], w_ukv[l])
        z, xbc, dt_raw, q, k, v_t = _inproj(
            x2, mod3, pre_norm_mix[l].reshape(1, d), w_in_r, q_norm[l].reshape(1, -1), wq2,
            kv_norm[l].reshape(1, -1), wk, wv_t, cos_t, sin_t, seq, tm)
        y_ssd = _ssd(xbc, z, dt_raw, conv_w[l], conv_b[l], dt_bias[l], a_log[l], d_skip[l],
                     ssd_norm[l], batch, seq)
        y_mla = _attention(q, k, v_t, mla_out_norm[l].reshape(1, -1), batch, seq)
        w_o = w_out[l].astype(BF16)
        pad_r = LANES - MOE_GROUPS - MOE_EXPERTS
        w_router = jnp.concatenate([w_group_router[l], w_expert_router[l], jnp.zeros((d, pad_r), F32)], axis=1)
        b_router = jnp.concatenate([b_group_router[l].reshape(-1), b_expert_router[l].reshape(-1),
                                    jnp.zeros((pad_r,), F32)]).reshape(1, LANES)
        wr_hi = w_router.astype(BF16)
        wr_lo = (w_router - wr_hi.astype(F32)).astype(BF16)
        x1, h2, col, info, cnt_tiles, chunk_expert = _outproj(
            y_ssd, y_mla, x2, mod3, w_o[:SSD_INNER], w_o[SSD_INNER:], post_norm_mix[l].reshape(1, d),
            pre_norm_ffn[l].reshape(1, d), wr_hi, wr_lo, b_router, seq, MOE_TOKENS)
        n_tok_tiles = t // MOE_TOKENS
        n_rows = _moe_rows(n_tok_tiles)
        dst, used, pad_start, pad_n, n_used, tile_expert = _moe_plan(cnt_tiles, chunk_expert, n_tok_tiles, n_rows)
        xs, wslot = _scatter_chunks(h2, info, dst, used, pad_start, pad_n, n_used, n_rows)
        ys = _experts(xs, tile_expert, n_used, w_gate[l], w_up[l], w_down[l])
        x2 = _combine(ys, dst, used, wslot, col, x1, mod3, post_norm_ffn[l].reshape(1, d), seq, MOE_TOKENS)
    return x2.reshape(batch, seq, d)
```

```python
import functools
import math

import jax
import jax.numpy as jnp
from jax import lax
from jax.experimental import pallas as pl
from jax.experimental.pallas import tpu as pltpu

F32 = jnp.float32
BF16 = jnp.bfloat16

D_MODEL = 1024
SSD_HEADS = 8
SSD_HEAD_DIM = 64
SSD_INNER = SSD_HEADS * SSD_HEAD_DIM
SSD_GROUPS = 2
SSD_STATE = 128
SSD_CONV = 4
SSD_CHUNK = 128
SSD_XBC = SSD_INNER + 2 * SSD_GROUPS * SSD_STATE
MLA_HEADS = 8
MLA_NOPE = 64
MLA_ROPE = 32
MLA_QK = MLA_NOPE + MLA_ROPE
MLA_V = 64
MLA_Q_RANK = 256
MLA_KV_RANK = 128
MLA_INNER = MLA_HEADS * MLA_V
ROPE_THETA = 10000.0
MOE_GROUPS = 4
MOE_EXPERTS_PER_GROUP = 8
MOE_EXPERTS = MOE_GROUPS * MOE_EXPERTS_PER_GROUP
MOE_FF = 256
N_MOD = 6
EPS = 1e-6
LOG2_E = math.log2(math.e)

LANES = 128
HEAD_PAD = 128
ATT_BLOCK = 256
MLA_V_ROWS = MLA_V + 16
SSD_STEP_ROWS = 4 * SSD_CHUNK
MOE_TOKENS = 512
MOE_CHUNK = 16
MOE_TILE = 512
MOE_SLOTS = 2 * MOE_TOKENS + MOE_EXPERTS * MOE_CHUNK
VMEM_LIMIT = 48 * 1024 * 1024

_C_Z = 0
_C_XBC = _C_Z + SSD_INNER
_C_CQ = _C_XBC + SSD_XBC
_C_CKV = _C_CQ + MLA_Q_RANK
_C_KR = _C_CKV + MLA_KV_RANK
_C_KRS = _C_KR + LANES
_C_DT = _C_KRS + LANES
_C_END = _C_DT + LANES


def _silu(v):
    return v * (1.0 / (1.0 + jnp.exp(-v)))


def _rms(v, gain):
    return v * lax.rsqrt(jnp.mean(v * v, axis=-1, keepdims=True) + EPS) * gain


def _params(*sem, flags=None):
    return pltpu.CompilerParams(dimension_semantics=sem, vmem_limit_bytes=VMEM_LIMIT, flags=flags)


def _mod_kernel(c_ref, w_ref, b_ref, o_ref):
    act = _silu(c_ref[...])
    a_hi = act.astype(BF16)
    a_lo = (act - a_hi.astype(F32)).astype(BF16)
    w = w_ref[...]
    w_hi = w.astype(BF16)
    w_lo = (w - w_hi.astype(F32)).astype(BF16)
    o_ref[...] = (jnp.dot(a_hi, w_hi, preferred_element_type=F32)
                  + jnp.dot(a_lo, w_hi, preferred_element_type=F32)
                  + jnp.dot(a_hi, w_lo, preferred_element_type=F32)) + b_ref[...]


def _modulation(c, ada_w, ada_b):
    b, d = c.shape
    n = ada_w.shape[1]
    return pl.pallas_call(
        _mod_kernel,
        grid=(n // d,),
        in_specs=[pl.BlockSpec((b, d), lambda j: (0, 0)),
                  pl.BlockSpec((d, d), lambda j: (0, j)),
                  pl.BlockSpec((1, d), lambda j: (0, j))],
        out_specs=pl.BlockSpec((b, d), lambda j: (0, j)),
        out_shape=jax.ShapeDtypeStruct((b, n), F32),
        compiler_params=_params("arbitrary"),
        name="modulation",
    )(c, ada_w, ada_b.reshape(1, n))


def _rope_kernel(pos_ref, inv_ref, cos_ref, sin_ref):
    ang = pos_ref[...].astype(F32) * inv_ref[...]
    blk = ang.shape[0]
    half = MLA_ROPE // 2
    per_row = LANES // half
    lane = lax.broadcasted_iota(jnp.int32, (blk * per_row, LANES), 1)
    first = (lane >= MLA_NOPE) & (lane < MLA_NOPE + half)
    second = (lane >= MLA_NOPE + half) & (lane < MLA_QK)
    for fn, out_ref, fill in ((jnp.cos, cos_ref, 1.0), (jnp.sin, sin_ref, 0.0)):
        compact = fn(ang)
        rep = jnp.broadcast_to(compact[:, None, :], (blk, per_row, LANES)).reshape(blk * per_row, LANES)
        a = pltpu.roll(rep, 0, 1, stride=half, stride_axis=0)
        b = pltpu.roll(a, half, 1)
        out_ref[...] = jnp.where(first, a, jnp.where(second, b, fill))


def _rope_tables(positions):
    t = positions.size
    half = MLA_ROPE // 2
    per_row = LANES // half
    inv = 1.0 / (ROPE_THETA ** (jnp.arange(0, MLA_ROPE, 2, dtype=F32) / MLA_ROPE))
    group_token = (MLA_NOPE // half - jnp.arange(per_row)) % per_row
    pos_rep = jnp.repeat(positions.reshape(t // per_row, per_row)[:, group_token], half, axis=1)
    inv_t = jnp.tile(inv, per_row).reshape(1, LANES)
    rows = t // per_row
    blk = min(rows, 512)
    return pl.pallas_call(
        _rope_kernel,
        grid=(rows // blk,),
        in_specs=[pl.BlockSpec((blk, LANES), lambda i: (i, 0)),
                  pl.BlockSpec((1, LANES), lambda i: (0, 0))],
        out_specs=[pl.BlockSpec((blk * per_row, LANES), lambda i: (i, 0))] * 2,
        out_shape=[jax.ShapeDtypeStruct((t, LANES), F32)] * 2,
        compiler_params=_params("arbitrary"),
        name="rope_tables",
    )(pos_rep, inv_t)


def _inproj_kernel(x_ref, mod_ref, gpre_ref, win_ref, qn_ref, wq_ref, kvn_ref, wk_ref, wvt_ref,
                   cos_ref, sin_ref, z_ref, xbc_ref, dt_ref, q_ref, k_ref, vt_ref):
    x = x_ref[...]
    sh = mod_ref[0, 0:1, :]
    sc = mod_ref[0, 1:2, :]
    h = (_rms(x, gpre_ref[...]) * (1.0 + sc) + sh).astype(BF16)
    u = jnp.dot(h, win_ref[...], preferred_element_type=F32)
    z_ref[...] = u[:, _C_Z:_C_XBC].astype(BF16)
    xbc_ref[...] = u[:, _C_XBC:_C_CQ].astype(BF16)
    dt_ref[...] = u[:, _C_DT:_C_END]
    cos_t = cos_ref[...]
    sin_t = sin_ref[...]
    cq = _rms(u[:, _C_CQ:_C_CKV], qn_ref[...]).astype(BF16)
    q2 = jnp.dot(cq, wq_ref[...], preferred_element_type=F32)
    scale = MLA_QK ** -0.5 * LOG2_E
    nq = MLA_HEADS * HEAD_PAD
    for hd in range(MLA_HEADS):
        a = q2[:, hd * HEAD_PAD:(hd + 1) * HEAD_PAD]
        b = q2[:, nq + hd * HEAD_PAD:nq + (hd + 1) * HEAD_PAD]
        q_ref[:, hd * HEAD_PAD:(hd + 1) * HEAD_PAD] = ((a * cos_t + b * sin_t) * scale).astype(BF16)
    ckv = _rms(u[:, _C_CKV:_C_KR], kvn_ref[...]).astype(BF16)
    kn = jnp.dot(ckv, wk_ref[...], preferred_element_type=F32)
    k_pe = u[:, _C_KR:_C_KRS] * cos_t + u[:, _C_KRS:_C_DT] * sin_t
    for hd in range(MLA_HEADS):
        k_ref[:, hd * HEAD_PAD:(hd + 1) * HEAD_PAD] = (
            kn[:, hd * HEAD_PAD:(hd + 1) * HEAD_PAD] + k_pe).astype(BF16)
    v_t = lax.dot_general(wvt_ref[...], ckv, (((1,), (1,)), ((), ())), preferred_element_type=F32)
    head_row = lax.rem(lax.broadcasted_iota(jnp.int32, (v_t.shape[0], 1), 0), MLA_V_ROWS)
    v_t = v_t + jnp.where(head_row == MLA_V, 1.0, 0.0)
    for s in range(vt_ref.shape[0]):
        vt_ref[s] = v_t[:, s * ATT_BLOCK:(s + 1) * ATT_BLOCK].astype(BF16)


def _inproj_weights(w_in, w_uq, w_ukv):
    d = w_in.shape[0]
    half = MLA_ROPE // 2
    o_z, o_xbc, o_dt = 0, SSD_INNER, SSD_INNER + SSD_XBC
    o_cq = o_dt + SSD_HEADS
    o_ckv = o_cq + MLA_Q_RANK
    o_kr = o_ckv + MLA_KV_RANK
    zeros = lambda n: jnp.zeros((d, n), F32)
    kr = w_in[:, o_kr:o_kr + MLA_ROPE]
    kr_blk = jnp.concatenate([zeros(MLA_NOPE), kr, zeros(HEAD_PAD - MLA_QK)], axis=1)
    krs_blk = jnp.concatenate([zeros(MLA_NOPE), -kr[:, half:], kr[:, :half], zeros(HEAD_PAD - MLA_QK)], axis=1)
    dt_blk = jnp.concatenate([w_in[:, o_dt:o_dt + SSD_HEADS], zeros(LANES - SSD_HEADS)], axis=1)
    w_in_r = jnp.concatenate([w_in[:, o_z:o_dt], w_in[:, o_cq:o_kr], kr_blk, krs_blk, dt_blk], axis=1)
    r = w_uq.shape[0]
    zq = jnp.zeros((r, MLA_HEADS, HEAD_PAD - MLA_QK), F32)
    zn = jnp.zeros((r, MLA_HEADS, MLA_NOPE), F32)
    wq_plain = jnp.concatenate([w_uq, zq], axis=2).reshape(r, MLA_HEADS * HEAD_PAD)
    wq_rot = jnp.concatenate([zn, -w_uq[:, :, MLA_NOPE + half:], w_uq[:, :, MLA_NOPE:MLA_NOPE + half], zq],
                             axis=2).reshape(r, MLA_HEADS * HEAD_PAD)
    wq2 = jnp.concatenate([wq_plain, wq_rot], axis=1)
    rk = w_ukv.shape[0]
    zk = jnp.zeros((rk, MLA_HEADS, HEAD_PAD - MLA_NOPE), F32)
    wk = jnp.concatenate([w_ukv[:, :, :MLA_NOPE], zk], axis=2).reshape(rk, MLA_HEADS * HEAD_PAD)
    zv = jnp.zeros((rk, MLA_HEADS, MLA_V_ROWS - MLA_V), F32)
    wv_t = jnp.concatenate([w_ukv[:, :, MLA_NOPE:], zv], axis=2).reshape(rk, MLA_HEADS * MLA_V_ROWS).T
    return w_in_r.astype(BF16), wq2.astype(BF16), wk.astype(BF16), wv_t.astype(BF16)


def _inproj(x2, mod3, gpre, w_in_r, q_norm, wq2, kv_norm, wk, wv_t, cos_t, sin_t, seq, tm):
    t, d = x2.shape
    per_b = seq // tm
    row = lambda i: (i, 0)
    const = lambda i: (0, 0)
    nqk = MLA_HEADS * HEAD_PAD
    slabs = tm // ATT_BLOCK
    return pl.pallas_call(
        _inproj_kernel,
        grid=(t // tm,),
        in_specs=[pl.BlockSpec((tm, d), row),
                  pl.BlockSpec((1, N_MOD, d), lambda i: (i // per_b, 0, 0)),
                  pl.BlockSpec((1, d), const),
                  pl.BlockSpec(w_in_r.shape, const),
                  pl.BlockSpec((1, MLA_Q_RANK), const),
                  pl.BlockSpec(wq2.shape, const),
                  pl.BlockSpec((1, MLA_KV_RANK), const),
                  pl.BlockSpec(wk.shape, const),
                  pl.BlockSpec(wv_t.shape, const),
                  pl.BlockSpec((tm, LANES), row),
                  pl.BlockSpec((tm, LANES), row)],
        out_specs=[pl.BlockSpec((tm, SSD_INNER), row),
                   pl.BlockSpec((tm, SSD_XBC), row),
                   pl.BlockSpec((tm, LANES), row),
                   pl.BlockSpec((tm, nqk), row),
                   pl.BlockSpec((tm, nqk), row),
                   pl.BlockSpec((slabs, MLA_HEADS * MLA_V_ROWS, ATT_BLOCK), lambda i: (i, 0, 0))],
        out_shape=[jax.ShapeDtypeStruct((t, SSD_INNER), BF16),
                   jax.ShapeDtypeStruct((t, SSD_XBC), BF16),
                   jax.ShapeDtypeStruct((t, LANES), F32),
                   jax.ShapeDtypeStruct((t, nqk), BF16),
                   jax.ShapeDtypeStruct((t, nqk), BF16),
                   jax.ShapeDtypeStruct((t // ATT_BLOCK, MLA_HEADS * MLA_V_ROWS, ATT_BLOCK), BF16)],
        compiler_params=_params("arbitrary"),
        name="inproj",
    )(x2, mod3, gpre, w_in_r, q_norm, wq2, kv_norm, wk, wv_t, cos_t, sin_t)


def _split3_packed(v):
    lane = lax.broadcasted_iota(jnp.int32, v.shape, 1)
    v = jnp.where(lane < SSD_HEADS, v, 0.0)
    hi = v.astype(BF16).astype(F32)
    rest = v - hi
    mid = rest.astype(BF16).astype(F32)
    lo = rest - mid
    return (hi + pltpu.roll(mid, SSD_HEADS, 1) + pltpu.roll(lo, 2 * SSD_HEADS, 1)).astype(BF16)


def _ssd_kernel(xbc_ref, z_ref, dt_ref, cw_ref, cb_ref, dtb_ref, alog_ref, dsk_ref, gn_ref,
                e_ref, y_ref, tail_ref, state_ref):
    @pl.when(pl.program_id(1) == 0)
    def _():
        tail_ref[...] = jnp.zeros_like(tail_ref)
        state_ref[...] = jnp.zeros_like(state_ref)

    for c in range(xbc_ref.shape[0] // SSD_CHUNK):
        _ssd_chunk(pl.ds(c * SSD_CHUNK, SSD_CHUNK), xbc_ref, z_ref, dt_ref, cw_ref, cb_ref, dtb_ref,
                   alog_ref, dsk_ref, gn_ref, e_ref, y_ref, tail_ref, state_ref)


def _ssd_chunk(rows, xbc_ref, z_ref, dt_ref, cw_ref, cb_ref, dtb_ref, alog_ref, dsk_ref, gn_ref,
               e_ref, y_ref, tail_ref, state_ref):
    L = SSD_CHUNK
    gw = SSD_INNER // SSD_GROUPS

    cur_b = xbc_ref[rows, :]
    ext = jnp.concatenate([tail_ref[...], cur_b], axis=0)
    out_row = lax.broadcasted_iota(jnp.int32, ((SSD_CONV - 1) * L, ext.shape[0]), 0)
    src_row = lax.broadcasted_iota(jnp.int32, ((SSD_CONV - 1) * L, ext.shape[0]), 1)
    tail_rows = ext.shape[0] - L
    shift = jnp.where(src_row + (out_row // L + 1) == lax.rem(out_row, L) + tail_rows, 1.0, 0.0)
    shifted = jnp.dot(shift.astype(BF16), ext, preferred_element_type=F32)
    acc = cur_b.astype(F32) * cw_ref[SSD_CONV - 1:SSD_CONV, :] + cb_ref[...]
    for j in range(1, SSD_CONV):
        acc = acc + shifted[(j - 1) * L:j * L] * cw_ref[SSD_CONV - 1 - j:SSD_CONV - j, :]
    tail_ref[...] = cur_b[L - tail_rows:L]
    act = _silu(acc)
    xs = act[:, :SSD_INNER]
    bm = act[:, SSD_INNER:SSD_INNER + SSD_GROUPS * SSD_STATE].astype(BF16)
    cm = act[:, SSD_INNER + SSD_GROUPS * SSD_STATE:].astype(BF16)

    dt_in = dt_ref[rows, :] + dtb_ref[...]
    dt = jnp.maximum(dt_in, 0.0) + jnp.log(1.0 + jnp.exp(-jnp.abs(dt_in)))
    adt = dt * (-jnp.exp(alog_ref[...]))
    ri = lax.broadcasted_iota(jnp.int32, (L, L), 0)
    ci = lax.broadcasted_iota(jnp.int32, (L, L), 1)
    causal = ci <= ri
    acs_p = jnp.dot(causal.astype(BF16), _split3_packed(adt), preferred_element_type=F32)
    a_cs = (acs_p + pltpu.roll(acs_p, LANES - SSD_HEADS, 1)
            + pltpu.roll(acs_p, LANES - 2 * SSD_HEADS, 1))
    a_cs_t = a_cs.T
    cs_all = jnp.dot(_split3_packed(a_cs), e_ref[...], preferred_element_type=F32)
    cs64 = cs_all[:, :SSD_INNER]
    cs128 = cs_all[:, SSD_INNER:]
    dt64 = jnp.dot(_split3_packed(dt), e_ref[:, :SSD_INNER], preferred_element_type=F32)

    xd = xs * dt64
    xd_b = xd.astype(BF16)
    last = cs64[L - 1:L, :]
    xdw = (xd * jnp.exp(last - cs64)).astype(BF16)
    chunk_decay = jnp.exp(last)
    in_decay = jnp.exp(cs64)

    lane = lax.broadcasted_iota(jnp.int32, (L, LANES), 1)
    lo = lane < SSD_HEAD_DIM
    zero_b = jnp.zeros((L, LANES), BF16)
    y_parts = []
    new_states = []
    for g in range(SSD_GROUPS):
        bg = bm[:, g * SSD_STATE:(g + 1) * SSD_STATE]
        cg = cm[:, g * SSD_STATE:(g + 1) * SSD_STATE]
        cb = lax.dot_general(cg, bg, (((1,), (1,)), ((), ())), preferred_element_type=F32)
        prev = state_ref[:, g * gw:(g + 1) * gw]
        y_off = jnp.dot(cg, prev.astype(BF16), preferred_element_type=F32) * in_decay[:, g * gw:(g + 1) * gw]
        st = lax.dot_general(bg, xdw[:, g * gw:(g + 1) * gw], (((0,), (0,)), ((), ())),
                             preferred_element_type=F32)
        new_states.append(prev * chunk_decay[:, g * gw:(g + 1) * gw] + st)
        heads_per_group = SSD_HEADS // SSD_GROUPS
        for pair in range(heads_per_group // 2):
            h0 = g * heads_per_group + 2 * pair
            blk = xd_b[:, h0 * SSD_HEAD_DIM:(h0 + 2) * SSD_HEAD_DIM]
            y_pair = None
            for k in range(2):
                hh = h0 + k
                diff = cs128[:, hh * LANES:(hh + 1) * LANES] - a_cs_t[hh:hh + 1, :]
                decay = jnp.where(causal, jnp.exp(diff), 0.0)
                m = (cb * decay).astype(BF16)
                rhs = jnp.where(lo, blk, zero_b) if k == 0 else jnp.where(lo, zero_b, blk)
                part = jnp.dot(m, rhs, preferred_element_type=F32)
                y_pair = part if y_pair is None else y_pair + part
            c0 = (2 * pair) * SSD_HEAD_DIM
            y_parts.append(y_pair + y_off[:, c0:c0 + LANES])
    state_ref[...] = jnp.concatenate(new_states, axis=1)
    y = jnp.concatenate(y_parts, axis=1) + dsk_ref[...] * xs
    gated = y * _silu(z_ref[rows, :].astype(F32))
    y_ref[rows, :] = _rms(gated, gn_ref[...]).astype(BF16)


def _ssd(xbc, z, dt_raw, conv_w, conv_b, dt_bias, a_log, d_skip, ssd_norm, batch, seq):
    t = xbc.shape[0]
    L = SSD_STEP_ROWS
    nc = seq // L
    row = lambda b, c: (b * nc + c, 0)
    const = lambda b, c: (0, 0)
    pad = lambda v: jnp.concatenate([v.reshape(1, -1), jnp.zeros((1, LANES - v.size), F32)], axis=1)
    head_of_lane = jnp.concatenate([jnp.arange(SSD_INNER) // SSD_HEAD_DIM,
                                    jnp.arange(SSD_HEADS * LANES) // LANES])
    rows = jnp.arange(LANES)[:, None]
    expand = ((rows < 3 * SSD_HEADS) & (rows % SSD_HEADS == head_of_lane[None, :])).astype(BF16)
    dsk = jnp.repeat(d_skip.astype(F32), SSD_HEAD_DIM).reshape(1, SSD_INNER)
    return pl.pallas_call(
        _ssd_kernel,
        grid=(batch, nc),
        in_specs=[pl.BlockSpec((L, SSD_XBC), row),
                  pl.BlockSpec((L, SSD_INNER), row),
                  pl.BlockSpec((L, LANES), row),
                  pl.BlockSpec((SSD_CONV, SSD_XBC), const),
                  pl.BlockSpec((1, SSD_XBC), const),
                  pl.BlockSpec((1, LANES), const),
                  pl.BlockSpec((1, LANES), const),
                  pl.BlockSpec((1, SSD_INNER), const),
                  pl.BlockSpec((1, SSD_INNER), const),
                  pl.BlockSpec(expand.shape, const)],
        out_specs=pl.BlockSpec((L, SSD_INNER), row),
        out_shape=jax.ShapeDtypeStruct((t, SSD_INNER), BF16),
        scratch_shapes=[pltpu.VMEM((16, SSD_XBC), BF16),
                        pltpu.VMEM((SSD_STATE, SSD_INNER), F32)],
        compiler_params=_params("arbitrary", "arbitrary"),
        name="ssd",
    )(xbc, z, dt_raw, conv_w, conv_b.reshape(1, -1), pad(dt_bias), pad(a_log), dsk,
      ssd_norm.reshape(1, -1), expand)


def _attn_kernel(q_ref, k_ref, vt_ref, gn_ref, o_ref, s_ref):
    blk = ATT_BLOCK
    qi = pl.program_id(1)
    causal = (lax.broadcasted_iota(jnp.int32, (blk, blk), 0)
              <= lax.broadcasted_iota(jnp.int32, (blk, blk), 1))
    dims = (((1,), (1,)), ((), ()))
    qs = [q_ref[:, hd * HEAD_PAD:(hd + 1) * HEAD_PAD] for hd in range(MLA_HEADS)]

    def step(j, carry, masked):
        rows = pl.ds(pl.multiple_of(j * blk, blk), blk)
        m_new = []
        for hd in range(MLA_HEADS):
            kk = k_ref[rows, hd * HEAD_PAD:(hd + 1) * HEAD_PAD]
            s = lax.dot_general(kk, qs[hd], dims, preferred_element_type=F32)
            if masked:
                s = jnp.where(causal, s, -jnp.inf)
            s_ref[hd] = s
            m_new.append(jnp.maximum(carry[hd][0], jnp.max(s, axis=0, keepdims=True)))
        new = []
        for hd in range(MLA_HEADS):
            m_old, acc = carry[hd]
            p = jnp.exp2(s_ref[hd] - m_new[hd])
            alpha = jnp.exp2(m_old - m_new[hd])
            vt = vt_ref[j, hd * MLA_V_ROWS:(hd + 1) * MLA_V_ROWS, :]
            acc = acc * alpha + jnp.dot(vt, p.astype(BF16), preferred_element_type=F32)
            new.append((m_new[hd], acc))
        return tuple(new)

    neg = jnp.full((1, blk), -jnp.inf, F32)
    init = tuple((neg, jnp.zeros((MLA_V_ROWS, blk), F32)) for _ in range(MLA_HEADS))
    carry = lax.fori_loop(0, qi, functools.partial(step, masked=False), init)
    final = step(qi, carry, True)
    y_t = jnp.concatenate([acc[:MLA_V] / acc[MLA_V:MLA_V + 1] for _, acc in final], axis=0)
    o_ref[...] = _rms(y_t.T, gn_ref[...]).astype(BF16)


def _attention(q, k, v_t, gn, batch, seq):
    t = q.shape[0]
    blk = ATT_BLOCK
    nq = seq // blk
    nqk = MLA_HEADS * HEAD_PAD
    return pl.pallas_call(
        _attn_kernel,
        grid=(batch, nq),
        in_specs=[pl.BlockSpec((blk, nqk), lambda b, i: (b * nq + i, 0)),
                  pl.BlockSpec((seq, nqk), lambda b, i: (b, 0)),
                  pl.BlockSpec((nq, MLA_HEADS * MLA_V_ROWS, blk), lambda b, i: (b, 0, 0)),
                  pl.BlockSpec((1, MLA_INNER), lambda b, i: (0, 0))],
        out_specs=pl.BlockSpec((blk, MLA_INNER), lambda b, i: (b * nq + i, 0)),
        out_shape=jax.ShapeDtypeStruct((t, MLA_INNER), BF16),
        scratch_shapes=[pltpu.VMEM((MLA_HEADS, blk, blk), F32)],
        compiler_params=_params("arbitrary", "arbitrary"),
        name="attention",
    )(q, k, v_t, gn)


def _outproj_kernel(ys_ref, ym_ref, x_ref, mod_ref, wtop_ref, wbot_ref, gpm_ref, gpf_ref,
                    wrh_ref, wrl_ref, br_ref, x1_ref, h2_ref, col_ref, info_ref, cnt_ref, ce_ref):
    y = (jnp.dot(ys_ref[...], wtop_ref[...], preferred_element_type=F32)
         + jnp.dot(ym_ref[...], wbot_ref[...], preferred_element_type=F32))
    g1 = mod_ref[0, 2:3, :]
    sh2 = mod_ref[0, 3:4, :]
    sc2 = mod_ref[0, 4:5, :]
    x1 = x_ref[...] + g1 * _rms(y, gpm_ref[...])
    x1_ref[...] = x1
    h2 = _rms(x1, gpf_ref[...]) * (1.0 + sc2) + sh2
    tm = h2.shape[0]
    h_hi = h2.astype(BF16)
    h_lo = (h2 - h_hi.astype(F32)).astype(BF16)
    logits = (jnp.dot(h_hi, wrh_ref[...], preferred_element_type=F32)
              + jnp.dot(h_lo, wrh_ref[...], preferred_element_type=F32)
              + jnp.dot(h_hi, wrl_ref[...], preferred_element_type=F32)) + br_ref[...]
    lt = logits.T
    row = lax.broadcasted_iota(jnp.int32, lt.shape, 0).astype(F32)
    ninf = -jnp.inf
    big = 1e9
    is_g = row < MOE_GROUPS
    gl = jnp.where(is_g, lt, ninf)
    gmax = jnp.max(gl, axis=0, keepdims=True)
    gsum = jnp.sum(jnp.where(is_g, jnp.exp(lt - gmax), 0.0), axis=0, keepdims=True)
    g_w = 1.0 / gsum
    g_idx = jnp.min(jnp.where(gl == gmax, row, big), axis=0, keepdims=True)
    first = MOE_GROUPS + MOE_EXPERTS_PER_GROUP * g_idx
    el = jnp.where(row >= first, jnp.where(row < first + MOE_EXPERTS_PER_GROUP, lt, ninf), ninf)
    m1 = jnp.max(el, axis=0, keepdims=True)
    i1 = jnp.min(jnp.where(el == m1, row, big), axis=0, keepdims=True)
    el2 = jnp.where(row == i1, ninf, el)
    m2 = jnp.max(el2, axis=0, keepdims=True)
    i2 = jnp.min(jnp.where(el2 == m2, row, big), axis=0, keepdims=True)
    r = jnp.exp(m2 - m1)
    w1 = g_w / (1.0 + r)
    w2 = g_w * r / (1.0 + r)
    sel1 = row == i1
    sel2 = row == i2
    both = jnp.where(sel1, 1.0, jnp.where(sel2, 1.0, 0.0))
    cnt16 = jnp.floor((jnp.sum(both, axis=1, keepdims=True) + (MOE_CHUNK - 1)) * (1.0 / MOE_CHUNK))
    er = lax.broadcasted_iota(jnp.int32, (LANES, LANES), 0)
    ec = lax.broadcasted_iota(jnp.int32, (LANES, LANES), 1)
    cnt16_b = jnp.broadcast_to(cnt16, (LANES, LANES))
    off16 = jnp.dot((ec < er).astype(BF16), cnt16_b.astype(BF16), preferred_element_type=F32)[:, 0:1]
    earlier = (lax.broadcasted_iota(jnp.int32, (tm, tm), 0)
               < lax.broadcasted_iota(jnp.int32, (tm, tm), 1)).astype(BF16)
    base = off16 * MOE_CHUNK + jnp.dot(both.astype(BF16), earlier, preferred_element_type=F32)
    lpos1 = jnp.sum(jnp.where(sel1, base, 0.0), axis=0, keepdims=True)
    lpos2 = jnp.sum(jnp.where(sel2, base, 0.0), axis=0, keepdims=True)
    h2_ref[...] = h_hi
    col_ref[...] = jnp.where(row == 0, lpos1, jnp.where(row == 1, lpos2, 0.0)).T
    row8 = lax.broadcasted_iota(jnp.int32, info_ref.shape, 0)
    info_ref[...] = jnp.where(row8 == 0, lpos1, jnp.where(row8 == 1, lpos2,
                              jnp.where(row8 == 2, w1, jnp.where(row8 == 3, w2, 0.0))))
    cnt_ref[...] = cnt16_b
    chunk = lax.broadcasted_iota(jnp.int32, (LANES, LANES), 1).astype(F32)
    ce = jnp.sum(jnp.where(off16 + cnt16 <= chunk, 1.0, 0.0), axis=0, keepdims=True) - MOE_GROUPS
    ce_ref[...] = jnp.broadcast_to(ce, ce_ref.shape)


def _outproj(y_ssd, y_mla, x2, mod3, w_top, w_bot, gpm, gpf, wr_hi, wr_lo, b_router, seq, tm):
    t, d = x2.shape
    per_b = seq // tm
    n_tok_tiles = t // tm
    row = lambda i: (i, 0)
    const = lambda i: (0, 0)
    return pl.pallas_call(
        _outproj_kernel,
        grid=(n_tok_tiles,),
        in_specs=[pl.BlockSpec((tm, SSD_INNER), row),
                  pl.BlockSpec((tm, MLA_INNER), row),
                  pl.BlockSpec((tm, d), row),
                  pl.BlockSpec((1, N_MOD, d), lambda i: (i // per_b, 0, 0)),
                  pl.BlockSpec(w_top.shape, const),
                  pl.BlockSpec(w_bot.shape, const),
                  pl.BlockSpec((1, d), const),
                  pl.BlockSpec((1, d), const),
                  pl.BlockSpec(wr_hi.shape, const),
                  pl.BlockSpec(wr_lo.shape, const),
                  pl.BlockSpec((1, LANES), const)],
        out_specs=[pl.BlockSpec((tm, d), row),
                   pl.BlockSpec((tm, d), row),
                   pl.BlockSpec((tm, LANES), row),
                   pl.BlockSpec((8, tm), row),
                   pl.BlockSpec((LANES, LANES), row),
                   pl.BlockSpec((8, LANES), row)],
        out_shape=[jax.ShapeDtypeStruct((t, d), F32),
                   jax.ShapeDtypeStruct((t, d), BF16),
                   jax.ShapeDtypeStruct((t, LANES), F32),
                   jax.ShapeDtypeStruct((n_tok_tiles * 8, tm), F32),
                   jax.ShapeDtypeStruct((n_tok_tiles * LANES, LANES), F32),
                   jax.ShapeDtypeStruct((n_tok_tiles * 8, LANES), F32)],
        compiler_params=_params("arbitrary"),
        name="outproj_router",
    )(y_ssd, y_mla, x2, mod3, w_top, w_bot, gpm, gpf, wr_hi, wr_lo, b_router)


def _moe_rows(n_tok_tiles):
    rows = n_tok_tiles * MOE_SLOTS + MOE_EXPERTS * (MOE_TILE - MOE_CHUNK)
    return (rows + MOE_TILE - 1) // MOE_TILE * MOE_TILE


def _moe_plan(cnt_tiles, chunk_expert, n_tok_tiles, n_rows):
    chunks_per_tile = MOE_SLOTS // MOE_CHUNK
    chunks_per_rows = MOE_TILE // MOE_CHUNK
    cnt = cnt_tiles.reshape(n_tok_tiles, LANES, LANES)[:, MOE_GROUPS:MOE_GROUPS + MOE_EXPERTS, 0]
    cnt = cnt.astype(jnp.int32)
    ce = chunk_expert.reshape(n_tok_tiles, 8, LANES)[:, 0, :chunks_per_tile].astype(jnp.int32)
    used = jnp.sum(cnt, axis=1)
    total = jnp.sum(cnt, axis=0)
    padded = (total + chunks_per_rows - 1) // chunks_per_rows * chunks_per_rows
    e_end = jnp.cumsum(padded)
    e_start = e_end - padded
    run_global = e_start[None, :] + jnp.cumsum(cnt, axis=0) - cnt
    run_local = jnp.cumsum(cnt, axis=1) - cnt
    experts = jnp.arange(MOE_EXPERTS, dtype=jnp.int32)
    shift = jnp.sum(jnp.where(ce[:, :, None] == experts, (run_global - run_local)[:, None, :], 0), axis=-1)
    dst = shift + jnp.arange(chunks_per_tile, dtype=jnp.int32)[None, :]
    n_row_tiles = n_rows // MOE_TILE
    n_used = e_end[-1] // chunks_per_rows
    j = jnp.minimum(jnp.arange(n_row_tiles, dtype=jnp.int32), jnp.maximum(n_used - 1, 0))
    tile_expert = jnp.sum((e_end[None, :] // chunks_per_rows <= j[:, None]).astype(jnp.int32), axis=1)
    tile_expert = jnp.minimum(tile_expert, MOE_EXPERTS - 1)
    i32 = lambda v: v.astype(jnp.int32)
    return (i32(dst.reshape(-1)), i32(used), i32(e_start + total), i32(padded - total),
            i32(n_used.reshape(1)), i32(tile_expert))


def _chunk_rows(chunk):
    return pl.ds(pl.multiple_of(chunk * MOE_CHUNK, MOE_CHUNK), MOE_CHUNK)


def _run_copies(*groups):
    for wait in (False, True):
        for lo, hi, make_copy in groups:
            def body(k, carry, wait=wait, make_copy=make_copy):
                cp = make_copy(k)
                cp.wait() if wait else cp.start()
                return carry
            lax.fori_loop(lo, hi, body, 0)


def _scatter_chunks_kernel(dst_ref, used_ref, pad_start_ref, pad_n_ref, nused_ref, h2_ref, info_ref,
                           o_ref, wslot_ref, src_ref, zero_ref, sem):
    i = pl.program_id(0)
    chunks = MOE_SLOTS // MOE_CHUNK
    slot = lax.rem(i, 2)

    info = info_ref[...]
    srow = lax.broadcasted_iota(jnp.int32, (MOE_SLOTS, info.shape[1]), 0)
    hit1 = srow == info[0:1, :].astype(jnp.int32)
    hit2 = srow == info[1:2, :].astype(jnp.int32)
    perm = jnp.where(hit1, 1.0, jnp.where(hit2, 1.0, 0.0)).astype(BF16)
    src_ref[slot] = jnp.dot(perm, h2_ref[...], preferred_element_type=F32).astype(BF16)
    wslot = jnp.sum(jnp.where(hit1, info[2:3, :], jnp.where(hit2, info[3:4, :], 0.0)), axis=1, keepdims=True)
    wslot_ref[...] = jnp.broadcast_to(wslot, wslot_ref.shape)

    @pl.when(i == 0)
    def _():
        zero_ref[...] = jnp.zeros_like(zero_ref)
        pads = [(0, pad_n_ref[e], lambda c, e=e: pltpu.make_async_copy(
            zero_ref.at[pl.ds(0, MOE_CHUNK), :], o_ref.at[_chunk_rows(pad_start_ref[e] + c), :], sem.at[2]))
            for e in range(MOE_EXPERTS)]
        tail = (nused_ref[0], o_ref.shape[0] // MOE_TILE, lambda j: pltpu.make_async_copy(
            zero_ref, o_ref.at[pl.ds(pl.multiple_of(j * MOE_TILE, MOE_TILE), MOE_TILE), :], sem.at[2]))
        _run_copies(tail, *pads)

    def chunk_copies(tile, buf, wait):
        def body(c, carry):
            cp = pltpu.make_async_copy(src_ref.at[buf, _chunk_rows(c), :],
                                       o_ref.at[_chunk_rows(dst_ref[tile * chunks + c]), :], sem.at[buf])
            cp.wait() if wait else cp.start()
            return carry
        lax.fori_loop(0, used_ref[tile], body, 0)

    chunk_copies(i, slot, False)

    @pl.when(i > 0)
    def _():
        chunk_copies(i - 1, 1 - slot, True)

    @pl.when(i == pl.num_programs(0) - 1)
    def _():
        chunk_copies(i, slot, True)


def _scatter_chunks(h2, info, dst, used, pad_start, pad_n, n_used, n_rows):
    t, d = h2.shape
    tm = info.shape[1]
    row = lambda i, *_: (i, 0)
    return pl.pallas_call(
        _scatter_chunks_kernel,
        grid_spec=pltpu.PrefetchScalarGridSpec(
            num_scalar_prefetch=5,
            grid=(used.shape[0],),
            in_specs=[pl.BlockSpec((tm, d), row), pl.BlockSpec((8, tm), row)],
            out_specs=[pl.BlockSpec(memory_space=pl.ANY), pl.BlockSpec((MOE_SLOTS, LANES), row)],
            scratch_shapes=[pltpu.VMEM((2, MOE_SLOTS, d), BF16), pltpu.VMEM((MOE_TILE, d), BF16),
                            pltpu.SemaphoreType.DMA((3,))]),
        out_shape=[jax.ShapeDtypeStruct((n_rows, d), BF16),
                   jax.ShapeDtypeStruct((used.shape[0] * MOE_SLOTS, LANES), F32)],
        compiler_params=_params("arbitrary"),
        name="moe_scatter_chunks",
    )(dst, used, pad_start, pad_n, n_used, h2, info)


def _experts_kernel(te_ref, nt_ref, xs_ref, wg_ref, wu_ref, wd_ref, o_ref):
    used = pl.program_id(0) < nt_ref[0]

    @pl.when(used)
    def _():
        x = xs_ref[...]
        gate = jnp.dot(x, wg_ref[0].astype(BF16), preferred_element_type=F32)
        up = jnp.dot(x, wu_ref[0].astype(BF16), preferred_element_type=F32)
        hid = (_silu(gate) * up).astype(BF16)
        o_ref[...] = jnp.dot(hid, wd_ref[0].astype(BF16), preferred_element_type=F32).astype(BF16)

    @pl.when(jnp.logical_not(used))
    def _():
        o_ref[...] = jnp.zeros_like(o_ref)


def _experts(xs, tile_expert, n_used, w_gate, w_up, w_down):
    n_rows, d = xs.shape
    tile = MOE_TILE
    by_expert = lambda j, te, nt: (te[j], 0, 0)
    return pl.pallas_call(
        _experts_kernel,
        grid_spec=pltpu.PrefetchScalarGridSpec(
            num_scalar_prefetch=2,
            grid=(n_rows // tile,),
            in_specs=[pl.BlockSpec((tile, d), lambda j, te, nt: (jnp.maximum(jnp.minimum(j, nt[0] - 1), 0), 0)),
                      pl.BlockSpec((1, d, MOE_FF), by_expert),
                      pl.BlockSpec((1, d, MOE_FF), by_expert),
                      pl.BlockSpec((1, MOE_FF, d), by_expert)],
            out_specs=pl.BlockSpec((tile, d), lambda j, te, nt: (j, 0))),
        out_shape=jax.ShapeDtypeStruct((n_rows, d), BF16),
        compiler_params=_params("arbitrary"),
        name="moe_experts",
    )(tile_expert, n_used, xs, w_gate, w_up, w_down)


def _combine_kernel(dst_ref, used_ref, wslot_ref, col_ref, x1_ref, mod_ref, gpost_ref, ys_ref, o_ref,
                    ysl_ref, sem):
    i = pl.program_id(0)
    chunks = MOE_SLOTS // MOE_CHUNK
    slot = lax.rem(i, 2)

    def fetch(tile, buf, wait):
        base = tile * chunks
        used = used_ref[tile]

        def copy(c, carry):
            cp = pltpu.make_async_copy(ys_ref.at[_chunk_rows(dst_ref[base + c]), :],
                                       ysl_ref.at[buf, _chunk_rows(c), :], sem.at[buf])
            cp.wait() if wait else cp.start()
            return carry

        def clear(c, carry):
            ysl_ref[buf, _chunk_rows(c), :] = jnp.zeros((MOE_CHUNK, ysl_ref.shape[2]), BF16)
            return carry

        lax.fori_loop(0, used, copy, 0)
        if not wait:
            lax.fori_loop(used, chunks, clear, 0)

    @pl.when(i == 0)
    def _():
        fetch(0, 0, False)

    @pl.when(i + 1 < pl.num_programs(0))
    def _():
        fetch(i + 1, 1 - slot, False)

    col = col_ref[...]
    tm = col.shape[0]
    lane = lax.broadcasted_iota(jnp.int32, (tm, MOE_SLOTS), 1)
    hit = jnp.where(lane == col[:, 0:1].astype(jnp.int32), 1.0,
                    jnp.where(lane == col[:, 1:2].astype(jnp.int32), 1.0, 0.0)).astype(BF16)
    fetch(i, slot, True)
    ysw = (ysl_ref[slot].astype(F32) * wslot_ref[:, 0:1]).astype(BF16)
    y = jnp.dot(hit, ysw, preferred_element_type=F32)
    g2 = mod_ref[0, 5:6, :]
    o_ref[...] = x1_ref[...] + g2 * _rms(y, gpost_ref[...])


def _combine(ys, dst, used, wslot, col, x1, mod3, gpost, seq, tm):
    t, d = x1.shape
    per_b = seq // tm
    row = lambda i, *_: (i, 0)
    return pl.pallas_call(
        _combine_kernel,
        grid_spec=pltpu.PrefetchScalarGridSpec(
            num_scalar_prefetch=2,
            grid=(t // tm,),
            in_specs=[pl.BlockSpec((MOE_SLOTS, LANES), row),
                      pl.BlockSpec((tm, LANES), row),
                      pl.BlockSpec((tm, d), row),
                      pl.BlockSpec((1, N_MOD, d), lambda i, *_: (i // per_b, 0, 0)),
                      pl.BlockSpec((1, d), lambda i, *_: (0, 0)),
                      pl.BlockSpec(memory_space=pl.ANY)],
            out_specs=pl.BlockSpec((tm, d), row),
            scratch_shapes=[pltpu.VMEM((2, MOE_SLOTS, d), BF16), pltpu.SemaphoreType.DMA((2,))]),
        out_shape=jax.ShapeDtypeStruct((t, d), F32),
        compiler_params=_params("arbitrary"),
        name="moe_combine",
    )(dst, used, wslot, col, x1, mod3, gpost, ys)


def kernel(x, c, positions, ada_w, ada_b, pre_norm_mix, post_norm_mix, pre_norm_ffn, post_norm_ffn, w_in, conv_w, conv_b, dt_bias, a_log, d_skip, ssd_norm, q_norm, w_uq, kv_norm, w_ukv, mla_out_norm, w_out, w_group_router, b_group_router, w_expert_router, b_expert_router, w_gate, w_up, w_down):
    batch, seq, d = x.shape
    t = batch * seq
    depth = ada_w.shape[0]
    tm = min(512, seq)
    half = MLA_ROPE // 2

    cos_t, sin_t = _rope_tables(positions)

    x2 = x.reshape(t, d)
    for l in range(depth):
        mod3 = _modulation(c, ada_w[l], ada_b[l]).reshape(batch, N_MOD, d)
        w_in_r, wq2, wk, wv_t = _inproj_weights(w_in[l], w_uq[l], w_ukv[l])
        z, xbc, dt_raw, q, k, v_t = _inproj(
            x2, mod3, pre_norm_mix[l].reshape(1, d), w_in_r, q_norm[l].reshape(1, -1), wq2,
            kv_norm[l].reshape(1, -1), wk, wv_t, cos_t, sin_t, seq, tm)
        y_ssd = _ssd(xbc, z, dt_raw, conv_w[l], conv_b[l], dt_bias[l], a_log[l], d_skip[l],
                     ssd_norm[l], batch, seq)
        y_mla = _attention(q, k, v_t, mla_out_norm[l].reshape(1, -1), batch, seq)
        w_o = w_out[l].astype(BF16)
        pad_r = LANES - MOE_GROUPS - MOE_EXPERTS
        w_router = jnp.concatenate([w_group_router[l], w_expert_router[l], jnp.zeros((d, pad_r), F32)], axis=1)
        b_router = jnp.concatenate([b_group_router[l].reshape(-1), b_expert_router[l].reshape(-1),
                                    jnp.zeros((pad_r,), F32)]).reshape(1, LANES)
        wr_hi = w_router.astype(BF16)
        wr_lo = (w_router - wr_hi.astype(F32)).astype(BF16)
        x1, h2, col, info, cnt_tiles, chunk_expert = _outproj(
            y_ssd, y_mla, x2, mod3, w_o[:SSD_INNER], w_o[SSD_INNER:], post_norm_mix[l].reshape(1, d),
            pre_norm_ffn[l].reshape(1, d), wr_hi, wr_lo, b_router, seq, MOE_TOKENS)
        n_tok_tiles = t // MOE_TOKENS
        n_rows = _moe_rows(n_tok_tiles)
        dst, used, pad_start, pad_n, n_used, tile_expert = _moe_plan(cnt_tiles, chunk_expert, n_tok_tiles, n_rows)
        xs, wslot = _scatter_chunks(h2, info, dst, used, pad_start, pad_n, n_used, n_rows)
        ys = _experts(xs, tile_expert, n_used, w_gate[l], w_up[l], w_down[l])
        x2 = _combine(ys, dst, used, wslot, col, x1, mod3, post_norm_ffn[l].reshape(1, d), seq, MOE_TOKENS)
    return x2.reshape(batch, seq, d)
```

```python
import functools
import math

import jax
import jax.numpy as jnp
from jax import lax
from jax.experimental import pallas as pl
from jax.experimental.pallas import tpu as pltpu

F32 = jnp.float32
BF16 = jnp.bfloat16

D_MODEL = 1024
SSD_HEADS = 8
SSD_HEAD_DIM = 64
SSD_INNER = SSD_HEADS * SSD_HEAD_DIM
SSD_GROUPS = 2
SSD_STATE = 128
SSD_CONV = 4
SSD_CHUNK = 128
SSD_XBC = SSD_INNER + 2 * SSD_GROUPS * SSD_STATE
MLA_HEADS = 8
MLA_NOPE = 64
MLA_ROPE = 32
MLA_QK = MLA_NOPE + MLA_ROPE
MLA_V = 64
MLA_Q_RANK = 256
MLA_KV_RANK = 128
MLA_INNER = MLA_HEADS * MLA_V
ROPE_THETA = 10000.0
MOE_GROUPS = 4
MOE_EXPERTS_PER_GROUP = 8
MOE_EXPERTS = MOE_GROUPS * MOE_EXPERTS_PER_GROUP
MOE_FF = 256
N_MOD = 6
EPS = 1e-6
LOG2_E = math.log2(math.e)

LANES = 128
HEAD_PAD = 128
ATT_BLOCK = 256
MLA_V_ROWS = MLA_V + 16
INPROJ_TOKENS = 512
SSD_STEP_ROWS = 4 * SSD_CHUNK
MOE_TOKENS = 512
MOE_CHUNK = 16
MOE_TILE = 512
MOE_SLOTS = 2 * MOE_TOKENS + MOE_EXPERTS * MOE_CHUNK
VMEM_LIMIT = 48 * 1024 * 1024

_C_Z = 0
_C_XBC = _C_Z + SSD_INNER
_C_CQ = _C_XBC + SSD_XBC
_C_CKV = _C_CQ + MLA_Q_RANK
_C_KR = _C_CKV + MLA_KV_RANK
_C_KRS = _C_KR + LANES
_C_DT = _C_KRS + LANES
_C_END = _C_DT + LANES


def _silu(v):
    return v * (1.0 / (1.0 + jnp.exp(-v)))


def _rms(v, gain):
    return v * lax.rsqrt(jnp.mean(v * v, axis=-1, keepdims=True) + EPS) * gain


def _params(*sem, flags=None):
    return pltpu.CompilerParams(dimension_semantics=sem, vmem_limit_bytes=VMEM_LIMIT, flags=flags)


def _mod_kernel(c_ref, w_ref, b_ref, o_ref):
    act = _silu(c_ref[...])
    a_hi = act.astype(BF16)
    a_lo = (act - a_hi.astype(F32)).astype(BF16)
    w = w_ref[...]
    w_hi = w.astype(BF16)
    w_lo = (w - w_hi.astype(F32)).astype(BF16)
    o_ref[...] = (jnp.dot(a_hi, w_hi, preferred_element_type=F32)
                  + jnp.dot(a_lo, w_hi, preferred_element_type=F32)
                  + jnp.dot(a_hi, w_lo, preferred_element_type=F32)) + b_ref[...]


def _modulation(c, ada_w, ada_b):
    b, d = c.shape
    n = ada_w.shape[1]
    return pl.pallas_call(
        _mod_kernel,
        grid=(n // d,),
        in_specs=[pl.BlockSpec((b, d), lambda j: (0, 0)),
                  pl.BlockSpec((d, d), lambda j: (0, j)),
                  pl.BlockSpec((1, d), lambda j: (0, j))],
        out_specs=pl.BlockSpec((b, d), lambda j: (0, j)),
        out_shape=jax.ShapeDtypeStruct((b, n), F32),
        compiler_params=_params("arbitrary"),
        name="modulation",
    )(c, ada_w, ada_b.reshape(1, n))


def _rope_kernel(pos_ref, inv_ref, cos_ref, sin_ref):
    ang = pos_ref[...].astype(F32) * inv_ref[...]
    blk = ang.shape[0]
    half = MLA_ROPE // 2
    per_row = LANES // half
    lane = lax.broadcasted_iota(jnp.int32, (blk * per_row, LANES), 1)
    first = (lane >= MLA_NOPE) & (lane < MLA_NOPE + half)
    second = (lane >= MLA_NOPE + half) & (lane < MLA_QK)
    for fn, out_ref, fill in ((jnp.cos, cos_ref, 1.0), (jnp.sin, sin_ref, 0.0)):
        compact = fn(ang)
        rep = jnp.broadcast_to(compact[:, None, :], (blk, per_row, LANES)).reshape(blk * per_row, LANES)
        a = pltpu.roll(rep, 0, 1, stride=half, stride_axis=0)
        b = pltpu.roll(a, half, 1)
        out_ref[...] = jnp.where(first, a, jnp.where(second, b, fill))


def _rope_tables(positions):
    t = positions.size
    half = MLA_ROPE // 2
    per_row = LANES // half
    inv = 1.0 / (ROPE_THETA ** (jnp.arange(0, MLA_ROPE, 2, dtype=F32) / MLA_ROPE))
    group_token = (MLA_NOPE // half - jnp.arange(per_row)) % per_row
    pos_rep = jnp.repeat(positions.reshape(t // per_row, per_row)[:, group_token], half, axis=1)
    inv_t = jnp.tile(inv, per_row).reshape(1, LANES)
    rows = t // per_row
    blk = min(rows, 512)
    return pl.pallas_call(
        _rope_kernel,
        grid=(rows // blk,),
        in_specs=[pl.BlockSpec((blk, LANES), lambda i: (i, 0)),
                  pl.BlockSpec((1, LANES), lambda i: (0, 0))],
        out_specs=[pl.BlockSpec((blk * per_row, LANES), lambda i: (i, 0))] * 2,
        out_shape=[jax.ShapeDtypeStruct((t, LANES), F32)] * 2,
        compiler_params=_params("arbitrary"),
        name="rope_tables",
    )(pos_rep, inv_t)


def _inproj_kernel(x_ref, mod_ref, gpre_ref, win_ref, qn_ref, wq_ref, kvn_ref, wk_ref, wvt_ref,
                   cos_ref, sin_ref, z_ref, xbc_ref, dt_ref, q_ref, k_ref, vt_ref):
    x = x_ref[...]
    sh = mod_ref[0, 0:1, :]
    sc = mod_ref[0, 1:2, :]
    h = (_rms(x, gpre_ref[...]) * (1.0 + sc) + sh).astype(BF16)
    u = jnp.dot(h, win_ref[...], preferred_element_type=F32)
    z_ref[...] = u[:, _C_Z:_C_XBC].astype(BF16)
    xbc_ref[...] = u[:, _C_XBC:_C_CQ].astype(BF16)
    dt_ref[...] = u[:, _C_DT:_C_END]
    cos_t = cos_ref[...]
    sin_t = sin_ref[...]
    cq = _rms(u[:, _C_CQ:_C_CKV], qn_ref[...]).astype(BF16)
    q2 = jnp.dot(cq, wq_ref[...], preferred_element_type=F32)
    scale = MLA_QK ** -0.5 * LOG2_E
    nq = MLA_HEADS * HEAD_PAD
    for hd in range(MLA_HEADS):
        a = q2[:, hd * HEAD_PAD:(hd + 1) * HEAD_PAD]
        b = q2[:, nq + hd * HEAD_PAD:nq + (hd + 1) * HEAD_PAD]
        q_ref[:, hd * HEAD_PAD:(hd + 1) * HEAD_PAD] = ((a * cos_t + b * sin_t) * scale).astype(BF16)
    ckv = _rms(u[:, _C_CKV:_C_KR], kvn_ref[...]).astype(BF16)
    kn = jnp.dot(ckv, wk_ref[...], preferred_element_type=F32)
    k_pe = u[:, _C_KR:_C_KRS] * cos_t + u[:, _C_KRS:_C_DT] * sin_t
    for hd in range(MLA_HEADS):
        k_ref[:, hd * HEAD_PAD:(hd + 1) * HEAD_PAD] = (
            kn[:, hd * HEAD_PAD:(hd + 1) * HEAD_PAD] + k_pe).astype(BF16)
    v_t = lax.dot_general(wvt_ref[...], ckv, (((1,), (1,)), ((), ())), preferred_element_type=F32)
    head_row = lax.rem(lax.broadcasted_iota(jnp.int32, (v_t.shape[0], 1), 0), MLA_V_ROWS)
    v_t = v_t + jnp.where(head_row == MLA_V, 1.0, 0.0)
    for s in range(vt_ref.shape[0]):
        vt_ref[s] = v_t[:, s * ATT_BLOCK:(s + 1) * ATT_BLOCK].astype(BF16)


def _inproj_weights(w_in, w_uq, w_ukv):
    d = w_in.shape[0]
    half = MLA_ROPE // 2
    o_z, o_xbc, o_dt = 0, SSD_INNER, SSD_INNER + SSD_XBC
    o_cq = o_dt + SSD_HEADS
    o_ckv = o_cq + MLA_Q_RANK
    o_kr = o_ckv + MLA_KV_RANK
    zeros = lambda n: jnp.zeros((d, n), F32)
    kr = w_in[:, o_kr:o_kr + MLA_ROPE]
    kr_blk = jnp.concatenate([zeros(MLA_NOPE), kr, zeros(HEAD_PAD - MLA_QK)], axis=1)
    krs_blk = jnp.concatenate([zeros(MLA_NOPE), -kr[:, half:], kr[:, :half], zeros(HEAD_PAD - MLA_QK)], axis=1)
    dt_blk = jnp.concatenate([w_in[:, o_dt:o_dt + SSD_HEADS], zeros(LANES - SSD_HEADS)], axis=1)
    w_in_r = jnp.concatenate([w_in[:, o_z:o_dt], w_in[:, o_cq:o_kr], kr_blk, krs_blk, dt_blk], axis=1)
    r = w_uq.shape[0]
    zq = jnp.zeros((r, MLA_HEADS, HEAD_PAD - MLA_QK), F32)
    zn = jnp.zeros((r, MLA_HEADS, MLA_NOPE), F32)
    wq_plain = jnp.concatenate([w_uq, zq], axis=2).reshape(r, MLA_HEADS * HEAD_PAD)
    wq_rot = jnp.concatenate([zn, -w_uq[:, :, MLA_NOPE + half:], w_uq[:, :, MLA_NOPE:MLA_NOPE + half], zq],
                             axis=2).reshape(r, MLA_HEADS * HEAD_PAD)
    wq2 = jnp.concatenate([wq_plain, wq_rot], axis=1)
    rk = w_ukv.shape[0]
    zk = jnp.zeros((rk, MLA_HEADS, HEAD_PAD - MLA_NOPE), F32)
    wk = jnp.concatenate([w_ukv[:, :, :MLA_NOPE], zk], axis=2).reshape(rk, MLA_HEADS * HEAD_PAD)
    zv = jnp.zeros((rk, MLA_HEADS, MLA_V_ROWS - MLA_V), F32)
    wv_t = jnp.concatenate([w_ukv[:, :, MLA_NOPE:], zv], axis=2).reshape(rk, MLA_HEADS * MLA_V_ROWS).T
    return w_in_r.astype(BF16), wq2.astype(BF16), wk.astype(BF16), wv_t.astype(BF16)


def _inproj(x2, mod3, gpre, w_in_r, q_norm, wq2, kv_norm, wk, wv_t, cos_t, sin_t, seq, tm):
    t, d = x2.shape
    per_b = seq // tm
    row = lambda i: (i, 0)
    const = lambda i: (0, 0)
    nqk = MLA_HEADS * HEAD_PAD
    slabs = tm // ATT_BLOCK
    return pl.pallas_call(
        _inproj_kernel,
        grid=(t // tm,),
        in_specs=[pl.BlockSpec((tm, d), row),
                  pl.BlockSpec((1, N_MOD, d), lambda i: (i // per_b, 0, 0)),
                  pl.BlockSpec((1, d), const),
                  pl.BlockSpec(w_in_r.shape, const),
                  pl.BlockSpec((1, MLA_Q_RANK), const),
                  pl.BlockSpec(wq2.shape, const),
                  pl.BlockSpec((1, MLA_KV_RANK), const),
                  pl.BlockSpec(wk.shape, const),
                  pl.BlockSpec(wv_t.shape, const),
                  pl.BlockSpec((tm, LANES), row),
                  pl.BlockSpec((tm, LANES), row)],
        out_specs=[pl.BlockSpec((tm, SSD_INNER), row),
                   pl.BlockSpec((tm, SSD_XBC), row),
                   pl.BlockSpec((tm, LANES), row),
                   pl.BlockSpec((tm, nqk), row),
                   pl.BlockSpec((tm, nqk), row),
                   pl.BlockSpec((slabs, MLA_HEADS * MLA_V_ROWS, ATT_BLOCK), lambda i: (i, 0, 0))],
        out_shape=[jax.ShapeDtypeStruct((t, SSD_INNER), BF16),
                   jax.ShapeDtypeStruct((t, SSD_XBC), BF16),
                   jax.ShapeDtypeStruct((t, LANES), F32),
                   jax.ShapeDtypeStruct((t, nqk), BF16),
                   jax.ShapeDtypeStruct((t, nqk), BF16),
                   jax.ShapeDtypeStruct((t // ATT_BLOCK, MLA_HEADS * MLA_V_ROWS, ATT_BLOCK), BF16)],
        compiler_params=_params("arbitrary"),
        name="inproj",
    )(x2, mod3, gpre, w_in_r, q_norm, wq2, kv_norm, wk, wv_t, cos_t, sin_t)


def _split3_packed(v):
    lane = lax.broadcasted_iota(jnp.int32, v.shape, 1)
    v = jnp.where(lane < SSD_HEADS, v, 0.0)
    hi = v.astype(BF16).astype(F32)
    rest = v - hi
    mid = rest.astype(BF16).astype(F32)
    lo = rest - mid
    return (hi + pltpu.roll(mid, SSD_HEADS, 1) + pltpu.roll(lo, 2 * SSD_HEADS, 1)).astype(BF16)


def _ssd_kernel(xbc_ref, z_ref, dt_ref, cw_ref, cb_ref, dtb_ref, alog_ref, dsk_ref, gn_ref,
                e_ref, y_ref, tail_ref, state_ref):
    @pl.when(pl.program_id(1) == 0)
    def _():
        tail_ref[...] = jnp.zeros_like(tail_ref)
        state_ref[...] = jnp.zeros_like(state_ref)

    for c in range(xbc_ref.shape[0] // SSD_CHUNK):
        _ssd_chunk(pl.ds(c * SSD_CHUNK, SSD_CHUNK), xbc_ref, z_ref, dt_ref, cw_ref, cb_ref, dtb_ref,
                   alog_ref, dsk_ref, gn_ref, e_ref, y_ref, tail_ref, state_ref)


def _ssd_chunk(rows, xbc_ref, z_ref, dt_ref, cw_ref, cb_ref, dtb_ref, alog_ref, dsk_ref, gn_ref,
               e_ref, y_ref, tail_ref, state_ref):
    L = SSD_CHUNK
    gw = SSD_INNER // SSD_GROUPS

    cur_b = xbc_ref[rows, :]
    ext = jnp.concatenate([tail_ref[...], cur_b], axis=0)
    out_row = lax.broadcasted_iota(jnp.int32, ((SSD_CONV - 1) * L, ext.shape[0]), 0)
    src_row = lax.broadcasted_iota(jnp.int32, ((SSD_CONV - 1) * L, ext.shape[0]), 1)
    tail_rows = ext.shape[0] - L
    shift = jnp.where(src_row + (out_row // L + 1) == lax.rem(out_row, L) + tail_rows, 1.0, 0.0)
    shifted = jnp.dot(shift.astype(BF16), ext, preferred_element_type=F32)
    acc = cur_b.astype(F32) * cw_ref[SSD_CONV - 1:SSD_CONV, :] + cb_ref[...]
    for j in range(1, SSD_CONV):
        acc = acc + shifted[(j - 1) * L:j * L] * cw_ref[SSD_CONV - 1 - j:SSD_CONV - j, :]
    tail_ref[...] = cur_b[L - tail_rows:L]
    act = _silu(acc)
    xs = act[:, :SSD_INNER]
    bm = act[:, SSD_INNER:SSD_INNER + SSD_GROUPS * SSD_STATE].astype(BF16)
    cm = act[:, SSD_INNER + SSD_GROUPS * SSD_STATE:].astype(BF16)

    dt_in = dt_ref[rows, :] + dtb_ref[...]
    dt = jnp.maximum(dt_in, 0.0) + jnp.log(1.0 + jnp.exp(-jnp.abs(dt_in)))
    adt = dt * (-jnp.exp(alog_ref[...]))
    ri = lax.broadcasted_iota(jnp.int32, (L, L), 0)
    ci = lax.broadcasted_iota(jnp.int32, (L, L), 1)
    causal = ci <= ri
    acs_p = jnp.dot(causal.astype(BF16), _split3_packed(adt), preferred_element_type=F32)
    a_cs = (acs_p + pltpu.roll(acs_p, LANES - SSD_HEADS, 1)
            + pltpu.roll(acs_p, LANES - 2 * SSD_HEADS, 1))
    a_cs_t = a_cs.T
    cs_all = jnp.dot(_split3_packed(a_cs), e_ref[...], preferred_element_type=F32)
    cs64 = cs_all[:, :SSD_INNER]
    cs128 = cs_all[:, SSD_INNER:]
    dt64 = jnp.dot(_split3_packed(dt), e_ref[:, :SSD_INNER], preferred_element_type=F32)

    xd = xs * dt64
    xd_b = xd.astype(BF16)
    last = cs64[L - 1:L, :]
    xdw = (xd * jnp.exp(last - cs64)).astype(BF16)
    chunk_decay = jnp.exp(last)
    in_decay = jnp.exp(cs64)

    lane = lax.broadcasted_iota(jnp.int32, (L, LANES), 1)
    lo = lane < SSD_HEAD_DIM
    zero_b = jnp.zeros((L, LANES), BF16)
    y_parts = []
    new_states = []
    for g in range(SSD_GROUPS):
        bg = bm[:, g * SSD_STATE:(g + 1) * SSD_STATE]
        cg = cm[:, g * SSD_STATE:(g + 1) * SSD_STATE]
        cb = lax.dot_general(cg, bg, (((1,), (1,)), ((), ())), preferred_element_type=F32)
        prev = state_ref[:, g * gw:(g + 1) * gw]
        y_off = jnp.dot(cg, prev.astype(BF16), preferred_element_type=F32) * in_decay[:, g * gw:(g + 1) * gw]
        st = lax.dot_general(bg, xdw[:, g * gw:(g + 1) * gw], (((0,), (0,)), ((), ())),
                             preferred_element_type=F32)
        new_states.append(prev * chunk_decay[:, g * gw:(g + 1) * gw] + st)
        heads_per_group = SSD_HEADS // SSD_GROUPS
        for pair in range(heads_per_group // 2):
            h0 = g * heads_per_group + 2 * pair
            blk = xd_b[:, h0 * SSD_HEAD_DIM:(h0 + 2) * SSD_HEAD_DIM]
            y_pair = None
            for k in range(2):
                hh = h0 + k
                diff = cs128[:, hh * LANES:(hh + 1) * LANES] - a_cs_t[hh:hh + 1, :]
                decay = jnp.where(causal, jnp.exp(diff), 0.0)
                m = (cb * decay).astype(BF16)
                rhs = jnp.where(lo, blk, zero_b) if k == 0 else jnp.where(lo, zero_b, blk)
                part = jnp.dot(m, rhs, preferred_element_type=F32)
                y_pair = part if y_pair is None else y_pair + part
            c0 = (2 * pair) * SSD_HEAD_DIM
            y_parts.append(y_pair + y_off[:, c0:c0 + LANES])
    state_ref[...] = jnp.concatenate(new_states, axis=1)
    y = jnp.concatenate(y_parts, axis=1) + dsk_ref[...] * xs
    gated = y * _silu(z_ref[rows, :].astype(F32))
    y_ref[rows, :] = _rms(gated, gn_ref[...]).astype(BF16)


def _ssd(xbc, z, dt_raw, conv_w, conv_b, dt_bias, a_log, d_skip, ssd_norm, batch, seq):
    t = xbc.shape[0]
    L = SSD_STEP_ROWS
    nc = seq // L
    row = lambda b, c: (b * nc + c, 0)
    const = lambda b, c: (0, 0)
    pad = lambda v: jnp.concatenate([v.reshape(1, -1), jnp.zeros((1, LANES - v.size), F32)], axis=1)
    head_of_lane = jnp.concatenate([jnp.arange(SSD_INNER) // SSD_HEAD_DIM,
                                    jnp.arange(SSD_HEADS * LANES) // LANES])
    rows = jnp.arange(LANES)[:, None]
    expand = ((rows < 3 * SSD_HEADS) & (rows % SSD_HEADS == head_of_lane[None, :])).astype(BF16)
    dsk = jnp.repeat(d_skip.astype(F32), SSD_HEAD_DIM).reshape(1, SSD_INNER)
    return pl.pallas_call(
        _ssd_kernel,
        grid=(batch, nc),
        in_specs=[pl.BlockSpec((L, SSD_XBC), row),
                  pl.BlockSpec((L, SSD_INNER), row),
                  pl.BlockSpec((L, LANES), row),
                  pl.BlockSpec((SSD_CONV, SSD_XBC), const),
                  pl.BlockSpec((1, SSD_XBC), const),
                  pl.BlockSpec((1, LANES), const),
                  pl.BlockSpec((1, LANES), const),
                  pl.BlockSpec((1, SSD_INNER), const),
                  pl.BlockSpec((1, SSD_INNER), const),
                  pl.BlockSpec(expand.shape, const)],
        out_specs=pl.BlockSpec((L, SSD_INNER), row),
        out_shape=jax.ShapeDtypeStruct((t, SSD_INNER), BF16),
        scratch_shapes=[pltpu.VMEM((16, SSD_XBC), BF16),
                        pltpu.VMEM((SSD_STATE, SSD_INNER), F32)],
        compiler_params=_params("arbitrary", "arbitrary"),
        name="ssd",
    )(xbc, z, dt_raw, conv_w, conv_b.reshape(1, -1), pad(dt_bias), pad(a_log), dsk,
      ssd_norm.reshape(1, -1), expand)


def _attn_kernel(q_ref, k_ref, vt_ref, gn_ref, o_ref, s_ref):
    blk = ATT_BLOCK
    qi = pl.program_id(1)
    causal = (lax.broadcasted_iota(jnp.int32, (blk, blk), 0)
              <= lax.broadcasted_iota(jnp.int32, (blk, blk), 1))
    dims = (((1,), (1,)), ((), ()))
    qs = [q_ref[:, hd * HEAD_PAD:(hd + 1) * HEAD_PAD] for hd in range(MLA_HEADS)]

    def step(j, carry, masked):
        rows = pl.ds(pl.multiple_of(j * blk, blk), blk)
        m_new = []
        for hd in range(MLA_HEADS):
            kk = k_ref[rows, hd * HEAD_PAD:(hd + 1) * HEAD_PAD]
            s = lax.dot_general(kk, qs[hd], dims, preferred_element_type=F32)
            if masked:
                s = jnp.where(causal, s, -jnp.inf)
            s_ref[hd] = s
            m_new.append(jnp.maximum(carry[hd][0], jnp.max(s, axis=0, keepdims=True)))
        new = []
        for hd in range(MLA_HEADS):
            m_old, acc = carry[hd]
            p = jnp.exp2(s_ref[hd] - m_new[hd])
            alpha = jnp.exp2(m_old - m_new[hd])
            vt = vt_ref[j, hd * MLA_V_ROWS:(hd + 1) * MLA_V_ROWS, :]
            acc = acc * alpha + jnp.dot(vt, p.astype(BF16), preferred_element_type=F32)
            new.append((m_new[hd], acc))
        return tuple(new)

    neg = jnp.full((1, blk), -jnp.inf, F32)
    init = tuple((neg, jnp.zeros((MLA_V_ROWS, blk), F32)) for _ in range(MLA_HEADS))
    carry = lax.fori_loop(0, qi, functools.partial(step, masked=False), init)
    final = step(qi, carry, True)
    y_t = jnp.concatenate([acc[:MLA_V] / acc[MLA_V:MLA_V + 1] for _, acc in final], axis=0)
    o_ref[...] = _rms(y_t.T, gn_ref[...]).astype(BF16)


def _attention(q, k, v_t, gn, batch, seq):
    t = q.shape[0]
    blk = ATT_BLOCK
    nq = seq // blk
    nqk = MLA_HEADS * HEAD_PAD
    return pl.pallas_call(
        _attn_kernel,
        grid=(batch, nq),
        in_specs=[pl.BlockSpec((blk, nqk), lambda b, i: (b * nq + i, 0)),
                  pl.BlockSpec((seq, nqk), lambda b, i: (b, 0)),
                  pl.BlockSpec((nq, MLA_HEADS * MLA_V_ROWS, blk), lambda b, i: (b, 0, 0)),
                  pl.BlockSpec((1, MLA_INNER), lambda b, i: (0, 0))],
        out_specs=pl.BlockSpec((blk, MLA_INNER), lambda b, i: (b * nq + i, 0)),
        out_shape=jax.ShapeDtypeStruct((t, MLA_INNER), BF16),
        scratch_shapes=[pltpu.VMEM((MLA_HEADS, blk, blk), F32)],
        compiler_params=_params("arbitrary", "arbitrary"),
        name="attention",
    )(q, k, v_t, gn)


def _outproj_kernel(ys_ref, ym_ref, x_ref, mod_ref, wtop_ref, wbot_ref, gpm_ref, gpf_ref,
                    wrh_ref, wrl_ref, br_ref, x1_ref, h2_ref, col_ref, info_ref, cnt_ref, ce_ref):
    y = (jnp.dot(ys_ref[...], wtop_ref[...], preferred_element_type=F32)
         + jnp.dot(ym_ref[...], wbot_ref[...], preferred_element_type=F32))
    g1 = mod_ref[0, 2:3, :]
    sh2 = mod_ref[0, 3:4, :]
    sc2 = mod_ref[0, 4:5, :]
    x1 = x_ref[...] + g1 * _rms(y, gpm_ref[...])
    x1_ref[...] = x1
    h2 = _rms(x1, gpf_ref[...]) * (1.0 + sc2) + sh2
    tm = h2.shape[0]
    h_hi = h2.astype(BF16)
    h_lo = (h2 - h_hi.astype(F32)).astype(BF16)
    logits = (jnp.dot(h_hi, wrh_ref[...], preferred_element_type=F32)
              + jnp.dot(h_lo, wrh_ref[...], preferred_element_type=F32)
              + jnp.dot(h_hi, wrl_ref[...], preferred_element_type=F32)) + br_ref[...]
    lt = logits.T
    row = lax.broadcasted_iota(jnp.int32, lt.shape, 0).astype(F32)
    ninf = -jnp.inf
    big = 1e9
    is_g = row < MOE_GROUPS
    gl = jnp.where(is_g, lt, ninf)
    gmax = jnp.max(gl, axis=0, keepdims=True)
    gsum = jnp.sum(jnp.where(is_g, jnp.exp(lt - gmax), 0.0), axis=0, keepdims=True)
    g_w = 1.0 / gsum
    g_idx = jnp.min(jnp.where(gl == gmax, row, big), axis=0, keepdims=True)
    first = MOE_GROUPS + MOE_EXPERTS_PER_GROUP * g_idx
    el = jnp.where(row >= first, jnp.where(row < first + MOE_EXPERTS_PER_GROUP, lt, ninf), ninf)
    m1 = jnp.max(el, axis=0, keepdims=True)
    i1 = jnp.min(jnp.where(el == m1, row, big), axis=0, keepdims=True)
    el2 = jnp.where(row == i1, ninf, el)
    m2 = jnp.max(el2, axis=0, keepdims=True)
    i2 = jnp.min(jnp.where(el2 == m2, row, big), axis=0, keepdims=True)
    r = jnp.exp(m2 - m1)
    w1 = g_w / (1.0 + r)
    w2 = g_w * r / (1.0 + r)
    sel1 = row == i1
    sel2 = row == i2
    both = jnp.where(sel1, 1.0, jnp.where(sel2, 1.0, 0.0))
    cnt16 = jnp.floor((jnp.sum(both, axis=1, keepdims=True) + (MOE_CHUNK - 1)) * (1.0 / MOE_CHUNK))
    er = lax.broadcasted_iota(jnp.int32, (LANES, LANES), 0)
    ec = lax.broadcasted_iota(jnp.int32, (LANES, LANES), 1)
    cnt16_b = jnp.broadcast_to(cnt16, (LANES, LANES))
    off16 = jnp.dot((ec < er).astype(BF16), cnt16_b.astype(BF16), preferred_element_type=F32)[:, 0:1]
    earlier = (lax.broadcasted_iota(jnp.int32, (tm, tm), 0)
               < lax.broadcasted_iota(jnp.int32, (tm, tm), 1)).astype(BF16)
    base = off16 * MOE_CHUNK + jnp.dot(both.astype(BF16), earlier, preferred_element_type=F32)
    lpos1 = jnp.sum(jnp.where(sel1, base, 0.0), axis=0, keepdims=True)
    lpos2 = jnp.sum(jnp.where(sel2, base, 0.0), axis=0, keepdims=True)
    h2_ref[...] = h_hi
    col_ref[...] = jnp.where(row == 0, lpos1, jnp.where(row == 1, lpos2, 0.0)).T
    row8 = lax.broadcasted_iota(jnp.int32, info_ref.shape, 0)
    info_ref[...] = jnp.where(row8 == 0, lpos1, jnp.where(row8 == 1, lpos2,
                              jnp.where(row8 == 2, w1, jnp.where(row8 == 3, w2, 0.0))))
    cnt_ref[...] = cnt16_b
    chunk = lax.broadcasted_iota(jnp.int32, (LANES, LANES), 1).astype(F32)
    ce = jnp.sum(jnp.where(off16 + cnt16 <= chunk, 1.0, 0.0), axis=0, keepdims=True) - MOE_GROUPS
    ce_ref[...] = jnp.broadcast_to(ce, ce_ref.shape)


def _outproj(y_ssd, y_mla, x2, mod3, w_top, w_bot, gpm, gpf, wr_hi, wr_lo, b_router, seq, tm):
    t, d = x2.shape
    per_b = seq // tm
    n_tok_tiles = t // tm
    row = lambda i: (i, 0)
    const = lambda i: (0, 0)
    return pl.pallas_call(
        _outproj_kernel,
        grid=(n_tok_tiles,),
        in_specs=[pl.BlockSpec((tm, SSD_INNER), row),
                  pl.BlockSpec((tm, MLA_INNER), row),
                  pl.BlockSpec((tm, d), row),
                  pl.BlockSpec((1, N_MOD, d), lambda i: (i // per_b, 0, 0)),
                  pl.BlockSpec(w_top.shape, const),
                  pl.BlockSpec(w_bot.shape, const),
                  pl.BlockSpec((1, d), const),
                  pl.BlockSpec((1, d), const),
                  pl.BlockSpec(wr_hi.shape, const),
                  pl.BlockSpec(wr_lo.shape, const),
                  pl.BlockSpec((1, LANES), const)],
        out_specs=[pl.BlockSpec((tm, d), row),
                   pl.BlockSpec((tm, d), row),
                   pl.BlockSpec((tm, LANES), row),
                   pl.BlockSpec((8, tm), row),
                   pl.BlockSpec((LANES, LANES), row),
                   pl.BlockSpec((8, LANES), row)],
        out_shape=[jax.ShapeDtypeStruct((t, d), F32),
                   jax.ShapeDtypeStruct((t, d), BF16),
                   jax.ShapeDtypeStruct((t, LANES), F32),
                   jax.ShapeDtypeStruct((n_tok_tiles * 8, tm), F32),
                   jax.ShapeDtypeStruct((n_tok_tiles * LANES, LANES), F32),
                   jax.ShapeDtypeStruct((n_tok_tiles * 8, LANES), F32)],
        compiler_params=_params("arbitrary"),
        name="outproj_router",
    )(y_ssd, y_mla, x2, mod3, w_top, w_bot, gpm, gpf, wr_hi, wr_lo, b_router)


def _moe_rows(n_tok_tiles):
    rows = n_tok_tiles * MOE_SLOTS + MOE_EXPERTS * (MOE_TILE - MOE_CHUNK)
    return (rows + MOE_TILE - 1) // MOE_TILE * MOE_TILE


def _moe_plan(cnt_tiles, chunk_expert, n_tok_tiles):
    chunks_per_tile = MOE_SLOTS // MOE_CHUNK
    chunks_per_rows = MOE_TILE // MOE_CHUNK
    cnt = cnt_tiles.reshape(n_tok_tiles, LANES, LANES)[:, MOE_GROUPS:MOE_GROUPS + MOE_EXPERTS, 0]
    cnt = cnt.astype(jnp.int32)
    ce = chunk_expert.reshape(n_tok_tiles, 8, LANES)[:, 0, :chunks_per_tile].astype(jnp.int32)
    used = jnp.sum(cnt, axis=1)
    total = jnp.sum(cnt, axis=0)
    padded = (total + chunks_per_rows - 1) // chunks_per_rows * chunks_per_rows
    e_end = jnp.cumsum(padded)
    e_start = e_end - padded
    run_global = e_start[None, :] + jnp.cumsum(cnt, axis=0) - cnt
    run_local = jnp.cumsum(cnt, axis=1) - cnt
    experts = jnp.arange(MOE_EXPERTS, dtype=jnp.int32)
    shift = jnp.sum(jnp.where(ce[:, :, None] == experts, (run_global - run_local)[:, None, :], 0), axis=-1)
    dst = shift + jnp.arange(chunks_per_tile, dtype=jnp.int32)[None, :]
    n_used = e_end[-1] // chunks_per_rows
    i32 = lambda v: v.astype(jnp.int32)
    return (i32(dst.reshape(-1)), i32(used), i32(e_start + total), i32(padded - total),
            i32(n_used.reshape(1)), i32(e_start // chunks_per_rows), i32(padded // chunks_per_rows))


def _chunk_rows(chunk):
    return pl.ds(pl.multiple_of(chunk * MOE_CHUNK, MOE_CHUNK), MOE_CHUNK)


def _run_copies(*groups):
    for wait in (False, True):
        for lo, hi, make_copy in groups:
            def body(k, carry, wait=wait, make_copy=make_copy):
                cp = make_copy(k)
                cp.wait() if wait else cp.start()
                return carry
            lax.fori_loop(lo, hi, body, 0)


def _scatter_chunks_kernel(dst_ref, used_ref, pad_start_ref, pad_n_ref, nused_ref, h2_ref, info_ref,
                           o_ref, wslot_ref, src_ref, zero_ref, sem):
    i = pl.program_id(0)
    chunks = MOE_SLOTS // MOE_CHUNK
    slot = lax.rem(i, 2)

    info = info_ref[...]
    srow = lax.broadcasted_iota(jnp.int32, (MOE_SLOTS, info.shape[1]), 0)
    hit1 = srow == info[0:1, :].astype(jnp.int32)
    hit2 = srow == info[1:2, :].astype(jnp.int32)
    perm = jnp.where(hit1, 1.0, jnp.where(hit2, 1.0, 0.0)).astype(BF16)
    src_ref[slot] = jnp.dot(perm, h2_ref[...], preferred_element_type=F32).astype(BF16)
    wslot = jnp.sum(jnp.where(hit1, info[2:3, :], jnp.where(hit2, info[3:4, :], 0.0)), axis=1, keepdims=True)
    wslot_ref[...] = jnp.broadcast_to(wslot, wslot_ref.shape)

    @pl.when(i == 0)
    def _():
        zero_ref[...] = jnp.zeros_like(zero_ref)
        pads = [(0, pad_n_ref[e], lambda c, e=e: pltpu.make_async_copy(
            zero_ref.at[pl.ds(0, MOE_CHUNK), :], o_ref.at[_chunk_rows(pad_start_ref[e] + c), :], sem.at[2]))
            for e in range(MOE_EXPERTS)]
        tail = (nused_ref[0], o_ref.shape[0] // MOE_TILE, lambda j: pltpu.make_async_copy(
            zero_ref, o_ref.at[pl.ds(pl.multiple_of(j * MOE_TILE, MOE_TILE), MOE_TILE), :], sem.at[2]))
        _run_copies(tail, *pads)

    def chunk_copies(tile, buf, wait):
        def body(c, carry):
            cp = pltpu.make_async_copy(src_ref.at[buf, _chunk_rows(c), :],
                                       o_ref.at[_chunk_rows(dst_ref[tile * chunks + c]), :], sem.at[buf])
            cp.wait() if wait else cp.start()
            return carry
        lax.fori_loop(0, used_ref[tile], body, 0)

    chunk_copies(i, slot, False)

    @pl.when(i > 0)
    def _():
        chunk_copies(i - 1, 1 - slot, True)

    @pl.when(i == pl.num_programs(0) - 1)
    def _():
        chunk_copies(i, slot, True)


def _scatter_chunks(h2, info, dst, used, pad_start, pad_n, n_used, n_rows):
    t, d = h2.shape
    tm = info.shape[1]
    row = lambda i, *_: (i, 0)
    return pl.pallas_call(
        _scatter_chunks_kernel,
        grid_spec=pltpu.PrefetchScalarGridSpec(
            num_scalar_prefetch=5,
            grid=(used.shape[0],),
            in_specs=[pl.BlockSpec((tm, d), row), pl.BlockSpec((8, tm), row)],
            out_specs=[pl.BlockSpec(memory_space=pl.ANY), pl.BlockSpec((MOE_SLOTS, LANES), row)],
            scratch_shapes=[pltpu.VMEM((2, MOE_SLOTS, d), BF16), pltpu.VMEM((MOE_TILE, d), BF16),
                            pltpu.SemaphoreType.DMA((3,))]),
        out_shape=[jax.ShapeDtypeStruct((n_rows, d), BF16),
                   jax.ShapeDtypeStruct((used.shape[0] * MOE_SLOTS, LANES), F32)],
        compiler_params=_params("arbitrary"),
        name="moe_scatter_chunks",
    )(dst, used, pad_start, pad_n, n_used, h2, info)


def _experts_kernel(first_ref, count_ref, nused_ref, wg_ref, wu_ref, wd_ref, xs_ref, ys_ref,
                    x_buf, y_buf, wg_b, wu_b, wd_b, sem_in, sem_out):
    e = pl.program_id(0)
    first = first_ref[e]
    count = count_ref[e]
    wg_b[...] = wg_ref[0].astype(BF16)
    wu_b[...] = wu_ref[0].astype(BF16)
    wd_b[...] = wd_ref[0].astype(BF16)

    def tile_rows(j):
        return pl.ds(pl.multiple_of((first + j) * MOE_TILE, MOE_TILE), MOE_TILE)

    def in_copy(j, buf):
        return pltpu.make_async_copy(xs_ref.at[tile_rows(j), :], x_buf.at[buf], sem_in.at[buf])

    def out_copy(j, buf):
        return pltpu.make_async_copy(y_buf.at[buf], ys_ref.at[tile_rows(j), :], sem_out.at[buf])

    @pl.when(count > 0)
    def _():
        in_copy(0, 0).start()

    def body(j, carry):
        buf = lax.rem(j, 2)
        in_copy(j, buf).wait()

        @pl.when(j + 1 < count)
        def _():
            in_copy(j + 1, 1 - buf).start()

        x = x_buf[buf]
        gate = jnp.dot(x, wg_b[...], preferred_element_type=F32)
        up = jnp.dot(x, wu_b[...], preferred_element_type=F32)
        hid = (_silu(gate) * up).astype(BF16)
        y = jnp.dot(hid, wd_b[...], preferred_element_type=F32).astype(BF16)

        @pl.when(j >= 2)
        def _():
            out_copy(j - 2, buf).wait()

        y_buf[buf] = y
        out_copy(j, buf).start()
        return carry

    lax.fori_loop(0, count, body, 0)

    @pl.when(count >= 2)
    def _():
        out_copy(count - 2, lax.rem(count, 2)).wait()

    @pl.when(count >= 1)
    def _():
        out_copy(count - 1, lax.rem(count - 1, 2)).wait()

    @pl.when(e == pl.num_programs(0) - 1)
    def _():
        y_buf[0] = jnp.zeros(y_buf.shape[1:], BF16)
        _run_copies((nused_ref[0], ys_ref.shape[0] // MOE_TILE, lambda j: pltpu.make_async_copy(
            y_buf.at[0], ys_ref.at[pl.ds(pl.multiple_of(j * MOE_TILE, MOE_TILE), MOE_TILE), :], sem_out.at[0])))


def _experts(xs, first_tile, tile_count, n_used, w_gate, w_up, w_down):
    n_rows, d = xs.shape
    by_expert = lambda e, *_: (e, 0, 0)
    return pl.pallas_call(
        _experts_kernel,
        grid_spec=pltpu.PrefetchScalarGridSpec(
            num_scalar_prefetch=3,
            grid=(MOE_EXPERTS,),
            in_specs=[pl.BlockSpec((1, d, MOE_FF), by_expert),
                      pl.BlockSpec((1, d, MOE_FF), by_expert),
                      pl.BlockSpec((1, MOE_FF, d), by_expert),
                      pl.BlockSpec(memory_space=pl.ANY)],
            out_specs=pl.BlockSpec(memory_space=pl.ANY),
            scratch_shapes=[pltpu.VMEM((2, MOE_TILE, d), BF16), pltpu.VMEM((2, MOE_TILE, d), BF16),
                            pltpu.VMEM((d, MOE_FF), BF16), pltpu.VMEM((d, MOE_FF), BF16),
                            pltpu.VMEM((MOE_FF, d), BF16),
                            pltpu.SemaphoreType.DMA((2,)), pltpu.SemaphoreType.DMA((2,))]),
        out_shape=jax.ShapeDtypeStruct((n_rows, d), BF16),
        compiler_params=_params("arbitrary"),
        name="moe_experts",
    )(first_tile, tile_count, n_used, w_gate, w_up, w_down, xs)


def _combine_kernel(dst_ref, used_ref, wslot_ref, col_ref, x1_ref, mod_ref, gpost_ref, ys_ref, o_ref,
                    ysl_ref, sem):
    i = pl.program_id(0)
    chunks = MOE_SLOTS // MOE_CHUNK
    slot = lax.rem(i, 2)

    def fetch(tile, buf, wait):
        base = tile * chunks
        used = used_ref[tile]

        def copy(c, carry):
            cp = pltpu.make_async_copy(ys_ref.at[_chunk_rows(dst_ref[base + c]), :],
                                       ysl_ref.at[buf, _chunk_rows(c), :], sem.at[buf])
            cp.wait() if wait else cp.start()
            return carry

        def clear(c, carry):
            ysl_ref[buf, _chunk_rows(c), :] = jnp.zeros((MOE_CHUNK, ysl_ref.shape[2]), BF16)
            return carry

        lax.fori_loop(0, used, copy, 0)
        if not wait:
            lax.fori_loop(used, chunks, clear, 0)

    @pl.when(i == 0)
    def _():
        fetch(0, 0, False)

    @pl.when(i + 1 < pl.num_programs(0))
    def _():
        fetch(i + 1, 1 - slot, False)

    col = col_ref[...]
    tm = col.shape[0]
    lane = lax.broadcasted_iota(jnp.int32, (tm, MOE_SLOTS), 1)
    hit = jnp.where(lane == col[:, 0:1].astype(jnp.int32), 1.0,
                    jnp.where(lane == col[:, 1:2].astype(jnp.int32), 1.0, 0.0)).astype(BF16)
    fetch(i, slot, True)
    ysw = (ysl_ref[slot].astype(F32) * wslot_ref[:, 0:1]).astype(BF16)
    y = jnp.dot(hit, ysw, preferred_element_type=F32)
    g2 = mod_ref[0, 5:6, :]
    o_ref[...] = x1_ref[...] + g2 * _rms(y, gpost_ref[...])


def _combine(ys, dst, used, wslot, col, x1, mod3, gpost, seq, tm):
    t, d = x1.shape
    per_b = seq // tm
    row = lambda i, *_: (i, 0)
    return pl.pallas_call(
        _combine_kernel,
        grid_spec=pltpu.PrefetchScalarGridSpec(
            num_scalar_prefetch=2,
            grid=(t // tm,),
            in_specs=[pl.BlockSpec((MOE_SLOTS, LANES), row),
                      pl.BlockSpec((tm, LANES), row),
                      pl.BlockSpec((tm, d), row),
                      pl.BlockSpec((1, N_MOD, d), lambda i, *_: (i // per_b, 0, 0)),
                      pl.BlockSpec((1, d), lambda i, *_: (0, 0)),
                      pl.BlockSpec(memory_space=pl.ANY)],
            out_specs=pl.BlockSpec((tm, d), row),
            scratch_shapes=[pltpu.VMEM((2, MOE_SLOTS, d), BF16), pltpu.SemaphoreType.DMA((2,))]),
        out_shape=jax.ShapeDtypeStruct((t, d), F32),
        compiler_params=_params("arbitrary"),
        name="moe_combine",
    )(dst, used, wslot, col, x1, mod3, gpost, ys)


def kernel(x, c, positions, ada_w, ada_b, pre_norm_mix, post_norm_mix, pre_norm_ffn, post_norm_ffn, w_in, conv_w, conv_b, dt_bias, a_log, d_skip, ssd_norm, q_norm, w_uq, kv_norm, w_ukv, mla_out_norm, w_out, w_group_router, b_group_router, w_expert_router, b_expert_router, w_gate, w_up, w_down):
    batch, seq, d = x.shape
    t = batch * seq
    depth = ada_w.shape[0]
    tm = min(INPROJ_TOKENS, seq)
    cos_t, sin_t = _rope_tables(positions)

    x2 = x.reshape(t, d)
    for l in range(depth):
        mod3 = _modulation(c, ada_w[l], ada_b[l]).reshape(batch, N_MOD, d)
        w_in_r, wq2, wk, wv_t = _inproj_weights(w_in[l], w_uq[l], w_ukv[l])
        z, xbc, dt_raw, q, k, v_t = _inproj(
            x2, mod3, pre_norm_mix[l].reshape(1, d), w_in_r, q_norm[l].reshape(1, -1), wq2,
            kv_norm[l].reshape(1, -1), wk, wv_t, cos_t, sin_t, seq, tm)
        y_ssd = _ssd(xbc, z, dt_raw, conv_w[l], conv_b[l], dt_bias[l], a_log[l], d_skip[l],
                     ssd_norm[l], batch, seq)
        y_mla = _attention(q, k, v_t, mla_out_norm[l].reshape(1, -1), batch, seq)
        w_o = w_out[l].astype(BF16)
        pad_r = LANES - MOE_GROUPS - MOE_EXPERTS
        w_router = jnp.concatenate([w_group_router[l], w_expert_router[l], jnp.zeros((d, pad_r), F32)], axis=1)
        b_router = jnp.concatenate([b_group_router[l].reshape(-1), b_expert_router[l].reshape(-1),
                                    jnp.zeros((pad_r,), F32)]).reshape(1, LANES)
        wr_hi = w_router.astype(BF16)
        wr_lo = (w_router - wr_hi.astype(F32)).astype(BF16)
        x1, h2, col, info, cnt_tiles, chunk_expert = _outproj(
            y_ssd, y_mla, x2, mod3, w_o[:SSD_INNER], w_o[SSD_INNER:], post_norm_mix[l].reshape(1, d),
            pre_norm_ffn[l].reshape(1, d), wr_hi, wr_lo, b_router, seq, MOE_TOKENS)
        n_tok_tiles = t // MOE_TOKENS
        n_rows = _moe_rows(n_tok_tiles)
        dst, used, pad_start, pad_n, n_used, first_tile, tile_count = _moe_plan(
            cnt_tiles, chunk_expert, n_tok_tiles)
        xs, wslot = _scatter_chunks(h2, info, dst, used, pad_start, pad_n, n_used, n_rows)
        ys = _experts(xs, first_tile, tile_count, n_used, w_gate[l], w_up[l], w_down[l])
        x2 = _combine(ys, dst, used, wslot, col, x1, mod3, post_norm_ffn[l].reshape(1, d), seq, MOE_TOKENS)
    return x2.reshape(batch, seq, d)
```

```python
import functools
import math

import jax
import jax.numpy as jnp
from jax import lax
from jax.experimental import pallas as pl
from jax.experimental.pallas import tpu as pltpu

F32 = jnp.float32
BF16 = jnp.bfloat16

D_MODEL = 1024
SSD_HEADS = 8
SSD_HEAD_DIM = 64
SSD_INNER = SSD_HEADS * SSD_HEAD_DIM
SSD_GROUPS = 2
SSD_STATE = 128
SSD_CONV = 4
SSD_CHUNK = 128
SSD_XBC = SSD_INNER + 2 * SSD_GROUPS * SSD_STATE
MLA_HEADS = 8
MLA_NOPE = 64
MLA_ROPE = 32
MLA_QK = MLA_NOPE + MLA_ROPE
MLA_V = 64
MLA_Q_RANK = 256
MLA_KV_RANK = 128
MLA_INNER = MLA_HEADS * MLA_V
ROPE_THETA = 10000.0
MOE_GROUPS = 4
MOE_EXPERTS_PER_GROUP = 8
MOE_EXPERTS = MOE_GROUPS * MOE_EXPERTS_PER_GROUP
MOE_FF = 256
N_MOD = 6
EPS = 1e-6
LOG2_E = math.log2(math.e)

LANES = 128
HEAD_PAD = 128
ATT_BLOCK = 256
MLA_V_ROWS = MLA_V + 16
INPROJ_TOKENS = 512
SSD_STEP_ROWS = 4 * SSD_CHUNK
MOE_TOKENS = 512
MOE_CHUNK = 16
MOE_TILE = 512
MOE_SLOTS = 2 * MOE_TOKENS + MOE_EXPERTS * MOE_CHUNK
VMEM_LIMIT = 48 * 1024 * 1024

_C_Z = 0
_C_XBC = _C_Z + SSD_INNER
_C_CQ = _C_XBC + SSD_XBC
_C_CKV = _C_CQ + MLA_Q_RANK
_C_KR = _C_CKV + MLA_KV_RANK
_C_KRS = _C_KR + LANES
_C_DT = _C_KRS + LANES
_C_END = _C_DT + LANES


def _silu(v):
    return v * (1.0 / (1.0 + jnp.exp(-v)))


def _rms(v, gain):
    return v * lax.rsqrt(jnp.mean(v * v, axis=-1, keepdims=True) + EPS) * gain


def _params(*sem, flags=None):
    return pltpu.CompilerParams(dimension_semantics=sem, vmem_limit_bytes=VMEM_LIMIT, flags=flags)


def _mod_kernel(c_ref, w_ref, b_ref, o_ref):
    act = _silu(c_ref[...])
    a_hi = act.astype(BF16)
    a_lo = (act - a_hi.astype(F32)).astype(BF16)
    w = w_ref[...]
    w_hi = w.astype(BF16)
    w_lo = (w - w_hi.astype(F32)).astype(BF16)
    o_ref[...] = (jnp.dot(a_hi, w_hi, preferred_element_type=F32)
                  + jnp.dot(a_lo, w_hi, preferred_element_type=F32)
                  + jnp.dot(a_hi, w_lo, preferred_element_type=F32)) + b_ref[...]


def _modulation(c, ada_w, ada_b):
    b, d = c.shape
    n = ada_w.shape[1]
    return pl.pallas_call(
        _mod_kernel,
        grid=(n // d,),
        in_specs=[pl.BlockSpec((b, d), lambda j: (0, 0)),
                  pl.BlockSpec((d, d), lambda j: (0, j)),
                  pl.BlockSpec((1, d), lambda j: (0, j))],
        out_specs=pl.BlockSpec((b, d), lambda j: (0, j)),
        out_shape=jax.ShapeDtypeStruct((b, n), F32),
        compiler_params=_params("arbitrary"),
        name="modulation",
    )(c, ada_w, ada_b.reshape(1, n))


def _rope_kernel(pos_ref, inv_ref, cos_ref, sin_ref):
    ang = pos_ref[...].astype(F32) * inv_ref[...]
    blk = ang.shape[0]
    half = MLA_ROPE // 2
    per_row = LANES // half
    lane = lax.broadcasted_iota(jnp.int32, (blk * per_row, LANES), 1)
    first = (lane >= MLA_NOPE) & (lane < MLA_NOPE + half)
    second = (lane >= MLA_NOPE + half) & (lane < MLA_QK)
    for fn, out_ref, fill in ((jnp.cos, cos_ref, 1.0), (jnp.sin, sin_ref, 0.0)):
        compact = fn(ang)
        rep = jnp.broadcast_to(compact[:, None, :], (blk, per_row, LANES)).reshape(blk * per_row, LANES)
        a = pltpu.roll(rep, 0, 1, stride=half, stride_axis=0)
        b = pltpu.roll(a, half, 1)
        out_ref[...] = jnp.where(first, a, jnp.where(second, b, fill))


def _rope_tables(positions):
    t = positions.size
    half = MLA_ROPE // 2
    per_row = LANES // half
    inv = 1.0 / (ROPE_THETA ** (jnp.arange(0, MLA_ROPE, 2, dtype=F32) / MLA_ROPE))
    group_token = (MLA_NOPE // half - jnp.arange(per_row)) % per_row
    pos_rep = jnp.repeat(positions.reshape(t // per_row, per_row)[:, group_token], half, axis=1)
    inv_t = jnp.tile(inv, per_row).reshape(1, LANES)
    rows = t // per_row
    blk = min(rows, 512)
    return pl.pallas_call(
        _rope_kernel,
        grid=(rows // blk,),
        in_specs=[pl.BlockSpec((blk, LANES), lambda i: (i, 0)),
                  pl.BlockSpec((1, LANES), lambda i: (0, 0))],
        out_specs=[pl.BlockSpec((blk * per_row, LANES), lambda i: (i, 0))] * 2,
        out_shape=[jax.ShapeDtypeStruct((t, LANES), F32)] * 2,
        compiler_params=_params("arbitrary"),
        name="rope_tables",
    )(pos_rep, inv_t)


def _inproj_kernel(x_ref, mod_ref, gpre_ref, win_ref, qn_ref, wq_ref, kvn_ref, wk_ref, wvt_ref,
                   cos_ref, sin_ref, z_ref, xbc_ref, dt_ref, q_ref, k_ref, vt_ref):
    x = x_ref[...]
    sh = mod_ref[0, 0:1, :]
    sc = mod_ref[0, 1:2, :]
    h = (_rms(x, gpre_ref[...]) * (1.0 + sc) + sh).astype(BF16)
    u = jnp.dot(h, win_ref[...], preferred_element_type=F32)
    z_ref[...] = u[:, _C_Z:_C_XBC].astype(BF16)
    xbc_ref[...] = u[:, _C_XBC:_C_CQ].astype(BF16)
    dt_ref[...] = u[:, _C_DT:_C_END]
    cos_t = cos_ref[...]
    sin_t = sin_ref[...]
    cq = _rms(u[:, _C_CQ:_C_CKV], qn_ref[...]).astype(BF16)
    q2 = jnp.dot(cq, wq_ref[...], preferred_element_type=F32)
    scale = MLA_QK ** -0.5 * LOG2_E
    nq = MLA_HEADS * HEAD_PAD
    for hd in range(MLA_HEADS):
        a = q2[:, hd * HEAD_PAD:(hd + 1) * HEAD_PAD]
        b = q2[:, nq + hd * HEAD_PAD:nq + (hd + 1) * HEAD_PAD]
        q_ref[:, hd * HEAD_PAD:(hd + 1) * HEAD_PAD] = ((a * cos_t + b * sin_t) * scale).astype(BF16)
    ckv = _rms(u[:, _C_CKV:_C_KR], kvn_ref[...]).astype(BF16)
    kn = jnp.dot(ckv, wk_ref[...], preferred_element_type=F32)
    k_pe = u[:, _C_KR:_C_KRS] * cos_t + u[:, _C_KRS:_C_DT] * sin_t
    for hd in range(MLA_HEADS):
        k_ref[:, hd * HEAD_PAD:(hd + 1) * HEAD_PAD] = (
            kn[:, hd * HEAD_PAD:(hd + 1) * HEAD_PAD] + k_pe).astype(BF16)
    v_t = lax.dot_general(wvt_ref[...], ckv, (((1,), (1,)), ((), ())), preferred_element_type=F32)
    head_row = lax.rem(lax.broadcasted_iota(jnp.int32, (v_t.shape[0], 1), 0), MLA_V_ROWS)
    v_t = v_t + jnp.where(head_row == MLA_V, 1.0, 0.0)
    for s in range(vt_ref.shape[0]):
        vt_ref[s] = v_t[:, s * ATT_BLOCK:(s + 1) * ATT_BLOCK].astype(BF16)


def _inproj_weights(w_in, w_uq, w_ukv):
    d = w_in.shape[0]
    half = MLA_ROPE // 2
    o_z, o_xbc, o_dt = 0, SSD_INNER, SSD_INNER + SSD_XBC
    o_cq = o_dt + SSD_HEADS
    o_ckv = o_cq + MLA_Q_RANK
    o_kr = o_ckv + MLA_KV_RANK
    zeros = lambda n: jnp.zeros((d, n), F32)
    kr = w_in[:, o_kr:o_kr + MLA_ROPE]
    kr_blk = jnp.concatenate([zeros(MLA_NOPE), kr, zeros(HEAD_PAD - MLA_QK)], axis=1)
    krs_blk = jnp.concatenate([zeros(MLA_NOPE), -kr[:, half:], kr[:, :half], zeros(HEAD_PAD - MLA_QK)], axis=1)
    dt_blk = jnp.concatenate([w_in[:, o_dt:o_dt + SSD_HEADS], zeros(LANES - SSD_HEADS)], axis=1)
    w_in_r = jnp.concatenate([w_in[:, o_z:o_dt], w_in[:, o_cq:o_kr], kr_blk, krs_blk, dt_blk], axis=1)
    r = w_uq.shape[0]
    zq = jnp.zeros((r, MLA_HEADS, HEAD_PAD - MLA_QK), F32)
    zn = jnp.zeros((r, MLA_HEADS, MLA_NOPE), F32)
    wq_plain = jnp.concatenate([w_uq, zq], axis=2).reshape(r, MLA_HEADS * HEAD_PAD)
    wq_rot = jnp.concatenate([zn, -w_uq[:, :, MLA_NOPE + half:], w_uq[:, :, MLA_NOPE:MLA_NOPE + half], zq],
                             axis=2).reshape(r, MLA_HEADS * HEAD_PAD)
    wq2 = jnp.concatenate([wq_plain, wq_rot], axis=1)
    rk = w_ukv.shape[0]
    zk = jnp.zeros((rk, MLA_HEADS, HEAD_PAD - MLA_NOPE), F32)
    wk = jnp.concatenate([w_ukv[:, :, :MLA_NOPE], zk], axis=2).reshape(rk, MLA_HEADS * HEAD_PAD)
    zv = jnp.zeros((rk, MLA_HEADS, MLA_V_ROWS - MLA_V), F32)
    wv_t = jnp.concatenate([w_ukv[:, :, MLA_NOPE:], zv], axis=2).reshape(rk, MLA_HEADS * MLA_V_ROWS).T
    return w_in_r.astype(BF16), wq2.astype(BF16), wk.astype(BF16), wv_t.astype(BF16)


def _inproj(x2, mod3, gpre, w_in_r, q_norm, wq2, kv_norm, wk, wv_t, cos_t, sin_t, seq, tm):
    t, d = x2.shape
    per_b = seq // tm
    row = lambda i: (i, 0)
    const = lambda i: (0, 0)
    nqk = MLA_HEADS * HEAD_PAD
    slabs = tm // ATT_BLOCK
    return pl.pallas_call(
        _inproj_kernel,
        grid=(t // tm,),
        in_specs=[pl.BlockSpec((tm, d), row),
                  pl.BlockSpec((1, N_MOD, d), lambda i: (i // per_b, 0, 0)),
                  pl.BlockSpec((1, d), const),
                  pl.BlockSpec(w_in_r.shape, const),
                  pl.BlockSpec((1, MLA_Q_RANK), const),
                  pl.BlockSpec(wq2.shape, const),
                  pl.BlockSpec((1, MLA_KV_RANK), const),
                  pl.BlockSpec(wk.shape, const),
                  pl.BlockSpec(wv_t.shape, const),
                  pl.BlockSpec((tm, LANES), row),
                  pl.BlockSpec((tm, LANES), row)],
        out_specs=[pl.BlockSpec((tm, SSD_INNER), row),
                   pl.BlockSpec((tm, SSD_XBC), row),
                   pl.BlockSpec((tm, LANES), row),
                   pl.BlockSpec((tm, nqk), row),
                   pl.BlockSpec((tm, nqk), row),
                   pl.BlockSpec((slabs, MLA_HEADS * MLA_V_ROWS, ATT_BLOCK), lambda i: (i, 0, 0))],
        out_shape=[jax.ShapeDtypeStruct((t, SSD_INNER), BF16),
                   jax.ShapeDtypeStruct((t, SSD_XBC), BF16),
                   jax.ShapeDtypeStruct((t, LANES), F32),
                   jax.ShapeDtypeStruct((t, nqk), BF16),
                   jax.ShapeDtypeStruct((t, nqk), BF16),
                   jax.ShapeDtypeStruct((t // ATT_BLOCK, MLA_HEADS * MLA_V_ROWS, ATT_BLOCK), BF16)],
        compiler_params=_params("arbitrary"),
        name="inproj",
    )(x2, mod3, gpre, w_in_r, q_norm, wq2, kv_norm, wk, wv_t, cos_t, sin_t)


def _split3_packed(v):
    lane = lax.broadcasted_iota(jnp.int32, v.shape, 1)
    v = jnp.where(lane < SSD_HEADS, v, 0.0)
    hi = v.astype(BF16).astype(F32)
    rest = v - hi
    mid = rest.astype(BF16).astype(F32)
    lo = rest - mid
    return (hi + pltpu.roll(mid, SSD_HEADS, 1) + pltpu.roll(lo, 2 * SSD_HEADS, 1)).astype(BF16)


def _ssd_kernel(xbc_ref, z_ref, dt_ref, cw_ref, cb_ref, dtb_ref, alog_ref, dsk_ref, gn_ref,
                e_ref, y_ref, tail_ref, state_ref):
    @pl.when(pl.program_id(1) == 0)
    def _():
        tail_ref[...] = jnp.zeros_like(tail_ref)
        state_ref[...] = jnp.zeros_like(state_ref)

    for c in range(xbc_ref.shape[0] // SSD_CHUNK):
        _ssd_chunk(pl.ds(c * SSD_CHUNK, SSD_CHUNK), xbc_ref, z_ref, dt_ref, cw_ref, cb_ref, dtb_ref,
                   alog_ref, dsk_ref, gn_ref, e_ref, y_ref, tail_ref, state_ref)


def _ssd_chunk(rows, xbc_ref, z_ref, dt_ref, cw_ref, cb_ref, dtb_ref, alog_ref, dsk_ref, gn_ref,
               e_ref, y_ref, tail_ref, state_ref):
    L = SSD_CHUNK
    gw = SSD_INNER // SSD_GROUPS

    cur_b = xbc_ref[rows, :]
    ext = jnp.concatenate([tail_ref[...], cur_b], axis=0)
    out_row = lax.broadcasted_iota(jnp.int32, ((SSD_CONV - 1) * L, ext.shape[0]), 0)
    src_row = lax.broadcasted_iota(jnp.int32, ((SSD_CONV - 1) * L, ext.shape[0]), 1)
    tail_rows = ext.shape[0] - L
    shift = jnp.where(src_row + (out_row // L + 1) == lax.rem(out_row, L) + tail_rows, 1.0, 0.0)
    shifted = jnp.dot(shift.astype(BF16), ext, preferred_element_type=F32)
    acc = cur_b.astype(F32) * cw_ref[SSD_CONV - 1:SSD_CONV, :] + cb_ref[...]
    for j in range(1, SSD_CONV):
        acc = acc + shifted[(j - 1) * L:j * L] * cw_ref[SSD_CONV - 1 - j:SSD_CONV - j, :]
    tail_ref[...] = cur_b[L - tail_rows:L]
    act = _silu(acc)
    xs = act[:, :SSD_INNER]
    bm = act[:, SSD_INNER:SSD_INNER + SSD_GROUPS * SSD_STATE].astype(BF16)
    cm = act[:, SSD_INNER + SSD_GROUPS * SSD_STATE:].astype(BF16)

    dt_in = dt_ref[rows, :] + dtb_ref[...]
    dt = jnp.maximum(dt_in, 0.0) + jnp.log(1.0 + jnp.exp(-jnp.abs(dt_in)))
    adt = dt * (-jnp.exp(alog_ref[...]))
    ri = lax.broadcasted_iota(jnp.int32, (L, L), 0)
    ci = lax.broadcasted_iota(jnp.int32, (L, L), 1)
    causal = ci <= ri
    acs_p = jnp.dot(causal.astype(BF16), _split3_packed(adt), preferred_element_type=F32)
    a_cs = (acs_p + pltpu.roll(acs_p, LANES - SSD_HEADS, 1)
            + pltpu.roll(acs_p, LANES - 2 * SSD_HEADS, 1))
    a_cs_t = a_cs.T
    cs_all = jnp.dot(_split3_packed(a_cs), e_ref[...], preferred_element_type=F32)
    cs64 = cs_all[:, :SSD_INNER]
    cs128 = cs_all[:, SSD_INNER:]
    dt64 = jnp.dot(_split3_packed(dt), e_ref[:, :SSD_INNER], preferred_element_type=F32)

    xd = xs * dt64
    xd_b = xd.astype(BF16)
    last = cs64[L - 1:L, :]
    xdw = (xd * jnp.exp(last - cs64)).astype(BF16)
    chunk_decay = jnp.exp(last)
    in_decay = jnp.exp(cs64)

    lane = lax.broadcasted_iota(jnp.int32, (L, LANES), 1)
    lo = lane < SSD_HEAD_DIM
    zero_b = jnp.zeros((L, LANES), BF16)
    y_parts = []
    new_states = []
    for g in range(SSD_GROUPS):
        bg = bm[:, g * SSD_STATE:(g + 1) * SSD_STATE]
        cg = cm[:, g * SSD_STATE:(g + 1) * SSD_STATE]
        cb = lax.dot_general(cg, bg, (((1,), (1,)), ((), ())), preferred_element_type=F32)
        prev = state_ref[:, g * gw:(g + 1) * gw]
        y_off = jnp.dot(cg, prev.astype(BF16), preferred_element_type=F32) * in_decay[:, g * gw:(g + 1) * gw]
        st = lax.dot_general(bg, xdw[:, g * gw:(g + 1) * gw], (((0,), (0,)), ((), ())),
                             preferred_element_type=F32)
        new_states.append(prev * chunk_decay[:, g * gw:(g + 1) * gw] + st)
        heads_per_group = SSD_HEADS // SSD_GROUPS
        for pair in range(heads_per_group // 2):
            h0 = g * heads_per_group + 2 * pair
            blk = xd_b[:, h0 * SSD_HEAD_DIM:(h0 + 2) * SSD_HEAD_DIM]
            y_pair = None
            for k in range(2):
                hh = h0 + k
                diff = cs128[:, hh * LANES:(hh + 1) * LANES] - a_cs_t[hh:hh + 1, :]
                decay = jnp.where(causal, jnp.exp(diff), 0.0)
                m = (cb * decay).astype(BF16)
                rhs = jnp.where(lo, blk, zero_b) if k == 0 else jnp.where(lo, zero_b, blk)
                part = jnp.dot(m, rhs, preferred_element_type=F32)
                y_pair = part if y_pair is None else y_pair + part
            c0 = (2 * pair) * SSD_HEAD_DIM
            y_parts.append(y_pair + y_off[:, c0:c0 + LANES])
    state_ref[...] = jnp.concatenate(new_states, axis=1)
    y = jnp.concatenate(y_parts, axis=1) + dsk_ref[...] * xs
    gated = y * _silu(z_ref[rows, :].astype(F32))
    y_ref[rows, :] = _rms(gated, gn_ref[...]).astype(BF16)


def _ssd(xbc, z, dt_raw, conv_w, conv_b, dt_bias, a_log, d_skip, ssd_norm, batch, seq):
    t = xbc.shape[0]
    L = SSD_STEP_ROWS
    nc = seq // L
    row = lambda b, c: (b * nc + c, 0)
    const = lambda b, c: (0, 0)
    pad = lambda v: jnp.concatenate([v.reshape(1, -1), jnp.zeros((1, LANES - v.size), F32)], axis=1)
    head_of_lane = jnp.concatenate([jnp.arange(SSD_INNER) // SSD_HEAD_DIM,
                                    jnp.arange(SSD_HEADS * LANES) // LANES])
    rows = jnp.arange(LANES)[:, None]
    expand = ((rows < 3 * SSD_HEADS) & (rows % SSD_HEADS == head_of_lane[None, :])).astype(BF16)
    dsk = jnp.repeat(d_skip.astype(F32), SSD_HEAD_DIM).reshape(1, SSD_INNER)
    return pl.pallas_call(
        _ssd_kernel,
        grid=(batch, nc),
        in_specs=[pl.BlockSpec((L, SSD_XBC), row),
                  pl.BlockSpec((L, SSD_INNER), row),
                  pl.BlockSpec((L, LANES), row),
                  pl.BlockSpec((SSD_CONV, SSD_XBC), const),
                  pl.BlockSpec((1, SSD_XBC), const),
                  pl.BlockSpec((1, LANES), const),
                  pl.BlockSpec((1, LANES), const),
                  pl.BlockSpec((1, SSD_INNER), const),
                  pl.BlockSpec((1, SSD_INNER), const),
                  pl.BlockSpec(expand.shape, const)],
        out_specs=pl.BlockSpec((L, SSD_INNER), row),
        out_shape=jax.ShapeDtypeStruct((t, SSD_INNER), BF16),
        scratch_shapes=[pltpu.VMEM((16, SSD_XBC), BF16),
                        pltpu.VMEM((SSD_STATE, SSD_INNER), F32)],
        compiler_params=_params("arbitrary", "arbitrary"),
        name="ssd",
    )(xbc, z, dt_raw, conv_w, conv_b.reshape(1, -1), pad(dt_bias), pad(a_log), dsk,
      ssd_norm.reshape(1, -1), expand)


def _attn_kernel(q_ref, k_ref, vt_ref, gn_ref, o_ref, s_ref):
    blk = ATT_BLOCK
    qi = pl.program_id(1)
    causal = (lax.broadcasted_iota(jnp.int32, (blk, blk), 0)
              <= lax.broadcasted_iota(jnp.int32, (blk, blk), 1))
    dims = (((1,), (1,)), ((), ()))
    qs = [q_ref[:, hd * HEAD_PAD:(hd + 1) * HEAD_PAD] for hd in range(MLA_HEADS)]

    def step(j, carry, nblk, masked):
        keys = nblk * blk
        rows = pl.ds(pl.multiple_of(j * blk, blk), keys)
        m_new = []
        for hd in range(MLA_HEADS):
            kk = k_ref[rows, hd * HEAD_PAD:(hd + 1) * HEAD_PAD]
            s = lax.dot_general(kk, qs[hd], dims, preferred_element_type=F32)
            if masked:
                s = jnp.where(causal, s, -jnp.inf)
            s_ref[hd, 0:keys, :] = s
            m_new.append(jnp.maximum(carry[hd][0], jnp.max(s, axis=0, keepdims=True)))
        new = []
        for hd in range(MLA_HEADS):
            m_old, acc = carry[hd]
            p = jnp.exp2(s_ref[hd, 0:keys, :] - m_new[hd]).astype(BF16)
            acc = acc * jnp.exp2(m_old - m_new[hd])
            for b in range(nblk):
                vt = vt_ref[j + b, hd * MLA_V_ROWS:(hd + 1) * MLA_V_ROWS, :]
                acc = acc + jnp.dot(vt, p[b * blk:(b + 1) * blk], preferred_element_type=F32)
            new.append((m_new[hd], acc))
        return tuple(new)

    neg = jnp.full((1, blk), -jnp.inf, F32)
    carry = tuple((neg, jnp.zeros((MLA_V_ROWS, blk), F32)) for _ in range(MLA_HEADS))
    carry = lax.fori_loop(0, qi // 2, lambda u, c: step(2 * u, c, 2, False), carry)
    carry = lax.fori_loop(0, lax.rem(qi, 2), lambda _, c: step(qi - 1, c, 1, False), carry)
    final = step(qi, carry, 1, True)
    y_t = jnp.concatenate([acc[:MLA_V] / acc[MLA_V:MLA_V + 1] for _, acc in final], axis=0)
    o_ref[...] = _rms(y_t.T, gn_ref[...]).astype(BF16)


def _attention(q, k, v_t, gn, batch, seq):
    t = q.shape[0]
    blk = ATT_BLOCK
    nq = seq // blk
    nqk = MLA_HEADS * HEAD_PAD
    return pl.pallas_call(
        _attn_kernel,
        grid=(batch, nq),
        in_specs=[pl.BlockSpec((blk, nqk), lambda b, i: (b * nq + i, 0)),
                  pl.BlockSpec((seq, nqk), lambda b, i: (b, 0)),
                  pl.BlockSpec((nq, MLA_HEADS * MLA_V_ROWS, blk), lambda b, i: (b, 0, 0)),
                  pl.BlockSpec((1, MLA_INNER), lambda b, i: (0, 0))],
        out_specs=pl.BlockSpec((blk, MLA_INNER), lambda b, i: (b * nq + i, 0)),
        out_shape=jax.ShapeDtypeStruct((t, MLA_INNER), BF16),
        scratch_shapes=[pltpu.VMEM((MLA_HEADS, 2 * blk, blk), F32)],
        compiler_params=_params("arbitrary", "arbitrary"),
        name="attention",
    )(q, k, v_t, gn)


def _outproj_kernel(ys_ref, ym_ref, x_ref, mod_ref, wtop_ref, wbot_ref, gpm_ref, gpf_ref,
                    wrh_ref, wrl_ref, br_ref, x1_ref, h2_ref, col_ref, info_ref, cnt_ref, ce_ref):
    y = (jnp.dot(ys_ref[...], wtop_ref[...], preferred_element_type=F32)
         + jnp.dot(ym_ref[...], wbot_ref[...], preferred_element_type=F32))
    g1 = mod_ref[0, 2:3, :]
    sh2 = mod_ref[0, 3:4, :]
    sc2 = mod_ref[0, 4:5, :]
    x1 = x_ref[...] + g1 * _rms(y, gpm_ref[...])
    x1_ref[...] = x1
    h2 = _rms(x1, gpf_ref[...]) * (1.0 + sc2) + sh2
    tm = h2.shape[0]
    h_hi = h2.astype(BF16)
    h_lo = (h2 - h_hi.astype(F32)).astype(BF16)
    logits = (jnp.dot(h_hi, wrh_ref[...], preferred_element_type=F32)
              + jnp.dot(h_lo, wrh_ref[...], preferred_element_type=F32)
              + jnp.dot(h_hi, wrl_ref[...], preferred_element_type=F32)) + br_ref[...]
    lt = logits.T
    row = lax.broadcasted_iota(jnp.int32, lt.shape, 0).astype(F32)
    ninf = -jnp.inf
    big = 1e9
    is_g = row < MOE_GROUPS
    gl = jnp.where(is_g, lt, ninf)
    gmax = jnp.max(gl, axis=0, keepdims=True)
    gsum = jnp.sum(jnp.where(is_g, jnp.exp(lt - gmax), 0.0), axis=0, keepdims=True)
    g_w = 1.0 / gsum
    g_idx = jnp.min(jnp.where(gl == gmax, row, big), axis=0, keepdims=True)
    first = MOE_GROUPS + MOE_EXPERTS_PER_GROUP * g_idx
    el = jnp.where(row >= first, jnp.where(row < first + MOE_EXPERTS_PER_GROUP, lt, ninf), ninf)
    m1 = jnp.max(el, axis=0, keepdims=True)
    i1 = jnp.min(jnp.where(el == m1, row, big), axis=0, keepdims=True)
    el2 = jnp.where(row == i1, ninf, el)
    m2 = jnp.max(el2, axis=0, keepdims=True)
    i2 = jnp.min(jnp.where(el2 == m2, row, big), axis=0, keepdims=True)
    r = jnp.exp(m2 - m1)
    w1 = g_w / (1.0 + r)
    w2 = g_w * r / (1.0 + r)
    sel1 = row == i1
    sel2 = row == i2
    both = jnp.where(sel1, 1.0, jnp.where(sel2, 1.0, 0.0))
    cnt16 = jnp.floor((jnp.sum(both, axis=1, keepdims=True) + (MOE_CHUNK - 1)) * (1.0 / MOE_CHUNK))
    er = lax.broadcasted_iota(jnp.int32, (LANES, LANES), 0)
    ec = lax.broadcasted_iota(jnp.int32, (LANES, LANES), 1)
    cnt16_b = jnp.broadcast_to(cnt16, (LANES, LANES))
    off16 = jnp.dot((ec < er).astype(BF16), cnt16_b.astype(BF16), preferred_element_type=F32)[:, 0:1]
    earlier = (lax.broadcasted_iota(jnp.int32, (tm, tm), 0)
               < lax.broadcasted_iota(jnp.int32, (tm, tm), 1)).astype(BF16)
    base = off16 * MOE_CHUNK + jnp.dot(both.astype(BF16), earlier, preferred_element_type=F32)
    lpos1 = jnp.sum(jnp.where(sel1, base, 0.0), axis=0, keepdims=True)
    lpos2 = jnp.sum(jnp.where(sel2, base, 0.0), axis=0, keepdims=True)
    h2_ref[...] = h_hi
    col_ref[...] = jnp.where(row == 0, lpos1, jnp.where(row == 1, lpos2, 0.0)).T
    row8 = lax.broadcasted_iota(jnp.int32, info_ref.shape, 0)
    info_ref[...] = jnp.where(row8 == 0, lpos1, jnp.where(row8 == 1, lpos2,
                              jnp.where(row8 == 2, w1, jnp.where(row8 == 3, w2, 0.0))))
    cnt_ref[...] = cnt16_b
    chunk = lax.broadcasted_iota(jnp.int32, (LANES, LANES), 1).astype(F32)
    ce = jnp.sum(jnp.where(off16 + cnt16 <= chunk, 1.0, 0.0), axis=0, keepdims=True) - MOE_GROUPS
    ce_ref[...] = jnp.broadcast_to(ce, ce_ref.shape)


def _outproj(y_ssd, y_mla, x2, mod3, w_top, w_bot, gpm, gpf, wr_hi, wr_lo, b_router, seq, tm):
    t, d = x2.shape
    per_b = seq // tm
    n_tok_tiles = t // tm
    row = lambda i: (i, 0)
    const = lambda i: (0, 0)
    return pl.pallas_call(
        _outproj_kernel,
        grid=(n_tok_tiles,),
        in_specs=[pl.BlockSpec((tm, SSD_INNER), row),
                  pl.BlockSpec((tm, MLA_INNER), row),
                  pl.BlockSpec((tm, d), row),
                  pl.BlockSpec((1, N_MOD, d), lambda i: (i // per_b, 0, 0)),
                  pl.BlockSpec(w_top.shape, const),
                  pl.BlockSpec(w_bot.shape, const),
                  pl.BlockSpec((1, d), const),
                  pl.BlockSpec((1, d), const),
                  pl.BlockSpec(wr_hi.shape, const),
                  pl.BlockSpec(wr_lo.shape, const),
                  pl.BlockSpec((1, LANES), const)],
        out_specs=[pl.BlockSpec((tm, d), row),
                   pl.BlockSpec((tm, d), row),
                   pl.BlockSpec((tm, LANES), row),
                   pl.BlockSpec((8, tm), row),
                   pl.BlockSpec((LANES, LANES), row),
                   pl.BlockSpec((8, LANES), row)],
        out_shape=[jax.ShapeDtypeStruct((t, d), F32),
                   jax.ShapeDtypeStruct((t, d), BF16),
                   jax.ShapeDtypeStruct((t, LANES), F32),
                   jax.ShapeDtypeStruct((n_tok_tiles * 8, tm), F32),
                   jax.ShapeDtypeStruct((n_tok_tiles * LANES, LANES), F32),
                   jax.ShapeDtypeStruct((n_tok_tiles * 8, LANES), F32)],
        compiler_params=_params("arbitrary"),
        name="outproj_router",
    )(y_ssd, y_mla, x2, mod3, w_top, w_bot, gpm, gpf, wr_hi, wr_lo, b_router)


def _moe_rows(n_tok_tiles):
    rows = n_tok_tiles * MOE_SLOTS + MOE_EXPERTS * (MOE_TILE - MOE_CHUNK)
    return (rows + MOE_TILE - 1) // MOE_TILE * MOE_TILE


def _moe_plan(cnt_tiles, chunk_expert, n_tok_tiles, n_rows):
    chunks_per_tile = MOE_SLOTS // MOE_CHUNK
    chunks_per_rows = MOE_TILE // MOE_CHUNK
    cnt = cnt_tiles.reshape(n_tok_tiles, LANES, LANES)[:, MOE_GROUPS:MOE_GROUPS + MOE_EXPERTS, 0]
    cnt = cnt.astype(jnp.int32)
    ce = chunk_expert.reshape(n_tok_tiles, 8, LANES)[:, 0, :chunks_per_tile].astype(jnp.int32)
    used = jnp.sum(cnt, axis=1)
    total = jnp.sum(cnt, axis=0)
    padded = (total + chunks_per_rows - 1) // chunks_per_rows * chunks_per_rows
    e_end = jnp.cumsum(padded)
    e_start = e_end - padded
    run_global = e_start[None, :] + jnp.cumsum(cnt, axis=0) - cnt
    run_local = jnp.cumsum(cnt, axis=1) - cnt
    experts = jnp.arange(MOE_EXPERTS, dtype=jnp.int32)
    shift = jnp.sum(jnp.where(ce[:, :, None] == experts, (run_global - run_local)[:, None, :], 0), axis=-1)
    dst = shift + jnp.arange(chunks_per_tile, dtype=jnp.int32)[None, :]
    n_row_tiles = n_rows // MOE_TILE
    n_used = e_end[-1] // chunks_per_rows
    j = jnp.minimum(jnp.arange(n_row_tiles, dtype=jnp.int32), jnp.maximum(n_used - 1, 0))
    tile_expert = jnp.sum((e_end[None, :] // chunks_per_rows <= j[:, None]).astype(jnp.int32), axis=1)
    tile_expert = jnp.minimum(tile_expert, MOE_EXPERTS - 1)
    i32 = lambda v: v.astype(jnp.int32)
    return (i32(dst.reshape(-1)), i32(used), i32(e_start + total), i32(padded - total),
            i32(n_used.reshape(1)), i32(tile_expert))


def _chunk_rows(chunk):
    return pl.ds(pl.multiple_of(chunk * MOE_CHUNK, MOE_CHUNK), MOE_CHUNK)


def _run_copies(*groups):
    for wait in (False, True):
        for lo, hi, make_copy in groups:
            def body(k, carry, wait=wait, make_copy=make_copy):
                cp = make_copy(k)
                cp.wait() if wait else cp.start()
                return carry
            lax.fori_loop(lo, hi, body, 0)


def _scatter_chunks_kernel(dst_ref, used_ref, pad_start_ref, pad_n_ref, nused_ref, h2_ref, info_ref,
                           o_ref, wslot_ref, src_ref, zero_ref, sem):
    i = pl.program_id(0)
    chunks = MOE_SLOTS // MOE_CHUNK
    slot = lax.rem(i, 2)

    info = info_ref[...]
    srow = lax.broadcasted_iota(jnp.int32, (MOE_SLOTS, info.shape[1]), 0)
    hit1 = srow == info[0:1, :].astype(jnp.int32)
    hit2 = srow == info[1:2, :].astype(jnp.int32)
    perm = jnp.where(hit1, 1.0, jnp.where(hit2, 1.0, 0.0)).astype(BF16)
    src_ref[slot] = jnp.dot(perm, h2_ref[...], preferred_element_type=F32).astype(BF16)
    wslot = jnp.sum(jnp.where(hit1, info[2:3, :], jnp.where(hit2, info[3:4, :], 0.0)), axis=1, keepdims=True)
    wslot_ref[...] = jnp.broadcast_to(wslot, wslot_ref.shape)

    @pl.when(i == 0)
    def _():
        zero_ref[...] = jnp.zeros_like(zero_ref)
        pads = [(0, pad_n_ref[e], lambda c, e=e: pltpu.make_async_copy(
            zero_ref.at[pl.ds(0, MOE_CHUNK), :], o_ref.at[_chunk_rows(pad_start_ref[e] + c), :], sem.at[2]))
            for e in range(MOE_EXPERTS)]
        tail = (nused_ref[0], o_ref.shape[0] // MOE_TILE, lambda j: pltpu.make_async_copy(
            zero_ref, o_ref.at[pl.ds(pl.multiple_of(j * MOE_TILE, MOE_TILE), MOE_TILE), :], sem.at[2]))
        _run_copies(tail, *pads)

    def chunk_copies(tile, buf, wait):
        def body(c, carry):
            cp = pltpu.make_async_copy(src_ref.at[buf, _chunk_rows(c), :],
                                       o_ref.at[_chunk_rows(dst_ref[tile * chunks + c]), :], sem.at[buf])
            cp.wait() if wait else cp.start()
            return carry
        lax.fori_loop(0, used_ref[tile], body, 0)

    chunk_copies(i, slot, False)

    @pl.when(i > 0)
    def _():
        chunk_copies(i - 1, 1 - slot, True)

    @pl.when(i == pl.num_programs(0) - 1)
    def _():
        chunk_copies(i, slot, True)


def _scatter_chunks(h2, info, dst, used, pad_start, pad_n, n_used, n_rows):
    t, d = h2.shape
    tm = info.shape[1]
    row = lambda i, *_: (i, 0)
    return pl.pallas_call(
        _scatter_chunks_kernel,
        grid_spec=pltpu.PrefetchScalarGridSpec(
            num_scalar_prefetch=5,
            grid=(used.shape[0],),
            in_specs=[pl.BlockSpec((tm, d), row), pl.BlockSpec((8, tm), row)],
            out_specs=[pl.BlockSpec(memory_space=pl.ANY), pl.BlockSpec((MOE_SLOTS, LANES), row)],
            scratch_shapes=[pltpu.VMEM((2, MOE_SLOTS, d), BF16), pltpu.VMEM((MOE_TILE, d), BF16),
                            pltpu.SemaphoreType.DMA((3,))]),
        out_shape=[jax.ShapeDtypeStruct((n_rows, d), BF16),
                   jax.ShapeDtypeStruct((used.shape[0] * MOE_SLOTS, LANES), F32)],
        compiler_params=_params("arbitrary"),
        name="moe_scatter_chunks",
    )(dst, used, pad_start, pad_n, n_used, h2, info)


def _experts_kernel(te_ref, nt_ref, xs_ref, wg_ref, wu_ref, wd_ref, o_ref):
    used = pl.program_id(0) < nt_ref[0]

    @pl.when(used)
    def _():
        x = xs_ref[...]
        gate = jnp.dot(x, wg_ref[0].astype(BF16), preferred_element_type=F32)
        up = jnp.dot(x, wu_ref[0].astype(BF16), preferred_element_type=F32)
        hid = (_silu(gate) * up).astype(BF16)
        o_ref[...] = jnp.dot(hid, wd_ref[0].astype(BF16), preferred_element_type=F32).astype(BF16)

    @pl.when(jnp.logical_not(used))
    def _():
        o_ref[...] = jnp.zeros_like(o_ref)


def _experts(xs, tile_expert, n_used, w_gate, w_up, w_down):
    n_rows, d = xs.shape
    tile = MOE_TILE
    by_expert = lambda j, te, nt: (te[j], 0, 0)
    return pl.pallas_call(
        _experts_kernel,
        grid_spec=pltpu.PrefetchScalarGridSpec(
            num_scalar_prefetch=2,
            grid=(n_rows // tile,),
            in_specs=[pl.BlockSpec((tile, d), lambda j, te, nt: (jnp.maximum(jnp.minimum(j, nt[0] - 1), 0), 0)),
                      pl.BlockSpec((1, d, MOE_FF), by_expert),
                      pl.BlockSpec((1, d, MOE_FF), by_expert),
                      pl.BlockSpec((1, MOE_FF, d), by_expert)],
            out_specs=pl.BlockSpec((tile, d), lambda j, te, nt: (j, 0))),
        out_shape=jax.ShapeDtypeStruct((n_rows, d), BF16),
        compiler_params=_params("arbitrary"),
        name="moe_experts",
    )(tile_expert, n_used, xs, w_gate, w_up, w_down)


def _combine_kernel(dst_ref, used_ref, wslot_ref, col_ref, x1_ref, mod_ref, gpost_ref, ys_ref, o_ref,
                    ysl_ref, sem):
    i = pl.program_id(0)
    chunks = MOE_SLOTS // MOE_CHUNK
    slot = lax.rem(i, 2)

    def fetch(tile, buf, wait):
        base = tile * chunks
        used = used_ref[tile]

        def copy(c, carry):
            cp = pltpu.make_async_copy(ys_ref.at[_chunk_rows(dst_ref[base + c]), :],
                                       ysl_ref.at[buf, _chunk_rows(c), :], sem.at[buf])
            cp.wait() if wait else cp.start()
            return carry

        def clear(c, carry):
            ysl_ref[buf, _chunk_rows(c), :] = jnp.zeros((MOE_CHUNK, ysl_ref.shape[2]), BF16)
            return carry

        lax.fori_loop(0, used, copy, 0)
        if not wait:
            lax.fori_loop(used, chunks, clear, 0)

    @pl.when(i == 0)
    def _():
        fetch(0, 0, False)

    @pl.when(i + 1 < pl.num_programs(0))
    def _():
        fetch(i + 1, 1 - slot, False)

    col = col_ref[...]
    tm = col.shape[0]
    lane = lax.broadcasted_iota(jnp.int32, (tm, MOE_SLOTS), 1)
    hit = jnp.where(lane == col[:, 0:1].astype(jnp.int32), 1.0,
                    jnp.where(lane == col[:, 1:2].astype(jnp.int32), 1.0, 0.0)).astype(BF16)
    fetch(i, slot, True)
    ysw = (ysl_ref[slot].astype(F32) * wslot_ref[:, 0:1]).astype(BF16)
    y = jnp.dot(hit, ysw, preferred_element_type=F32)
    g2 = mod_ref[0, 5:6, :]
    o_ref[...] = x1_ref[...] + g2 * _rms(y, gpost_ref[...])


def _combine(ys, dst, used, wslot, col, x1, mod3, gpost, seq, tm):
    t, d = x1.shape
    per_b = seq // tm
    row = lambda i, *_: (i, 0)
    return pl.pallas_call(
        _combine_kernel,
        grid_spec=pltpu.PrefetchScalarGridSpec(
            num_scalar_prefetch=2,
            grid=(t // tm,),
            in_specs=[pl.BlockSpec((MOE_SLOTS, LANES), row),
                      pl.BlockSpec((tm, LANES), row),
                      pl.BlockSpec((tm, d), row),
                      pl.BlockSpec((1, N_MOD, d), lambda i, *_: (i // per_b, 0, 0)),
                      pl.BlockSpec((1, d), lambda i, *_: (0, 0)),
                      pl.BlockSpec(memory_space=pl.ANY)],
            out_specs=pl.BlockSpec((tm, d), row),
            scratch_shapes=[pltpu.VMEM((2, MOE_SLOTS, d), BF16), pltpu.SemaphoreType.DMA((2,))]),
        out_shape=jax.ShapeDtypeStruct((t, d), F32),
        compiler_params=_params("arbitrary"),
        name="moe_combine",
    )(dst, used, wslot, col, x1, mod3, gpost, ys)


def kernel(x, c, positions, ada_w, ada_b, pre_norm_mix, post_norm_mix, pre_norm_ffn, post_norm_ffn, w_in, conv_w, conv_b, dt_bias, a_log, d_skip, ssd_norm, q_norm, w_uq, kv_norm, w_ukv, mla_out_norm, w_out, w_group_router, b_group_router, w_expert_router, b_expert_router, w_gate, w_up, w_down):
    batch, seq, d = x.shape
    t = batch * seq
    depth = ada_w.shape[0]
    tm = min(INPROJ_TOKENS, seq)
    cos_t, sin_t = _rope_tables(positions)

    x2 = x.reshape(t, d)
    for l in range(depth):
        mod3 = _modulation(c, ada_w[l], ada_b[l]).reshape(batch, N_MOD, d)
        w_in_r, wq2, wk, wv_t = _inproj_weights(w_in[l], w_uq[l], w_ukv[l])
        z, xbc, dt_raw, q, k, v_t = _inproj(
            x2, mod3, pre_norm_mix[l].reshape(1, d), w_in_r, q_norm[l].reshape(1, -1), wq2,
            kv_norm[l].reshape(1, -1), wk, wv_t, cos_t, sin_t, seq, tm)
        y_ssd = _ssd(xbc, z, dt_raw, conv_w[l], conv_b[l], dt_bias[l], a_log[l], d_skip[l],
                     ssd_norm[l], batch, seq)
        y_mla = _attention(q, k, v_t, mla_out_norm[l].reshape(1, -1), batch, seq)
        w_o = w_out[l].astype(BF16)
        pad_r = LANES - MOE_GROUPS - MOE_EXPERTS
        w_router = jnp.concatenate([w_group_router[l], w_expert_router[l], jnp.zeros((d, pad_r), F32)], axis=1)
        b_router = jnp.concatenate([b_group_router[l].reshape(-1), b_expert_router[l].reshape(-1),
                                    jnp.zeros((pad_r,), F32)]).reshape(1, LANES)
        wr_hi = w_router.astype(BF16)
        wr_lo = (w_router - wr_hi.astype(F32)).astype(BF16)
        x1, h2, col, info, cnt_tiles, chunk_expert = _outproj(
            y_ssd, y_mla, x2, mod3, w_o[:SSD_INNER], w_o[SSD_INNER:], post_norm_mix[l].reshape(1, d),
            pre_norm_ffn[l].reshape(1, d), wr_hi, wr_lo, b_router, seq, MOE_TOKENS)
        n_tok_tiles = t // MOE_TOKENS
        n_rows = _moe_rows(n_tok_tiles)
        dst, used, pad_start, pad_n, n_used, tile_expert = _moe_plan(cnt_tiles, chunk_expert, n_tok_tiles, n_rows)
        xs, wslot = _scatter_chunks(h2, info, dst, used, pad_start, pad_n, n_used, n_rows)
        ys = _experts(xs, tile_expert, n_used, w_gate[l], w_up[l], w_down[l])
        x2 = _combine(ys, dst, used, wslot, col, x1, mod3, post_norm_ffn[l].reshape(1, d), seq, MOE_TOKENS)
    return x2.reshape(batch, seq, d)
```

```python
import functools
import math

import jax
import jax.numpy as jnp
from jax import lax
from jax.experimental import pallas as pl
from jax.experimental.pallas import tpu as pltpu

F32 = jnp.float32
BF16 = jnp.bfloat16

D_MODEL = 1024
SSD_HEADS = 8
SSD_HEAD_DIM = 64
SSD_INNER = SSD_HEADS * SSD_HEAD_DIM
SSD_GROUPS = 2
SSD_STATE = 128
SSD_CONV = 4
SSD_CHUNK = 128
SSD_XBC = SSD_INNER + 2 * SSD_GROUPS * SSD_STATE
MLA_HEADS = 8
MLA_NOPE = 64
MLA_ROPE = 32
MLA_QK = MLA_NOPE + MLA_ROPE
MLA_V = 64
MLA_Q_RANK = 256
MLA_KV_RANK = 128
MLA_INNER = MLA_HEADS * MLA_V
ROPE_THETA = 10000.0
MOE_GROUPS = 4
MOE_EXPERTS_PER_GROUP = 8
MOE_EXPERTS = MOE_GROUPS * MOE_EXPERTS_PER_GROUP
MOE_FF = 256
N_MOD = 6
EPS = 1e-6
LOG2_E = math.log2(math.e)

LANES = 128
HEAD_PAD = 128
ATT_BLOCK = 256
MLA_V_ROWS = MLA_V + 16
INPROJ_TOKENS = 512
SSD_STEP_ROWS = 4 * SSD_CHUNK
MOE_TOKENS = 512
MOE_CHUNK = 16
MOE_TILE = 512
MOE_SLOTS = 2 * MOE_TOKENS + MOE_EXPERTS * MOE_CHUNK
VMEM_LIMIT = 48 * 1024 * 1024

_C_Z = 0
_C_XBC = _C_Z + SSD_INNER
_C_CQ = _C_XBC + SSD_XBC
_C_CKV = _C_CQ + MLA_Q_RANK
_C_KR = _C_CKV + MLA_KV_RANK
_C_KRS = _C_KR + LANES
_C_DT = _C_KRS + LANES
_C_END = _C_DT + LANES


def _silu(v):
    return v * (1.0 / (1.0 + jnp.exp(-v)))


def _rms(v, gain):
    return v * lax.rsqrt(jnp.mean(v * v, axis=-1, keepdims=True) + EPS) * gain


def _params(*sem, flags=None):
    return pltpu.CompilerParams(dimension_semantics=sem, vmem_limit_bytes=VMEM_LIMIT, flags=flags)


def _mod_kernel(c_ref, w_ref, b_ref, o_ref):
    act = _silu(c_ref[...])
    a_hi = act.astype(BF16)
    a_lo = (act - a_hi.astype(F32)).astype(BF16)
    w = w_ref[...]
    w_hi = w.astype(BF16)
    w_lo = (w - w_hi.astype(F32)).astype(BF16)
    o_ref[...] = (jnp.dot(a_hi, w_hi, preferred_element_type=F32)
                  + jnp.dot(a_lo, w_hi, preferred_element_type=F32)
                  + jnp.dot(a_hi, w_lo, preferred_element_type=F32)) + b_ref[...]


def _modulation(c, ada_w, ada_b):
    b, d = c.shape
    n = ada_w.shape[1]
    return pl.pallas_call(
        _mod_kernel,
        grid=(n // d,),
        in_specs=[pl.BlockSpec((b, d), lambda j: (0, 0)),
                  pl.BlockSpec((d, d), lambda j: (0, j)),
                  pl.BlockSpec((1, d), lambda j: (0, j))],
        out_specs=pl.BlockSpec((b, d), lambda j: (0, j)),
        out_shape=jax.ShapeDtypeStruct((b, n), F32),
        compiler_params=_params("arbitrary"),
        name="modulation",
    )(c, ada_w, ada_b.reshape(1, n))


def _rope_kernel(pos_ref, inv_ref, cos_ref, sin_ref):
    ang = pos_ref[...].astype(F32) * inv_ref[...]
    blk = ang.shape[0]
    half = MLA_ROPE // 2
    per_row = LANES // half
    lane = lax.broadcasted_iota(jnp.int32, (blk * per_row, LANES), 1)
    first = (lane >= MLA_NOPE) & (lane < MLA_NOPE + half)
    second = (lane >= MLA_NOPE + half) & (lane < MLA_QK)
    for fn, out_ref, fill in ((jnp.cos, cos_ref, 1.0), (jnp.sin, sin_ref, 0.0)):
        compact = fn(ang)
        rep = jnp.broadcast_to(compact[:, None, :], (blk, per_row, LANES)).reshape(blk * per_row, LANES)
        a = pltpu.roll(rep, 0, 1, stride=half, stride_axis=0)
        b = pltpu.roll(a, half, 1)
        out_ref[...] = jnp.where(first, a, jnp.where(second, b, fill))


def _rope_tables(positions):
    t = positions.size
    half = MLA_ROPE // 2
    per_row = LANES // half
    inv = 1.0 / (ROPE_THETA ** (jnp.arange(0, MLA_ROPE, 2, dtype=F32) / MLA_ROPE))
    group_token = (MLA_NOPE // half - jnp.arange(per_row)) % per_row
    pos_rep = jnp.repeat(positions.reshape(t // per_row, per_row)[:, group_token], half, axis=1)
    inv_t = jnp.tile(inv, per_row).reshape(1, LANES)
    rows = t // per_row
    blk = min(rows, 512)
    return pl.pallas_call(
        _rope_kernel,
        grid=(rows // blk,),
        in_specs=[pl.BlockSpec((blk, LANES), lambda i: (i, 0)),
                  pl.BlockSpec((1, LANES), lambda i: (0, 0))],
        out_specs=[pl.BlockSpec((blk * per_row, LANES), lambda i: (i, 0))] * 2,
        out_shape=[jax.ShapeDtypeStruct((t, LANES), F32)] * 2,
        compiler_params=_params("arbitrary"),
        name="rope_tables",
    )(pos_rep, inv_t)


def _inproj_kernel(x_ref, mod_ref, gpre_ref, win_ref, qn_ref, wq_ref, kvn_ref, wk_ref, wvt_ref,
                   cos_ref, sin_ref, z_ref, xbc_ref, dt_ref, q_ref, k_ref, vt_ref):
    x = x_ref[...]
    sh = mod_ref[0, 0:1, :]
    sc = mod_ref[0, 1:2, :]
    h = (_rms(x, gpre_ref[...]) * (1.0 + sc) + sh).astype(BF16)
    u = jnp.dot(h, win_ref[...], preferred_element_type=F32)
    z_ref[...] = u[:, _C_Z:_C_XBC].astype(BF16)
    xbc_ref[...] = u[:, _C_XBC:_C_CQ].astype(BF16)
    dt_ref[...] = u[:, _C_DT:_C_END]
    cos_t = cos_ref[...]
    sin_t = sin_ref[...]
    cq = _rms(u[:, _C_CQ:_C_CKV], qn_ref[...]).astype(BF16)
    q2 = jnp.dot(cq, wq_ref[...], preferred_element_type=F32)
    scale = MLA_QK ** -0.5 * LOG2_E
    nq = MLA_HEADS * HEAD_PAD
    for hd in range(MLA_HEADS):
        a = q2[:, hd * HEAD_PAD:(hd + 1) * HEAD_PAD]
        b = q2[:, nq + hd * HEAD_PAD:nq + (hd + 1) * HEAD_PAD]
        q_ref[:, hd * HEAD_PAD:(hd + 1) * HEAD_PAD] = ((a * cos_t + b * sin_t) * scale).astype(BF16)
    ckv = _rms(u[:, _C_CKV:_C_KR], kvn_ref[...]).astype(BF16)
    kn = jnp.dot(ckv, wk_ref[...], preferred_element_type=F32)
    k_pe = u[:, _C_KR:_C_KRS] * cos_t + u[:, _C_KRS:_C_DT] * sin_t
    for hd in range(MLA_HEADS):
        k_ref[:, hd * HEAD_PAD:(hd + 1) * HEAD_PAD] = (
            kn[:, hd * HEAD_PAD:(hd + 1) * HEAD_PAD] + k_pe).astype(BF16)
    v_t = lax.dot_general(wvt_ref[...], ckv, (((1,), (1,)), ((), ())), preferred_element_type=F32)
    head_row = lax.rem(lax.broadcasted_iota(jnp.int32, (v_t.shape[0], 1), 0), MLA_V_ROWS)
    v_t = v_t + jnp.where(head_row == MLA_V, 1.0, 0.0)
    for s in range(vt_ref.shape[0]):
        vt_ref[s] = v_t[:, s * ATT_BLOCK:(s + 1) * ATT_BLOCK].astype(BF16)


def _inproj_weights(w_in, w_uq, w_ukv):
    d = w_in.shape[0]
    half = MLA_ROPE // 2
    o_z, o_xbc, o_dt = 0, SSD_INNER, SSD_INNER + SSD_XBC
    o_cq = o_dt + SSD_HEADS
    o_ckv = o_cq + MLA_Q_RANK
    o_kr = o_ckv + MLA_KV_RANK
    zeros = lambda n: jnp.zeros((d, n), F32)
    kr = w_in[:, o_kr:o_kr + MLA_ROPE]
    kr_blk = jnp.concatenate([zeros(MLA_NOPE), kr, zeros(HEAD_PAD - MLA_QK)], axis=1)
    krs_blk = jnp.concatenate([zeros(MLA_NOPE), -kr[:, half:], kr[:, :half], zeros(HEAD_PAD - MLA_QK)], axis=1)
    dt_blk = jnp.concatenate([w_in[:, o_dt:o_dt + SSD_HEADS], zeros(LANES - SSD_HEADS)], axis=1)
    w_in_r = jnp.concatenate([w_in[:, o_z:o_dt], w_in[:, o_cq:o_kr], kr_blk, krs_blk, dt_blk], axis=1)
    r = w_uq.shape[0]
    zq = jnp.zeros((r, MLA_HEADS, HEAD_PAD - MLA_QK), F32)
    zn = jnp.zeros((r, MLA_HEADS, MLA_NOPE), F32)
    wq_plain = jnp.concatenate([w_uq, zq], axis=2).reshape(r, MLA_HEADS * HEAD_PAD)
    wq_rot = jnp.concatenate([zn, -w_uq[:, :, MLA_NOPE + half:], w_uq[:, :, MLA_NOPE:MLA_NOPE + half], zq],
                             axis=2).reshape(r, MLA_HEADS * HEAD_PAD)
    wq2 = jnp.concatenate([wq_plain, wq_rot], axis=1)
    rk = w_ukv.shape[0]
    zk = jnp.zeros((rk, MLA_HEADS, HEAD_PAD - MLA_NOPE), F32)
    wk = jnp.concatenate([w_ukv[:, :, :MLA_NOPE], zk], axis=2).reshape(rk, MLA_HEADS * HEAD_PAD)
    zv = jnp.zeros((rk, MLA_HEADS, MLA_V_ROWS - MLA_V), F32)
    wv_t = jnp.concatenate([w_ukv[:, :, MLA_NOPE:], zv], axis=2).reshape(rk, MLA_HEADS * MLA_V_ROWS).T
    return w_in_r.astype(BF16), wq2.astype(BF16), wk.astype(BF16), wv_t.astype(BF16)


def _inproj(x2, mod3, gpre, w_in_r, q_norm, wq2, kv_norm, wk, wv_t, cos_t, sin_t, seq, tm):
    t, d = x2.shape
    per_b = seq // tm
    row = lambda i: (i, 0)
    const = lambda i: (0, 0)
    nqk = MLA_HEADS * HEAD_PAD
    slabs = tm // ATT_BLOCK
    return pl.pallas_call(
        _inproj_kernel,
        grid=(t // tm,),
        in_specs=[pl.BlockSpec((tm, d), row),
                  pl.BlockSpec((1, N_MOD, d), lambda i: (i // per_b, 0, 0)),
                  pl.BlockSpec((1, d), const),
                  pl.BlockSpec(w_in_r.shape, const),
                  pl.BlockSpec((1, MLA_Q_RANK), const),
                  pl.BlockSpec(wq2.shape, const),
                  pl.BlockSpec((1, MLA_KV_RANK), const),
                  pl.BlockSpec(wk.shape, const),
                  pl.BlockSpec(wv_t.shape, const),
                  pl.BlockSpec((tm, LANES), row),
                  pl.BlockSpec((tm, LANES), row)],
        out_specs=[pl.BlockSpec((tm, SSD_INNER), row),
                   pl.BlockSpec((tm, SSD_XBC), row),
                   pl.BlockSpec((tm, LANES), row),
                   pl.BlockSpec((tm, nqk), row),
                   pl.BlockSpec((tm, nqk), row),
                   pl.BlockSpec((slabs, MLA_HEADS * MLA_V_ROWS, ATT_BLOCK), lambda i: (i, 0, 0))],
        out_shape=[jax.ShapeDtypeStruct((t, SSD_INNER), BF16),
                   jax.ShapeDtypeStruct((t, SSD_XBC), BF16),
                   jax.ShapeDtypeStruct((t, LANES), F32),
                   jax.ShapeDtypeStruct((t, nqk), BF16),
                   jax.ShapeDtypeStruct((t, nqk), BF16),
                   jax.ShapeDtypeStruct((t // ATT_BLOCK, MLA_HEADS * MLA_V_ROWS, ATT_BLOCK), BF16)],
        compiler_params=_params("arbitrary"),
        name="inproj",
    )(x2, mod3, gpre, w_in_r, q_norm, wq2, kv_norm, wk, wv_t, cos_t, sin_t)


def _split3_packed(v):
    lane = lax.broadcasted_iota(jnp.int32, v.shape, 1)
    v = jnp.where(lane < SSD_HEADS, v, 0.0)
    hi = v.astype(BF16).astype(F32)
    rest = v - hi
    mid = rest.astype(BF16).astype(F32)
    lo = rest - mid
    return (hi + pltpu.roll(mid, SSD_HEADS, 1) + pltpu.roll(lo, 2 * SSD_HEADS, 1)).astype(BF16)


def _ssd_kernel(xbc_ref, z_ref, dt_ref, cw_ref, cb_ref, dtb_ref, alog_ref, dsk_ref, gn_ref,
                e_ref, y_ref, tail_ref, state_ref):
    @pl.when(pl.program_id(1) == 0)
    def _():
        tail_ref[...] = jnp.zeros_like(tail_ref)
        state_ref[...] = jnp.zeros_like(state_ref)

    for c in range(xbc_ref.shape[0] // SSD_CHUNK):
        _ssd_chunk(pl.ds(c * SSD_CHUNK, SSD_CHUNK), xbc_ref, z_ref, dt_ref, cw_ref, cb_ref, dtb_ref,
                   alog_ref, dsk_ref, gn_ref, e_ref, y_ref, tail_ref, state_ref)


def _ssd_chunk(rows, xbc_ref, z_ref, dt_ref, cw_ref, cb_ref, dtb_ref, alog_ref, dsk_ref, gn_ref,
               e_ref, y_ref, tail_ref, state_ref):
    L = SSD_CHUNK
    gw = SSD_INNER // SSD_GROUPS

    cur_b = xbc_ref[rows, :]
    ext = jnp.concatenate([tail_ref[...], cur_b], axis=0)
    out_row = lax.broadcasted_iota(jnp.int32, ((SSD_CONV - 1) * L, ext.shape[0]), 0)
    src_row = lax.broadcasted_iota(jnp.int32, ((SSD_CONV - 1) * L, ext.shape[0]), 1)
    tail_rows = ext.shape[0] - L
    shift = jnp.where(src_row + (out_row // L + 1) == lax.rem(out_row, L) + tail_rows, 1.0, 0.0)
    shifted = jnp.dot(shift.astype(BF16), ext, preferred_element_type=F32)
    acc = cur_b.astype(F32) * cw_ref[SSD_CONV - 1:SSD_CONV, :] + cb_ref[...]
    for j in range(1, SSD_CONV):
        acc = acc + shifted[(j - 1) * L:j * L] * cw_ref[SSD_CONV - 1 - j:SSD_CONV - j, :]
    tail_ref[...] = cur_b[L - tail_rows:L]
    act = _silu(acc)
    xs = act[:, :SSD_INNER]
    bm = act[:, SSD_INNER:SSD_INNER + SSD_GROUPS * SSD_STATE].astype(BF16)
    cm = act[:, SSD_INNER + SSD_GROUPS * SSD_STATE:].astype(BF16)

    dt_in = dt_ref[rows, :] + dtb_ref[...]
    dt = jnp.maximum(dt_in, 0.0) + jnp.log(1.0 + jnp.exp(-jnp.abs(dt_in)))
    adt = dt * (-jnp.exp(alog_ref[...]))
    ri = lax.broadcasted_iota(jnp.int32, (L, L), 0)
    ci = lax.broadcasted_iota(jnp.int32, (L, L), 1)
    causal = ci <= ri
    acs_p = jnp.dot(causal.astype(BF16), _split3_packed(adt), preferred_element_type=F32)
    a_cs = (acs_p + pltpu.roll(acs_p, LANES - SSD_HEADS, 1)
            + pltpu.roll(acs_p, LANES - 2 * SSD_HEADS, 1))
    a_cs_t = a_cs.T
    cs_all = jnp.dot(_split3_packed(a_cs), e_ref[...], preferred_element_type=F32)
    cs64 = cs_all[:, :SSD_INNER]
    cs128 = cs_all[:, SSD_INNER:]
    dt64 = jnp.dot(_split3_packed(dt), e_ref[:, :SSD_INNER], preferred_element_type=F32)

    xd = xs * dt64
    xd_b = xd.astype(BF16)
    last = cs64[L - 1:L, :]
    xdw = (xd * jnp.exp(last - cs64)).astype(BF16)
    chunk_decay = jnp.exp(last)
    in_decay = jnp.exp(cs64)

    lane = lax.broadcasted_iota(jnp.int32, (L, LANES), 1)
    lo = lane < SSD_HEAD_DIM
    zero_b = jnp.zeros((L, LANES), BF16)
    y_parts = []
    new_states = []
    for g in range(SSD_GROUPS):
        bg = bm[:, g * SSD_STATE:(g + 1) * SSD_STATE]
        cg = cm[:, g * SSD_STATE:(g + 1) * SSD_STATE]
        cb = lax.dot_general(cg, bg, (((1,), (1,)), ((), ())), preferred_element_type=F32)
        prev = state_ref[:, g * gw:(g + 1) * gw]
        y_off = jnp.dot(cg, prev.astype(BF16), preferred_element_type=F32) * in_decay[:, g * gw:(g + 1) * gw]
        st = lax.dot_general(bg, xdw[:, g * gw:(g + 1) * gw], (((0,), (0,)), ((), ())),
                             preferred_element_type=F32)
        new_states.append(prev * chunk_decay[:, g * gw:(g + 1) * gw] + st)
        heads_per_group = SSD_HEADS // SSD_GROUPS
        for pair in range(heads_per_group // 2):
            h0 = g * heads_per_group + 2 * pair
            blk = xd_b[:, h0 * SSD_HEAD_DIM:(h0 + 2) * SSD_HEAD_DIM]
            y_pair = None
            for k in range(2):
                hh = h0 + k
                diff = cs128[:, hh * LANES:(hh + 1) * LANES] - a_cs_t[hh:hh + 1, :]
                decay = jnp.where(causal, jnp.exp(diff), 0.0)
                m = (cb * decay).astype(BF16)
                rhs = jnp.where(lo, blk, zero_b) if k == 0 else jnp.where(lo, zero_b, blk)
                part = jnp.dot(m, rhs, preferred_element_type=F32)
                y_pair = part if y_pair is None else y_pair + part
            c0 = (2 * pair) * SSD_HEAD_DIM
            y_parts.append(y_pair + y_off[:, c0:c0 + LANES])
    state_ref[...] = jnp.concatenate(new_states, axis=1)
    y = jnp.concatenate(y_parts, axis=1) + dsk_ref[...] * xs
    gated = y * _silu(z_ref[rows, :].astype(F32))
    y_ref[rows, :] = _rms(gated, gn_ref[...]).astype(BF16)


def _ssd(xbc, z, dt_raw, conv_w, conv_b, dt_bias, a_log, d_skip, ssd_norm, batch, seq):
    t = xbc.shape[0]
    L = SSD_STEP_ROWS
    nc = seq // L
    row = lambda b, c: (b * nc + c, 0)
    const = lambda b, c: (0, 0)
    pad = lambda v: jnp.concatenate([v.reshape(1, -1), jnp.zeros((1, LANES - v.size), F32)], axis=1)
    head_of_lane = jnp.concatenate([jnp.arange(SSD_INNER) // SSD_HEAD_DIM,
                                    jnp.arange(SSD_HEADS * LANES) // LANES])
    rows = jnp.arange(LANES)[:, None]
    expand = ((rows < 3 * SSD_HEADS) & (rows % SSD_HEADS == head_of_lane[None, :])).astype(BF16)
    dsk = jnp.repeat(d_skip.astype(F32), SSD_HEAD_DIM).reshape(1, SSD_INNER)
    return pl.pallas_call(
        _ssd_kernel,
        grid=(batch, nc),
        in_specs=[pl.BlockSpec((L, SSD_XBC), row),
                  pl.BlockSpec((L, SSD_INNER), row),
                  pl.BlockSpec((L, LANES), row),
                  pl.BlockSpec((SSD_CONV, SSD_XBC), const),
                  pl.BlockSpec((1, SSD_XBC), const),
                  pl.BlockSpec((1, LANES), const),
                  pl.BlockSpec((1, LANES), const),
                  pl.BlockSpec((1, SSD_INNER), const),
                  pl.BlockSpec((1, SSD_INNER), const),
                  pl.BlockSpec(expand.shape, const)],
        out_specs=pl.BlockSpec((L, SSD_INNER), row),
        out_shape=jax.ShapeDtypeStruct((t, SSD_INNER), BF16),
        scratch_shapes=[pltpu.VMEM((16, SSD_XBC), BF16),
                        pltpu.VMEM((SSD_STATE, SSD_INNER), F32)],
        compiler_params=_params("arbitrary", "arbitrary"),
        name="ssd",
    )(xbc, z, dt_raw, conv_w, conv_b.reshape(1, -1), pad(dt_bias), pad(a_log), dsk,
      ssd_norm.reshape(1, -1), expand)


def _attn_kernel(q_ref, k_ref, vt_ref, gn_ref, o_ref, s_ref):
    blk = ATT_BLOCK
    qi = pl.program_id(1)
    causal = (lax.broadcasted_iota(jnp.int32, (blk, blk), 0)
              <= lax.broadcasted_iota(jnp.int32, (blk, blk), 1))
    dims = (((1,), (1,)), ((), ()))
    qs = [q_ref[:, hd * HEAD_PAD:(hd + 1) * HEAD_PAD] for hd in range(MLA_HEADS)]

    def step(j, carry, nblk, masked):
        keys = nblk * blk
        rows = pl.ds(pl.multiple_of(j * blk, blk), keys)
        m_new = []
        for hd in range(MLA_HEADS):
            kk = k_ref[rows, hd * HEAD_PAD:(hd + 1) * HEAD_PAD]
            s = lax.dot_general(kk, qs[hd], dims, preferred_element_type=F32)
            if masked:
                s = jnp.where(causal, s, -jnp.inf)
            s_ref[hd, 0:keys, :] = s
            m_new.append(jnp.maximum(carry[hd][0], jnp.max(s, axis=0, keepdims=True)))
        new = []
        for hd in range(MLA_HEADS):
            m_old, acc = carry[hd]
            p = jnp.exp2(s_ref[hd, 0:keys, :] - m_new[hd]).astype(BF16)
            acc = acc * jnp.exp2(m_old - m_new[hd])
            for b in range(nblk):
                vt = vt_ref[j + b, hd * MLA_V_ROWS:(hd + 1) * MLA_V_ROWS, :]
                acc = acc + jnp.dot(vt, p[b * blk:(b + 1) * blk], preferred_element_type=F32)
            new.append((m_new[hd], acc))
        return tuple(new)

    neg = jnp.full((1, blk), -jnp.inf, F32)
    carry = tuple((neg, jnp.zeros((MLA_V_ROWS, blk), F32)) for _ in range(MLA_HEADS))
    quads = lax.shift_right_logical(qi, 2)
    has_pair = lax.rem(lax.shift_right_logical(qi, 1), 2)
    carry = lax.fori_loop(0, quads, lambda u, c: step(4 * u, c, 4, False), carry)
    carry = lax.fori_loop(0, has_pair, lambda _, c: step(4 * quads, c, 2, False), carry)
    carry = lax.fori_loop(0, lax.rem(qi, 2), lambda _, c: step(qi - 1, c, 1, False), carry)
    final = step(qi, carry, 1, True)
    y_t = jnp.concatenate([acc[:MLA_V] / acc[MLA_V:MLA_V + 1] for _, acc in final], axis=0)
    o_ref[...] = _rms(y_t.T, gn_ref[...]).astype(BF16)


def _attention(q, k, v_t, gn, batch, seq):
    t = q.shape[0]
    blk = ATT_BLOCK
    nq = seq // blk
    nqk = MLA_HEADS * HEAD_PAD
    return pl.pallas_call(
        _attn_kernel,
        grid=(batch, nq),
        in_specs=[pl.BlockSpec((blk, nqk), lambda b, i: (b * nq + i, 0)),
                  pl.BlockSpec((seq, nqk), lambda b, i: (b, 0)),
                  pl.BlockSpec((nq, MLA_HEADS * MLA_V_ROWS, blk), lambda b, i: (b, 0, 0)),
                  pl.BlockSpec((1, MLA_INNER), lambda b, i: (0, 0))],
        out_specs=pl.BlockSpec((blk, MLA_INNER), lambda b, i: (b * nq + i, 0)),
        out_shape=jax.ShapeDtypeStruct((t, MLA_INNER), BF16),
        scratch_shapes=[pltpu.VMEM((MLA_HEADS, 4 * blk, blk), F32)],
        compiler_params=_params("arbitrary", "arbitrary"),
        name="attention",
    )(q, k, v_t, gn)


def _outproj_kernel(ys_ref, ym_ref, x_ref, mod_ref, wtop_ref, wbot_ref, gpm_ref, gpf_ref,
                    wrh_ref, wrl_ref, br_ref, x1_ref, h2_ref, col_ref, info_ref, cnt_ref, ce_ref):
    y = (jnp.dot(ys_ref[...], wtop_ref[...], preferred_element_type=F32)
         + jnp.dot(ym_ref[...], wbot_ref[...], preferred_element_type=F32))
    g1 = mod_ref[0, 2:3, :]
    sh2 = mod_ref[0, 3:4, :]
    sc2 = mod_ref[0, 4:5, :]
    x1 = x_ref[...] + g1 * _rms(y, gpm_ref[...])
    x1_ref[...] = x1
    h2 = _rms(x1, gpf_ref[...]) * (1.0 + sc2) + sh2
    tm = h2.shape[0]
    h_hi = h2.astype(BF16)
    h_lo = (h2 - h_hi.astype(F32)).astype(BF16)
    logits = (jnp.dot(h_hi, wrh_ref[...], preferred_element_type=F32)
              + jnp.dot(h_lo, wrh_ref[...], preferred_element_type=F32)
              + jnp.dot(h_hi, wrl_ref[...], preferred_element_type=F32)) + br_ref[...]
    lt = logits.T
    row = lax.broadcasted_iota(jnp.int32, lt.shape, 0).astype(F32)
    ninf = -jnp.inf
    big = 1e9
    is_g = row < MOE_GROUPS
    gl = jnp.where(is_g, lt, ninf)
    gmax = jnp.max(gl, axis=0, keepdims=True)
    gsum = jnp.sum(jnp.where(is_g, jnp.exp(lt - gmax), 0.0), axis=0, keepdims=True)
    g_w = 1.0 / gsum
    g_idx = jnp.min(jnp.where(gl == gmax, row, big), axis=0, keepdims=True)
    first = MOE_GROUPS + MOE_EXPERTS_PER_GROUP * g_idx
    el = jnp.where(row >= first, jnp.where(row < first + MOE_EXPERTS_PER_GROUP, lt, ninf), ninf)
    m1 = jnp.max(el, axis=0, keepdims=True)
    i1 = jnp.min(jnp.where(el == m1, row, big), axis=0, keepdims=True)
    el2 = jnp.where(row == i1, ninf, el)
    m2 = jnp.max(el2, axis=0, keepdims=True)
    i2 = jnp.min(jnp.where(el2 == m2, row, big), axis=0, keepdims=True)
    r = jnp.exp(m2 - m1)
    w1 = g_w / (1.0 + r)
    w2 = g_w * r / (1.0 + r)
    sel1 = row == i1
    sel2 = row == i2
    both = jnp.where(sel1, 1.0, jnp.where(sel2, 1.0, 0.0))
    cnt16 = jnp.floor((jnp.sum(both, axis=1, keepdims=True) + (MOE_CHUNK - 1)) * (1.0 / MOE_CHUNK))
    er = lax.broadcasted_iota(jnp.int32, (LANES, LANES), 0)
    ec = lax.broadcasted_iota(jnp.int32, (LANES, LANES), 1)
    cnt16_b = jnp.broadcast_to(cnt16, (LANES, LANES))
    off16 = jnp.dot((ec < er).astype(BF16), cnt16_b.astype(BF16), preferred_element_type=F32)[:, 0:1]
    earlier = (lax.broadcasted_iota(jnp.int32, (tm, tm), 0)
               < lax.broadcasted_iota(jnp.int32, (tm, tm), 1)).astype(BF16)
    base = off16 * MOE_CHUNK + jnp.dot(both.astype(BF16), earlier, preferred_element_type=F32)
    lpos1 = jnp.sum(jnp.where(sel1, base, 0.0), axis=0, keepdims=True)
    lpos2 = jnp.sum(jnp.where(sel2, base, 0.0), axis=0, keepdims=True)
    h2_ref[...] = h_hi
    col_ref[...] = jnp.where(row == 0, lpos1, jnp.where(row == 1, lpos2, 0.0)).T
    row8 = lax.broadcasted_iota(jnp.int32, info_ref.shape, 0)
    info_ref[...] = jnp.where(row8 == 0, lpos1, jnp.where(row8 == 1, lpos2,
                              jnp.where(row8 == 2, w1, jnp.where(row8 == 3, w2, 0.0))))
    cnt_ref[...] = cnt16_b
    chunk = lax.broadcasted_iota(jnp.int32, (LANES, LANES), 1).astype(F32)
    ce = jnp.sum(jnp.where(off16 + cnt16 <= chunk, 1.0, 0.0), axis=0, keepdims=True) - MOE_GROUPS
    ce_ref[...] = jnp.broadcast_to(ce, ce_ref.shape)


def _outproj(y_ssd, y_mla, x2, mod3, w_top, w_bot, gpm, gpf, wr_hi, wr_lo, b_router, seq, tm):
    t, d = x2.shape
    per_b = seq // tm
    n_tok_tiles = t // tm
    row = lambda i: (i, 0)
    const = lambda i: (0, 0)
    return pl.pallas_call(
        _outproj_kernel,
        grid=(n_tok_tiles,),
        in_specs=[pl.BlockSpec((tm, SSD_INNER), row),
                  pl.BlockSpec((tm, MLA_INNER), row),
                  pl.BlockSpec((tm, d), row),
                  pl.BlockSpec((1, N_MOD, d), lambda i: (i // per_b, 0, 0)),
                  pl.BlockSpec(w_top.shape, const),
                  pl.BlockSpec(w_bot.shape, const),
                  pl.BlockSpec((1, d), const),
                  pl.BlockSpec((1, d), const),
                  pl.BlockSpec(wr_hi.shape, const),
                  pl.BlockSpec(wr_lo.shape, const),
                  pl.BlockSpec((1, LANES), const)],
        out_specs=[pl.BlockSpec((tm, d), row),
                   pl.BlockSpec((tm, d), row),
                   pl.BlockSpec((tm, LANES), row),
                   pl.BlockSpec((8, tm), row),
                   pl.BlockSpec((LANES, LANES), row),
                   pl.BlockSpec((8, LANES), row)],
        out_shape=[jax.ShapeDtypeStruct((t, d), F32),
                   jax.ShapeDtypeStruct((t, d), BF16),
                   jax.ShapeDtypeStruct((t, LANES), F32),
                   jax.ShapeDtypeStruct((n_tok_tiles * 8, tm), F32),
                   jax.ShapeDtypeStruct((n_tok_tiles * LANES, LANES), F32),
                   jax.ShapeDtypeStruct((n_tok_tiles * 8, LANES), F32)],
        compiler_params=_params("arbitrary"),
        name="outproj_router",
    )(y_ssd, y_mla, x2, mod3, w_top, w_bot, gpm, gpf, wr_hi, wr_lo, b_router)


def _moe_rows(n_tok_tiles):
    rows = n_tok_tiles * MOE_SLOTS + MOE_EXPERTS * (MOE_TILE - MOE_CHUNK)
    return (rows + MOE_TILE - 1) // MOE_TILE * MOE_TILE


def _moe_plan(cnt_tiles, chunk_expert, n_tok_tiles, n_rows):
    chunks_per_tile = MOE_SLOTS // MOE_CHUNK
    chunks_per_rows = MOE_TILE // MOE_CHUNK
    cnt = cnt_tiles.reshape(n_tok_tiles, LANES, LANES)[:, MOE_GROUPS:MOE_GROUPS + MOE_EXPERTS, 0]
    cnt = cnt.astype(jnp.int32)
    ce = chunk_expert.reshape(n_tok_tiles, 8, LANES)[:, 0, :chunks_per_tile].astype(jnp.int32)
    used = jnp.sum(cnt, axis=1)
    total = jnp.sum(cnt, axis=0)
    padded = (total + chunks_per_rows - 1) // chunks_per_rows * chunks_per_rows
    e_end = jnp.cumsum(padded)
    e_start = e_end - padded
    run_global = e_start[None, :] + jnp.cumsum(cnt, axis=0) - cnt
    run_local = jnp.cumsum(cnt, axis=1) - cnt
    experts = jnp.arange(MOE_EXPERTS, dtype=jnp.int32)
    shift = jnp.sum(jnp.where(ce[:, :, None] == experts, (run_global - run_local)[:, None, :], 0), axis=-1)
    dst = shift + jnp.arange(chunks_per_tile, dtype=jnp.int32)[None, :]
    n_row_tiles = n_rows // MOE_TILE
    n_used = e_end[-1] // chunks_per_rows
    j = jnp.minimum(jnp.arange(n_row_tiles, dtype=jnp.int32), jnp.maximum(n_used - 1, 0))
    tile_expert = jnp.sum((e_end[None, :] // chunks_per_rows <= j[:, None]).astype(jnp.int32), axis=1)
    tile_expert = jnp.minimum(tile_expert, MOE_EXPERTS - 1)
    i32 = lambda v: v.astype(jnp.int32)
    return (i32(dst.reshape(-1)), i32(used), i32(e_start + total), i32(padded - total),
            i32(n_used.reshape(1)), i32(tile_expert))


def _chunk_rows(chunk):
    return pl.ds(pl.multiple_of(chunk * MOE_CHUNK, MOE_CHUNK), MOE_CHUNK)


def _run_copies(*groups):
    for wait in (False, True):
        for lo, hi, make_copy in groups:
            def body(k, carry, wait=wait, make_copy=make_copy):
                cp = make_copy(k)
                cp.wait() if wait else cp.start()
                return carry
            lax.fori_loop(lo, hi, body, 0)


def _scatter_chunks_kernel(dst_ref, used_ref, pad_start_ref, pad_n_ref, nused_ref, h2_ref, info_ref,
                           o_ref, wslot_ref, src_ref, zero_ref, sem):
    i = pl.program_id(0)
    chunks = MOE_SLOTS // MOE_CHUNK
    slot = lax.rem(i, 2)

    info = info_ref[...]
    srow = lax.broadcasted_iota(jnp.int32, (MOE_SLOTS, info.shape[1]), 0)
    hit1 = srow == info[0:1, :].astype(jnp.int32)
    hit2 = srow == info[1:2, :].astype(jnp.int32)
    perm = jnp.where(hit1, 1.0, jnp.where(hit2, 1.0, 0.0)).astype(BF16)
    src_ref[slot] = jnp.dot(perm, h2_ref[...], preferred_element_type=F32).astype(BF16)
    wslot = jnp.sum(jnp.where(hit1, info[2:3, :], jnp.where(hit2, info[3:4, :], 0.0)), axis=1, keepdims=True)
    wslot_ref[...] = jnp.broadcast_to(wslot, wslot_ref.shape)

    @pl.when(i == 0)
    def _():
        zero_ref[...] = jnp.zeros_like(zero_ref)
        pads = [(0, pad_n_ref[e], lambda c, e=e: pltpu.make_async_copy(
            zero_ref.at[pl.ds(0, MOE_CHUNK), :], o_ref.at[_chunk_rows(pad_start_ref[e] + c), :], sem.at[2]))
            for e in range(MOE_EXPERTS)]
        tail = (nused_ref[0], o_ref.shape[0] // MOE_TILE, lambda j: pltpu.make_async_copy(
            zero_ref, o_ref.at[pl.ds(pl.multiple_of(j * MOE_TILE, MOE_TILE), MOE_TILE), :], sem.at[2]))
        _run_copies(tail, *pads)

    def chunk_copies(tile, buf, wait):
        def body(c, carry):
            cp = pltpu.make_async_copy(src_ref.at[buf, _chunk_rows(c), :],
                                       o_ref.at[_chunk_rows(dst_ref[tile * chunks + c]), :], sem.at[buf])
            cp.wait() if wait else cp.start()
            return carry
        lax.fori_loop(0, used_ref[tile], body, 0)

    chunk_copies(i, slot, False)

    @pl.when(i > 0)
    def _():
        chunk_copies(i - 1, 1 - slot, True)

    @pl.when(i == pl.num_programs(0) - 1)
    def _():
        chunk_copies(i, slot, True)


def _scatter_chunks(h2, info, dst, used, pad_start, pad_n, n_used, n_rows):
    t, d = h2.shape
    tm = info.shape[1]
    row = lambda i, *_: (i, 0)
    return pl.pallas_call(
        _scatter_chunks_kernel,
        grid_spec=pltpu.PrefetchScalarGridSpec(
            num_scalar_prefetch=5,
            grid=(used.shape[0],),
            in_specs=[pl.BlockSpec((tm, d), row), pl.BlockSpec((8, tm), row)],
            out_specs=[pl.BlockSpec(memory_space=pl.ANY), pl.BlockSpec((MOE_SLOTS, LANES), row)],
            scratch_shapes=[pltpu.VMEM((2, MOE_SLOTS, d), BF16), pltpu.VMEM((MOE_TILE, d), BF16),
                            pltpu.SemaphoreType.DMA((3,))]),
        out_shape=[jax.ShapeDtypeStruct((n_rows, d), BF16),
                   jax.ShapeDtypeStruct((used.shape[0] * MOE_SLOTS, LANES), F32)],
        compiler_params=_params("arbitrary"),
        name="moe_scatter_chunks",
    )(dst, used, pad_start, pad_n, n_used, h2, info)


def _experts_kernel(te_ref, nt_ref, xs_ref, wg_ref, wu_ref, wd_ref, o_ref):
    used = pl.program_id(0) < nt_ref[0]

    @pl.when(used)
    def _():
        x = xs_ref[...]
        gate = jnp.dot(x, wg_ref[0].astype(BF16), preferred_element_type=F32)
        up = jnp.dot(x, wu_ref[0].astype(BF16), preferred_element_type=F32)
        hid = (_silu(gate) * up).astype(BF16)
        o_ref[...] = jnp.dot(hid, wd_ref[0].astype(BF16), preferred_element_type=F32).astype(BF16)

    @pl.when(jnp.logical_not(used))
    def _():
        o_ref[...] = jnp.zeros_like(o_ref)


def _experts(xs, tile_expert, n_used, w_gate, w_up, w_down):
    n_rows, d = xs.shape
    tile = MOE_TILE
    by_expert = lambda j, te, nt: (te[j], 0, 0)
    return pl.pallas_call(
        _experts_kernel,
        grid_spec=pltpu.PrefetchScalarGridSpec(
            num_scalar_prefetch=2,
            grid=(n_rows // tile,),
            in_specs=[pl.BlockSpec((tile, d), lambda j, te, nt: (jnp.maximum(jnp.minimum(j, nt[0] - 1), 0), 0)),
                      pl.BlockSpec((1, d, MOE_FF), by_expert),
                      pl.BlockSpec((1, d, MOE_FF), by_expert),
                      pl.BlockSpec((1, MOE_FF, d), by_expert)],
            out_specs=pl.BlockSpec((tile, d), lambda j, te, nt: (j, 0))),
        out_shape=jax.ShapeDtypeStruct((n_rows, d), BF16),
        compiler_params=_params("arbitrary"),
        name="moe_experts",
    )(tile_expert, n_used, xs, w_gate, w_up, w_down)


def _combine_kernel(dst_ref, used_ref, wslot_ref, col_ref, x1_ref, mod_ref, gpost_ref, ys_ref, o_ref,
                    ysl_ref, sem):
    i = pl.program_id(0)
    chunks = MOE_SLOTS // MOE_CHUNK
    slot = lax.rem(i, 2)

    def fetch(tile, buf, wait):
        base = tile * chunks
        used = used_ref[tile]

        def copy(c, carry):
            cp = pltpu.make_async_copy(ys_ref.at[_chunk_rows(dst_ref[base + c]), :],
                                       ysl_ref.at[buf, _chunk_rows(c), :], sem.at[buf])
            cp.wait() if wait else cp.start()
            return carry

        def clear(c, carry):
            ysl_ref[buf, _chunk_rows(c), :] = jnp.zeros((MOE_CHUNK, ysl_ref.shape[2]), BF16)
            return carry

        lax.fori_loop(0, used, copy, 0)
        if not wait:
            lax.fori_loop(used, chunks, clear, 0)

    @pl.when(i == 0)
    def _():
        fetch(0, 0, False)

    @pl.when(i + 1 < pl.num_programs(0))
    def _():
        fetch(i + 1, 1 - slot, False)

    col = col_ref[...]
    tm = col.shape[0]
    lane = lax.broadcasted_iota(jnp.int32, (tm, MOE_SLOTS), 1)
    hit = jnp.where(lane == col[:, 0:1].astype(jnp.int32), 1.0,
                    jnp.where(lane == col[:, 1:2].astype(jnp.int32), 1.0, 0.0)).astype(BF16)
    fetch(i, slot, True)
    ysw = (ysl_ref[slot].astype(F32) * wslot_ref[:, 0:1]).astype(BF16)
    y = jnp.dot(hit, ysw, preferred_element_type=F32)
    g2 = mod_ref[0, 5:6, :]
    o_ref[...] = x1_ref[...] + g2 * _rms(y, gpost_ref[...])


def _combine(ys, dst, used, wslot, col, x1, mod3, gpost, seq, tm):
    t, d = x1.shape
    per_b = seq // tm
    row = lambda i, *_: (i, 0)
    return pl.pallas_call(
        _combine_kernel,
        grid_spec=pltpu.PrefetchScalarGridSpec(
            num_scalar_prefetch=2,
            grid=(t // tm,),
            in_specs=[pl.BlockSpec((MOE_SLOTS, LANES), row),
                      pl.BlockSpec((tm, LANES), row),
                      pl.BlockSpec((tm, d), row),
                      pl.BlockSpec((1, N_MOD, d), lambda i, *_: (i // per_b, 0, 0)),
                      pl.BlockSpec((1, d), lambda i, *_: (0, 0)),
                      pl.BlockSpec(memory_space=pl.ANY)],
            out_specs=pl.BlockSpec((tm, d), row),
            scratch_shapes=[pltpu.VMEM((2, MOE_SLOTS, d), BF16), pltpu.SemaphoreType.DMA((2,))]),
        out_shape=jax.ShapeDtypeStruct((t, d), F32),
        compiler_params=_params("arbitrary"),
        name="moe_combine",
    )(dst, used, wslot, col, x1, mod3, gpost, ys)


def kernel(x, c, positions, ada_w, ada_b, pre_norm_mix, post_norm_mix, pre_norm_ffn, post_norm_ffn, w_in, conv_w, conv_b, dt_bias, a_log, d_skip, ssd_norm, q_norm, w_uq, kv_norm, w_ukv, mla_out_norm, w_out, w_group_router, b_group_router, w_expert_router, b_expert_router, w_gate, w_up, w_down):
    batch, seq, d = x.shape
    t = batch * seq
    depth = ada_w.shape[0]
    tm = min(INPROJ_TOKENS, seq)
    cos_t, sin_t = _rope_tables(positions)

    x2 = x.reshape(t, d)
    for l in range(depth):
        mod3 = _modulation(c, ada_w[l], ada_b[l]).reshape(batch, N_MOD, d)
        w_in_r, wq2, wk, wv_t = _inproj_weights(w_in[l], w_uq[l], w_ukv[l])
        z, xbc, dt_raw, q, k, v_t = _inproj(
            x2, mod3, pre_norm_mix[l].reshape(1, d), w_in_r, q_norm[l].reshape(1, -1), wq2,
            kv_norm[l].reshape(1, -1), wk, wv_t, cos_t, sin_t, seq, tm)
        y_ssd = _ssd(xbc, z, dt_raw, conv_w[l], conv_b[l], dt_bias[l], a_log[l], d_skip[l],
                     ssd_norm[l], batch, seq)
        y_mla = _attention(q, k, v_t, mla_out_norm[l].reshape(1, -1), batch, seq)
        w_o = w_out[l].astype(BF16)
        pad_r = LANES - MOE_GROUPS - MOE_EXPERTS
        w_router = jnp.concatenate([w_group_router[l], w_expert_router[l], jnp.zeros((d, pad_r), F32)], axis=1)
        b_router = jnp.concatenate([b_group_router[l].reshape(-1), b_expert_router[l].reshape(-1),
                                    jnp.zeros((pad_r,), F32)]).reshape(1, LANES)
        wr_hi = w_router.astype(BF16)
        wr_lo = (w_router - wr_hi.astype(F32)).astype(BF16)
        x1, h2, col, info, cnt_tiles, chunk_expert = _outproj(
            y_ssd, y_mla, x2, mod3, w_o[:SSD_INNER], w_o[SSD_INNER:], post_norm_mix[l].reshape(1, d),
            pre_norm_ffn[l].reshape(1, d), wr_hi, wr_lo, b_router, seq, MOE_TOKENS)
        n_tok_tiles = t // MOE_TOKENS
        n_rows = _moe_rows(n_tok_tiles)
        dst, used, pad_start, pad_n, n_used, tile_expert = _moe_plan(cnt_tiles, chunk_expert, n_tok_tiles, n_rows)
        xs, wslot = _scatter_chunks(h2, info, dst, used, pad_start, pad_n, n_used, n_rows)
        ys = _experts(xs, tile_expert, n_used, w_gate[l], w_up[l], w_down[l])
        x2 = _combine(ys, dst, used, wslot, col, x1, mod3, post_norm_ffn[l].reshape(1, d), seq, MOE_TOKENS)
    return x2.reshape(batch, seq, d)
```

```python
import functools
import math

import jax
import jax.numpy as jnp
from jax import lax
from jax.experimental import pallas as pl
from jax.experimental.pallas import tpu as pltpu

F32 = jnp.float32
BF16 = jnp.bfloat16

D_MODEL = 1024
SSD_HEADS = 8
SSD_HEAD_DIM = 64
SSD_INNER = SSD_HEADS * SSD_HEAD_DIM
SSD_GROUPS = 2
SSD_STATE = 128
SSD_CONV = 4
SSD_CHUNK = 128
SSD_XBC = SSD_INNER + 2 * SSD_GROUPS * SSD_STATE
MLA_HEADS = 8
MLA_NOPE = 64
MLA_ROPE = 32
MLA_QK = MLA_NOPE + MLA_ROPE
MLA_V = 64
MLA_Q_RANK = 256
MLA_KV_RANK = 128
MLA_INNER = MLA_HEADS * MLA_V
ROPE_THETA = 10000.0
MOE_GROUPS = 4
MOE_EXPERTS_PER_GROUP = 8
MOE_EXPERTS = MOE_GROUPS * MOE_EXPERTS_PER_GROUP
MOE_FF = 256
N_MOD = 6
EPS = 1e-6
LOG2_E = math.log2(math.e)

LANES = 128
HEAD_PAD = 128
ATT_BLOCK = 256
MLA_V_ROWS = MLA_V + 16
INPROJ_TOKENS = 512
SSD_STEP_ROWS = 4 * SSD_CHUNK
MOE_TOKENS = 512
MOE_CHUNK = 16
MOE_TILE = 512
MOE_SLOTS = 2 * MOE_TOKENS + MOE_EXPERTS * MOE_CHUNK
VMEM_LIMIT = 48 * 1024 * 1024

_C_Z = 0
_C_XBC = _C_Z + SSD_INNER
_C_CQ = _C_XBC + SSD_XBC
_C_CKV = _C_CQ + MLA_Q_RANK
_C_KR = _C_CKV + MLA_KV_RANK
_C_KRS = _C_KR + LANES
_C_DT = _C_KRS + LANES
_C_END = _C_DT + LANES


def _silu(v):
    return v * (1.0 / (1.0 + jnp.exp(-v)))


def _rms(v, gain):
    return v * lax.rsqrt(jnp.mean(v * v, axis=-1, keepdims=True) + EPS) * gain


def _params(*sem, flags=None):
    return pltpu.CompilerParams(dimension_semantics=sem, vmem_limit_bytes=VMEM_LIMIT, flags=flags)


def _mod_kernel(c_ref, w_ref, b_ref, o_ref):
    act = _silu(c_ref[...])
    a_hi = act.astype(BF16)
    a_lo = (act - a_hi.astype(F32)).astype(BF16)
    w = w_ref[...]
    w_hi = w.astype(BF16)
    w_lo = (w - w_hi.astype(F32)).astype(BF16)
    o_ref[...] = (jnp.dot(a_hi, w_hi, preferred_element_type=F32)
                  + jnp.dot(a_lo, w_hi, preferred_element_type=F32)
                  + jnp.dot(a_hi, w_lo, preferred_element_type=F32)) + b_ref[...]


def _modulation(c, ada_w, ada_b):
    b, d = c.shape
    n = ada_w.shape[1]
    return pl.pallas_call(
        _mod_kernel,
        grid=(n // d,),
        in_specs=[pl.BlockSpec((b, d), lambda j: (0, 0)),
                  pl.BlockSpec((d, d), lambda j: (0, j)),
                  pl.BlockSpec((1, d), lambda j: (0, j))],
        out_specs=pl.BlockSpec((b, d), lambda j: (0, j)),
        out_shape=jax.ShapeDtypeStruct((b, n), F32),
        compiler_params=_params("arbitrary"),
        name="modulation",
    )(c, ada_w, ada_b.reshape(1, n))


def _rope_kernel(pos_ref, inv_ref, cos_ref, sin_ref):
    ang = pos_ref[...].astype(F32) * inv_ref[...]
    blk = ang.shape[0]
    half = MLA_ROPE // 2
    per_row = LANES // half
    lane = lax.broadcasted_iota(jnp.int32, (blk * per_row, LANES), 1)
    first = (lane >= MLA_NOPE) & (lane < MLA_NOPE + half)
    second = (lane >= MLA_NOPE + half) & (lane < MLA_QK)
    for fn, out_ref, fill in ((jnp.cos, cos_ref, 1.0), (jnp.sin, sin_ref, 0.0)):
        compact = fn(ang)
        rep = jnp.broadcast_to(compact[:, None, :], (blk, per_row, LANES)).reshape(blk * per_row, LANES)
        a = pltpu.roll(rep, 0, 1, stride=half, stride_axis=0)
        b = pltpu.roll(a, half, 1)
        out_ref[...] = jnp.where(first, a, jnp.where(second, b, fill))


def _rope_tables(positions):
    t = positions.size
    half = MLA_ROPE // 2
    per_row = LANES // half
    inv = 1.0 / (ROPE_THETA ** (jnp.arange(0, MLA_ROPE, 2, dtype=F32) / MLA_ROPE))
    group_token = (MLA_NOPE // half - jnp.arange(per_row)) % per_row
    pos_rep = jnp.repeat(positions.reshape(t // per_row, per_row)[:, group_token], half, axis=1)
    inv_t = jnp.tile(inv, per_row).reshape(1, LANES)
    rows = t // per_row
    blk = min(rows, 512)
    return pl.pallas_call(
        _rope_kernel,
        grid=(rows // blk,),
        in_specs=[pl.BlockSpec((blk, LANES), lambda i: (i, 0)),
                  pl.BlockSpec((1, LANES), lambda i: (0, 0))],
        out_specs=[pl.BlockSpec((blk * per_row, LANES), lambda i: (i, 0))] * 2,
        out_shape=[jax.ShapeDtypeStruct((t, LANES), F32)] * 2,
        compiler_params=_params("arbitrary"),
        name="rope_tables",
    )(pos_rep, inv_t)


def _inproj_kernel(x_ref, mod_ref, gpre_ref, win_ref, qn_ref, wq_ref, kvn_ref, wk_ref, wvt_ref,
                   cos_ref, sin_ref, z_ref, xbc_ref, dt_ref, q_ref, k_ref, vt_ref):
    x = x_ref[...]
    sh = mod_ref[0, 0:1, :]
    sc = mod_ref[0, 1:2, :]
    h = (_rms(x, gpre_ref[...]) * (1.0 + sc) + sh).astype(BF16)
    u = jnp.dot(h, win_ref[...], preferred_element_type=F32)
    z_ref[...] = u[:, _C_Z:_C_XBC].astype(BF16)
    xbc_ref[...] = u[:, _C_XBC:_C_CQ].astype(BF16)
    dt_ref[...] = u[:, _C_DT:_C_END]
    cos_t = cos_ref[...]
    sin_t = sin_ref[...]
    cq = _rms(u[:, _C_CQ:_C_CKV], qn_ref[...]).astype(BF16)
    q2 = jnp.dot(cq, wq_ref[...], preferred_element_type=F32)
    scale = MLA_QK ** -0.5 * LOG2_E
    nq = MLA_HEADS * HEAD_PAD
    for hd in range(MLA_HEADS):
        a = q2[:, hd * HEAD_PAD:(hd + 1) * HEAD_PAD]
        b = q2[:, nq + hd * HEAD_PAD:nq + (hd + 1) * HEAD_PAD]
        q_ref[:, hd * HEAD_PAD:(hd + 1) * HEAD_PAD] = ((a * cos_t + b * sin_t) * scale).astype(BF16)
    ckv = _rms(u[:, _C_CKV:_C_KR], kvn_ref[...]).astype(BF16)
    kn = jnp.dot(ckv, wk_ref[...], preferred_element_type=F32)
    k_pe = u[:, _C_KR:_C_KRS] * cos_t + u[:, _C_KRS:_C_DT] * sin_t
    for hd in range(MLA_HEADS):
        k_ref[:, hd * HEAD_PAD:(hd + 1) * HEAD_PAD] = (
            kn[:, hd * HEAD_PAD:(hd + 1) * HEAD_PAD] + k_pe).astype(BF16)
    v_t = lax.dot_general(wvt_ref[...], ckv, (((1,), (1,)), ((), ())), preferred_element_type=F32)
    head_row = lax.rem(lax.broadcasted_iota(jnp.int32, (v_t.shape[0], 1), 0), MLA_V_ROWS)
    v_t = v_t + jnp.where(head_row == MLA_V, 1.0, 0.0)
    for s in range(vt_ref.shape[0]):
        vt_ref[s] = v_t[:, s * ATT_BLOCK:(s + 1) * ATT_BLOCK].astype(BF16)


def _inproj_weights(w_in, w_uq, w_ukv):
    d = w_in.shape[0]
    half = MLA_ROPE // 2
    o_z, o_xbc, o_dt = 0, SSD_INNER, SSD_INNER + SSD_XBC
    o_cq = o_dt + SSD_HEADS
    o_ckv = o_cq + MLA_Q_RANK
    o_kr = o_ckv + MLA_KV_RANK
    zeros = lambda n: jnp.zeros((d, n), F32)
    kr = w_in[:, o_kr:o_kr + MLA_ROPE]
    kr_blk = jnp.concatenate([zeros(MLA_NOPE), kr, zeros(HEAD_PAD - MLA_QK)], axis=1)
    krs_blk = jnp.concatenate([zeros(MLA_NOPE), -kr[:, half:], kr[:, :half], zeros(HEAD_PAD - MLA_QK)], axis=1)
    dt_blk = jnp.concatenate([w_in[:, o_dt:o_dt + SSD_HEADS], zeros(LANES - SSD_HEADS)], axis=1)
    w_in_r = jnp.concatenate([w_in[:, o_z:o_dt], w_in[:, o_cq:o_kr], kr_blk, krs_blk, dt_blk], axis=1)
    r = w_uq.shape[0]
    zq = jnp.zeros((r, MLA_HEADS, HEAD_PAD - MLA_QK), F32)
    zn = jnp.zeros((r, MLA_HEADS, MLA_NOPE), F32)
    wq_plain = jnp.concatenate([w_uq, zq], axis=2).reshape(r, MLA_HEADS * HEAD_PAD)
    wq_rot = jnp.concatenate([zn, -w_uq[:, :, MLA_NOPE + half:], w_uq[:, :, MLA_NOPE:MLA_NOPE + half], zq],
                             axis=2).reshape(r, MLA_HEADS * HEAD_PAD)
    wq2 = jnp.concatenate([wq_plain, wq_rot], axis=1)
    rk = w_ukv.shape[0]
    zk = jnp.zeros((rk, MLA_HEADS, HEAD_PAD - MLA_NOPE), F32)
    wk = jnp.concatenate([w_ukv[:, :, :MLA_NOPE], zk], axis=2).reshape(rk, MLA_HEADS * HEAD_PAD)
    zv = jnp.zeros((rk, MLA_HEADS, MLA_V_ROWS - MLA_V), F32)
    wv_t = jnp.concatenate([w_ukv[:, :, MLA_NOPE:], zv], axis=2).reshape(rk, MLA_HEADS * MLA_V_ROWS).T
    return w_in_r.astype(BF16), wq2.astype(BF16), wk.astype(BF16), wv_t.astype(BF16)


def _inproj(x2, mod3, gpre, w_in_r, q_norm, wq2, kv_norm, wk, wv_t, cos_t, sin_t, seq, tm):
    t, d = x2.shape
    per_b = seq // tm
    row = lambda i: (i, 0)
    const = lambda i: (0, 0)
    nqk = MLA_HEADS * HEAD_PAD
    slabs = tm // ATT_BLOCK
    return pl.pallas_call(
        _inproj_kernel,
        grid=(t // tm,),
        in_specs=[pl.BlockSpec((tm, d), row),
                  pl.BlockSpec((1, N_MOD, d), lambda i: (i // per_b, 0, 0)),
                  pl.BlockSpec((1, d), const),
                  pl.BlockSpec(w_in_r.shape, const),
                  pl.BlockSpec((1, MLA_Q_RANK), const),
                  pl.BlockSpec(wq2.shape, const),
                  pl.BlockSpec((1, MLA_KV_RANK), const),
                  pl.BlockSpec(wk.shape, const),
                  pl.BlockSpec(wv_t.shape, const),
                  pl.BlockSpec((tm, LANES), row),
                  pl.BlockSpec((tm, LANES), row)],
        out_specs=[pl.BlockSpec((tm, SSD_INNER), row),
                   pl.BlockSpec((tm, SSD_XBC), row),
                   pl.BlockSpec((tm, LANES), row),
                   pl.BlockSpec((tm, nqk), row),
                   pl.BlockSpec((tm, nqk), row),
                   pl.BlockSpec((slabs, MLA_HEADS * MLA_V_ROWS, ATT_BLOCK), lambda i: (i, 0, 0))],
        out_shape=[jax.ShapeDtypeStruct((t, SSD_INNER), BF16),
                   jax.ShapeDtypeStruct((t, SSD_XBC), BF16),
                   jax.ShapeDtypeStruct((t, LANES), F32),
                   jax.ShapeDtypeStruct((t, nqk), BF16),
                   jax.ShapeDtypeStruct((t, nqk), BF16),
                   jax.ShapeDtypeStruct((t // ATT_BLOCK, MLA_HEADS * MLA_V_ROWS, ATT_BLOCK), BF16)],
        compiler_params=_params("arbitrary"),
        name="inproj",
    )(x2, mod3, gpre, w_in_r, q_norm, wq2, kv_norm, wk, wv_t, cos_t, sin_t)


def _split3_packed(v):
    lane = lax.broadcasted_iota(jnp.int32, v.shape, 1)
    v = jnp.where(lane < SSD_HEADS, v, 0.0)
    hi = v.astype(BF16).astype(F32)
    rest = v - hi
    mid = rest.astype(BF16).astype(F32)
    lo = rest - mid
    return (hi + pltpu.roll(mid, SSD_HEADS, 1) + pltpu.roll(lo, 2 * SSD_HEADS, 1)).astype(BF16)


def _ssd_kernel(xbc_ref, z_ref, dt_ref, cw_ref, cb_ref, dtb_ref, alog_ref, dsk_ref, gn_ref,
                e_ref, y_ref, tail_ref, state_ref):
    @pl.when(pl.program_id(1) == 0)
    def _():
        tail_ref[...] = jnp.zeros_like(tail_ref)
        state_ref[...] = jnp.zeros_like(state_ref)

    for c in range(xbc_ref.shape[0] // SSD_CHUNK):
        _ssd_chunk(pl.ds(c * SSD_CHUNK, SSD_CHUNK), xbc_ref, z_ref, dt_ref, cw_ref, cb_ref, dtb_ref,
                   alog_ref, dsk_ref, gn_ref, e_ref, y_ref, tail_ref, state_ref)


def _ssd_chunk(rows, xbc_ref, z_ref, dt_ref, cw_ref, cb_ref, dtb_ref, alog_ref, dsk_ref, gn_ref,
               e_ref, y_ref, tail_ref, state_ref):
    L = SSD_CHUNK
    gw = SSD_INNER // SSD_GROUPS

    cur_b = xbc_ref[rows, :]
    ext = jnp.concatenate([tail_ref[...], cur_b], axis=0)
    out_row = lax.broadcasted_iota(jnp.int32, ((SSD_CONV - 1) * L, ext.shape[0]), 0)
    src_row = lax.broadcasted_iota(jnp.int32, ((SSD_CONV - 1) * L, ext.shape[0]), 1)
    tail_rows = ext.shape[0] - L
    shift = jnp.where(src_row + (out_row // L + 1) == lax.rem(out_row, L) + tail_rows, 1.0, 0.0)
    shifted = jnp.dot(shift.astype(BF16), ext, preferred_element_type=F32)
    acc = cur_b.astype(F32) * cw_ref[SSD_CONV - 1:SSD_CONV, :] + cb_ref[...]
    for j in range(1, SSD_CONV):
        acc = acc + shifted[(j - 1) * L:j * L] * cw_ref[SSD_CONV - 1 - j:SSD_CONV - j, :]
    tail_ref[...] = cur_b[L - tail_rows:L]
    act = _silu(acc)
    xs = act[:, :SSD_INNER]
    bm = act[:, SSD_INNER:SSD_INNER + SSD_GROUPS * SSD_STATE].astype(BF16)
    cm = act[:, SSD_INNER + SSD_GROUPS * SSD_STATE:].astype(BF16)

    dt_in = dt_ref[rows, :] + dtb_ref[...]
    dt = jnp.maximum(dt_in, 0.0) + jnp.log(1.0 + jnp.exp(-jnp.abs(dt_in)))
    adt = dt * (-jnp.exp(alog_ref[...]))
    ri = lax.broadcasted_iota(jnp.int32, (L, L), 0)
    ci = lax.broadcasted_iota(jnp.int32, (L, L), 1)
    causal = ci <= ri
    acs_p = jnp.dot(causal.astype(BF16), _split3_packed(adt), preferred_element_type=F32)
    a_cs = (acs_p + pltpu.roll(acs_p, LANES - SSD_HEADS, 1)
            + pltpu.roll(acs_p, LANES - 2 * SSD_HEADS, 1))
    a_cs_t = a_cs.T
    cs_all = jnp.dot(_split3_packed(a_cs), e_ref[...], preferred_element_type=F32)
    cs64 = cs_all[:, :SSD_INNER]
    cs128 = cs_all[:, SSD_INNER:]
    dt64 = jnp.dot(_split3_packed(dt), e_ref[:, :SSD_INNER], preferred_element_type=F32)

    xd = xs * dt64
    xd_b = xd.astype(BF16)
    last = cs64[L - 1:L, :]
    xdw = (xd * jnp.exp(last - cs64)).astype(BF16)
    chunk_decay = jnp.exp(last)
    in_decay = jnp.exp(cs64)

    lane = lax.broadcasted_iota(jnp.int32, (L, LANES), 1)
    lo = lane < SSD_HEAD_DIM
    zero_b = jnp.zeros((L, LANES), BF16)
    y_parts = []
    new_states = []
    for g in range(SSD_GROUPS):
        bg = bm[:, g * SSD_STATE:(g + 1) * SSD_STATE]
        cg = cm[:, g * SSD_STATE:(g + 1) * SSD_STATE]
        cb = lax.dot_general(cg, bg, (((1,), (1,)), ((), ())), preferred_element_type=F32)
        prev = state_ref[:, g * gw:(g + 1) * gw]
        y_off = jnp.dot(cg, prev.astype(BF16), preferred_element_type=F32) * in_decay[:, g * gw:(g + 1) * gw]
        st = lax.dot_general(bg, xdw[:, g * gw:(g + 1) * gw], (((0,), (0,)), ((), ())),
                             preferred_element_type=F32)
        new_states.append(prev * chunk_decay[:, g * gw:(g + 1) * gw] + st)
        heads_per_group = SSD_HEADS // SSD_GROUPS
        for pair in range(heads_per_group // 2):
            h0 = g * heads_per_group + 2 * pair
            blk = xd_b[:, h0 * SSD_HEAD_DIM:(h0 + 2) * SSD_HEAD_DIM]
            y_pair = None
            for k in range(2):
                hh = h0 + k
                diff = cs128[:, hh * LANES:(hh + 1) * LANES] - a_cs_t[hh:hh + 1, :]
                decay = jnp.where(causal, jnp.exp(diff), 0.0)
                m = (cb * decay).astype(BF16)
                rhs = jnp.where(lo, blk, zero_b) if k == 0 else jnp.where(lo, zero_b, blk)
                part = jnp.dot(m, rhs, preferred_element_type=F32)
                y_pair = part if y_pair is None else y_pair + part
            c0 = (2 * pair) * SSD_HEAD_DIM
            y_parts.append(y_pair + y_off[:, c0:c0 + LANES])
    state_ref[...] = jnp.concatenate(new_states, axis=1)
    y = jnp.concatenate(y_parts, axis=1) + dsk_ref[...] * xs
    gated = y * _silu(z_ref[rows, :].astype(F32))
    y_ref[rows, :] = _rms(gated, gn_ref[...]).astype(BF16)


def _ssd(xbc, z, dt_raw, conv_w, conv_b, dt_bias, a_log, d_skip, ssd_norm, batch, seq):
    t = xbc.shape[0]
    L = SSD_STEP_ROWS
    nc = seq // L
    row = lambda b, c: (b * nc + c, 0)
    const = lambda b, c: (0, 0)
    pad = lambda v: jnp.concatenate([v.reshape(1, -1), jnp.zeros((1, LANES - v.size), F32)], axis=1)
    head_of_lane = jnp.concatenate([jnp.arange(SSD_INNER) // SSD_HEAD_DIM,
                                    jnp.arange(SSD_HEADS * LANES) // LANES])
    rows = jnp.arange(LANES)[:, None]
    expand = ((rows < 3 * SSD_HEADS) & (rows % SSD_HEADS == head_of_lane[None, :])).astype(BF16)
    dsk = jnp.repeat(d_skip.astype(F32), SSD_HEAD_DIM).reshape(1, SSD_INNER)
    return pl.pallas_call(
        _ssd_kernel,
        grid=(batch, nc),
        in_specs=[pl.BlockSpec((L, SSD_XBC), row),
                  pl.BlockSpec((L, SSD_INNER), row),
                  pl.BlockSpec((L, LANES), row),
                  pl.BlockSpec((SSD_CONV, SSD_XBC), const),
                  pl.BlockSpec((1, SSD_XBC), const),
                  pl.BlockSpec((1, LANES), const),
                  pl.BlockSpec((1, LANES), const),
                  pl.BlockSpec((1, SSD_INNER), const),
                  pl.BlockSpec((1, SSD_INNER), const),
                  pl.BlockSpec(expand.shape, const)],
        out_specs=pl.BlockSpec((L, SSD_INNER), row),
        out_shape=jax.ShapeDtypeStruct((t, SSD_INNER), BF16),
        scratch_shapes=[pltpu.VMEM((16, SSD_XBC), BF16),
                        pltpu.VMEM((SSD_STATE, SSD_INNER), F32)],
        compiler_params=_params("arbitrary", "arbitrary"),
        name="ssd",
    )(xbc, z, dt_raw, conv_w, conv_b.reshape(1, -1), pad(dt_bias), pad(a_log), dsk,
      ssd_norm.reshape(1, -1), expand)


def _attn_kernel(q_ref, k_ref, vt_ref, gn_ref, o_ref, s_ref):
    blk = ATT_BLOCK
    qi = pl.program_id(1)
    causal = (lax.broadcasted_iota(jnp.int32, (blk, blk), 0)
              <= lax.broadcasted_iota(jnp.int32, (blk, blk), 1))
    dims = (((1,), (1,)), ((), ()))
    qs = [q_ref[:, hd * HEAD_PAD:(hd + 1) * HEAD_PAD] for hd in range(MLA_HEADS)]

    def step(j, carry, nblk, masked):
        keys = nblk * blk
        rows = pl.ds(pl.multiple_of(j * blk, blk), keys)
        m_new = []
        for hd in range(MLA_HEADS):
            kk = k_ref[rows, hd * HEAD_PAD:(hd + 1) * HEAD_PAD]
            s = lax.dot_general(kk, qs[hd], dims, preferred_element_type=F32)
            if masked:
                s = jnp.where(causal, s, -jnp.inf)
            s_ref[hd, 0:keys, :] = s
            m_new.append(jnp.maximum(carry[hd][0], jnp.max(s, axis=0, keepdims=True)))
        new = []
        for hd in range(MLA_HEADS):
            m_old, acc = carry[hd]
            p = jnp.exp2(s_ref[hd, 0:keys, :] - m_new[hd]).astype(BF16)
            acc = acc * jnp.exp2(m_old - m_new[hd])
            for b in range(nblk):
                vt = vt_ref[j + b, hd * MLA_V_ROWS:(hd + 1) * MLA_V_ROWS, :]
                acc = acc + jnp.dot(vt, p[b * blk:(b + 1) * blk], preferred_element_type=F32)
            new.append((m_new[hd], acc))
        return tuple(new)

    neg = jnp.full((1, blk), -jnp.inf, F32)
    carry = tuple((neg, jnp.zeros((MLA_V_ROWS, blk), F32)) for _ in range(MLA_HEADS))
    carry = lax.fori_loop(0, qi // 2, lambda u, c: step(2 * u, c, 2, False), carry)
    carry = lax.fori_loop(0, lax.rem(qi, 2), lambda _, c: step(qi - 1, c, 1, False), carry)
    final = step(qi, carry, 1, True)
    y_t = jnp.concatenate([acc[:MLA_V] / acc[MLA_V:MLA_V + 1] for _, acc in final], axis=0)
    o_ref[...] = _rms(y_t.T, gn_ref[...]).astype(BF16)


def _attention(q, k, v_t, gn, batch, seq):
    t = q.shape[0]
    blk = ATT_BLOCK
    nq = seq // blk
    nqk = MLA_HEADS * HEAD_PAD
    return pl.pallas_call(
        _attn_kernel,
        grid=(batch, nq),
        in_specs=[pl.BlockSpec((blk, nqk), lambda b, i: (b * nq + i, 0)),
                  pl.BlockSpec((seq, nqk), lambda b, i: (b, 0)),
                  pl.BlockSpec((nq, MLA_HEADS * MLA_V_ROWS, blk), lambda b, i: (b, 0, 0)),
                  pl.BlockSpec((1, MLA_INNER), lambda b, i: (0, 0))],
        out_specs=pl.BlockSpec((blk, MLA_INNER), lambda b, i: (b * nq + i, 0)),
        out_shape=jax.ShapeDtypeStruct((t, MLA_INNER), BF16),
        scratch_shapes=[pltpu.VMEM((MLA_HEADS, 2 * blk, blk), F32)],
        compiler_params=_params("arbitrary", "arbitrary"),
        name="attention",
    )(q, k, v_t, gn)


def _outproj_kernel(ys_ref, ym_ref, x_ref, mod_ref, wtop_ref, wbot_ref, gpm_ref, gpf_ref,
                    wrh_ref, wrl_ref, br_ref, x1_ref, h2_ref, col_ref, info_ref, cnt_ref, ce_ref):
    y = (jnp.dot(ys_ref[...], wtop_ref[...], preferred_element_type=F32)
         + jnp.dot(ym_ref[...], wbot_ref[...], preferred_element_type=F32))
    g1 = mod_ref[0, 2:3, :]
    sh2 = mod_ref[0, 3:4, :]
    sc2 = mod_ref[0, 4:5, :]
    x1 = x_ref[...] + g1 * _rms(y, gpm_ref[...])
    x1_ref[...] = x1
    h2 = _rms(x1, gpf_ref[...]) * (1.0 + sc2) + sh2
    tm = h2.shape[0]
    h_hi = h2.astype(BF16)
    h_lo = (h2 - h_hi.astype(F32)).astype(BF16)
    logits = (jnp.dot(h_hi, wrh_ref[...], preferred_element_type=F32)
              + jnp.dot(h_lo, wrh_ref[...], preferred_element_type=F32)
              + jnp.dot(h_hi, wrl_ref[...], preferred_element_type=F32)) + br_ref[...]
    lt = logits.T
    row = lax.broadcasted_iota(jnp.int32, lt.shape, 0).astype(F32)
    ninf = -jnp.inf
    big = 1e9
    is_g = row < MOE_GROUPS
    gl = jnp.where(is_g, lt, ninf)
    gmax = jnp.max(gl, axis=0, keepdims=True)
    gsum = jnp.sum(jnp.where(is_g, jnp.exp(lt - gmax), 0.0), axis=0, keepdims=True)
    g_w = 1.0 / gsum
    g_idx = jnp.min(jnp.where(gl == gmax, row, big), axis=0, keepdims=True)
    first = MOE_GROUPS + MOE_EXPERTS_PER_GROUP * g_idx
    el = jnp.where(row >= first, jnp.where(row < first + MOE_EXPERTS_PER_GROUP, lt, ninf), ninf)
    m1 = jnp.max(el, axis=0, keepdims=True)
    i1 = jnp.min(jnp.where(el == m1, row, big), axis=0, keepdims=True)
    el2 = jnp.where(row == i1, ninf, el)
    m2 = jnp.max(el2, axis=0, keepdims=True)
    i2 = jnp.min(jnp.where(el2 == m2, row, big), axis=0, keepdims=True)
    r = jnp.exp(m2 - m1)
    w1 = g_w / (1.0 + r)
    w2 = g_w * r / (1.0 + r)
    sel1 = row == i1
    sel2 = row == i2
    both = jnp.where(sel1, 1.0, jnp.where(sel2, 1.0, 0.0))
    cnt16 = jnp.floor((jnp.sum(both, axis=1, keepdims=True) + (MOE_CHUNK - 1)) * (1.0 / MOE_CHUNK))
    er = lax.broadcasted_iota(jnp.int32, (LANES, LANES), 0)
    ec = lax.broadcasted_iota(jnp.int32, (LANES, LANES), 1)
    cnt16_b = jnp.broadcast_to(cnt16, (LANES, LANES))
    off16 = jnp.dot((ec < er).astype(BF16), cnt16_b.astype(BF16), preferred_element_type=F32)[:, 0:1]
    earlier = (lax.broadcasted_iota(jnp.int32, (tm, tm), 0)
               < lax.broadcasted_iota(jnp.int32, (tm, tm), 1)).astype(BF16)
    base = off16 * MOE_CHUNK + jnp.dot(both.astype(BF16), earlier, preferred_element_type=F32)
    lpos1 = jnp.sum(jnp.where(sel1, base, 0.0), axis=0, keepdims=True)
    lpos2 = jnp.sum(jnp.where(sel2, base, 0.0), axis=0, keepdims=True)
    h2_ref[...] = h_hi
    col_ref[...] = jnp.where(row == 0, lpos1, jnp.where(row == 1, lpos2, 0.0)).T
    row8 = lax.broadcasted_iota(jnp.int32, info_ref.shape, 0)
    info_ref[...] = jnp.where(row8 == 0, lpos1, jnp.where(row8 == 1, lpos2,
                              jnp.where(row8 == 2, w1, jnp.where(row8 == 3, w2, 0.0))))
    cnt_ref[...] = cnt16_b
    chunk = lax.broadcasted_iota(jnp.int32, (LANES, LANES), 1).astype(F32)
    ce = jnp.sum(jnp.where(off16 + cnt16 <= chunk, 1.0, 0.0), axis=0, keepdims=True) - MOE_GROUPS
    ce_ref[...] = jnp.broadcast_to(ce, ce_ref.shape)


def _outproj(y_ssd, y_mla, x2, mod3, w_top, w_bot, gpm, gpf, wr_hi, wr_lo, b_router, seq, tm):
    t, d = x2.shape
    per_b = seq // tm
    n_tok_tiles = t // tm
    row = lambda i: (i, 0)
    const = lambda i: (0, 0)
    return pl.pallas_call(
        _outproj_kernel,
        grid=(n_tok_tiles,),
        in_specs=[pl.BlockSpec((tm, SSD_INNER), row),
                  pl.BlockSpec((tm, MLA_INNER), row),
                  pl.BlockSpec((tm, d), row),
                  pl.BlockSpec((1, N_MOD, d), lambda i: (i // per_b, 0, 0)),
                  pl.BlockSpec(w_top.shape, const),
                  pl.BlockSpec(w_bot.shape, const),
                  pl.BlockSpec((1, d), const),
                  pl.BlockSpec((1, d), const),
                  pl.BlockSpec(wr_hi.shape, const),
                  pl.BlockSpec(wr_lo.shape, const),
                  pl.BlockSpec((1, LANES), const)],
        out_specs=[pl.BlockSpec((tm, d), row),
                   pl.BlockSpec((tm, d), row),
                   pl.BlockSpec((tm, LANES), row),
                   pl.BlockSpec((8, tm), row),
                   pl.BlockSpec((LANES, LANES), row),
                   pl.BlockSpec((8, LANES), row)],
        out_shape=[jax.ShapeDtypeStruct((t, d), F32),
                   jax.ShapeDtypeStruct((t, d), BF16),
                   jax.ShapeDtypeStruct((t, LANES), F32),
                   jax.ShapeDtypeStruct((n_tok_tiles * 8, tm), F32),
                   jax.ShapeDtypeStruct((n_tok_tiles * LANES, LANES), F32),
                   jax.ShapeDtypeStruct((n_tok_tiles * 8, LANES), F32)],
        compiler_params=_params("arbitrary"),
        name="outproj_router",
    )(y_ssd, y_mla, x2, mod3, w_top, w_bot, gpm, gpf, wr_hi, wr_lo, b_router)


def _moe_rows(n_tok_tiles):
    rows = n_tok_tiles * MOE_SLOTS + MOE_EXPERTS * (MOE_TILE - MOE_CHUNK)
    return (rows + MOE_TILE - 1) // MOE_TILE * MOE_TILE


def _moe_plan(cnt_tiles, chunk_expert, n_tok_tiles, n_rows):
    chunks_per_tile = MOE_SLOTS // MOE_CHUNK
    chunks_per_rows = MOE_TILE // MOE_CHUNK
    cnt = cnt_tiles.reshape(n_tok_tiles, LANES, LANES)[:, MOE_GROUPS:MOE_GROUPS + MOE_EXPERTS, 0]
    cnt = cnt.astype(jnp.int32)
    ce = chunk_expert.reshape(n_tok_tiles, 8, LANES)[:, 0, :chunks_per_tile].astype(jnp.int32)
    used = jnp.sum(cnt, axis=1)
    total = jnp.sum(cnt, axis=0)
    padded = (total + chunks_per_rows - 1) // chunks_per_rows * chunks_per_rows
    e_end = jnp.cumsum(padded)
    e_start = e_end - padded
    run_global = e_start[None, :] + jnp.cumsum(cnt, axis=0) - cnt
    run_local = jnp.cumsum(cnt, axis=1) - cnt
    experts = jnp.arange(MOE_EXPERTS, dtype=jnp.int32)
    shift = jnp.sum(jnp.where(ce[:, :, None] == experts, (run_global - run_local)[:, None, :], 0), axis=-1)
    local = jnp.arange(chunks_per_tile, dtype=jnp.int32)[None, :]
    dst = jnp.where(local < used[:, None], shift + local, 0)
    n_row_tiles = n_rows // MOE_TILE
    n_used = e_end[-1] // chunks_per_rows
    j = jnp.minimum(jnp.arange(n_row_tiles, dtype=jnp.int32), jnp.maximum(n_used - 1, 0))
    tile_expert = jnp.sum((e_end[None, :] // chunks_per_rows <= j[:, None]).astype(jnp.int32), axis=1)
    tile_expert = jnp.minimum(tile_expert, MOE_EXPERTS - 1)
    i32 = lambda v: v.astype(jnp.int32)
    return (i32(dst.reshape(-1)), i32(used), i32(e_start + total), i32(padded - total),
            i32(n_used.reshape(1)), i32(tile_expert))


def _chunk_rows(chunk):
    return pl.ds(pl.multiple_of(chunk * MOE_CHUNK, MOE_CHUNK), MOE_CHUNK)


def _run_copies(*groups):
    for wait in (False, True):
        for lo, hi, make_copy in groups:
            def body(k, carry, wait=wait, make_copy=make_copy):
                cp = make_copy(k)
                cp.wait() if wait else cp.start()
                return carry
            lax.fori_loop(lo, hi, body, 0)


def _scatter_chunks_kernel(dst_ref, used_ref, pad_start_ref, pad_n_ref, nused_ref, h2_ref, info_ref,
                           o_ref, wslot_ref, src_ref, zero_ref, sem):
    i = pl.program_id(0)
    chunks = MOE_SLOTS // MOE_CHUNK
    slot = lax.rem(i, 2)

    info = info_ref[...]
    srow = lax.broadcasted_iota(jnp.int32, (MOE_SLOTS, info.shape[1]), 0)
    hit1 = srow == info[0:1, :].astype(jnp.int32)
    hit2 = srow == info[1:2, :].astype(jnp.int32)
    perm = jnp.where(hit1, 1.0, jnp.where(hit2, 1.0, 0.0)).astype(BF16)
    src_ref[slot] = jnp.dot(perm, h2_ref[...], preferred_element_type=F32).astype(BF16)
    wslot = jnp.sum(jnp.where(hit1, info[2:3, :], jnp.where(hit2, info[3:4, :], 0.0)), axis=1, keepdims=True)
    wslot_ref[...] = jnp.broadcast_to(wslot, wslot_ref.shape)

    @pl.when(i == 0)
    def _():
        zero_ref[...] = jnp.zeros_like(zero_ref)
        pads = [(0, pad_n_ref[e], lambda c, e=e: pltpu.make_async_copy(
            zero_ref.at[pl.ds(0, MOE_CHUNK), :], o_ref.at[_chunk_rows(pad_start_ref[e] + c), :], sem.at[2]))
            for e in range(MOE_EXPERTS)]
        tail = (nused_ref[0], o_ref.shape[0] // MOE_TILE, lambda j: pltpu.make_async_copy(
            zero_ref, o_ref.at[pl.ds(pl.multiple_of(j * MOE_TILE, MOE_TILE), MOE_TILE), :], sem.at[2]))
        _run_copies(tail, *pads)

    def chunk_copies(tile, buf, wait):
        def body(c, carry):
            cp = pltpu.make_async_copy(src_ref.at[buf, _chunk_rows(c), :],
                                       o_ref.at[_chunk_rows(dst_ref[tile * chunks + c]), :], sem.at[buf])
            cp.wait() if wait else cp.start()
            return carry
        lax.fori_loop(0, used_ref[tile], body, 0)

    chunk_copies(i, slot, False)

    @pl.when(i > 0)
    def _():
        chunk_copies(i - 1, 1 - slot, True)

    @pl.when(i == pl.num_programs(0) - 1)
    def _():
        chunk_copies(i, slot, True)


def _scatter_chunks(h2, info, dst, used, pad_start, pad_n, n_used, n_rows):
    t, d = h2.shape
    tm = info.shape[1]
    row = lambda i, *_: (i, 0)
    return pl.pallas_call(
        _scatter_chunks_kernel,
        grid_spec=pltpu.PrefetchScalarGridSpec(
            num_scalar_prefetch=5,
            grid=(used.shape[0],),
            in_specs=[pl.BlockSpec((tm, d), row), pl.BlockSpec((8, tm), row)],
            out_specs=[pl.BlockSpec(memory_space=pl.ANY), pl.BlockSpec((MOE_SLOTS, LANES), row)],
            scratch_shapes=[pltpu.VMEM((2, MOE_SLOTS, d), BF16), pltpu.VMEM((MOE_TILE, d), BF16),
                            pltpu.SemaphoreType.DMA((3,))]),
        out_shape=[jax.ShapeDtypeStruct((n_rows, d), BF16),
                   jax.ShapeDtypeStruct((used.shape[0] * MOE_SLOTS, LANES), F32)],
        compiler_params=_params("arbitrary"),
        name="moe_scatter_chunks",
    )(dst, used, pad_start, pad_n, n_used, h2, info)


def _experts_kernel(te_ref, nt_ref, xs_ref, wg_ref, wu_ref, wd_ref, o_ref):
    used = pl.program_id(0) < nt_ref[0]

    @pl.when(used)
    def _():
        x = xs_ref[...]
        gate = jnp.dot(x, wg_ref[0].astype(BF16), preferred_element_type=F32)
        up = jnp.dot(x, wu_ref[0].astype(BF16), preferred_element_type=F32)
        hid = (_silu(gate) * up).astype(BF16)
        o_ref[...] = jnp.dot(hid, wd_ref[0].astype(BF16), preferred_element_type=F32).astype(BF16)

    @pl.when(jnp.logical_not(used))
    def _():
        o_ref[...] = jnp.zeros_like(o_ref)


def _experts(xs, tile_expert, n_used, w_gate, w_up, w_down):
    n_rows, d = xs.shape
    tile = MOE_TILE
    by_expert = lambda j, te, nt: (te[j], 0, 0)
    return pl.pallas_call(
        _experts_kernel,
        grid_spec=pltpu.PrefetchScalarGridSpec(
            num_scalar_prefetch=2,
            grid=(n_rows // tile,),
            in_specs=[pl.BlockSpec((tile, d), lambda j, te, nt: (jnp.maximum(jnp.minimum(j, nt[0] - 1), 0), 0)),
                      pl.BlockSpec((1, d, MOE_FF), by_expert),
                      pl.BlockSpec((1, d, MOE_FF), by_expert),
                      pl.BlockSpec((1, MOE_FF, d), by_expert)],
            out_specs=pl.BlockSpec((tile, d), lambda j, te, nt: (j, 0))),
        out_shape=jax.ShapeDtypeStruct((n_rows, d), BF16),
        compiler_params=_params("arbitrary"),
        name="moe_experts",
    )(tile_expert, n_used, xs, w_gate, w_up, w_down)


def _combine_kernel(dst_ref, used_ref, wslot_ref, col_ref, x1_ref, mod_ref, gpost_ref, ys_ref, o_ref,
                    ysl_ref, sem):
    i = pl.program_id(0)
    chunks = MOE_SLOTS // MOE_CHUNK
    slot = lax.rem(i, 2)

    def fetch(tile, buf, wait):
        base = tile * chunks

        def copy(c, carry):
            cp = pltpu.make_async_copy(ys_ref.at[_chunk_rows(dst_ref[base + c]), :],
                                       ysl_ref.at[buf, _chunk_rows(c), :], sem.at[buf])
            cp.wait() if wait else cp.start()
            return carry

        lax.fori_loop(0, chunks, copy, 0, unroll=8)

    @pl.when(i == 0)
    def _():
        fetch(0, 0, False)

    @pl.when(i + 1 < pl.num_programs(0))
    def _():
        fetch(i + 1, 1 - slot, False)

    col = col_ref[...]
    tm = col.shape[0]
    lane = lax.broadcasted_iota(jnp.int32, (tm, MOE_SLOTS), 1)
    hit = jnp.where(lane == col[:, 0:1].astype(jnp.int32), 1.0,
                    jnp.where(lane == col[:, 1:2].astype(jnp.int32), 1.0, 0.0)).astype(BF16)
    fetch(i, slot, True)
    in_use = lax.broadcasted_iota(jnp.int32, (MOE_SLOTS, 1), 0) < used_ref[i] * MOE_CHUNK
    ysw = jnp.where(in_use, ysl_ref[slot].astype(F32) * wslot_ref[:, 0:1], 0.0).astype(BF16)
    y = jnp.dot(hit, ysw, preferred_element_type=F32)
    g2 = mod_ref[0, 5:6, :]
    o_ref[...] = x1_ref[...] + g2 * _rms(y, gpost_ref[...])


def _combine(ys, dst, used, wslot, col, x1, mod3, gpost, seq, tm):
    t, d = x1.shape
    per_b = seq // tm
    row = lambda i, *_: (i, 0)
    return pl.pallas_call(
        _combine_kernel,
        grid_spec=pltpu.PrefetchScalarGridSpec(
            num_scalar_prefetch=2,
            grid=(t // tm,),
            in_specs=[pl.BlockSpec((MOE_SLOTS, LANES), row),
                      pl.BlockSpec((tm, LANES), row),
                      pl.BlockSpec((tm, d), row),
                      pl.BlockSpec((1, N_MOD, d), lambda i, *_: (i // per_b, 0, 0)),
                      pl.BlockSpec((1, d), lambda i, *_: (0, 0)),
                      pl.BlockSpec(memory_space=pl.ANY)],
            out_specs=pl.BlockSpec((tm, d), row),
            scratch_shapes=[pltpu.VMEM((2, MOE_SLOTS, d), BF16), pltpu.SemaphoreType.DMA((2,))]),
        out_shape=jax.ShapeDtypeStruct((t, d), F32),
        compiler_params=_params("arbitrary"),
        name="moe_combine",
    )(dst, used, wslot, col, x1, mod3, gpost, ys)


def kernel(x, c, positions, ada_w, ada_b, pre_norm_mix, post_norm_mix, pre_norm_ffn, post_norm_ffn, w_in, conv_w, conv_b, dt_bias, a_log, d_skip, ssd_norm, q_norm, w_uq, kv_norm, w_ukv, mla_out_norm, w_out, w_group_router, b_group_router, w_expert_router, b_expert_router, w_gate, w_up, w_down):
    batch, seq, d = x.shape
    t = batch * seq
    depth = ada_w.shape[0]
    tm = min(INPROJ_TOKENS, seq)
    cos_t, sin_t = _rope_tables(positions)

    x2 = x.reshape(t, d)
    for l in range(depth):
        mod3 = _modulation(c, ada_w[l], ada_b[l]).reshape(batch, N_MOD, d)
        w_in_r, wq2, wk, wv_t = _inproj_weights(w_in[l], w_uq[l], w_ukv[l])
        z, xbc, dt_raw, q, k, v_t = _inproj(
            x2, mod3, pre_norm_mix[l].reshape(1, d), w_in_r, q_norm[l].reshape(1, -1), wq2,
            kv_norm[l].reshape(1, -1), wk, wv_t, cos_t, sin_t, seq, tm)
        y_ssd = _ssd(xbc, z, dt_raw, conv_w[l], conv_b[l], dt_bias[l], a_log[l], d_skip[l],
                     ssd_norm[l], batch, seq)
        y_mla = _attention(q, k, v_t, mla_out_norm[l].reshape(1, -1), batch, seq)
        w_o = w_out[l].astype(BF16)
        pad_r = LANES - MOE_GROUPS - MOE_EXPERTS
        w_router = jnp.concatenate([w_group_router[l], w_expert_router[l], jnp.zeros((d, pad_r), F32)], axis=1)
        b_router = jnp.concatenate([b_group_router[l].reshape(-1), b_expert_router[l].reshape(-1),
                                    jnp.zeros((pad_r,), F32)]).reshape(1, LANES)
        wr_hi = w_router.astype(BF16)
        wr_lo = (w_router - wr_hi.astype(F32)).astype(BF16)
        x1, h2, col, info, cnt_tiles, chunk_expert = _outproj(
            y_ssd, y_mla, x2, mod3, w_o[:SSD_INNER], w_o[SSD_INNER:], post_norm_mix[l].reshape(1, d),
            pre_norm_ffn[l].reshape(1, d), wr_hi, wr_lo, b_router, seq, MOE_TOKENS)
        n_tok_tiles = t // MOE_TOKENS
        n_rows = _moe_rows(n_tok_tiles)
        dst, used, pad_start, pad_n, n_used, tile_expert = _moe_plan(cnt_tiles, chunk_expert, n_tok_tiles, n_rows)
        xs, wslot = _scatter_chunks(h2, info, dst, used, pad_start, pad_n, n_used, n_rows)
        ys = _experts(xs, tile_expert, n_used, w_gate[l], w_up[l], w_down[l])
        x2 = _combine(ys, dst, used, wslot, col, x1, mod3, post_norm_ffn[l].reshape(1, d), seq, MOE_TOKENS)
    return x2.reshape(batch, seq, d)
```

```python
import functools
import math

import jax
import jax.numpy as jnp
from jax import lax
from jax.experimental import pallas as pl
from jax.experimental.pallas import tpu as pltpu

F32 = jnp.float32
BF16 = jnp.bfloat16

D_MODEL = 1024
SSD_HEADS = 8
SSD_HEAD_DIM = 64
SSD_INNER = SSD_HEADS * SSD_HEAD_DIM
SSD_GROUPS = 2
SSD_STATE = 128
SSD_CONV = 4
SSD_CHUNK = 128
SSD_XBC = SSD_INNER + 2 * SSD_GROUPS * SSD_STATE
MLA_HEADS = 8
MLA_NOPE = 64
MLA_ROPE = 32
MLA_QK = MLA_NOPE + MLA_ROPE
MLA_V = 64
MLA_Q_RANK = 256
MLA_KV_RANK = 128
MLA_INNER = MLA_HEADS * MLA_V
ROPE_THETA = 10000.0
MOE_GROUPS = 4
MOE_EXPERTS_PER_GROUP = 8
MOE_EXPERTS = MOE_GROUPS * MOE_EXPERTS_PER_GROUP
MOE_FF = 256
N_MOD = 6
EPS = 1e-6
LOG2_E = math.log2(math.e)

LANES = 128
HEAD_PAD = 128
ATT_BLOCK = 256
MLA_V_ROWS = MLA_V + 16
INPROJ_TOKENS = 512
SSD_STEP_ROWS = 8 * SSD_CHUNK
MOE_TOKENS = 512
MOE_CHUNK = 16
MOE_TILE = 512
MOE_SLOTS = 2 * MOE_TOKENS + MOE_EXPERTS * MOE_CHUNK
VMEM_LIMIT = 48 * 1024 * 1024

_C_Z = 0
_C_XBC = _C_Z + SSD_INNER
_C_CQ = _C_XBC + SSD_XBC
_C_CKV = _C_CQ + MLA_Q_RANK
_C_KR = _C_CKV + MLA_KV_RANK
_C_KRS = _C_KR + LANES
_C_DT = _C_KRS + LANES
_C_END = _C_DT + LANES


def _silu(v):
    return v * (1.0 / (1.0 + jnp.exp(-v)))


def _rms(v, gain):
    return v * lax.rsqrt(jnp.mean(v * v, axis=-1, keepdims=True) + EPS) * gain


def _params(*sem, flags=None):
    return pltpu.CompilerParams(dimension_semantics=sem, vmem_limit_bytes=VMEM_LIMIT, flags=flags)


def _mod_kernel(c_ref, w_ref, b_ref, o_ref):
    act = _silu(c_ref[...])
    a_hi = act.astype(BF16)
    a_lo = (act - a_hi.astype(F32)).astype(BF16)
    w = w_ref[...]
    w_hi = w.astype(BF16)
    w_lo = (w - w_hi.astype(F32)).astype(BF16)
    o_ref[...] = (jnp.dot(a_hi, w_hi, preferred_element_type=F32)
                  + jnp.dot(a_lo, w_hi, preferred_element_type=F32)
                  + jnp.dot(a_hi, w_lo, preferred_element_type=F32)) + b_ref[...]


def _modulation(c, ada_w, ada_b):
    b, d = c.shape
    n = ada_w.shape[1]
    return pl.pallas_call(
        _mod_kernel,
        grid=(n // d,),
        in_specs=[pl.BlockSpec((b, d), lambda j: (0, 0)),
                  pl.BlockSpec((d, d), lambda j: (0, j)),
                  pl.BlockSpec((1, d), lambda j: (0, j))],
        out_specs=pl.BlockSpec((b, d), lambda j: (0, j)),
        out_shape=jax.ShapeDtypeStruct((b, n), F32),
        compiler_params=_params("arbitrary"),
        name="modulation",
    )(c, ada_w, ada_b.reshape(1, n))


def _rope_kernel(pos_ref, inv_ref, cos_ref, sin_ref):
    ang = pos_ref[...].astype(F32) * inv_ref[...]
    blk = ang.shape[0]
    half = MLA_ROPE // 2
    per_row = LANES // half
    lane = lax.broadcasted_iota(jnp.int32, (blk * per_row, LANES), 1)
    first = (lane >= MLA_NOPE) & (lane < MLA_NOPE + half)
    second = (lane >= MLA_NOPE + half) & (lane < MLA_QK)
    for fn, out_ref, fill in ((jnp.cos, cos_ref, 1.0), (jnp.sin, sin_ref, 0.0)):
        compact = fn(ang)
        rep = jnp.broadcast_to(compact[:, None, :], (blk, per_row, LANES)).reshape(blk * per_row, LANES)
        a = pltpu.roll(rep, 0, 1, stride=half, stride_axis=0)
        b = pltpu.roll(a, half, 1)
        out_ref[...] = jnp.where(first, a, jnp.where(second, b, fill))


def _rope_tables(positions):
    t = positions.size
    half = MLA_ROPE // 2
    per_row = LANES // half
    inv = 1.0 / (ROPE_THETA ** (jnp.arange(0, MLA_ROPE, 2, dtype=F32) / MLA_ROPE))
    group_token = (MLA_NOPE // half - jnp.arange(per_row)) % per_row
    pos_rep = jnp.repeat(positions.reshape(t // per_row, per_row)[:, group_token], half, axis=1)
    inv_t = jnp.tile(inv, per_row).reshape(1, LANES)
    rows = t // per_row
    blk = min(rows, 512)
    return pl.pallas_call(
        _rope_kernel,
        grid=(rows // blk,),
        in_specs=[pl.BlockSpec((blk, LANES), lambda i: (i, 0)),
                  pl.BlockSpec((1, LANES), lambda i: (0, 0))],
        out_specs=[pl.BlockSpec((blk * per_row, LANES), lambda i: (i, 0))] * 2,
        out_shape=[jax.ShapeDtypeStruct((t, LANES), F32)] * 2,
        compiler_params=_params("arbitrary"),
        name="rope_tables",
    )(pos_rep, inv_t)


def _inproj_kernel(x_ref, mod_ref, gpre_ref, wmain_ref, wrest_ref, qn_ref, wq_ref, kvn_ref, wk_ref,
                   wvt_ref, cos_ref, sin_ref, z_ref, xbc_ref, dt_ref, q_ref, k_ref, vt_ref):
    x = x_ref[...]
    sh = mod_ref[0, 0:1, :]
    sc = mod_ref[0, 1:2, :]
    h = (_rms(x, gpre_ref[...]) * (1.0 + sc) + sh).astype(BF16)
    u = jnp.concatenate([jnp.dot(h, wmain_ref[...], preferred_element_type=F32),
                         jnp.dot(h, wrest_ref[...], preferred_element_type=F32)], axis=1)
    z_ref[...] = u[:, _C_Z:_C_XBC].astype(BF16)
    xbc_ref[...] = u[:, _C_XBC:_C_CQ].astype(BF16)
    dt_ref[...] = u[:, _C_DT:_C_END]
    cos_t = cos_ref[...]
    sin_t = sin_ref[...]
    cq = _rms(u[:, _C_CQ:_C_CKV], qn_ref[...]).astype(BF16)
    q2 = jnp.dot(cq, wq_ref[...], preferred_element_type=F32)
    scale = MLA_QK ** -0.5 * LOG2_E
    nq = MLA_HEADS * HEAD_PAD
    for hd in range(MLA_HEADS):
        a = q2[:, hd * HEAD_PAD:(hd + 1) * HEAD_PAD]
        b = q2[:, nq + hd * HEAD_PAD:nq + (hd + 1) * HEAD_PAD]
        q_ref[:, hd * HEAD_PAD:(hd + 1) * HEAD_PAD] = ((a * cos_t + b * sin_t) * scale).astype(BF16)
    ckv = _rms(u[:, _C_CKV:_C_KR], kvn_ref[...]).astype(BF16)
    kn = jnp.dot(ckv, wk_ref[...], preferred_element_type=F32)
    k_pe = u[:, _C_KR:_C_KRS] * cos_t + u[:, _C_KRS:_C_DT] * sin_t
    for hd in range(MLA_HEADS):
        k_ref[:, hd * HEAD_PAD:(hd + 1) * HEAD_PAD] = (
            kn[:, hd * HEAD_PAD:(hd + 1) * HEAD_PAD] + k_pe).astype(BF16)
    v_t = lax.dot_general(wvt_ref[...], ckv, (((1,), (1,)), ((), ())), preferred_element_type=F32)
    head_row = lax.rem(lax.broadcasted_iota(jnp.int32, (v_t.shape[0], 1), 0), MLA_V_ROWS)
    v_t = v_t + jnp.where(head_row == MLA_V, 1.0, 0.0)
    for s in range(vt_ref.shape[0]):
        vt_ref[s] = v_t[:, s * ATT_BLOCK:(s + 1) * ATT_BLOCK].astype(BF16)


def _inproj_weights(w_in, w_uq, w_ukv):
    d = w_in.shape[0]
    half = MLA_ROPE // 2
    o_z, o_xbc, o_dt = 0, SSD_INNER, SSD_INNER + SSD_XBC
    o_cq = o_dt + SSD_HEADS
    o_ckv = o_cq + MLA_Q_RANK
    o_kr = o_ckv + MLA_KV_RANK
    zeros = lambda n: jnp.zeros((d, n), F32)
    kr = w_in[:, o_kr:o_kr + MLA_ROPE]
    kr_blk = jnp.concatenate([zeros(MLA_NOPE), kr, zeros(HEAD_PAD - MLA_QK)], axis=1)
    krs_blk = jnp.concatenate([zeros(MLA_NOPE), -kr[:, half:], kr[:, :half], zeros(HEAD_PAD - MLA_QK)], axis=1)
    dt_blk = jnp.concatenate([w_in[:, o_dt:o_dt + SSD_HEADS], zeros(LANES - SSD_HEADS)], axis=1)
    w_main = w_in[:, o_z:o_dt]
    w_rest = jnp.concatenate([w_in[:, o_cq:o_kr], kr_blk, krs_blk, dt_blk], axis=1)
    r = w_uq.shape[0]
    zq = jnp.zeros((r, MLA_HEADS, HEAD_PAD - MLA_QK), F32)
    zn = jnp.zeros((r, MLA_HEADS, MLA_NOPE), F32)
    wq_plain = jnp.concatenate([w_uq, zq], axis=2).reshape(r, MLA_HEADS * HEAD_PAD)
    wq_rot = jnp.concatenate([zn, -w_uq[:, :, MLA_NOPE + half:], w_uq[:, :, MLA_NOPE:MLA_NOPE + half], zq],
                             axis=2).reshape(r, MLA_HEADS * HEAD_PAD)
    wq2 = jnp.concatenate([wq_plain, wq_rot], axis=1)
    rk = w_ukv.shape[0]
    zk = jnp.zeros((rk, MLA_HEADS, HEAD_PAD - MLA_NOPE), F32)
    wk = jnp.concatenate([w_ukv[:, :, :MLA_NOPE], zk], axis=2).reshape(rk, MLA_HEADS * HEAD_PAD)
    zv = jnp.zeros((rk, MLA_HEADS, MLA_V_ROWS - MLA_V), F32)
    wv_t = jnp.concatenate([w_ukv[:, :, MLA_NOPE:], zv], axis=2).reshape(rk, MLA_HEADS * MLA_V_ROWS).T
    return w_main.astype(BF16), w_rest.astype(BF16), wq2.astype(BF16), wk.astype(BF16), wv_t.astype(BF16)


def _inproj(x2, mod3, gpre, w_main, w_rest, q_norm, wq2, kv_norm, wk, wv_t, cos_t, sin_t, seq, tm):
    t, d = x2.shape
    per_b = seq // tm
    row = lambda i: (i, 0)
    const = lambda i: (0, 0)
    nqk = MLA_HEADS * HEAD_PAD
    slabs = tm // ATT_BLOCK
    return pl.pallas_call(
        _inproj_kernel,
        grid=(t // tm,),
        in_specs=[pl.BlockSpec((tm, d), row),
                  pl.BlockSpec((1, N_MOD, d), lambda i: (i // per_b, 0, 0)),
                  pl.BlockSpec((1, d), const),
                  pl.BlockSpec(w_main.shape, const),
                  pl.BlockSpec(w_rest.shape, const),
                  pl.BlockSpec((1, MLA_Q_RANK), const),
                  pl.BlockSpec(wq2.shape, const),
                  pl.BlockSpec((1, MLA_KV_RANK), const),
                  pl.BlockSpec(wk.shape, const),
                  pl.BlockSpec(wv_t.shape, const),
                  pl.BlockSpec((tm, LANES), row),
                  pl.BlockSpec((tm, LANES), row)],
        out_specs=[pl.BlockSpec((tm, SSD_INNER), row),
                   pl.BlockSpec((tm, SSD_XBC), row),
                   pl.BlockSpec((tm, LANES), row),
                   pl.BlockSpec((tm, nqk), row),
                   pl.BlockSpec((tm, nqk), row),
                   pl.BlockSpec((slabs, MLA_HEADS * MLA_V_ROWS, ATT_BLOCK), lambda i: (i, 0, 0))],
        out_shape=[jax.ShapeDtypeStruct((t, SSD_INNER), BF16),
                   jax.ShapeDtypeStruct((t, SSD_XBC), BF16),
                   jax.ShapeDtypeStruct((t, LANES), F32),
                   jax.ShapeDtypeStruct((t, nqk), BF16),
                   jax.ShapeDtypeStruct((t, nqk), BF16),
                   jax.ShapeDtypeStruct((t // ATT_BLOCK, MLA_HEADS * MLA_V_ROWS, ATT_BLOCK), BF16)],
        compiler_params=_params("arbitrary"),
        name="inproj",
    )(x2, mod3, gpre, w_main, w_rest, q_norm, wq2, kv_norm, wk, wv_t, cos_t, sin_t)


def _split3_packed(v):
    lane = lax.broadcasted_iota(jnp.int32, v.shape, 1)
    v = jnp.where(lane < SSD_HEADS, v, 0.0)
    hi = v.astype(BF16).astype(F32)
    rest = v - hi
    mid = rest.astype(BF16).astype(F32)
    lo = rest - mid
    return (hi + pltpu.roll(mid, SSD_HEADS, 1) + pltpu.roll(lo, 2 * SSD_HEADS, 1)).astype(BF16)


def _ssd_kernel(xbc_ref, z_ref, dt_ref, cw_ref, cb_ref, dtb_ref, alog_ref, dsk_ref, gn_ref,
                e_ref, y_ref, tail_ref, state_ref):
    @pl.when(pl.program_id(1) == 0)
    def _():
        tail_ref[...] = jnp.zeros_like(tail_ref)
        state_ref[...] = jnp.zeros_like(state_ref)

    for c in range(xbc_ref.shape[0] // SSD_CHUNK):
        _ssd_chunk(pl.ds(c * SSD_CHUNK, SSD_CHUNK), xbc_ref, z_ref, dt_ref, cw_ref, cb_ref, dtb_ref,
                   alog_ref, dsk_ref, gn_ref, e_ref, y_ref, tail_ref, state_ref)


def _ssd_chunk(rows, xbc_ref, z_ref, dt_ref, cw_ref, cb_ref, dtb_ref, alog_ref, dsk_ref, gn_ref,
               e_ref, y_ref, tail_ref, state_ref):
    L = SSD_CHUNK
    gw = SSD_INNER // SSD_GROUPS

    cur_b = xbc_ref[rows, :]
    ext = jnp.concatenate([tail_ref[...], cur_b], axis=0)
    out_row = lax.broadcasted_iota(jnp.int32, ((SSD_CONV - 1) * L, ext.shape[0]), 0)
    src_row = lax.broadcasted_iota(jnp.int32, ((SSD_CONV - 1) * L, ext.shape[0]), 1)
    tail_rows = ext.shape[0] - L
    shift = jnp.where(src_row + (out_row // L + 1) == lax.rem(out_row, L) + tail_rows, 1.0, 0.0)
    shifted = jnp.dot(shift.astype(BF16), ext, preferred_element_type=F32)
    acc = cur_b.astype(F32) * cw_ref[SSD_CONV - 1:SSD_CONV, :] + cb_ref[...]
    for j in range(1, SSD_CONV):
        acc = acc + shifted[(j - 1) * L:j * L] * cw_ref[SSD_CONV - 1 - j:SSD_CONV - j, :]
    tail_ref[...] = cur_b[L - tail_rows:L]
    act = _silu(acc)
    xs = act[:, :SSD_INNER]
    bm = act[:, SSD_INNER:SSD_INNER + SSD_GROUPS * SSD_STATE].astype(BF16)
    cm = act[:, SSD_INNER + SSD_GROUPS * SSD_STATE:].astype(BF16)

    dt_in = dt_ref[rows, :] + dtb_ref[...]
    dt = jnp.maximum(dt_in, 0.0) + jnp.log(1.0 + jnp.exp(-jnp.abs(dt_in)))
    adt = dt * (-jnp.exp(alog_ref[...]))
    ri = lax.broadcasted_iota(jnp.int32, (L, L), 0)
    ci = lax.broadcasted_iota(jnp.int32, (L, L), 1)
    causal = ci <= ri
    acs_p = jnp.dot(causal.astype(BF16), _split3_packed(adt), preferred_element_type=F32)
    a_cs = (acs_p + pltpu.roll(acs_p, LANES - SSD_HEADS, 1)
            + pltpu.roll(acs_p, LANES - 2 * SSD_HEADS, 1))
    a_cs_t = a_cs.T
    cs_all = jnp.dot(_split3_packed(a_cs), e_ref[...], preferred_element_type=F32)
    cs64 = cs_all[:, :SSD_INNER]
    cs128 = cs_all[:, SSD_INNER:]
    dt64 = jnp.dot(_split3_packed(dt), e_ref[:, :SSD_INNER], preferred_element_type=F32)

    xd = xs * dt64
    xd_b = xd.astype(BF16)
    last = cs64[L - 1:L, :]
    xdw = (xd * jnp.exp(last - cs64)).astype(BF16)
    chunk_decay = jnp.exp(last)
    in_decay = jnp.exp(cs64)

    lane = lax.broadcasted_iota(jnp.int32, (L, LANES), 1)
    lo = lane < SSD_HEAD_DIM
    zero_b = jnp.zeros((L, LANES), BF16)
    y_parts = []
    new_states = []
    for g in range(SSD_GROUPS):
        bg = bm[:, g * SSD_STATE:(g + 1) * SSD_STATE]
        cg = cm[:, g * SSD_STATE:(g + 1) * SSD_STATE]
        cb = lax.dot_general(cg, bg, (((1,), (1,)), ((), ())), preferred_element_type=F32)
        prev = state_ref[:, g * gw:(g + 1) * gw]
        y_off = jnp.dot(cg, prev.astype(BF16), preferred_element_type=F32) * in_decay[:, g * gw:(g + 1) * gw]
        st = lax.dot_general(bg, xdw[:, g * gw:(g + 1) * gw], (((0,), (0,)), ((), ())),
                             preferred_element_type=F32)
        new_states.append(prev * chunk_decay[:, g * gw:(g + 1) * gw] + st)
        heads_per_group = SSD_HEADS // SSD_GROUPS
        for pair in range(heads_per_group // 2):
            h0 = g * heads_per_group + 2 * pair
            blk = xd_b[:, h0 * SSD_HEAD_DIM:(h0 + 2) * SSD_HEAD_DIM]
            y_pair = None
            for k in range(2):
                hh = h0 + k
                diff = cs128[:, hh * LANES:(hh + 1) * LANES] - a_cs_t[hh:hh + 1, :]
                decay = jnp.where(causal, jnp.exp(diff), 0.0)
                m = (cb * decay).astype(BF16)
                rhs = jnp.where(lo, blk, zero_b) if k == 0 else jnp.where(lo, zero_b, blk)
                part = jnp.dot(m, rhs, preferred_element_type=F32)
                y_pair = part if y_pair is None else y_pair + part
            c0 = (2 * pair) * SSD_HEAD_DIM
            y_parts.append(y_pair + y_off[:, c0:c0 + LANES])
    state_ref[...] = jnp.concatenate(new_states, axis=1)
    y = jnp.concatenate(y_parts, axis=1) + dsk_ref[...] * xs
    gated = y * _silu(z_ref[rows, :].astype(F32))
    y_ref[rows, :] = _rms(gated, gn_ref[...]).astype(BF16)


def _ssd(xbc, z, dt_raw, conv_w, conv_b, dt_bias, a_log, d_skip, ssd_norm, batch, seq):
    t = xbc.shape[0]
    L = min(SSD_STEP_ROWS, seq)
    nc = seq // L
    row = lambda b, c: (b * nc + c, 0)
    const = lambda b, c: (0, 0)
    pad = lambda v: jnp.concatenate([v.reshape(1, -1), jnp.zeros((1, LANES - v.size), F32)], axis=1)
    head_of_lane = jnp.concatenate([jnp.arange(SSD_INNER) // SSD_HEAD_DIM,
                                    jnp.arange(SSD_HEADS * LANES) // LANES])
    rows = jnp.arange(LANES)[:, None]
    expand = ((rows < 3 * SSD_HEADS) & (rows % SSD_HEADS == head_of_lane[None, :])).astype(BF16)
    dsk = jnp.repeat(d_skip.astype(F32), SSD_HEAD_DIM).reshape(1, SSD_INNER)
    return pl.pallas_call(
        _ssd_kernel,
        grid=(batch, nc),
        in_specs=[pl.BlockSpec((L, SSD_XBC), row),
                  pl.BlockSpec((L, SSD_INNER), row),
                  pl.BlockSpec((L, LANES), row),
                  pl.BlockSpec((SSD_CONV, SSD_XBC), const),
                  pl.BlockSpec((1, SSD_XBC), const),
                  pl.BlockSpec((1, LANES), const),
                  pl.BlockSpec((1, LANES), const),
                  pl.BlockSpec((1, SSD_INNER), const),
                  pl.BlockSpec((1, SSD_INNER), const),
                  pl.BlockSpec(expand.shape, const)],
        out_specs=pl.BlockSpec((L, SSD_INNER), row),
        out_shape=jax.ShapeDtypeStruct((t, SSD_INNER), BF16),
        scratch_shapes=[pltpu.VMEM((16, SSD_XBC), BF16),
                        pltpu.VMEM((SSD_STATE, SSD_INNER), F32)],
        compiler_params=_params("arbitrary", "arbitrary"),
        name="ssd",
    )(xbc, z, dt_raw, conv_w, conv_b.reshape(1, -1), pad(dt_bias), pad(a_log), dsk,
      ssd_norm.reshape(1, -1), expand)


def _attn_kernel(q_ref, k_ref, vt_ref, gn_ref, o_ref, s_ref):
    blk = ATT_BLOCK
    qi = pl.program_id(1)
    causal = (lax.broadcasted_iota(jnp.int32, (blk, blk), 0)
              <= lax.broadcasted_iota(jnp.int32, (blk, blk), 1))
    dims = (((1,), (1,)), ((), ()))
    qs = [q_ref[:, hd * HEAD_PAD:(hd + 1) * HEAD_PAD] for hd in range(MLA_HEADS)]

    def step(j, carry, nblk, masked):
        keys = nblk * blk
        rows = pl.ds(pl.multiple_of(j * blk, blk), keys)
        m_new = []
        for hd in range(MLA_HEADS):
            kk = k_ref[rows, hd * HEAD_PAD:(hd + 1) * HEAD_PAD]
            s = lax.dot_general(kk, qs[hd], dims, preferred_element_type=F32)
            if masked:
                s = jnp.where(causal, s, -jnp.inf)
            s_ref[hd, 0:keys, :] = s
            m_new.append(jnp.maximum(carry[hd][0], jnp.max(s, axis=0, keepdims=True)))
        new = []
        for hd in range(MLA_HEADS):
            m_old, acc = carry[hd]
            p = jnp.exp2(s_ref[hd, 0:keys, :] - m_new[hd]).astype(BF16)
            acc = acc * jnp.exp2(m_old - m_new[hd])
            for b in range(nblk):
                vt = vt_ref[j + b, hd * MLA_V_ROWS:(hd + 1) * MLA_V_ROWS, :]
                acc = acc + jnp.dot(vt, p[b * blk:(b + 1) * blk], preferred_element_type=F32)
            new.append((m_new[hd], acc))
        return tuple(new)

    neg = jnp.full((1, blk), -jnp.inf, F32)
    carry = tuple((neg, jnp.zeros((MLA_V_ROWS, blk), F32)) for _ in range(MLA_HEADS))
    carry = lax.fori_loop(0, qi // 2, lambda u, c: step(2 * u, c, 2, False), carry)
    carry = lax.fori_loop(0, lax.rem(qi, 2), lambda _, c: step(qi - 1, c, 1, False), carry)
    final = step(qi, carry, 1, True)
    y_t = jnp.concatenate([acc[:MLA_V] / acc[MLA_V:MLA_V + 1] for _, acc in final], axis=0)
    o_ref[...] = _rms(y_t.T, gn_ref[...]).astype(BF16)


def _attention(q, k, v_t, gn, batch, seq):
    t = q.shape[0]
    blk = ATT_BLOCK
    nq = seq // blk
    nqk = MLA_HEADS * HEAD_PAD
    return pl.pallas_call(
        _attn_kernel,
        grid=(batch, nq),
        in_specs=[pl.BlockSpec((blk, nqk), lambda b, i: (b * nq + i, 0)),
                  pl.BlockSpec((seq, nqk), lambda b, i: (b, 0)),
                  pl.BlockSpec((nq, MLA_HEADS * MLA_V_ROWS, blk), lambda b, i: (b, 0, 0)),
                  pl.BlockSpec((1, MLA_INNER), lambda b, i: (0, 0))],
        out_specs=pl.BlockSpec((blk, MLA_INNER), lambda b, i: (b * nq + i, 0)),
        out_shape=jax.ShapeDtypeStruct((t, MLA_INNER), BF16),
        scratch_shapes=[pltpu.VMEM((MLA_HEADS, 2 * blk, blk), F32)],
        compiler_params=_params("arbitrary", "arbitrary"),
        name="attention",
    )(q, k, v_t, gn)


def _outproj_kernel(ys_ref, ym_ref, x_ref, mod_ref, wtop_ref, wbot_ref, gpm_ref, gpf_ref,
                    wrh_ref, wrl_ref, br_ref, x1_ref, h2_ref, col_ref, info_ref, cnt_ref, ce_ref):
    y = (jnp.dot(ys_ref[...], wtop_ref[...], preferred_element_type=F32)
         + jnp.dot(ym_ref[...], wbot_ref[...], preferred_element_type=F32))
    g1 = mod_ref[0, 2:3, :]
    sh2 = mod_ref[0, 3:4, :]
    sc2 = mod_ref[0, 4:5, :]
    x1 = x_ref[...] + g1 * _rms(y, gpm_ref[...])
    x1_ref[...] = x1
    h2 = _rms(x1, gpf_ref[...]) * (1.0 + sc2) + sh2
    tm = h2.shape[0]
    h_hi = h2.astype(BF16)
    h_lo = (h2 - h_hi.astype(F32)).astype(BF16)
    logits = (jnp.dot(h_hi, wrh_ref[...], preferred_element_type=F32)
              + jnp.dot(h_lo, wrh_ref[...], preferred_element_type=F32)
              + jnp.dot(h_hi, wrl_ref[...], preferred_element_type=F32)) + br_ref[...]
    lt = logits.T
    row = lax.broadcasted_iota(jnp.int32, lt.shape, 0).astype(F32)
    ninf = -jnp.inf
    big = 1e9
    is_g = row < MOE_GROUPS
    gl = jnp.where(is_g, lt, ninf)
    gmax = jnp.max(gl, axis=0, keepdims=True)
    gsum = jnp.sum(jnp.where(is_g, jnp.exp(lt - gmax), 0.0), axis=0, keepdims=True)
    g_w = 1.0 / gsum
    g_idx = jnp.min(jnp.where(gl == gmax, row, big), axis=0, keepdims=True)
    first = MOE_GROUPS + MOE_EXPERTS_PER_GROUP * g_idx
    el = jnp.where(row >= first, jnp.where(row < first + MOE_EXPERTS_PER_GROUP, lt, ninf), ninf)
    m1 = jnp.max(el, axis=0, keepdims=True)
    i1 = jnp.min(jnp.where(el == m1, row, big), axis=0, keepdims=True)
    el2 = jnp.where(row == i1, ninf, el)
    m2 = jnp.max(el2, axis=0, keepdims=True)
    i2 = jnp.min(jnp.where(el2 == m2, row, big), axis=0, keepdims=True)
    r = jnp.exp(m2 - m1)
    w1 = g_w / (1.0 + r)
    w2 = g_w * r / (1.0 + r)
    sel1 = row == i1
    sel2 = row == i2
    both = jnp.where(sel1, 1.0, jnp.where(sel2, 1.0, 0.0))
    cnt16 = jnp.floor((jnp.sum(both, axis=1, keepdims=True) + (MOE_CHUNK - 1)) * (1.0 / MOE_CHUNK))
    er = lax.broadcasted_iota(jnp.int32, (LANES, LANES), 0)
    ec = lax.broadcasted_iota(jnp.int32, (LANES, LANES), 1)
    cnt16_b = jnp.broadcast_to(cnt16, (LANES, LANES))
    off16 = jnp.dot((ec < er).astype(BF16), cnt16_b.astype(BF16), preferred_element_type=F32)[:, 0:1]
    earlier = (lax.broadcasted_iota(jnp.int32, (tm, tm), 0)
               < lax.broadcasted_iota(jnp.int32, (tm, tm), 1)).astype(BF16)
    base = off16 * MOE_CHUNK + jnp.dot(both.astype(BF16), earlier, preferred_element_type=F32)
    lpos1 = jnp.sum(jnp.where(sel1, base, 0.0), axis=0, keepdims=True)
    lpos2 = jnp.sum(jnp.where(sel2, base, 0.0), axis=0, keepdims=True)
    h2_ref[...] = h_hi
    col_ref[...] = jnp.where(row == 0, lpos1, jnp.where(row == 1, lpos2, 0.0)).T
    row8 = lax.broadcasted_iota(jnp.int32, info_ref.shape, 0)
    info_ref[...] = jnp.where(row8 == 0, lpos1, jnp.where(row8 == 1, lpos2,
                              jnp.where(row8 == 2, w1, jnp.where(row8 == 3, w2, 0.0))))
    cnt_ref[...] = cnt16_b
    chunk = lax.broadcasted_iota(jnp.int32, (LANES, LANES), 1).astype(F32)
    ce = jnp.sum(jnp.where(off16 + cnt16 <= chunk, 1.0, 0.0), axis=0, keepdims=True) - MOE_GROUPS
    ce_ref[...] = jnp.broadcast_to(ce, ce_ref.shape)


def _outproj(y_ssd, y_mla, x2, mod3, w_top, w_bot, gpm, gpf, wr_hi, wr_lo, b_router, seq, tm):
    t, d = x2.shape
    per_b = seq // tm
    n_tok_tiles = t // tm
    row = lambda i: (i, 0)
    const = lambda i: (0, 0)
    return pl.pallas_call(
        _outproj_kernel,
        grid=(n_tok_tiles,),
        in_specs=[pl.BlockSpec((tm, SSD_INNER), row),
                  pl.BlockSpec((tm, MLA_INNER), row),
                  pl.BlockSpec((tm, d), row),
                  pl.BlockSpec((1, N_MOD, d), lambda i: (i // per_b, 0, 0)),
                  pl.BlockSpec(w_top.shape, const),
                  pl.BlockSpec(w_bot.shape, const),
                  pl.BlockSpec((1, d), const),
                  pl.BlockSpec((1, d), const),
                  pl.BlockSpec(wr_hi.shape, const),
                  pl.BlockSpec(wr_lo.shape, const),
                  pl.BlockSpec((1, LANES), const)],
        out_specs=[pl.BlockSpec((tm, d), row),
                   pl.BlockSpec((tm, d), row),
                   pl.BlockSpec((tm, LANES), row),
                   pl.BlockSpec((8, tm), row),
                   pl.BlockSpec((LANES, LANES), row),
                   pl.BlockSpec((8, LANES), row)],
        out_shape=[jax.ShapeDtypeStruct((t, d), F32),
                   jax.ShapeDtypeStruct((t, d), BF16),
                   jax.ShapeDtypeStruct((t, LANES), F32),
                   jax.ShapeDtypeStruct((n_tok_tiles * 8, tm), F32),
                   jax.ShapeDtypeStruct((n_tok_tiles * LANES, LANES), F32),
                   jax.ShapeDtypeStruct((n_tok_tiles * 8, LANES), F32)],
        compiler_params=_params("arbitrary"),
        name="outproj_router",
    )(y_ssd, y_mla, x2, mod3, w_top, w_bot, gpm, gpf, wr_hi, wr_lo, b_router)


def _moe_rows(n_tok_tiles):
    rows = n_tok_tiles * MOE_SLOTS + MOE_EXPERTS * (MOE_TILE - MOE_CHUNK)
    return (rows + MOE_TILE - 1) // MOE_TILE * MOE_TILE


def _moe_plan(cnt_tiles, chunk_expert, n_tok_tiles, n_rows):
    chunks_per_tile = MOE_SLOTS // MOE_CHUNK
    chunks_per_rows = MOE_TILE // MOE_CHUNK
    cnt = cnt_tiles.reshape(n_tok_tiles, LANES, LANES)[:, MOE_GROUPS:MOE_GROUPS + MOE_EXPERTS, 0]
    cnt = cnt.astype(jnp.int32)
    ce = chunk_expert.reshape(n_tok_tiles, 8, LANES)[:, 0, :chunks_per_tile].astype(jnp.int32)
    used = jnp.sum(cnt, axis=1)
    total = jnp.sum(cnt, axis=0)
    padded = (total + chunks_per_rows - 1) // chunks_per_rows * chunks_per_rows
    e_end = jnp.cumsum(padded)
    e_start = e_end - padded
    run_global = e_start[None, :] + jnp.cumsum(cnt, axis=0) - cnt
    run_local = jnp.cumsum(cnt, axis=1) - cnt
    experts = jnp.arange(MOE_EXPERTS, dtype=jnp.int32)
    shift = jnp.sum(jnp.where(ce[:, :, None] == experts, (run_global - run_local)[:, None, :], 0), axis=-1)
    local = jnp.arange(chunks_per_tile, dtype=jnp.int32)[None, :]
    dst = jnp.where(local < used[:, None], shift + local, 0)
    n_row_tiles = n_rows // MOE_TILE
    n_used = e_end[-1] // chunks_per_rows
    j = jnp.minimum(jnp.arange(n_row_tiles, dtype=jnp.int32), jnp.maximum(n_used - 1, 0))
    tile_expert = jnp.sum((e_end[None, :] // chunks_per_rows <= j[:, None]).astype(jnp.int32), axis=1)
    tile_expert = jnp.minimum(tile_expert, MOE_EXPERTS - 1)
    i32 = lambda v: v.astype(jnp.int32)
    return (i32(dst.reshape(-1)), i32(used), i32(e_start + total), i32(padded - total),
            i32(n_used.reshape(1)), i32(tile_expert))


def _chunk_rows(chunk):
    return pl.ds(pl.multiple_of(chunk * MOE_CHUNK, MOE_CHUNK), MOE_CHUNK)


def _run_copies(*groups):
    for wait in (False, True):
        for lo, hi, make_copy in groups:
            def body(k, carry, wait=wait, make_copy=make_copy):
                cp = make_copy(k)
                cp.wait() if wait else cp.start()
                return carry
            lax.fori_loop(lo, hi, body, 0)


def _scatter_chunks_kernel(dst_ref, used_ref, pad_start_ref, pad_n_ref, nused_ref, h2_ref, info_ref,
                           o_ref, wslot_ref, src_ref, zero_ref, sem):
    i = pl.program_id(0)
    chunks = MOE_SLOTS // MOE_CHUNK
    slot = lax.rem(i, 2)

    info = info_ref[...]
    srow = lax.broadcasted_iota(jnp.int32, (MOE_SLOTS, info.shape[1]), 0)
    hit1 = srow == info[0:1, :].astype(jnp.int32)
    hit2 = srow == info[1:2, :].astype(jnp.int32)
    perm = jnp.where(hit1, 1.0, jnp.where(hit2, 1.0, 0.0)).astype(BF16)
    src_ref[slot] = jnp.dot(perm, h2_ref[...], preferred_element_type=F32).astype(BF16)
    wslot = jnp.sum(jnp.where(hit1, info[2:3, :], jnp.where(hit2, info[3:4, :], 0.0)), axis=1, keepdims=True)
    wslot_ref[...] = jnp.broadcast_to(wslot, wslot_ref.shape)

    @pl.when(i == 0)
    def _():
        zero_ref[...] = jnp.zeros_like(zero_ref)
        pads = [(0, pad_n_ref[e], lambda c, e=e: pltpu.make_async_copy(
            zero_ref.at[pl.ds(0, MOE_CHUNK), :], o_ref.at[_chunk_rows(pad_start_ref[e] + c), :], sem.at[2]))
            for e in range(MOE_EXPERTS)]
        tail = (nused_ref[0], o_ref.shape[0] // MOE_TILE, lambda j: pltpu.make_async_copy(
            zero_ref, o_ref.at[pl.ds(pl.multiple_of(j * MOE_TILE, MOE_TILE), MOE_TILE), :], sem.at[2]))
        _run_copies(tail, *pads)

    def chunk_copies(tile, buf, wait):
        def body(c, carry):
            cp = pltpu.make_async_copy(src_ref.at[buf, _chunk_rows(c), :],
                                       o_ref.at[_chunk_rows(dst_ref[tile * chunks + c]), :], sem.at[buf])
            cp.wait() if wait else cp.start()
            return carry
        lax.fori_loop(0, used_ref[tile], body, 0)

    chunk_copies(i, slot, False)

    @pl.when(i > 0)
    def _():
        chunk_copies(i - 1, 1 - slot, True)

    @pl.when(i == pl.num_programs(0) - 1)
    def _():
        chunk_copies(i, slot, True)


def _scatter_chunks(h2, info, dst, used, pad_start, pad_n, n_used, n_rows):
    t, d = h2.shape
    tm = info.shape[1]
    row = lambda i, *_: (i, 0)
    return pl.pallas_call(
        _scatter_chunks_kernel,
        grid_spec=pltpu.PrefetchScalarGridSpec(
            num_scalar_prefetch=5,
            grid=(used.shape[0],),
            in_specs=[pl.BlockSpec((tm, d), row), pl.BlockSpec((8, tm), row)],
            out_specs=[pl.BlockSpec(memory_space=pl.ANY), pl.BlockSpec((MOE_SLOTS, LANES), row)],
            scratch_shapes=[pltpu.VMEM((2, MOE_SLOTS, d), BF16), pltpu.VMEM((MOE_TILE, d), BF16),
                            pltpu.SemaphoreType.DMA((3,))]),
        out_shape=[jax.ShapeDtypeStruct((n_rows, d), BF16),
                   jax.ShapeDtypeStruct((used.shape[0] * MOE_SLOTS, LANES), F32)],
        compiler_params=_params("arbitrary"),
        name="moe_scatter_chunks",
    )(dst, used, pad_start, pad_n, n_used, h2, info)


def _experts_kernel(te_ref, nt_ref, xs_ref, wg_ref, wu_ref, wd_ref, o_ref):
    used = pl.program_id(0) < nt_ref[0]

    @pl.when(used)
    def _():
        x = xs_ref[...]
        gate = jnp.dot(x, wg_ref[0].astype(BF16), preferred_element_type=F32)
        up = jnp.dot(x, wu_ref[0].astype(BF16), preferred_element_type=F32)
        hid = (_silu(gate) * up).astype(BF16)
        o_ref[...] = jnp.dot(hid, wd_ref[0].astype(BF16), preferred_element_type=F32).astype(BF16)

    @pl.when(jnp.logical_not(used))
    def _():
        o_ref[...] = jnp.zeros_like(o_ref)


def _experts(xs, tile_expert, n_used, w_gate, w_up, w_down):
    n_rows, d = xs.shape
    tile = MOE_TILE
    by_expert = lambda j, te, nt: (te[j], 0, 0)
    return pl.pallas_call(
        _experts_kernel,
        grid_spec=pltpu.PrefetchScalarGridSpec(
            num_scalar_prefetch=2,
            grid=(n_rows // tile,),
            in_specs=[pl.BlockSpec((tile, d), lambda j, te, nt: (jnp.maximum(jnp.minimum(j, nt[0] - 1), 0), 0)),
                      pl.BlockSpec((1, d, MOE_FF), by_expert),
                      pl.BlockSpec((1, d, MOE_FF), by_expert),
                      pl.BlockSpec((1, MOE_FF, d), by_expert)],
            out_specs=pl.BlockSpec((tile, d), lambda j, te, nt: (j, 0))),
        out_shape=jax.ShapeDtypeStruct((n_rows, d), BF16),
        compiler_params=_params("arbitrary"),
        name="moe_experts",
    )(tile_expert, n_used, xs, w_gate, w_up, w_down)


def _combine_kernel(dst_ref, used_ref, wslot_ref, col_ref, x1_ref, mod_ref, gpost_ref, ys_ref, o_ref,
                    ysl_ref, sem):
    i = pl.program_id(0)
    chunks = MOE_SLOTS // MOE_CHUNK
    slot = lax.rem(i, 2)

    def fetch(tile, buf, wait):
        base = tile * chunks

        def copy(c, carry):
            cp = pltpu.make_async_copy(ys_ref.at[_chunk_rows(dst_ref[base + c]), :],
                                       ysl_ref.at[buf, _chunk_rows(c), :], sem.at[buf])
            cp.wait() if wait else cp.start()
            return carry

        lax.fori_loop(0, chunks, copy, 0, unroll=8)

    @pl.when(i == 0)
    def _():
        fetch(0, 0, False)

    @pl.when(i + 1 < pl.num_programs(0))
    def _():
        fetch(i + 1, 1 - slot, False)

    col = col_ref[...]
    tm = col.shape[0]
    lane = lax.broadcasted_iota(jnp.int32, (tm, MOE_SLOTS), 1)
    hit = jnp.where(lane == col[:, 0:1].astype(jnp.int32), 1.0,
                    jnp.where(lane == col[:, 1:2].astype(jnp.int32), 1.0, 0.0)).astype(BF16)
    fetch(i, slot, True)
    in_use = lax.broadcasted_iota(jnp.int32, (MOE_SLOTS, 1), 0) < used_ref[i] * MOE_CHUNK
    ysw = jnp.where(in_use, ysl_ref[slot].astype(F32) * wslot_ref[:, 0:1], 0.0).astype(BF16)
    y = jnp.dot(hit, ysw, preferred_element_type=F32)
    g2 = mod_ref[0, 5:6, :]
    o_ref[...] = x1_ref[...] + g2 * _rms(y, gpost_ref[...])


def _combine(ys, dst, used, wslot, col, x1, mod3, gpost, seq, tm):
    t, d = x1.shape
    per_b = seq // tm
    row = lambda i, *_: (i, 0)
    return pl.pallas_call(
        _combine_kernel,
        grid_spec=pltpu.PrefetchScalarGridSpec(
            num_scalar_prefetch=2,
            grid=(t // tm,),
            in_specs=[pl.BlockSpec((MOE_SLOTS, LANES), row),
                      pl.BlockSpec((tm, LANES), row),
                      pl.BlockSpec((tm, d), row),
                      pl.BlockSpec((1, N_MOD, d), lambda i, *_: (i // per_b, 0, 0)),
                      pl.BlockSpec((1, d), lambda i, *_: (0, 0)),
                      pl.BlockSpec(memory_space=pl.ANY)],
            out_specs=pl.BlockSpec((tm, d), row),
            scratch_shapes=[pltpu.VMEM((2, MOE_SLOTS, d), BF16), pltpu.SemaphoreType.DMA((2,))]),
        out_shape=jax.ShapeDtypeStruct((t, d), F32),
        compiler_params=_params("arbitrary"),
        name="moe_combine",
    )(dst, used, wslot, col, x1, mod3, gpost, ys)


def kernel(x, c, positions, ada_w, ada_b, pre_norm_mix, post_norm_mix, pre_norm_ffn, post_norm_ffn, w_in, conv_w, conv_b, dt_bias, a_log, d_skip, ssd_norm, q_norm, w_uq, kv_norm, w_ukv, mla_out_norm, w_out, w_group_router, b_group_router, w_expert_router, b_expert_router, w_gate, w_up, w_down):
    batch, seq, d = x.shape
    t = batch * seq
    depth = ada_w.shape[0]
    tm = min(INPROJ_TOKENS, seq)
    cos_t, sin_t = _rope_tables(positions)

    x2 = x.reshape(t, d)
    for l in range(depth):
        mod3 = _modulation(c, ada_w[l], ada_b[l]).reshape(batch, N_MOD, d)
        w_main, w_rest, wq2, wk, wv_t = _inproj_weights(w_in[l], w_uq[l], w_ukv[l])
        z, xbc, dt_raw, q, k, v_t = _inproj(
            x2, mod3, pre_norm_mix[l].reshape(1, d), w_main, w_rest, q_norm[l].reshape(1, -1), wq2,
            kv_norm[l].reshape(1, -1), wk, wv_t, cos_t, sin_t, seq, tm)
        y_ssd = _ssd(xbc, z, dt_raw, conv_w[l], conv_b[l], dt_bias[l], a_log[l], d_skip[l],
                     ssd_norm[l], batch, seq)
        y_mla = _attention(q, k, v_t, mla_out_norm[l].reshape(1, -1), batch, seq)
        w_o = w_out[l].astype(BF16)
        pad_r = LANES - MOE_GROUPS - MOE_EXPERTS
        w_router = jnp.concatenate([w_group_router[l], w_expert_router[l], jnp.zeros((d, pad_r), F32)], axis=1)
        b_router = jnp.concatenate([b_group_router[l].reshape(-1), b_expert_router[l].reshape(-1),
                                    jnp.zeros((pad_r,), F32)]).reshape(1, LANES)
        wr_hi = w_router.astype(BF16)
        wr_lo = (w_router - wr_hi.astype(F32)).astype(BF16)
        x1, h2, col, info, cnt_tiles, chunk_expert = _outproj(
            y_ssd, y_mla, x2, mod3, w_o[:SSD_INNER], w_o[SSD_INNER:], post_norm_mix[l].reshape(1, d),
            pre_norm_ffn[l].reshape(1, d), wr_hi, wr_lo, b_router, seq, MOE_TOKENS)
        n_tok_tiles = t // MOE_TOKENS
        n_rows = _moe_rows(n_tok_tiles)
        dst, used, pad_start, pad_n, n_used, tile_expert = _moe_plan(cnt_tiles, chunk_expert, n_tok_tiles, n_rows)
        xs, wslot = _scatter_chunks(h2, info, dst, used, pad_start, pad_n, n_used, n_rows)
        ys = _experts(xs, tile_expert, n_used, w_gate[l], w_up[l], w_down[l])
        x2 = _combine(ys, dst, used, wslot, col, x1, mod3, post_norm_ffn[l].reshape(1, d), seq, MOE_TOKENS)
    return x2.reshape(batch, seq, d)
```

```python
import functools
import math

import jax
import jax.numpy as jnp
from jax import lax
from jax.experimental import pallas as pl
from jax.experimental.pallas import tpu as pltpu

F32 = jnp.float32
BF16 = jnp.bfloat16

D_MODEL = 1024
SSD_HEADS = 8
SSD_HEAD_DIM = 64
SSD_INNER = SSD_HEADS * SSD_HEAD_DIM
SSD_GROUPS = 2
SSD_STATE = 128
SSD_CONV = 4
SSD_CHUNK = 128
SSD_XBC = SSD_INNER + 2 * SSD_GROUPS * SSD_STATE
MLA_HEADS = 8
MLA_NOPE = 64
MLA_ROPE = 32
MLA_QK = MLA_NOPE + MLA_ROPE
MLA_V = 64
MLA_Q_RANK = 256
MLA_KV_RANK = 128
MLA_INNER = MLA_HEADS * MLA_V
ROPE_THETA = 10000.0
MOE_GROUPS = 4
MOE_EXPERTS_PER_GROUP = 8
MOE_EXPERTS = MOE_GROUPS * MOE_EXPERTS_PER_GROUP
MOE_FF = 256
N_MOD = 6
EPS = 1e-6
LOG2_E = math.log2(math.e)

LANES = 128
HEAD_PAD = 128
ATT_BLOCK = 256
ATT_STEP_BLOCKS = 2
MLA_V_ROWS = MLA_V + 16
INPROJ_TOKENS = 512
SSD_STEP_ROWS = 8 * SSD_CHUNK
MOE_TOKENS = 512
MOE_CHUNK = 16
MOE_TILE = 512
MOE_SLOTS = 2 * MOE_TOKENS + MOE_EXPERTS * MOE_CHUNK
VMEM_LIMIT = 48 * 1024 * 1024

_C_Z = 0
_C_XBC = _C_Z + SSD_INNER
_C_CQ = _C_XBC + SSD_XBC
_C_CKV = _C_CQ + MLA_Q_RANK
_C_KR = _C_CKV + MLA_KV_RANK
_C_KRS = _C_KR + LANES
_C_DT = _C_KRS + LANES
_C_END = _C_DT + LANES


def _silu(v):
    return v * (1.0 / (1.0 + jnp.exp(-v)))


def _rms(v, gain):
    return v * lax.rsqrt(jnp.mean(v * v, axis=-1, keepdims=True) + EPS) * gain


def _params(*sem, flags=None):
    return pltpu.CompilerParams(dimension_semantics=sem, vmem_limit_bytes=VMEM_LIMIT, flags=flags)


def _mod_kernel(c_ref, w_ref, b_ref, o_ref):
    act = _silu(c_ref[...])
    a_hi = act.astype(BF16)
    a_lo = (act - a_hi.astype(F32)).astype(BF16)
    w = w_ref[...]
    w_hi = w.astype(BF16)
    w_lo = (w - w_hi.astype(F32)).astype(BF16)
    o_ref[...] = (jnp.dot(a_hi, w_hi, preferred_element_type=F32)
                  + jnp.dot(a_lo, w_hi, preferred_element_type=F32)
                  + jnp.dot(a_hi, w_lo, preferred_element_type=F32)) + b_ref[...]


def _modulation(c, ada_w, ada_b):
    b, d = c.shape
    n = ada_w.shape[1]
    return pl.pallas_call(
        _mod_kernel,
        grid=(n // d,),
        in_specs=[pl.BlockSpec((b, d), lambda j: (0, 0)),
                  pl.BlockSpec((d, d), lambda j: (0, j)),
                  pl.BlockSpec((1, d), lambda j: (0, j))],
        out_specs=pl.BlockSpec((b, d), lambda j: (0, j)),
        out_shape=jax.ShapeDtypeStruct((b, n), F32),
        compiler_params=_params("arbitrary"),
        name="modulation",
    )(c, ada_w, ada_b.reshape(1, n))


def _rope_kernel(pos_ref, inv_ref, cos_ref, sin_ref):
    ang = pos_ref[...].astype(F32) * inv_ref[...]
    blk = ang.shape[0]
    half = MLA_ROPE // 2
    per_row = LANES // half
    lane = lax.broadcasted_iota(jnp.int32, (blk * per_row, LANES), 1)
    first = (lane >= MLA_NOPE) & (lane < MLA_NOPE + half)
    second = (lane >= MLA_NOPE + half) & (lane < MLA_QK)
    for fn, out_ref, fill in ((jnp.cos, cos_ref, 1.0), (jnp.sin, sin_ref, 0.0)):
        compact = fn(ang)
        rep = jnp.broadcast_to(compact[:, None, :], (blk, per_row, LANES)).reshape(blk * per_row, LANES)
        a = pltpu.roll(rep, 0, 1, stride=half, stride_axis=0)
        b = pltpu.roll(a, half, 1)
        out_ref[...] = jnp.where(first, a, jnp.where(second, b, fill))


def _rope_tables(positions):
    t = positions.size
    half = MLA_ROPE // 2
    per_row = LANES // half
    inv = 1.0 / (ROPE_THETA ** (jnp.arange(0, MLA_ROPE, 2, dtype=F32) / MLA_ROPE))
    group_token = (MLA_NOPE // half - jnp.arange(per_row)) % per_row
    pos_rep = jnp.repeat(positions.reshape(t // per_row, per_row)[:, group_token], half, axis=1)
    inv_t = jnp.tile(inv, per_row).reshape(1, LANES)
    rows = t // per_row
    blk = min(rows, 512)
    return pl.pallas_call(
        _rope_kernel,
        grid=(rows // blk,),
        in_specs=[pl.BlockSpec((blk, LANES), lambda i: (i, 0)),
                  pl.BlockSpec((1, LANES), lambda i: (0, 0))],
        out_specs=[pl.BlockSpec((blk * per_row, LANES), lambda i: (i, 0))] * 2,
        out_shape=[jax.ShapeDtypeStruct((t, LANES), F32)] * 2,
        compiler_params=_params("arbitrary"),
        name="rope_tables",
    )(pos_rep, inv_t)


def _inproj_kernel(x_ref, mod_ref, gpre_ref, wmain_ref, wrest_ref, qn_ref, wq_ref, kvn_ref, wk_ref,
                   wvt_ref, cos_ref, sin_ref, z_ref, xbc_ref, dt_ref, q_ref, k_ref, vt_ref):
    x = x_ref[...]
    sh = mod_ref[0, 0:1, :]
    sc = mod_ref[0, 1:2, :]
    h = (_rms(x, gpre_ref[...]) * (1.0 + sc) + sh).astype(BF16)
    u = jnp.concatenate([jnp.dot(h, wmain_ref[...], preferred_element_type=F32),
                         jnp.dot(h, wrest_ref[...], preferred_element_type=F32)], axis=1)
    z_ref[...] = u[:, _C_Z:_C_XBC].astype(BF16)
    xbc_ref[...] = u[:, _C_XBC:_C_CQ].astype(BF16)
    dt_ref[...] = u[:, _C_DT:_C_END]
    cos_t = cos_ref[...]
    sin_t = sin_ref[...]
    cq = _rms(u[:, _C_CQ:_C_CKV], qn_ref[...]).astype(BF16)
    q2 = jnp.dot(cq, wq_ref[...], preferred_element_type=F32)
    scale = MLA_QK ** -0.5 * LOG2_E
    nq = MLA_HEADS * HEAD_PAD
    for hd in range(MLA_HEADS):
        a = q2[:, hd * HEAD_PAD:(hd + 1) * HEAD_PAD]
        b = q2[:, nq + hd * HEAD_PAD:nq + (hd + 1) * HEAD_PAD]
        q_ref[:, hd * HEAD_PAD:(hd + 1) * HEAD_PAD] = ((a * cos_t + b * sin_t) * scale).astype(BF16)
    ckv = _rms(u[:, _C_CKV:_C_KR], kvn_ref[...]).astype(BF16)
    kn = jnp.dot(ckv, wk_ref[...], preferred_element_type=F32)
    k_pe = u[:, _C_KR:_C_KRS] * cos_t + u[:, _C_KRS:_C_DT] * sin_t
    for hd in range(MLA_HEADS):
        k_ref[:, hd * HEAD_PAD:(hd + 1) * HEAD_PAD] = (
            kn[:, hd * HEAD_PAD:(hd + 1) * HEAD_PAD] + k_pe).astype(BF16)
    v_t = lax.dot_general(wvt_ref[...], ckv, (((1,), (1,)), ((), ())), preferred_element_type=F32)
    head_row = lax.rem(lax.broadcasted_iota(jnp.int32, (v_t.shape[0], 1), 0), MLA_V_ROWS)
    v_t = v_t + jnp.where(head_row == MLA_V, 1.0, 0.0)
    for s in range(vt_ref.shape[0]):
        vt_ref[s] = v_t[:, s * ATT_BLOCK:(s + 1) * ATT_BLOCK].astype(BF16)


def _inproj_weights(w_in, w_uq, w_ukv):
    d = w_in.shape[0]
    half = MLA_ROPE // 2
    o_z, o_xbc, o_dt = 0, SSD_INNER, SSD_INNER + SSD_XBC
    o_cq = o_dt + SSD_HEADS
    o_ckv = o_cq + MLA_Q_RANK
    o_kr = o_ckv + MLA_KV_RANK
    zeros = lambda n: jnp.zeros((d, n), F32)
    kr = w_in[:, o_kr:o_kr + MLA_ROPE]
    kr_blk = jnp.concatenate([zeros(MLA_NOPE), kr, zeros(HEAD_PAD - MLA_QK)], axis=1)
    krs_blk = jnp.concatenate([zeros(MLA_NOPE), -kr[:, half:], kr[:, :half], zeros(HEAD_PAD - MLA_QK)], axis=1)
    dt_blk = jnp.concatenate([w_in[:, o_dt:o_dt + SSD_HEADS], zeros(LANES - SSD_HEADS)], axis=1)
    w_main = w_in[:, o_z:o_dt]
    w_rest = jnp.concatenate([w_in[:, o_cq:o_kr], kr_blk, krs_blk, dt_blk], axis=1)
    r = w_uq.shape[0]
    zq = jnp.zeros((r, MLA_HEADS, HEAD_PAD - MLA_QK), F32)
    zn = jnp.zeros((r, MLA_HEADS, MLA_NOPE), F32)
    wq_plain = jnp.concatenate([w_uq, zq], axis=2).reshape(r, MLA_HEADS * HEAD_PAD)
    wq_rot = jnp.concatenate([zn, -w_uq[:, :, MLA_NOPE + half:], w_uq[:, :, MLA_NOPE:MLA_NOPE + half], zq],
                             axis=2).reshape(r, MLA_HEADS * HEAD_PAD)
    wq2 = jnp.concatenate([wq_plain, wq_rot], axis=1)
    rk = w_ukv.shape[0]
    zk = jnp.zeros((rk, MLA_HEADS, HEAD_PAD - MLA_NOPE), F32)
    wk = jnp.concatenate([w_ukv[:, :, :MLA_NOPE], zk], axis=2).reshape(rk, MLA_HEADS * HEAD_PAD)
    zv = jnp.zeros((rk, MLA_HEADS, MLA_V_ROWS - MLA_V), F32)
    wv_t = jnp.concatenate([w_ukv[:, :, MLA_NOPE:], zv], axis=2).reshape(rk, MLA_HEADS * MLA_V_ROWS).T
    return w_main.astype(BF16), w_rest.astype(BF16), wq2.astype(BF16), wk.astype(BF16), wv_t.astype(BF16)


def _inproj(x2, mod3, gpre, w_main, w_rest, q_norm, wq2, kv_norm, wk, wv_t, cos_t, sin_t, seq, tm):
    t, d = x2.shape
    per_b = seq // tm
    row = lambda i: (i, 0)
    const = lambda i: (0, 0)
    nqk = MLA_HEADS * HEAD_PAD
    slabs = tm // ATT_BLOCK
    return pl.pallas_call(
        _inproj_kernel,
        grid=(t // tm,),
        in_specs=[pl.BlockSpec((tm, d), row),
                  pl.BlockSpec((1, N_MOD, d), lambda i: (i // per_b, 0, 0)),
                  pl.BlockSpec((1, d), const),
                  pl.BlockSpec(w_main.shape, const),
                  pl.BlockSpec(w_rest.shape, const),
                  pl.BlockSpec((1, MLA_Q_RANK), const),
                  pl.BlockSpec(wq2.shape, const),
                  pl.BlockSpec((1, MLA_KV_RANK), const),
                  pl.BlockSpec(wk.shape, const),
                  pl.BlockSpec(wv_t.shape, const),
                  pl.BlockSpec((tm, LANES), row),
                  pl.BlockSpec((tm, LANES), row)],
        out_specs=[pl.BlockSpec((tm, SSD_INNER), row),
                   pl.BlockSpec((tm, SSD_XBC), row),
                   pl.BlockSpec((tm, LANES), row),
                   pl.BlockSpec((tm, nqk), row),
                   pl.BlockSpec((tm, nqk), row),
                   pl.BlockSpec((slabs, MLA_HEADS * MLA_V_ROWS, ATT_BLOCK), lambda i: (i, 0, 0))],
        out_shape=[jax.ShapeDtypeStruct((t, SSD_INNER), BF16),
                   jax.ShapeDtypeStruct((t, SSD_XBC), BF16),
                   jax.ShapeDtypeStruct((t, LANES), F32),
                   jax.ShapeDtypeStruct((t, nqk), BF16),
                   jax.ShapeDtypeStruct((t, nqk), BF16),
                   jax.ShapeDtypeStruct((t // ATT_BLOCK, MLA_HEADS * MLA_V_ROWS, ATT_BLOCK), BF16)],
        compiler_params=_params("arbitrary"),
        name="inproj",
    )(x2, mod3, gpre, w_main, w_rest, q_norm, wq2, kv_norm, wk, wv_t, cos_t, sin_t)


def _split3_packed(v):
    lane = lax.broadcasted_iota(jnp.int32, v.shape, 1)
    v = jnp.where(lane < SSD_HEADS, v, 0.0)
    hi = v.astype(BF16).astype(F32)
    rest = v - hi
    mid = rest.astype(BF16).astype(F32)
    lo = rest - mid
    return (hi + pltpu.roll(mid, SSD_HEADS, 1) + pltpu.roll(lo, 2 * SSD_HEADS, 1)).astype(BF16)


def _ssd_kernel(xbc_ref, z_ref, dt_ref, cw_ref, cb_ref, dtb_ref, alog_ref, dsk_ref, gn_ref,
                e_ref, y_ref, tail_ref, state_ref):
    @pl.when(pl.program_id(1) == 0)
    def _():
        tail_ref[...] = jnp.zeros_like(tail_ref)
        state_ref[...] = jnp.zeros_like(state_ref)

    for c in range(xbc_ref.shape[0] // SSD_CHUNK):
        _ssd_chunk(pl.ds(c * SSD_CHUNK, SSD_CHUNK), xbc_ref, z_ref, dt_ref, cw_ref, cb_ref, dtb_ref,
                   alog_ref, dsk_ref, gn_ref, e_ref, y_ref, tail_ref, state_ref)


def _ssd_chunk(rows, xbc_ref, z_ref, dt_ref, cw_ref, cb_ref, dtb_ref, alog_ref, dsk_ref, gn_ref,
               e_ref, y_ref, tail_ref, state_ref):
    L = SSD_CHUNK
    gw = SSD_INNER // SSD_GROUPS

    cur_b = xbc_ref[rows, :]
    ext = jnp.concatenate([tail_ref[...], cur_b], axis=0)
    out_row = lax.broadcasted_iota(jnp.int32, ((SSD_CONV - 1) * L, ext.shape[0]), 0)
    src_row = lax.broadcasted_iota(jnp.int32, ((SSD_CONV - 1) * L, ext.shape[0]), 1)
    tail_rows = ext.shape[0] - L
    shift = jnp.where(src_row + (out_row // L + 1) == lax.rem(out_row, L) + tail_rows, 1.0, 0.0)
    shifted = jnp.dot(shift.astype(BF16), ext, preferred_element_type=F32)
    acc = cur_b.astype(F32) * cw_ref[SSD_CONV - 1:SSD_CONV, :] + cb_ref[...]
    for j in range(1, SSD_CONV):
        acc = acc + shifted[(j - 1) * L:j * L] * cw_ref[SSD_CONV - 1 - j:SSD_CONV - j, :]
    tail_ref[...] = cur_b[L - tail_rows:L]
    act = _silu(acc)
    xs = act[:, :SSD_INNER]
    bm = act[:, SSD_INNER:SSD_INNER + SSD_GROUPS * SSD_STATE].astype(BF16)
    cm = act[:, SSD_INNER + SSD_GROUPS * SSD_STATE:].astype(BF16)

    dt_in = dt_ref[rows, :] + dtb_ref[...]
    dt = jnp.maximum(dt_in, 0.0) + jnp.log(1.0 + jnp.exp(-jnp.abs(dt_in)))
    adt = dt * (-jnp.exp(alog_ref[...]))
    ri = lax.broadcasted_iota(jnp.int32, (L, L), 0)
    ci = lax.broadcasted_iota(jnp.int32, (L, L), 1)
    causal = ci <= ri
    acs_p = jnp.dot(causal.astype(BF16), _split3_packed(adt), preferred_element_type=F32)
    a_cs = (acs_p + pltpu.roll(acs_p, LANES - SSD_HEADS, 1)
            + pltpu.roll(acs_p, LANES - 2 * SSD_HEADS, 1))
    a_cs_t = a_cs.T
    cs_all = jnp.dot(_split3_packed(a_cs), e_ref[...], preferred_element_type=F32)
    cs64 = cs_all[:, :SSD_INNER]
    cs128 = cs_all[:, SSD_INNER:]
    dt64 = jnp.dot(_split3_packed(dt), e_ref[:, :SSD_INNER], preferred_element_type=F32)

    xd = xs * dt64
    xd_b = xd.astype(BF16)
    last = cs64[L - 1:L, :]
    xdw = (xd * jnp.exp(last - cs64)).astype(BF16)
    chunk_decay = jnp.exp(last)
    in_decay = jnp.exp(cs64)

    lane = lax.broadcasted_iota(jnp.int32, (L, LANES), 1)
    lo = lane < SSD_HEAD_DIM
    zero_b = jnp.zeros((L, LANES), BF16)
    y_parts = []
    new_states = []
    for g in range(SSD_GROUPS):
        bg = bm[:, g * SSD_STATE:(g + 1) * SSD_STATE]
        cg = cm[:, g * SSD_STATE:(g + 1) * SSD_STATE]
        cb = lax.dot_general(cg, bg, (((1,), (1,)), ((), ())), preferred_element_type=F32)
        prev = state_ref[:, g * gw:(g + 1) * gw]
        y_off = jnp.dot(cg, prev.astype(BF16), preferred_element_type=F32) * in_decay[:, g * gw:(g + 1) * gw]
        st = lax.dot_general(bg, xdw[:, g * gw:(g + 1) * gw], (((0,), (0,)), ((), ())),
                             preferred_element_type=F32)
        new_states.append(prev * chunk_decay[:, g * gw:(g + 1) * gw] + st)
        heads_per_group = SSD_HEADS // SSD_GROUPS
        for pair in range(heads_per_group // 2):
            h0 = g * heads_per_group + 2 * pair
            blk = xd_b[:, h0 * SSD_HEAD_DIM:(h0 + 2) * SSD_HEAD_DIM]
            y_pair = None
            for k in range(2):
                hh = h0 + k
                diff = cs128[:, hh * LANES:(hh + 1) * LANES] - a_cs_t[hh:hh + 1, :]
                decay = jnp.where(causal, jnp.exp(diff), 0.0)
                m = (cb * decay).astype(BF16)
                rhs = jnp.where(lo, blk, zero_b) if k == 0 else jnp.where(lo, zero_b, blk)
                part = jnp.dot(m, rhs, preferred_element_type=F32)
                y_pair = part if y_pair is None else y_pair + part
            c0 = (2 * pair) * SSD_HEAD_DIM
            y_parts.append(y_pair + y_off[:, c0:c0 + LANES])
    state_ref[...] = jnp.concatenate(new_states, axis=1)
    y = jnp.concatenate(y_parts, axis=1) + dsk_ref[...] * xs
    gated = y * _silu(z_ref[rows, :].astype(F32))
    y_ref[rows, :] = _rms(gated, gn_ref[...]).astype(BF16)


def _ssd(xbc, z, dt_raw, conv_w, conv_b, dt_bias, a_log, d_skip, ssd_norm, batch, seq):
    t = xbc.shape[0]
    L = min(SSD_STEP_ROWS, seq)
    nc = seq // L
    row = lambda b, c: (b * nc + c, 0)
    const = lambda b, c: (0, 0)
    pad = lambda v: jnp.concatenate([v.reshape(1, -1), jnp.zeros((1, LANES - v.size), F32)], axis=1)
    head_of_lane = jnp.concatenate([jnp.arange(SSD_INNER) // SSD_HEAD_DIM,
                                    jnp.arange(SSD_HEADS * LANES) // LANES])
    rows = jnp.arange(LANES)[:, None]
    expand = ((rows < 3 * SSD_HEADS) & (rows % SSD_HEADS == head_of_lane[None, :])).astype(BF16)
    dsk = jnp.repeat(d_skip.astype(F32), SSD_HEAD_DIM).reshape(1, SSD_INNER)
    return pl.pallas_call(
        _ssd_kernel,
        grid=(batch, nc),
        in_specs=[pl.BlockSpec((L, SSD_XBC), row),
                  pl.BlockSpec((L, SSD_INNER), row),
                  pl.BlockSpec((L, LANES), row),
                  pl.BlockSpec((SSD_CONV, SSD_XBC), const),
                  pl.BlockSpec((1, SSD_XBC), const),
                  pl.BlockSpec((1, LANES), const),
                  pl.BlockSpec((1, LANES), const),
                  pl.BlockSpec((1, SSD_INNER), const),
                  pl.BlockSpec((1, SSD_INNER), const),
                  pl.BlockSpec(expand.shape, const)],
        out_specs=pl.BlockSpec((L, SSD_INNER), row),
        out_shape=jax.ShapeDtypeStruct((t, SSD_INNER), BF16),
        scratch_shapes=[pltpu.VMEM((16, SSD_XBC), BF16),
                        pltpu.VMEM((SSD_STATE, SSD_INNER), F32)],
        compiler_params=_params("arbitrary", "arbitrary"),
        name="ssd",
    )(xbc, z, dt_raw, conv_w, conv_b.reshape(1, -1), pad(dt_bias), pad(a_log), dsk,
      ssd_norm.reshape(1, -1), expand)


def _attn_kernel(q_ref, k_ref, vt_ref, gn_ref, o_ref, s_ref):
    blk = ATT_BLOCK
    per_step = q_ref.shape[0] // blk
    for sub in range(per_step):
        _attn_query_block(pl.program_id(1) * per_step + sub, pl.ds(sub * blk, blk),
                          q_ref, k_ref, vt_ref, gn_ref, o_ref, s_ref)


def _attn_query_block(qi, q_rows, q_ref, k_ref, vt_ref, gn_ref, o_ref, s_ref):
    blk = ATT_BLOCK
    causal = (lax.broadcasted_iota(jnp.int32, (blk, blk), 0)
              <= lax.broadcasted_iota(jnp.int32, (blk, blk), 1))
    dims = (((1,), (1,)), ((), ()))
    qs = [q_ref[q_rows, hd * HEAD_PAD:(hd + 1) * HEAD_PAD] for hd in range(MLA_HEADS)]

    def step(j, carry, nblk, masked):
        keys = nblk * blk
        rows = pl.ds(pl.multiple_of(j * blk, blk), keys)
        m_new = []
        for hd in range(MLA_HEADS):
            kk = k_ref[rows, hd * HEAD_PAD:(hd + 1) * HEAD_PAD]
            s = lax.dot_general(kk, qs[hd], dims, preferred_element_type=F32)
            if masked:
                s = jnp.where(causal, s, -jnp.inf)
            s_ref[hd, 0:keys, :] = s
            m_new.append(jnp.maximum(carry[hd][0], jnp.max(s, axis=0, keepdims=True)))
        new = []
        for hd in range(MLA_HEADS):
            m_old, acc = carry[hd]
            p = jnp.exp2(s_ref[hd, 0:keys, :] - m_new[hd]).astype(BF16)
            acc = acc * jnp.exp2(m_old - m_new[hd])
            for b in range(nblk):
                vt = vt_ref[j + b, hd * MLA_V_ROWS:(hd + 1) * MLA_V_ROWS, :]
                acc = acc + jnp.dot(vt, p[b * blk:(b + 1) * blk], preferred_element_type=F32)
            new.append((m_new[hd], acc))
        return tuple(new)

    neg = jnp.full((1, blk), -jnp.inf, F32)
    carry = tuple((neg, jnp.zeros((MLA_V_ROWS, blk), F32)) for _ in range(MLA_HEADS))
    carry = lax.fori_loop(0, qi // 2, lambda u, c: step(2 * u, c, 2, False), carry)
    carry = lax.fori_loop(0, lax.rem(qi, 2), lambda _, c: step(qi - 1, c, 1, False), carry)
    final = step(qi, carry, 1, True)
    y_t = jnp.concatenate([acc[:MLA_V] / acc[MLA_V:MLA_V + 1] for _, acc in final], axis=0)
    o_ref[q_rows, :] = _rms(y_t.T, gn_ref[...]).astype(BF16)


def _attention(q, k, v_t, gn, batch, seq):
    t = q.shape[0]
    blk = ATT_BLOCK
    nq = seq // blk
    rows = ATT_STEP_BLOCKS * blk
    steps = seq // rows
    nqk = MLA_HEADS * HEAD_PAD
    return pl.pallas_call(
        _attn_kernel,
        grid=(batch, steps),
        in_specs=[pl.BlockSpec((rows, nqk), lambda b, i: (b * steps + i, 0)),
                  pl.BlockSpec((seq, nqk), lambda b, i: (b, 0)),
                  pl.BlockSpec((nq, MLA_HEADS * MLA_V_ROWS, blk), lambda b, i: (b, 0, 0)),
                  pl.BlockSpec((1, MLA_INNER), lambda b, i: (0, 0))],
        out_specs=pl.BlockSpec((rows, MLA_INNER), lambda b, i: (b * steps + i, 0)),
        out_shape=jax.ShapeDtypeStruct((t, MLA_INNER), BF16),
        scratch_shapes=[pltpu.VMEM((MLA_HEADS, 2 * blk, blk), F32)],
        compiler_params=_params("arbitrary", "arbitrary"),
        name="attention",
    )(q, k, v_t, gn)


def _outproj_kernel(ys_ref, ym_ref, x_ref, mod_ref, wtop_ref, wbot_ref, gpm_ref, gpf_ref,
                    wrh_ref, wrl_ref, br_ref, x1_ref, h2_ref, col_ref, info_ref, cnt_ref, ce_ref):
    y = (jnp.dot(ys_ref[...], wtop_ref[...], preferred_element_type=F32)
         + jnp.dot(ym_ref[...], wbot_ref[...], preferred_element_type=F32))
    g1 = mod_ref[0, 2:3, :]
    sh2 = mod_ref[0, 3:4, :]
    sc2 = mod_ref[0, 4:5, :]
    x1 = x_ref[...] + g1 * _rms(y, gpm_ref[...])
    x1_ref[...] = x1
    h2 = _rms(x1, gpf_ref[...]) * (1.0 + sc2) + sh2
    tm = h2.shape[0]
    h_hi = h2.astype(BF16)
    h_lo = (h2 - h_hi.astype(F32)).astype(BF16)
    logits = (jnp.dot(h_hi, wrh_ref[...], preferred_element_type=F32)
              + jnp.dot(h_lo, wrh_ref[...], preferred_element_type=F32)
              + jnp.dot(h_hi, wrl_ref[...], preferred_element_type=F32)) + br_ref[...]
    lt = logits.T
    row = lax.broadcasted_iota(jnp.int32, lt.shape, 0).astype(F32)
    ninf = -jnp.inf
    big = 1e9
    is_g = row < MOE_GROUPS
    gl = jnp.where(is_g, lt, ninf)
    gmax = jnp.max(gl, axis=0, keepdims=True)
    gsum = jnp.sum(jnp.where(is_g, jnp.exp(lt - gmax), 0.0), axis=0, keepdims=True)
    g_w = 1.0 / gsum
    g_idx = jnp.min(jnp.where(gl == gmax, row, big), axis=0, keepdims=True)
    first = MOE_GROUPS + MOE_EXPERTS_PER_GROUP * g_idx
    el = jnp.where(row >= first, jnp.where(row < first + MOE_EXPERTS_PER_GROUP, lt, ninf), ninf)
    m1 = jnp.max(el, axis=0, keepdims=True)
    i1 = jnp.min(jnp.where(el == m1, row, big), axis=0, keepdims=True)
    el2 = jnp.where(row == i1, ninf, el)
    m2 = jnp.max(el2, axis=0, keepdims=True)
    i2 = jnp.min(jnp.where(el2 == m2, row, big), axis=0, keepdims=True)
    r = jnp.exp(m2 - m1)
    w1 = g_w / (1.0 + r)
    w2 = g_w * r / (1.0 + r)
    sel1 = row == i1
    sel2 = row == i2
    both = jnp.where(sel1, 1.0, jnp.where(sel2, 1.0, 0.0))
    cnt16 = jnp.floor((jnp.sum(both, axis=1, keepdims=True) + (MOE_CHUNK - 1)) * (1.0 / MOE_CHUNK))
    er = lax.broadcasted_iota(jnp.int32, (LANES, LANES), 0)
    ec = lax.broadcasted_iota(jnp.int32, (LANES, LANES), 1)
    cnt16_b = jnp.broadcast_to(cnt16, (LANES, LANES))
    off16 = jnp.dot((ec < er).astype(BF16), cnt16_b.astype(BF16), preferred_element_type=F32)[:, 0:1]
    earlier = (lax.broadcasted_iota(jnp.int32, (tm, tm), 0)
               < lax.broadcasted_iota(jnp.int32, (tm, tm), 1)).astype(BF16)
    base = off16 * MOE_CHUNK + jnp.dot(both.astype(BF16), earlier, preferred_element_type=F32)
    lpos1 = jnp.sum(jnp.where(sel1, base, 0.0), axis=0, keepdims=True)
    lpos2 = jnp.sum(jnp.where(sel2, base, 0.0), axis=0, keepdims=True)
    h2_ref[...] = h_hi
    col_ref[...] = jnp.where(row == 0, lpos1, jnp.where(row == 1, lpos2, 0.0)).T
    row8 = lax.broadcasted_iota(jnp.int32, info_ref.shape, 0)
    info_ref[...] = jnp.where(row8 == 0, lpos1, jnp.where(row8 == 1, lpos2,
                              jnp.where(row8 == 2, w1, jnp.where(row8 == 3, w2, 0.0))))
    cnt_ref[...] = cnt16_b
    chunk = lax.broadcasted_iota(jnp.int32, (LANES, LANES), 1).astype(F32)
    ce = jnp.sum(jnp.where(off16 + cnt16 <= chunk, 1.0, 0.0), axis=0, keepdims=True) - MOE_GROUPS
    ce_ref[...] = jnp.broadcast_to(ce, ce_ref.shape)


def _outproj(y_ssd, y_mla, x2, mod3, w_top, w_bot, gpm, gpf, wr_hi, wr_lo, b_router, seq, tm):
    t, d = x2.shape
    per_b = seq // tm
    n_tok_tiles = t // tm
    row = lambda i: (i, 0)
    const = lambda i: (0, 0)
    return pl.pallas_call(
        _outproj_kernel,
        grid=(n_tok_tiles,),
        in_specs=[pl.BlockSpec((tm, SSD_INNER), row),
                  pl.BlockSpec((tm, MLA_INNER), row),
                  pl.BlockSpec((tm, d), row),
                  pl.BlockSpec((1, N_MOD, d), lambda i: (i // per_b, 0, 0)),
                  pl.BlockSpec(w_top.shape, const),
                  pl.BlockSpec(w_bot.shape, const),
                  pl.BlockSpec((1, d), const),
                  pl.BlockSpec((1, d), const),
                  pl.BlockSpec(wr_hi.shape, const),
                  pl.BlockSpec(wr_lo.shape, const),
                  pl.BlockSpec((1, LANES), const)],
        out_specs=[pl.BlockSpec((tm, d), row),
                   pl.BlockSpec((tm, d), row),
                   pl.BlockSpec((tm, LANES), row),
                   pl.BlockSpec((8, tm), row),
                   pl.BlockSpec((LANES, LANES), row),
                   pl.BlockSpec((8, LANES), row)],
        out_shape=[jax.ShapeDtypeStruct((t, d), F32),
                   jax.ShapeDtypeStruct((t, d), BF16),
                   jax.ShapeDtypeStruct((t, LANES), F32),
                   jax.ShapeDtypeStruct((n_tok_tiles * 8, tm), F32),
                   jax.ShapeDtypeStruct((n_tok_tiles * LANES, LANES), F32),
                   jax.ShapeDtypeStruct((n_tok_tiles * 8, LANES), F32)],
        compiler_params=_params("arbitrary"),
        name="outproj_router",
    )(y_ssd, y_mla, x2, mod3, w_top, w_bot, gpm, gpf, wr_hi, wr_lo, b_router)


def _moe_rows(n_tok_tiles):
    rows = n_tok_tiles * MOE_SLOTS + MOE_EXPERTS * (MOE_TILE - MOE_CHUNK)
    return (rows + MOE_TILE - 1) // MOE_TILE * MOE_TILE


def _moe_plan(cnt_tiles, chunk_expert, n_tok_tiles, n_rows):
    chunks_per_tile = MOE_SLOTS // MOE_CHUNK
    chunks_per_rows = MOE_TILE // MOE_CHUNK
    cnt = cnt_tiles.reshape(n_tok_tiles, LANES, LANES)[:, MOE_GROUPS:MOE_GROUPS + MOE_EXPERTS, 0]
    cnt = cnt.astype(jnp.int32)
    ce = chunk_expert.reshape(n_tok_tiles, 8, LANES)[:, 0, :chunks_per_tile].astype(jnp.int32)
    used = jnp.sum(cnt, axis=1)
    total = jnp.sum(cnt, axis=0)
    padded = (total + chunks_per_rows - 1) // chunks_per_rows * chunks_per_rows
    e_end = jnp.cumsum(padded)
    e_start = e_end - padded
    run_global = e_start[None, :] + jnp.cumsum(cnt, axis=0) - cnt
    run_local = jnp.cumsum(cnt, axis=1) - cnt
    experts = jnp.arange(MOE_EXPERTS, dtype=jnp.int32)
    shift = jnp.sum(jnp.where(ce[:, :, None] == experts, (run_global - run_local)[:, None, :], 0), axis=-1)
    local = jnp.arange(chunks_per_tile, dtype=jnp.int32)[None, :]
    dst = jnp.where(local < used[:, None], shift + local, 0)
    n_row_tiles = n_rows // MOE_TILE
    n_used = e_end[-1] // chunks_per_rows
    j = jnp.minimum(jnp.arange(n_row_tiles, dtype=jnp.int32), jnp.maximum(n_used - 1, 0))
    tile_expert = jnp.sum((e_end[None, :] // chunks_per_rows <= j[:, None]).astype(jnp.int32), axis=1)
    tile_expert = jnp.minimum(tile_expert, MOE_EXPERTS - 1)
    i32 = lambda v: v.astype(jnp.int32)
    return (i32(dst.reshape(-1)), i32(used), i32(e_start + total), i32(padded - total),
            i32(n_used.reshape(1)), i32(tile_expert))


def _chunk_rows(chunk):
    return pl.ds(pl.multiple_of(chunk * MOE_CHUNK, MOE_CHUNK), MOE_CHUNK)


def _run_copies(*groups):
    for wait in (False, True):
        for lo, hi, make_copy in groups:
            def body(k, carry, wait=wait, make_copy=make_copy):
                cp = make_copy(k)
                cp.wait() if wait else cp.start()
                return carry
            lax.fori_loop(lo, hi, body, 0)


def _scatter_chunks_kernel(dst_ref, used_ref, pad_start_ref, pad_n_ref, nused_ref, h2_ref, info_ref,
                           o_ref, wslot_ref, src_ref, zero_ref, sem):
    i = pl.program_id(0)
    chunks = MOE_SLOTS // MOE_CHUNK
    slot = lax.rem(i, 2)

    info = info_ref[...]
    srow = lax.broadcasted_iota(jnp.int32, (MOE_SLOTS, info.shape[1]), 0)
    hit1 = srow == info[0:1, :].astype(jnp.int32)
    hit2 = srow == info[1:2, :].astype(jnp.int32)
    perm = jnp.where(hit1, 1.0, jnp.where(hit2, 1.0, 0.0)).astype(BF16)
    src_ref[slot] = jnp.dot(perm, h2_ref[...], preferred_element_type=F32).astype(BF16)
    wslot = jnp.sum(jnp.where(hit1, info[2:3, :], jnp.where(hit2, info[3:4, :], 0.0)), axis=1, keepdims=True)
    wslot_ref[...] = jnp.broadcast_to(wslot, wslot_ref.shape)

    @pl.when(i == 0)
    def _():
        zero_ref[...] = jnp.zeros_like(zero_ref)
        pads = [(0, pad_n_ref[e], lambda c, e=e: pltpu.make_async_copy(
            zero_ref.at[pl.ds(0, MOE_CHUNK), :], o_ref.at[_chunk_rows(pad_start_ref[e] + c), :], sem.at[2]))
            for e in range(MOE_EXPERTS)]
        tail = (nused_ref[0], o_ref.shape[0] // MOE_TILE, lambda j: pltpu.make_async_copy(
            zero_ref, o_ref.at[pl.ds(pl.multiple_of(j * MOE_TILE, MOE_TILE), MOE_TILE), :], sem.at[2]))
        _run_copies(tail, *pads)

    def chunk_copies(tile, buf, wait):
        def body(c, carry):
            cp = pltpu.make_async_copy(src_ref.at[buf, _chunk_rows(c), :],
                                       o_ref.at[_chunk_rows(dst_ref[tile * chunks + c]), :], sem.at[buf])
            cp.wait() if wait else cp.start()
            return carry
        lax.fori_loop(0, used_ref[tile], body, 0)

    chunk_copies(i, slot, False)

    @pl.when(i > 0)
    def _():
        chunk_copies(i - 1, 1 - slot, True)

    @pl.when(i == pl.num_programs(0) - 1)
    def _():
        chunk_copies(i, slot, True)


def _scatter_chunks(h2, info, dst, used, pad_start, pad_n, n_used, n_rows):
    t, d = h2.shape
    tm = info.shape[1]
    row = lambda i, *_: (i, 0)
    return pl.pallas_call(
        _scatter_chunks_kernel,
        grid_spec=pltpu.PrefetchScalarGridSpec(
            num_scalar_prefetch=5,
            grid=(used.shape[0],),
            in_specs=[pl.BlockSpec((tm, d), row), pl.BlockSpec((8, tm), row)],
            out_specs=[pl.BlockSpec(memory_space=pl.ANY), pl.BlockSpec((MOE_SLOTS, LANES), row)],
            scratch_shapes=[pltpu.VMEM((2, MOE_SLOTS, d), BF16), pltpu.VMEM((MOE_TILE, d), BF16),
                            pltpu.SemaphoreType.DMA((3,))]),
        out_shape=[jax.ShapeDtypeStruct((n_rows, d), BF16),
                   jax.ShapeDtypeStruct((used.shape[0] * MOE_SLOTS, LANES), F32)],
        compiler_params=_params("arbitrary"),
        name="moe_scatter_chunks",
    )(dst, used, pad_start, pad_n, n_used, h2, info)


def _experts_kernel(te_ref, nt_ref, xs_ref, wg_ref, wu_ref, wd_ref, o_ref):
    used = pl.program_id(0) < nt_ref[0]

    @pl.when(used)
    def _():
        x = xs_ref[...]
        gate = jnp.dot(x, wg_ref[0].astype(BF16), preferred_element_type=F32)
        up = jnp.dot(x, wu_ref[0].astype(BF16), preferred_element_type=F32)
        hid = (_silu(gate) * up).astype(BF16)
        o_ref[...] = jnp.dot(hid, wd_ref[0].astype(BF16), preferred_element_type=F32).astype(BF16)

    @pl.when(jnp.logical_not(used))
    def _():
        o_ref[...] = jnp.zeros_like(o_ref)


def _experts(xs, tile_expert, n_used, w_gate, w_up, w_down):
    n_rows, d = xs.shape
    tile = MOE_TILE
    by_expert = lambda j, te, nt: (te[j], 0, 0)
    return pl.pallas_call(
        _experts_kernel,
        grid_spec=pltpu.PrefetchScalarGridSpec(
            num_scalar_prefetch=2,
            grid=(n_rows // tile,),
            in_specs=[pl.BlockSpec((tile, d), lambda j, te, nt: (jnp.maximum(jnp.minimum(j, nt[0] - 1), 0), 0)),
                      pl.BlockSpec((1, d, MOE_FF), by_expert),
                      pl.BlockSpec((1, d, MOE_FF), by_expert),
                      pl.BlockSpec((1, MOE_FF, d), by_expert)],
            out_specs=pl.BlockSpec((tile, d), lambda j, te, nt: (j, 0))),
        out_shape=jax.ShapeDtypeStruct((n_rows, d), BF16),
        compiler_params=_params("arbitrary"),
        name="moe_experts",
    )(tile_expert, n_used, xs, w_gate, w_up, w_down)


def _combine_kernel(dst_ref, used_ref, wslot_ref, col_ref, x1_ref, mod_ref, gpost_ref, ys_ref, o_ref,
                    ysl_ref, sem):
    i = pl.program_id(0)
    chunks = MOE_SLOTS // MOE_CHUNK
    slot = lax.rem(i, 2)

    def fetch(tile, buf, wait):
        base = tile * chunks

        def copy(c, carry):
            cp = pltpu.make_async_copy(ys_ref.at[_chunk_rows(dst_ref[base + c]), :],
                                       ysl_ref.at[buf, _chunk_rows(c), :], sem.at[buf])
            cp.wait() if wait else cp.start()
            return carry

        lax.fori_loop(0, chunks, copy, 0, unroll=8)

    @pl.when(i == 0)
    def _():
        fetch(0, 0, False)

    @pl.when(i + 1 < pl.num_programs(0))
    def _():
        fetch(i + 1, 1 - slot, False)

    col = col_ref[...]
    tm = col.shape[0]
    lane = lax.broadcasted_iota(jnp.int32, (tm, MOE_SLOTS), 1)
    hit = jnp.where(lane == col[:, 0:1].astype(jnp.int32), 1.0,
                    jnp.where(lane == col[:, 1:2].astype(jnp.int32), 1.0, 0.0)).astype(BF16)
    fetch(i, slot, True)
    in_use = lax.broadcasted_iota(jnp.int32, (MOE_SLOTS, 1), 0) < used_ref[i] * MOE_CHUNK
    ysw = jnp.where(in_use, ysl_ref[slot].astype(F32) * wslot_ref[:, 0:1], 0.0).astype(BF16)
    y = jnp.dot(hit, ysw, preferred_element_type=F32)
    g2 = mod_ref[0, 5:6, :]
    o_ref[...] = x1_ref[...] + g2 * _rms(y, gpost_ref[...])


def _combine(ys, dst, used, wslot, col, x1, mod3, gpost, seq, tm):
    t, d = x1.shape
    per_b = seq // tm
    row = lambda i, *_: (i, 0)
    return pl.pallas_call(
        _combine_kernel,
        grid_spec=pltpu.PrefetchScalarGridSpec(
            num_scalar_prefetch=2,
            grid=(t // tm,),
            in_specs=[pl.BlockSpec((MOE_SLOTS, LANES), row),
                      pl.BlockSpec((tm, LANES), row),
                      pl.BlockSpec((tm, d), row),
                      pl.BlockSpec((1, N_MOD, d), lambda i, *_: (i // per_b, 0, 0)),
                      pl.BlockSpec((1, d), lambda i, *_: (0, 0)),
                      pl.BlockSpec(memory_space=pl.ANY)],
            out_specs=pl.BlockSpec((tm, d), row),
            scratch_shapes=[pltpu.VMEM((2, MOE_SLOTS, d), BF16), pltpu.SemaphoreType.DMA((2,))]),
        out_shape=jax.ShapeDtypeStruct((t, d), F32),
        compiler_params=_params("arbitrary"),
        name="moe_combine",
    )(dst, used, wslot, col, x1, mod3, gpost, ys)


def kernel(x, c, positions, ada_w, ada_b, pre_norm_mix, post_norm_mix, pre_norm_ffn, post_norm_ffn, w_in, conv_w, conv_b, dt_bias, a_log, d_skip, ssd_norm, q_norm, w_uq, kv_norm, w_ukv, mla_out_norm, w_out, w_group_router, b_group_router, w_expert_router, b_expert_router, w_gate, w_up, w_down):
    batch, seq, d = x.shape
    t = batch * seq
    depth = ada_w.shape[0]
    tm = min(INPROJ_TOKENS, seq)
    cos_t, sin_t = _rope_tables(positions)

    x2 = x.reshape(t, d)
    for l in range(depth):
        mod3 = _modulation(c, ada_w[l], ada_b[l]).reshape(batch, N_MOD, d)
        w_main, w_rest, wq2, wk, wv_t = _inproj_weights(w_in[l], w_uq[l], w_ukv[l])
        z, xbc, dt_raw, q, k, v_t = _inproj(
            x2, mod3, pre_norm_mix[l].reshape(1, d), w_main, w_rest, q_norm[l].reshape(1, -1), wq2,
            kv_norm[l].reshape(1, -1), wk, wv_t, cos_t, sin_t, seq, tm)
        y_ssd = _ssd(xbc, z, dt_raw, conv_w[l], conv_b[l], dt_bias[l], a_log[l], d_skip[l],
                     ssd_norm[l], batch, seq)
        y_mla = _attention(q, k, v_t, mla_out_norm[l].reshape(1, -1), batch, seq)
        w_o = w_out[l].astype(BF16)
        pad_r = LANES - MOE_GROUPS - MOE_EXPERTS
        w_router = jnp.concatenate([w_group_router[l], w_expert_router[l], jnp.zeros((d, pad_r), F32)], axis=1)
        b_router = jnp.concatenate([b_group_router[l].reshape(-1), b_expert_router[l].reshape(-1),
                                    jnp.zeros((pad_r,), F32)]).reshape(1, LANES)
        wr_hi = w_router.astype(BF16)
        wr_lo = (w_router - wr_hi.astype(F32)).astype(BF16)
        x1, h2, col, info, cnt_tiles, chunk_expert = _outproj(
            y_ssd, y_mla, x2, mod3, w_o[:SSD_INNER], w_o[SSD_INNER:], post_norm_mix[l].reshape(1, d),
            pre_norm_ffn[l].reshape(1, d), wr_hi, wr_lo, b_router, seq, MOE_TOKENS)
        n_tok_tiles = t // MOE_TOKENS
        n_rows = _moe_rows(n_tok_tiles)
        dst, used, pad_start, pad_n, n_used, tile_expert = _moe_plan(cnt_tiles, chunk_expert, n_tok_tiles, n_rows)
        xs, wslot = _scatter_chunks(h2, info, dst, used, pad_start, pad_n, n_used, n_rows)
        ys = _experts(xs, tile_expert, n_used, w_gate[l], w_up[l], w_down[l])
        x2 = _combine(ys, dst, used, wslot, col, x1, mod3, post_norm_ffn[l].reshape(1, d), seq, MOE_TOKENS)
    return x2.reshape(batch, seq, d)
```

```python
import functools
import math

import jax
import jax.numpy as jnp
from jax import lax
from jax.experimental import pallas as pl
from jax.experimental.pallas import tpu as pltpu

F32 = jnp.float32
BF16 = jnp.bfloat16

D_MODEL = 1024
SSD_HEADS = 8
SSD_HEAD_DIM = 64
SSD_INNER = SSD_HEADS * SSD_HEAD_DIM
SSD_GROUPS = 2
SSD_STATE = 128
SSD_CONV = 4
SSD_CHUNK = 128
SSD_XBC = SSD_INNER + 2 * SSD_GROUPS * SSD_STATE
MLA_HEADS = 8
MLA_NOPE = 64
MLA_ROPE = 32
MLA_QK = MLA_NOPE + MLA_ROPE
MLA_V = 64
MLA_Q_RANK = 256
MLA_KV_RANK = 128
MLA_INNER = MLA_HEADS * MLA_V
ROPE_THETA = 10000.0
MOE_GROUPS = 4
MOE_EXPERTS_PER_GROUP = 8
MOE_EXPERTS = MOE_GROUPS * MOE_EXPERTS_PER_GROUP
MOE_FF = 256
N_MOD = 6
EPS = 1e-6
LOG2_E = math.log2(math.e)

LANES = 128
HEAD_PAD = 128
ATT_BLOCK = 256
MLA_V_ROWS = MLA_V + 16
INPROJ_TOKENS = 512
SSD_STEP_ROWS = 8 * SSD_CHUNK
MOE_TOKENS = 512
MOE_CHUNK = 16
MOE_TILE = 512
MOE_SLOTS = 2 * MOE_TOKENS + MOE_EXPERTS * MOE_CHUNK
VMEM_LIMIT = 48 * 1024 * 1024

_C_Z = 0
_C_XBC = _C_Z + SSD_INNER
_C_CQ = _C_XBC + SSD_XBC
_C_CKV = _C_CQ + MLA_Q_RANK
_C_KR = _C_CKV + MLA_KV_RANK
_C_KRS = _C_KR + LANES
_C_DT = _C_KRS + LANES
_C_END = _C_DT + LANES


def _silu(v):
    return v * (1.0 / (1.0 + jnp.exp(-v)))


def _rms(v, gain):
    return v * lax.rsqrt(jnp.mean(v * v, axis=-1, keepdims=True) + EPS) * gain


def _params(*sem, flags=None):
    return pltpu.CompilerParams(dimension_semantics=sem, vmem_limit_bytes=VMEM_LIMIT, flags=flags)


def _mod_kernel(c_ref, w_ref, b_ref, o_ref):
    act = _silu(c_ref[...])
    a_hi = act.astype(BF16)
    a_lo = (act - a_hi.astype(F32)).astype(BF16)
    w = w_ref[...]
    w_hi = w.astype(BF16)
    w_lo = (w - w_hi.astype(F32)).astype(BF16)
    o_ref[...] = (jnp.dot(a_hi, w_hi, preferred_element_type=F32)
                  + jnp.dot(a_lo, w_hi, preferred_element_type=F32)
                  + jnp.dot(a_hi, w_lo, preferred_element_type=F32)) + b_ref[...]


def _modulation(c, ada_w, ada_b):
    b, d = c.shape
    n = ada_w.shape[1]
    return pl.pallas_call(
        _mod_kernel,
        grid=(n // d,),
        in_specs=[pl.BlockSpec((b, d), lambda j: (0, 0)),
                  pl.BlockSpec((d, d), lambda j: (0, j)),
                  pl.BlockSpec((1, d), lambda j: (0, j))],
        out_specs=pl.BlockSpec((b, d), lambda j: (0, j)),
        out_shape=jax.ShapeDtypeStruct((b, n), F32),
        compiler_params=_params("arbitrary"),
        name="modulation",
    )(c, ada_w, ada_b.reshape(1, n))


def _rope_kernel(pos_ref, inv_ref, cos_ref, sin_ref):
    ang = pos_ref[...].astype(F32) * inv_ref[...]
    blk = ang.shape[0]
    half = MLA_ROPE // 2
    per_row = LANES // half
    lane = lax.broadcasted_iota(jnp.int32, (blk * per_row, LANES), 1)
    first = (lane >= MLA_NOPE) & (lane < MLA_NOPE + half)
    second = (lane >= MLA_NOPE + half) & (lane < MLA_QK)
    for fn, out_ref, fill in ((jnp.cos, cos_ref, 1.0), (jnp.sin, sin_ref, 0.0)):
        compact = fn(ang)
        rep = jnp.broadcast_to(compact[:, None, :], (blk, per_row, LANES)).reshape(blk * per_row, LANES)
        a = pltpu.roll(rep, 0, 1, stride=half, stride_axis=0)
        b = pltpu.roll(a, half, 1)
        out_ref[...] = jnp.where(first, a, jnp.where(second, b, fill))


def _rope_tables(positions):
    t = positions.size
    half = MLA_ROPE // 2
    per_row = LANES // half
    inv = 1.0 / (ROPE_THETA ** (jnp.arange(0, MLA_ROPE, 2, dtype=F32) / MLA_ROPE))
    group_token = (MLA_NOPE // half - jnp.arange(per_row)) % per_row
    pos_rep = jnp.repeat(positions.reshape(t // per_row, per_row)[:, group_token], half, axis=1)
    inv_t = jnp.tile(inv, per_row).reshape(1, LANES)
    rows = t // per_row
    blk = min(rows, 512)
    return pl.pallas_call(
        _rope_kernel,
        grid=(rows // blk,),
        in_specs=[pl.BlockSpec((blk, LANES), lambda i: (i, 0)),
                  pl.BlockSpec((1, LANES), lambda i: (0, 0))],
        out_specs=[pl.BlockSpec((blk * per_row, LANES), lambda i: (i, 0))] * 2,
        out_shape=[jax.ShapeDtypeStruct((t, LANES), F32)] * 2,
        compiler_params=_params("arbitrary"),
        name="rope_tables",
    )(pos_rep, inv_t)


def _inproj_kernel(x_ref, mod_ref, gpre_ref, wmain_ref, wrest_ref, qn_ref, wq_ref, kvn_ref, wk_ref,
                   wvt_ref, cos_ref, sin_ref, z_ref, xbc_ref, dt_ref, q_ref, k_ref, vt_ref):
    x = x_ref[...]
    sh = mod_ref[0, 0:1, :]
    sc = mod_ref[0, 1:2, :]
    h = (_rms(x, gpre_ref[...]) * (1.0 + sc) + sh).astype(BF16)
    u = jnp.concatenate([jnp.dot(h, wmain_ref[...], preferred_element_type=F32),
                         jnp.dot(h, wrest_ref[...], preferred_element_type=F32)], axis=1)
    z_ref[...] = u[:, _C_Z:_C_XBC].astype(BF16)
    xbc_ref[...] = u[:, _C_XBC:_C_CQ].astype(BF16)
    dt_ref[...] = u[:, _C_DT:_C_END]
    cos_t = cos_ref[...]
    sin_t = sin_ref[...]
    cq = _rms(u[:, _C_CQ:_C_CKV], qn_ref[...]).astype(BF16)
    q2 = jnp.dot(cq, wq_ref[...], preferred_element_type=F32)
    scale = MLA_QK ** -0.5 * LOG2_E
    nq = MLA_HEADS * HEAD_PAD
    for hd in range(MLA_HEADS):
        a = q2[:, hd * HEAD_PAD:(hd + 1) * HEAD_PAD]
        b = q2[:, nq + hd * HEAD_PAD:nq + (hd + 1) * HEAD_PAD]
        q_ref[:, hd * HEAD_PAD:(hd + 1) * HEAD_PAD] = ((a * cos_t + b * sin_t) * scale).astype(BF16)
    ckv = _rms(u[:, _C_CKV:_C_KR], kvn_ref[...]).astype(BF16)
    kn = jnp.dot(ckv, wk_ref[...], preferred_element_type=F32)
    k_pe = u[:, _C_KR:_C_KRS] * cos_t + u[:, _C_KRS:_C_DT] * sin_t
    for hd in range(MLA_HEADS):
        k_ref[:, hd * HEAD_PAD:(hd + 1) * HEAD_PAD] = (
            kn[:, hd * HEAD_PAD:(hd + 1) * HEAD_PAD] + k_pe).astype(BF16)
    v_t = lax.dot_general(wvt_ref[...], ckv, (((1,), (1,)), ((), ())), preferred_element_type=F32)
    head_row = lax.rem(lax.broadcasted_iota(jnp.int32, (v_t.shape[0], 1), 0), MLA_V_ROWS)
    v_t = v_t + jnp.where(head_row == MLA_V, 1.0, 0.0)
    for s in range(vt_ref.shape[0]):
        vt_ref[s] = v_t[:, s * ATT_BLOCK:(s + 1) * ATT_BLOCK].astype(BF16)


def _inproj_weights(w_in, w_uq, w_ukv):
    d = w_in.shape[0]
    half = MLA_ROPE // 2
    o_z, o_xbc, o_dt = 0, SSD_INNER, SSD_INNER + SSD_XBC
    o_cq = o_dt + SSD_HEADS
    o_ckv = o_cq + MLA_Q_RANK
    o_kr = o_ckv + MLA_KV_RANK
    zeros = lambda n: jnp.zeros((d, n), F32)
    kr = w_in[:, o_kr:o_kr + MLA_ROPE]
    kr_blk = jnp.concatenate([zeros(MLA_NOPE), kr, zeros(HEAD_PAD - MLA_QK)], axis=1)
    krs_blk = jnp.concatenate([zeros(MLA_NOPE), -kr[:, half:], kr[:, :half], zeros(HEAD_PAD - MLA_QK)], axis=1)
    dt_blk = jnp.concatenate([w_in[:, o_dt:o_dt + SSD_HEADS], zeros(LANES - SSD_HEADS)], axis=1)
    w_main = w_in[:, o_z:o_dt]
    w_rest = jnp.concatenate([w_in[:, o_cq:o_kr], kr_blk, krs_blk, dt_blk], axis=1)
    r = w_uq.shape[0]
    zq = jnp.zeros((r, MLA_HEADS, HEAD_PAD - MLA_QK), F32)
    zn = jnp.zeros((r, MLA_HEADS, MLA_NOPE), F32)
    wq_plain = jnp.concatenate([w_uq, zq], axis=2).reshape(r, MLA_HEADS * HEAD_PAD)
    wq_rot = jnp.concatenate([zn, -w_uq[:, :, MLA_NOPE + half:], w_uq[:, :, MLA_NOPE:MLA_NOPE + half], zq],
                             axis=2).reshape(r, MLA_HEADS * HEAD_PAD)
    wq2 = jnp.concatenate([wq_plain, wq_rot], axis=1)
    rk = w_ukv.shape[0]
    zk = jnp.zeros((rk, MLA_HEADS, HEAD_PAD - MLA_NOPE), F32)
    wk = jnp.concatenate([w_ukv[:, :, :MLA_NOPE], zk], axis=2).reshape(rk, MLA_HEADS * HEAD_PAD)
    zv = jnp.zeros((rk, MLA_HEADS, MLA_V_ROWS - MLA_V), F32)
    wv_t = jnp.concatenate([w_ukv[:, :, MLA_NOPE:], zv], axis=2).reshape(rk, MLA_HEADS * MLA_V_ROWS).T
    return w_main.astype(BF16), w_rest.astype(BF16), wq2.astype(BF16), wk.astype(BF16), wv_t.astype(BF16)


def _inproj(x2, mod3, gpre, w_main, w_rest, q_norm, wq2, kv_norm, wk, wv_t, cos_t, sin_t, seq, tm):
    t, d = x2.shape
    per_b = seq // tm
    row = lambda i: (i, 0)
    const = lambda i: (0, 0)
    nqk = MLA_HEADS * HEAD_PAD
    slabs = tm // ATT_BLOCK
    return pl.pallas_call(
        _inproj_kernel,
        grid=(t // tm,),
        in_specs=[pl.BlockSpec((tm, d), row),
                  pl.BlockSpec((1, N_MOD, d), lambda i: (i // per_b, 0, 0)),
                  pl.BlockSpec((1, d), const),
                  pl.BlockSpec(w_main.shape, const),
                  pl.BlockSpec(w_rest.shape, const),
                  pl.BlockSpec((1, MLA_Q_RANK), const),
                  pl.BlockSpec(wq2.shape, const),
                  pl.BlockSpec((1, MLA_KV_RANK), const),
                  pl.BlockSpec(wk.shape, const),
                  pl.BlockSpec(wv_t.shape, const),
                  pl.BlockSpec((tm, LANES), row),
                  pl.BlockSpec((tm, LANES), row)],
        out_specs=[pl.BlockSpec((tm, SSD_INNER), row),
                   pl.BlockSpec((tm, SSD_XBC), row),
                   pl.BlockSpec((tm, LANES), row),
                   pl.BlockSpec((tm, nqk), row),
                   pl.BlockSpec((tm, nqk), row),
                   pl.BlockSpec((slabs, MLA_HEADS * MLA_V_ROWS, ATT_BLOCK), lambda i: (i, 0, 0))],
        out_shape=[jax.ShapeDtypeStruct((t, SSD_INNER), BF16),
                   jax.ShapeDtypeStruct((t, SSD_XBC), BF16),
                   jax.ShapeDtypeStruct((t, LANES), F32),
                   jax.ShapeDtypeStruct((t, nqk), BF16),
                   jax.ShapeDtypeStruct((t, nqk), BF16),
                   jax.ShapeDtypeStruct((t // ATT_BLOCK, MLA_HEADS * MLA_V_ROWS, ATT_BLOCK), BF16)],
        compiler_params=_params("arbitrary"),
        name="inproj",
    )(x2, mod3, gpre, w_main, w_rest, q_norm, wq2, kv_norm, wk, wv_t, cos_t, sin_t)


def _split3_packed(v):
    lane = lax.broadcasted_iota(jnp.int32, v.shape, 1)
    v = jnp.where(lane < SSD_HEADS, v, 0.0)
    hi = v.astype(BF16).astype(F32)
    rest = v - hi
    mid = rest.astype(BF16).astype(F32)
    lo = rest - mid
    return (hi + pltpu.roll(mid, SSD_HEADS, 1) + pltpu.roll(lo, 2 * SSD_HEADS, 1)).astype(BF16)


def _ssd_kernel(xbc_ref, z_ref, dt_ref, cw_ref, cb_ref, dtb_ref, alog_ref, dsk_ref, gn_ref,
                e_ref, y_ref, tail_ref, state_ref):
    @pl.when(pl.program_id(1) == 0)
    def _():
        tail_ref[...] = jnp.zeros_like(tail_ref)
        state_ref[...] = jnp.zeros_like(state_ref)

    for c in range(xbc_ref.shape[0] // SSD_CHUNK):
        _ssd_chunk(pl.ds(c * SSD_CHUNK, SSD_CHUNK), xbc_ref, z_ref, dt_ref, cw_ref, cb_ref, dtb_ref,
                   alog_ref, dsk_ref, gn_ref, e_ref, y_ref, tail_ref, state_ref)


def _ssd_chunk(rows, xbc_ref, z_ref, dt_ref, cw_ref, cb_ref, dtb_ref, alog_ref, dsk_ref, gn_ref,
               e_ref, y_ref, tail_ref, state_ref):
    L = SSD_CHUNK
    gw = SSD_INNER // SSD_GROUPS

    cur_b = xbc_ref[rows, :]
    ext = jnp.concatenate([tail_ref[...], cur_b], axis=0)
    out_row = lax.broadcasted_iota(jnp.int32, ((SSD_CONV - 1) * L, ext.shape[0]), 0)
    src_row = lax.broadcasted_iota(jnp.int32, ((SSD_CONV - 1) * L, ext.shape[0]), 1)
    tail_rows = ext.shape[0] - L
    shift = jnp.where(src_row + (out_row // L + 1) == lax.rem(out_row, L) + tail_rows, 1.0, 0.0)
    shifted = jnp.dot(shift.astype(BF16), ext, preferred_element_type=F32)
    acc = cur_b.astype(F32) * cw_ref[SSD_CONV - 1:SSD_CONV, :] + cb_ref[...]
    for j in range(1, SSD_CONV):
        acc = acc + shifted[(j - 1) * L:j * L] * cw_ref[SSD_CONV - 1 - j:SSD_CONV - j, :]
    tail_ref[...] = cur_b[L - tail_rows:L]
    act = _silu(acc)
    xs = act[:, :SSD_INNER]
    bm = act[:, SSD_INNER:SSD_INNER + SSD_GROUPS * SSD_STATE].astype(BF16)
    cm = act[:, SSD_INNER + SSD_GROUPS * SSD_STATE:].astype(BF16)

    dt_in = dt_ref[rows, :] + dtb_ref[...]
    dt = jnp.maximum(dt_in, 0.0) + jnp.log(1.0 + jnp.exp(-jnp.abs(dt_in)))
    adt = dt * (-jnp.exp(alog_ref[...]))
    ri = lax.broadcasted_iota(jnp.int32, (L, L), 0)
    ci = lax.broadcasted_iota(jnp.int32, (L, L), 1)
    causal = ci <= ri
    acs_p = jnp.dot(causal.astype(BF16), _split3_packed(adt), preferred_element_type=F32)
    a_cs = (acs_p + pltpu.roll(acs_p, LANES - SSD_HEADS, 1)
            + pltpu.roll(acs_p, LANES - 2 * SSD_HEADS, 1))
    a_cs_t = a_cs.T
    cs_all = jnp.dot(_split3_packed(a_cs), e_ref[...], preferred_element_type=F32)
    cs64 = cs_all[:, :SSD_INNER]
    cs128 = cs_all[:, SSD_INNER:]
    dt64 = jnp.dot(_split3_packed(dt), e_ref[:, :SSD_INNER], preferred_element_type=F32)

    xd = xs * dt64
    xd_b = xd.astype(BF16)
    last = cs64[L - 1:L, :]
    xdw = (xd * jnp.exp(last - cs64)).astype(BF16)
    chunk_decay = jnp.exp(last)
    in_decay = jnp.exp(cs64)

    lane = lax.broadcasted_iota(jnp.int32, (L, LANES), 1)
    lo = lane < SSD_HEAD_DIM
    zero_b = jnp.zeros((L, LANES), BF16)
    y_parts = []
    new_states = []
    for g in range(SSD_GROUPS):
        bg = bm[:, g * SSD_STATE:(g + 1) * SSD_STATE]
        cg = cm[:, g * SSD_STATE:(g + 1) * SSD_STATE]
        cb = lax.dot_general(cg, bg, (((1,), (1,)), ((), ())), preferred_element_type=F32)
        prev = state_ref[:, g * gw:(g + 1) * gw]
        y_off = jnp.dot(cg, prev.astype(BF16), preferred_element_type=F32) * in_decay[:, g * gw:(g + 1) * gw]
        st = lax.dot_general(bg, xdw[:, g * gw:(g + 1) * gw], (((0,), (0,)), ((), ())),
                             preferred_element_type=F32)
        new_states.append(prev * chunk_decay[:, g * gw:(g + 1) * gw] + st)
        heads_per_group = SSD_HEADS // SSD_GROUPS
        for pair in range(heads_per_group // 2):
            h0 = g * heads_per_group + 2 * pair
            blk = xd_b[:, h0 * SSD_HEAD_DIM:(h0 + 2) * SSD_HEAD_DIM]
            y_pair = None
            for k in range(2):
                hh = h0 + k
                diff = cs128[:, hh * LANES:(hh + 1) * LANES] - a_cs_t[hh:hh + 1, :]
                decay = jnp.where(causal, jnp.exp(diff), 0.0)
                m = (cb * decay).astype(BF16)
                rhs = jnp.where(lo, blk, zero_b) if k == 0 else jnp.where(lo, zero_b, blk)
                part = jnp.dot(m, rhs, preferred_element_type=F32)
                y_pair = part if y_pair is None else y_pair + part
            c0 = (2 * pair) * SSD_HEAD_DIM
            y_parts.append(y_pair + y_off[:, c0:c0 + LANES])
    state_ref[...] = jnp.concatenate(new_states, axis=1)
    y = jnp.concatenate(y_parts, axis=1) + dsk_ref[...] * xs
    gated = y * _silu(z_ref[rows, :].astype(F32))
    y_ref[rows, :] = _rms(gated, gn_ref[...]).astype(BF16)


def _ssd(xbc, z, dt_raw, conv_w, conv_b, dt_bias, a_log, d_skip, ssd_norm, batch, seq):
    t = xbc.shape[0]
    L = min(SSD_STEP_ROWS, seq)
    nc = seq // L
    row = lambda b, c: (b * nc + c, 0)
    const = lambda b, c: (0, 0)
    pad = lambda v: jnp.concatenate([v.reshape(1, -1), jnp.zeros((1, LANES - v.size), F32)], axis=1)
    head_of_lane = jnp.concatenate([jnp.arange(SSD_INNER) // SSD_HEAD_DIM,
                                    jnp.arange(SSD_HEADS * LANES) // LANES])
    rows = jnp.arange(LANES)[:, None]
    expand = ((rows < 3 * SSD_HEADS) & (rows % SSD_HEADS == head_of_lane[None, :])).astype(BF16)
    dsk = jnp.repeat(d_skip.astype(F32), SSD_HEAD_DIM).reshape(1, SSD_INNER)
    return pl.pallas_call(
        _ssd_kernel,
        grid=(batch, nc),
        in_specs=[pl.BlockSpec((L, SSD_XBC), row),
                  pl.BlockSpec((L, SSD_INNER), row),
                  pl.BlockSpec((L, LANES), row),
                  pl.BlockSpec((SSD_CONV, SSD_XBC), const),
                  pl.BlockSpec((1, SSD_XBC), const),
                  pl.BlockSpec((1, LANES), const),
                  pl.BlockSpec((1, LANES), const),
                  pl.BlockSpec((1, SSD_INNER), const),
                  pl.BlockSpec((1, SSD_INNER), const),
                  pl.BlockSpec(expand.shape, const)],
        out_specs=pl.BlockSpec((L, SSD_INNER), row),
        out_shape=jax.ShapeDtypeStruct((t, SSD_INNER), BF16),
        scratch_shapes=[pltpu.VMEM((16, SSD_XBC), BF16),
                        pltpu.VMEM((SSD_STATE, SSD_INNER), F32)],
        compiler_params=_params("arbitrary", "arbitrary"),
        name="ssd",
    )(xbc, z, dt_raw, conv_w, conv_b.reshape(1, -1), pad(dt_bias), pad(a_log), dsk,
      ssd_norm.reshape(1, -1), expand)


def _attn_kernel(q_ref, k_ref, vt_ref, gn_ref, o_ref, s_ref):
    blk = ATT_BLOCK
    qi = pl.program_id(1)
    causal = (lax.broadcasted_iota(jnp.int32, (blk, blk), 0)
              <= lax.broadcasted_iota(jnp.int32, (blk, blk), 1))
    dims = (((1,), (1,)), ((), ()))
    qs = [q_ref[:, hd * HEAD_PAD:(hd + 1) * HEAD_PAD] for hd in range(MLA_HEADS)]

    def step(j, carry, nblk, masked):
        keys = nblk * blk
        rows = pl.ds(pl.multiple_of(j * blk, blk), keys)
        m_new = []
        for hd in range(MLA_HEADS):
            kk = k_ref[rows, hd * HEAD_PAD:(hd + 1) * HEAD_PAD]
            s = lax.dot_general(kk, qs[hd], dims, preferred_element_type=F32)
            if masked:
                s = jnp.where(causal, s, -jnp.inf)
            s_ref[hd, 0:keys, :] = s
            m_new.append(jnp.maximum(carry[hd][0], jnp.max(s, axis=0, keepdims=True)))
        new = []
        for hd in range(MLA_HEADS):
            m_old, acc = carry[hd]
            p = jnp.exp2(s_ref[hd, 0:keys, :] - m_new[hd]).astype(BF16)
            acc = acc * jnp.exp2(m_old - m_new[hd])
            for b in range(nblk):
                vt = vt_ref[j + b, hd * MLA_V_ROWS:(hd + 1) * MLA_V_ROWS, :]
                acc = acc + jnp.dot(vt, p[b * blk:(b + 1) * blk], preferred_element_type=F32)
            new.append((m_new[hd], acc))
        return tuple(new)

    neg = jnp.full((1, blk), -jnp.inf, F32)
    carry = tuple((neg, jnp.zeros((MLA_V_ROWS, blk), F32)) for _ in range(MLA_HEADS))
    carry = lax.fori_loop(0, qi // 2, lambda u, c: step(2 * u, c, 2, False), carry)
    carry = lax.fori_loop(0, lax.rem(qi, 2), lambda _, c: step(qi - 1, c, 1, False), carry)
    final = step(qi, carry, 1, True)
    y_t = jnp.concatenate([acc[:MLA_V] / acc[MLA_V:MLA_V + 1] for _, acc in final], axis=0)
    o_ref[...] = _rms(y_t.T, gn_ref[...]).astype(BF16)


def _attention(q, k, v_t, gn, batch, seq):
    t = q.shape[0]
    blk = ATT_BLOCK
    nq = seq // blk
    nqk = MLA_HEADS * HEAD_PAD
    return pl.pallas_call(
        _attn_kernel,
        grid=(batch, nq),
        in_specs=[pl.BlockSpec((blk, nqk), lambda b, i: (b * nq + i, 0)),
                  pl.BlockSpec((seq, nqk), lambda b, i: (b, 0)),
                  pl.BlockSpec((nq, MLA_HEADS * MLA_V_ROWS, blk), lambda b, i: (b, 0, 0)),
                  pl.BlockSpec((1, MLA_INNER), lambda b, i: (0, 0))],
        out_specs=pl.BlockSpec((blk, MLA_INNER), lambda b, i: (b * nq + i, 0)),
        out_shape=jax.ShapeDtypeStruct((t, MLA_INNER), BF16),
        scratch_shapes=[pltpu.VMEM((MLA_HEADS, 2 * blk, blk), F32)],
        compiler_params=_params("arbitrary", "arbitrary"),
        name="attention",
    )(q, k, v_t, gn)


def _outproj_kernel(ys_ref, ym_ref, x_ref, mod_ref, wtop_ref, wbot_ref, gpm_ref, gpf_ref,
                    wrh_ref, wrl_ref, br_ref, x1_ref, h2_ref, col_ref, info_ref, cnt_ref, ce_ref):
    y = (jnp.dot(ys_ref[...], wtop_ref[...], preferred_element_type=F32)
         + jnp.dot(ym_ref[...], wbot_ref[...], preferred_element_type=F32))
    g1 = mod_ref[0, 2:3, :]
    sh2 = mod_ref[0, 3:4, :]
    sc2 = mod_ref[0, 4:5, :]
    x1 = x_ref[...] + g1 * _rms(y, gpm_ref[...])
    x1_ref[...] = x1
    h2 = _rms(x1, gpf_ref[...]) * (1.0 + sc2) + sh2
    tm = h2.shape[0]
    h_hi = h2.astype(BF16)
    h_lo = (h2 - h_hi.astype(F32)).astype(BF16)
    logits = (jnp.dot(h_hi, wrh_ref[...], preferred_element_type=F32)
              + jnp.dot(h_lo, wrh_ref[...], preferred_element_type=F32)
              + jnp.dot(h_hi, wrl_ref[...], preferred_element_type=F32)) + br_ref[...]
    lt = logits.T
    row = lax.broadcasted_iota(jnp.int32, lt.shape, 0).astype(F32)
    ninf = -jnp.inf
    big = 1e9
    is_g = row < MOE_GROUPS
    gl = jnp.where(is_g, lt, ninf)
    gmax = jnp.max(gl, axis=0, keepdims=True)
    gsum = jnp.sum(jnp.where(is_g, jnp.exp(lt - gmax), 0.0), axis=0, keepdims=True)
    g_w = 1.0 / gsum
    g_idx = jnp.min(jnp.where(gl == gmax, row, big), axis=0, keepdims=True)
    first = MOE_GROUPS + MOE_EXPERTS_PER_GROUP * g_idx
    el = jnp.where(row >= first, jnp.where(row < first + MOE_EXPERTS_PER_GROUP, lt, ninf), ninf)
    m1 = jnp.max(el, axis=0, keepdims=True)
    i1 = jnp.min(jnp.where(el == m1, row, big), axis=0, keepdims=True)
    el2 = jnp.where(row == i1, ninf, el)
    m2 = jnp.max(el2, axis=0, keepdims=True)
    i2 = jnp.min(jnp.where(el2 == m2, row, big), axis=0, keepdims=True)
    r = jnp.exp(m2 - m1)
    w1 = g_w / (1.0 + r)
    w2 = g_w * r / (1.0 + r)
    sel1 = row == i1
    sel2 = row == i2
    both = jnp.where(sel1, 1.0, jnp.where(sel2, 1.0, 0.0))
    cnt16 = jnp.floor((jnp.sum(both, axis=1, keepdims=True) + (MOE_CHUNK - 1)) * (1.0 / MOE_CHUNK))
    er = lax.broadcasted_iota(jnp.int32, (LANES, LANES), 0)
    ec = lax.broadcasted_iota(jnp.int32, (LANES, LANES), 1)
    cnt16_b = jnp.broadcast_to(cnt16, (LANES, LANES))
    off16 = jnp.dot((ec < er).astype(BF16), cnt16_b.astype(BF16), preferred_element_type=F32)[:, 0:1]
    earlier = (lax.broadcasted_iota(jnp.int32, (tm, tm), 0)
               < lax.broadcasted_iota(jnp.int32, (tm, tm), 1)).astype(BF16)
    base = off16 * MOE_CHUNK + jnp.dot(both.astype(BF16), earlier, preferred_element_type=F32)
    lpos1 = jnp.sum(jnp.where(sel1, base, 0.0), axis=0, keepdims=True)
    lpos2 = jnp.sum(jnp.where(sel2, base, 0.0), axis=0, keepdims=True)
    h2_ref[...] = h_hi
    col_ref[...] = jnp.where(row == 0, lpos1, jnp.where(row == 1, lpos2, 0.0)).T
    row8 = lax.broadcasted_iota(jnp.int32, info_ref.shape, 0)
    info_ref[...] = jnp.where(row8 == 0, lpos1, jnp.where(row8 == 1, lpos2,
                              jnp.where(row8 == 2, w1, jnp.where(row8 == 3, w2, 0.0))))
    cnt_ref[...] = cnt16_b
    chunk = lax.broadcasted_iota(jnp.int32, (LANES, LANES), 1).astype(F32)
    ce = jnp.sum(jnp.where(off16 + cnt16 <= chunk, 1.0, 0.0), axis=0, keepdims=True) - MOE_GROUPS
    ce_ref[...] = jnp.broadcast_to(ce, ce_ref.shape)


def _outproj(y_ssd, y_mla, x2, mod3, w_top, w_bot, gpm, gpf, wr_hi, wr_lo, b_router, seq, tm):
    t, d = x2.shape
    per_b = seq // tm
    n_tok_tiles = t // tm
    row = lambda i: (i, 0)
    const = lambda i: (0, 0)
    return pl.pallas_call(
        _outproj_kernel,
        grid=(n_tok_tiles,),
        in_specs=[pl.BlockSpec((tm, SSD_INNER), row),
                  pl.BlockSpec((tm, MLA_INNER), row),
                  pl.BlockSpec((tm, d), row),
                  pl.BlockSpec((1, N_MOD, d), lambda i: (i // per_b, 0, 0)),
                  pl.BlockSpec(w_top.shape, const),
                  pl.BlockSpec(w_bot.shape, const),
                  pl.BlockSpec((1, d), const),
                  pl.BlockSpec((1, d), const),
                  pl.BlockSpec(wr_hi.shape, const),
                  pl.BlockSpec(wr_lo.shape, const),
                  pl.BlockSpec((1, LANES), const)],
        out_specs=[pl.BlockSpec((tm, d), row),
                   pl.BlockSpec((tm, d), row),
                   pl.BlockSpec((tm, LANES), row),
                   pl.BlockSpec((8, tm), row),
                   pl.BlockSpec((LANES, LANES), row),
                   pl.BlockSpec((8, LANES), row)],
        out_shape=[jax.ShapeDtypeStruct((t, d), F32),
                   jax.ShapeDtypeStruct((t, d), BF16),
                   jax.ShapeDtypeStruct((t, LANES), F32),
                   jax.ShapeDtypeStruct((n_tok_tiles * 8, tm), F32),
                   jax.ShapeDtypeStruct((n_tok_tiles * LANES, LANES), F32),
                   jax.ShapeDtypeStruct((n_tok_tiles * 8, LANES), F32)],
        compiler_params=_params("arbitrary"),
        name="outproj_router",
    )(y_ssd, y_mla, x2, mod3, w_top, w_bot, gpm, gpf, wr_hi, wr_lo, b_router)


def _moe_rows(n_tok_tiles):
    rows = n_tok_tiles * MOE_SLOTS + MOE_EXPERTS * (MOE_TILE - MOE_CHUNK)
    return (rows + MOE_TILE - 1) // MOE_TILE * MOE_TILE


def _moe_plan(cnt_tiles, chunk_expert, n_tok_tiles, n_rows):
    chunks_per_tile = MOE_SLOTS // MOE_CHUNK
    chunks_per_rows = MOE_TILE // MOE_CHUNK
    cnt = cnt_tiles.reshape(n_tok_tiles, LANES, LANES)[:, MOE_GROUPS:MOE_GROUPS + MOE_EXPERTS, 0]
    cnt = cnt.astype(jnp.int32)
    ce = chunk_expert.reshape(n_tok_tiles, 8, LANES)[:, 0, :chunks_per_tile].astype(jnp.int32)
    used = jnp.sum(cnt, axis=1)
    total = jnp.sum(cnt, axis=0)
    padded = (total + chunks_per_rows - 1) // chunks_per_rows * chunks_per_rows
    e_end = jnp.cumsum(padded)
    e_start = e_end - padded
    run_global = e_start[None, :] + jnp.cumsum(cnt, axis=0) - cnt
    run_local = jnp.cumsum(cnt, axis=1) - cnt
    experts = jnp.arange(MOE_EXPERTS, dtype=jnp.int32)
    shift = jnp.sum(jnp.where(ce[:, :, None] == experts, (run_global - run_local)[:, None, :], 0), axis=-1)
    local = jnp.arange(chunks_per_tile, dtype=jnp.int32)[None, :]
    dst = jnp.where(local < used[:, None], shift + local, 0)
    n_row_tiles = n_rows // MOE_TILE
    n_used = e_end[-1] // chunks_per_rows
    j = jnp.minimum(jnp.arange(n_row_tiles, dtype=jnp.int32), jnp.maximum(n_used - 1, 0))
    tile_expert = jnp.sum((e_end[None, :] // chunks_per_rows <= j[:, None]).astype(jnp.int32), axis=1)
    tile_expert = jnp.minimum(tile_expert, MOE_EXPERTS - 1)
    i32 = lambda v: v.astype(jnp.int32)
    return (i32(dst.reshape(-1)), i32(used), i32(e_start + total), i32(padded - total),
            i32(n_used.reshape(1)), i32(tile_expert))


def _chunk_rows(chunk):
    return pl.ds(pl.multiple_of(chunk * MOE_CHUNK, MOE_CHUNK), MOE_CHUNK)


def _run_copies(*groups):
    for wait in (False, True):
        for lo, hi, make_copy in groups:
            def body(k, carry, wait=wait, make_copy=make_copy):
                cp = make_copy(k)
                cp.wait() if wait else cp.start()
                return carry
            lax.fori_loop(lo, hi, body, 0)


def _scatter_chunks_kernel(dst_ref, used_ref, pad_start_ref, pad_n_ref, nused_ref, h2_ref, info_ref,
                           o_ref, wslot_ref, src_ref, zero_ref, sem):
    i = pl.program_id(0)
    chunks = MOE_SLOTS // MOE_CHUNK
    slot = lax.rem(i, 2)

    info = info_ref[...]
    srow = lax.broadcasted_iota(jnp.int32, (MOE_SLOTS, info.shape[1]), 0)
    hit1 = srow == info[0:1, :].astype(jnp.int32)
    hit2 = srow == info[1:2, :].astype(jnp.int32)
    perm = jnp.where(hit1, 1.0, jnp.where(hit2, 1.0, 0.0)).astype(BF16)
    src_ref[slot] = jnp.dot(perm, h2_ref[...], preferred_element_type=F32).astype(BF16)
    wslot = jnp.sum(jnp.where(hit1, info[2:3, :], jnp.where(hit2, info[3:4, :], 0.0)), axis=1, keepdims=True)
    wslot_ref[...] = jnp.broadcast_to(wslot, wslot_ref.shape)

    @pl.when(i == 0)
    def _():
        zero_ref[...] = jnp.zeros_like(zero_ref)
        pads = [(0, pad_n_ref[e], lambda c, e=e: pltpu.make_async_copy(
            zero_ref.at[pl.ds(0, MOE_CHUNK), :], o_ref.at[_chunk_rows(pad_start_ref[e] + c), :], sem.at[2]))
            for e in range(MOE_EXPERTS)]
        tail = (nused_ref[0], o_ref.shape[0] // MOE_TILE, lambda j: pltpu.make_async_copy(
            zero_ref, o_ref.at[pl.ds(pl.multiple_of(j * MOE_TILE, MOE_TILE), MOE_TILE), :], sem.at[2]))
        _run_copies(tail, *pads)

    def chunk_copies(tile, buf, wait):
        def body(c, carry):
            cp = pltpu.make_async_copy(src_ref.at[buf, _chunk_rows(c), :],
                                       o_ref.at[_chunk_rows(dst_ref[tile * chunks + c]), :], sem.at[buf])
            cp.wait() if wait else cp.start()
            return carry
        lax.fori_loop(0, used_ref[tile], body, 0)

    chunk_copies(i, slot, False)

    @pl.when(i > 0)
    def _():
        chunk_copies(i - 1, 1 - slot, True)

    @pl.when(i == pl.num_programs(0) - 1)
    def _():
        chunk_copies(i, slot, True)


def _scatter_chunks(h2, info, dst, used, pad_start, pad_n, n_used, n_rows):
    t, d = h2.shape
    tm = info.shape[1]
    row = lambda i, *_: (i, 0)
    return pl.pallas_call(
        _scatter_chunks_kernel,
        grid_spec=pltpu.PrefetchScalarGridSpec(
            num_scalar_prefetch=5,
            grid=(used.shape[0],),
            in_specs=[pl.BlockSpec((tm, d), row), pl.BlockSpec((8, tm), row)],
            out_specs=[pl.BlockSpec(memory_space=pl.ANY), pl.BlockSpec((MOE_SLOTS, LANES), row)],
            scratch_shapes=[pltpu.VMEM((2, MOE_SLOTS, d), BF16), pltpu.VMEM((MOE_TILE, d), BF16),
                            pltpu.SemaphoreType.DMA((3,))]),
        out_shape=[jax.ShapeDtypeStruct((n_rows, d), BF16),
                   jax.ShapeDtypeStruct((used.shape[0] * MOE_SLOTS, LANES), F32)],
        compiler_params=_params("arbitrary"),
        name="moe_scatter_chunks",
    )(dst, used, pad_start, pad_n, n_used, h2, info)


def _experts_kernel(te_ref, nt_ref, xs_ref, wg_ref, wu_ref, wd_ref, o_ref):
    used = pl.program_id(0) < nt_ref[0]

    @pl.when(used)
    def _():
        x = xs_ref[...]
        gate = jnp.dot(x, wg_ref[0].astype(BF16), preferred_element_type=F32)
        up = jnp.dot(x, wu_ref[0].astype(BF16), preferred_element_type=F32)
        hid = (_silu(gate) * up).astype(BF16)
        o_ref[...] = jnp.dot(hid, wd_ref[0].astype(BF16), preferred_element_type=F32).astype(BF16)


def _experts(xs, tile_expert, n_used, w_gate, w_up, w_down):
    n_rows, d = xs.shape
    tile = MOE_TILE
    by_expert = lambda j, te, nt: (te[j], 0, 0)
    row_tile = lambda j, te, nt: (jnp.maximum(jnp.minimum(j, nt[0] - 1), 0), 0)
    return pl.pallas_call(
        _experts_kernel,
        grid_spec=pltpu.PrefetchScalarGridSpec(
            num_scalar_prefetch=2,
            grid=(n_rows // tile,),
            in_specs=[pl.BlockSpec((tile, d), row_tile),
                      pl.BlockSpec((1, d, MOE_FF), by_expert),
                      pl.BlockSpec((1, d, MOE_FF), by_expert),
                      pl.BlockSpec((1, MOE_FF, d), by_expert)],
            out_specs=pl.BlockSpec((tile, d), row_tile)),
        out_shape=jax.ShapeDtypeStruct((n_rows, d), BF16),
        input_output_aliases={2: 0},
        compiler_params=_params("arbitrary"),
        name="moe_experts",
    )(tile_expert, n_used, xs, w_gate, w_up, w_down)


def _combine_kernel(dst_ref, used_ref, wslot_ref, col_ref, x1_ref, mod_ref, gpost_ref, ys_ref, o_ref,
                    ysl_ref, sem):
    i = pl.program_id(0)
    chunks = MOE_SLOTS // MOE_CHUNK
    slot = lax.rem(i, 2)

    def fetch(tile, buf, wait):
        base = tile * chunks

        def copy(c, carry):
            cp = pltpu.make_async_copy(ys_ref.at[_chunk_rows(dst_ref[base + c]), :],
                                       ysl_ref.at[buf, _chunk_rows(c), :], sem.at[buf])
            cp.wait() if wait else cp.start()
            return carry

        lax.fori_loop(0, chunks, copy, 0, unroll=8)

    @pl.when(i == 0)
    def _():
        fetch(0, 0, False)

    @pl.when(i + 1 < pl.num_programs(0))
    def _():
        fetch(i + 1, 1 - slot, False)

    col = col_ref[...]
    tm = col.shape[0]
    lane = lax.broadcasted_iota(jnp.int32, (tm, MOE_SLOTS), 1)
    hit = jnp.where(lane == col[:, 0:1].astype(jnp.int32), 1.0,
                    jnp.where(lane == col[:, 1:2].astype(jnp.int32), 1.0, 0.0)).astype(BF16)
    fetch(i, slot, True)
    in_use = lax.broadcasted_iota(jnp.int32, (MOE_SLOTS, 1), 0) < used_ref[i] * MOE_CHUNK
    ysw = jnp.where(in_use, ysl_ref[slot].astype(F32) * wslot_ref[:, 0:1], 0.0).astype(BF16)
    y = jnp.dot(hit, ysw, preferred_element_type=F32)
    g2 = mod_ref[0, 5:6, :]
    o_ref[...] = x1_ref[...] + g2 * _rms(y, gpost_ref[...])


def _combine(ys, dst, used, wslot, col, x1, mod3, gpost, seq, tm):
    t, d = x1.shape
    per_b = seq // tm
    row = lambda i, *_: (i, 0)
    return pl.pallas_call(
        _combine_kernel,
        grid_spec=pltpu.PrefetchScalarGridSpec(
            num_scalar_prefetch=2,
            grid=(t // tm,),
            in_specs=[pl.BlockSpec((MOE_SLOTS, LANES), row),
                      pl.BlockSpec((tm, LANES), row),
                      pl.BlockSpec((tm, d), row),
                      pl.BlockSpec((1, N_MOD, d), lambda i, *_: (i // per_b, 0, 0)),
                      pl.BlockSpec((1, d), lambda i, *_: (0, 0)),
                      pl.BlockSpec(memory_space=pl.ANY)],
            out_specs=pl.BlockSpec((tm, d), row),
            scratch_shapes=[pltpu.VMEM((2, MOE_SLOTS, d), BF16), pltpu.SemaphoreType.DMA((2,))]),
        out_shape=jax.ShapeDtypeStruct((t, d), F32),
        compiler_params=_params("arbitrary"),
        name="moe_combine",
    )(dst, used, wslot, col, x1, mod3, gpost, ys)


def kernel(x, c, positions, ada_w, ada_b, pre_norm_mix, post_norm_mix, pre_norm_ffn, post_norm_ffn, w_in, conv_w, conv_b, dt_bias, a_log, d_skip, ssd_norm, q_norm, w_uq, kv_norm, w_ukv, mla_out_norm, w_out, w_group_router, b_group_router, w_expert_router, b_expert_router, w_gate, w_up, w_down):
    batch, seq, d = x.shape
    t = batch * seq
    depth = ada_w.shape[0]
    tm = min(INPROJ_TOKENS, seq)
    cos_t, sin_t = _rope_tables(positions)

    x2 = x.reshape(t, d)
    for l in range(depth):
        mod3 = _modulation(c, ada_w[l], ada_b[l]).reshape(batch, N_MOD, d)
        w_main, w_rest, wq2, wk, wv_t = _inproj_weights(w_in[l], w_uq[l], w_ukv[l])
        z, xbc, dt_raw, q, k, v_t = _inproj(
            x2, mod3, pre_norm_mix[l].reshape(1, d), w_main, w_rest, q_norm[l].reshape(1, -1), wq2,
            kv_norm[l].reshape(1, -1), wk, wv_t, cos_t, sin_t, seq, tm)
        y_ssd = _ssd(xbc, z, dt_raw, conv_w[l], conv_b[l], dt_bias[l], a_log[l], d_skip[l],
                     ssd_norm[l], batch, seq)
        y_mla = _attention(q, k, v_t, mla_out_norm[l].reshape(1, -1), batch, seq)
        w_o = w_out[l].astype(BF16)
        pad_r = LANES - MOE_GROUPS - MOE_EXPERTS
        w_router = jnp.concatenate([w_group_router[l], w_expert_router[l], jnp.zeros((d, pad_r), F32)], axis=1)
        b_router = jnp.concatenate([b_group_router[l].reshape(-1), b_expert_router[l].reshape(-1),
                                    jnp.zeros((pad_r,), F32)]).reshape(1, LANES)
        wr_hi = w_router.astype(BF16)
        wr_lo = (w_router - wr_hi.astype(F32)).astype(BF16)
        x1, h2, col, info, cnt_tiles, chunk_expert = _outproj(
            y_ssd, y_mla, x2, mod3, w_o[:SSD_INNER], w_o[SSD_INNER:], post_norm_mix[l].reshape(1, d),
            pre_norm_ffn[l].reshape(1, d), wr_hi, wr_lo, b_router, seq, MOE_TOKENS)
        n_tok_tiles = t // MOE_TOKENS
        n_rows = _moe_rows(n_tok_tiles)
        dst, used, pad_start, pad_n, n_used, tile_expert = _moe_plan(cnt_tiles, chunk_expert, n_tok_tiles, n_rows)
        xs, wslot = _scatter_chunks(h2, info, dst, used, pad_start, pad_n, n_used, n_rows)
        ys = _experts(xs, tile_expert, n_used, w_gate[l], w_up[l], w_down[l])
        x2 = _combine(ys, dst, used, wslot, col, x1, mod3, post_norm_ffn[l].reshape(1, d), seq, MOE_TOKENS)
    return x2.reshape(batch, seq, d)
```

```python
import functools
import math

import jax
import jax.numpy as jnp
from jax import lax
from jax.experimental import pallas as pl
from jax.experimental.pallas import tpu as pltpu

F32 = jnp.float32
BF16 = jnp.bfloat16

D_MODEL = 1024
SSD_HEADS = 8
SSD_HEAD_DIM = 64
SSD_INNER = SSD_HEADS * SSD_HEAD_DIM
SSD_GROUPS = 2
SSD_STATE = 128
SSD_CONV = 4
SSD_CHUNK = 128
SSD_XBC = SSD_INNER + 2 * SSD_GROUPS * SSD_STATE
MLA_HEADS = 8
MLA_NOPE = 64
MLA_ROPE = 32
MLA_QK = MLA_NOPE + MLA_ROPE
MLA_V = 64
MLA_Q_RANK = 256
MLA_KV_RANK = 128
MLA_INNER = MLA_HEADS * MLA_V
ROPE_THETA = 10000.0
MOE_GROUPS = 4
MOE_EXPERTS_PER_GROUP = 8
MOE_EXPERTS = MOE_GROUPS * MOE_EXPERTS_PER_GROUP
MOE_FF = 256
N_MOD = 6
EPS = 1e-6
LOG2_E = math.log2(math.e)

LANES = 128
HEAD_PAD = 128
ATT_BLOCK = 256
MLA_V_ROWS = MLA_V + 16
INPROJ_TOKENS = 512
SSD_STEP_ROWS = 8 * SSD_CHUNK
MOE_TOKENS = 512
MOE_CHUNK = 16
MOE_TILE = 512
MOE_SLOTS = 2 * MOE_TOKENS + MOE_EXPERTS * MOE_CHUNK
VMEM_LIMIT = 48 * 1024 * 1024

_C_Z = 0
_C_XBC = _C_Z + SSD_INNER
_C_CQ = _C_XBC + SSD_XBC
_C_CKV = _C_CQ + MLA_Q_RANK
_C_KR = _C_CKV + MLA_KV_RANK
_C_KRS = _C_KR + LANES
_C_DT = _C_KRS + LANES
_C_END = _C_DT + LANES


def _silu(v):
    return v * (1.0 / (1.0 + jnp.exp(-v)))


def _rms(v, gain):
    return v * lax.rsqrt(jnp.mean(v * v, axis=-1, keepdims=True) + EPS) * gain


def _params(*sem, flags=None):
    return pltpu.CompilerParams(dimension_semantics=sem, vmem_limit_bytes=VMEM_LIMIT, flags=flags)


def _mod_kernel(c_ref, w_ref, b_ref, o_ref):
    act = _silu(c_ref[...])
    a_hi = act.astype(BF16)
    a_lo = (act - a_hi.astype(F32)).astype(BF16)
    w = w_ref[...]
    w_hi = w.astype(BF16)
    w_lo = (w - w_hi.astype(F32)).astype(BF16)
    o_ref[...] = (jnp.dot(a_hi, w_hi, preferred_element_type=F32)
                  + jnp.dot(a_lo, w_hi, preferred_element_type=F32)
                  + jnp.dot(a_hi, w_lo, preferred_element_type=F32)) + b_ref[...]


def _modulation(c, ada_w, ada_b):
    b, d = c.shape
    n = ada_w.shape[1]
    return pl.pallas_call(
        _mod_kernel,
        grid=(n // d,),
        in_specs=[pl.BlockSpec((b, d), lambda j: (0, 0)),
                  pl.BlockSpec((d, d), lambda j: (0, j)),
                  pl.BlockSpec((1, d), lambda j: (0, j))],
        out_specs=pl.BlockSpec((b, d), lambda j: (0, j)),
        out_shape=jax.ShapeDtypeStruct((b, n), F32),
        compiler_params=_params("arbitrary"),
        name="modulation",
    )(c, ada_w, ada_b.reshape(1, n))


def _rope_kernel(pos_ref, inv_ref, cos_ref, sin_ref):
    ang = pos_ref[...].astype(F32) * inv_ref[...]
    blk = ang.shape[0]
    half = MLA_ROPE // 2
    per_row = LANES // half
    lane = lax.broadcasted_iota(jnp.int32, (blk * per_row, LANES), 1)
    first = (lane >= MLA_NOPE) & (lane < MLA_NOPE + half)
    second = (lane >= MLA_NOPE + half) & (lane < MLA_QK)
    for fn, out_ref, fill in ((jnp.cos, cos_ref, 1.0), (jnp.sin, sin_ref, 0.0)):
        compact = fn(ang)
        rep = jnp.broadcast_to(compact[:, None, :], (blk, per_row, LANES)).reshape(blk * per_row, LANES)
        a = pltpu.roll(rep, 0, 1, stride=half, stride_axis=0)
        b = pltpu.roll(a, half, 1)
        out_ref[...] = jnp.where(first, a, jnp.where(second, b, fill))


def _rope_tables(positions):
    t = positions.size
    half = MLA_ROPE // 2
    per_row = LANES // half
    inv = 1.0 / (ROPE_THETA ** (jnp.arange(0, MLA_ROPE, 2, dtype=F32) / MLA_ROPE))
    group_token = (MLA_NOPE // half - jnp.arange(per_row)) % per_row
    pos_rep = jnp.repeat(positions.reshape(t // per_row, per_row)[:, group_token], half, axis=1)
    inv_t = jnp.tile(inv, per_row).reshape(1, LANES)
    rows = t // per_row
    blk = min(rows, 512)
    return pl.pallas_call(
        _rope_kernel,
        grid=(rows // blk,),
        in_specs=[pl.BlockSpec((blk, LANES), lambda i: (i, 0)),
                  pl.BlockSpec((1, LANES), lambda i: (0, 0))],
        out_specs=[pl.BlockSpec((blk * per_row, LANES), lambda i: (i, 0))] * 2,
        out_shape=[jax.ShapeDtypeStruct((t, LANES), F32)] * 2,
        compiler_params=_params("arbitrary"),
        name="rope_tables",
    )(pos_rep, inv_t)


def _inproj_kernel(x_ref, mod_ref, gpre_ref, wmain_ref, wrest_ref, qn_ref, wq_ref, kvn_ref, wk_ref,
                   wvt_ref, cos_ref, sin_ref, z_ref, xbc_ref, dt_ref, q_ref, k_ref, vt_ref):
    x = x_ref[...]
    sh = mod_ref[0, 0:1, :]
    sc = mod_ref[0, 1:2, :]
    h = (_rms(x, gpre_ref[...]) * (1.0 + sc) + sh).astype(BF16)
    u = jnp.concatenate([jnp.dot(h, wmain_ref[...], preferred_element_type=F32),
                         jnp.dot(h, wrest_ref[...], preferred_element_type=F32)], axis=1)
    z_ref[...] = u[:, _C_Z:_C_XBC].astype(BF16)
    xbc_ref[...] = u[:, _C_XBC:_C_CQ].astype(BF16)
    dt_ref[...] = u[:, _C_DT:_C_END]
    cos_t = cos_ref[...]
    sin_t = sin_ref[...]
    cq = _rms(u[:, _C_CQ:_C_CKV], qn_ref[...]).astype(BF16)
    q2 = jnp.dot(cq, wq_ref[...], preferred_element_type=F32)
    scale = MLA_QK ** -0.5 * LOG2_E
    nq = MLA_HEADS * HEAD_PAD
    for hd in range(MLA_HEADS):
        a = q2[:, hd * HEAD_PAD:(hd + 1) * HEAD_PAD]
        b = q2[:, nq + hd * HEAD_PAD:nq + (hd + 1) * HEAD_PAD]
        q_ref[:, hd * HEAD_PAD:(hd + 1) * HEAD_PAD] = ((a * cos_t + b * sin_t) * scale).astype(BF16)
    ckv = _rms(u[:, _C_CKV:_C_KR], kvn_ref[...]).astype(BF16)
    kn = jnp.dot(ckv, wk_ref[...], preferred_element_type=F32)
    k_pe = u[:, _C_KR:_C_KRS] * cos_t + u[:, _C_KRS:_C_DT] * sin_t
    for hd in range(MLA_HEADS):
        k_ref[:, hd * HEAD_PAD:(hd + 1) * HEAD_PAD] = (
            kn[:, hd * HEAD_PAD:(hd + 1) * HEAD_PAD] + k_pe).astype(BF16)
    v_t = lax.dot_general(wvt_ref[...], ckv, (((1,), (1,)), ((), ())), preferred_element_type=F32)
    head_row = lax.rem(lax.broadcasted_iota(jnp.int32, (v_t.shape[0], 1), 0), MLA_V_ROWS)
    v_t = v_t + jnp.where(head_row == MLA_V, 1.0, 0.0)
    for s in range(vt_ref.shape[0]):
        vt_ref[s] = v_t[:, s * ATT_BLOCK:(s + 1) * ATT_BLOCK].astype(BF16)


def _inproj_weights(w_in, w_uq, w_ukv):
    d = w_in.shape[0]
    half = MLA_ROPE // 2
    o_z, o_xbc, o_dt = 0, SSD_INNER, SSD_INNER + SSD_XBC
    o_cq = o_dt + SSD_HEADS
    o_ckv = o_cq + MLA_Q_RANK
    o_kr = o_ckv + MLA_KV_RANK
    zeros = lambda n: jnp.zeros((d, n), F32)
    kr = w_in[:, o_kr:o_kr + MLA_ROPE]
    kr_blk = jnp.concatenate([zeros(MLA_NOPE), kr, zeros(HEAD_PAD - MLA_QK)], axis=1)
    krs_blk = jnp.concatenate([zeros(MLA_NOPE), -kr[:, half:], kr[:, :half], zeros(HEAD_PAD - MLA_QK)], axis=1)
    dt_blk = jnp.concatenate([w_in[:, o_dt:o_dt + SSD_HEADS], zeros(LANES - SSD_HEADS)], axis=1)
    w_main = w_in[:, o_z:o_dt]
    w_rest = jnp.concatenate([w_in[:, o_cq:o_kr], kr_blk, krs_blk, dt_blk], axis=1)
    r = w_uq.shape[0]
    zq = jnp.zeros((r, MLA_HEADS, HEAD_PAD - MLA_QK), F32)
    zn = jnp.zeros((r, MLA_HEADS, MLA_NOPE), F32)
    wq_plain = jnp.concatenate([w_uq, zq], axis=2).reshape(r, MLA_HEADS * HEAD_PAD)
    wq_rot = jnp.concatenate([zn, -w_uq[:, :, MLA_NOPE + half:], w_uq[:, :, MLA_NOPE:MLA_NOPE + half], zq],
                             axis=2).reshape(r, MLA_HEADS * HEAD_PAD)
    wq2 = jnp.concatenate([wq_plain, wq_rot], axis=1)
    rk = w_ukv.shape[0]
    zk = jnp.zeros((rk, MLA_HEADS, HEAD_PAD - MLA_NOPE), F32)
    wk = jnp.concatenate([w_ukv[:, :, :MLA_NOPE], zk], axis=2).reshape(rk, MLA_HEADS * HEAD_PAD)
    zv = jnp.zeros((rk, MLA_HEADS, MLA_V_ROWS - MLA_V), F32)
    wv_t = jnp.concatenate([w_ukv[:, :, MLA_NOPE:], zv], axis=2).reshape(rk, MLA_HEADS * MLA_V_ROWS).T
    return w_main.astype(BF16), w_rest.astype(BF16), wq2.astype(BF16), wk.astype(BF16), wv_t.astype(BF16)


def _inproj(x2, mod3, gpre, w_main, w_rest, q_norm, wq2, kv_norm, wk, wv_t, cos_t, sin_t, seq, tm):
    t, d = x2.shape
    per_b = seq // tm
    row = lambda i: (i, 0)
    const = lambda i: (0, 0)
    nqk = MLA_HEADS * HEAD_PAD
    slabs = tm // ATT_BLOCK
    return pl.pallas_call(
        _inproj_kernel,
        grid=(t // tm,),
        in_specs=[pl.BlockSpec((tm, d), row),
                  pl.BlockSpec((1, N_MOD, d), lambda i: (i // per_b, 0, 0)),
                  pl.BlockSpec((1, d), const),
                  pl.BlockSpec(w_main.shape, const),
                  pl.BlockSpec(w_rest.shape, const),
                  pl.BlockSpec((1, MLA_Q_RANK), const),
                  pl.BlockSpec(wq2.shape, const),
                  pl.BlockSpec((1, MLA_KV_RANK), const),
                  pl.BlockSpec(wk.shape, const),
                  pl.BlockSpec(wv_t.shape, const),
                  pl.BlockSpec((tm, LANES), row),
                  pl.BlockSpec((tm, LANES), row)],
        out_specs=[pl.BlockSpec((tm, SSD_INNER), row),
                   pl.BlockSpec((tm, SSD_XBC), row),
                   pl.BlockSpec((tm, LANES), row),
                   pl.BlockSpec((tm, nqk), row),
                   pl.BlockSpec((tm, nqk), row),
                   pl.BlockSpec((slabs, MLA_HEADS * MLA_V_ROWS, ATT_BLOCK), lambda i: (i, 0, 0))],
        out_shape=[jax.ShapeDtypeStruct((t, SSD_INNER), BF16),
                   jax.ShapeDtypeStruct((t, SSD_XBC), BF16),
                   jax.ShapeDtypeStruct((t, LANES), F32),
                   jax.ShapeDtypeStruct((t, nqk), BF16),
                   jax.ShapeDtypeStruct((t, nqk), BF16),
                   jax.ShapeDtypeStruct((t // ATT_BLOCK, MLA_HEADS * MLA_V_ROWS, ATT_BLOCK), BF16)],
        compiler_params=_params("arbitrary"),
        name="inproj",
    )(x2, mod3, gpre, w_main, w_rest, q_norm, wq2, kv_norm, wk, wv_t, cos_t, sin_t)


def _split3_packed(v):
    lane = lax.broadcasted_iota(jnp.int32, v.shape, 1)
    v = jnp.where(lane < SSD_HEADS, v, 0.0)
    hi = v.astype(BF16).astype(F32)
    rest = v - hi
    mid = rest.astype(BF16).astype(F32)
    lo = rest - mid
    return (hi + pltpu.roll(mid, SSD_HEADS, 1) + pltpu.roll(lo, 2 * SSD_HEADS, 1)).astype(BF16)


def _ssd_kernel(xbc_ref, z_ref, dt_ref, cw_ref, cb_ref, dtb_ref, alog_ref, dsk_ref, gn_ref,
                e_ref, y_ref, tail_ref, state_ref):
    @pl.when(pl.program_id(1) == 0)
    def _():
        tail_ref[...] = jnp.zeros_like(tail_ref)
        state_ref[...] = jnp.zeros_like(state_ref)

    for c in range(xbc_ref.shape[0] // SSD_CHUNK):
        _ssd_chunk(pl.ds(c * SSD_CHUNK, SSD_CHUNK), xbc_ref, z_ref, dt_ref, cw_ref, cb_ref, dtb_ref,
                   alog_ref, dsk_ref, gn_ref, e_ref, y_ref, tail_ref, state_ref)


def _ssd_chunk(rows, xbc_ref, z_ref, dt_ref, cw_ref, cb_ref, dtb_ref, alog_ref, dsk_ref, gn_ref,
               e_ref, y_ref, tail_ref, state_ref):
    L = SSD_CHUNK
    gw = SSD_INNER // SSD_GROUPS

    cur_b = xbc_ref[rows, :]
    ext = jnp.concatenate([tail_ref[...], cur_b], axis=0)
    out_row = lax.broadcasted_iota(jnp.int32, ((SSD_CONV - 1) * L, ext.shape[0]), 0)
    src_row = lax.broadcasted_iota(jnp.int32, ((SSD_CONV - 1) * L, ext.shape[0]), 1)
    tail_rows = ext.shape[0] - L
    shift = jnp.where(src_row + (out_row // L + 1) == lax.rem(out_row, L) + tail_rows, 1.0, 0.0)
    shifted = jnp.dot(shift.astype(BF16), ext, preferred_element_type=F32)
    acc = cur_b.astype(F32) * cw_ref[SSD_CONV - 1:SSD_CONV, :] + cb_ref[...]
    for j in range(1, SSD_CONV):
        acc = acc + shifted[(j - 1) * L:j * L] * cw_ref[SSD_CONV - 1 - j:SSD_CONV - j, :]
    tail_ref[...] = cur_b[L - tail_rows:L]
    act = _silu(acc)
    xs = act[:, :SSD_INNER]
    bm = act[:, SSD_INNER:SSD_INNER + SSD_GROUPS * SSD_STATE].astype(BF16)
    cm = act[:, SSD_INNER + SSD_GROUPS * SSD_STATE:].astype(BF16)

    dt_in = dt_ref[rows, :] + dtb_ref[...]
    dt = jnp.maximum(dt_in, 0.0) + jnp.log(1.0 + jnp.exp(-jnp.abs(dt_in)))
    adt = dt * (-jnp.exp(alog_ref[...]))
    ri = lax.broadcasted_iota(jnp.int32, (L, L), 0)
    ci = lax.broadcasted_iota(jnp.int32, (L, L), 1)
    causal = ci <= ri
    acs_p = jnp.dot(causal.astype(BF16), _split3_packed(adt), preferred_element_type=F32)
    a_cs = (acs_p + pltpu.roll(acs_p, LANES - SSD_HEADS, 1)
            + pltpu.roll(acs_p, LANES - 2 * SSD_HEADS, 1))
    a_cs_t = a_cs.T
    cs_all = jnp.dot(_split3_packed(a_cs), e_ref[...], preferred_element_type=F32)
    cs64 = cs_all[:, :SSD_INNER]
    cs128 = cs_all[:, SSD_INNER:]
    dt64 = jnp.dot(_split3_packed(dt), e_ref[:, :SSD_INNER], preferred_element_type=F32)

    xd = xs * dt64
    xd_b = xd.astype(BF16)
    last = cs64[L - 1:L, :]
    xdw = (xd * jnp.exp(last - cs64)).astype(BF16)
    chunk_decay = jnp.exp(last)
    in_decay = jnp.exp(cs64)

    lane = lax.broadcasted_iota(jnp.int32, (L, LANES), 1)
    lo = lane < SSD_HEAD_DIM
    zero_b = jnp.zeros((L, LANES), BF16)
    y_parts = []
    new_states = []
    for g in range(SSD_GROUPS):
        bg = bm[:, g * SSD_STATE:(g + 1) * SSD_STATE]
        cg = cm[:, g * SSD_STATE:(g + 1) * SSD_STATE]
        cb = lax.dot_general(cg, bg, (((1,), (1,)), ((), ())), preferred_element_type=F32)
        prev = state_ref[:, g * gw:(g + 1) * gw]
        y_off = jnp.dot(cg, prev.astype(BF16), preferred_element_type=F32) * in_decay[:, g * gw:(g + 1) * gw]
        st = lax.dot_general(bg, xdw[:, g * gw:(g + 1) * gw], (((0,), (0,)), ((), ())),
                             preferred_element_type=F32)
        new_states.append(prev * chunk_decay[:, g * gw:(g + 1) * gw] + st)
        heads_per_group = SSD_HEADS // SSD_GROUPS
        for pair in range(heads_per_group // 2):
            h0 = g * heads_per_group + 2 * pair
            blk = xd_b[:, h0 * SSD_HEAD_DIM:(h0 + 2) * SSD_HEAD_DIM]
            y_pair = None
            for k in range(2):
                hh = h0 + k
                diff = cs128[:, hh * LANES:(hh + 1) * LANES] - a_cs_t[hh:hh + 1, :]
                decay = jnp.where(causal, jnp.exp(diff), 0.0)
                m = (cb * decay).astype(BF16)
                rhs = jnp.where(lo, blk, zero_b) if k == 0 else jnp.where(lo, zero_b, blk)
                part = jnp.dot(m, rhs, preferred_element_type=F32)
                y_pair = part if y_pair is None else y_pair + part
            c0 = (2 * pair) * SSD_HEAD_DIM
            y_parts.append(y_pair + y_off[:, c0:c0 + LANES])
    state_ref[...] = jnp.concatenate(new_states, axis=1)
    y = jnp.concatenate(y_parts, axis=1) + dsk_ref[...] * xs
    gated = y * _silu(z_ref[rows, :].astype(F32))
    y_ref[rows, :] = _rms(gated, gn_ref[...]).astype(BF16)


def _ssd(xbc, z, dt_raw, conv_w, conv_b, dt_bias, a_log, d_skip, ssd_norm, batch, seq):
    t = xbc.shape[0]
    L = min(SSD_STEP_ROWS, seq)
    nc = seq // L
    row = lambda b, c: (b * nc + c, 0)
    const = lambda b, c: (0, 0)
    pad = lambda v: jnp.concatenate([v.reshape(1, -1), jnp.zeros((1, LANES - v.size), F32)], axis=1)
    head_of_lane = jnp.concatenate([jnp.arange(SSD_INNER) // SSD_HEAD_DIM,
                                    jnp.arange(SSD_HEADS * LANES) // LANES])
    rows = jnp.arange(LANES)[:, None]
    expand = ((rows < 3 * SSD_HEADS) & (rows % SSD_HEADS == head_of_lane[None, :])).astype(BF16)
    dsk = jnp.repeat(d_skip.astype(F32), SSD_HEAD_DIM).reshape(1, SSD_INNER)
    return pl.pallas_call(
        _ssd_kernel,
        grid=(batch, nc),
        in_specs=[pl.BlockSpec((L, SSD_XBC), row),
                  pl.BlockSpec((L, SSD_INNER), row),
                  pl.BlockSpec((L, LANES), row),
                  pl.BlockSpec((SSD_CONV, SSD_XBC), const),
                  pl.BlockSpec((1, SSD_XBC), const),
                  pl.BlockSpec((1, LANES), const),
                  pl.BlockSpec((1, LANES), const),
                  pl.BlockSpec((1, SSD_INNER), const),
                  pl.BlockSpec((1, SSD_INNER), const),
                  pl.BlockSpec(expand.shape, const)],
        out_specs=pl.BlockSpec((L, SSD_INNER), row),
        out_shape=jax.ShapeDtypeStruct((t, SSD_INNER), BF16),
        scratch_shapes=[pltpu.VMEM((16, SSD_XBC), BF16),
                        pltpu.VMEM((SSD_STATE, SSD_INNER), F32)],
        compiler_params=_params("arbitrary", "arbitrary"),
        name="ssd",
    )(xbc, z, dt_raw, conv_w, conv_b.reshape(1, -1), pad(dt_bias), pad(a_log), dsk,
      ssd_norm.reshape(1, -1), expand)


def _attn_kernel(q_ref, k_ref, vt_ref, gn_ref, o_ref, s_ref):
    blk = ATT_BLOCK
    qi = pl.program_id(1)
    causal = (lax.broadcasted_iota(jnp.int32, (blk, blk), 0)
              <= lax.broadcasted_iota(jnp.int32, (blk, blk), 1))
    dims = (((1,), (1,)), ((), ()))
    qs = [q_ref[:, hd * HEAD_PAD:(hd + 1) * HEAD_PAD] for hd in range(MLA_HEADS)]

    def step(j, carry, nblk, masked):
        keys = nblk * blk
        rows = pl.ds(pl.multiple_of(j * blk, blk), keys)
        m_new = []
        for hd in range(MLA_HEADS):
            kk = k_ref[rows, hd * HEAD_PAD:(hd + 1) * HEAD_PAD]
            s = lax.dot_general(kk, qs[hd], dims, preferred_element_type=F32)
            if masked:
                s = jnp.where(causal, s, -jnp.inf)
            s_ref[hd, 0:keys, :] = s
            m_new.append(jnp.maximum(carry[hd][0], jnp.max(s, axis=0, keepdims=True)))
        new = []
        for hd in range(MLA_HEADS):
            m_old, acc = carry[hd]
            p = jnp.exp2(s_ref[hd, 0:keys, :] - m_new[hd]).astype(BF16)
            acc = acc * jnp.exp2(m_old - m_new[hd])
            for b in range(nblk):
                vt = vt_ref[j + b, hd * MLA_V_ROWS:(hd + 1) * MLA_V_ROWS, :]
                acc = acc + jnp.dot(vt, p[b * blk:(b + 1) * blk], preferred_element_type=F32)
            new.append((m_new[hd], acc))
        return tuple(new)

    neg = jnp.full((1, blk), -jnp.inf, F32)
    carry = tuple((neg, jnp.zeros((MLA_V_ROWS, blk), F32)) for _ in range(MLA_HEADS))
    carry = lax.fori_loop(0, qi // 2, lambda u, c: step(2 * u, c, 2, False), carry)
    carry = lax.fori_loop(0, lax.rem(qi, 2), lambda _, c: step(qi - 1, c, 1, False), carry)
    final = step(qi, carry, 1, True)
    y_t = jnp.concatenate([acc[:MLA_V] / acc[MLA_V:MLA_V + 1] for _, acc in final], axis=0)
    o_ref[...] = _rms(y_t.T, gn_ref[...]).astype(BF16)


def _attention(q, k, v_t, gn, batch, seq):
    t = q.shape[0]
    blk = ATT_BLOCK
    nq = seq // blk
    nqk = MLA_HEADS * HEAD_PAD
    return pl.pallas_call(
        _attn_kernel,
        grid=(batch, nq),
        in_specs=[pl.BlockSpec((blk, nqk), lambda b, i: (b * nq + i, 0)),
                  pl.BlockSpec((seq, nqk), lambda b, i: (b, 0)),
                  pl.BlockSpec((nq, MLA_HEADS * MLA_V_ROWS, blk), lambda b, i: (b, 0, 0)),
                  pl.BlockSpec((1, MLA_INNER), lambda b, i: (0, 0))],
        out_specs=pl.BlockSpec((blk, MLA_INNER), lambda b, i: (b * nq + i, 0)),
        out_shape=jax.ShapeDtypeStruct((t, MLA_INNER), BF16),
        scratch_shapes=[pltpu.VMEM((MLA_HEADS, 2 * blk, blk), F32)],
        compiler_params=_params("arbitrary", "arbitrary"),
        name="attention",
    )(q, k, v_t, gn)


def _outproj_kernel(ys_ref, ym_ref, x_ref, mod_ref, wtop_ref, wbot_ref, gpm_ref, gpf_ref,
                    wrh_ref, wrl_ref, br_ref, x1_ref, h2_ref, col_ref, info_ref, cnt_ref, ce_ref):
    y = (jnp.dot(ys_ref[...], wtop_ref[...], preferred_element_type=F32)
         + jnp.dot(ym_ref[...], wbot_ref[...], preferred_element_type=F32))
    g1 = mod_ref[0, 2:3, :]
    sh2 = mod_ref[0, 3:4, :]
    sc2 = mod_ref[0, 4:5, :]
    x1 = x_ref[...] + g1 * _rms(y, gpm_ref[...])
    x1_ref[...] = x1
    h2 = _rms(x1, gpf_ref[...]) * (1.0 + sc2) + sh2
    tm = h2.shape[0]
    h_hi = h2.astype(BF16)
    h_lo = (h2 - h_hi.astype(F32)).astype(BF16)
    logits = (jnp.dot(h_hi, wrh_ref[...], preferred_element_type=F32)
              + jnp.dot(h_lo, wrh_ref[...], preferred_element_type=F32)
              + jnp.dot(h_hi, wrl_ref[...], preferred_element_type=F32)) + br_ref[...]
    lt = logits.T
    row = lax.broadcasted_iota(jnp.int32, lt.shape, 0).astype(F32)
    ninf = -jnp.inf
    big = 1e9
    is_g = row < MOE_GROUPS
    gl = jnp.where(is_g, lt, ninf)
    gmax = jnp.max(gl, axis=0, keepdims=True)
    gsum = jnp.sum(jnp.where(is_g, jnp.exp(lt - gmax), 0.0), axis=0, keepdims=True)
    g_w = 1.0 / gsum
    g_idx = jnp.min(jnp.where(gl == gmax, row, big), axis=0, keepdims=True)
    first = MOE_GROUPS + MOE_EXPERTS_PER_GROUP * g_idx
    el = jnp.where(row >= first, jnp.where(row < first + MOE_EXPERTS_PER_GROUP, lt, ninf), ninf)
    m1 = jnp.max(el, axis=0, keepdims=True)
    i1 = jnp.min(jnp.where(el == m1, row, big), axis=0, keepdims=True)
    el2 = jnp.where(row == i1, ninf, el)
    m2 = jnp.max(el2, axis=0, keepdims=True)
    i2 = jnp.min(jnp.where(el2 == m2, row, big), axis=0, keepdims=True)
    r = jnp.exp(m2 - m1)
    w1 = g_w / (1.0 + r)
    w2 = g_w * r / (1.0 + r)
    sel1 = row == i1
    sel2 = row == i2
    both = jnp.where(sel1, 1.0, jnp.where(sel2, 1.0, 0.0))
    cnt16 = jnp.floor((jnp.sum(both, axis=1, keepdims=True) + (MOE_CHUNK - 1)) * (1.0 / MOE_CHUNK))
    er = lax.broadcasted_iota(jnp.int32, (LANES, LANES), 0)
    ec = lax.broadcasted_iota(jnp.int32, (LANES, LANES), 1)
    cnt16_b = jnp.broadcast_to(cnt16, (LANES, LANES))
    off16 = jnp.dot((ec < er).astype(BF16), cnt16_b.astype(BF16), preferred_element_type=F32)[:, 0:1]
    earlier = (lax.broadcasted_iota(jnp.int32, (tm, tm), 0)
               < lax.broadcasted_iota(jnp.int32, (tm, tm), 1)).astype(BF16)
    base = off16 * MOE_CHUNK + jnp.dot(both.astype(BF16), earlier, preferred_element_type=F32)
    lpos1 = jnp.sum(jnp.where(sel1, base, 0.0), axis=0, keepdims=True)
    lpos2 = jnp.sum(jnp.where(sel2, base, 0.0), axis=0, keepdims=True)
    h2_ref[...] = h_hi
    col_ref[...] = jnp.where(row == 0, lpos1, jnp.where(row == 1, lpos2, 0.0)).T
    row8 = lax.broadcasted_iota(jnp.int32, info_ref.shape, 0)
    info_ref[...] = jnp.where(row8 == 0, lpos1, jnp.where(row8 == 1, lpos2,
                              jnp.where(row8 == 2, w1, jnp.where(row8 == 3, w2, 0.0))))
    cnt_ref[...] = cnt16_b
    chunk = lax.broadcasted_iota(jnp.int32, (LANES, LANES), 1).astype(F32)
    ce = jnp.sum(jnp.where(off16 + cnt16 <= chunk, 1.0, 0.0), axis=0, keepdims=True) - MOE_GROUPS
    ce_ref[...] = jnp.broadcast_to(ce, ce_ref.shape)


def _outproj(y_ssd, y_mla, x2, mod3, w_top, w_bot, gpm, gpf, wr_hi, wr_lo, b_router, seq, tm):
    t, d = x2.shape
    per_b = seq // tm
    n_tok_tiles = t // tm
    row = lambda i: (i, 0)
    const = lambda i: (0, 0)
    return pl.pallas_call(
        _outproj_kernel,
        grid=(n_tok_tiles,),
        in_specs=[pl.BlockSpec((tm, SSD_INNER), row),
                  pl.BlockSpec((tm, MLA_INNER), row),
                  pl.BlockSpec((tm, d), row),
                  pl.BlockSpec((1, N_MOD, d), lambda i: (i // per_b, 0, 0)),
                  pl.BlockSpec(w_top.shape, const),
                  pl.BlockSpec(w_bot.shape, const),
                  pl.BlockSpec((1, d), const),
                  pl.BlockSpec((1, d), const),
                  pl.BlockSpec(wr_hi.shape, const),
                  pl.BlockSpec(wr_lo.shape, const),
                  pl.BlockSpec((1, LANES), const)],
        out_specs=[pl.BlockSpec((tm, d), row),
                   pl.BlockSpec((tm, d), row),
                   pl.BlockSpec((tm, LANES), row),
                   pl.BlockSpec((8, tm), row),
                   pl.BlockSpec((LANES, LANES), row),
                   pl.BlockSpec((8, LANES), row)],
        out_shape=[jax.ShapeDtypeStruct((t, d), F32),
                   jax.ShapeDtypeStruct((t, d), BF16),
                   jax.ShapeDtypeStruct((t, LANES), F32),
                   jax.ShapeDtypeStruct((n_tok_tiles * 8, tm), F32),
                   jax.ShapeDtypeStruct((n_tok_tiles * LANES, LANES), F32),
                   jax.ShapeDtypeStruct((n_tok_tiles * 8, LANES), F32)],
        compiler_params=_params("arbitrary"),
        name="outproj_router",
    )(y_ssd, y_mla, x2, mod3, w_top, w_bot, gpm, gpf, wr_hi, wr_lo, b_router)


def _moe_rows(n_tok_tiles):
    rows = n_tok_tiles * MOE_SLOTS + MOE_EXPERTS * (MOE_TILE - MOE_CHUNK)
    return (rows + MOE_TILE - 1) // MOE_TILE * MOE_TILE


def _moe_plan(cnt_tiles, chunk_expert, n_tok_tiles, n_rows):
    chunks_per_tile = MOE_SLOTS // MOE_CHUNK
    chunks_per_rows = MOE_TILE // MOE_CHUNK
    cnt = cnt_tiles.reshape(n_tok_tiles, LANES, LANES)[:, MOE_GROUPS:MOE_GROUPS + MOE_EXPERTS, 0]
    cnt = cnt.astype(jnp.int32)
    ce = chunk_expert.reshape(n_tok_tiles, 8, LANES)[:, 0, :chunks_per_tile].astype(jnp.int32)
    used = jnp.sum(cnt, axis=1)
    total = jnp.sum(cnt, axis=0)
    padded = (total + chunks_per_rows - 1) // chunks_per_rows * chunks_per_rows
    e_end = jnp.cumsum(padded)
    e_start = e_end - padded
    run_global = e_start[None, :] + jnp.cumsum(cnt, axis=0) - cnt
    run_local = jnp.cumsum(cnt, axis=1) - cnt
    experts = jnp.arange(MOE_EXPERTS, dtype=jnp.int32)
    shift = jnp.sum(jnp.where(ce[:, :, None] == experts, (run_global - run_local)[:, None, :], 0), axis=-1)
    local = jnp.arange(chunks_per_tile, dtype=jnp.int32)[None, :]
    dst = jnp.where(local < used[:, None], shift + local, 0)
    n_row_tiles = n_rows // MOE_TILE
    n_used = e_end[-1] // chunks_per_rows
    j = jnp.minimum(jnp.arange(n_row_tiles, dtype=jnp.int32), jnp.maximum(n_used - 1, 0))
    tile_expert = jnp.sum((e_end[None, :] // chunks_per_rows <= j[:, None]).astype(jnp.int32), axis=1)
    tile_expert = jnp.minimum(tile_expert, MOE_EXPERTS - 1)
    i32 = lambda v: v.astype(jnp.int32)
    return (i32(dst.reshape(-1)), i32(used), i32(e_start + total), i32(padded - total),
            i32(n_used.reshape(1)), i32(tile_expert))


def _chunk_rows(chunk):
    return pl.ds(pl.multiple_of(chunk * MOE_CHUNK, MOE_CHUNK), MOE_CHUNK)


def _copy_phase(groups, wait):
    for lo, hi, make_copy in groups:
        def body(k, carry, make_copy=make_copy):
            cp = make_copy(k)
            cp.wait() if wait else cp.start()
            return carry
        lax.fori_loop(lo, hi, body, 0)


def _scatter_chunks_kernel(dst_ref, used_ref, pad_start_ref, pad_n_ref, nused_ref, h2_ref, info_ref,
                           o_ref, wslot_ref, src_ref, zero_ref, sem):
    i = pl.program_id(0)
    chunks = MOE_SLOTS // MOE_CHUNK
    slot = lax.rem(i, 2)

    info = info_ref[...]
    srow = lax.broadcasted_iota(jnp.int32, (MOE_SLOTS, info.shape[1]), 0)
    hit1 = srow == info[0:1, :].astype(jnp.int32)
    hit2 = srow == info[1:2, :].astype(jnp.int32)
    perm = jnp.where(hit1, 1.0, jnp.where(hit2, 1.0, 0.0)).astype(BF16)
    src_ref[slot] = jnp.dot(perm, h2_ref[...], preferred_element_type=F32).astype(BF16)
    wslot = jnp.sum(jnp.where(hit1, info[2:3, :], jnp.where(hit2, info[3:4, :], 0.0)), axis=1, keepdims=True)
    wslot_ref[...] = jnp.broadcast_to(wslot, wslot_ref.shape)

    last = pl.num_programs(0) - 1
    fills = [(0, pad_n_ref[e], lambda c, e=e: pltpu.make_async_copy(
        zero_ref.at[pl.ds(0, MOE_CHUNK), :], o_ref.at[_chunk_rows(pad_start_ref[e] + c), :], sem.at[2]))
        for e in range(MOE_EXPERTS)]
    fills.append((nused_ref[0], o_ref.shape[0] // MOE_TILE, lambda j: pltpu.make_async_copy(
        zero_ref, o_ref.at[pl.ds(pl.multiple_of(j * MOE_TILE, MOE_TILE), MOE_TILE), :], sem.at[2])))

    @pl.when(i == 0)
    def _():
        zero_ref[...] = jnp.zeros_like(zero_ref)
        _copy_phase(fills, wait=False)

    def chunk_copies(tile, buf, wait):
        def body(c, carry):
            cp = pltpu.make_async_copy(src_ref.at[buf, _chunk_rows(c), :],
                                       o_ref.at[_chunk_rows(dst_ref[tile * chunks + c]), :], sem.at[buf])
            cp.wait() if wait else cp.start()
            return carry
        lax.fori_loop(0, used_ref[tile], body, 0)

    chunk_copies(i, slot, False)

    @pl.when(i > 0)
    def _():
        chunk_copies(i - 1, 1 - slot, True)

    @pl.when(i == last)
    def _():
        chunk_copies(i, slot, True)
        _copy_phase(fills, wait=True)


def _scatter_chunks(h2, info, dst, used, pad_start, pad_n, n_used, n_rows):
    t, d = h2.shape
    tm = info.shape[1]
    row = lambda i, *_: (i, 0)
    return pl.pallas_call(
        _scatter_chunks_kernel,
        grid_spec=pltpu.PrefetchScalarGridSpec(
            num_scalar_prefetch=5,
            grid=(used.shape[0],),
            in_specs=[pl.BlockSpec((tm, d), row), pl.BlockSpec((8, tm), row)],
            out_specs=[pl.BlockSpec(memory_space=pl.ANY), pl.BlockSpec((MOE_SLOTS, LANES), row)],
            scratch_shapes=[pltpu.VMEM((2, MOE_SLOTS, d), BF16), pltpu.VMEM((MOE_TILE, d), BF16),
                            pltpu.SemaphoreType.DMA((3,))]),
        out_shape=[jax.ShapeDtypeStruct((n_rows, d), BF16),
                   jax.ShapeDtypeStruct((used.shape[0] * MOE_SLOTS, LANES), F32)],
        compiler_params=_params("arbitrary"),
        name="moe_scatter_chunks",
    )(dst, used, pad_start, pad_n, n_used, h2, info)


def _experts_kernel(te_ref, nt_ref, xs_ref, wg_ref, wu_ref, wd_ref, o_ref):
    used = pl.program_id(0) < nt_ref[0]

    @pl.when(used)
    def _():
        x = xs_ref[...]
        gate = jnp.dot(x, wg_ref[0].astype(BF16), preferred_element_type=F32)
        up = jnp.dot(x, wu_ref[0].astype(BF16), preferred_element_type=F32)
        hid = (_silu(gate) * up).astype(BF16)
        o_ref[...] = jnp.dot(hid, wd_ref[0].astype(BF16), preferred_element_type=F32).astype(BF16)


def _experts(xs, tile_expert, n_used, w_gate, w_up, w_down):
    n_rows, d = xs.shape
    tile = MOE_TILE
    by_expert = lambda j, te, nt: (te[j], 0, 0)
    row_tile = lambda j, te, nt: (jnp.maximum(jnp.minimum(j, nt[0] - 1), 0), 0)
    return pl.pallas_call(
        _experts_kernel,
        grid_spec=pltpu.PrefetchScalarGridSpec(
            num_scalar_prefetch=2,
            grid=(n_rows // tile,),
            in_specs=[pl.BlockSpec((tile, d), row_tile),
                      pl.BlockSpec((1, d, MOE_FF), by_expert),
                      pl.BlockSpec((1, d, MOE_FF), by_expert),
                      pl.BlockSpec((1, MOE_FF, d), by_expert)],
            out_specs=pl.BlockSpec((tile, d), row_tile)),
        out_shape=jax.ShapeDtypeStruct((n_rows, d), BF16),
        input_output_aliases={2: 0},
        compiler_params=_params("arbitrary"),
        name="moe_experts",
    )(tile_expert, n_used, xs, w_gate, w_up, w_down)


def _combine_kernel(dst_ref, used_ref, wslot_ref, col_ref, x1_ref, mod_ref, gpost_ref, ys_ref, o_ref,
                    ysl_ref, sem):
    i = pl.program_id(0)
    chunks = MOE_SLOTS // MOE_CHUNK
    slot = lax.rem(i, 2)

    def fetch(tile, buf, wait):
        base = tile * chunks

        def copy(c, carry):
            cp = pltpu.make_async_copy(ys_ref.at[_chunk_rows(dst_ref[base + c]), :],
                                       ysl_ref.at[buf, _chunk_rows(c), :], sem.at[buf])
            cp.wait() if wait else cp.start()
            return carry

        lax.fori_loop(0, chunks, copy, 0, unroll=8)

    @pl.when(i == 0)
    def _():
        fetch(0, 0, False)

    @pl.when(i + 1 < pl.num_programs(0))
    def _():
        fetch(i + 1, 1 - slot, False)

    col = col_ref[...]
    tm = col.shape[0]
    lane = lax.broadcasted_iota(jnp.int32, (tm, MOE_SLOTS), 1)
    hit = jnp.where(lane == col[:, 0:1].astype(jnp.int32), 1.0,
                    jnp.where(lane == col[:, 1:2].astype(jnp.int32), 1.0, 0.0)).astype(BF16)
    fetch(i, slot, True)
    in_use = lax.broadcasted_iota(jnp.int32, (MOE_SLOTS, 1), 0) < used_ref[i] * MOE_CHUNK
    ysw = jnp.where(in_use, ysl_ref[slot].astype(F32) * wslot_ref[:, 0:1], 0.0).astype(BF16)
    y = jnp.dot(hit, ysw, preferred_element_type=F32)
    g2 = mod_ref[0, 5:6, :]
    o_ref[...] = x1_ref[...] + g2 * _rms(y, gpost_ref[...])


def _combine(ys, dst, used, wslot, col, x1, mod3, gpost, seq, tm):
    t, d = x1.shape
    per_b = seq // tm
    row = lambda i, *_: (i, 0)
    return pl.pallas_call(
        _combine_kernel,
        grid_spec=pltpu.PrefetchScalarGridSpec(
            num_scalar_prefetch=2,
            grid=(t // tm,),
            in_specs=[pl.BlockSpec((MOE_SLOTS, LANES), row),
                      pl.BlockSpec((tm, LANES), row),
                      pl.BlockSpec((tm, d), row),
                      pl.BlockSpec((1, N_MOD, d), lambda i, *_: (i // per_b, 0, 0)),
                      pl.BlockSpec((1, d), lambda i, *_: (0, 0)),
                      pl.BlockSpec(memory_space=pl.ANY)],
            out_specs=pl.BlockSpec((tm, d), row),
            scratch_shapes=[pltpu.VMEM((2, MOE_SLOTS, d), BF16), pltpu.SemaphoreType.DMA((2,))]),
        out_shape=jax.ShapeDtypeStruct((t, d), F32),
        compiler_params=_params("arbitrary"),
        name="moe_combine",
    )(dst, used, wslot, col, x1, mod3, gpost, ys)


def kernel(x, c, positions, ada_w, ada_b, pre_norm_mix, post_norm_mix, pre_norm_ffn, post_norm_ffn, w_in, conv_w, conv_b, dt_bias, a_log, d_skip, ssd_norm, q_norm, w_uq, kv_norm, w_ukv, mla_out_norm, w_out, w_group_router, b_group_router, w_expert_router, b_expert_router, w_gate, w_up, w_down):
    batch, seq, d = x.shape
    t = batch * seq
    depth = ada_w.shape[0]
    tm = min(INPROJ_TOKENS, seq)
    cos_t, sin_t = _rope_tables(positions)

    x2 = x.reshape(t, d)
    for l in range(depth):
        mod3 = _modulation(c, ada_w[l], ada_b[l]).reshape(batch, N_MOD, d)
        w_main, w_rest, wq2, wk, wv_t = _inproj_weights(w_in[l], w_uq[l], w_ukv[l])
        z, xbc, dt_raw, q, k, v_t = _inproj(
            x2, mod3, pre_norm_mix[l].reshape(1, d), w_main, w_rest, q_norm[l].reshape(1, -1), wq2,
            kv_norm[l].reshape(1, -1), wk, wv_t, cos_t, sin_t, seq, tm)
        y_ssd = _ssd(xbc, z, dt_raw, conv_w[l], conv_b[l], dt_bias[l], a_log[l], d_skip[l],
                     ssd_norm[l], batch, seq)
        y_mla = _attention(q, k, v_t, mla_out_norm[l].reshape(1, -1), batch, seq)
        w_o = w_out[l].astype(BF16)
        pad_r = LANES - MOE_GROUPS - MOE_EXPERTS
        w_router = jnp.concatenate([w_group_router[l], w_expert_router[l], jnp.zeros((d, pad_r), F32)], axis=1)
        b_router = jnp.concatenate([b_group_router[l].reshape(-1), b_expert_router[l].reshape(-1),
                                    jnp.zeros((pad_r,), F32)]).reshape(1, LANES)
        wr_hi = w_router.astype(BF16)
        wr_lo = (w_router - wr_hi.astype(F32)).astype(BF16)
        x1, h2, col, info, cnt_tiles, chunk_expert = _outproj(
            y_ssd, y_mla, x2, mod3, w_o[:SSD_INNER], w_o[SSD_INNER:], post_norm_mix[l].reshape(1, d),
            pre_norm_ffn[l].reshape(1, d), wr_hi, wr_lo, b_router, seq, MOE_TOKENS)
        n_tok_tiles = t // MOE_TOKENS
        n_rows = _moe_rows(n_tok_tiles)
        dst, used, pad_start, pad_n, n_used, tile_expert = _moe_plan(cnt_tiles, chunk_expert, n_tok_tiles, n_rows)
        xs, wslot = _scatter_chunks(h2, info, dst, used, pad_start, pad_n, n_used, n_rows)
        ys = _experts(xs, tile_expert, n_used, w_gate[l], w_up[l], w_down[l])
        x2 = _combine(ys, dst, used, wslot, col, x1, mod3, post_norm_ffn[l].reshape(1, d), seq, MOE_TOKENS)
    return x2.reshape(batch, seq, d)
```

```python
import functools
import math

import jax
import jax.numpy as jnp
from jax import lax
from jax.experimental import pallas as pl
from jax.experimental.pallas import tpu as pltpu

F32 = jnp.float32
BF16 = jnp.bfloat16

D_MODEL = 1024
SSD_HEADS = 8
SSD_HEAD_DIM = 64
SSD_INNER = SSD_HEADS * SSD_HEAD_DIM
SSD_GROUPS = 2
SSD_STATE = 128
SSD_CONV = 4
SSD_CHUNK = 128
SSD_XBC = SSD_INNER + 2 * SSD_GROUPS * SSD_STATE
MLA_HEADS = 8
MLA_NOPE = 64
MLA_ROPE = 32
MLA_QK = MLA_NOPE + MLA_ROPE
MLA_V = 64
MLA_Q_RANK = 256
MLA_KV_RANK = 128
MLA_INNER = MLA_HEADS * MLA_V
ROPE_THETA = 10000.0
MOE_GROUPS = 4
MOE_EXPERTS_PER_GROUP = 8
MOE_EXPERTS = MOE_GROUPS * MOE_EXPERTS_PER_GROUP
MOE_FF = 256
N_MOD = 6
EPS = 1e-6
LOG2_E = math.log2(math.e)

LANES = 128
HEAD_PAD = 128
ATT_BLOCK = 256
MLA_V_ROWS = MLA_V + 16
INPROJ_TOKENS = 512
SSD_STEP_ROWS = 8 * SSD_CHUNK
MOE_TOKENS = 512
MOE_CHUNK = 16
MOE_TILE = 512
MOE_SLOTS = 2 * MOE_TOKENS + MOE_EXPERTS * MOE_CHUNK
VMEM_LIMIT = 48 * 1024 * 1024

_C_Z = 0
_C_XBC = _C_Z + SSD_INNER
_C_CQ = _C_XBC + SSD_XBC
_C_CKV = _C_CQ + MLA_Q_RANK
_C_KR = _C_CKV + MLA_KV_RANK
_C_KRS = _C_KR + LANES
_C_DT = _C_KRS + LANES
_C_END = _C_DT + LANES


def _silu(v):
    return v * (1.0 / (1.0 + jnp.exp(-v)))


def _rms(v, gain):
    return v * lax.rsqrt(jnp.mean(v * v, axis=-1, keepdims=True) + EPS) * gain


def _params(*sem, flags=None):
    return pltpu.CompilerParams(dimension_semantics=sem, vmem_limit_bytes=VMEM_LIMIT, flags=flags)


def _mod_kernel(c_ref, w_ref, b_ref, o_ref):
    act = _silu(c_ref[...])
    a_hi = act.astype(BF16)
    a_lo = (act - a_hi.astype(F32)).astype(BF16)
    w = w_ref[...]
    w_hi = w.astype(BF16)
    w_lo = (w - w_hi.astype(F32)).astype(BF16)
    o_ref[...] = (jnp.dot(a_hi, w_hi, preferred_element_type=F32)
                  + jnp.dot(a_lo, w_hi, preferred_element_type=F32)
                  + jnp.dot(a_hi, w_lo, preferred_element_type=F32)) + b_ref[...]


def _modulation(c, ada_w, ada_b):
    b, d = c.shape
    n = ada_w.shape[1]
    return pl.pallas_call(
        _mod_kernel,
        grid=(n // d,),
        in_specs=[pl.BlockSpec((b, d), lambda j: (0, 0)),
                  pl.BlockSpec((d, d), lambda j: (0, j)),
                  pl.BlockSpec((1, d), lambda j: (0, j))],
        out_specs=pl.BlockSpec((b, d), lambda j: (0, j)),
        out_shape=jax.ShapeDtypeStruct((b, n), F32),
        compiler_params=_params("arbitrary"),
        name="modulation",
    )(c, ada_w, ada_b.reshape(1, n))


def _rope_kernel(pos_ref, inv_ref, cos_ref, sin_ref):
    ang = pos_ref[...].astype(F32) * inv_ref[...]
    blk = ang.shape[0]
    half = MLA_ROPE // 2
    per_row = LANES // half
    lane = lax.broadcasted_iota(jnp.int32, (blk * per_row, LANES), 1)
    first = (lane >= MLA_NOPE) & (lane < MLA_NOPE + half)
    second = (lane >= MLA_NOPE + half) & (lane < MLA_QK)
    for fn, out_ref, fill in ((jnp.cos, cos_ref, 1.0), (jnp.sin, sin_ref, 0.0)):
        compact = fn(ang)
        rep = jnp.broadcast_to(compact[:, None, :], (blk, per_row, LANES)).reshape(blk * per_row, LANES)
        a = pltpu.roll(rep, 0, 1, stride=half, stride_axis=0)
        b = pltpu.roll(a, half, 1)
        out_ref[...] = jnp.where(first, a, jnp.where(second, b, fill))


def _rope_tables(positions):
    t = positions.size
    half = MLA_ROPE // 2
    per_row = LANES // half
    inv = 1.0 / (ROPE_THETA ** (jnp.arange(0, MLA_ROPE, 2, dtype=F32) / MLA_ROPE))
    group_token = (MLA_NOPE // half - jnp.arange(per_row)) % per_row
    pos_rep = jnp.repeat(positions.reshape(t // per_row, per_row)[:, group_token], half, axis=1)
    inv_t = jnp.tile(inv, per_row).reshape(1, LANES)
    rows = t // per_row
    blk = min(rows, 512)
    return pl.pallas_call(
        _rope_kernel,
        grid=(rows // blk,),
        in_specs=[pl.BlockSpec((blk, LANES), lambda i: (i, 0)),
                  pl.BlockSpec((1, LANES), lambda i: (0, 0))],
        out_specs=[pl.BlockSpec((blk * per_row, LANES), lambda i: (i, 0))] * 2,
        out_shape=[jax.ShapeDtypeStruct((t, LANES), F32)] * 2,
        compiler_params=_params("arbitrary"),
        name="rope_tables",
    )(pos_rep, inv_t)


def _inproj_kernel(x_ref, mod_ref, gpre_ref, wmain_ref, wrest_ref, qn_ref, wq_ref, kvn_ref, wk_ref,
                   wvt_ref, cos_ref, sin_ref, z_ref, xbc_ref, dt_ref, q_ref, k_ref, vt_ref):
    x = x_ref[...]
    sh = mod_ref[0, 0:1, :]
    sc = mod_ref[0, 1:2, :]
    h = (_rms(x, gpre_ref[...]) * (1.0 + sc) + sh).astype(BF16)
    u = jnp.concatenate([jnp.dot(h, wmain_ref[...], preferred_element_type=F32),
                         jnp.dot(h, wrest_ref[...], preferred_element_type=F32)], axis=1)
    z_ref[...] = u[:, _C_Z:_C_XBC].astype(BF16)
    xbc_ref[...] = u[:, _C_XBC:_C_CQ].astype(BF16)
    dt_ref[...] = u[:, _C_DT:_C_END]
    cos_t = cos_ref[...]
    sin_t = sin_ref[...]
    cq = _rms(u[:, _C_CQ:_C_CKV], qn_ref[...]).astype(BF16)
    q2 = jnp.dot(cq, wq_ref[...], preferred_element_type=F32)
    scale = MLA_QK ** -0.5 * LOG2_E
    nq = MLA_HEADS * HEAD_PAD
    for hd in range(MLA_HEADS):
        a = q2[:, hd * HEAD_PAD:(hd + 1) * HEAD_PAD]
        b = q2[:, nq + hd * HEAD_PAD:nq + (hd + 1) * HEAD_PAD]
        q_ref[:, hd * HEAD_PAD:(hd + 1) * HEAD_PAD] = ((a * cos_t + b * sin_t) * scale).astype(BF16)
    ckv = _rms(u[:, _C_CKV:_C_KR], kvn_ref[...]).astype(BF16)
    kn = jnp.dot(ckv, wk_ref[...], preferred_element_type=F32)
    k_pe = u[:, _C_KR:_C_KRS] * cos_t + u[:, _C_KRS:_C_DT] * sin_t
    for hd in range(MLA_HEADS):
        k_ref[:, hd * HEAD_PAD:(hd + 1) * HEAD_PAD] = (
            kn[:, hd * HEAD_PAD:(hd + 1) * HEAD_PAD] + k_pe).astype(BF16)
    v_t = lax.dot_general(wvt_ref[...], ckv, (((1,), (1,)), ((), ())), preferred_element_type=F32)
    head_row = lax.rem(lax.broadcasted_iota(jnp.int32, (v_t.shape[0], 1), 0), MLA_V_ROWS)
    v_t = v_t + jnp.where(head_row == MLA_V, 1.0, 0.0)
    for s in range(vt_ref.shape[0]):
        vt_ref[s] = v_t[:, s * ATT_BLOCK:(s + 1) * ATT_BLOCK].astype(BF16)


def _inproj_weights(w_in, w_uq, w_ukv):
    d = w_in.shape[0]
    half = MLA_ROPE // 2
    o_z, o_xbc, o_dt = 0, SSD_INNER, SSD_INNER + SSD_XBC
    o_cq = o_dt + SSD_HEADS
    o_ckv = o_cq + MLA_Q_RANK
    o_kr = o_ckv + MLA_KV_RANK
    zeros = lambda n: jnp.zeros((d, n), F32)
    kr = w_in[:, o_kr:o_kr + MLA_ROPE]
    kr_blk = jnp.concatenate([zeros(MLA_NOPE), kr, zeros(HEAD_PAD - MLA_QK)], axis=1)
    krs_blk = jnp.concatenate([zeros(MLA_NOPE), -kr[:, half:], kr[:, :half], zeros(HEAD_PAD - MLA_QK)], axis=1)
    dt_blk = jnp.concatenate([w_in[:, o_dt:o_dt + SSD_HEADS], zeros(LANES - SSD_HEADS)], axis=1)
    w_main = w_in[:, o_z:o_dt]
    w_rest = jnp.concatenate([w_in[:, o_cq:o_kr], kr_blk, krs_blk, dt_blk], axis=1)
    r = w_uq.shape[0]
    zq = jnp.zeros((r, MLA_HEADS, HEAD_PAD - MLA_QK), F32)
    zn = jnp.zeros((r, MLA_HEADS, MLA_NOPE), F32)
    wq_plain = jnp.concatenate([w_uq, zq], axis=2).reshape(r, MLA_HEADS * HEAD_PAD)
    wq_rot = jnp.concatenate([zn, -w_uq[:, :, MLA_NOPE + half:], w_uq[:, :, MLA_NOPE:MLA_NOPE + half], zq],
                             axis=2).reshape(r, MLA_HEADS * HEAD_PAD)
    wq2 = jnp.concatenate([wq_plain, wq_rot], axis=1)
    rk = w_ukv.shape[0]
    zk = jnp.zeros((rk, MLA_HEADS, HEAD_PAD - MLA_NOPE), F32)
    wk = jnp.concatenate([w_ukv[:, :, :MLA_NOPE], zk], axis=2).reshape(rk, MLA_HEADS * HEAD_PAD)
    zv = jnp.zeros((rk, MLA_HEADS, MLA_V_ROWS - MLA_V), F32)
    wv_t = jnp.concatenate([w_ukv[:, :, MLA_NOPE:], zv], axis=2).reshape(rk, MLA_HEADS * MLA_V_ROWS).T
    return w_main.astype(BF16), w_rest.astype(BF16), wq2.astype(BF16), wk.astype(BF16), wv_t.astype(BF16)


def _inproj(x2, mod3, gpre, w_main, w_rest, q_norm, wq2, kv_norm, wk, wv_t, cos_t, sin_t, seq, tm):
    t, d = x2.shape
    per_b = seq // tm
    row = lambda i: (i, 0)
    const = lambda i: (0, 0)
    nqk = MLA_HEADS * HEAD_PAD
    slabs = tm // ATT_BLOCK
    return pl.pallas_call(
        _inproj_kernel,
        grid=(t // tm,),
        in_specs=[pl.BlockSpec((tm, d), row),
                  pl.BlockSpec((1, N_MOD, d), lambda i: (i // per_b, 0, 0)),
                  pl.BlockSpec((1, d), const),
                  pl.BlockSpec(w_main.shape, const),
                  pl.BlockSpec(w_rest.shape, const),
                  pl.BlockSpec((1, MLA_Q_RANK), const),
                  pl.BlockSpec(wq2.shape, const),
                  pl.BlockSpec((1, MLA_KV_RANK), const),
                  pl.BlockSpec(wk.shape, const),
                  pl.BlockSpec(wv_t.shape, const),
                  pl.BlockSpec((tm, LANES), row),
                  pl.BlockSpec((tm, LANES), row)],
        out_specs=[pl.BlockSpec((tm, SSD_INNER), row),
                   pl.BlockSpec((tm, SSD_XBC), row),
                   pl.BlockSpec((tm, LANES), row),
                   pl.BlockSpec((tm, nqk), row),
                   pl.BlockSpec((tm, nqk), row),
                   pl.BlockSpec((slabs, MLA_HEADS * MLA_V_ROWS, ATT_BLOCK), lambda i: (i, 0, 0))],
        out_shape=[jax.ShapeDtypeStruct((t, SSD_INNER), BF16),
                   jax.ShapeDtypeStruct((t, SSD_XBC), BF16),
                   jax.ShapeDtypeStruct((t, LANES), F32),
                   jax.ShapeDtypeStruct((t, nqk), BF16),
                   jax.ShapeDtypeStruct((t, nqk), BF16),
                   jax.ShapeDtypeStruct((t // ATT_BLOCK, MLA_HEADS * MLA_V_ROWS, ATT_BLOCK), BF16)],
        compiler_params=_params("arbitrary"),
        name="inproj",
    )(x2, mod3, gpre, w_main, w_rest, q_norm, wq2, kv_norm, wk, wv_t, cos_t, sin_t)


def _split3_packed(v):
    lane = lax.broadcasted_iota(jnp.int32, v.shape, 1)
    v = jnp.where(lane < SSD_HEADS, v, 0.0)
    hi = v.astype(BF16).astype(F32)
    rest = v - hi
    mid = rest.astype(BF16).astype(F32)
    lo = rest - mid
    return (hi + pltpu.roll(mid, SSD_HEADS, 1) + pltpu.roll(lo, 2 * SSD_HEADS, 1)).astype(BF16)


def _ssd_kernel(xbc_ref, z_ref, dt_ref, cw_ref, cb_ref, dtb_ref, alog_ref, dsk_ref, gn_ref,
                e_ref, y_ref, tail_ref, state_ref):
    @pl.when(pl.program_id(1) == 0)
    def _():
        tail_ref[...] = jnp.zeros_like(tail_ref)
        state_ref[...] = jnp.zeros_like(state_ref)

    for c in range(xbc_ref.shape[0] // SSD_CHUNK):
        _ssd_chunk(pl.ds(c * SSD_CHUNK, SSD_CHUNK), xbc_ref, z_ref, dt_ref, cw_ref, cb_ref, dtb_ref,
                   alog_ref, dsk_ref, gn_ref, e_ref, y_ref, tail_ref, state_ref)


def _ssd_chunk(rows, xbc_ref, z_ref, dt_ref, cw_ref, cb_ref, dtb_ref, alog_ref, dsk_ref, gn_ref,
               e_ref, y_ref, tail_ref, state_ref):
    L = SSD_CHUNK
    gw = SSD_INNER // SSD_GROUPS

    cur_b = xbc_ref[rows, :]
    ext = jnp.concatenate([tail_ref[...], cur_b], axis=0)
    out_row = lax.broadcasted_iota(jnp.int32, ((SSD_CONV - 1) * L, ext.shape[0]), 0)
    src_row = lax.broadcasted_iota(jnp.int32, ((SSD_CONV - 1) * L, ext.shape[0]), 1)
    tail_rows = ext.shape[0] - L
    shift = jnp.where(src_row + (out_row // L + 1) == lax.rem(out_row, L) + tail_rows, 1.0, 0.0)
    shifted = jnp.dot(shift.astype(BF16), ext, preferred_element_type=F32)
    acc = cur_b.astype(F32) * cw_ref[SSD_CONV - 1:SSD_CONV, :] + cb_ref[...]
    for j in range(1, SSD_CONV):
        acc = acc + shifted[(j - 1) * L:j * L] * cw_ref[SSD_CONV - 1 - j:SSD_CONV - j, :]
    tail_ref[...] = cur_b[L - tail_rows:L]
    act = _silu(acc)
    xs = act[:, :SSD_INNER]
    bm = act[:, SSD_INNER:SSD_INNER + SSD_GROUPS * SSD_STATE].astype(BF16)
    cm = act[:, SSD_INNER + SSD_GROUPS * SSD_STATE:].astype(BF16)

    dt_in = dt_ref[rows, :] + dtb_ref[...]
    dt = jnp.maximum(dt_in, 0.0) + jnp.log(1.0 + jnp.exp(-jnp.abs(dt_in)))
    adt = dt * (-jnp.exp(alog_ref[...]))
    ri = lax.broadcasted_iota(jnp.int32, (L, L), 0)
    ci = lax.broadcasted_iota(jnp.int32, (L, L), 1)
    causal = ci <= ri
    acs_p = jnp.dot(causal.astype(BF16), _split3_packed(adt), preferred_element_type=F32)
    a_cs = (acs_p + pltpu.roll(acs_p, LANES - SSD_HEADS, 1)
            + pltpu.roll(acs_p, LANES - 2 * SSD_HEADS, 1))
    a_cs_t = a_cs.T
    cs_all = jnp.dot(_split3_packed(a_cs), e_ref[...], preferred_element_type=F32)
    cs64 = cs_all[:, :SSD_INNER]
    cs128 = cs_all[:, SSD_INNER:]
    dt64 = jnp.dot(_split3_packed(dt), e_ref[:, :SSD_INNER], preferred_element_type=F32)

    xd = xs * dt64
    xd_b = xd.astype(BF16)
    last = cs64[L - 1:L, :]
    xdw = (xd * jnp.exp(last - cs64)).astype(BF16)
    chunk_decay = jnp.exp(last)
    in_decay = jnp.exp(cs64)

    lane = lax.broadcasted_iota(jnp.int32, (L, LANES), 1)
    lo = lane < SSD_HEAD_DIM
    zero_b = jnp.zeros((L, LANES), BF16)
    y_parts = []
    new_states = []
    for g in range(SSD_GROUPS):
        bg = bm[:, g * SSD_STATE:(g + 1) * SSD_STATE]
        cg = cm[:, g * SSD_STATE:(g + 1) * SSD_STATE]
        cb = lax.dot_general(cg, bg, (((1,), (1,)), ((), ())), preferred_element_type=F32)
        prev = state_ref[:, g * gw:(g + 1) * gw]
        y_off = jnp.dot(cg, prev.astype(BF16), preferred_element_type=F32) * in_decay[:, g * gw:(g + 1) * gw]
        st = lax.dot_general(bg, xdw[:, g * gw:(g + 1) * gw], (((0,), (0,)), ((), ())),
                             preferred_element_type=F32)
        new_states.append(prev * chunk_decay[:, g * gw:(g + 1) * gw] + st)
        heads_per_group = SSD_HEADS // SSD_GROUPS
        for pair in range(heads_per_group // 2):
            h0 = g * heads_per_group + 2 * pair
            blk = xd_b[:, h0 * SSD_HEAD_DIM:(h0 + 2) * SSD_HEAD_DIM]
            y_pair = None
            for k in range(2):
                hh = h0 + k
                diff = cs128[:, hh * LANES:(hh + 1) * LANES] - a_cs_t[hh:hh + 1, :]
                decay = jnp.where(causal, jnp.exp(diff), 0.0)
                m = (cb * decay).astype(BF16)
                rhs = jnp.where(lo, blk, zero_b) if k == 0 else jnp.where(lo, zero_b, blk)
                part = jnp.dot(m, rhs, preferred_element_type=F32)
                y_pair = part if y_pair is None else y_pair + part
            c0 = (2 * pair) * SSD_HEAD_DIM
            y_parts.append(y_pair + y_off[:, c0:c0 + LANES])
    state_ref[...] = jnp.concatenate(new_states, axis=1)
    y = jnp.concatenate(y_parts, axis=1) + dsk_ref[...] * xs
    gated = y * _silu(z_ref[rows, :].astype(F32))
    y_ref[rows, :] = _rms(gated, gn_ref[...]).astype(BF16)


def _ssd(xbc, z, dt_raw, conv_w, conv_b, dt_bias, a_log, d_skip, ssd_norm, batch, seq):
    t = xbc.shape[0]
    L = min(SSD_STEP_ROWS, seq)
    nc = seq // L
    row = lambda b, c: (b * nc + c, 0)
    const = lambda b, c: (0, 0)
    pad = lambda v: jnp.concatenate([v.reshape(1, -1), jnp.zeros((1, LANES - v.size), F32)], axis=1)
    head_of_lane = jnp.concatenate([jnp.arange(SSD_INNER) // SSD_HEAD_DIM,
                                    jnp.arange(SSD_HEADS * LANES) // LANES])
    rows = jnp.arange(LANES)[:, None]
    expand = ((rows < 3 * SSD_HEADS) & (rows % SSD_HEADS == head_of_lane[None, :])).astype(BF16)
    dsk = jnp.repeat(d_skip.astype(F32), SSD_HEAD_DIM).reshape(1, SSD_INNER)
    return pl.pallas_call(
        _ssd_kernel,
        grid=(batch, nc),
        in_specs=[pl.BlockSpec((L, SSD_XBC), row),
                  pl.BlockSpec((L, SSD_INNER), row),
                  pl.BlockSpec((L, LANES), row),
                  pl.BlockSpec((SSD_CONV, SSD_XBC), const),
                  pl.BlockSpec((1, SSD_XBC), const),
                  pl.BlockSpec((1, LANES), const),
                  pl.BlockSpec((1, LANES), const),
                  pl.BlockSpec((1, SSD_INNER), const),
                  pl.BlockSpec((1, SSD_INNER), const),
                  pl.BlockSpec(expand.shape, const)],
        out_specs=pl.BlockSpec((L, SSD_INNER), row),
        out_shape=jax.ShapeDtypeStruct((t, SSD_INNER), BF16),
        scratch_shapes=[pltpu.VMEM((16, SSD_XBC), BF16),
                        pltpu.VMEM((SSD_STATE, SSD_INNER), F32)],
        compiler_params=_params("arbitrary", "arbitrary"),
        name="ssd",
    )(xbc, z, dt_raw, conv_w, conv_b.reshape(1, -1), pad(dt_bias), pad(a_log), dsk,
      ssd_norm.reshape(1, -1), expand)


def _attn_kernel(q_ref, k_ref, vt_ref, gn_ref, o_ref, s_ref):
    blk = ATT_BLOCK
    qi = pl.program_id(1)
    causal = (lax.broadcasted_iota(jnp.int32, (blk, blk), 0)
              <= lax.broadcasted_iota(jnp.int32, (blk, blk), 1))
    dims = (((1,), (1,)), ((), ()))
    qs = [q_ref[:, hd * HEAD_PAD:(hd + 1) * HEAD_PAD] for hd in range(MLA_HEADS)]

    def step(j, carry, nblk, masked):
        keys = nblk * blk
        rows = pl.ds(pl.multiple_of(j * blk, blk), keys)
        m_new = []
        for hd in range(MLA_HEADS):
            kk = k_ref[rows, hd * HEAD_PAD:(hd + 1) * HEAD_PAD]
            s = lax.dot_general(kk, qs[hd], dims, preferred_element_type=F32)
            if masked:
                s = jnp.where(causal, s, -jnp.inf)
            s_ref[hd, 0:keys, :] = s
            m_new.append(jnp.maximum(carry[hd][0], jnp.max(s, axis=0, keepdims=True)))
        new = []
        for hd in range(MLA_HEADS):
            m_old, acc = carry[hd]
            p = jnp.exp2(s_ref[hd, 0:keys, :] - m_new[hd]).astype(BF16)
            acc = acc * jnp.exp2(m_old - m_new[hd])
            for b in range(nblk):
                vt = vt_ref[j + b, hd * MLA_V_ROWS:(hd + 1) * MLA_V_ROWS, :]
                acc = acc + jnp.dot(vt, p[b * blk:(b + 1) * blk], preferred_element_type=F32)
            new.append((m_new[hd], acc))
        return tuple(new)

    neg = jnp.full((1, blk), -jnp.inf, F32)
    carry = tuple((neg, jnp.zeros((MLA_V_ROWS, blk), F32)) for _ in range(MLA_HEADS))
    carry = lax.fori_loop(0, qi // 2, lambda u, c: step(2 * u, c, 2, False), carry)
    carry = lax.fori_loop(0, lax.rem(qi, 2), lambda _, c: step(qi - 1, c, 1, False), carry)
    final = step(qi, carry, 1, True)
    y_t = jnp.concatenate([acc[:MLA_V] / acc[MLA_V:MLA_V + 1] for _, acc in final], axis=0)
    o_ref[...] = _rms(y_t.T, gn_ref[...]).astype(BF16)


def _attention(q, k, v_t, gn, batch, seq):
    t = q.shape[0]
    blk = ATT_BLOCK
    nq = seq // blk
    nqk = MLA_HEADS * HEAD_PAD
    return pl.pallas_call(
        _attn_kernel,
        grid=(batch, nq),
        in_specs=[pl.BlockSpec((blk, nqk), lambda b, i: (b * nq + i, 0)),
                  pl.BlockSpec((seq, nqk), lambda b, i: (b, 0)),
                  pl.BlockSpec((nq, MLA_HEADS * MLA_V_ROWS, blk), lambda b, i: (b, 0, 0)),
                  pl.BlockSpec((1, MLA_INNER), lambda b, i: (0, 0))],
        out_specs=pl.BlockSpec((blk, MLA_INNER), lambda b, i: (b * nq + i, 0)),
        out_shape=jax.ShapeDtypeStruct((t, MLA_INNER), BF16),
        scratch_shapes=[pltpu.VMEM((MLA_HEADS, 2 * blk, blk), F32)],
        compiler_params=_params("arbitrary", "arbitrary"),
        name="attention",
    )(q, k, v_t, gn)


def _outproj_kernel(ys_ref, ym_ref, x_ref, mod_ref, wtop_ref, wbot_ref, gpm_ref, gpf_ref,
                    wrh_ref, wrl_ref, br_ref, x1_ref, h2_ref, col_ref, info_ref, cnt_ref, ce_ref):
    y = (jnp.dot(ys_ref[...], wtop_ref[...], preferred_element_type=F32)
         + jnp.dot(ym_ref[...], wbot_ref[...], preferred_element_type=F32))
    g1 = mod_ref[0, 2:3, :]
    sh2 = mod_ref[0, 3:4, :]
    sc2 = mod_ref[0, 4:5, :]
    x1 = x_ref[...] + g1 * _rms(y, gpm_ref[...])
    x1_ref[...] = x1
    h2 = _rms(x1, gpf_ref[...]) * (1.0 + sc2) + sh2
    tm = h2.shape[0]
    h_hi = h2.astype(BF16)
    h_lo = (h2 - h_hi.astype(F32)).astype(BF16)
    logits = (jnp.dot(h_hi, wrh_ref[...], preferred_element_type=F32)
              + jnp.dot(h_lo, wrh_ref[...], preferred_element_type=F32)
              + jnp.dot(h_hi, wrl_ref[...], preferred_element_type=F32)) + br_ref[...]
    lt = logits.T
    row = lax.broadcasted_iota(jnp.int32, lt.shape, 0).astype(F32)
    ninf = -jnp.inf
    big = 1e9
    is_g = row < MOE_GROUPS
    gl = jnp.where(is_g, lt, ninf)
    gmax = jnp.max(gl, axis=0, keepdims=True)
    gsum = jnp.sum(jnp.where(is_g, jnp.exp(lt - gmax), 0.0), axis=0, keepdims=True)
    g_w = 1.0 / gsum
    g_idx = jnp.min(jnp.where(gl == gmax, row, big), axis=0, keepdims=True)
    first = MOE_GROUPS + MOE_EXPERTS_PER_GROUP * g_idx
    el = jnp.where(row >= first, jnp.where(row < first + MOE_EXPERTS_PER_GROUP, lt, ninf), ninf)
    m1 = jnp.max(el, axis=0, keepdims=True)
    i1 = jnp.min(jnp.where(el == m1, row, big), axis=0, keepdims=True)
    el2 = jnp.where(row == i1, ninf, el)
    m2 = jnp.max(el2, axis=0, keepdims=True)
    i2 = jnp.min(jnp.where(el2 == m2, row, big), axis=0, keepdims=True)
    r = jnp.exp(m2 - m1)
    w1 = g_w / (1.0 + r)
    w2 = g_w * r / (1.0 + r)
    sel1 = row == i1
    sel2 = row == i2
    both = jnp.where(sel1, 1.0, jnp.where(sel2, 1.0, 0.0))
    cnt16 = jnp.floor((jnp.sum(both, axis=1, keepdims=True) + (MOE_CHUNK - 1)) * (1.0 / MOE_CHUNK))
    er = lax.broadcasted_iota(jnp.int32, (LANES, LANES), 0)
    ec = lax.broadcasted_iota(jnp.int32, (LANES, LANES), 1)
    cnt16_b = jnp.broadcast_to(cnt16, (LANES, LANES))
    off16 = jnp.dot((ec < er).astype(BF16), cnt16_b.astype(BF16), preferred_element_type=F32)[:, 0:1]
    earlier = (lax.broadcasted_iota(jnp.int32, (tm, tm), 0)
               < lax.broadcasted_iota(jnp.int32, (tm, tm), 1)).astype(BF16)
    base = off16 * MOE_CHUNK + jnp.dot(both.astype(BF16), earlier, preferred_element_type=F32)
    lpos1 = jnp.sum(jnp.where(sel1, base, 0.0), axis=0, keepdims=True)
    lpos2 = jnp.sum(jnp.where(sel2, base, 0.0), axis=0, keepdims=True)
    h2_ref[...] = h_hi
    col_ref[...] = jnp.where(row == 0, lpos1, jnp.where(row == 1, lpos2, 0.0)).T
    row8 = lax.broadcasted_iota(jnp.int32, info_ref.shape, 0)
    info_ref[...] = jnp.where(row8 == 0, lpos1, jnp.where(row8 == 1, lpos2,
                              jnp.where(row8 == 2, w1, jnp.where(row8 == 3, w2, 0.0))))
    cnt_ref[...] = cnt16_b
    chunk = lax.broadcasted_iota(jnp.int32, (LANES, LANES), 1).astype(F32)
    ce = jnp.sum(jnp.where(off16 + cnt16 <= chunk, 1.0, 0.0), axis=0, keepdims=True) - MOE_GROUPS
    ce_ref[...] = jnp.broadcast_to(ce, ce_ref.shape)


def _outproj(y_ssd, y_mla, x2, mod3, w_top, w_bot, gpm, gpf, wr_hi, wr_lo, b_router, seq, tm):
    t, d = x2.shape
    per_b = seq // tm
    n_tok_tiles = t // tm
    row = lambda i: (i, 0)
    const = lambda i: (0, 0)
    return pl.pallas_call(
        _outproj_kernel,
        grid=(n_tok_tiles,),
        in_specs=[pl.BlockSpec((tm, SSD_INNER), row),
                  pl.BlockSpec((tm, MLA_INNER), row),
                  pl.BlockSpec((tm, d), row),
                  pl.BlockSpec((1, N_MOD, d), lambda i: (i // per_b, 0, 0)),
                  pl.BlockSpec(w_top.shape, const),
                  pl.BlockSpec(w_bot.shape, const),
                  pl.BlockSpec((1, d), const),
                  pl.BlockSpec((1, d), const),
                  pl.BlockSpec(wr_hi.shape, const),
                  pl.BlockSpec(wr_lo.shape, const),
                  pl.BlockSpec((1, LANES), const)],
        out_specs=[pl.BlockSpec((tm, d), row),
                   pl.BlockSpec((tm, d), row),
                   pl.BlockSpec((tm, LANES), row),
                   pl.BlockSpec((8, tm), row),
                   pl.BlockSpec((LANES, LANES), row),
                   pl.BlockSpec((8, LANES), row)],
        out_shape=[jax.ShapeDtypeStruct((t, d), F32),
                   jax.ShapeDtypeStruct((t, d), BF16),
                   jax.ShapeDtypeStruct((t, LANES), F32),
                   jax.ShapeDtypeStruct((n_tok_tiles * 8, tm), F32),
                   jax.ShapeDtypeStruct((n_tok_tiles * LANES, LANES), F32),
                   jax.ShapeDtypeStruct((n_tok_tiles * 8, LANES), F32)],
        compiler_params=_params("arbitrary"),
        name="outproj_router",
    )(y_ssd, y_mla, x2, mod3, w_top, w_bot, gpm, gpf, wr_hi, wr_lo, b_router)


def _moe_rows(n_tok_tiles):
    rows = n_tok_tiles * MOE_SLOTS + MOE_EXPERTS * (MOE_TILE - MOE_CHUNK)
    return (rows + MOE_TILE - 1) // MOE_TILE * MOE_TILE


def _moe_plan(cnt_tiles, chunk_expert, n_tok_tiles, n_rows):
    chunks_per_tile = MOE_SLOTS // MOE_CHUNK
    chunks_per_rows = MOE_TILE // MOE_CHUNK
    cnt = cnt_tiles.reshape(n_tok_tiles, LANES, LANES)[:, MOE_GROUPS:MOE_GROUPS + MOE_EXPERTS, 0]
    cnt = cnt.astype(jnp.int32)
    ce = chunk_expert.reshape(n_tok_tiles, 8, LANES)[:, 0, :chunks_per_tile].astype(jnp.int32)
    used = jnp.sum(cnt, axis=1)
    total = jnp.sum(cnt, axis=0)
    padded = (total + chunks_per_rows - 1) // chunks_per_rows * chunks_per_rows
    e_end = jnp.cumsum(padded)
    e_start = e_end - padded
    run_global = e_start[None, :] + jnp.cumsum(cnt, axis=0) - cnt
    run_local = jnp.cumsum(cnt, axis=1) - cnt
    experts = jnp.arange(MOE_EXPERTS, dtype=jnp.int32)
    shift = jnp.sum(jnp.where(ce[:, :, None] == experts, (run_global - run_local)[:, None, :], 0), axis=-1)
    local = jnp.arange(chunks_per_tile, dtype=jnp.int32)[None, :]
    dst = jnp.where(local < used[:, None], shift + local, 0)
    n_row_tiles = n_rows // MOE_TILE
    n_used = e_end[-1] // chunks_per_rows
    j = jnp.minimum(jnp.arange(n_row_tiles, dtype=jnp.int32), jnp.maximum(n_used - 1, 0))
    tile_expert = jnp.sum((e_end[None, :] // chunks_per_rows <= j[:, None]).astype(jnp.int32), axis=1)
    tile_expert = jnp.minimum(tile_expert, MOE_EXPERTS - 1)
    i32 = lambda v: v.astype(jnp.int32)
    return (i32(dst.reshape(-1)), i32(used), i32(e_start + total), i32(padded - total),
            i32(n_used.reshape(1)), i32(tile_expert))


def _chunk_rows(chunk):
    return pl.ds(pl.multiple_of(chunk * MOE_CHUNK, MOE_CHUNK), MOE_CHUNK)


def _run_copies(*groups):
    for wait in (False, True):
        for lo, hi, make_copy in groups:
            def body(k, carry, wait=wait, make_copy=make_copy):
                cp = make_copy(k)
                cp.wait() if wait else cp.start()
                return carry
            lax.fori_loop(lo, hi, body, 0)


def _scatter_chunks_kernel(dst_ref, used_ref, pad_start_ref, pad_n_ref, nused_ref, h2_ref, info_ref,
                           o_ref, src_ref, zero_ref, sem):
    i = pl.program_id(0)
    chunks = MOE_SLOTS // MOE_CHUNK
    slot = lax.rem(i, 2)
    d = h2_ref.shape[1]

    info = info_ref[...]
    srow = lax.broadcasted_iota(jnp.int32, (MOE_SLOTS, info.shape[1]), 0)
    hit1 = srow == info[0:1, :].astype(jnp.int32)
    hit2 = srow == info[1:2, :].astype(jnp.int32)
    perm = jnp.where(hit1, 1.0, jnp.where(hit2, 1.0, 0.0)).astype(BF16)
    src_ref[slot, :, 0:d] = jnp.dot(perm, h2_ref[...], preferred_element_type=F32).astype(BF16)
    wslot = jnp.sum(jnp.where(hit1, info[2:3, :], jnp.where(hit2, info[3:4, :], 0.0)), axis=1, keepdims=True)
    w_hi = wslot.astype(BF16).astype(F32)
    lane = lax.broadcasted_iota(jnp.int32, (MOE_SLOTS, LANES), 1)
    src_ref[slot, :, d:d + LANES] = jnp.where(lane == 0, w_hi, jnp.where(lane == 1, wslot - w_hi, 0.0)).astype(BF16)

    @pl.when(i == 0)
    def _():
        zero_ref[...] = jnp.zeros_like(zero_ref)
        pads = [(0, pad_n_ref[e], lambda c, e=e: pltpu.make_async_copy(
            zero_ref.at[pl.ds(0, MOE_CHUNK), :], o_ref.at[_chunk_rows(pad_start_ref[e] + c), :], sem.at[2]))
            for e in range(MOE_EXPERTS)]
        tail = (nused_ref[0], o_ref.shape[0] // MOE_TILE, lambda j: pltpu.make_async_copy(
            zero_ref, o_ref.at[pl.ds(pl.multiple_of(j * MOE_TILE, MOE_TILE), MOE_TILE), :], sem.at[2]))
        _run_copies(tail, *pads)

    def chunk_copies(tile, buf, wait):
        def body(c, carry):
            cp = pltpu.make_async_copy(src_ref.at[buf, _chunk_rows(c), :],
                                       o_ref.at[_chunk_rows(dst_ref[tile * chunks + c]), :], sem.at[buf])
            cp.wait() if wait else cp.start()
            return carry
        lax.fori_loop(0, used_ref[tile], body, 0)

    chunk_copies(i, slot, False)

    @pl.when(i > 0)
    def _():
        chunk_copies(i - 1, 1 - slot, True)

    @pl.when(i == pl.num_programs(0) - 1)
    def _():
        chunk_copies(i, slot, True)


def _scatter_chunks(h2, info, dst, used, pad_start, pad_n, n_used, n_rows):
    t, d = h2.shape
    tm = info.shape[1]
    row = lambda i, *_: (i, 0)
    return pl.pallas_call(
        _scatter_chunks_kernel,
        grid_spec=pltpu.PrefetchScalarGridSpec(
            num_scalar_prefetch=5,
            grid=(used.shape[0],),
            in_specs=[pl.BlockSpec((tm, d), row), pl.BlockSpec((8, tm), row)],
            out_specs=pl.BlockSpec(memory_space=pl.ANY),
            scratch_shapes=[pltpu.VMEM((2, MOE_SLOTS, d + LANES), BF16),
                            pltpu.VMEM((MOE_TILE, d + LANES), BF16),
                            pltpu.SemaphoreType.DMA((3,))]),
        out_shape=jax.ShapeDtypeStruct((n_rows, d + LANES), BF16),
        compiler_params=_params("arbitrary"),
        name="moe_scatter_chunks",
    )(dst, used, pad_start, pad_n, n_used, h2, info)


def _experts_kernel(te_ref, nt_ref, xs_ref, wg_ref, wu_ref, wd_ref, o_ref):
    used = pl.program_id(0) < nt_ref[0]

    @pl.when(used)
    def _():
        d = wd_ref.shape[2]
        x = xs_ref[:, 0:d]
        tail = xs_ref[:, d:]
        w = tail[:, 0:1].astype(F32) + tail[:, 1:2].astype(F32)
        gate = jnp.dot(x, wg_ref[0].astype(BF16), preferred_element_type=F32)
        up = jnp.dot(x, wu_ref[0].astype(BF16), preferred_element_type=F32)
        hid = (_silu(gate) * up).astype(BF16)
        y = jnp.dot(hid, wd_ref[0].astype(BF16), preferred_element_type=F32)
        o_ref[:, 0:d] = (y * w).astype(BF16)
        o_ref[:, d:] = tail


def _experts(xs, tile_expert, n_used, w_gate, w_up, w_down):
    n_rows, dw = xs.shape
    d = w_down.shape[2]
    tile = MOE_TILE
    by_expert = lambda j, te, nt: (te[j], 0, 0)
    row_tile = lambda j, te, nt: (jnp.maximum(jnp.minimum(j, nt[0] - 1), 0), 0)
    return pl.pallas_call(
        _experts_kernel,
        grid_spec=pltpu.PrefetchScalarGridSpec(
            num_scalar_prefetch=2,
            grid=(n_rows // tile,),
            in_specs=[pl.BlockSpec((tile, dw), row_tile),
                      pl.BlockSpec((1, d, MOE_FF), by_expert),
                      pl.BlockSpec((1, d, MOE_FF), by_expert),
                      pl.BlockSpec((1, MOE_FF, d), by_expert)],
            out_specs=pl.BlockSpec((tile, dw), row_tile)),
        out_shape=jax.ShapeDtypeStruct((n_rows, dw), BF16),
        input_output_aliases={2: 0},
        compiler_params=_params("arbitrary"),
        name="moe_experts",
    )(tile_expert, n_used, xs, w_gate, w_up, w_down)


def _combine_kernel(dst_ref, used_ref, col_ref, x1_ref, mod_ref, gpost_ref, ys_ref, o_ref,
                    ysl_ref, sem):
    i = pl.program_id(0)
    chunks = MOE_SLOTS // MOE_CHUNK
    slot = lax.rem(i, 2)

    def fetch(tile, buf, wait):
        base = tile * chunks

        def copy(c, carry):
            cp = pltpu.make_async_copy(ys_ref.at[_chunk_rows(dst_ref[base + c]), pl.ds(0, ysl_ref.shape[2])],
                                       ysl_ref.at[buf, _chunk_rows(c), :], sem.at[buf])
            cp.wait() if wait else cp.start()
            return carry

        lax.fori_loop(0, chunks, copy, 0, unroll=8)

    @pl.when(i == 0)
    def _():
        fetch(0, 0, False)

    @pl.when(i + 1 < pl.num_programs(0))
    def _():
        fetch(i + 1, 1 - slot, False)

    col = col_ref[...]
    tm = col.shape[0]
    lane = lax.broadcasted_iota(jnp.int32, (tm, MOE_SLOTS), 1)
    hit = jnp.where(lane == col[:, 0:1].astype(jnp.int32), 1.0,
                    jnp.where(lane == col[:, 1:2].astype(jnp.int32), 1.0, 0.0)).astype(BF16)
    fetch(i, slot, True)
    in_use = lax.broadcasted_iota(jnp.int32, (MOE_SLOTS, 1), 0) < used_ref[i] * MOE_CHUNK
    ysw = jnp.where(in_use, ysl_ref[slot], jnp.zeros((), BF16))
    y = jnp.dot(hit, ysw, preferred_element_type=F32)
    g2 = mod_ref[0, 5:6, :]
    o_ref[...] = x1_ref[...] + g2 * _rms(y, gpost_ref[...])


def _combine(ys, dst, used, col, x1, mod3, gpost, seq, tm):
    t, d = x1.shape
    per_b = seq // tm
    row = lambda i, *_: (i, 0)
    return pl.pallas_call(
        _combine_kernel,
        grid_spec=pltpu.PrefetchScalarGridSpec(
            num_scalar_prefetch=2,
            grid=(t // tm,),
            in_specs=[pl.BlockSpec((tm, LANES), row),
                      pl.BlockSpec((tm, d), row),
                      pl.BlockSpec((1, N_MOD, d), lambda i, *_: (i // per_b, 0, 0)),
                      pl.BlockSpec((1, d), lambda i, *_: (0, 0)),
                      pl.BlockSpec(memory_space=pl.ANY)],
            out_specs=pl.BlockSpec((tm, d), row),
            scratch_shapes=[pltpu.VMEM((2, MOE_SLOTS, d), BF16), pltpu.SemaphoreType.DMA((2,))]),
        out_shape=jax.ShapeDtypeStruct((t, d), F32),
        compiler_params=_params("arbitrary"),
        name="moe_combine",
    )(dst, used, col, x1, mod3, gpost, ys)


def kernel(x, c, positions, ada_w, ada_b, pre_norm_mix, post_norm_mix, pre_norm_ffn, post_norm_ffn, w_in, conv_w, conv_b, dt_bias, a_log, d_skip, ssd_norm, q_norm, w_uq, kv_norm, w_ukv, mla_out_norm, w_out, w_group_router, b_group_router, w_expert_router, b_expert_router, w_gate, w_up, w_down):
    batch, seq, d = x.shape
    t = batch * seq
    depth = ada_w.shape[0]
    tm = min(INPROJ_TOKENS, seq)
    cos_t, sin_t = _rope_tables(positions)

    x2 = x.reshape(t, d)
    for l in range(depth):
        mod3 = _modulation(c, ada_w[l], ada_b[l]).reshape(batch, N_MOD, d)
        w_main, w_rest, wq2, wk, wv_t = _inproj_weights(w_in[l], w_uq[l], w_ukv[l])
        z, xbc, dt_raw, q, k, v_t = _inproj(
            x2, mod3, pre_norm_mix[l].reshape(1, d), w_main, w_rest, q_norm[l].reshape(1, -1), wq2,
            kv_norm[l].reshape(1, -1), wk, wv_t, cos_t, sin_t, seq, tm)
        y_ssd = _ssd(xbc, z, dt_raw, conv_w[l], conv_b[l], dt_bias[l], a_log[l], d_skip[l],
                     ssd_norm[l], batch, seq)
        y_mla = _attention(q, k, v_t, mla_out_norm[l].reshape(1, -1), batch, seq)
        w_o = w_out[l].astype(BF16)
        pad_r = LANES - MOE_GROUPS - MOE_EXPERTS
        w_router = jnp.concatenate([w_group_router[l], w_expert_router[l], jnp.zeros((d, pad_r), F32)], axis=1)
        b_router = jnp.concatenate([b_group_router[l].reshape(-1), b_expert_router[l].reshape(-1),
                                    jnp.zeros((pad_r,), F32)]).reshape(1, LANES)
        wr_hi = w_router.astype(BF16)
        wr_lo = (w_router - wr_hi.astype(F32)).astype(BF16)
        x1, h2, col, info, cnt_tiles, chunk_expert = _outproj(
            y_ssd, y_mla, x2, mod3, w_o[:SSD_INNER], w_o[SSD_INNER:], post_norm_mix[l].reshape(1, d),
            pre_norm_ffn[l].reshape(1, d), wr_hi, wr_lo, b_router, seq, MOE_TOKENS)
        n_tok_tiles = t // MOE_TOKENS
        n_rows = _moe_rows(n_tok_tiles)
        dst, used, pad_start, pad_n, n_used, tile_expert = _moe_plan(cnt_tiles, chunk_expert, n_tok_tiles, n_rows)
        xs = _scatter_chunks(h2, info, dst, used, pad_start, pad_n, n_used, n_rows)
        ys = _experts(xs, tile_expert, n_used, w_gate[l], w_up[l], w_down[l])
        x2 = _combine(ys, dst, used, col, x1, mod3, post_norm_ffn[l].reshape(1, d), seq, MOE_TOKENS)
    return x2.reshape(batch, seq, d)
```

```python
import functools
import math

import jax
import jax.numpy as jnp
from jax import lax
from jax.experimental import pallas as pl
from jax.experimental.pallas import tpu as pltpu

F32 = jnp.float32
BF16 = jnp.bfloat16

D_MODEL = 1024
SSD_HEADS = 8
SSD_HEAD_DIM = 64
SSD_INNER = SSD_HEADS * SSD_HEAD_DIM
SSD_GROUPS = 2
SSD_STATE = 128
SSD_CONV = 4
SSD_CHUNK = 128
SSD_XBC = SSD_INNER + 2 * SSD_GROUPS * SSD_STATE
MLA_HEADS = 8
MLA_NOPE = 64
MLA_ROPE = 32
MLA_QK = MLA_NOPE + MLA_ROPE
MLA_V = 64
MLA_Q_RANK = 256
MLA_KV_RANK = 128
MLA_INNER = MLA_HEADS * MLA_V
ROPE_THETA = 10000.0
MOE_GROUPS = 4
MOE_EXPERTS_PER_GROUP = 8
MOE_EXPERTS = MOE_GROUPS * MOE_EXPERTS_PER_GROUP
MOE_FF = 256
N_MOD = 6
EPS = 1e-6
LOG2_E = math.log2(math.e)

LANES = 128
HEAD_PAD = 128
ATT_BLOCK = 256
MLA_V_ROWS = MLA_V + 16
INPROJ_TOKENS = 512
SSD_STEP_ROWS = 8 * SSD_CHUNK
MOE_TOKENS = 512
MOE_CHUNK = 16
MOE_TILE = 512
MOE_SLOTS = 2 * MOE_TOKENS + MOE_EXPERTS * MOE_CHUNK
VMEM_LIMIT = 48 * 1024 * 1024

_C_Z = 0
_C_XBC = _C_Z + SSD_INNER
_C_CQ = _C_XBC + SSD_XBC
_C_CKV = _C_CQ + MLA_Q_RANK
_C_KR = _C_CKV + MLA_KV_RANK
_C_KRS = _C_KR + LANES
_C_DT = _C_KRS + LANES
_C_END = _C_DT + LANES


def _silu(v):
    return v * (1.0 / (1.0 + jnp.exp(-v)))


def _rms(v, gain):
    return v * lax.rsqrt(jnp.mean(v * v, axis=-1, keepdims=True) + EPS) * gain


def _params(*sem, flags=None):
    return pltpu.CompilerParams(dimension_semantics=sem, vmem_limit_bytes=VMEM_LIMIT, flags=flags)


def _mod_kernel(c_ref, w_ref, b_ref, o_ref):
    act = _silu(c_ref[...])
    a_hi = act.astype(BF16)
    a_lo = (act - a_hi.astype(F32)).astype(BF16)
    w = w_ref[...]
    w_hi = w.astype(BF16)
    w_lo = (w - w_hi.astype(F32)).astype(BF16)
    o_ref[...] = (jnp.dot(a_hi, w_hi, preferred_element_type=F32)
                  + jnp.dot(a_lo, w_hi, preferred_element_type=F32)
                  + jnp.dot(a_hi, w_lo, preferred_element_type=F32)) + b_ref[...]


def _modulation(c, ada_w, ada_b):
    b, d = c.shape
    n = ada_w.shape[1]
    return pl.pallas_call(
        _mod_kernel,
        grid=(n // d,),
        in_specs=[pl.BlockSpec((b, d), lambda j: (0, 0)),
                  pl.BlockSpec((d, d), lambda j: (0, j)),
                  pl.BlockSpec((1, d), lambda j: (0, j))],
        out_specs=pl.BlockSpec((b, d), lambda j: (0, j)),
        out_shape=jax.ShapeDtypeStruct((b, n), F32),
        compiler_params=_params("arbitrary"),
        name="modulation",
    )(c, ada_w, ada_b.reshape(1, n))


def _rope_kernel(pos_ref, inv_ref, cos_ref, sin_ref):
    ang = pos_ref[...].astype(F32) * inv_ref[...]
    blk = ang.shape[0]
    half = MLA_ROPE // 2
    per_row = LANES // half
    lane = lax.broadcasted_iota(jnp.int32, (blk * per_row, LANES), 1)
    first = (lane >= MLA_NOPE) & (lane < MLA_NOPE + half)
    second = (lane >= MLA_NOPE + half) & (lane < MLA_QK)
    for fn, out_ref, fill in ((jnp.cos, cos_ref, 1.0), (jnp.sin, sin_ref, 0.0)):
        compact = fn(ang)
        rep = jnp.broadcast_to(compact[:, None, :], (blk, per_row, LANES)).reshape(blk * per_row, LANES)
        a = pltpu.roll(rep, 0, 1, stride=half, stride_axis=0)
        b = pltpu.roll(a, half, 1)
        out_ref[...] = jnp.where(first, a, jnp.where(second, b, fill))


def _rope_tables(positions):
    t = positions.size
    half = MLA_ROPE // 2
    per_row = LANES // half
    inv = 1.0 / (ROPE_THETA ** (jnp.arange(0, MLA_ROPE, 2, dtype=F32) / MLA_ROPE))
    group_token = (MLA_NOPE // half - jnp.arange(per_row)) % per_row
    pos_rep = jnp.repeat(positions.reshape(t // per_row, per_row)[:, group_token], half, axis=1)
    inv_t = jnp.tile(inv, per_row).reshape(1, LANES)
    rows = t // per_row
    blk = min(rows, 512)
    return pl.pallas_call(
        _rope_kernel,
        grid=(rows // blk,),
        in_specs=[pl.BlockSpec((blk, LANES), lambda i: (i, 0)),
                  pl.BlockSpec((1, LANES), lambda i: (0, 0))],
        out_specs=[pl.BlockSpec((blk * per_row, LANES), lambda i: (i, 0))] * 2,
        out_shape=[jax.ShapeDtypeStruct((t, LANES), F32)] * 2,
        compiler_params=_params("arbitrary"),
        name="rope_tables",
    )(pos_rep, inv_t)


def _inproj_kernel(x_ref, mod_ref, gpre_ref, wmain_ref, wrest_ref, qn_ref, wq_ref, kvn_ref, wk_ref,
                   wvt_ref, cos_ref, sin_ref, z_ref, xbc_ref, dt_ref, q_ref, k_ref, vt_ref):
    x = x_ref[...]
    sh = mod_ref[0, 0:1, :]
    sc = mod_ref[0, 1:2, :]
    h = (_rms(x, gpre_ref[...]) * (1.0 + sc) + sh).astype(BF16)
    u = jnp.concatenate([jnp.dot(h, wmain_ref[...], preferred_element_type=F32),
                         jnp.dot(h, wrest_ref[...], preferred_element_type=F32)], axis=1)
    z_ref[...] = u[:, _C_Z:_C_XBC].astype(BF16)
    xbc_ref[...] = u[:, _C_XBC:_C_CQ].astype(BF16)
    dt_ref[...] = u[:, _C_DT:_C_END]
    cos_t = cos_ref[...]
    sin_t = sin_ref[...]
    cq = _rms(u[:, _C_CQ:_C_CKV], qn_ref[...]).astype(BF16)
    q2 = jnp.dot(cq, wq_ref[...], preferred_element_type=F32)
    scale = MLA_QK ** -0.5 * LOG2_E
    nq = MLA_HEADS * HEAD_PAD
    for hd in range(MLA_HEADS):
        a = q2[:, hd * HEAD_PAD:(hd + 1) * HEAD_PAD]
        b = q2[:, nq + hd * HEAD_PAD:nq + (hd + 1) * HEAD_PAD]
        q_ref[:, hd * HEAD_PAD:(hd + 1) * HEAD_PAD] = ((a * cos_t + b * sin_t) * scale).astype(BF16)
    ckv = _rms(u[:, _C_CKV:_C_KR], kvn_ref[...]).astype(BF16)
    kn = jnp.dot(ckv, wk_ref[...], preferred_element_type=F32)
    k_pe = u[:, _C_KR:_C_KRS] * cos_t + u[:, _C_KRS:_C_DT] * sin_t
    for hd in range(MLA_HEADS):
        k_ref[:, hd * HEAD_PAD:(hd + 1) * HEAD_PAD] = (
            kn[:, hd * HEAD_PAD:(hd + 1) * HEAD_PAD] + k_pe).astype(BF16)
    v_t = lax.dot_general(wvt_ref[...], ckv, (((1,), (1,)), ((), ())), preferred_element_type=F32)
    head_row = lax.rem(lax.broadcasted_iota(jnp.int32, (v_t.shape[0], 1), 0), MLA_V_ROWS)
    v_t = v_t + jnp.where(head_row == MLA_V, 1.0, 0.0)
    for s in range(vt_ref.shape[0]):
        vt_ref[s] = v_t[:, s * ATT_BLOCK:(s + 1) * ATT_BLOCK].astype(BF16)


def _inproj_weights(w_in, w_uq, w_ukv):
    d = w_in.shape[0]
    half = MLA_ROPE // 2
    o_z, o_xbc, o_dt = 0, SSD_INNER, SSD_INNER + SSD_XBC
    o_cq = o_dt + SSD_HEADS
    o_ckv = o_cq + MLA_Q_RANK
    o_kr = o_ckv + MLA_KV_RANK
    zeros = lambda n: jnp.zeros((d, n), F32)
    kr = w_in[:, o_kr:o_kr + MLA_ROPE]
    kr_blk = jnp.concatenate([zeros(MLA_NOPE), kr, zeros(HEAD_PAD - MLA_QK)], axis=1)
    krs_blk = jnp.concatenate([zeros(MLA_NOPE), -kr[:, half:], kr[:, :half], zeros(HEAD_PAD - MLA_QK)], axis=1)
    dt_blk = jnp.concatenate([w_in[:, o_dt:o_dt + SSD_HEADS], zeros(LANES - SSD_HEADS)], axis=1)
    w_main = w_in[:, o_z:o_dt]
    w_rest = jnp.concatenate([w_in[:, o_cq:o_kr], kr_blk, krs_blk, dt_blk], axis=1)
    r = w_uq.shape[0]
    zq = jnp.zeros((r, MLA_HEADS, HEAD_PAD - MLA_QK), F32)
    zn = jnp.zeros((r, MLA_HEADS, MLA_NOPE), F32)
    wq_plain = jnp.concatenate([w_uq, zq], axis=2).reshape(r, MLA_HEADS * HEAD_PAD)
    wq_rot = jnp.concatenate([zn, -w_uq[:, :, MLA_NOPE + half:], w_uq[:, :, MLA_NOPE:MLA_NOPE + half], zq],
                             axis=2).reshape(r, MLA_HEADS * HEAD_PAD)
    wq2 = jnp.concatenate([wq_plain, wq_rot], axis=1)
    rk = w_ukv.shape[0]
    zk = jnp.zeros((rk, MLA_HEADS, HEAD_PAD - MLA_NOPE), F32)
    wk = jnp.concatenate([w_ukv[:, :, :MLA_NOPE], zk], axis=2).reshape(rk, MLA_HEADS * HEAD_PAD)
    zv = jnp.zeros((rk, MLA_HEADS, MLA_V_ROWS - MLA_V), F32)
    wv_t = jnp.concatenate([w_ukv[:, :, MLA_NOPE:], zv], axis=2).reshape(rk, MLA_HEADS * MLA_V_ROWS).T
    return w_main.astype(BF16), w_rest.astype(BF16), wq2.astype(BF16), wk.astype(BF16), wv_t.astype(BF16)


def _inproj(x2, mod3, gpre, w_main, w_rest, q_norm, wq2, kv_norm, wk, wv_t, cos_t, sin_t, seq, tm):
    t, d = x2.shape
    per_b = seq // tm
    row = lambda i: (i, 0)
    const = lambda i: (0, 0)
    nqk = MLA_HEADS * HEAD_PAD
    slabs = tm // ATT_BLOCK
    return pl.pallas_call(
        _inproj_kernel,
        grid=(t // tm,),
        in_specs=[pl.BlockSpec((tm, d), row),
                  pl.BlockSpec((1, N_MOD, d), lambda i: (i // per_b, 0, 0)),
                  pl.BlockSpec((1, d), const),
                  pl.BlockSpec(w_main.shape, const),
                  pl.BlockSpec(w_rest.shape, const),
                  pl.BlockSpec((1, MLA_Q_RANK), const),
                  pl.BlockSpec(wq2.shape, const),
                  pl.BlockSpec((1, MLA_KV_RANK), const),
                  pl.BlockSpec(wk.shape, const),
                  pl.BlockSpec(wv_t.shape, const),
                  pl.BlockSpec((tm, LANES), row),
                  pl.BlockSpec((tm, LANES), row)],
        out_specs=[pl.BlockSpec((tm, SSD_INNER), row),
                   pl.BlockSpec((tm, SSD_XBC), row),
                   pl.BlockSpec((tm, LANES), row),
                   pl.BlockSpec((tm, nqk), row),
                   pl.BlockSpec((tm, nqk), row),
                   pl.BlockSpec((slabs, MLA_HEADS * MLA_V_ROWS, ATT_BLOCK), lambda i: (i, 0, 0))],
        out_shape=[jax.ShapeDtypeStruct((t, SSD_INNER), BF16),
                   jax.ShapeDtypeStruct((t, SSD_XBC), BF16),
                   jax.ShapeDtypeStruct((t, LANES), F32),
                   jax.ShapeDtypeStruct((t, nqk), BF16),
                   jax.ShapeDtypeStruct((t, nqk), BF16),
                   jax.ShapeDtypeStruct((t // ATT_BLOCK, MLA_HEADS * MLA_V_ROWS, ATT_BLOCK), BF16)],
        compiler_params=_params("arbitrary"),
        name="inproj",
    )(x2, mod3, gpre, w_main, w_rest, q_norm, wq2, kv_norm, wk, wv_t, cos_t, sin_t)


def _split3_packed(v):
    lane = lax.broadcasted_iota(jnp.int32, v.shape, 1)
    v = jnp.where(lane < SSD_HEADS, v, 0.0)
    hi = v.astype(BF16).astype(F32)
    rest = v - hi
    mid = rest.astype(BF16).astype(F32)
    lo = rest - mid
    return (hi + pltpu.roll(mid, SSD_HEADS, 1) + pltpu.roll(lo, 2 * SSD_HEADS, 1)).astype(BF16)


def _ssd_kernel(xbc_ref, z_ref, dt_ref, cw_ref, cb_ref, dtb_ref, alog_ref, dsk_ref, gn_ref,
                e_ref, y_ref, tail_ref, state_ref):
    @pl.when(pl.program_id(1) == 0)
    def _():
        tail_ref[...] = jnp.zeros_like(tail_ref)
        state_ref[...] = jnp.zeros_like(state_ref)

    for c in range(xbc_ref.shape[0] // SSD_CHUNK):
        _ssd_chunk(pl.ds(c * SSD_CHUNK, SSD_CHUNK), xbc_ref, z_ref, dt_ref, cw_ref, cb_ref, dtb_ref,
                   alog_ref, dsk_ref, gn_ref, e_ref, y_ref, tail_ref, state_ref)


def _ssd_chunk(rows, xbc_ref, z_ref, dt_ref, cw_ref, cb_ref, dtb_ref, alog_ref, dsk_ref, gn_ref,
               e_ref, y_ref, tail_ref, state_ref):
    L = SSD_CHUNK
    gw = SSD_INNER // SSD_GROUPS

    cur_b = xbc_ref[rows, :]
    ext = jnp.concatenate([tail_ref[...], cur_b], axis=0)
    out_row = lax.broadcasted_iota(jnp.int32, ((SSD_CONV - 1) * L, ext.shape[0]), 0)
    src_row = lax.broadcasted_iota(jnp.int32, ((SSD_CONV - 1) * L, ext.shape[0]), 1)
    tail_rows = ext.shape[0] - L
    shift = jnp.where(src_row + (out_row // L + 1) == lax.rem(out_row, L) + tail_rows, 1.0, 0.0)
    shifted = jnp.dot(shift.astype(BF16), ext, preferred_element_type=F32)
    acc = cur_b.astype(F32) * cw_ref[SSD_CONV - 1:SSD_CONV, :] + cb_ref[...]
    for j in range(1, SSD_CONV):
        acc = acc + shifted[(j - 1) * L:j * L] * cw_ref[SSD_CONV - 1 - j:SSD_CONV - j, :]
    tail_ref[...] = cur_b[L - tail_rows:L]
    act = _silu(acc)
    xs = act[:, :SSD_INNER]
    bm = act[:, SSD_INNER:SSD_INNER + SSD_GROUPS * SSD_STATE].astype(BF16)
    cm = act[:, SSD_INNER + SSD_GROUPS * SSD_STATE:].astype(BF16)

    dt_in = dt_ref[rows, :] + dtb_ref[...]
    dt = jnp.maximum(dt_in, 0.0) + jnp.log(1.0 + jnp.exp(-jnp.abs(dt_in)))
    adt = dt * (-jnp.exp(alog_ref[...]))
    ri = lax.broadcasted_iota(jnp.int32, (L, L), 0)
    ci = lax.broadcasted_iota(jnp.int32, (L, L), 1)
    causal = ci <= ri
    acs_p = jnp.dot(causal.astype(BF16), _split3_packed(adt), preferred_element_type=F32)
    a_cs = (acs_p + pltpu.roll(acs_p, LANES - SSD_HEADS, 1)
            + pltpu.roll(acs_p, LANES - 2 * SSD_HEADS, 1))
    a_cs_t = a_cs.T
    cs_all = jnp.dot(_split3_packed(a_cs), e_ref[...], preferred_element_type=F32)
    cs64 = cs_all[:, :SSD_INNER]
    cs128 = cs_all[:, SSD_INNER:]
    dt64 = jnp.dot(_split3_packed(dt), e_ref[:, :SSD_INNER], preferred_element_type=F32)

    xd = xs * dt64
    xd_b = xd.astype(BF16)
    last = cs64[L - 1:L, :]
    xdw = (xd * jnp.exp(last - cs64)).astype(BF16)
    chunk_decay = jnp.exp(last)
    in_decay = jnp.exp(cs64)

    lane = lax.broadcasted_iota(jnp.int32, (L, LANES), 1)
    lo = lane < SSD_HEAD_DIM
    zero_b = jnp.zeros((L, LANES), BF16)
    y_parts = []
    new_states = []
    for g in range(SSD_GROUPS):
        bg = bm[:, g * SSD_STATE:(g + 1) * SSD_STATE]
        cg = cm[:, g * SSD_STATE:(g + 1) * SSD_STATE]
        cb = lax.dot_general(cg, bg, (((1,), (1,)), ((), ())), preferred_element_type=F32)
        prev = state_ref[:, g * gw:(g + 1) * gw]
        y_off = jnp.dot(cg, prev.astype(BF16), preferred_element_type=F32) * in_decay[:, g * gw:(g + 1) * gw]
        st = lax.dot_general(bg, xdw[:, g * gw:(g + 1) * gw], (((0,), (0,)), ((), ())),
                             preferred_element_type=F32)
        new_states.append(prev * chunk_decay[:, g * gw:(g + 1) * gw] + st)
        heads_per_group = SSD_HEADS // SSD_GROUPS
        for pair in range(heads_per_group // 2):
            h0 = g * heads_per_group + 2 * pair
            blk = xd_b[:, h0 * SSD_HEAD_DIM:(h0 + 2) * SSD_HEAD_DIM]
            y_pair = None
            for k in range(2):
                hh = h0 + k
                diff = cs128[:, hh * LANES:(hh + 1) * LANES] - a_cs_t[hh:hh + 1, :]
                decay = jnp.where(causal, jnp.exp(diff), 0.0)
                m = (cb * decay).astype(BF16)
                rhs = jnp.where(lo, blk, zero_b) if k == 0 else jnp.where(lo, zero_b, blk)
                part = jnp.dot(m, rhs, preferred_element_type=F32)
                y_pair = part if y_pair is None else y_pair + part
            c0 = (2 * pair) * SSD_HEAD_DIM
            y_parts.append(y_pair + y_off[:, c0:c0 + LANES])
    state_ref[...] = jnp.concatenate(new_states, axis=1)
    y = jnp.concatenate(y_parts, axis=1) + dsk_ref[...] * xs
    gated = y * _silu(z_ref[rows, :].astype(F32))
    y_ref[rows, :] = _rms(gated, gn_ref[...]).astype(BF16)


def _ssd(xbc, z, dt_raw, conv_w, conv_b, dt_bias, a_log, d_skip, ssd_norm, batch, seq):
    t = xbc.shape[0]
    L = min(SSD_STEP_ROWS, seq)
    nc = seq // L
    row = lambda b, c: (b * nc + c, 0)
    const = lambda b, c: (0, 0)
    pad = lambda v: jnp.concatenate([v.reshape(1, -1), jnp.zeros((1, LANES - v.size), F32)], axis=1)
    head_of_lane = jnp.concatenate([jnp.arange(SSD_INNER) // SSD_HEAD_DIM,
                                    jnp.arange(SSD_HEADS * LANES) // LANES])
    rows = jnp.arange(LANES)[:, None]
    expand = ((rows < 3 * SSD_HEADS) & (rows % SSD_HEADS == head_of_lane[None, :])).astype(BF16)
    dsk = jnp.repeat(d_skip.astype(F32), SSD_HEAD_DIM).reshape(1, SSD_INNER)
    return pl.pallas_call(
        _ssd_kernel,
        grid=(batch, nc),
        in_specs=[pl.BlockSpec((L, SSD_XBC), row),
                  pl.BlockSpec((L, SSD_INNER), row),
                  pl.BlockSpec((L, LANES), row),
                  pl.BlockSpec((SSD_CONV, SSD_XBC), const),
                  pl.BlockSpec((1, SSD_XBC), const),
                  pl.BlockSpec((1, LANES), const),
                  pl.BlockSpec((1, LANES), const),
                  pl.BlockSpec((1, SSD_INNER), const),
                  pl.BlockSpec((1, SSD_INNER), const),
                  pl.BlockSpec(expand.shape, const)],
        out_specs=pl.BlockSpec((L, SSD_INNER), row),
        out_shape=jax.ShapeDtypeStruct((t, SSD_INNER), BF16),
        scratch_shapes=[pltpu.VMEM((16, SSD_XBC), BF16),
                        pltpu.VMEM((SSD_STATE, SSD_INNER), F32)],
        compiler_params=_params("arbitrary", "arbitrary"),
        name="ssd",
    )(xbc, z, dt_raw, conv_w, conv_b.reshape(1, -1), pad(dt_bias), pad(a_log), dsk,
      ssd_norm.reshape(1, -1), expand)


def _attn_kernel(q_ref, k_ref, vt_ref, gn_ref, o_ref, s_ref):
    blk = ATT_BLOCK
    qi = pl.program_id(1)
    causal = (lax.broadcasted_iota(jnp.int32, (blk, blk), 0)
              <= lax.broadcasted_iota(jnp.int32, (blk, blk), 1))
    last_causal = (lax.broadcasted_iota(jnp.int32, (2 * blk, blk), 0) - blk
                   <= lax.broadcasted_iota(jnp.int32, (2 * blk, blk), 1))
    dims = (((1,), (1,)), ((), ()))
    qs = [q_ref[:, hd * HEAD_PAD:(hd + 1) * HEAD_PAD] for hd in range(MLA_HEADS)]

    def step(j, carry, nblk, masked):
        keys = nblk * blk
        rows = pl.ds(pl.multiple_of(j * blk, blk), keys)
        m_new = []
        for hd in range(MLA_HEADS):
            kk = k_ref[rows, hd * HEAD_PAD:(hd + 1) * HEAD_PAD]
            s = lax.dot_general(kk, qs[hd], dims, preferred_element_type=F32)
            if masked:
                s = jnp.where(causal if nblk == 1 else last_causal, s, -jnp.inf)
            s_ref[hd, 0:keys, :] = s
            m_new.append(jnp.maximum(carry[hd][0], jnp.max(s, axis=0, keepdims=True)))
        new = []
        for hd in range(MLA_HEADS):
            m_old, acc = carry[hd]
            p = jnp.exp2(s_ref[hd, 0:keys, :] - m_new[hd]).astype(BF16)
            acc = acc * jnp.exp2(m_old - m_new[hd])
            for b in range(nblk):
                vt = vt_ref[j + b, hd * MLA_V_ROWS:(hd + 1) * MLA_V_ROWS, :]
                acc = acc + jnp.dot(vt, p[b * blk:(b + 1) * blk], preferred_element_type=F32)
            new.append((m_new[hd], acc))
        return tuple(new)

    neg = jnp.full((1, blk), -jnp.inf, F32)
    carry = tuple((neg, jnp.zeros((MLA_V_ROWS, blk), F32)) for _ in range(MLA_HEADS))
    odd = lax.rem(qi, 2)
    carry = lax.fori_loop(0, qi // 2, lambda u, c: step(2 * u, c, 2, False), carry)
    carry = lax.fori_loop(0, odd, lambda _, c: step(qi - 1, c, 2, True), carry)
    final = lax.fori_loop(0, 1 - odd, lambda _, c: step(qi, c, 1, True), carry)
    y_t = jnp.concatenate([acc[:MLA_V] / acc[MLA_V:MLA_V + 1] for _, acc in final], axis=0)
    o_ref[...] = _rms(y_t.T, gn_ref[...]).astype(BF16)


def _attention(q, k, v_t, gn, batch, seq):
    t = q.shape[0]
    blk = ATT_BLOCK
    nq = seq // blk
    nqk = MLA_HEADS * HEAD_PAD
    return pl.pallas_call(
        _attn_kernel,
        grid=(batch, nq),
        in_specs=[pl.BlockSpec((blk, nqk), lambda b, i: (b * nq + i, 0)),
                  pl.BlockSpec((seq, nqk), lambda b, i: (b, 0)),
                  pl.BlockSpec((nq, MLA_HEADS * MLA_V_ROWS, blk), lambda b, i: (b, 0, 0)),
                  pl.BlockSpec((1, MLA_INNER), lambda b, i: (0, 0))],
        out_specs=pl.BlockSpec((blk, MLA_INNER), lambda b, i: (b * nq + i, 0)),
        out_shape=jax.ShapeDtypeStruct((t, MLA_INNER), BF16),
        scratch_shapes=[pltpu.VMEM((MLA_HEADS, 2 * blk, blk), F32)],
        compiler_params=_params("arbitrary", "arbitrary"),
        name="attention",
    )(q, k, v_t, gn)


def _outproj_kernel(ys_ref, ym_ref, x_ref, mod_ref, wtop_ref, wbot_ref, gpm_ref, gpf_ref,
                    wrh_ref, wrl_ref, br_ref, x1_ref, h2_ref, col_ref, info_ref, cnt_ref, ce_ref):
    y = (jnp.dot(ys_ref[...], wtop_ref[...], preferred_element_type=F32)
         + jnp.dot(ym_ref[...], wbot_ref[...], preferred_element_type=F32))
    g1 = mod_ref[0, 2:3, :]
    sh2 = mod_ref[0, 3:4, :]
    sc2 = mod_ref[0, 4:5, :]
    x1 = x_ref[...] + g1 * _rms(y, gpm_ref[...])
    x1_ref[...] = x1
    h2 = _rms(x1, gpf_ref[...]) * (1.0 + sc2) + sh2
    tm = h2.shape[0]
    h_hi = h2.astype(BF16)
    h_lo = (h2 - h_hi.astype(F32)).astype(BF16)
    logits = (jnp.dot(h_hi, wrh_ref[...], preferred_element_type=F32)
              + jnp.dot(h_lo, wrh_ref[...], preferred_element_type=F32)
              + jnp.dot(h_hi, wrl_ref[...], preferred_element_type=F32)) + br_ref[...]
    lt = logits.T
    row = lax.broadcasted_iota(jnp.int32, lt.shape, 0).astype(F32)
    ninf = -jnp.inf
    big = 1e9
    is_g = row < MOE_GROUPS
    gl = jnp.where(is_g, lt, ninf)
    gmax = jnp.max(gl, axis=0, keepdims=True)
    gsum = jnp.sum(jnp.where(is_g, jnp.exp(lt - gmax), 0.0), axis=0, keepdims=True)
    g_w = 1.0 / gsum
    g_idx = jnp.min(jnp.where(gl == gmax, row, big), axis=0, keepdims=True)
    first = MOE_GROUPS + MOE_EXPERTS_PER_GROUP * g_idx
    el = jnp.where(row >= first, jnp.where(row < first + MOE_EXPERTS_PER_GROUP, lt, ninf), ninf)
    m1 = jnp.max(el, axis=0, keepdims=True)
    i1 = jnp.min(jnp.where(el == m1, row, big), axis=0, keepdims=True)
    el2 = jnp.where(row == i1, ninf, el)
    m2 = jnp.max(el2, axis=0, keepdims=True)
    i2 = jnp.min(jnp.where(el2 == m2, row, big), axis=0, keepdims=True)
    r = jnp.exp(m2 - m1)
    w1 = g_w / (1.0 + r)
    w2 = g_w * r / (1.0 + r)
    sel1 = row == i1
    sel2 = row == i2
    both = jnp.where(sel1, 1.0, jnp.where(sel2, 1.0, 0.0))
    cnt16 = jnp.floor((jnp.sum(both, axis=1, keepdims=True) + (MOE_CHUNK - 1)) * (1.0 / MOE_CHUNK))
    er = lax.broadcasted_iota(jnp.int32, (LANES, LANES), 0)
    ec = lax.broadcasted_iota(jnp.int32, (LANES, LANES), 1)
    cnt16_b = jnp.broadcast_to(cnt16, (LANES, LANES))
    off16 = jnp.dot((ec < er).astype(BF16), cnt16_b.astype(BF16), preferred_element_type=F32)[:, 0:1]
    earlier = (lax.broadcasted_iota(jnp.int32, (tm, tm), 0)
               < lax.broadcasted_iota(jnp.int32, (tm, tm), 1)).astype(BF16)
    base = off16 * MOE_CHUNK + jnp.dot(both.astype(BF16), earlier, preferred_element_type=F32)
    lpos1 = jnp.sum(jnp.where(sel1, base, 0.0), axis=0, keepdims=True)
    lpos2 = jnp.sum(jnp.where(sel2, base, 0.0), axis=0, keepdims=True)
    h2_ref[...] = h_hi
    col_ref[...] = jnp.where(row == 0, lpos1, jnp.where(row == 1, lpos2, 0.0)).T
    row8 = lax.broadcasted_iota(jnp.int32, info_ref.shape, 0)
    info_ref[...] = jnp.where(row8 == 0, lpos1, jnp.where(row8 == 1, lpos2,
                              jnp.where(row8 == 2, w1, jnp.where(row8 == 3, w2, 0.0))))
    cnt_ref[...] = cnt16_b
    chunk = lax.broadcasted_iota(jnp.int32, (LANES, LANES), 1).astype(F32)
    ce = jnp.sum(jnp.where(off16 + cnt16 <= chunk, 1.0, 0.0), axis=0, keepdims=True) - MOE_GROUPS
    ce_ref[...] = jnp.broadcast_to(ce, ce_ref.shape)


def _outproj(y_ssd, y_mla, x2, mod3, w_top, w_bot, gpm, gpf, wr_hi, wr_lo, b_router, seq, tm):
    t, d = x2.shape
    per_b = seq // tm
    n_tok_tiles = t // tm
    row = lambda i: (i, 0)
    const = lambda i: (0, 0)
    return pl.pallas_call(
        _outproj_kernel,
        grid=(n_tok_tiles,),
        in_specs=[pl.BlockSpec((tm, SSD_INNER), row),
                  pl.BlockSpec((tm, MLA_INNER), row),
                  pl.BlockSpec((tm, d), row),
                  pl.BlockSpec((1, N_MOD, d), lambda i: (i // per_b, 0, 0)),
                  pl.BlockSpec(w_top.shape, const),
                  pl.BlockSpec(w_bot.shape, const),
                  pl.BlockSpec((1, d), const),
                  pl.BlockSpec((1, d), const),
                  pl.BlockSpec(wr_hi.shape, const),
                  pl.BlockSpec(wr_lo.shape, const),
                  pl.BlockSpec((1, LANES), const)],
        out_specs=[pl.BlockSpec((tm, d), row),
                   pl.BlockSpec((tm, d), row),
                   pl.BlockSpec((tm, LANES), row),
                   pl.BlockSpec((8, tm), row),
                   pl.BlockSpec((LANES, LANES), row),
                   pl.BlockSpec((8, LANES), row)],
        out_shape=[jax.ShapeDtypeStruct((t, d), F32),
                   jax.ShapeDtypeStruct((t, d), BF16),
                   jax.ShapeDtypeStruct((t, LANES), F32),
                   jax.ShapeDtypeStruct((n_tok_tiles * 8, tm), F32),
                   jax.ShapeDtypeStruct((n_tok_tiles * LANES, LANES), F32),
                   jax.ShapeDtypeStruct((n_tok_tiles * 8, LANES), F32)],
        compiler_params=_params("arbitrary"),
        name="outproj_router",
    )(y_ssd, y_mla, x2, mod3, w_top, w_bot, gpm, gpf, wr_hi, wr_lo, b_router)


def _moe_rows(n_tok_tiles):
    rows = n_tok_tiles * MOE_SLOTS + MOE_EXPERTS * (MOE_TILE - MOE_CHUNK)
    return (rows + MOE_TILE - 1) // MOE_TILE * MOE_TILE


def _moe_plan(cnt_tiles, chunk_expert, n_tok_tiles, n_rows):
    chunks_per_tile = MOE_SLOTS // MOE_CHUNK
    chunks_per_rows = MOE_TILE // MOE_CHUNK
    cnt = cnt_tiles.reshape(n_tok_tiles, LANES, LANES)[:, MOE_GROUPS:MOE_GROUPS + MOE_EXPERTS, 0]
    cnt = cnt.astype(jnp.int32)
    ce = chunk_expert.reshape(n_tok_tiles, 8, LANES)[:, 0, :chunks_per_tile].astype(jnp.int32)
    used = jnp.sum(cnt, axis=1)
    total = jnp.sum(cnt, axis=0)
    padded = (total + chunks_per_rows - 1) // chunks_per_rows * chunks_per_rows
    e_end = jnp.cumsum(padded)
    e_start = e_end - padded
    run_global = e_start[None, :] + jnp.cumsum(cnt, axis=0) - cnt
    run_local = jnp.cumsum(cnt, axis=1) - cnt
    experts = jnp.arange(MOE_EXPERTS, dtype=jnp.int32)
    shift = jnp.sum(jnp.where(ce[:, :, None] == experts, (run_global - run_local)[:, None, :], 0), axis=-1)
    local = jnp.arange(chunks_per_tile, dtype=jnp.int32)[None, :]
    dst = jnp.where(local < used[:, None], shift + local, 0)
    n_row_tiles = n_rows // MOE_TILE
    n_used = e_end[-1] // chunks_per_rows
    j = jnp.minimum(jnp.arange(n_row_tiles, dtype=jnp.int32), jnp.maximum(n_used - 1, 0))
    tile_expert = jnp.sum((e_end[None, :] // chunks_per_rows <= j[:, None]).astype(jnp.int32), axis=1)
    tile_expert = jnp.minimum(tile_expert, MOE_EXPERTS - 1)
    i32 = lambda v: v.astype(jnp.int32)
    return (i32(dst.reshape(-1)), i32(used), i32(e_start + total), i32(padded - total),
            i32(n_used.reshape(1)), i32(tile_expert))


def _chunk_rows(chunk):
    return pl.ds(pl.multiple_of(chunk * MOE_CHUNK, MOE_CHUNK), MOE_CHUNK)


def _run_copies(*groups):
    for wait in (False, True):
        for lo, hi, make_copy in groups:
            def body(k, carry, wait=wait, make_copy=make_copy):
                cp = make_copy(k)
                cp.wait() if wait else cp.start()
                return carry
            lax.fori_loop(lo, hi, body, 0)


def _scatter_chunks_kernel(dst_ref, used_ref, pad_start_ref, pad_n_ref, nused_ref, h2_ref, info_ref,
                           o_ref, wslot_ref, src_ref, zero_ref, sem):
    i = pl.program_id(0)
    chunks = MOE_SLOTS // MOE_CHUNK
    slot = lax.rem(i, 2)

    info = info_ref[...]
    srow = lax.broadcasted_iota(jnp.int32, (MOE_SLOTS, info.shape[1]), 0)
    hit1 = srow == info[0:1, :].astype(jnp.int32)
    hit2 = srow == info[1:2, :].astype(jnp.int32)
    perm = jnp.where(hit1, 1.0, jnp.where(hit2, 1.0, 0.0)).astype(BF16)
    src_ref[slot] = jnp.dot(perm, h2_ref[...], preferred_element_type=F32).astype(BF16)
    wslot = jnp.sum(jnp.where(hit1, info[2:3, :], jnp.where(hit2, info[3:4, :], 0.0)), axis=1, keepdims=True)
    wslot_ref[...] = jnp.broadcast_to(wslot, wslot_ref.shape)

    @pl.when(i == 0)
    def _():
        zero_ref[...] = jnp.zeros_like(zero_ref)
        pads = [(0, pad_n_ref[e], lambda c, e=e: pltpu.make_async_copy(
            zero_ref.at[pl.ds(0, MOE_CHUNK), :], o_ref.at[_chunk_rows(pad_start_ref[e] + c), :], sem.at[2]))
            for e in range(MOE_EXPERTS)]
        tail = (nused_ref[0], o_ref.shape[0] // MOE_TILE, lambda j: pltpu.make_async_copy(
            zero_ref, o_ref.at[pl.ds(pl.multiple_of(j * MOE_TILE, MOE_TILE), MOE_TILE), :], sem.at[2]))
        _run_copies(tail, *pads)

    def chunk_copies(tile, buf, wait):
        def body(c, carry):
            cp = pltpu.make_async_copy(src_ref.at[buf, _chunk_rows(c), :],
                                       o_ref.at[_chunk_rows(dst_ref[tile * chunks + c]), :], sem.at[buf])
            cp.wait() if wait else cp.start()
            return carry
        lax.fori_loop(0, used_ref[tile], body, 0)

    chunk_copies(i, slot, False)

    @pl.when(i > 0)
    def _():
        chunk_copies(i - 1, 1 - slot, True)

    @pl.when(i == pl.num_programs(0) - 1)
    def _():
        chunk_copies(i, slot, True)


def _scatter_chunks(h2, info, dst, used, pad_start, pad_n, n_used, n_rows):
    t, d = h2.shape
    tm = info.shape[1]
    row = lambda i, *_: (i, 0)
    return pl.pallas_call(
        _scatter_chunks_kernel,
        grid_spec=pltpu.PrefetchScalarGridSpec(
            num_scalar_prefetch=5,
            grid=(used.shape[0],),
            in_specs=[pl.BlockSpec((tm, d), row), pl.BlockSpec((8, tm), row)],
            out_specs=[pl.BlockSpec(memory_space=pl.ANY), pl.BlockSpec((MOE_SLOTS, LANES), row)],
            scratch_shapes=[pltpu.VMEM((2, MOE_SLOTS, d), BF16), pltpu.VMEM((MOE_TILE, d), BF16),
                            pltpu.SemaphoreType.DMA((3,))]),
        out_shape=[jax.ShapeDtypeStruct((n_rows, d), BF16),
                   jax.ShapeDtypeStruct((used.shape[0] * MOE_SLOTS, LANES), F32)],
        compiler_params=_params("arbitrary"),
        name="moe_scatter_chunks",
    )(dst, used, pad_start, pad_n, n_used, h2, info)


def _experts_kernel(te_ref, nt_ref, xs_ref, wg_ref, wu_ref, wd_ref, o_ref):
    used = pl.program_id(0) < nt_ref[0]

    @pl.when(used)
    def _():
        x = xs_ref[...]
        gate = jnp.dot(x, wg_ref[0].astype(BF16), preferred_element_type=F32)
        up = jnp.dot(x, wu_ref[0].astype(BF16), preferred_element_type=F32)
        hid = (_silu(gate) * up).astype(BF16)
        o_ref[...] = jnp.dot(hid, wd_ref[0].astype(BF16), preferred_element_type=F32).astype(BF16)


def _experts(xs, tile_expert, n_used, w_gate, w_up, w_down):
    n_rows, d = xs.shape
    tile = MOE_TILE
    by_expert = lambda j, te, nt: (te[j], 0, 0)
    row_tile = lambda j, te, nt: (jnp.maximum(jnp.minimum(j, nt[0] - 1), 0), 0)
    return pl.pallas_call(
        _experts_kernel,
        grid_spec=pltpu.PrefetchScalarGridSpec(
            num_scalar_prefetch=2,
            grid=(n_rows // tile,),
            in_specs=[pl.BlockSpec((tile, d), row_tile),
                      pl.BlockSpec((1, d, MOE_FF), by_expert),
                      pl.BlockSpec((1, d, MOE_FF), by_expert),
                      pl.BlockSpec((1, MOE_FF, d), by_expert)],
            out_specs=pl.BlockSpec((tile, d), row_tile)),
        out_shape=jax.ShapeDtypeStruct((n_rows, d), BF16),
        input_output_aliases={2: 0},
        compiler_params=_params("arbitrary"),
        name="moe_experts",
    )(tile_expert, n_used, xs, w_gate, w_up, w_down)


def _combine_kernel(dst_ref, used_ref, wslot_ref, col_ref, x1_ref, mod_ref, gpost_ref, ys_ref, o_ref,
                    ysl_ref, sem):
    i = pl.program_id(0)
    chunks = MOE_SLOTS // MOE_CHUNK
    slot = lax.rem(i, 2)

    def fetch(tile, buf, wait):
        base = tile * chunks

        def copy(c, carry):
            cp = pltpu.make_async_copy(ys_ref.at[_chunk_rows(dst_ref[base + c]), :],
                                       ysl_ref.at[buf, _chunk_rows(c), :], sem.at[buf])
            cp.wait() if wait else cp.start()
            return carry

        lax.fori_loop(0, chunks, copy, 0, unroll=8)

    @pl.when(i == 0)
    def _():
        fetch(0, 0, False)

    @pl.when(i + 1 < pl.num_programs(0))
    def _():
        fetch(i + 1, 1 - slot, False)

    col = col_ref[...]
    tm = col.shape[0]
    lane = lax.broadcasted_iota(jnp.int32, (tm, MOE_SLOTS), 1)
    hit = jnp.where(lane == col[:, 0:1].astype(jnp.int32), 1.0,
                    jnp.where(lane == col[:, 1:2].astype(jnp.int32), 1.0, 0.0)).astype(BF16)
    fetch(i, slot, True)
    in_use = lax.broadcasted_iota(jnp.int32, (MOE_SLOTS, 1), 0) < used_ref[i] * MOE_CHUNK
    ysw = jnp.where(in_use, ysl_ref[slot].astype(F32) * wslot_ref[:, 0:1], 0.0).astype(BF16)
    y = jnp.dot(hit, ysw, preferred_element_type=F32)
    g2 = mod_ref[0, 5:6, :]
    o_ref[...] = x1_ref[...] + g2 * _rms(y, gpost_ref[...])


def _combine(ys, dst, used, wslot, col, x1, mod3, gpost, seq, tm):
    t, d = x1.shape
    per_b = seq // tm
    row = lambda i, *_: (i, 0)
    return pl.pallas_call(
        _combine_kernel,
        grid_spec=pltpu.PrefetchScalarGridSpec(
            num_scalar_prefetch=2,
            grid=(t // tm,),
            in_specs=[pl.BlockSpec((MOE_SLOTS, LANES), row),
                      pl.BlockSpec((tm, LANES), row),
                      pl.BlockSpec((tm, d), row),
                      pl.BlockSpec((1, N_MOD, d), lambda i, *_: (i // per_b, 0, 0)),
                      pl.BlockSpec((1, d), lambda i, *_: (0, 0)),
                      pl.BlockSpec(memory_space=pl.ANY)],
            out_specs=pl.BlockSpec((tm, d), row),
            scratch_shapes=[pltpu.VMEM((2, MOE_SLOTS, d), BF16), pltpu.SemaphoreType.DMA((2,))]),
        out_shape=jax.ShapeDtypeStruct((t, d), F32),
        compiler_params=_params("arbitrary"),
        name="moe_combine",
    )(dst, used, wslot, col, x1, mod3, gpost, ys)


def kernel(x, c, positions, ada_w, ada_b, pre_norm_mix, post_norm_mix, pre_norm_ffn, post_norm_ffn, w_in, conv_w, conv_b, dt_bias, a_log, d_skip, ssd_norm, q_norm, w_uq, kv_norm, w_ukv, mla_out_norm, w_out, w_group_router, b_group_router, w_expert_router, b_expert_router, w_gate, w_up, w_down):
    batch, seq, d = x.shape
    t = batch * seq
    depth = ada_w.shape[0]
    tm = min(INPROJ_TOKENS, seq)
    cos_t, sin_t = _rope_tables(positions)

    x2 = x.reshape(t, d)
    for l in range(depth):
        mod3 = _modulation(c, ada_w[l], ada_b[l]).reshape(batch, N_MOD, d)
        w_main, w_rest, wq2, wk, wv_t = _inproj_weights(w_in[l], w_uq[l], w_ukv[l])
        z, xbc, dt_raw, q, k, v_t = _inproj(
            x2, mod3, pre_norm_mix[l].reshape(1, d), w_main, w_rest, q_norm[l].reshape(1, -1), wq2,
            kv_norm[l].reshape(1, -1), wk, wv_t, cos_t, sin_t, seq, tm)
        y_ssd = _ssd(xbc, z, dt_raw, conv_w[l], conv_b[l], dt_bias[l], a_log[l], d_skip[l],
                     ssd_norm[l], batch, seq)
        y_mla = _attention(q, k, v_t, mla_out_norm[l].reshape(1, -1), batch, seq)
        w_o = w_out[l].astype(BF16)
        pad_r = LANES - MOE_GROUPS - MOE_EXPERTS
        w_router = jnp.concatenate([w_group_router[l], w_expert_router[l], jnp.zeros((d, pad_r), F32)], axis=1)
        b_router = jnp.concatenate([b_group_router[l].reshape(-1), b_expert_router[l].reshape(-1),
                                    jnp.zeros((pad_r,), F32)]).reshape(1, LANES)
        wr_hi = w_router.astype(BF16)
        wr_lo = (w_router - wr_hi.astype(F32)).astype(BF16)
        x1, h2, col, info, cnt_tiles, chunk_expert = _outproj(
            y_ssd, y_mla, x2, mod3, w_o[:SSD_INNER], w_o[SSD_INNER:], post_norm_mix[l].reshape(1, d),
            pre_norm_ffn[l].reshape(1, d), wr_hi, wr_lo, b_router, seq, MOE_TOKENS)
        n_tok_tiles = t // MOE_TOKENS
        n_rows = _moe_rows(n_tok_tiles)
        dst, used, pad_start, pad_n, n_used, tile_expert = _moe_plan(cnt_tiles, chunk_expert, n_tok_tiles, n_rows)
        xs, wslot = _scatter_chunks(h2, info, dst, used, pad_start, pad_n, n_used, n_rows)
        ys = _experts(xs, tile_expert, n_used, w_gate[l], w_up[l], w_down[l])
        x2 = _combine(ys, dst, used, wslot, col, x1, mod3, post_norm_ffn[l].reshape(1, d), seq, MOE_TOKENS)
    return x2.reshape(batch, seq, d)
```
